```python
import math
import jax, jax.numpy as jnp
from jax import lax
import numpy as np

D_MODEL = 1024
BATCH = 32
SEQ = 2048
DEPTH = 4

HEAD_DIM = 64
A_HEADS = 4
A_WIDTH = A_HEADS * HEAD_DIM
B_WIDTH = 256
SHORT_CONV = 3
C_WIDTH = 256
CONF_CONV = 31
D_Q_HEADS = 8
D_KV_HEADS = 2
D_GROUP = D_Q_HEADS // D_KV_HEADS
D_WIDTH = D_Q_HEADS * HEAD_DIM
D_KV_WIDTH = D_KV_HEADS * HEAD_DIM
WINDOW = 128
Q_BLOCK = 128
N_BRANCH = 4
REL_BUCKETS = 32
REL_MAX_DIST = 128
PLE_DIM = 256
D_FF = 2816
EPS = 1e-6
NEG_INF = -1e30

A_QKV_END = 3 * A_WIDTH
A_F_END = A_QKV_END + A_HEADS
B_END = A_F_END + 3 * B_WIDTH
C_END = B_END + 2 * C_WIDTH
D_END = C_END + D_WIDTH + 2 * D_KV_WIDTH
N_IN = D_END + N_BRANCH * D_MODEL

kernel_name = 'hybrid_gated_parallel_mixer'


def rms_norm(x, g):
    xf = x.astype(jnp.float32)
    y = xf * lax.rsqrt(jnp.mean(xf * xf, axis=-1, keepdims=True) + EPS)
    return (y * g.astype(jnp.float32)).astype(x.dtype)


def layer_norm(x, g, b):
    xf = x.astype(jnp.float32)
    mu = jnp.mean(xf, axis=-1, keepdims=True)
    xc = xf - mu
    y = xc * lax.rsqrt(jnp.mean(xc * xc, axis=-1, keepdims=True) + EPS)
    return (y * g.astype(jnp.float32) + b.astype(jnp.float32)).astype(x.dtype)


def swiglu_ffn(x, w_gu, w_down):
    gate, up = jnp.split(x @ w_gu, 2, axis=-1)
    return (jax.nn.silu(gate) * up) @ w_down


def causal_depthwise_conv(x, w):
    K, C = w.shape
    return lax.conv_general_dilated(
        x, w[:, None, :].astype(x.dtype), window_strides=(1,), padding=[(K - 1, 0)],
        dimension_numbers=('NWC', 'WIO', 'NWC'), feature_group_count=C)


def t5_causal_bucket(dist):
    max_exact = REL_BUCKETS // 2
    large = max_exact + (jnp.log(jnp.maximum(dist, 1).astype(jnp.float32) / max_exact)
                         / math.log(REL_MAX_DIST / max_exact)
                         * (REL_BUCKETS - max_exact)).astype(jnp.int32)
    large = jnp.minimum(large, REL_BUCKETS - 1)
    return jnp.where(dist < max_exact, dist, large)


def forgetting_attention(q, k, v, log_f):
    S, d = q.shape[1], q.shape[3]
    c = jnp.transpose(jnp.cumsum(log_f, axis=1), (0, 2, 1))
    scale = d ** -0.5
    outs = []
    for blk in range(S // Q_BLOCK):
        q0 = blk * Q_BLOCK
        end = q0 + Q_BLOCK
        s = jnp.einsum('bqhd,bkhd->bhqk', q[:, q0:end], k[:, :end]).astype(jnp.float32) * scale
        decay = c[:, :, q0:end, None] - c[:, :, None, :end]
        causal = (q0 + jnp.arange(Q_BLOCK))[:, None] >= jnp.arange(end)[None, :]
        s = jnp.where(causal, s + decay, NEG_INF)
        pr = jax.nn.softmax(s, axis=-1).astype(v.dtype)
        outs.append(jnp.einsum('bhqk,bkhd->bqhd', pr, v[:, :end]))
    return jnp.concatenate(outs, axis=1)


def sliding_window_attention(q, k, v, sinks, band_bias):
    Bn, S, _, d = q.shape
    nb = S // Q_BLOCK
    qb = q.reshape(Bn, nb, Q_BLOCK, D_KV_HEADS, D_GROUP, d)

    def band(t):
        tb = t.reshape(Bn, nb, Q_BLOCK, D_KV_HEADS, d)
        prev = jnp.concatenate([jnp.zeros_like(tb[:, :1]), tb[:, :-1]], axis=1)
        return jnp.concatenate([prev, tb], axis=2)

    kb, vb = band(k), band(v)
    s = jnp.einsum('bnqkgd,bnskd->bnkgqs', qb, kb).astype(jnp.float32) * d ** -0.5
    s = s + band_bias.reshape(D_KV_HEADS, D_GROUP, Q_BLOCK, 2 * Q_BLOCK)
    kj = jnp.arange(2 * Q_BLOCK)[None, :]
    dist = jnp.arange(Q_BLOCK)[:, None] + Q_BLOCK - kj
    in_window = (dist >= 0) & (dist < WINDOW)
    key_pos = jnp.arange(nb)[:, None] * Q_BLOCK - Q_BLOCK + kj
    valid = in_window[None] & (key_pos >= 0)[:, None, :]
    s = jnp.where(valid[None, :, None, None], s, NEG_INF)
    sink = sinks.astype(jnp.float32).reshape(1, 1, D_KV_HEADS, D_GROUP, 1, 1)
    m = jnp.maximum(jnp.max(s, axis=-1, keepdims=True), sink)
    e = jnp.exp(s - m)
    pr = (e / (jnp.sum(e, axis=-1, keepdims=True) + jnp.exp(sink - m))).astype(v.dtype)
    out = jnp.einsum('bnkgqs,bnskd->bnqkgd', pr, vb)
    return out.reshape(Bn, S, D_Q_HEADS * d)


def _fwd_setup_inputs(seed: int = 0) -> dict:
    key = jax.random.key(seed)
    ks = iter(jax.random.split(key, 48))

    def nrm(shape, scale):
        return scale * jax.random.normal(next(ks), shape, jnp.float32)

    def gain(shape):
        return 1.0 + nrm(shape, 0.05)

    D, F = D_MODEL, D_FF
    return {
        'x': nrm((BATCH, SEQ, D), 1.0),
        'p': nrm((DEPTH, BATCH, SEQ, PLE_DIM), 1.0),
        'ffn1_norm_pre': gain((DEPTH, D)),
        'ffn1_w_gu': nrm((DEPTH, D, 2 * F), D ** -0.5),
        'ffn1_w_down': nrm((DEPTH, F, D), F ** -0.5),
        'ffn1_norm_post': gain((DEPTH, D)),
        'mix_norm_pre': gain((DEPTH, D)),
        'w_in': nrm((DEPTH, D, N_IN), D ** -0.5),
        'b_forget': 3.0 + nrm((DEPTH, A_HEADS), 1.0),
        'b_gate': nrm((DEPTH, N_BRANCH * D), 0.1),
        'conv_short': nrm((DEPTH, SHORT_CONV, B_WIDTH), SHORT_CONV ** -0.5),
        'conv_dw': nrm((DEPTH, CONF_CONV, C_WIDTH), CONF_CONV ** -0.5),
        'conv_dw_bias': nrm((DEPTH, C_WIDTH), 0.02),
        'conv_ln_gain': gain((DEPTH, C_WIDTH)),
        'conv_ln_bias': nrm((DEPTH, C_WIDTH), 0.02),
        'attn_sinks': nrm((DEPTH, D_Q_HEADS), 0.5),
        'rel_bias': nrm((REL_BUCKETS, D_Q_HEADS), 0.5),
        'w_br_a': nrm((DEPTH, A_WIDTH, D), A_WIDTH ** -0.5),
        'w_br_b': nrm((DEPTH, B_WIDTH, D), B_WIDTH ** -0.5),
        'w_br_c': nrm((DEPTH, C_WIDTH, D), C_WIDTH ** -0.5),
        'w_br_d': nrm((DEPTH, D_WIDTH, D), D_WIDTH ** -0.5),
        'w_o': nrm((DEPTH, D, D), D ** -0.5),
        'mix_norm_post': gain((DEPTH, D)),
        'ffn2_norm_pre': gain((DEPTH, D)),
        'ffn2_w_gu': nrm((DEPTH, D, 2 * F), D ** -0.5),
        'ffn2_w_down': nrm((DEPTH, F, D), F ** -0.5),
        'ffn2_norm_post': gain((DEPTH, D)),
        'ple_norm_gate': gain((DEPTH, D)),
        'w_ple_gate': nrm((DEPTH, D, D), D ** -0.5),
        'w_ple': nrm((DEPTH, PLE_DIM, D), PLE_DIM ** -0.5),
        'ple_norm_post': gain((DEPTH, D)),
    }


def _fwd_reference(x, p, ffn1_norm_pre, ffn1_w_gu, ffn1_w_down, ffn1_norm_post,
              mix_norm_pre, w_in, b_forget, b_gate, conv_short, conv_dw, conv_dw_bias,
              conv_ln_gain, conv_ln_bias, attn_sinks, rel_bias,
              w_br_a, w_br_b, w_br_c, w_br_d, w_o, mix_norm_post,
              ffn2_norm_pre, ffn2_w_gu, ffn2_w_down, ffn2_norm_post,
              ple_norm_gate, w_ple_gate, w_ple, ple_norm_post):
    Bn, S, _ = x.shape
    band_dist = jnp.maximum(jnp.arange(Q_BLOCK)[:, None] + Q_BLOCK - jnp.arange(2 * Q_BLOCK)[None, :], 0)
    band_bias = jnp.transpose(rel_bias[t5_causal_bucket(band_dist)], (2, 0, 1)).astype(jnp.float32)

    h = x
    for i in range(DEPTH):
        f1 = swiglu_ffn(rms_norm(h, ffn1_norm_pre[i]), ffn1_w_gu[i], ffn1_w_down[i])
        h = h + 0.5 * rms_norm(f1, ffn1_norm_post[i])

        u = rms_norm(h, mix_norm_pre[i])
        proj = u @ w_in[i]
        a_qkv = proj[..., :A_QKV_END]
        a_f = proj[..., A_QKV_END:A_F_END]
        b_in = proj[..., A_F_END:B_END]
        c_in = proj[..., B_END:C_END]
        d_qkv = proj[..., C_END:D_END]
        gates = proj[..., D_END:]

        qa, ka, va = [t.reshape(Bn, S, A_HEADS, HEAD_DIM) for t in jnp.split(a_qkv, 3, axis=-1)]
        log_f = jax.nn.log_sigmoid(a_f.astype(jnp.float32) + b_forget[i].astype(jnp.float32))
        ya = forgetting_attention(qa, ka, va, log_f).reshape(Bn, S, A_WIDTH)

        bg, cg, xb = jnp.split(b_in, 3, axis=-1)
        yb = bg * causal_depthwise_conv(cg * xb, conv_short[i])

        glu = c_in[..., :C_WIDTH] * jax.nn.sigmoid(c_in[..., C_WIDTH:])
        yc = causal_depthwise_conv(glu, conv_dw[i]) + conv_dw_bias[i]
        yc = jax.nn.silu(layer_norm(yc, conv_ln_gain[i], conv_ln_bias[i]))

        qd = d_qkv[..., :D_WIDTH].reshape(Bn, S, D_Q_HEADS, HEAD_DIM)
        kd = d_qkv[..., D_WIDTH:D_WIDTH + D_KV_WIDTH].reshape(Bn, S, D_KV_HEADS, HEAD_DIM)
        vd = d_qkv[..., D_WIDTH + D_KV_WIDTH:].reshape(Bn, S, D_KV_HEADS, HEAD_DIM)
        yd = sliding_window_attention(qd, kd, vd, attn_sinks[i], band_bias)

        g = jax.nn.sigmoid(gates + b_gate[i]).reshape(Bn, S, N_BRANCH, D_MODEL)
        merged = (g[..., 0, :] * (ya @ w_br_a[i]) + g[..., 1, :] * (yb @ w_br_b[i])
                  + g[..., 2, :] * (yc @ w_br_c[i]) + g[..., 3, :] * (yd @ w_br_d[i]))
        h = h + rms_norm(merged @ w_o[i], mix_norm_post[i])

        f2 = swiglu_ffn(rms_norm(h, ffn2_norm_pre[i]), ffn2_w_gu[i], ffn2_w_down[i])
        h = h + 0.5 * rms_norm(f2, ffn2_norm_post[i])

        pg = jax.nn.sigmoid(rms_norm(h, ple_norm_gate[i]) @ w_ple_gate[i])
        h = h + pg * rms_norm(p[i] @ w_ple[i], ple_norm_post[i])
    return h


import jax as _jax
import jax.numpy as _jnp

TWIN_FORMAT = 'train_step'
FWD_PARAMS = ['x', 'p', 'ffn1_norm_pre', 'ffn1_w_gu', 'ffn1_w_down', 'ffn1_norm_post', 'mix_norm_pre', 'w_in', 'b_forget', 'b_gate', 'conv_short', 'conv_dw', 'conv_dw_bias', 'conv_ln_gain', 'conv_ln_bias', 'attn_sinks', 'rel_bias', 'w_br_a', 'w_br_b', 'w_br_c', 'w_br_d', 'w_o', 'mix_norm_post', 'ffn2_norm_pre', 'ffn2_w_gu', 'ffn2_w_down', 'ffn2_norm_post', 'ple_norm_gate', 'w_ple_gate', 'w_ple', 'ple_norm_post']
TWIN_WEIGHTS = ['ffn1_norm_pre', 'ffn1_w_gu', 'ffn1_w_down', 'ffn1_norm_post', 'mix_norm_pre', 'w_in', 'b_forget', 'b_gate', 'conv_short', 'conv_dw', 'conv_dw_bias', 'conv_ln_gain', 'conv_ln_bias', 'attn_sinks', 'rel_bias', 'w_br_a', 'w_br_b', 'w_br_c', 'w_br_d', 'w_o', 'mix_norm_post', 'ffn2_norm_pre', 'ffn2_w_gu', 'ffn2_w_down', 'ffn2_norm_post', 'ple_norm_gate', 'w_ple_gate', 'w_ple', 'ple_norm_post']
TWIN_DIFF_INPUT = 'x'
TWIN_INPUTS = ['x', 'p', 'ffn1_norm_pre', 'ffn1_w_gu', 'ffn1_w_down', 'ffn1_norm_post', 'mix_norm_pre', 'w_in', 'b_forget', 'b_gate', 'conv_short', 'conv_dw', 'conv_dw_bias', 'conv_ln_gain', 'conv_ln_bias', 'attn_sinks', 'rel_bias', 'w_br_a', 'w_br_b', 'w_br_c', 'w_br_d', 'w_o', 'mix_norm_post', 'ffn2_norm_pre', 'ffn2_w_gu', 'ffn2_w_down', 'ffn2_norm_post', 'ple_norm_gate', 'w_ple_gate', 'w_ple', 'ple_norm_post', 'loss_target', 'm_ffn1_norm_pre', 'm_ffn1_w_gu', 'm_ffn1_w_down', 'm_ffn1_norm_post', 'm_mix_norm_pre', 'm_w_in', 'm_b_forget', 'm_b_gate', 'm_conv_short', 'm_conv_dw', 'm_conv_dw_bias', 'm_conv_ln_gain', 'm_conv_ln_bias', 'm_attn_sinks', 'm_rel_bias', 'm_w_br_a', 'm_w_br_b', 'm_w_br_c', 'm_w_br_d', 'm_w_o', 'm_mix_norm_post', 'm_ffn2_norm_pre', 'm_ffn2_w_gu', 'm_ffn2_w_down', 'm_ffn2_norm_post', 'm_ple_norm_gate', 'm_w_ple_gate', 'm_w_ple', 'm_ple_norm_post', 'v_ffn1_norm_pre', 'v_ffn1_w_gu', 'v_ffn1_w_down', 'v_ffn1_norm_post', 'v_mix_norm_pre', 'v_w_in', 'v_b_forget', 'v_b_gate', 'v_conv_short', 'v_conv_dw', 'v_conv_dw_bias', 'v_conv_ln_gain', 'v_conv_ln_bias', 'v_attn_sinks', 'v_rel_bias', 'v_w_br_a', 'v_w_br_b', 'v_w_br_c', 'v_w_br_d', 'v_w_o', 'v_mix_norm_post', 'v_ffn2_norm_pre', 'v_ffn2_w_gu', 'v_ffn2_w_down', 'v_ffn2_norm_post', 'v_ple_norm_gate', 'v_w_ple_gate', 'v_w_ple', 'v_ple_norm_post']
TWIN_OUTPUTS = ['loss', 'grad_x', 'grad_ffn1_norm_pre', 'grad_ffn1_w_gu', 'grad_ffn1_w_down', 'grad_ffn1_norm_post', 'grad_mix_norm_pre', 'grad_w_in', 'grad_b_forget', 'grad_b_gate', 'grad_conv_short', 'grad_conv_dw', 'grad_conv_dw_bias', 'grad_conv_ln_gain', 'grad_conv_ln_bias', 'grad_attn_sinks', 'grad_rel_bias', 'grad_w_br_a', 'grad_w_br_b', 'grad_w_br_c', 'grad_w_br_d', 'grad_w_o', 'grad_mix_norm_post', 'grad_ffn2_norm_pre', 'grad_ffn2_w_gu', 'grad_ffn2_w_down', 'grad_ffn2_norm_post', 'grad_ple_norm_gate', 'grad_w_ple_gate', 'grad_w_ple', 'grad_ple_norm_post', 'delta_ffn1_norm_pre', 'delta_ffn1_w_gu', 'delta_ffn1_w_down', 'delta_ffn1_norm_post', 'delta_mix_norm_pre', 'delta_w_in', 'delta_b_forget', 'delta_b_gate', 'delta_conv_short', 'delta_conv_dw', 'delta_conv_dw_bias', 'delta_conv_ln_gain', 'delta_conv_ln_bias', 'delta_attn_sinks', 'delta_rel_bias', 'delta_w_br_a', 'delta_w_br_b', 'delta_w_br_c', 'delta_w_br_d', 'delta_w_o', 'delta_mix_norm_post', 'delta_ffn2_norm_pre', 'delta_ffn2_w_gu', 'delta_ffn2_w_down', 'delta_ffn2_norm_post', 'delta_ple_norm_gate', 'delta_w_ple_gate', 'delta_w_ple', 'delta_ple_norm_post', 'new_m_ffn1_norm_pre', 'new_m_ffn1_w_gu', 'new_m_ffn1_w_down', 'new_m_ffn1_norm_post', 'new_m_mix_norm_pre', 'new_m_w_in', 'new_m_b_forget', 'new_m_b_gate', 'new_m_conv_short', 'new_m_conv_dw', 'new_m_conv_dw_bias', 'new_m_conv_ln_gain', 'new_m_conv_ln_bias', 'new_m_attn_sinks', 'new_m_rel_bias', 'new_m_w_br_a', 'new_m_w_br_b', 'new_m_w_br_c', 'new_m_w_br_d', 'new_m_w_o', 'new_m_mix_norm_post', 'new_m_ffn2_norm_pre', 'new_m_ffn2_w_gu', 'new_m_ffn2_w_down', 'new_m_ffn2_norm_post', 'new_m_ple_norm_gate', 'new_m_w_ple_gate', 'new_m_w_ple', 'new_m_ple_norm_post', 'new_v_ffn1_norm_pre', 'new_v_ffn1_w_gu', 'new_v_ffn1_w_down', 'new_v_ffn1_norm_post', 'new_v_mix_norm_pre', 'new_v_w_in', 'new_v_b_forget', 'new_v_b_gate', 'new_v_conv_short', 'new_v_conv_dw', 'new_v_conv_dw_bias', 'new_v_conv_ln_gain', 'new_v_conv_ln_bias', 'new_v_attn_sinks', 'new_v_rel_bias', 'new_v_w_br_a', 'new_v_w_br_b', 'new_v_w_br_c', 'new_v_w_br_d', 'new_v_w_o', 'new_v_mix_norm_post', 'new_v_ffn2_norm_pre', 'new_v_ffn2_w_gu', 'new_v_ffn2_w_down', 'new_v_ffn2_norm_post', 'new_v_ple_norm_gate', 'new_v_w_ple_gate', 'new_v_w_ple', 'new_v_ple_norm_post']
TWIN_LEAF_KINDS = {'loss': 'loss', 'grad_x': 'grad_x', 'grad_ffn1_norm_pre': 'grad_w', 'grad_ffn1_w_gu': 'grad_w', 'grad_ffn1_w_down': 'grad_w', 'grad_ffn1_norm_post': 'grad_w', 'grad_mix_norm_pre': 'grad_w', 'grad_w_in': 'grad_w', 'grad_b_forget': 'grad_w', 'grad_b_gate': 'grad_w', 'grad_conv_short': 'grad_w', 'grad_conv_dw': 'grad_w', 'grad_conv_dw_bias': 'grad_w', 'grad_conv_ln_gain': 'grad_w', 'grad_conv_ln_bias': 'grad_w', 'grad_attn_sinks': 'grad_w', 'grad_rel_bias': 'grad_w', 'grad_w_br_a': 'grad_w', 'grad_w_br_b': 'grad_w', 'grad_w_br_c': 'grad_w', 'grad_w_br_d': 'grad_w', 'grad_w_o': 'grad_w', 'grad_mix_norm_post': 'grad_w', 'grad_ffn2_norm_pre': 'grad_w', 'grad_ffn2_w_gu': 'grad_w', 'grad_ffn2_w_down': 'grad_w', 'grad_ffn2_norm_post': 'grad_w', 'grad_ple_norm_gate': 'grad_w', 'grad_w_ple_gate': 'grad_w', 'grad_w_ple': 'grad_w', 'grad_ple_norm_post': 'grad_w', 'delta_ffn1_norm_pre': 'delta_w', 'delta_ffn1_w_gu': 'delta_w', 'delta_ffn1_w_down': 'delta_w', 'delta_ffn1_norm_post': 'delta_w', 'delta_mix_norm_pre': 'delta_w', 'delta_w_in': 'delta_w', 'delta_b_forget': 'delta_w', 'delta_b_gate': 'delta_w', 'delta_conv_short': 'delta_w', 'delta_conv_dw': 'delta_w', 'delta_conv_dw_bias': 'delta_w', 'delta_conv_ln_gain': 'delta_w', 'delta_conv_ln_bias': 'delta_w', 'delta_attn_sinks': 'delta_w', 'delta_rel_bias': 'delta_w', 'delta_w_br_a': 'delta_w', 'delta_w_br_b': 'delta_w', 'delta_w_br_c': 'delta_w', 'delta_w_br_d': 'delta_w', 'delta_w_o': 'delta_w', 'delta_mix_norm_post': 'delta_w', 'delta_ffn2_norm_pre': 'delta_w', 'delta_ffn2_w_gu': 'delta_w', 'delta_ffn2_w_down': 'delta_w', 'delta_ffn2_norm_post': 'delta_w', 'delta_ple_norm_gate': 'delta_w', 'delta_w_ple_gate': 'delta_w', 'delta_w_ple': 'delta_w', 'delta_ple_norm_post': 'delta_w', 'new_m_ffn1_norm_pre': 'new_m', 'new_m_ffn1_w_gu': 'new_m', 'new_m_ffn1_w_down': 'new_m', 'new_m_ffn1_norm_post': 'new_m', 'new_m_mix_norm_pre': 'new_m', 'new_m_w_in': 'new_m', 'new_m_b_forget': 'new_m', 'new_m_b_gate': 'new_m', 'new_m_conv_short': 'new_m', 'new_m_conv_dw': 'new_m', 'new_m_conv_dw_bias': 'new_m', 'new_m_conv_ln_gain': 'new_m', 'new_m_conv_ln_bias': 'new_m', 'new_m_attn_sinks': 'new_m', 'new_m_rel_bias': 'new_m', 'new_m_w_br_a': 'new_m', 'new_m_w_br_b': 'new_m', 'new_m_w_br_c': 'new_m', 'new_m_w_br_d': 'new_m', 'new_m_w_o': 'new_m', 'new_m_mix_norm_post': 'new_m', 'new_m_ffn2_norm_pre': 'new_m', 'new_m_ffn2_w_gu': 'new_m', 'new_m_ffn2_w_down': 'new_m', 'new_m_ffn2_norm_post': 'new_m', 'new_m_ple_norm_gate': 'new_m', 'new_m_w_ple_gate': 'new_m', 'new_m_w_ple': 'new_m', 'new_m_ple_norm_post': 'new_m', 'new_v_ffn1_norm_pre': 'new_v', 'new_v_ffn1_w_gu': 'new_v', 'new_v_ffn1_w_down': 'new_v', 'new_v_ffn1_norm_post': 'new_v', 'new_v_mix_norm_pre': 'new_v', 'new_v_w_in': 'new_v', 'new_v_b_forget': 'new_v', 'new_v_b_gate': 'new_v', 'new_v_conv_short': 'new_v', 'new_v_conv_dw': 'new_v', 'new_v_conv_dw_bias': 'new_v', 'new_v_conv_ln_gain': 'new_v', 'new_v_conv_ln_bias': 'new_v', 'new_v_attn_sinks': 'new_v', 'new_v_rel_bias': 'new_v', 'new_v_w_br_a': 'new_v', 'new_v_w_br_b': 'new_v', 'new_v_w_br_c': 'new_v', 'new_v_w_br_d': 'new_v', 'new_v_w_o': 'new_v', 'new_v_mix_norm_post': 'new_v', 'new_v_ffn2_norm_pre': 'new_v', 'new_v_ffn2_w_gu': 'new_v', 'new_v_ffn2_w_down': 'new_v', 'new_v_ffn2_norm_post': 'new_v', 'new_v_ple_norm_gate': 'new_v', 'new_v_w_ple_gate': 'new_v', 'new_v_w_ple': 'new_v', 'new_v_ple_norm_post': 'new_v'}


def _forward(args):
    return _fwd_reference(*[args[k] for k in FWD_PARAMS])


def _output_shape():
    out = _jax.eval_shape(lambda: _forward(_fwd_setup_inputs(0)))
    return out.shape, out.dtype

N_MICROBATCH = 1
ADAM_LR = 0.001
ADAM_B1 = 0.9
ADAM_B2 = 0.999
ADAM_EPS = 1e-08
ADAM_WD = 0.01
ADAM_STEP = 10
PER_EXAMPLE_BATCH_AXIS = {'x': 0, 'p': 1, 'loss_target': 0}
SHARED_INPUTS = []
_WEIGHT_DTYPES = {'ffn1_norm_pre': _jnp.float32, 'ffn1_w_gu': _jnp.float32, 'ffn1_w_down': _jnp.float32, 'ffn1_norm_post': _jnp.float32, 'mix_norm_pre': _jnp.float32, 'w_in': _jnp.float32, 'b_forget': _jnp.float32, 'b_gate': _jnp.float32, 'conv_short': _jnp.float32, 'conv_dw': _jnp.float32, 'conv_dw_bias': _jnp.float32, 'conv_ln_gain': _jnp.float32, 'conv_ln_bias': _jnp.float32, 'attn_sinks': _jnp.float32, 'rel_bias': _jnp.float32, 'w_br_a': _jnp.float32, 'w_br_b': _jnp.float32, 'w_br_c': _jnp.float32, 'w_br_d': _jnp.float32, 'w_o': _jnp.float32, 'mix_norm_post': _jnp.float32, 'ffn2_norm_pre': _jnp.float32, 'ffn2_w_gu': _jnp.float32, 'ffn2_w_down': _jnp.float32, 'ffn2_norm_post': _jnp.float32, 'ple_norm_gate': _jnp.float32, 'w_ple_gate': _jnp.float32, 'w_ple': _jnp.float32, 'ple_norm_post': _jnp.float32}
MOMENT_SCALE = {'ffn1_norm_pre': 1.753731e+00, 'ffn1_w_gu': 7.584980e-01, 'ffn1_w_down': 1.270779e+00, 'ffn1_norm_post': 1.394360e+01, 'mix_norm_pre': 2.706412e+00, 'w_in': 1.028179e+00, 'b_forget': 3.773870e+00, 'b_gate': 5.814401e-01, 'conv_short': 2.412659e+00, 'conv_dw': 2.022606e+00, 'conv_dw_bias': 2.967816e+01, 'conv_ln_gain': 1.127642e+01, 'conv_ln_bias': 1.883612e+01, 'attn_sinks': 2.236526e-01, 'rel_bias': 9.401521e-01, 'w_br_a': 9.752468e-01, 'w_br_b': 1.236735e+00, 'w_br_c': 2.883007e+00, 'w_br_d': 8.953059e-01, 'w_o': 2.902355e+00, 'mix_norm_post': 6.294085e+01, 'ffn2_norm_pre': 1.222042e+00, 'ffn2_w_gu': 4.943392e-01, 'ffn2_w_down': 8.768564e-01, 'ffn2_norm_post': 1.500549e+01, 'ple_norm_gate': 1.482388e+00, 'w_ple_gate': 4.773744e-01, 'w_ple': 7.479462e-01, 'ple_norm_post': 1.775683e+01}


def _to_microbatches(a, axis):
    t = _jnp.moveaxis(a, axis, 0)
    t = t.reshape((N_MICROBATCH, t.shape[0] // N_MICROBATCH) + t.shape[1:])
    return _jnp.moveaxis(t, 1, axis + 1)


def setup_inputs(seed: int = 0) -> dict:
    inp = _fwd_setup_inputs(seed)
    key = _jax.random.fold_in(_jax.random.key(seed), 7919)
    shape, _ = _output_shape()
    out = dict(inp)
    out["loss_target"] = _jax.random.normal(_jax.random.fold_in(key, 0), shape, _jnp.float32)
    for i, name in enumerate(TWIN_WEIGHTS):
        w = inp[name].astype(_jnp.float32)
        if MOMENT_SCALE is None:
            s = _jnp.sqrt(_jnp.mean(_jnp.square(w)) + 1e-30)
        else:
            s = MOMENT_SCALE[name]
        km, kv = _jax.random.split(_jax.random.fold_in(key, i + 1))
        out[name] = w
        out["m_" + name] = s * _jax.random.normal(km, w.shape, _jnp.float32)
        out["v_" + name] = (s * s) * _jax.random.uniform(kv, w.shape, _jnp.float32, 0.5, 1.5)
    if N_MICROBATCH > 1:
        for name, axis in PER_EXAMPLE_BATCH_AXIS.items():
            out[name] = _to_microbatches(out[name], axis)
    return {'x': out['x'], 'p': out['p'], 'ffn1_norm_pre': out['ffn1_norm_pre'], 'ffn1_w_gu': out['ffn1_w_gu'], 'ffn1_w_down': out['ffn1_w_down'], 'ffn1_norm_post': out['ffn1_norm_post'], 'mix_norm_pre': out['mix_norm_pre'], 'w_in': out['w_in'], 'b_forget': out['b_forget'], 'b_gate': out['b_gate'], 'conv_short': out['conv_short'], 'conv_dw': out['conv_dw'], 'conv_dw_bias': out['conv_dw_bias'], 'conv_ln_gain': out['conv_ln_gain'], 'conv_ln_bias': out['conv_ln_bias'], 'attn_sinks': out['attn_sinks'], 'rel_bias': out['rel_bias'], 'w_br_a': out['w_br_a'], 'w_br_b': out['w_br_b'], 'w_br_c': out['w_br_c'], 'w_br_d': out['w_br_d'], 'w_o': out['w_o'], 'mix_norm_post': out['mix_norm_post'], 'ffn2_norm_pre': out['ffn2_norm_pre'], 'ffn2_w_gu': out['ffn2_w_gu'], 'ffn2_w_down': out['ffn2_w_down'], 'ffn2_norm_post': out['ffn2_norm_post'], 'ple_norm_gate': out['ple_norm_gate'], 'w_ple_gate': out['w_ple_gate'], 'w_ple': out['w_ple'], 'ple_norm_post': out['ple_norm_post'], 'loss_target': out['loss_target'], 'm_ffn1_norm_pre': out['m_ffn1_norm_pre'], 'm_ffn1_w_gu': out['m_ffn1_w_gu'], 'm_ffn1_w_down': out['m_ffn1_w_down'], 'm_ffn1_norm_post': out['m_ffn1_norm_post'], 'm_mix_norm_pre': out['m_mix_norm_pre'], 'm_w_in': out['m_w_in'], 'm_b_forget': out['m_b_forget'], 'm_b_gate': out['m_b_gate'], 'm_conv_short': out['m_conv_short'], 'm_conv_dw': out['m_conv_dw'], 'm_conv_dw_bias': out['m_conv_dw_bias'], 'm_conv_ln_gain': out['m_conv_ln_gain'], 'm_conv_ln_bias': out['m_conv_ln_bias'], 'm_attn_sinks': out['m_attn_sinks'], 'm_rel_bias': out['m_rel_bias'], 'm_w_br_a': out['m_w_br_a'], 'm_w_br_b': out['m_w_br_b'], 'm_w_br_c': out['m_w_br_c'], 'm_w_br_d': out['m_w_br_d'], 'm_w_o': out['m_w_o'], 'm_mix_norm_post': out['m_mix_norm_post'], 'm_ffn2_norm_pre': out['m_ffn2_norm_pre'], 'm_ffn2_w_gu': out['m_ffn2_w_gu'], 'm_ffn2_w_down': out['m_ffn2_w_down'], 'm_ffn2_norm_post': out['m_ffn2_norm_post'], 'm_ple_norm_gate': out['m_ple_norm_gate'], 'm_w_ple_gate': out['m_w_ple_gate'], 'm_w_ple': out['m_w_ple'], 'm_ple_norm_post': out['m_ple_norm_post'], 'v_ffn1_norm_pre': out['v_ffn1_norm_pre'], 'v_ffn1_w_gu': out['v_ffn1_w_gu'], 'v_ffn1_w_down': out['v_ffn1_w_down'], 'v_ffn1_norm_post': out['v_ffn1_norm_post'], 'v_mix_norm_pre': out['v_mix_norm_pre'], 'v_w_in': out['v_w_in'], 'v_b_forget': out['v_b_forget'], 'v_b_gate': out['v_b_gate'], 'v_conv_short': out['v_conv_short'], 'v_conv_dw': out['v_conv_dw'], 'v_conv_dw_bias': out['v_conv_dw_bias'], 'v_conv_ln_gain': out['v_conv_ln_gain'], 'v_conv_ln_bias': out['v_conv_ln_bias'], 'v_attn_sinks': out['v_attn_sinks'], 'v_rel_bias': out['v_rel_bias'], 'v_w_br_a': out['v_w_br_a'], 'v_w_br_b': out['v_w_br_b'], 'v_w_br_c': out['v_w_br_c'], 'v_w_br_d': out['v_w_br_d'], 'v_w_o': out['v_w_o'], 'v_mix_norm_post': out['v_mix_norm_post'], 'v_ffn2_norm_pre': out['v_ffn2_norm_pre'], 'v_ffn2_w_gu': out['v_ffn2_w_gu'], 'v_ffn2_w_down': out['v_ffn2_w_down'], 'v_ffn2_norm_post': out['v_ffn2_norm_post'], 'v_ple_norm_gate': out['v_ple_norm_gate'], 'v_w_ple_gate': out['v_w_ple_gate'], 'v_w_ple': out['v_w_ple'], 'v_ple_norm_post': out['v_ple_norm_post']}


def _loss(weights, diff, rest, loss_target):
    with _jax.named_scope("forward"):
        args = {**rest, TWIN_DIFF_INPUT: diff, **{k: w.astype(_WEIGHT_DTYPES[k]) for k, w in weights.items()}}
        y = _forward(args)
    with _jax.named_scope("loss_head"):
        err = _jnp.square(y.astype(_jnp.float32) - loss_target)
        return 0.5 * _jnp.sum(_jnp.mean(err, axis=-1)) if err.ndim else 0.5 * err


def _adamw(w, g, m, v):
    m = ADAM_B1 * m + (1.0 - ADAM_B1) * g
    v = ADAM_B2 * v + (1.0 - ADAM_B2) * _jnp.square(g)
    m_hat = m / (1.0 - ADAM_B1 ** ADAM_STEP)
    v_hat = v / (1.0 - ADAM_B2 ** ADAM_STEP)
    delta = -ADAM_LR * (m_hat / (_jnp.sqrt(v_hat) + ADAM_EPS) + ADAM_WD * w)
    return delta, m, v


def reference(x, p, ffn1_norm_pre, ffn1_w_gu, ffn1_w_down, ffn1_norm_post, mix_norm_pre, w_in, b_forget, b_gate, conv_short, conv_dw, conv_dw_bias, conv_ln_gain, conv_ln_bias, attn_sinks, rel_bias, w_br_a, w_br_b, w_br_c, w_br_d, w_o, mix_norm_post, ffn2_norm_pre, ffn2_w_gu, ffn2_w_down, ffn2_norm_post, ple_norm_gate, w_ple_gate, w_ple, ple_norm_post, loss_target, m_ffn1_norm_pre, m_ffn1_w_gu, m_ffn1_w_down, m_ffn1_norm_post, m_mix_norm_pre, m_w_in, m_b_forget, m_b_gate, m_conv_short, m_conv_dw, m_conv_dw_bias, m_conv_ln_gain, m_conv_ln_bias, m_attn_sinks, m_rel_bias, m_w_br_a, m_w_br_b, m_w_br_c, m_w_br_d, m_w_o, m_mix_norm_post, m_ffn2_norm_pre, m_ffn2_w_gu, m_ffn2_w_down, m_ffn2_norm_post, m_ple_norm_gate, m_w_ple_gate, m_w_ple, m_ple_norm_post, v_ffn1_norm_pre, v_ffn1_w_gu, v_ffn1_w_down, v_ffn1_norm_post, v_mix_norm_pre, v_w_in, v_b_forget, v_b_gate, v_conv_short, v_conv_dw, v_conv_dw_bias, v_conv_ln_gain, v_conv_ln_bias, v_attn_sinks, v_rel_bias, v_w_br_a, v_w_br_b, v_w_br_c, v_w_br_d, v_w_o, v_mix_norm_post, v_ffn2_norm_pre, v_ffn2_w_gu, v_ffn2_w_down, v_ffn2_norm_post, v_ple_norm_gate, v_w_ple_gate, v_w_ple, v_ple_norm_post):
    given = dict(x=x, p=p, ffn1_norm_pre=ffn1_norm_pre, ffn1_w_gu=ffn1_w_gu, ffn1_w_down=ffn1_w_down, ffn1_norm_post=ffn1_norm_post, mix_norm_pre=mix_norm_pre, w_in=w_in, b_forget=b_forget, b_gate=b_gate, conv_short=conv_short, conv_dw=conv_dw, conv_dw_bias=conv_dw_bias, conv_ln_gain=conv_ln_gain, conv_ln_bias=conv_ln_bias, attn_sinks=attn_sinks, rel_bias=rel_bias, w_br_a=w_br_a, w_br_b=w_br_b, w_br_c=w_br_c, w_br_d=w_br_d, w_o=w_o, mix_norm_post=mix_norm_post, ffn2_norm_pre=ffn2_norm_pre, ffn2_w_gu=ffn2_w_gu, ffn2_w_down=ffn2_w_down, ffn2_norm_post=ffn2_norm_post, ple_norm_gate=ple_norm_gate, w_ple_gate=w_ple_gate, w_ple=w_ple, ple_norm_post=ple_norm_post, loss_target=loss_target, m_ffn1_norm_pre=m_ffn1_norm_pre, m_ffn1_w_gu=m_ffn1_w_gu, m_ffn1_w_down=m_ffn1_w_down, m_ffn1_norm_post=m_ffn1_norm_post, m_mix_norm_pre=m_mix_norm_pre, m_w_in=m_w_in, m_b_forget=m_b_forget, m_b_gate=m_b_gate, m_conv_short=m_conv_short, m_conv_dw=m_conv_dw, m_conv_dw_bias=m_conv_dw_bias, m_conv_ln_gain=m_conv_ln_gain, m_conv_ln_bias=m_conv_ln_bias, m_attn_sinks=m_attn_sinks, m_rel_bias=m_rel_bias, m_w_br_a=m_w_br_a, m_w_br_b=m_w_br_b, m_w_br_c=m_w_br_c, m_w_br_d=m_w_br_d, m_w_o=m_w_o, m_mix_norm_post=m_mix_norm_post, m_ffn2_norm_pre=m_ffn2_norm_pre, m_ffn2_w_gu=m_ffn2_w_gu, m_ffn2_w_down=m_ffn2_w_down, m_ffn2_norm_post=m_ffn2_norm_post, m_ple_norm_gate=m_ple_norm_gate, m_w_ple_gate=m_w_ple_gate, m_w_ple=m_w_ple, m_ple_norm_post=m_ple_norm_post, v_ffn1_norm_pre=v_ffn1_norm_pre, v_ffn1_w_gu=v_ffn1_w_gu, v_ffn1_w_down=v_ffn1_w_down, v_ffn1_norm_post=v_ffn1_norm_post, v_mix_norm_pre=v_mix_norm_pre, v_w_in=v_w_in, v_b_forget=v_b_forget, v_b_gate=v_b_gate, v_conv_short=v_conv_short, v_conv_dw=v_conv_dw, v_conv_dw_bias=v_conv_dw_bias, v_conv_ln_gain=v_conv_ln_gain, v_conv_ln_bias=v_conv_ln_bias, v_attn_sinks=v_attn_sinks, v_rel_bias=v_rel_bias, v_w_br_a=v_w_br_a, v_w_br_b=v_w_br_b, v_w_br_c=v_w_br_c, v_w_br_d=v_w_br_d, v_w_o=v_w_o, v_mix_norm_post=v_mix_norm_post, v_ffn2_norm_pre=v_ffn2_norm_pre, v_ffn2_w_gu=v_ffn2_w_gu, v_ffn2_w_down=v_ffn2_w_down, v_ffn2_norm_post=v_ffn2_norm_post, v_ple_norm_gate=v_ple_norm_gate, v_w_ple_gate=v_w_ple_gate, v_w_ple=v_w_ple, v_ple_norm_post=v_ple_norm_post)
    weights = {n: given[n] for n in TWIN_WEIGHTS}
    shared = {n: given[n] for n in SHARED_INPUTS}
    per_example = {n: given[n] for n in ['x', 'p']}
    grad_fn = _jax.value_and_grad(_loss, argnums=(0, 1))

    def one_microbatch(ex, loss_target):
        ex = dict(ex)
        diff = ex.pop(TWIN_DIFF_INPUT)
        return grad_fn(weights, diff, {**shared, **ex}, loss_target)

    if N_MICROBATCH == 1:
        loss, (grad_w, grad_x) = one_microbatch(per_example, given["loss_target"])
    else:
        def body(carry, xs):
            loss_sum, grad_sum = carry
            l_k, (gw_k, gx_k) = one_microbatch(xs[0], xs[1])
            with _jax.named_scope("update"):
                return (loss_sum + l_k, _jax.tree.map(_jnp.add, grad_sum, gw_k)), gx_k

        init = (_jnp.zeros((), _jnp.float32), _jax.tree.map(_jnp.zeros_like, weights))
        (loss, grad_w), grad_x = _jax.lax.scan(body, init, (per_example, given["loss_target"]))
    with _jax.named_scope("update"):
        delta_w, new_m, new_v = {}, {}, {}
        for n in TWIN_WEIGHTS:
            delta_w[n], new_m[n], new_v[n] = _adamw(weights[n], grad_w[n], given["m_" + n], given["v_" + n])
    return (loss, grad_x, *[grad_w[n] for n in TWIN_WEIGHTS], *[delta_w[n] for n in TWIN_WEIGHTS],
            *[new_m[n] for n in TWIN_WEIGHTS], *[new_v[n] for n in TWIN_WEIGHTS])
```

```python
import functools
import math

import jax
import jax.numpy as jnp
from jax import lax
from jax.experimental import pallas as pl
from jax.experimental.pallas import tpu as pltpu

F32 = jnp.float32
BF16 = jnp.bfloat16
MESH = pl.DeviceIdType.MESH

D_MODEL = 1024
DEPTH = 4
HEAD_DIM = 64
A_HEADS = 4
D_Q_HEADS = 8
D_KV_HEADS = 2
D_GROUP = 4
WINDOW = 128
QB = 128
REL_BUCKETS = 32
REL_MAX_DIST = 128
D_FF = 2816
EPS = 1e-6
NEG = -1e30
SCALE = HEAD_DIM ** -0.5
N_CHIPS = 4
N_DEV = 8

ADAM_LR = 0.001
ADAM_B1 = 0.9
ADAM_B2 = 0.999
ADAM_EPS = 1e-08
ADAM_WD = 0.01
ADAM_STEP = 10

C_GATE = 0
C_AQ, C_AK, C_AV = 4096, 4352, 4608
C_BG, C_CG, C_XB = 4864, 5120, 5376
C_CA, C_CB = 5632, 5888
C_DQ, C_DK, C_DV = 6144, 6656, 6784
C_AF = 6912
N_PROJ = 7168

VMEM_LIMIT = 56 * 1024 * 1024
PACK_COLS = 1024
PACK_ROW_ALIGN = 1024

SHARDED = ('ffn1_w_gu', 'ffn1_w_down', 'w_in', 'w_br_a', 'w_br_b', 'w_br_c', 'w_br_d', 'w_o',
           'ffn2_w_gu', 'ffn2_w_down', 'w_ple_gate', 'w_ple')
ROW_SHARDED = ('ffn1_w_down', 'w_o', 'ffn2_w_down', 'w_ple_gate')
CONV_SHARDED = ('conv_short', 'conv_dw')
REPLICATED = ('ffn1_norm_pre', 'ffn1_norm_post', 'mix_norm_pre', 'b_forget', 'b_gate', 'conv_dw_bias',
              'conv_ln_gain', 'conv_ln_bias', 'attn_sinks', 'rel_bias', 'mix_norm_post', 'ffn2_norm_pre',
              'ffn2_norm_post', 'ple_norm_gate', 'ple_norm_post')
WEIGHTS = ('ffn1_norm_pre', 'ffn1_w_gu', 'ffn1_w_down', 'ffn1_norm_post', 'mix_norm_pre', 'w_in', 'b_forget',
           'b_gate', 'conv_short', 'conv_dw', 'conv_dw_bias', 'conv_ln_gain', 'conv_ln_bias', 'attn_sinks',
           'rel_bias', 'w_br_a', 'w_br_b', 'w_br_c', 'w_br_d', 'w_o', 'mix_norm_post', 'ffn2_norm_pre',
           'ffn2_w_gu', 'ffn2_w_down', 'ffn2_norm_post', 'ple_norm_gate', 'w_ple_gate', 'w_ple', 'ple_norm_post')


def _cparams(sem=None):
    return pltpu.CompilerParams(dimension_semantics=sem, vmem_limit_bytes=VMEM_LIMIT)


def _pick(dim, cands):
    for c in cands:
        if dim % c == 0:
            return c
    return dim


def _mm(a, b, *, ta=False, tb=False, out_dtype=F32, name="mm"):
    if ta:
        kdim, m = a.shape
    else:
        m, kdim = a.shape
    if tb:
        n, kb = b.shape
    else:
        kb, n = b.shape
    assert kb == kdim, (a.shape, b.shape, ta, tb)
    tm = _pick(m, (1024, 512, 256, 128))
    tn = _pick(n, (1024, 512, 256, 128))
    tk = _pick(kdim, (512, 256, 128))
    nk = kdim // tk
    dims = (((0,) if ta else (1,), (1,) if tb else (0,)), ((), ()))

    def body(a_ref, b_ref, o_ref, acc_ref):
        k = pl.program_id(2)

        @pl.when(k == 0)
        def _():
            acc_ref[...] = jnp.zeros_like(acc_ref)

        acc_ref[...] += lax.dot_general(a_ref[...].astype(BF16), b_ref[...].astype(BF16), dims,
                                        preferred_element_type=F32)

        @pl.when(k == nk - 1)
        def _():
            o_ref[...] = acc_ref[...].astype(o_ref.dtype)

    a_spec = pl.BlockSpec((tk, tm), lambda i, j, k: (k, i)) if ta else pl.BlockSpec((tm, tk), lambda i, j, k: (i, k))
    b_spec = pl.BlockSpec((tn, tk), lambda i, j, k: (j, k)) if tb else pl.BlockSpec((tk, tn), lambda i, j, k: (k, j))
    return pl.pallas_call(
        body, name=name, grid=(m // tm, n // tn, nk),
        in_specs=[a_spec, b_spec], out_specs=pl.BlockSpec((tm, tn), lambda i, j, k: (i, j)),
        out_shape=jax.ShapeDtypeStruct((m, n), out_dtype),
        scratch_shapes=[pltpu.VMEM((tm, tn), F32)],
        compiler_params=_cparams(("parallel", "parallel", "arbitrary")),
    )(a, b)


def _rowwise(fn, rows, params, outs, pouts=(), *, tm=256, name="rowwise"):
    t = rows[0][0].shape[0]
    assert t % tm == 0
    n_r, n_p, n_o, n_po = len(rows), len(params), len(outs), len(pouts)

    def body(*refs):
        r_refs = refs[:n_r]
        p_refs = refs[n_r:n_r + n_p]
        o_refs = refs[n_r + n_p:n_r + n_p + n_o]
        po_refs = refs[n_r + n_p + n_o:]
        res = fn(*[r[...] for r in r_refs], *[p[...] for p in p_refs])
        if not isinstance(res, (tuple, list)):
            res = (res,)
        assert len(res) == n_o + n_po, (len(res), n_o, n_po)
        for o, val in zip(o_refs, res[:n_o]):
            o[...] = val.astype(o.dtype)
        if n_po:
            first = pl.program_id(0) == 0

            @pl.when(first)
            def _():
                for o, val in zip(po_refs, res[n_o:]):
                    o[...] = val.astype(F32)

            @pl.when(jnp.logical_not(first))
            def _():
                for o, val in zip(po_refs, res[n_o:]):
                    o[...] += val.astype(F32)

    in_specs = [pl.BlockSpec((tm, w), functools.partial(lambda i, cb: (i, cb), cb=cb)) for (_, w, cb) in rows]
    in_specs += [pl.BlockSpec(p.shape, lambda i: (0, 0)) for p in params]
    out_specs = [pl.BlockSpec((tm, w), lambda i: (i, 0)) for (w, _) in outs]
    out_specs += [pl.BlockSpec((1, w), lambda i: (0, 0)) for w in pouts]
    out_shape = [jax.ShapeDtypeStruct((t, w), dt) for (w, dt) in outs]
    out_shape += [jax.ShapeDtypeStruct((1, w), F32) for w in pouts]
    res = pl.pallas_call(
        body, name=name, grid=(t // tm,), in_specs=in_specs, out_specs=out_specs, out_shape=out_shape,
        compiler_params=_cparams(("arbitrary",)),
    )(*[r[0] for r in rows], *params)
    return res


def _full(a):
    return (a, a.shape[1], 0)


def _rms(x, g):
    x = x.astype(F32)
    return x * lax.rsqrt(jnp.mean(x * x, axis=-1, keepdims=True) + EPS) * g


def _sum0(v):
    return jnp.sum(v, axis=0, keepdims=True)


def rms_fwd(h, g, name):
    return _rowwise(lambda x, gg: _rms(x, gg), [_full(h)], [g], [(h.shape[1], BF16)], name=name)[0]


def rms_bwd(h, g, dn, dres, name):
    def fn(x, d, r, gg):
        _, vjp = jax.vjp(_rms, x, gg)
        dx, dg = vjp(d.astype(F32))
        return dx + r, dg
    w = h.shape[1]
    return _rowwise(fn, [_full(h), _full(dn), _full(dres)], [g], [(w, F32)], [w], name=name)


def res_rms_fwd(h, f, g, coef, name):
    return _rowwise(lambda x, y, gg: x + coef * _rms(y, gg), [_full(h), _full(f)], [g], [(h.shape[1], F32)],
                    name=name)[0]


def res_rms_bwd(f, g, dh, coef, name):
    def fn(y, d, gg):
        _, vjp = jax.vjp(lambda a, b: coef * _rms(a, b), y, gg)
        dy, dg = vjp(d)
        return dy, dg
    w = f.shape[1]
    return _rowwise(fn, [_full(f), _full(dh)], [g], [(w, BF16)], [w], name=name)


def swiglu_fwd(gu, name):
    f = gu.shape[1] // 2

    def fn(gate, up):
        gate = gate.astype(F32)
        return gate * jax.nn.sigmoid(gate) * up.astype(F32)
    return _rowwise(fn, [(gu, f, 0), (gu, f, 1)], [], [(f, BF16)], name=name)[0]


def swiglu_bwd(gu, da, name):
    t, f2 = gu.shape
    f = f2 // 2
    tm = 256

    def body(gate_ref, up_ref, da_ref, o_ref):
        gate = gate_ref[...].astype(F32)
        up = up_ref[...].astype(F32)
        d = da_ref[...].astype(F32)
        sg = jax.nn.sigmoid(gate)
        silu = gate * sg
        o_ref[:, :f] = (d * up * (sg + silu * (1.0 - sg))).astype(o_ref.dtype)
        o_ref[:, f:] = (d * silu).astype(o_ref.dtype)

    return pl.pallas_call(
        body, name=name, grid=(t // tm,),
        in_specs=[pl.BlockSpec((tm, f), lambda i: (i, 0)), pl.BlockSpec((tm, f), lambda i: (i, 1)),
                  pl.BlockSpec((tm, f), lambda i: (i, 0))],
        out_specs=pl.BlockSpec((tm, f2), lambda i: (i, 0)),
        out_shape=jax.ShapeDtypeStruct((t, f2), BF16),
        compiler_params=_cparams(("parallel",)),
    )(gu, gu, da)


def _merge(g0, g1, g2, g3, z0, z1, z2, z3, b0, b1, b2, b3):
    acc = jax.nn.sigmoid(g0.astype(F32) + b0) * z0.astype(F32)
    acc += jax.nn.sigmoid(g1.astype(F32) + b1) * z1.astype(F32)
    acc += jax.nn.sigmoid(g2.astype(F32) + b2) * z2.astype(F32)
    acc += jax.nn.sigmoid(g3.astype(F32) + b3) * z3.astype(F32)
    return acc


def merge_fwd(proj, zs, bs, name):
    rows = [(proj, D_MODEL, k) for k in range(4)] + [_full(z) for z in zs]
    return _rowwise(_merge, rows, list(bs), [(D_MODEL, BF16)], name=name)[0]


def merge_bwd(proj, zs, bs, dmerged, name):
    def fn(*args):
        d = args[8].astype(F32)
        prim = args[:8] + args[9:]
        _, vjp = jax.vjp(_merge, *prim)
        return vjp(d)
    rows = [(proj, D_MODEL, k) for k in range(4)] + [_full(z) for z in zs] + [_full(dmerged)]
    outs = [(D_MODEL, BF16)] * 8
    return _rowwise(fn, rows, list(bs), outs, [D_MODEL] * 4, name=name)


def _ple(pgl, pr, g):
    return jax.nn.sigmoid(pgl.astype(F32)) * _rms(pr, g)


def ple_fwd(h, pgl, pr, g, name):
    return _rowwise(lambda x, a, b, gg: x + _ple(a, b, gg), [_full(h), _full(pgl), _full(pr)], [g],
                    [(D_MODEL, F32)], name=name)[0]


def ple_bwd(pgl, pr, g, dh, name):
    def fn(a, b, d, gg):
        _, vjp = jax.vjp(_ple, a, b, gg)
        return vjp(d)
    return _rowwise(fn, [_full(pgl), _full(pr), _full(dh)], [g], [(D_MODEL, BF16), (D_MODEL, BF16)], [D_MODEL],
                    name=name)


def loss_fwd_bwd(y, target, name):
    def fn(a, b):
        err = a - b
        return err * (1.0 / D_MODEL), _sum0(err * err) * (0.5 / D_MODEL)
    return _rowwise(fn, [_full(y), _full(target)], [], [(D_MODEL, F32)], [D_MODEL], name=name)


def _shift_down(x, d, row):
    if d == 0:
        return x
    return jnp.where(row >= d, pltpu.roll(x, d, 0), 0.0)


def _shift_up(x, d, row):
    if d == 0:
        return x
    s = x.shape[0]
    return jnp.where(row < s - d, pltpu.roll(x, s - d, 0), 0.0)


def fgate_fwd(proj, bf, bn, s, name):
    def body(a_ref, b_ref, o_ref):
        x = a_ref[...].astype(F32) + b_ref[...]
        c = jnp.minimum(x, 0.0) - jnp.log(1.0 + jnp.exp(-jnp.abs(x)))
        row = lax.broadcasted_iota(jnp.int32, c.shape, 0)
        sh = 1
        while sh < s:
            c = c + _shift_down(c, sh, row)
            sh *= 2
        o_ref[...] = c

    return pl.pallas_call(
        body, name=name, grid=(bn,),
        in_specs=[pl.BlockSpec((s, 128), lambda b: (b, C_AF // 128)), pl.BlockSpec((1, 128), lambda b: (0, 0))],
        out_specs=pl.BlockSpec((s, 128), lambda b: (b, 0)),
        out_shape=jax.ShapeDtypeStruct((bn * s, 128), F32),
        compiler_params=_cparams(("parallel",)),
    )(proj, bf)


def fgate_bwd(proj, bf, dc, bn, s, name):
    def body(a_ref, b_ref, dc_ref, da_ref, db_ref):
        x = a_ref[...].astype(F32) + b_ref[...]
        d = dc_ref[...]
        row = lax.broadcasted_iota(jnp.int32, d.shape, 0)
        sh = 1
        while sh < s:
            d = d + _shift_up(d, sh, row)
            sh *= 2
        da = d * jax.nn.sigmoid(-x)
        da_ref[...] = da.astype(da_ref.dtype)
        first = pl.program_id(0) == 0

        @pl.when(first)
        def _():
            db_ref[...] = _sum0(da)

        @pl.when(jnp.logical_not(first))
        def _():
            db_ref[...] += _sum0(da)

    return pl.pallas_call(
        body, name=name, grid=(bn,),
        in_specs=[pl.BlockSpec((s, 128), lambda b: (b, C_AF // 128)), pl.BlockSpec((1, 128), lambda b: (0, 0)),
                  pl.BlockSpec((s, 128), lambda b: (b, 0))],
        out_specs=[pl.BlockSpec((s, 128), lambda b: (b, 0)), pl.BlockSpec((1, 128), lambda b: (0, 0))],
        out_shape=[jax.ShapeDtypeStruct((bn * s, 128), BF16), jax.ShapeDtypeStruct((1, 128), F32)],
        compiler_params=_cparams(("arbitrary",)),
    )(proj, bf, dc)


FOX_T = 256


def _fox_scores(q, k, cq, ck, j, i):
    t = FOX_T
    s = lax.dot_general(q, k, (((1,), (1,)), ((), ())), preferred_element_type=F32) * SCALE
    qpos = j * t + lax.broadcasted_iota(jnp.int32, (t, t), 0)
    kpos = i * t + lax.broadcasted_iota(jnp.int32, (t, t), 1)
    return jnp.where(qpos >= kpos, s + (cq - ck), NEG)


def fox_fwd(q, k, v, c_col, c_row, name):
    bn, h, s, d = q.shape
    t = FOX_T
    nq = s // t

    def body(q_ref, k_ref, v_ref, cq_ref, ck_ref, o_ref, lse_ref):
        j = pl.program_id(2)
        qv = q_ref[...]
        cq = cq_ref[...]

        def step(i, carry):
            m, l, acc = carry
            ks = pl.multiple_of(i * t, t)
            kc = k_ref[pl.ds(ks, t), :]
            vc = v_ref[pl.ds(ks, t), :]
            sc = _fox_scores(qv, kc, cq, ck_ref[i], j, i)
            m_new = jnp.maximum(m, jnp.max(sc, axis=-1, keepdims=True))
            alpha = jnp.exp(m - m_new)
            p = jnp.exp(sc - m_new)
            l = alpha * l + jnp.sum(p, axis=-1, keepdims=True)
            acc = alpha * acc + jnp.dot(p.astype(BF16), vc, preferred_element_type=F32)
            return m_new, l, acc

        init = (jnp.full((t, 1), NEG, F32), jnp.zeros((t, 1), F32), jnp.zeros((t, d), F32))
        m, l, acc = lax.fori_loop(0, j + 1, step, init)
        o_ref[...] = (acc / l).astype(o_ref.dtype)
        lse_ref[...] = m + jnp.log(l)

    blk_q = pl.BlockSpec((None, None, t, d), lambda b, hh, j: (b, hh, j, 0))
    blk_kv = pl.BlockSpec((None, None, s, d), lambda b, hh, j: (b, hh, 0, 0))
    blk_c1 = pl.BlockSpec((None, None, t, 1), lambda b, hh, j: (b, hh, j, 0))
    blk_cr = pl.BlockSpec((None, None, nq, 1, t), lambda b, hh, j: (b, hh, 0, 0, 0))
    return pl.pallas_call(
        body, name=name, grid=(bn, h, nq),
        in_specs=[blk_q, blk_kv, blk_kv, blk_c1, blk_cr],
        out_specs=[blk_q, blk_c1],
        out_shape=[jax.ShapeDtypeStruct((bn, h, s, d), F32), jax.ShapeDtypeStruct((bn, h, s, 1), F32)],
        compiler_params=_cparams(("parallel", "parallel", "arbitrary")),
    )(q, k, v, c_col, c_row)


def fox_bwd(q, k, v, c_col, c_row, o, lse, do, name):
    bn, h, s, d = q.shape
    t = FOX_T
    nq = s // t

    def body(q_ref, k_ref, v_ref, cq_ref, ck_ref, o_ref, lse_ref, do_ref, dq_ref, dk_ref, dv_ref, dck_ref,
             dcq_ref):
        j = pl.program_id(2)

        @pl.when(j == 0)
        def _():
            dk_ref[...] = jnp.zeros_like(dk_ref)
            dv_ref[...] = jnp.zeros_like(dv_ref)
            dck_ref[...] = jnp.zeros_like(dck_ref)

        qv = q_ref[...]
        cq = cq_ref[...]
        dov = do_ref[...]
        lse = lse_ref[...]
        delta = jnp.sum(dov.astype(F32) * o_ref[...].astype(F32), axis=-1, keepdims=True)

        def step(i, carry):
            dq, dcq = carry
            ks = pl.multiple_of(i * t, t)
            kc = k_ref[pl.ds(ks, t), :]
            vc = v_ref[pl.ds(ks, t), :]
            sc = _fox_scores(qv, kc, cq, ck_ref[i], j, i)
            p = jnp.exp(sc - lse)
            dp = lax.dot_general(dov, vc, (((1,), (1,)), ((), ())), preferred_element_type=F32)
            ds = p * (dp - delta)
            dsb = ds.astype(BF16)
            dq = dq + jnp.dot(dsb, kc, preferred_element_type=F32) * SCALE
            dk_ref[pl.ds(ks, t), :] += lax.dot_general(dsb, qv, (((0,), (0,)), ((), ())),
                                                       preferred_element_type=F32) * SCALE
            dv_ref[pl.ds(ks, t), :] += lax.dot_general(p.astype(BF16), dov, (((0,), (0,)), ((), ())),
                                                       preferred_element_type=F32)
            dck_ref[i] += -_sum0(ds)
            return dq, dcq + jnp.sum(ds, axis=-1, keepdims=True)

        dq, dcq = lax.fori_loop(0, j + 1, step, (jnp.zeros((t, d), F32), jnp.zeros((t, 1), F32)))
        dq_ref[...] = dq
        dcq_ref[...] = dcq

    blk_q = pl.BlockSpec((None, None, t, d), lambda b, hh, j: (b, hh, j, 0))
    blk_kv = pl.BlockSpec((None, None, s, d), lambda b, hh, j: (b, hh, 0, 0))
    blk_c1 = pl.BlockSpec((None, None, t, 1), lambda b, hh, j: (b, hh, j, 0))
    blk_cr = pl.BlockSpec((None, None, nq, 1, t), lambda b, hh, j: (b, hh, 0, 0, 0))
    return pl.pallas_call(
        body, name=name, grid=(bn, h, nq),
        in_specs=[blk_q, blk_kv, blk_kv, blk_c1, blk_cr, blk_q, blk_c1, blk_q],
        out_specs=[blk_q, blk_kv, blk_kv, blk_cr, blk_c1],
        out_shape=[jax.ShapeDtypeStruct((bn, h, s, d), F32), jax.ShapeDtypeStruct((bn, h, s, d), F32),
                   jax.ShapeDtypeStruct((bn, h, s, d), F32), jax.ShapeDtypeStruct((bn, h, nq, 1, t), F32),
                   jax.ShapeDtypeStruct((bn, h, s, 1), F32)],
        compiler_params=_cparams(("parallel", "parallel", "arbitrary")),
    )(q, k, v, c_col, c_row, o, lse, do)


def _swa_valid(n):
    qi = lax.broadcasted_iota(jnp.int32, (QB, 2 * QB), 0)
    kj = lax.broadcasted_iota(jnp.int32, (QB, 2 * QB), 1)
    dist = qi + QB - kj
    return (dist >= 0) & (dist < WINDOW) & ((kj >= QB) | (n > 0))


def _swa_band(ref, n):
    qs = pl.multiple_of(n * QB, QB)
    ps = pl.multiple_of(jnp.maximum(n - 1, 0) * QB, QB)
    return jnp.concatenate([ref[pl.ds(ps, QB), :], ref[pl.ds(qs, QB), :]], axis=0), qs, ps


def swa_fwd(q, k, v, bias, sinks, name):
    bn, hq, s, d = q.shape
    nb = s // QB

    def body(q_ref, k_ref, v_ref, b_ref, s_ref, o_ref, lse_ref):
        def step(n, _):
            kb, qs, _ps = _swa_band(k_ref, n)
            vb, _, _ = _swa_band(v_ref, n)
            valid = _swa_valid(n)
            for g in range(D_GROUP):
                qg = q_ref[g, pl.ds(qs, QB), :]
                sc = lax.dot_general(qg, kb, (((1,), (1,)), ((), ())), preferred_element_type=F32) * SCALE
                sc = jnp.where(valid, sc + b_ref[g], NEG)
                sink = s_ref[g]
                m = jnp.maximum(jnp.max(sc, axis=-1, keepdims=True), sink)
                e = jnp.exp(sc - m)
                z = jnp.sum(e, axis=-1, keepdims=True) + jnp.exp(sink - m)
                p = e / z
                o_ref[g, pl.ds(qs, QB), :] = jnp.dot(p.astype(BF16), vb, preferred_element_type=F32
                                                     ).astype(o_ref.dtype)
                lse_ref[g, pl.ds(qs, QB), :] = m + jnp.log(z)
            return 0

        lax.fori_loop(0, nb, step, 0)

    blk_q = pl.BlockSpec((None, D_GROUP, s, d), lambda b, kh: (b, kh, 0, 0))
    blk_kv = pl.BlockSpec((None, None, s, d), lambda b, kh: (b, kh, 0, 0))
    blk_l = pl.BlockSpec((None, D_GROUP, s, 1), lambda b, kh: (b, kh, 0, 0))
    return pl.pallas_call(
        body, name=name, grid=(bn, D_KV_HEADS),
        in_specs=[blk_q, blk_kv, blk_kv, pl.BlockSpec((D_GROUP, QB, 2 * QB), lambda b, kh: (kh, 0, 0)),
                  pl.BlockSpec((D_GROUP, QB, 1), lambda b, kh: (kh, 0, 0))],
        out_specs=[blk_q, blk_l],
        out_shape=[jax.ShapeDtypeStruct((bn, hq, s, d), BF16), jax.ShapeDtypeStruct((bn, hq, s, 1), F32)],
        compiler_params=_cparams(("parallel", "parallel")),
    )(q, k, v, bias, sinks)


def swa_bwd(q, k, v, bias, sinks, o, lse, do, name):
    bn, hq, s, d = q.shape
    nb = s // QB

    def body(q_ref, k_ref, v_ref, b_ref, s_ref, o_ref, lse_ref, do_ref, dq_ref, dk_ref, dv_ref, db_ref, dsk_ref):
        @pl.when(pl.program_id(1) == 0)
        def _():
            db_ref[...] = jnp.zeros_like(db_ref)
            dsk_ref[...] = jnp.zeros_like(dsk_ref)

        dk_ref[...] = jnp.zeros_like(dk_ref)
        dv_ref[...] = jnp.zeros_like(dv_ref)

        def step(n, _):
            kb, qs, ps = _swa_band(k_ref, n)
            vb, _, _ = _swa_band(v_ref, n)
            valid = _swa_valid(n)
            dkb = jnp.zeros((2 * QB, d), F32)
            dvb = jnp.zeros((2 * QB, d), F32)
            for g in range(D_GROUP):
                qg = q_ref[g, pl.ds(qs, QB), :]
                dog = do_ref[g, pl.ds(qs, QB), :]
                og = o_ref[g, pl.ds(qs, QB), :]
                lse = lse_ref[g, pl.ds(qs, QB), :]
                sc = lax.dot_general(qg, kb, (((1,), (1,)), ((), ())), preferred_element_type=F32) * SCALE
                sc = jnp.where(valid, sc + b_ref[g], NEG)
                p = jnp.exp(sc - lse)
                delta = jnp.sum(dog.astype(F32) * og.astype(F32), axis=-1, keepdims=True)
                dp = lax.dot_general(dog, vb, (((1,), (1,)), ((), ())), preferred_element_type=F32)
                ds = p * (dp - delta)
                dsb = ds.astype(BF16)
                dq_ref[g, pl.ds(qs, QB), :] = jnp.dot(dsb, kb, preferred_element_type=F32) * SCALE
                dkb = dkb + lax.dot_general(dsb, qg, (((0,), (0,)), ((), ())), preferred_element_type=F32) * SCALE
                dvb = dvb + lax.dot_general(p.astype(BF16), dog, (((0,), (0,)), ((), ())),
                                            preferred_element_type=F32)
                db_ref[g] += ds
                dsk_ref[g] += -jnp.exp(s_ref[g] - lse) * delta
            dk_ref[pl.ds(ps, QB), :] += dkb[:QB]
            dk_ref[pl.ds(qs, QB), :] += dkb[QB:]
            dv_ref[pl.ds(ps, QB), :] += dvb[:QB]
            dv_ref[pl.ds(qs, QB), :] += dvb[QB:]
            return 0

        lax.fori_loop(0, nb, step, 0)

    blk_q = pl.BlockSpec((None, D_GROUP, s, d), lambda kh, b: (b, kh, 0, 0))
    blk_kv = pl.BlockSpec((None, None, s, d), lambda kh, b: (b, kh, 0, 0))
    blk_l = pl.BlockSpec((None, D_GROUP, s, 1), lambda kh, b: (b, kh, 0, 0))
    blk_b = pl.BlockSpec((D_GROUP, QB, 2 * QB), lambda kh, b: (kh, 0, 0))
    blk_s = pl.BlockSpec((D_GROUP, QB, 1), lambda kh, b: (kh, 0, 0))
    return pl.pallas_call(
        body, name=name, grid=(D_KV_HEADS, bn),
        in_specs=[blk_q, blk_kv, blk_kv, blk_b, blk_s, blk_q, blk_l, blk_q],
        out_specs=[blk_q, blk_kv, blk_kv, blk_b, blk_s],
        out_shape=[jax.ShapeDtypeStruct((bn, hq, s, d), F32), jax.ShapeDtypeStruct((bn, D_KV_HEADS, s, d), F32),
                   jax.ShapeDtypeStruct((bn, D_KV_HEADS, s, d), F32),
                   jax.ShapeDtypeStruct((hq, QB, 2 * QB), F32), jax.ShapeDtypeStruct((hq, QB, 1), F32)],
        compiler_params=_cparams(("parallel", "arbitrary")),
    )(q, k, v, bias, sinks, o, lse, do)


def _bucket_table():
    dist = jnp.maximum(jnp.arange(QB)[:, None] + QB - jnp.arange(2 * QB)[None, :], 0)
    max_exact = REL_BUCKETS // 2
    large = max_exact + (jnp.log(jnp.maximum(dist, 1).astype(F32) / max_exact)
                         / math.log(REL_MAX_DIST / max_exact) * (REL_BUCKETS - max_exact)).astype(jnp.int32)
    large = jnp.minimum(large, REL_BUCKETS - 1)
    return jnp.where(dist < max_exact, dist, large).astype(F32)


def band_bias_fwd(bucket, rel_bias, name):
    def body(bk_ref, rel_ref, o_ref):
        bk = bk_ref[...]
        for hh in range(D_Q_HEADS):
            acc = jnp.zeros(bk.shape, F32)
            for b in range(REL_BUCKETS):
                acc = jnp.where(bk == float(b), rel_ref[b, hh], acc)
            o_ref[hh] = acc

    return pl.pallas_call(
        body, name=name,
        in_specs=[pl.BlockSpec(memory_space=pltpu.VMEM), pl.BlockSpec(memory_space=pltpu.SMEM)],
        out_specs=pl.BlockSpec(memory_space=pltpu.VMEM),
        out_shape=jax.ShapeDtypeStruct((D_Q_HEADS, QB, 2 * QB), F32),
    )(bucket, rel_bias)


def band_bias_bwd(bucket, dbias_layers, name):
    nl = len(dbias_layers)

    def body(bk_ref, *refs):
        o_ref = refs[nl]
        bk = bk_ref[...]
        for hh in range(D_Q_HEADS):
            tot = refs[0][hh]
            for r in refs[1:nl]:
                tot = tot + r[hh]
            for b in range(REL_BUCKETS):
                part = jnp.sum(jnp.where(bk == float(b), tot, 0.0), axis=0, keepdims=True)
                val = jnp.sum(part, axis=1, keepdims=True)
                o_ref[hh, b:b + 1, :] = jnp.broadcast_to(val, (1, 128))

    return pl.pallas_call(
        body, name=name,
        in_specs=[pl.BlockSpec(memory_space=pltpu.VMEM)] * (nl + 1),
        out_specs=pl.BlockSpec(memory_space=pltpu.VMEM),
        out_shape=jax.ShapeDtypeStruct((D_Q_HEADS, REL_BUCKETS, 128), F32),
    )(bucket, *dbias_layers)


def _proj_blk(s, col):
    return pl.BlockSpec((s, 256), functools.partial(lambda b, cb: (b, cb), cb=col // 256))


def convb_fwd(proj, w, bn, s, name):
    kk = w.shape[0]

    def body(bg_ref, cg_ref, xb_ref, w_ref, o_ref):
        x = cg_ref[...].astype(F32) * xb_ref[...].astype(F32)
        row = lax.broadcasted_iota(jnp.int32, x.shape, 0)
        y = jnp.zeros_like(x)
        for k in range(kk):
            y = y + w_ref[k:k + 1, :] * _shift_down(x, kk - 1 - k, row)
        o_ref[...] = (bg_ref[...].astype(F32) * y).astype(o_ref.dtype)

    return pl.pallas_call(
        body, name=name, grid=(bn,),
        in_specs=[_proj_blk(s, C_BG), _proj_blk(s, C_CG), _proj_blk(s, C_XB), pl.BlockSpec(w.shape, lambda b: (0, 0))],
        out_specs=pl.BlockSpec((s, 256), lambda b: (b, 0)),
        out_shape=jax.ShapeDtypeStruct((bn * s, 256), BF16),
        compiler_params=_cparams(("parallel",)),
    )(proj, proj, proj, w)


def convb_bwd(proj, w, dyb, bn, s, name):
    kk = w.shape[0]

    def body(bg_ref, cg_ref, xb_ref, w_ref, d_ref, dbg_ref, dcg_ref, dxb_ref, dw_ref):
        @pl.when(pl.program_id(0) == 0)
        def _():
            dw_ref[...] = jnp.zeros_like(dw_ref)

        cg = cg_ref[...].astype(F32)
        xb = xb_ref[...].astype(F32)
        d = d_ref[...].astype(F32)
        x = cg * xb
        row = lax.broadcasted_iota(jnp.int32, x.shape, 0)
        dy = d * bg_ref[...].astype(F32)
        y = jnp.zeros_like(x)
        dx = jnp.zeros_like(x)
        for k in range(kk):
            xs = _shift_down(x, kk - 1 - k, row)
            y = y + w_ref[k:k + 1, :] * xs
            dx = dx + w_ref[k:k + 1, :] * _shift_up(dy, kk - 1 - k, row)
            dw_ref[k:k + 1, :] += _sum0(dy * xs)
        dbg_ref[...] = (d * y).astype(dbg_ref.dtype)
        dcg_ref[...] = (dx * xb).astype(dcg_ref.dtype)
        dxb_ref[...] = (dx * cg).astype(dxb_ref.dtype)

    blk = pl.BlockSpec((s, 256), lambda b: (b, 0))
    return pl.pallas_call(
        body, name=name, grid=(bn,),
        in_specs=[_proj_blk(s, C_BG), _proj_blk(s, C_CG), _proj_blk(s, C_XB), pl.BlockSpec(w.shape, lambda b: (0, 0)),
                  blk],
        out_specs=[blk, blk, blk, pl.BlockSpec((8, 256), lambda b: (0, 0))],
        out_shape=[jax.ShapeDtypeStruct((bn * s, 256), BF16)] * 3 + [jax.ShapeDtypeStruct((8, 256), F32)],
        compiler_params=_cparams(("arbitrary",)),
    )(proj, proj, proj, w, dyb)


def _convc_core(ca, cb, w_ref, bias, kk, row):
    sg = jax.nn.sigmoid(cb)
    glu = ca * sg
    y = jnp.zeros_like(glu)
    for k in range(kk):
        y = y + w_ref[k:k + 1, :] * _shift_down(glu, kk - 1 - k, row)
    y = y + bias
    mu = jnp.mean(y, axis=-1, keepdims=True)
    xc = y - mu
    r = lax.rsqrt(jnp.mean(xc * xc, axis=-1, keepdims=True) + EPS)
    return sg, glu, xc * r, r


def convc_fwd(proj, w, bias, gain, lbias, bn, s, name):
    kk = w.shape[0]

    def body(ca_ref, cb_ref, w_ref, b_ref, g_ref, lb_ref, o_ref):
        ca = ca_ref[...].astype(F32)
        row = lax.broadcasted_iota(jnp.int32, ca.shape, 0)
        _, _, xh, _ = _convc_core(ca, cb_ref[...].astype(F32), w_ref, b_ref[...], kk, row)
        ln = xh * g_ref[...] + lb_ref[...]
        o_ref[...] = (ln * jax.nn.sigmoid(ln)).astype(o_ref.dtype)

    vec = pl.BlockSpec((1, 256), lambda b: (0, 0))
    return pl.pallas_call(
        body, name=name, grid=(bn,),
        in_specs=[_proj_blk(s, C_CA), _proj_blk(s, C_CB), pl.BlockSpec(w.shape, lambda b: (0, 0)), vec, vec, vec],
        out_specs=pl.BlockSpec((s, 256), lambda b: (b, 0)),
        out_shape=jax.ShapeDtypeStruct((bn * s, 256), BF16),
        compiler_params=_cparams(("parallel",)),
    )(proj, proj, w, bias, gain, lbias)


def convc_bwd(proj, w, bias, gain, lbias, dyc, bn, s, name):
    kk = w.shape[0]

    def body(ca_ref, cb_ref, w_ref, b_ref, g_ref, lb_ref, d_ref, dca_ref, dcb_ref, dw_ref, db_ref, dg_ref, dlb_ref):
        @pl.when(pl.program_id(0) == 0)
        def _():
            dw_ref[...] = jnp.zeros_like(dw_ref)
            db_ref[...] = jnp.zeros_like(db_ref)
            dg_ref[...] = jnp.zeros_like(dg_ref)
            dlb_ref[...] = jnp.zeros_like(dlb_ref)

        ca = ca_ref[...].astype(F32)
        row = lax.broadcasted_iota(jnp.int32, ca.shape, 0)
        sg, glu, xh, r = _convc_core(ca, cb_ref[...].astype(F32), w_ref, b_ref[...], kk, row)
        ln = xh * g_ref[...] + lb_ref[...]
        sl = jax.nn.sigmoid(ln)
        dl = d_ref[...].astype(F32) * (sl + ln * sl * (1.0 - sl))
        dg_ref[...] += _sum0(dl * xh)
        dlb_ref[...] += _sum0(dl)
        dxh = dl * g_ref[...]
        dy = r * (dxh - jnp.mean(dxh, axis=-1, keepdims=True) - xh * jnp.mean(dxh * xh, axis=-1, keepdims=True))
        db_ref[...] += _sum0(dy)
        dglu = jnp.zeros_like(glu)
        for k in range(kk):
            dw_ref[k:k + 1, :] += _sum0(dy * _shift_down(glu, kk - 1 - k, row))
            dglu = dglu + w_ref[k:k + 1, :] * _shift_up(dy, kk - 1 - k, row)
        dca_ref[...] = (dglu * sg).astype(dca_ref.dtype)
        dcb_ref[...] = (dglu * ca * sg * (1.0 - sg)).astype(dcb_ref.dtype)

    vec = pl.BlockSpec((1, 256), lambda b: (0, 0))
    blk = pl.BlockSpec((s, 256), lambda b: (b, 0))
    return pl.pallas_call(
        body, name=name, grid=(bn,),
        in_specs=[_proj_blk(s, C_CA), _proj_blk(s, C_CB), pl.BlockSpec(w.shape, lambda b: (0, 0)), vec, vec, vec, blk],
        out_specs=[blk, blk, pl.BlockSpec((32, 256), lambda b: (0, 0)), vec, vec, vec],
        out_shape=[jax.ShapeDtypeStruct((bn * s, 256), BF16)] * 2 + [jax.ShapeDtypeStruct((32, 256), F32)]
        + [jax.ShapeDtypeStruct((1, 256), F32)] * 3,
        compiler_params=_cparams(("arbitrary",)),
    )(proj, proj, w, bias, gain, lbias, dyc)


def adamw(w, g, m, v, name):
    shape = w.shape
    cols = shape[-1]
    rows = w.size // cols
    tr = _pick(rows, (256, 128, 64, 32, 16, 8))

    def body(w_ref, g_ref, m_ref, v_ref, d_ref, nm_ref, nv_ref):
        gg = g_ref[...]
        mm = ADAM_B1 * m_ref[...] + (1.0 - ADAM_B1) * gg
        vv = ADAM_B2 * v_ref[...] + (1.0 - ADAM_B2) * jnp.square(gg)
        m_hat = mm / (1.0 - ADAM_B1 ** ADAM_STEP)
        v_hat = vv / (1.0 - ADAM_B2 ** ADAM_STEP)
        d_ref[...] = -ADAM_LR * (m_hat / (jnp.sqrt(v_hat) + ADAM_EPS) + ADAM_WD * w_ref[...])
        nm_ref[...] = mm
        nv_ref[...] = vv

    blk = pl.BlockSpec((tr, cols), lambda i: (i, 0))
    outs = pl.pallas_call(
        body, name=name, grid=(rows // tr,), in_specs=[blk] * 4, out_specs=[blk] * 3,
        out_shape=[jax.ShapeDtypeStruct((rows, cols), F32)] * 3,
        compiler_params=_cparams(("parallel",)),
    )(*[a.reshape(rows, cols) for a in (w, g, m, v)])
    return [o.reshape(shape) for o in outs]


def add_halves(own, recv, name):
    n, r, c = own.shape
    tr = _pick(r, (512, 256, 128, 64, 32, 16, 8))
    blk = pl.BlockSpec((None, tr, c), lambda i, j: (i, j, 0))

    def body(a_ref, b_ref, o_ref):
        o_ref[...] = a_ref[...] + b_ref[...]

    return pl.pallas_call(
        body, name=name, grid=(n, r // tr), in_specs=[blk, blk], out_specs=blk,
        out_shape=jax.ShapeDtypeStruct((n, r, c), F32), compiler_params=_cparams(("parallel", "parallel")),
    )(own, recv)


def sum_slots(slots, name):
    n, r, c = slots.shape
    tr = _pick(r, (512, 256, 128, 64, 32, 16, 8))

    def body(a_ref, o_ref):
        acc = a_ref[0]
        for k in range(1, n):
            acc = acc + a_ref[k]
        o_ref[...] = acc

    return pl.pallas_call(
        body, name=name, grid=(r // tr,), in_specs=[pl.BlockSpec((n, tr, c), lambda j: (0, j, 0))],
        out_specs=pl.BlockSpec((tr, c), lambda j: (j, 0)),
        out_shape=jax.ShapeDtypeStruct((r, c), F32), compiler_params=_cparams(("parallel",)),
    )(slots)


ANY = pl.BlockSpec(memory_space=pl.ANY)


def _place():
    x, y, c = lax.axis_index("x"), lax.axis_index("y"), lax.axis_index("c")
    return x, y, c


def gather_shards(pack, name):
    r, cols = pack.shape
    half = r // 2

    def body(src_ref, out_ref, send_sems, recv_sems, local_sem):
        x, y, c = _place()
        sibling = (x, y, 1 - c)
        chips = [(1 - x, y), (x, 1 - y), (1 - x, 1 - y)]

        def rows(px, py, pc):
            return out_ref.at[2 * px + py, pl.ds(pc * half, half), :]

        mine = pltpu.make_async_copy(src_ref, out_ref.at[2 * x + y], local_sem)
        mine.start()

        def copy(k, blk, to, src=None):
            return pltpu.make_async_remote_copy(
                src_ref=rows(*blk) if src is None else src, dst_ref=rows(*blk),
                send_sem=send_sems.at[k], recv_sem=recv_sems.at[k], device_id=to, device_id_type=MESH)

        first = [copy(j, (x, y, c), (*chip, c), src=src_ref.at[pl.ds(c * half, half), :])
                 for j, chip in enumerate(chips)]
        for cp in first:
            cp.start()
        passed = [copy(3 + j, (*chip, c), sibling) for j, chip in enumerate(chips)]
        for j, chip in enumerate(chips):
            copy(j, (*chip, c), (x, y, c)).wait_recv()
            passed[j].start()
        for j, chip in enumerate(chips):
            copy(3 + j, (*chip, 1 - c), (x, y, c)).wait_recv()
        for cp in first + passed:
            cp.wait_send()
        mine.wait()

    return pl.pallas_call(
        body, name=name, in_specs=[ANY], out_specs=ANY,
        out_shape=jax.ShapeDtypeStruct((N_CHIPS, r, cols), pack.dtype),
        scratch_shapes=[pltpu.SemaphoreType.DMA((6,)), pltpu.SemaphoreType.DMA((6,)), pltpu.SemaphoreType.DMA],
    )(pack)


def exchange_sibling_halves(g, name):
    n, r, cols = g.shape
    half = r // 2

    def body(g_ref, own_ref, recv_ref, send_sems, recv_sems, local_sem):
        x, y, c = _place()
        sibling = (x, y, 1 - c)
        mine = pltpu.make_async_copy(g_ref.at[:, pl.ds(c * half, half), :], own_ref, local_sem)
        mine.start()
        cp = pltpu.make_async_remote_copy(
            src_ref=g_ref.at[:, pl.ds((1 - c) * half, half), :], dst_ref=recv_ref,
            send_sem=send_sems.at[0], recv_sem=recv_sems.at[0], device_id=sibling, device_id_type=MESH)
        cp.start()
        cp.wait()
        mine.wait()

    return pl.pallas_call(
        body, name=name, in_specs=[ANY], out_specs=[ANY, ANY],
        out_shape=[jax.ShapeDtypeStruct((n, half, cols), g.dtype)] * 2,
        scratch_shapes=[pltpu.SemaphoreType.DMA((1,)), pltpu.SemaphoreType.DMA((1,)), pltpu.SemaphoreType.DMA],
    )(g)


def scatter_to_chips(part, name):
    n, h, cols = part.shape

    def body(p_ref, slot_ref, send_sems, recv_sems, local_sem):
        x, y, c = _place()
        me = 2 * x + y
        chips = [(1 - x, y), (x, 1 - y), (1 - x, 1 - y)]
        mine = pltpu.make_async_copy(p_ref.at[me], slot_ref.at[me], local_sem)
        mine.start()
        cps = [pltpu.make_async_remote_copy(
            src_ref=p_ref.at[2 * px + py], dst_ref=slot_ref.at[me],
            send_sem=send_sems.at[j], recv_sem=recv_sems.at[j], device_id=(px, py, c), device_id_type=MESH)
            for j, (px, py) in enumerate(chips)]
        for cp in cps:
            cp.start()
        for j, (px, py) in enumerate(chips):
            pltpu.make_async_remote_copy(
                src_ref=p_ref.at[me], dst_ref=slot_ref.at[2 * px + py],
                send_sem=send_sems.at[j], recv_sem=recv_sems.at[j], device_id=(px, py, c),
                device_id_type=MESH).wait_recv()
        for cp in cps:
            cp.wait_send()
        mine.wait()

    return pl.pallas_call(
        body, name=name, in_specs=[ANY], out_specs=ANY,
        out_shape=jax.ShapeDtypeStruct((n, h, cols), part.dtype),
        scratch_shapes=[pltpu.SemaphoreType.DMA((3,)), pltpu.SemaphoreType.DMA((3,)), pltpu.SemaphoreType.DMA],
    )(part)


def join_sibling_halves(mine_half, name):
    h, cols = mine_half.shape

    def body(m_ref, out_ref, send_sems, recv_sems, local_sem):
        x, y, c = _place()
        sibling = (x, y, 1 - c)
        own = pltpu.make_async_copy(m_ref, out_ref.at[pl.ds(c * h, h), :], local_sem)
        own.start()
        cp = pltpu.make_async_remote_copy(
            src_ref=m_ref, dst_ref=out_ref.at[pl.ds(c * h, h), :],
            send_sem=send_sems.at[0], recv_sem=recv_sems.at[0], device_id=sibling, device_id_type=MESH)
        cp.start()
        pltpu.make_async_remote_copy(
            src_ref=m_ref, dst_ref=out_ref.at[pl.ds((1 - c) * h, h), :],
            send_sem=send_sems.at[0], recv_sem=recv_sems.at[0], device_id=sibling, device_id_type=MESH).wait_recv()
        cp.wait_send()
        own.wait()

    return pl.pallas_call(
        body, name=name, in_specs=[ANY], out_specs=ANY,
        out_shape=jax.ShapeDtypeStruct((2 * h, cols), mine_half.dtype),
        scratch_shapes=[pltpu.SemaphoreType.DMA((1,)), pltpu.SemaphoreType.DMA((1,)), pltpu.SemaphoreType.DMA],
    )(mine_half)


def gather_small(v, name):
    r, cols = v.shape

    def body(v_ref, out_ref, send_sems, recv_sems):
        x, y, c = _place()
        me = 4 * x + 2 * y + c
        out_ref[me] = v_ref[...]
        cps = []
        for rel in range(1, N_DEV):
            px = 1 - x if (rel >> 2) & 1 else x
            py = 1 - y if (rel >> 1) & 1 else y
            pc = 1 - c if rel & 1 else c
            cps.append(pltpu.make_async_remote_copy(
                src_ref=v_ref, dst_ref=out_ref.at[me], send_sem=send_sems.at[rel - 1],
                recv_sem=recv_sems.at[rel - 1], device_id=(px, py, pc), device_id_type=MESH))
        for cp in cps:
            cp.start()
        for cp in cps:
            cp.wait()

    return pl.pallas_call(
        body, name=name, in_specs=[pl.BlockSpec(memory_space=pltpu.VMEM)],
        out_specs=pl.BlockSpec(memory_space=pltpu.VMEM),
        out_shape=jax.ShapeDtypeStruct((N_DEV, r, cols), v.dtype),
        scratch_shapes=[pltpu.SemaphoreType.DMA((N_DEV - 1,)), pltpu.SemaphoreType.DMA((N_DEV - 1,))],
        compiler_params=pltpu.CompilerParams(vmem_limit_bytes=VMEM_LIMIT),
    )(v)


def _pad_rows(flat, row_align):
    n = flat.shape[-1]
    unit = PACK_COLS * row_align
    tot = -(-n // unit) * unit
    pad = [(0, 0)] * (flat.ndim - 1) + [(0, tot - n)]
    return jnp.pad(flat, pad)


def _pack_shards(ws):
    flat = jnp.concatenate([ws[n].astype(BF16).reshape(-1) for n in SHARDED])
    return _pad_rows(flat, PACK_ROW_ALIGN).reshape(-1, PACK_COLS)


def _unpack_full(gathered, shard_shapes):
    flat = gathered.reshape(N_CHIPS, -1)
    out, off = {}, 0
    for n in SHARDED:
        shp = shard_shapes[n]
        size = math.prod(shp)
        seg = flat[:, off:off + size].reshape((N_CHIPS,) + tuple(shp))
        off += size
        if n in ROW_SHARDED:
            out[n] = jnp.transpose(seg, (1, 0, 2, 3)).reshape(shp[0], N_CHIPS * shp[1], shp[2])
        else:
            out[n] = jnp.transpose(seg, (1, 2, 0, 3)).reshape(shp[0], shp[1], N_CHIPS * shp[2])
    return out


def _pack_grads(gfull, shard_shapes):
    segs = []
    for n in SHARDED:
        shp = shard_shapes[n]
        g = gfull[n]
        if n in ROW_SHARDED:
            seg = jnp.transpose(g.reshape(shp[0], N_CHIPS, shp[1], shp[2]), (1, 0, 2, 3))
        else:
            seg = jnp.transpose(g.reshape(shp[0], shp[1], N_CHIPS, shp[2]), (2, 0, 1, 3))
        segs.append(seg.reshape(N_CHIPS, -1))
    flat = jnp.concatenate(segs, axis=1)
    return _pad_rows(flat, PACK_ROW_ALIGN).reshape(N_CHIPS, -1, PACK_COLS)


def _unpack_shard_grads(red, shard_shapes):
    flat = red.reshape(-1)
    out, off = {}, 0
    for n in SHARDED:
        shp = shard_shapes[n]
        size = math.prod(shp)
        out[n] = flat[off:off + size].reshape(shp)
        off += size
    return out


def _pack_small(parts):
    flat = jnp.concatenate([p.astype(F32).reshape(-1) for p in parts])
    return _pad_rows(flat, 8).reshape(-1, PACK_COLS)


def _unpack_small(flat2d, shapes):
    flat = flat2d.reshape(-1)
    out, off = [], 0
    for shp in shapes:
        size = math.prod(shp)
        out.append(flat[off:off + size].reshape(shp))
        off += size
    return out


def _heads(a, bn, s, h):
    return jnp.transpose(a.reshape(bn, s, h, HEAD_DIM), (0, 2, 1, 3))


def _unheads(a):
    bn, h, s, d = a.shape
    return jnp.transpose(a, (0, 2, 1, 3)).reshape(bn * s, h * d)


def _reorder_w_in(w):
    d = w.shape[0]
    return jnp.concatenate([w[:, 2820:6916], w[:, 0:768], w[:, 772:1540], w[:, 1540:2052], w[:, 2052:2820],
                            w[:, 768:772], jnp.zeros((d, N_PROJ - 6916), w.dtype)], axis=1)


def _restore_dw_in(g):
    return jnp.concatenate([g[:, 4096:4864], g[:, 6912:6916], g[:, 4864:5632], g[:, 5632:6144], g[:, 6144:6912],
                            g[:, 0:4096]], axis=1)


def _ffn_fwd(h, g_pre, w_gu, w_down, g_post, tag):
    n = rms_fwd(h, g_pre, f"{tag}_rms")
    gu = _mm(n, w_gu, out_dtype=BF16, name=f"{tag}_mm_gu")
    a = swiglu_fwd(gu, f"{tag}_swiglu")
    f = _mm(a, w_down, out_dtype=F32, name=f"{tag}_mm_down")
    h_out = res_rms_fwd(h, f, g_post, 0.5, f"{tag}_res")
    return h_out, (h, n, gu, a, f)


def _ffn_bwd(dh, saved, g_pre, w_gu, w_down, g_post, tag):
    h, n, gu, a, f = saved
    df, dg_post = res_rms_bwd(f, g_post, dh, 0.5, f"{tag}_res_bwd")
    da = _mm(df, w_down, tb=True, out_dtype=BF16, name=f"{tag}_mm_da")
    dw_down = _mm(a, df, ta=True, name=f"{tag}_mm_dwdown")
    dgu = swiglu_bwd(gu, da, f"{tag}_swiglu_bwd")
    dw_gu = _mm(n, dgu, ta=True, name=f"{tag}_mm_dwgu")
    dn = _mm(dgu, w_gu, tb=True, out_dtype=BF16, name=f"{tag}_mm_dn")
    dh_in, dg_pre = rms_bwd(h, g_pre, dn, dh, f"{tag}_rms_bwd")
    return dh_in, dg_pre, dw_gu, dw_down, dg_post


def kernel(x, p, ffn1_norm_pre, ffn1_w_gu, ffn1_w_down, ffn1_norm_post, mix_norm_pre, w_in, b_forget, b_gate, conv_short, conv_dw, conv_dw_bias, conv_ln_gain, conv_ln_bias, attn_sinks, rel_bias, w_br_a, w_br_b, w_br_c, w_br_d, w_o, mix_norm_post, ffn2_norm_pre, ffn2_w_gu, ffn2_w_down, ffn2_norm_post, ple_norm_gate, w_ple_gate, w_ple, ple_norm_post, loss_target, m_ffn1_norm_pre, m_ffn1_w_gu, m_ffn1_w_down, m_ffn1_norm_post, m_mix_norm_pre, m_w_in, m_b_forget, m_b_gate, m_conv_short, m_conv_dw, m_conv_dw_bias, m_conv_ln_gain, m_conv_ln_bias, m_attn_sinks, m_rel_bias, m_w_br_a, m_w_br_b, m_w_br_c, m_w_br_d, m_w_o, m_mix_norm_post, m_ffn2_norm_pre, m_ffn2_w_gu, m_ffn2_w_down, m_ffn2_norm_post, m_ple_norm_gate, m_w_ple_gate, m_w_ple, m_ple_norm_post, v_ffn1_norm_pre, v_ffn1_w_gu, v_ffn1_w_down, v_ffn1_norm_post, v_mix_norm_pre, v_w_in, v_b_forget, v_b_gate, v_conv_short, v_conv_dw, v_conv_dw_bias, v_conv_ln_gain, v_conv_ln_bias, v_attn_sinks, v_rel_bias, v_w_br_a, v_w_br_b, v_w_br_c, v_w_br_d, v_w_o, v_mix_norm_post, v_ffn2_norm_pre, v_ffn2_w_gu, v_ffn2_w_down, v_ffn2_norm_post, v_ple_norm_gate, v_w_ple_gate, v_w_ple, v_ple_norm_post):
    args = dict(locals())
    ws = {n: args[n] for n in WEIGHTS}
    ms = {n: args["m_" + n] for n in WEIGHTS}
    vs = {n: args["v_" + n] for n in WEIGHTS}
    return _step(x, p, loss_target, ws, ms, vs)


def _local(x, p, loss_target, ws, wf, w_short, w_dw):
    bn, s, d = x.shape
    t = bn * s
    depth = w_short.shape[0]

    def vec(a, i):
        return a[i].reshape(1, -1)

    bucket = _bucket_table()
    band_bias = band_bias_fwd(bucket, ws['rel_bias'], "band_bias")

    h = x.reshape(t, d)
    saved = []
    for i in range(depth):
        sv = {}
        h, sv['ffn1'] = _ffn_fwd(h, vec(ws['ffn1_norm_pre'], i), wf['ffn1_w_gu'][i], wf['ffn1_w_down'][i],
                                 vec(ws['ffn1_norm_post'], i), f"l{i}_ffn1")
        h1 = h
        u = rms_fwd(h1, vec(ws['mix_norm_pre'], i), f"l{i}_mix_rms")
        w_in_r = _reorder_w_in(wf['w_in'][i])
        proj = _mm(u, w_in_r, out_dtype=BF16, name=f"l{i}_mm_proj")
        bf = jnp.pad(vec(ws['b_forget'], i), ((0, 0), (0, 128 - A_HEADS)))
        cc = fgate_fwd(proj, bf, bn, s, f"l{i}_fgate")
        c4 = jnp.transpose(cc.reshape(bn, s, 128)[:, :, :A_HEADS], (0, 2, 1))
        c_col = c4[..., None]
        c_row = c4.reshape(bn, A_HEADS, s // FOX_T, 1, FOX_T)
        qa = _heads(proj[:, C_AQ:C_AQ + 256], bn, s, A_HEADS)
        ka = _heads(proj[:, C_AK:C_AK + 256], bn, s, A_HEADS)
        va = _heads(proj[:, C_AV:C_AV + 256], bn, s, A_HEADS)
        oa, lse_a = fox_fwd(qa, ka, va, c_col, c_row, f"l{i}_fox")
        ya = _unheads(oa)
        w_sh = jnp.pad(w_short[i], ((0, 8 - w_short.shape[1]), (0, 0)))
        w_cv = jnp.pad(w_dw[i], ((0, 32 - w_dw.shape[1]), (0, 0)))
        yb = convb_fwd(proj, w_sh[:3], bn, s, f"l{i}_convb")
        cvec = (vec(ws['conv_dw_bias'], i), vec(ws['conv_ln_gain'], i), vec(ws['conv_ln_bias'], i))
        yc = convc_fwd(proj, w_cv[:31], *cvec, bn, s, f"l{i}_convc")
        qd = _heads(proj[:, C_DQ:C_DQ + 512], bn, s, D_Q_HEADS)
        kd = _heads(proj[:, C_DK:C_DK + 128], bn, s, D_KV_HEADS)
        vd = _heads(proj[:, C_DV:C_DV + 128], bn, s, D_KV_HEADS)
        sinks = jnp.broadcast_to(ws['attn_sinks'][i].reshape(D_Q_HEADS, 1, 1), (D_Q_HEADS, QB, 1))
        od, lse_d = swa_fwd(qd, kd, vd, band_bias, sinks, f"l{i}_swa")
        yd = _unheads(od)
        ys = (ya, yb, yc, yd)
        wbr = (wf['w_br_a'][i], wf['w_br_b'][i], wf['w_br_c'][i], wf['w_br_d'][i])
        zs = [_mm(yk, wk, out_dtype=BF16, name=f"l{i}_mm_br{k}") for k, (yk, wk) in enumerate(zip(ys, wbr))]
        bgs = [ws['b_gate'][i, k * d:(k + 1) * d].reshape(1, d) for k in range(4)]
        merged = merge_fwd(proj, zs, bgs, f"l{i}_merge")
        mo = _mm(merged, wf['w_o'][i], out_dtype=F32, name=f"l{i}_mm_o")
        h2 = res_rms_fwd(h1, mo, vec(ws['mix_norm_post'], i), 1.0, f"l{i}_mix_res")
        sv['mix'] = dict(h1=h1, u=u, proj=proj, w_in_r=w_in_r, bf=bf, c_col=c_col, c_row=c_row, qa=qa, ka=ka, va=va,
                         oa=oa, lse_a=lse_a, w_sh=w_sh, w_cv=w_cv, cvec=cvec, qd=qd, kd=kd, vd=vd, sinks=sinks,
                         od=od, lse_d=lse_d, ys=ys, wbr=wbr, zs=zs, bgs=bgs, merged=merged, mo=mo)
        h, sv['ffn2'] = _ffn_fwd(h2, vec(ws['ffn2_norm_pre'], i), wf['ffn2_w_gu'][i], wf['ffn2_w_down'][i],
                                 vec(ws['ffn2_norm_post'], i), f"l{i}_ffn2")
        h3 = h
        ng = rms_fwd(h3, vec(ws['ple_norm_gate'], i), f"l{i}_ple_rms")
        pgl = _mm(ng, wf['w_ple_gate'][i], out_dtype=BF16, name=f"l{i}_mm_pgl")
        p_i = p[i].reshape(t, -1)
        pr = _mm(p_i, wf['w_ple'][i], out_dtype=F32, name=f"l{i}_mm_pr")
        h = ple_fwd(h3, pgl, pr, vec(ws['ple_norm_post'], i), f"l{i}_ple")
        sv['ple'] = dict(h3=h3, ng=ng, pgl=pgl, p_i=p_i, pr=pr)
        saved.append(sv)

    dh, loss_vec = loss_fwd_bwd(h, loss_target.reshape(t, d), "loss")
    loss_part = jnp.sum(loss_vec)

    gfull = {n: [None] * depth for n in SHARDED}
    gsmall = {n: [None] * depth for n in REPLICATED + CONV_SHARDED if n != 'rel_bias'}
    dbias_layers = []
    for i in reversed(range(depth)):
        sv = saved[i]
        pv = sv['ple']
        dpgl, dpr, dg = ple_bwd(pv['pgl'], pv['pr'], vec(ws['ple_norm_post'], i), dh, f"l{i}_ple_bwd")
        gsmall['ple_norm_post'][i] = dg
        gfull['w_ple'][i] = _mm(pv['p_i'], dpr, ta=True, name=f"l{i}_mm_dwple")
        gfull['w_ple_gate'][i] = _mm(pv['ng'], dpgl, ta=True, name=f"l{i}_mm_dwpg")
        dng = _mm(dpgl, wf['w_ple_gate'][i], tb=True, out_dtype=BF16, name=f"l{i}_mm_dng")
        dh, gsmall['ple_norm_gate'][i] = rms_bwd(pv['h3'], vec(ws['ple_norm_gate'], i), dng, dh,
                                                  f"l{i}_ple_rms_bwd")
        dh, gsmall['ffn2_norm_pre'][i], gfull['ffn2_w_gu'][i], gfull['ffn2_w_down'][i], \
            gsmall['ffn2_norm_post'][i] = _ffn_bwd(
                dh, sv['ffn2'], vec(ws['ffn2_norm_pre'], i), wf['ffn2_w_gu'][i], wf['ffn2_w_down'][i],
                vec(ws['ffn2_norm_post'], i), f"l{i}_ffn2")
        mv = sv['mix']
        dmo, gsmall['mix_norm_post'][i] = res_rms_bwd(mv['mo'], vec(ws['mix_norm_post'], i), dh, 1.0,
                                                      f"l{i}_mix_res_bwd")
        gfull['w_o'][i] = _mm(mv['merged'], dmo, ta=True, name=f"l{i}_mm_dwo")
        dmerged = _mm(dmo, wf['w_o'][i], tb=True, out_dtype=BF16, name=f"l{i}_mm_dmerged")
        mb = merge_bwd(mv['proj'], mv['zs'], mv['bgs'], dmerged, f"l{i}_merge_bwd")
        dgates, dzs, dbg = mb[0:4], mb[4:8], mb[8:12]
        gsmall['b_gate'][i] = jnp.concatenate(dbg, axis=1)
        dys = []
        for k, nm in enumerate(('w_br_a', 'w_br_b', 'w_br_c', 'w_br_d')):
            gfull[nm][i] = _mm(mv['ys'][k], dzs[k], ta=True, name=f"l{i}_mm_dwbr{k}")
            dys.append(_mm(dzs[k], mv['wbr'][k], tb=True, out_dtype=BF16, name=f"l{i}_mm_dy{k}"))
        doa = _heads(dys[0], bn, s, A_HEADS)
        dqa, dka, dva, dck, dcq = fox_bwd(mv['qa'], mv['ka'], mv['va'], mv['c_col'], mv['c_row'], mv['oa'], mv['lse_a'],
                                     doa, f"l{i}_fox_bwd")
        dc = jnp.transpose(dck.reshape(bn, A_HEADS, s) + dcq.reshape(bn, A_HEADS, s), (0, 2, 1))
        dc = jnp.pad(dc, ((0, 0), (0, 0), (0, 128 - A_HEADS))).reshape(t, 128)
        daf, dbf = fgate_bwd(mv['proj'], mv['bf'], dc, bn, s, f"l{i}_fgate_bwd")
        gsmall['b_forget'][i] = dbf[:, :A_HEADS]
        dbg_, dcg_, dxb_, dwsh = convb_bwd(mv['proj'], mv['w_sh'][:3], dys[1], bn, s, f"l{i}_convb_bwd")
        gsmall['conv_short'][i] = dwsh[:3]
        dca, dcb, dwcv, dcbias, dlg, dlb = convc_bwd(mv['proj'], mv['w_cv'][:31], *mv['cvec'], dys[2], bn, s,
                                                     f"l{i}_convc_bwd")
        gsmall['conv_dw'][i] = dwcv[:31]
        gsmall['conv_dw_bias'][i] = dcbias
        gsmall['conv_ln_gain'][i] = dlg
        gsmall['conv_ln_bias'][i] = dlb
        dod = _heads(dys[3], bn, s, D_Q_HEADS)
        dqd, dkd, dvd, dbias, dsink = swa_bwd(mv['qd'], mv['kd'], mv['vd'], band_bias, mv['sinks'], mv['od'],
                                              mv['lse_d'], dod, f"l{i}_swa_bwd")
        dbias_layers.append(dbias)
        gsmall['attn_sinks'][i] = jnp.sum(dsink, axis=(1, 2))
        dproj = jnp.concatenate(
            list(dgates)
            + [_unheads(a).astype(BF16) for a in (dqa, dka, dva)]
            + [dbg_, dcg_, dxb_, dca, dcb]
            + [_unheads(a).astype(BF16) for a in (dqd, dkd, dvd)]
            + [daf, jnp.zeros((t, N_PROJ - C_AF - 128), BF16)], axis=1)
        gfull['w_in'][i] = _restore_dw_in(_mm(mv['u'], dproj, ta=True, name=f"l{i}_mm_dwin"))
        du = _mm(dproj, mv['w_in_r'], tb=True, out_dtype=BF16, name=f"l{i}_mm_du")
        dh, gsmall['mix_norm_pre'][i] = rms_bwd(mv['h1'], vec(ws['mix_norm_pre'], i), du, dh, f"l{i}_mix_rms_bwd")
        dh, gsmall['ffn1_norm_pre'][i], gfull['ffn1_w_gu'][i], gfull['ffn1_w_down'][i], \
            gsmall['ffn1_norm_post'][i] = _ffn_bwd(
                dh, sv['ffn1'], vec(ws['ffn1_norm_pre'], i), wf['ffn1_w_gu'][i], wf['ffn1_w_down'][i],
                vec(ws['ffn1_norm_post'], i), f"l{i}_ffn1")
    grad_x = dh.reshape(bn, s, d)

    drel = band_bias_bwd(bucket, dbias_layers, "band_bias_bwd")
    gfull = {n: jnp.stack(gfull[n]) for n in SHARDED}
    full_shapes = {n: ws[n].shape for n in REPLICATED}
    full_shapes['conv_short'] = w_short.shape
    full_shapes['conv_dw'] = w_dw.shape
    gs = {n: jnp.stack([a.reshape(full_shapes[n][1:]) for a in gsmall[n]]) for n in gsmall}
    gs['rel_bias'] = jnp.transpose(drel[:, :, 0])
    return loss_part, grad_x, gfull, gs


def _step(x, p, loss_target, ws, ms, vs):
    chip = 2 * lax.axis_index("x") + lax.axis_index("y")

    shard_shapes = {n: ws[n].shape for n in SHARDED}
    gathered = gather_shards(_pack_shards(ws), "gather_weights")
    wf = _unpack_full(gathered, shard_shapes)
    conv_shapes = [ws[n].shape for n in CONV_SHARDED]
    conv_all = gather_small(_pack_small([ws[n] for n in CONV_SHARDED]), "gather_conv")
    conv_full = []
    for idx, n in enumerate(CONV_SHARDED):
        per_chip = [_unpack_small(conv_all[2 * j], conv_shapes)[idx] for j in range(N_CHIPS)]
        conv_full.append(jnp.concatenate(per_chip, axis=-1))
    w_short, w_dw = conv_full

    loss_part, grad_x, gfull, gs = _local(x, p, loss_target, {n: ws[n] for n in REPLICATED}, wf, w_short, w_dw)

    gpack = _pack_grads(gfull, shard_shapes)
    own, recv = exchange_sibling_halves(gpack, "rs_sibling")
    chip_sum = add_halves(own, recv, "rs_add")
    slots = scatter_to_chips(chip_sum, "rs_chips")
    red_half = sum_slots(slots, "rs_sum")
    red = join_sibling_halves(red_half, "rs_join")
    g_shard = _unpack_shard_grads(red, shard_shapes)

    small_names = [n for n in REPLICATED + CONV_SHARDED]
    small_parts = [gs[n] for n in small_names]
    small_shapes = [g.shape for g in small_parts]
    small_parts.append(loss_part.reshape(1))
    small_shapes.append((1,))
    small_all = gather_small(_pack_small(small_parts), "gather_small")
    small_red = sum_slots(small_all, "small_sum")
    small_g = _unpack_small(small_red, small_shapes)
    loss = small_g[-1].reshape(())
    g_small = dict(zip(small_names, small_g[:-1]))

    grads = {}
    for n in WEIGHTS:
        if n in SHARDED:
            grads[n] = g_shard[n]
        elif n in CONV_SHARDED:
            wdt = ws[n].shape[-1]
            grads[n] = lax.dynamic_slice_in_dim(g_small[n], chip * wdt, wdt, axis=2)
        else:
            grads[n] = g_small[n]

    deltas, new_m, new_v = {}, {}, {}
    small_upd = [n for n in WEIGHTS if n not in SHARDED]
    for n in SHARDED:
        deltas[n], new_m[n], new_v[n] = adamw(ws[n], grads[n], ms[n], vs[n], f"adamw_{n}")
    shapes_u = [ws[n].shape for n in small_upd]
    packs = [_pack_small([src[n] for n in small_upd]) for src in (ws, grads, ms, vs)]
    upd = adamw(*packs, "adamw_small")
    for res, dst in zip(upd, (deltas, new_m, new_v)):
        for n, a in zip(small_upd, _unpack_small(res, shapes_u)):
            dst[n] = a

    return (loss, grad_x, *[grads[n] for n in WEIGHTS], *[deltas[n] for n in WEIGHTS],
            *[new_m[n] for n in WEIGHTS], *[new_v[n] for n in WEIGHTS])
```

```python
import functools
import math

import jax
import jax.numpy as jnp
from jax import lax
from jax.experimental import pallas as pl
from jax.experimental.pallas import tpu as pltpu

F32 = jnp.float32
BF16 = jnp.bfloat16
MESH = pl.DeviceIdType.MESH

D_MODEL = 1024
DEPTH = 4
HEAD_DIM = 64
A_HEADS = 4
D_Q_HEADS = 8
D_KV_HEADS = 2
D_GROUP = 4
WINDOW = 128
QB = 128
REL_BUCKETS = 32
REL_MAX_DIST = 128
D_FF = 2816
EPS = 1e-6
NEG = -1e30
SCALE = HEAD_DIM ** -0.5
N_CHIPS = 4
N_DEV = 8

ADAM_LR = 0.001
ADAM_B1 = 0.9
ADAM_B2 = 0.999
ADAM_EPS = 1e-08
ADAM_WD = 0.01
ADAM_STEP = 10

C_GATE = 0
C_AQ, C_AK, C_AV = 4096, 4352, 4608
C_BG, C_CG, C_XB = 4864, 5120, 5376
C_CA, C_CB = 5632, 5888
C_DQ, C_DK, C_DV = 6144, 6656, 6784
C_AF = 6912
N_PROJ = 7168

VMEM_LIMIT = 56 * 1024 * 1024
PACK_COLS = 1024
PACK_ROW_ALIGN = 1024

SHARDED = ('ffn1_w_gu', 'ffn1_w_down', 'w_in', 'w_br_a', 'w_br_b', 'w_br_c', 'w_br_d', 'w_o',
           'ffn2_w_gu', 'ffn2_w_down', 'w_ple_gate', 'w_ple')
ROW_SHARDED = ('ffn1_w_down', 'w_o', 'ffn2_w_down', 'w_ple_gate')
CONV_SHARDED = ('conv_short', 'conv_dw')
REPLICATED = ('ffn1_norm_pre', 'ffn1_norm_post', 'mix_norm_pre', 'b_forget', 'b_gate', 'conv_dw_bias',
              'conv_ln_gain', 'conv_ln_bias', 'attn_sinks', 'rel_bias', 'mix_norm_post', 'ffn2_norm_pre',
              'ffn2_norm_post', 'ple_norm_gate', 'ple_norm_post')
WEIGHTS = ('ffn1_norm_pre', 'ffn1_w_gu', 'ffn1_w_down', 'ffn1_norm_post', 'mix_norm_pre', 'w_in', 'b_forget',
           'b_gate', 'conv_short', 'conv_dw', 'conv_dw_bias', 'conv_ln_gain', 'conv_ln_bias', 'attn_sinks',
           'rel_bias', 'w_br_a', 'w_br_b', 'w_br_c', 'w_br_d', 'w_o', 'mix_norm_post', 'ffn2_norm_pre',
           'ffn2_w_gu', 'ffn2_w_down', 'ffn2_norm_post', 'ple_norm_gate', 'w_ple_gate', 'w_ple', 'ple_norm_post')


def _cparams(sem=None):
    return pltpu.CompilerParams(dimension_semantics=sem, vmem_limit_bytes=VMEM_LIMIT)


def _pick(dim, cands):
    for c in cands:
        if dim % c == 0:
            return c
    return dim


MM_VMEM_BUDGET = 40 * 1024 * 1024
MXU_FLOPS = 9.0e14
HBM_BYTES_PER_S = 3.0e12
GRID_STEP_S = 0.35e-6


def _divisors(dim, cands):
    out = [c for c in cands if c <= dim and dim % c == 0]
    return out or [dim]


def _mm_tiles(m, n, k, ab, bb, ob):
    best = None
    for tm in _divisors(m, (2048, 1408, 1024, 512, 256, 128)):
        for tn in _divisors(n, (2816, 2048, 1792, 1408, 1024, 512, 256, 128)):
            for tk in _divisors(k, (k if k <= 2048 else 2816, 2816, 2048, 1792, 1408, 1024, 512, 256, 128)):
                nk = k // tk
                vmem = 2 * (tm * tk * ab + tk * tn * bb + tm * tn * ob) + tm * tn * 4 * (2 if nk > 1 else 1)
                if vmem > MM_VMEM_BUDGET:
                    continue
                steps = (m // tm) * (n // tn) * nk
                a_bytes = m * k * ab * (1 if nk == 1 else n // tn)
                b_bytes = k * n * bb * (1 if (nk == 1 and n == tn) else m // tm)
                mem = (a_bytes + b_bytes + m * n * ob) / HBM_BYTES_PER_S
                acc = steps * tm * tn * 1.5e-12 if nk > 1 else 0.0
                cost = steps * GRID_STEP_S + max(2.0 * m * n * k / MXU_FLOPS, mem) + acc
                if best is None or cost < best[0]:
                    best = (cost, tm, tn, tk)
    assert best is not None, (m, n, k)
    return best[1:]


def _mm(a, b, *, ta=False, tb=False, out_dtype=F32, name="mm", stack=None):
    if ta:
        kdim, m = a.shape
    else:
        m, kdim = a.shape
    if tb:
        n, kb = b.shape
    else:
        kb, n = b.shape
    assert kb == kdim, (a.shape, b.shape, ta, tb)
    tm, tn, tk = _mm_tiles(m, n, kdim, a.dtype.itemsize, b.dtype.itemsize, jnp.dtype(out_dtype).itemsize)
    nk = kdim // tk
    dims = (((0,) if ta else (1,), (1,) if tb else (0,)), ((), ()))

    def dot(a_ref, b_ref):
        return lax.dot_general(a_ref[...].astype(BF16), b_ref[...].astype(BF16), dims, preferred_element_type=F32)

    if nk == 1:
        def body(a_ref, b_ref, *rest):
            o_ref = rest[-1]
            o_ref[...] = dot(a_ref, b_ref).astype(o_ref.dtype)
        scratch = []
    else:
        def body(a_ref, b_ref, *rest):
            o_ref, acc_ref = rest[-2], rest[-1]
            k = pl.program_id(2)

            @pl.when(k == 0)
            def _():
                acc_ref[...] = dot(a_ref, b_ref)

            @pl.when(jnp.logical_and(k > 0, k < nk - 1))
            def _():
                acc_ref[...] += dot(a_ref, b_ref)

            @pl.when(k == nk - 1)
            def _():
                o_ref[...] = (acc_ref[...] + dot(a_ref, b_ref)).astype(o_ref.dtype)
        scratch = [pltpu.VMEM((tm, tn), F32)]

    a_spec = pl.BlockSpec((tk, tm), lambda i, j, k: (k, i)) if ta else pl.BlockSpec((tm, tk), lambda i, j, k: (i, k))
    b_spec = pl.BlockSpec((tn, tk), lambda i, j, k: (j, k)) if tb else pl.BlockSpec((tk, tn), lambda i, j, k: (k, j))
    in_specs, operands, aliases = [a_spec, b_spec], [a, b], {}
    if stack is None:
        out_spec = pl.BlockSpec((tm, tn), lambda i, j, k: (i, j))
        out_shape = jax.ShapeDtypeStruct((m, n), out_dtype)
    else:
        buf, depth, layer = stack
        out_spec = pl.BlockSpec((None, tm, tn), lambda i, j, k: (layer, i, j))
        out_shape = jax.ShapeDtypeStruct((depth, m, n), out_dtype)
        if buf is not None:
            in_specs.append(pl.BlockSpec(memory_space=pl.ANY))
            operands.append(buf)
            aliases = {2: 0}
    return pl.pallas_call(
        body, name=name, grid=(m // tm, n // tn, nk),
        in_specs=in_specs, out_specs=out_spec, out_shape=out_shape, scratch_shapes=scratch,
        input_output_aliases=aliases,
        compiler_params=_cparams(("parallel", "parallel", "arbitrary")),
    )(*operands)


def _rowwise(fn, rows, params, outs, pouts=(), *, tm=256, name="rowwise"):
    t = rows[0][0].shape[0]
    assert t % tm == 0
    n_r, n_p, n_o, n_po = len(rows), len(params), len(outs), len(pouts)

    def body(*refs):
        r_refs = refs[:n_r]
        p_refs = refs[n_r:n_r + n_p]
        o_refs = refs[n_r + n_p:n_r + n_p + n_o]
        po_refs = refs[n_r + n_p + n_o:]
        res = fn(*[r[...] for r in r_refs], *[p[...] for p in p_refs])
        if not isinstance(res, (tuple, list)):
            res = (res,)
        assert len(res) == n_o + n_po, (len(res), n_o, n_po)
        for o, val in zip(o_refs, res[:n_o]):
            o[...] = val.astype(o.dtype)
        if n_po:
            first = pl.program_id(0) == 0

            @pl.when(first)
            def _():
                for o, val in zip(po_refs, res[n_o:]):
                    o[...] = val.astype(F32)

            @pl.when(jnp.logical_not(first))
            def _():
                for o, val in zip(po_refs, res[n_o:]):
                    o[...] += val.astype(F32)

    in_specs = [pl.BlockSpec((tm, w), functools.partial(lambda i, cb: (i, cb), cb=cb)) for (_, w, cb) in rows]
    in_specs += [pl.BlockSpec(p.shape, lambda i: (0, 0)) for p in params]
    out_specs = [pl.BlockSpec((tm, w), lambda i: (i, 0)) for (w, _) in outs]
    out_specs += [pl.BlockSpec((1, w), lambda i: (0, 0)) for w in pouts]
    out_shape = [jax.ShapeDtypeStruct((t, w), dt) for (w, dt) in outs]
    out_shape += [jax.ShapeDtypeStruct((1, w), F32) for w in pouts]
    res = pl.pallas_call(
        body, name=name, grid=(t // tm,), in_specs=in_specs, out_specs=out_specs, out_shape=out_shape,
        compiler_params=_cparams(("arbitrary",)),
    )(*[r[0] for r in rows], *params)
    return res


def _full(a):
    return (a, a.shape[1], 0)


def _rms(x, g):
    x = x.astype(F32)
    return x * lax.rsqrt(jnp.mean(x * x, axis=-1, keepdims=True) + EPS) * g


def _sum0(v):
    return jnp.sum(v, axis=0, keepdims=True)


def rms_fwd(h, g, name):
    return _rowwise(lambda x, gg: _rms(x, gg), [_full(h)], [g], [(h.shape[1], BF16)], name=name)[0]


def rms_bwd(h, g, dn, dres, name):
    def fn(x, d, r, gg):
        _, vjp = jax.vjp(_rms, x, gg)
        dx, dg = vjp(d.astype(F32))
        return dx + r, dg
    w = h.shape[1]
    return _rowwise(fn, [_full(h), _full(dn), _full(dres)], [g], [(w, F32)], [w], name=name)


def res_rms_fwd(h, f, g, coef, name):
    return _rowwise(lambda x, y, gg: x + coef * _rms(y, gg), [_full(h), _full(f)], [g], [(h.shape[1], F32)],
                    name=name)[0]


def res_rms_bwd(f, g, dh, coef, name):
    def fn(y, d, gg):
        _, vjp = jax.vjp(lambda a, b: coef * _rms(a, b), y, gg)
        dy, dg = vjp(d)
        return dy, dg
    w = f.shape[1]
    return _rowwise(fn, [_full(f), _full(dh)], [g], [(w, BF16)], [w], name=name)


def swiglu_fwd(gu, name):
    f = gu.shape[1] // 2

    def fn(gate, up):
        gate = gate.astype(F32)
        return gate * jax.nn.sigmoid(gate) * up.astype(F32)
    return _rowwise(fn, [(gu, f, 0), (gu, f, 1)], [], [(f, BF16)], name=name)[0]


def swiglu_bwd(gu, da, name):
    t, f2 = gu.shape
    f = f2 // 2
    tm = 256

    def body(gate_ref, up_ref, da_ref, o_ref):
        gate = gate_ref[...].astype(F32)
        up = up_ref[...].astype(F32)
        d = da_ref[...].astype(F32)
        sg = jax.nn.sigmoid(gate)
        silu = gate * sg
        o_ref[:, :f] = (d * up * (sg + silu * (1.0 - sg))).astype(o_ref.dtype)
        o_ref[:, f:] = (d * silu).astype(o_ref.dtype)

    return pl.pallas_call(
        body, name=name, grid=(t // tm,),
        in_specs=[pl.BlockSpec((tm, f), lambda i: (i, 0)), pl.BlockSpec((tm, f), lambda i: (i, 1)),
                  pl.BlockSpec((tm, f), lambda i: (i, 0))],
        out_specs=pl.BlockSpec((tm, f2), lambda i: (i, 0)),
        out_shape=jax.ShapeDtypeStruct((t, f2), BF16),
        compiler_params=_cparams(("parallel",)),
    )(gu, gu, da)


def _merge(g0, g1, g2, g3, z0, z1, z2, z3, b0, b1, b2, b3):
    acc = jax.nn.sigmoid(g0.astype(F32) + b0) * z0.astype(F32)
    acc += jax.nn.sigmoid(g1.astype(F32) + b1) * z1.astype(F32)
    acc += jax.nn.sigmoid(g2.astype(F32) + b2) * z2.astype(F32)
    acc += jax.nn.sigmoid(g3.astype(F32) + b3) * z3.astype(F32)
    return acc


def merge_fwd(proj, zs, bs, name):
    rows = [(proj, D_MODEL, k) for k in range(4)] + [_full(z) for z in zs]
    return _rowwise(_merge, rows, list(bs), [(D_MODEL, BF16)], name=name)[0]


def merge_bwd(proj, zs, bs, dmerged, name):
    def fn(*args):
        d = args[8].astype(F32)
        prim = args[:8] + args[9:]
        _, vjp = jax.vjp(_merge, *prim)
        return vjp(d)
    rows = [(proj, D_MODEL, k) for k in range(4)] + [_full(z) for z in zs] + [_full(dmerged)]
    outs = [(D_MODEL, BF16)] * 8
    return _rowwise(fn, rows, list(bs), outs, [D_MODEL] * 4, name=name)


def _ple(pgl, pr, g):
    return jax.nn.sigmoid(pgl.astype(F32)) * _rms(pr, g)


def ple_fwd(h, pgl, pr, g, name):
    return _rowwise(lambda x, a, b, gg: x + _ple(a, b, gg), [_full(h), _full(pgl), _full(pr)], [g],
                    [(D_MODEL, F32)], name=name)[0]


def ple_bwd(pgl, pr, g, dh, name):
    def fn(a, b, d, gg):
        _, vjp = jax.vjp(_ple, a, b, gg)
        return vjp(d)
    return _rowwise(fn, [_full(pgl), _full(pr), _full(dh)], [g], [(D_MODEL, BF16), (D_MODEL, BF16)], [D_MODEL],
                    name=name)


def loss_fwd_bwd(y, target, name):
    def fn(a, b):
        err = a - b
        return err * (1.0 / D_MODEL), _sum0(err * err) * (0.5 / D_MODEL)
    return _rowwise(fn, [_full(y), _full(target)], [], [(D_MODEL, F32)], [D_MODEL], name=name)


def _shift_down(x, d, row):
    if d == 0:
        return x
    return jnp.where(row >= d, pltpu.roll(x, d, 0), 0.0)


def _shift_up(x, d, row):
    if d == 0:
        return x
    s = x.shape[0]
    return jnp.where(row < s - d, pltpu.roll(x, s - d, 0), 0.0)


def fgate_fwd(proj, bf, bn, s, name):
    def body(a_ref, b_ref, o_ref):
        x = a_ref[...].astype(F32) + b_ref[...]
        c = jnp.minimum(x, 0.0) - jnp.log(1.0 + jnp.exp(-jnp.abs(x)))
        row = lax.broadcasted_iota(jnp.int32, c.shape, 0)
        sh = 1
        while sh < s:
            c = c + _shift_down(c, sh, row)
            sh *= 2
        o_ref[...] = c

    return pl.pallas_call(
        body, name=name, grid=(bn,),
        in_specs=[pl.BlockSpec((s, 128), lambda b: (b, C_AF // 128)), pl.BlockSpec((1, 128), lambda b: (0, 0))],
        out_specs=pl.BlockSpec((s, 128), lambda b: (b, 0)),
        out_shape=jax.ShapeDtypeStruct((bn * s, 128), F32),
        compiler_params=_cparams(("parallel",)),
    )(proj, bf)


def fgate_bwd(proj, bf, dc, bn, s, name):
    def body(a_ref, b_ref, dc_ref, da_ref, db_ref):
        x = a_ref[...].astype(F32) + b_ref[...]
        d = dc_ref[...]
        row = lax.broadcasted_iota(jnp.int32, d.shape, 0)
        sh = 1
        while sh < s:
            d = d + _shift_up(d, sh, row)
            sh *= 2
        da = d * jax.nn.sigmoid(-x)
        da_ref[...] = da.astype(da_ref.dtype)
        first = pl.program_id(0) == 0

        @pl.when(first)
        def _():
            db_ref[...] = _sum0(da)

        @pl.when(jnp.logical_not(first))
        def _():
            db_ref[...] += _sum0(da)

    return pl.pallas_call(
        body, name=name, grid=(bn,),
        in_specs=[pl.BlockSpec((s, 128), lambda b: (b, C_AF // 128)), pl.BlockSpec((1, 128), lambda b: (0, 0)),
                  pl.BlockSpec((s, 128), lambda b: (b, 0))],
        out_specs=[pl.BlockSpec((s, 128), lambda b: (b, 0)), pl.BlockSpec((1, 128), lambda b: (0, 0))],
        out_shape=[jax.ShapeDtypeStruct((bn * s, 128), BF16), jax.ShapeDtypeStruct((1, 128), F32)],
        compiler_params=_cparams(("arbitrary",)),
    )(proj, bf, dc)


FOX_T = 256


def _fox_scores(q, k, cq, ck, j, i):
    t = FOX_T
    s = lax.dot_general(q, k, (((1,), (1,)), ((), ())), preferred_element_type=F32) * SCALE
    qpos = j * t + lax.broadcasted_iota(jnp.int32, (t, t), 0)
    kpos = i * t + lax.broadcasted_iota(jnp.int32, (t, t), 1)
    return jnp.where(qpos >= kpos, s + (cq - ck), NEG)


def fox_fwd(q, k, v, c_col, c_row, name):
    bn, h, s, d = q.shape
    t = FOX_T
    nq = s // t

    def body(q_ref, k_ref, v_ref, cq_ref, ck_ref, o_ref, lse_ref):
        j = pl.program_id(2)
        qv = q_ref[...]
        cq = cq_ref[...]

        def step(i, carry):
            m, l, acc = carry
            ks = pl.multiple_of(i * t, t)
            kc = k_ref[pl.ds(ks, t), :]
            vc = v_ref[pl.ds(ks, t), :]
            sc = _fox_scores(qv, kc, cq, ck_ref[i], j, i)
            m_new = jnp.maximum(m, jnp.max(sc, axis=-1, keepdims=True))
            alpha = jnp.exp(m - m_new)
            p = jnp.exp(sc - m_new)
            l = alpha * l + jnp.sum(p, axis=-1, keepdims=True)
            acc = alpha * acc + jnp.dot(p.astype(BF16), vc, preferred_element_type=F32)
            return m_new, l, acc

        init = (jnp.full((t, 1), NEG, F32), jnp.zeros((t, 1), F32), jnp.zeros((t, d), F32))
        m, l, acc = lax.fori_loop(0, j + 1, step, init)
        o_ref[...] = (acc / l).astype(o_ref.dtype)
        lse_ref[...] = m + jnp.log(l)

    blk_q = pl.BlockSpec((None, None, t, d), lambda b, hh, j: (b, hh, j, 0))
    blk_kv = pl.BlockSpec((None, None, s, d), lambda b, hh, j: (b, hh, 0, 0))
    blk_c1 = pl.BlockSpec((None, None, t, 1), lambda b, hh, j: (b, hh, j, 0))
    blk_cr = pl.BlockSpec((None, None, nq, 1, t), lambda b, hh, j: (b, hh, 0, 0, 0))
    return pl.pallas_call(
        body, name=name, grid=(bn, h, nq),
        in_specs=[blk_q, blk_kv, blk_kv, blk_c1, blk_cr],
        out_specs=[blk_q, blk_c1],
        out_shape=[jax.ShapeDtypeStruct((bn, h, s, d), F32), jax.ShapeDtypeStruct((bn, h, s, 1), F32)],
        compiler_params=_cparams(("parallel", "parallel", "arbitrary")),
    )(q, k, v, c_col, c_row)


def fox_bwd(q, k, v, c_col, c_row, o, lse, do, name):
    bn, h, s, d = q.shape
    t = FOX_T
    nq = s // t

    def body(q_ref, k_ref, v_ref, cq_ref, ck_ref, o_ref, lse_ref, do_ref, dq_ref, dk_ref, dv_ref, dck_ref,
             dcq_ref):
        j = pl.program_id(2)

        @pl.when(j == 0)
        def _():
            dk_ref[...] = jnp.zeros_like(dk_ref)
            dv_ref[...] = jnp.zeros_like(dv_ref)
            dck_ref[...] = jnp.zeros_like(dck_ref)

        qv = q_ref[...]
        cq = cq_ref[...]
        dov = do_ref[...]
        lse = lse_ref[...]
        delta = jnp.sum(dov.astype(F32) * o_ref[...].astype(F32), axis=-1, keepdims=True)

        def step(i, carry):
            dq, dcq = carry
            ks = pl.multiple_of(i * t, t)
            kc = k_ref[pl.ds(ks, t), :]
            vc = v_ref[pl.ds(ks, t), :]
            sc = _fox_scores(qv, kc, cq, ck_ref[i], j, i)
            p = jnp.exp(sc - lse)
            dp = lax.dot_general(dov, vc, (((1,), (1,)), ((), ())), preferred_element_type=F32)
            ds = p * (dp - delta)
            dsb = ds.astype(BF16)
            dq = dq + jnp.dot(dsb, kc, preferred_element_type=F32) * SCALE
            dk_ref[pl.ds(ks, t), :] += lax.dot_general(dsb, qv, (((0,), (0,)), ((), ())),
                                                       preferred_element_type=F32) * SCALE
            dv_ref[pl.ds(ks, t), :] += lax.dot_general(p.astype(BF16), dov, (((0,), (0,)), ((), ())),
                                                       preferred_element_type=F32)
            dck_ref[i] += -_sum0(ds)
            return dq, dcq + jnp.sum(ds, axis=-1, keepdims=True)

        dq, dcq = lax.fori_loop(0, j + 1, step, (jnp.zeros((t, d), F32), jnp.zeros((t, 1), F32)))
        dq_ref[...] = dq
        dcq_ref[...] = dcq

    blk_q = pl.BlockSpec((None, None, t, d), lambda b, hh, j: (b, hh, j, 0))
    blk_kv = pl.BlockSpec((None, None, s, d), lambda b, hh, j: (b, hh, 0, 0))
    blk_c1 = pl.BlockSpec((None, None, t, 1), lambda b, hh, j: (b, hh, j, 0))
    blk_cr = pl.BlockSpec((None, None, nq, 1, t), lambda b, hh, j: (b, hh, 0, 0, 0))
    return pl.pallas_call(
        body, name=name, grid=(bn, h, nq),
        in_specs=[blk_q, blk_kv, blk_kv, blk_c1, blk_cr, blk_q, blk_c1, blk_q],
        out_specs=[blk_q, blk_kv, blk_kv, blk_cr, blk_c1],
        out_shape=[jax.ShapeDtypeStruct((bn, h, s, d), F32), jax.ShapeDtypeStruct((bn, h, s, d), F32),
                   jax.ShapeDtypeStruct((bn, h, s, d), F32), jax.ShapeDtypeStruct((bn, h, nq, 1, t), F32),
                   jax.ShapeDtypeStruct((bn, h, s, 1), F32)],
        compiler_params=_cparams(("parallel", "parallel", "arbitrary")),
    )(q, k, v, c_col, c_row, o, lse, do)


def _swa_valid(n):
    qi = lax.broadcasted_iota(jnp.int32, (QB, 2 * QB), 0)
    kj = lax.broadcasted_iota(jnp.int32, (QB, 2 * QB), 1)
    dist = qi + QB - kj
    return (dist >= 0) & (dist < WINDOW) & ((kj >= QB) | (n > 0))


def _swa_band(ref, n):
    qs = pl.multiple_of(n * QB, QB)
    ps = pl.multiple_of(jnp.maximum(n - 1, 0) * QB, QB)
    return jnp.concatenate([ref[pl.ds(ps, QB), :], ref[pl.ds(qs, QB), :]], axis=0), qs, ps


def swa_fwd(q, k, v, bias, sinks, name):
    bn, hq, s, d = q.shape
    nb = s // QB

    def body(q_ref, k_ref, v_ref, b_ref, s_ref, o_ref, lse_ref):
        def step(n, _):
            kb, qs, _ps = _swa_band(k_ref, n)
            vb, _, _ = _swa_band(v_ref, n)
            valid = _swa_valid(n)
            for g in range(D_GROUP):
                qg = q_ref[g, pl.ds(qs, QB), :]
                sc = lax.dot_general(qg, kb, (((1,), (1,)), ((), ())), preferred_element_type=F32) * SCALE
                sc = jnp.where(valid, sc + b_ref[g], NEG)
                sink = s_ref[g]
                m = jnp.maximum(jnp.max(sc, axis=-1, keepdims=True), sink)
                e = jnp.exp(sc - m)
                z = jnp.sum(e, axis=-1, keepdims=True) + jnp.exp(sink - m)
                p = e / z
                o_ref[g, pl.ds(qs, QB), :] = jnp.dot(p.astype(BF16), vb, preferred_element_type=F32
                                                     ).astype(o_ref.dtype)
                lse_ref[g, pl.ds(qs, QB), :] = m + jnp.log(z)
            return 0

        lax.fori_loop(0, nb, step, 0)

    blk_q = pl.BlockSpec((None, D_GROUP, s, d), lambda b, kh: (b, kh, 0, 0))
    blk_kv = pl.BlockSpec((None, None, s, d), lambda b, kh: (b, kh, 0, 0))
    blk_l = pl.BlockSpec((None, D_GROUP, s, 1), lambda b, kh: (b, kh, 0, 0))
    return pl.pallas_call(
        body, name=name, grid=(bn, D_KV_HEADS),
        in_specs=[blk_q, blk_kv, blk_kv, pl.BlockSpec((D_GROUP, QB, 2 * QB), lambda b, kh: (kh, 0, 0)),
                  pl.BlockSpec((D_GROUP, QB, 1), lambda b, kh: (kh, 0, 0))],
        out_specs=[blk_q, blk_l],
        out_shape=[jax.ShapeDtypeStruct((bn, hq, s, d), BF16), jax.ShapeDtypeStruct((bn, hq, s, 1), F32)],
        compiler_params=_cparams(("parallel", "parallel")),
    )(q, k, v, bias, sinks)


def swa_bwd(q, k, v, bias, sinks, o, lse, do, name):
    bn, hq, s, d = q.shape
    nb = s // QB

    def body(q_ref, k_ref, v_ref, b_ref, s_ref, o_ref, lse_ref, do_ref, dq_ref, dk_ref, dv_ref, db_ref, dsk_ref):
        @pl.when(pl.program_id(1) == 0)
        def _():
            db_ref[...] = jnp.zeros_like(db_ref)
            dsk_ref[...] = jnp.zeros_like(dsk_ref)

        dk_ref[...] = jnp.zeros_like(dk_ref)
        dv_ref[...] = jnp.zeros_like(dv_ref)

        def step(n, _):
            kb, qs, ps = _swa_band(k_ref, n)
            vb, _, _ = _swa_band(v_ref, n)
            valid = _swa_valid(n)
            dkb = jnp.zeros((2 * QB, d), F32)
            dvb = jnp.zeros((2 * QB, d), F32)
            for g in range(D_GROUP):
                qg = q_ref[g, pl.ds(qs, QB), :]
                dog = do_ref[g, pl.ds(qs, QB), :]
                og = o_ref[g, pl.ds(qs, QB), :]
                lse = lse_ref[g, pl.ds(qs, QB), :]
                sc = lax.dot_general(qg, kb, (((1,), (1,)), ((), ())), preferred_element_type=F32) * SCALE
                sc = jnp.where(valid, sc + b_ref[g], NEG)
                p = jnp.exp(sc - lse)
                delta = jnp.sum(dog.astype(F32) * og.astype(F32), axis=-1, keepdims=True)
                dp = lax.dot_general(dog, vb, (((1,), (1,)), ((), ())), preferred_element_type=F32)
                ds = p * (dp - delta)
                dsb = ds.astype(BF16)
                dq_ref[g, pl.ds(qs, QB), :] = jnp.dot(dsb, kb, preferred_element_type=F32) * SCALE
                dkb = dkb + lax.dot_general(dsb, qg, (((0,), (0,)), ((), ())), preferred_element_type=F32) * SCALE
                dvb = dvb + lax.dot_general(p.astype(BF16), dog, (((0,), (0,)), ((), ())),
                                            preferred_element_type=F32)
                db_ref[g] += ds
                dsk_ref[g] += -jnp.exp(s_ref[g] - lse) * delta
            dk_ref[pl.ds(ps, QB), :] += dkb[:QB]
            dk_ref[pl.ds(qs, QB), :] += dkb[QB:]
            dv_ref[pl.ds(ps, QB), :] += dvb[:QB]
            dv_ref[pl.ds(qs, QB), :] += dvb[QB:]
            return 0

        lax.fori_loop(0, nb, step, 0)

    blk_q = pl.BlockSpec((None, D_GROUP, s, d), lambda kh, b: (b, kh, 0, 0))
    blk_kv = pl.BlockSpec((None, None, s, d), lambda kh, b: (b, kh, 0, 0))
    blk_l = pl.BlockSpec((None, D_GROUP, s, 1), lambda kh, b: (b, kh, 0, 0))
    blk_b = pl.BlockSpec((D_GROUP, QB, 2 * QB), lambda kh, b: (kh, 0, 0))
    blk_s = pl.BlockSpec((D_GROUP, QB, 1), lambda kh, b: (kh, 0, 0))
    return pl.pallas_call(
        body, name=name, grid=(D_KV_HEADS, bn),
        in_specs=[blk_q, blk_kv, blk_kv, blk_b, blk_s, blk_q, blk_l, blk_q],
        out_specs=[blk_q, blk_kv, blk_kv, blk_b, blk_s],
        out_shape=[jax.ShapeDtypeStruct((bn, hq, s, d), F32), jax.ShapeDtypeStruct((bn, D_KV_HEADS, s, d), F32),
                   jax.ShapeDtypeStruct((bn, D_KV_HEADS, s, d), F32),
                   jax.ShapeDtypeStruct((hq, QB, 2 * QB), F32), jax.ShapeDtypeStruct((hq, QB, 1), F32)],
        compiler_params=_cparams(("parallel", "arbitrary")),
    )(q, k, v, bias, sinks, o, lse, do)


def _bucket_table():
    dist = jnp.maximum(jnp.arange(QB)[:, None] + QB - jnp.arange(2 * QB)[None, :], 0)
    max_exact = REL_BUCKETS // 2
    large = max_exact + (jnp.log(jnp.maximum(dist, 1).astype(F32) / max_exact)
                         / math.log(REL_MAX_DIST / max_exact) * (REL_BUCKETS - max_exact)).astype(jnp.int32)
    large = jnp.minimum(large, REL_BUCKETS - 1)
    return jnp.where(dist < max_exact, dist, large).astype(F32)


def band_bias_fwd(bucket, rel_bias, name):
    def body(bk_ref, rel_ref, o_ref):
        bk = bk_ref[...]
        for hh in range(D_Q_HEADS):
            acc = jnp.zeros(bk.shape, F32)
            for b in range(REL_BUCKETS):
                acc = jnp.where(bk == float(b), rel_ref[b, hh], acc)
            o_ref[hh] = acc

    return pl.pallas_call(
        body, name=name,
        in_specs=[pl.BlockSpec(memory_space=pltpu.VMEM), pl.BlockSpec(memory_space=pltpu.SMEM)],
        out_specs=pl.BlockSpec(memory_space=pltpu.VMEM),
        out_shape=jax.ShapeDtypeStruct((D_Q_HEADS, QB, 2 * QB), F32),
    )(bucket, rel_bias)


def band_bias_bwd(bucket, dbias_layers, name):
    nl = len(dbias_layers)

    def body(bk_ref, *refs):
        o_ref = refs[nl]
        bk = bk_ref[...]
        for hh in range(D_Q_HEADS):
            tot = refs[0][hh]
            for r in refs[1:nl]:
                tot = tot + r[hh]
            for b in range(REL_BUCKETS):
                part = jnp.sum(jnp.where(bk == float(b), tot, 0.0), axis=0, keepdims=True)
                val = jnp.sum(part, axis=1, keepdims=True)
                o_ref[hh, b:b + 1, :] = jnp.broadcast_to(val, (1, 128))

    return pl.pallas_call(
        body, name=name,
        in_specs=[pl.BlockSpec(memory_space=pltpu.VMEM)] * (nl + 1),
        out_specs=pl.BlockSpec(memory_space=pltpu.VMEM),
        out_shape=jax.ShapeDtypeStruct((D_Q_HEADS, REL_BUCKETS, 128), F32),
    )(bucket, *dbias_layers)


def _proj_blk(s, col):
    return pl.BlockSpec((s, 256), functools.partial(lambda b, cb: (b, cb), cb=col // 256))


def convb_fwd(proj, w, bn, s, name):
    kk = w.shape[0]

    def body(bg_ref, cg_ref, xb_ref, w_ref, o_ref):
        x = cg_ref[...].astype(F32) * xb_ref[...].astype(F32)
        row = lax.broadcasted_iota(jnp.int32, x.shape, 0)
        y = jnp.zeros_like(x)
        for k in range(kk):
            y = y + w_ref[k:k + 1, :] * _shift_down(x, kk - 1 - k, row)
        o_ref[...] = (bg_ref[...].astype(F32) * y).astype(o_ref.dtype)

    return pl.pallas_call(
        body, name=name, grid=(bn,),
        in_specs=[_proj_blk(s, C_BG), _proj_blk(s, C_CG), _proj_blk(s, C_XB), pl.BlockSpec(w.shape, lambda b: (0, 0))],
        out_specs=pl.BlockSpec((s, 256), lambda b: (b, 0)),
        out_shape=jax.ShapeDtypeStruct((bn * s, 256), BF16),
        compiler_params=_cparams(("parallel",)),
    )(proj, proj, proj, w)


def convb_bwd(proj, w, dyb, bn, s, name):
    kk = w.shape[0]

    def body(bg_ref, cg_ref, xb_ref, w_ref, d_ref, dbg_ref, dcg_ref, dxb_ref, dw_ref):
        @pl.when(pl.program_id(0) == 0)
        def _():
            dw_ref[...] = jnp.zeros_like(dw_ref)

        cg = cg_ref[...].astype(F32)
        xb = xb_ref[...].astype(F32)
        d = d_ref[...].astype(F32)
        x = cg * xb
        row = lax.broadcasted_iota(jnp.int32, x.shape, 0)
        dy = d * bg_ref[...].astype(F32)
        y = jnp.zeros_like(x)
        dx = jnp.zeros_like(x)
        for k in range(kk):
            xs = _shift_down(x, kk - 1 - k, row)
            y = y + w_ref[k:k + 1, :] * xs
            dx = dx + w_ref[k:k + 1, :] * _shift_up(dy, kk - 1 - k, row)
            dw_ref[k:k + 1, :] += _sum0(dy * xs)
        dbg_ref[...] = (d * y).astype(dbg_ref.dtype)
        dcg_ref[...] = (dx * xb).astype(dcg_ref.dtype)
        dxb_ref[...] = (dx * cg).astype(dxb_ref.dtype)

    blk = pl.BlockSpec((s, 256), lambda b: (b, 0))
    return pl.pallas_call(
        body, name=name, grid=(bn,),
        in_specs=[_proj_blk(s, C_BG), _proj_blk(s, C_CG), _proj_blk(s, C_XB), pl.BlockSpec(w.shape, lambda b: (0, 0)),
                  blk],
        out_specs=[blk, blk, blk, pl.BlockSpec((8, 256), lambda b: (0, 0))],
        out_shape=[jax.ShapeDtypeStruct((bn * s, 256), BF16)] * 3 + [jax.ShapeDtypeStruct((8, 256), F32)],
        compiler_params=_cparams(("arbitrary",)),
    )(proj, proj, proj, w, dyb)


def _convc_core(ca, cb, w_ref, bias, kk, row):
    sg = jax.nn.sigmoid(cb)
    glu = ca * sg
    y = jnp.zeros_like(glu)
    for k in range(kk):
        y = y + w_ref[k:k + 1, :] * _shift_down(glu, kk - 1 - k, row)
    y = y + bias
    mu = jnp.mean(y, axis=-1, keepdims=True)
    xc = y - mu
    r = lax.rsqrt(jnp.mean(xc * xc, axis=-1, keepdims=True) + EPS)
    return sg, glu, xc * r, r


def convc_fwd(proj, w, bias, gain, lbias, bn, s, name):
    kk = w.shape[0]

    def body(ca_ref, cb_ref, w_ref, b_ref, g_ref, lb_ref, o_ref):
        ca = ca_ref[...].astype(F32)
        row = lax.broadcasted_iota(jnp.int32, ca.shape, 0)
        _, _, xh, _ = _convc_core(ca, cb_ref[...].astype(F32), w_ref, b_ref[...], kk, row)
        ln = xh * g_ref[...] + lb_ref[...]
        o_ref[...] = (ln * jax.nn.sigmoid(ln)).astype(o_ref.dtype)

    vec = pl.BlockSpec((1, 256), lambda b: (0, 0))
    return pl.pallas_call(
        body, name=name, grid=(bn,),
        in_specs=[_proj_blk(s, C_CA), _proj_blk(s, C_CB), pl.BlockSpec(w.shape, lambda b: (0, 0)), vec, vec, vec],
        out_specs=pl.BlockSpec((s, 256), lambda b: (b, 0)),
        out_shape=jax.ShapeDtypeStruct((bn * s, 256), BF16),
        compiler_params=_cparams(("parallel",)),
    )(proj, proj, w, bias, gain, lbias)


def convc_bwd(proj, w, bias, gain, lbias, dyc, bn, s, name):
    kk = w.shape[0]

    def body(ca_ref, cb_ref, w_ref, b_ref, g_ref, lb_ref, d_ref, dca_ref, dcb_ref, dw_ref, db_ref, dg_ref, dlb_ref):
        @pl.when(pl.program_id(0) == 0)
        def _():
            dw_ref[...] = jnp.zeros_like(dw_ref)
            db_ref[...] = jnp.zeros_like(db_ref)
            dg_ref[...] = jnp.zeros_like(dg_ref)
            dlb_ref[...] = jnp.zeros_like(dlb_ref)

        ca = ca_ref[...].astype(F32)
        row = lax.broadcasted_iota(jnp.int32, ca.shape, 0)
        sg, glu, xh, r = _convc_core(ca, cb_ref[...].astype(F32), w_ref, b_ref[...], kk, row)
        ln = xh * g_ref[...] + lb_ref[...]
        sl = jax.nn.sigmoid(ln)
        dl = d_ref[...].astype(F32) * (sl + ln * sl * (1.0 - sl))
        dg_ref[...] += _sum0(dl * xh)
        dlb_ref[...] += _sum0(dl)
        dxh = dl * g_ref[...]
        dy = r * (dxh - jnp.mean(dxh, axis=-1, keepdims=True) - xh * jnp.mean(dxh * xh, axis=-1, keepdims=True))
        db_ref[...] += _sum0(dy)
        dglu = jnp.zeros_like(glu)
        for k in range(kk):
            dw_ref[k:k + 1, :] += _sum0(dy * _shift_down(glu, kk - 1 - k, row))
            dglu = dglu + w_ref[k:k + 1, :] * _shift_up(dy, kk - 1 - k, row)
        dca_ref[...] = (dglu * sg).astype(dca_ref.dtype)
        dcb_ref[...] = (dglu * ca * sg * (1.0 - sg)).astype(dcb_ref.dtype)

    vec = pl.BlockSpec((1, 256), lambda b: (0, 0))
    blk = pl.BlockSpec((s, 256), lambda b: (b, 0))
    return pl.pallas_call(
        body, name=name, grid=(bn,),
        in_specs=[_proj_blk(s, C_CA), _proj_blk(s, C_CB), pl.BlockSpec(w.shape, lambda b: (0, 0)), vec, vec, vec, blk],
        out_specs=[blk, blk, pl.BlockSpec((32, 256), lambda b: (0, 0)), vec, vec, vec],
        out_shape=[jax.ShapeDtypeStruct((bn * s, 256), BF16)] * 2 + [jax.ShapeDtypeStruct((32, 256), F32)]
        + [jax.ShapeDtypeStruct((1, 256), F32)] * 3,
        compiler_params=_cparams(("arbitrary",)),
    )(proj, proj, w, bias, gain, lbias, dyc)


def adamw(w, g, m, v, name):
    shape = w.shape
    cols = shape[-1]
    rows = w.size // cols
    tr = _pick(rows, (256, 128, 64, 32, 16, 8))

    def body(w_ref, g_ref, m_ref, v_ref, d_ref, nm_ref, nv_ref):
        gg = g_ref[...]
        mm = ADAM_B1 * m_ref[...] + (1.0 - ADAM_B1) * gg
        vv = ADAM_B2 * v_ref[...] + (1.0 - ADAM_B2) * jnp.square(gg)
        m_hat = mm / (1.0 - ADAM_B1 ** ADAM_STEP)
        v_hat = vv / (1.0 - ADAM_B2 ** ADAM_STEP)
        d_ref[...] = -ADAM_LR * (m_hat / (jnp.sqrt(v_hat) + ADAM_EPS) + ADAM_WD * w_ref[...])
        nm_ref[...] = mm
        nv_ref[...] = vv

    blk = pl.BlockSpec((tr, cols), lambda i: (i, 0))
    outs = pl.pallas_call(
        body, name=name, grid=(rows // tr,), in_specs=[blk] * 4, out_specs=[blk] * 3,
        out_shape=[jax.ShapeDtypeStruct((rows, cols), F32)] * 3,
        compiler_params=_cparams(("parallel",)),
    )(*[a.reshape(rows, cols) for a in (w, g, m, v)])
    return [o.reshape(shape) for o in outs]


def add_halves(own, recv, name):
    n, r, c = own.shape
    tr = _pick(r, (512, 256, 128, 64, 32, 16, 8))
    blk = pl.BlockSpec((None, tr, c), lambda i, j: (i, j, 0))

    def body(a_ref, b_ref, o_ref):
        o_ref[...] = a_ref[...] + b_ref[...]

    return pl.pallas_call(
        body, name=name, grid=(n, r // tr), in_specs=[blk, blk], out_specs=blk,
        out_shape=jax.ShapeDtypeStruct((n, r, c), F32), compiler_params=_cparams(("parallel", "parallel")),
    )(own, recv)


def sum_slots(slots, name):
    n, r, c = slots.shape
    tr = _pick(r, (512, 256, 128, 64, 32, 16, 8))

    def body(a_ref, o_ref):
        acc = a_ref[0]
        for k in range(1, n):
            acc = acc + a_ref[k]
        o_ref[...] = acc

    return pl.pallas_call(
        body, name=name, grid=(r // tr,), in_specs=[pl.BlockSpec((n, tr, c), lambda j: (0, j, 0))],
        out_specs=pl.BlockSpec((tr, c), lambda j: (j, 0)),
        out_shape=jax.ShapeDtypeStruct((r, c), F32), compiler_params=_cparams(("parallel",)),
    )(slots)


ANY = pl.BlockSpec(memory_space=pl.ANY)


def _place():
    x, y, c = lax.axis_index("x"), lax.axis_index("y"), lax.axis_index("c")
    return x, y, c


def gather_shards(pack, name):
    r, cols = pack.shape
    half = r // 2

    def body(src_ref, out_ref, send_sems, recv_sems, local_sem):
        x, y, c = _place()
        sibling = (x, y, 1 - c)
        chips = [(1 - x, y), (x, 1 - y), (1 - x, 1 - y)]

        def rows(px, py, pc):
            return out_ref.at[2 * px + py, pl.ds(pc * half, half), :]

        mine = pltpu.make_async_copy(src_ref, out_ref.at[2 * x + y], local_sem)
        mine.start()

        def copy(k, blk, to, src=None):
            return pltpu.make_async_remote_copy(
                src_ref=rows(*blk) if src is None else src, dst_ref=rows(*blk),
                send_sem=send_sems.at[k], recv_sem=recv_sems.at[k], device_id=to, device_id_type=MESH)

        first = [copy(j, (x, y, c), (*chip, c), src=src_ref.at[pl.ds(c * half, half), :])
                 for j, chip in enumerate(chips)]
        for cp in first:
            cp.start()
        passed = [copy(3 + j, (*chip, c), sibling) for j, chip in enumerate(chips)]
        for j, chip in enumerate(chips):
            copy(j, (*chip, c), (x, y, c)).wait_recv()
            passed[j].start()
        for j, chip in enumerate(chips):
            copy(3 + j, (*chip, 1 - c), (x, y, c)).wait_recv()
        for cp in first + passed:
            cp.wait_send()
        mine.wait()

    return pl.pallas_call(
        body, name=name, in_specs=[ANY], out_specs=ANY,
        out_shape=jax.ShapeDtypeStruct((N_CHIPS, r, cols), pack.dtype),
        scratch_shapes=[pltpu.SemaphoreType.DMA((6,)), pltpu.SemaphoreType.DMA((6,)), pltpu.SemaphoreType.DMA],
    )(pack)


def exchange_sibling_halves(g, name):
    n, r, cols = g.shape
    half = r // 2

    def body(g_ref, own_ref, recv_ref, send_sems, recv_sems, local_sem):
        x, y, c = _place()
        sibling = (x, y, 1 - c)
        mine = pltpu.make_async_copy(g_ref.at[:, pl.ds(c * half, half), :], own_ref, local_sem)
        mine.start()
        cp = pltpu.make_async_remote_copy(
            src_ref=g_ref.at[:, pl.ds((1 - c) * half, half), :], dst_ref=recv_ref,
            send_sem=send_sems.at[0], recv_sem=recv_sems.at[0], device_id=sibling, device_id_type=MESH)
        cp.start()
        cp.wait()
        mine.wait()

    return pl.pallas_call(
        body, name=name, in_specs=[ANY], out_specs=[ANY, ANY],
        out_shape=[jax.ShapeDtypeStruct((n, half, cols), g.dtype)] * 2,
        scratch_shapes=[pltpu.SemaphoreType.DMA((1,)), pltpu.SemaphoreType.DMA((1,)), pltpu.SemaphoreType.DMA],
    )(g)


def scatter_to_chips(part, name):
    n, h, cols = part.shape

    def body(p_ref, slot_ref, send_sems, recv_sems, local_sem):
        x, y, c = _place()
        me = 2 * x + y
        chips = [(1 - x, y), (x, 1 - y), (1 - x, 1 - y)]
        mine = pltpu.make_async_copy(p_ref.at[me], slot_ref.at[me], local_sem)
        mine.start()
        cps = [pltpu.make_async_remote_copy(
            src_ref=p_ref.at[2 * px + py], dst_ref=slot_ref.at[me],
            send_sem=send_sems.at[j], recv_sem=recv_sems.at[j], device_id=(px, py, c), device_id_type=MESH)
            for j, (px, py) in enumerate(chips)]
        for cp in cps:
            cp.start()
        for j, (px, py) in enumerate(chips):
            pltpu.make_async_remote_copy(
                src_ref=p_ref.at[me], dst_ref=slot_ref.at[2 * px + py],
                send_sem=send_sems.at[j], recv_sem=recv_sems.at[j], device_id=(px, py, c),
                device_id_type=MESH).wait_recv()
        for cp in cps:
            cp.wait_send()
        mine.wait()

    return pl.pallas_call(
        body, name=name, in_specs=[ANY], out_specs=ANY,
        out_shape=jax.ShapeDtypeStruct((n, h, cols), part.dtype),
        scratch_shapes=[pltpu.SemaphoreType.DMA((3,)), pltpu.SemaphoreType.DMA((3,)), pltpu.SemaphoreType.DMA],
    )(part)


def join_sibling_halves(mine_half, name):
    h, cols = mine_half.shape

    def body(m_ref, out_ref, send_sems, recv_sems, local_sem):
        x, y, c = _place()
        sibling = (x, y, 1 - c)
        own = pltpu.make_async_copy(m_ref, out_ref.at[pl.ds(c * h, h), :], local_sem)
        own.start()
        cp = pltpu.make_async_remote_copy(
            src_ref=m_ref, dst_ref=out_ref.at[pl.ds(c * h, h), :],
            send_sem=send_sems.at[0], recv_sem=recv_sems.at[0], device_id=sibling, device_id_type=MESH)
        cp.start()
        pltpu.make_async_remote_copy(
            src_ref=m_ref, dst_ref=out_ref.at[pl.ds((1 - c) * h, h), :],
            send_sem=send_sems.at[0], recv_sem=recv_sems.at[0], device_id=sibling, device_id_type=MESH).wait_recv()
        cp.wait_send()
        own.wait()

    return pl.pallas_call(
        body, name=name, in_specs=[ANY], out_specs=ANY,
        out_shape=jax.ShapeDtypeStruct((2 * h, cols), mine_half.dtype),
        scratch_shapes=[pltpu.SemaphoreType.DMA((1,)), pltpu.SemaphoreType.DMA((1,)), pltpu.SemaphoreType.DMA],
    )(mine_half)


def _kind(n):
    return 'win' if n == 'w_in' else ('row' if n in ROW_SHARDED else 'col')


def _chip_ids():
    x, y, c = _place()
    chips = [(1 - x, y), (x, 1 - y), (1 - x, 1 - y)]
    return x, y, c, 2 * x + y, chips, [2 * px + py for px, py in chips]


def gather_weights(shards, name):
    names = list(SHARDED)
    nt = len(names)
    kinds = [_kind(n) for n in names]
    shapes = [shards[n].shape for n in names]
    depth = shapes[0][0]
    half = depth // 2

    def out_shape(kind, shp):
        if kind == 'col':
            return (shp[0], shp[1], N_CHIPS * shp[2])
        if kind == 'row':
            return (shp[0], N_CHIPS * shp[1], shp[2])
        return (N_CHIPS,) + tuple(shp)

    def body(*refs):
        src, out = refs[:nt], refs[nt:2 * nt]
        send_sems, recv_sems, local_sems = refs[2 * nt:]
        x, y, c, me, chips, chip_idx = _chip_ids()
        sibling = (x, y, 1 - c)

        def win(t, chip, lo, cnt):
            _, a, b = shapes[t]
            if kinds[t] == 'col':
                return out[t].at[pl.ds(lo, cnt), :, pl.ds(chip * b, b)]
            if kinds[t] == 'row':
                return out[t].at[pl.ds(lo, cnt), pl.ds(chip * a, a), :]
            return out[t].at[chip, pl.ds(lo, cnt)]

        def remote(t, k, chip, lo, to, src_ref=None):
            w = win(t, chip, lo, half)
            return pltpu.make_async_remote_copy(
                src_ref=w if src_ref is None else src_ref, dst_ref=w, send_sem=send_sems.at[6 * t + k],
                recv_sem=recv_sems.at[6 * t + k], device_id=to, device_id_type=MESH)

        mine = [pltpu.make_async_copy(src[t], win(t, me, 0, depth), local_sems.at[t]) for t in range(nt)]
        for cp in mine:
            cp.start()
        first = [[remote(t, j, me, c * half, (*chips[j], c), src_ref=src[t].at[pl.ds(c * half, half)])
                  for j in range(3)] for t in range(nt)]
        for t in range(nt):
            for cp in first[t]:
                cp.start()
        passed = [[remote(t, 3 + j, chip_idx[j], c * half, sibling) for j in range(3)] for t in range(nt)]
        for t in range(nt):
            for j in range(3):
                remote(t, j, chip_idx[j], c * half, (x, y, c)).wait_recv()
                passed[t][j].start()
        for t in range(nt):
            for j in range(3):
                remote(t, 3 + j, chip_idx[j], (1 - c) * half, (x, y, c)).wait_recv()
        for t in range(nt):
            for cp in first[t] + passed[t]:
                cp.wait_send()
            mine[t].wait()

    outs = pl.pallas_call(
        body, name=name, in_specs=[ANY] * nt, out_specs=[ANY] * nt,
        out_shape=[jax.ShapeDtypeStruct(out_shape(k, s), BF16) for k, s in zip(kinds, shapes)],
        scratch_shapes=[pltpu.SemaphoreType.DMA((6 * nt,)), pltpu.SemaphoreType.DMA((6 * nt,)),
                        pltpu.SemaphoreType.DMA((nt,))],
    )(*[shards[n] for n in names])
    return dict(zip(names, outs))


def _half_win(ref, kind, hc, layer):
    if kind == 'col':
        hk = ref.shape[1] // 2
        return ref.at[layer, pl.ds(hc * hk, hk), :]
    if kind == 'row':
        hn = ref.shape[2] // 2
        return ref.at[layer, :, pl.ds(hc * hn, hn)]
    hk = ref.shape[2] // 2
    return ref.at[layer, :, pl.ds(hc * hk, hk), :]


def _half_shape(kind, shp):
    if kind == 'col':
        return (shp[0], shp[1] // 2, shp[2])
    if kind == 'row':
        return (shp[0], shp[1], shp[2] // 2)
    return (shp[0], shp[1], shp[2] // 2, shp[3])


def rs_sibling(grads, name):
    names = list(SHARDED)
    nt = len(names)
    kinds = [_kind(n) for n in names]
    shapes = [grads[n].shape for n in names]
    depth = shapes[0][0]

    def body(*refs):
        src, out = refs[:nt], refs[nt:2 * nt]
        send_sems, recv_sems = refs[2 * nt:]
        x, y, c = _place()
        cps = []
        for t in range(nt):
            for l in range(depth):
                cps.append(pltpu.make_async_remote_copy(
                    src_ref=_half_win(src[t], kinds[t], 1 - c, l), dst_ref=out[t].at[l],
                    send_sem=send_sems.at[depth * t + l], recv_sem=recv_sems.at[depth * t + l],
                    device_id=(x, y, 1 - c), device_id_type=MESH))
        for cp in cps:
            cp.start()
        for cp in cps:
            cp.wait()

    outs = pl.pallas_call(
        body, name=name, in_specs=[ANY] * nt, out_specs=[ANY] * nt,
        out_shape=[jax.ShapeDtypeStruct(_half_shape(k, s), F32) for k, s in zip(kinds, shapes)],
        scratch_shapes=[pltpu.SemaphoreType.DMA((depth * nt,)), pltpu.SemaphoreType.DMA((depth * nt,))],
    )(*[grads[n] for n in names])
    return dict(zip(names, outs))


EW_BLOCK_ELEMS = 512 * 1024


def rs_add(kind, g, recv, c_arr, name):
    shp = recv.shape
    rows, cols = shp[-2], shp[-1]
    tr = _pick(rows, [r for r in (1408, 1024, 704, 512, 256, 128, 64, 32, 16, 8) if r * cols <= EW_BLOCK_ELEMS])
    nb = rows // tr
    lead = (None,) * (len(shp) - 2)
    blk = pl.BlockSpec(lead + (tr, cols), lambda *a: tuple(a[:len(shp) - 2]) + (a[len(shp) - 2], 0))
    if kind == 'row':
        g_blk = pl.BlockSpec(lead + (tr, cols), lambda *a: tuple(a[:len(shp) - 2]) + (a[len(shp) - 2], a[-1][0]))
    else:
        g_blk = pl.BlockSpec(lead + (tr, cols),
                             lambda *a: tuple(a[:len(shp) - 2]) + (a[-1][0] * nb + a[len(shp) - 2], 0))

    def body(c_ref, g_ref, r_ref, o_ref):
        o_ref[...] = (g_ref[...] + r_ref[...]).astype(o_ref.dtype)

    grid_spec = pltpu.PrefetchScalarGridSpec(
        num_scalar_prefetch=1, grid=tuple(shp[:-2]) + (nb,), in_specs=[g_blk, blk], out_specs=blk)
    return pl.pallas_call(
        body, name=name, grid_spec=grid_spec, out_shape=jax.ShapeDtypeStruct(shp, BF16),
        compiler_params=_cparams(None),
    )(c_arr, g, recv)


def _chip_win(ref, kind, chip):
    if kind == 'col':
        ns = ref.shape[2] // N_CHIPS
        return ref.at[:, :, pl.ds(chip * ns, ns)]
    if kind == 'row':
        ks = ref.shape[1] // N_CHIPS
        return ref.at[:, pl.ds(chip * ks, ks), :]
    return ref.at[:, chip]


def _chip_shape(kind, shp):
    if kind == 'col':
        return (shp[0], shp[1], shp[2] // N_CHIPS)
    if kind == 'row':
        return (shp[0], shp[1] // N_CHIPS, shp[2])
    return (shp[0], shp[2], shp[3])


def rs_chips(parts, name):
    names = list(SHARDED)
    nt = len(names)
    kinds = [_kind(n) for n in names]
    shapes = [parts[n].shape for n in names]

    def body(*refs):
        src, out = refs[:nt], refs[nt:2 * nt]
        send_sems, recv_sems, local_sems = refs[2 * nt:]
        x, y, c, me, chips, chip_idx = _chip_ids()
        mine = [pltpu.make_async_copy(_chip_win(src[t], kinds[t], me), out[t].at[me], local_sems.at[t])
                for t in range(nt)]
        for cp in mine:
            cp.start()
        cps = [[pltpu.make_async_remote_copy(
            src_ref=_chip_win(src[t], kinds[t], chip_idx[j]), dst_ref=out[t].at[me],
            send_sem=send_sems.at[3 * t + j], recv_sem=recv_sems.at[3 * t + j],
            device_id=(*chips[j], c), device_id_type=MESH) for j in range(3)] for t in range(nt)]
        for t in range(nt):
            for cp in cps[t]:
                cp.start()
        for t in range(nt):
            for j in range(3):
                pltpu.make_async_remote_copy(
                    src_ref=_chip_win(src[t], kinds[t], me), dst_ref=out[t].at[chip_idx[j]],
                    send_sem=send_sems.at[3 * t + j], recv_sem=recv_sems.at[3 * t + j],
                    device_id=(*chips[j], c), device_id_type=MESH).wait_recv()
        for t in range(nt):
            for cp in cps[t]:
                cp.wait_send()
            mine[t].wait()

    outs = pl.pallas_call(
        body, name=name, in_specs=[ANY] * nt, out_specs=[ANY] * nt,
        out_shape=[jax.ShapeDtypeStruct((N_CHIPS,) + _chip_shape(k, s), parts[n].dtype)
                   for n, k, s in zip(names, kinds, shapes)],
        scratch_shapes=[pltpu.SemaphoreType.DMA((3 * nt,)), pltpu.SemaphoreType.DMA((3 * nt,)),
                        pltpu.SemaphoreType.DMA((nt,))],
    )(*[parts[n] for n in names])
    return dict(zip(names, outs))


def rs_sum(slots, name):
    n, depth, r, cols = slots.shape
    tr = _pick(r, [q for q in (1408, 1024, 704, 512, 256, 128, 64, 32, 16, 8) if q * cols * n <= 2 * EW_BLOCK_ELEMS])

    def body(a_ref, o_ref):
        acc = a_ref[0].astype(F32)
        for k in range(1, n):
            acc = acc + a_ref[k].astype(F32)
        o_ref[...] = acc

    return pl.pallas_call(
        body, name=name, grid=(depth, r // tr),
        in_specs=[pl.BlockSpec((n, None, tr, cols), lambda l, i: (0, l, i, 0))],
        out_specs=pl.BlockSpec((None, tr, cols), lambda l, i: (l, i, 0)),
        out_shape=jax.ShapeDtypeStruct((depth, r, cols), F32), compiler_params=_cparams(("parallel", "parallel")),
    )(slots)


def rs_join(reds, name):
    names = list(SHARDED)
    nt = len(names)
    kinds = [_kind(n) for n in names]
    shapes = [reds[n].shape for n in names]
    depth = shapes[0][0]

    def full_shape(kind, shp):
        if kind == 'row':
            return (shp[0], shp[1], 2 * shp[2])
        return (shp[0], 2 * shp[1], shp[2])

    def win(ref, kind, hc, layer):
        if kind == 'row':
            hn = ref.shape[2] // 2
            return ref.at[layer, :, pl.ds(hc * hn, hn)]
        hk = ref.shape[1] // 2
        return ref.at[layer, pl.ds(hc * hk, hk), :]

    def body(*refs):
        src, out = refs[:nt], refs[nt:2 * nt]
        send_sems, recv_sems, local_sems = refs[2 * nt:]
        x, y, c = _place()
        own, cps = [], []
        for t in range(nt):
            for l in range(depth):
                i = depth * t + l
                own.append(pltpu.make_async_copy(src[t].at[l], win(out[t], kinds[t], c, l), local_sems.at[i]))
                cps.append(pltpu.make_async_remote_copy(
                    src_ref=src[t].at[l], dst_ref=win(out[t], kinds[t], c, l), send_sem=send_sems.at[i],
                    recv_sem=recv_sems.at[i], device_id=(x, y, 1 - c), device_id_type=MESH))
        for cp in own + cps:
            cp.start()
        for t in range(nt):
            for l in range(depth):
                i = depth * t + l
                pltpu.make_async_remote_copy(
                    src_ref=src[t].at[l], dst_ref=win(out[t], kinds[t], 1 - c, l), send_sem=send_sems.at[i],
                    recv_sem=recv_sems.at[i], device_id=(x, y, 1 - c), device_id_type=MESH).wait_recv()
        for cp in cps:
            cp.wait_send()
        for cp in own:
            cp.wait()

    outs = pl.pallas_call(
        body, name=name, in_specs=[ANY] * nt, out_specs=[ANY] * nt,
        out_shape=[jax.ShapeDtypeStruct(full_shape(k, s), F32) for k, s in zip(kinds, shapes)],
        scratch_shapes=[pltpu.SemaphoreType.DMA((depth * nt,)), pltpu.SemaphoreType.DMA((depth * nt,)),
                        pltpu.SemaphoreType.DMA((depth * nt,))],
    )(*[reds[n] for n in names])
    return dict(zip(names, outs))


def gather_small(v, name):
    r, cols = v.shape

    def body(v_ref, out_ref, send_sems, recv_sems):
        x, y, c = _place()
        me = 4 * x + 2 * y + c
        out_ref[me] = v_ref[...]
        cps = []
        for rel in range(1, N_DEV):
            px = 1 - x if (rel >> 2) & 1 else x
            py = 1 - y if (rel >> 1) & 1 else y
            pc = 1 - c if rel & 1 else c
            cps.append(pltpu.make_async_remote_copy(
                src_ref=v_ref, dst_ref=out_ref.at[me], send_sem=send_sems.at[rel - 1],
                recv_sem=recv_sems.at[rel - 1], device_id=(px, py, pc), device_id_type=MESH))
        for cp in cps:
            cp.start()
        for cp in cps:
            cp.wait()

    return pl.pallas_call(
        body, name=name, in_specs=[pl.BlockSpec(memory_space=pltpu.VMEM)],
        out_specs=pl.BlockSpec(memory_space=pltpu.VMEM),
        out_shape=jax.ShapeDtypeStruct((N_DEV, r, cols), v.dtype),
        scratch_shapes=[pltpu.SemaphoreType.DMA((N_DEV - 1,)), pltpu.SemaphoreType.DMA((N_DEV - 1,))],
        compiler_params=pltpu.CompilerParams(vmem_limit_bytes=VMEM_LIMIT),
    )(v)


def _pad_rows(flat, row_align):
    n = flat.shape[-1]
    unit = PACK_COLS * row_align
    tot = -(-n // unit) * unit
    pad = [(0, 0)] * (flat.ndim - 1) + [(0, tot - n)]
    return jnp.pad(flat, pad)


def _pack_shards(ws):
    flat = jnp.concatenate([ws[n].astype(BF16).reshape(-1) for n in SHARDED])
    return _pad_rows(flat, PACK_ROW_ALIGN).reshape(-1, PACK_COLS)


def _unpack_full(gathered, shard_shapes):
    flat = gathered.reshape(N_CHIPS, -1)
    out, off = {}, 0
    for n in SHARDED:
        shp = shard_shapes[n]
        size = math.prod(shp)
        seg = flat[:, off:off + size].reshape((N_CHIPS,) + tuple(shp))
        off += size
        if n in ROW_SHARDED:
            out[n] = jnp.transpose(seg, (1, 0, 2, 3)).reshape(shp[0], N_CHIPS * shp[1], shp[2])
        else:
            out[n] = jnp.transpose(seg, (1, 2, 0, 3)).reshape(shp[0], shp[1], N_CHIPS * shp[2])
    return out


def _pack_grads(gfull, shard_shapes):
    segs = []
    for n in SHARDED:
        shp = shard_shapes[n]
        g = gfull[n]
        if n in ROW_SHARDED:
            seg = jnp.transpose(g.reshape(shp[0], N_CHIPS, shp[1], shp[2]), (1, 0, 2, 3))
        else:
            seg = jnp.transpose(g.reshape(shp[0], shp[1], N_CHIPS, shp[2]), (2, 0, 1, 3))
        segs.append(seg.reshape(N_CHIPS, -1))
    flat = jnp.concatenate(segs, axis=1)
    return _pad_rows(flat, PACK_ROW_ALIGN).reshape(N_CHIPS, -1, PACK_COLS)


def _unpack_shard_grads(red, shard_shapes):
    flat = red.reshape(-1)
    out, off = {}, 0
    for n in SHARDED:
        shp = shard_shapes[n]
        size = math.prod(shp)
        out[n] = flat[off:off + size].reshape(shp)
        off += size
    return out


def _pack_small(parts):
    flat = jnp.concatenate([p.astype(F32).reshape(-1) for p in parts])
    return _pad_rows(flat, 8).reshape(-1, PACK_COLS)


def _unpack_small(flat2d, shapes):
    flat = flat2d.reshape(-1)
    out, off = [], 0
    for shp in shapes:
        size = math.prod(shp)
        out.append(flat[off:off + size].reshape(shp))
        off += size
    return out


def _heads(a, bn, s, h):
    return jnp.transpose(a.reshape(bn, s, h, HEAD_DIM), (0, 2, 1, 3))


def _unheads(a):
    bn, h, s, d = a.shape
    return jnp.transpose(a, (0, 2, 1, 3)).reshape(bn * s, h * d)


def _reorder_w_in(w):
    d = w.shape[0]
    return jnp.concatenate([w[:, 2820:6916], w[:, 0:768], w[:, 772:1540], w[:, 1540:2052], w[:, 2052:2820],
                            w[:, 768:772], jnp.zeros((d, N_PROJ - 6916), w.dtype)], axis=1)


def _restore_dw_in(g):
    return jnp.concatenate([g[:, 4096:4864], g[:, 6912:6916], g[:, 4864:5632], g[:, 5632:6144], g[:, 6144:6912],
                            g[:, 0:4096]], axis=1)


def _ffn_fwd(h, g_pre, w_gu, w_down, g_post, tag):
    n = rms_fwd(h, g_pre, f"{tag}_rms")
    gu = _mm(n, w_gu, out_dtype=BF16, name=f"{tag}_mm_gu")
    a = swiglu_fwd(gu, f"{tag}_swiglu")
    f = _mm(a, w_down, out_dtype=F32, name=f"{tag}_mm_down")
    h_out = res_rms_fwd(h, f, g_post, 0.5, f"{tag}_res")
    return h_out, (h, n, gu, a, f)


def _ffn_bwd(dh, saved, g_pre, w_gu, w_down, g_post, tag, dw, n_gu, n_down):
    h, n, gu, a, f = saved
    df, dg_post = res_rms_bwd(f, g_post, dh, 0.5, f"{tag}_res_bwd")
    da = _mm(df, w_down, tb=True, out_dtype=BF16, name=f"{tag}_mm_da")
    dw(n_down, a, df, f"{tag}_mm_dwdown")
    dgu = swiglu_bwd(gu, da, f"{tag}_swiglu_bwd")
    dw(n_gu, n, dgu, f"{tag}_mm_dwgu")
    dn = _mm(dgu, w_gu, tb=True, out_dtype=BF16, name=f"{tag}_mm_dn")
    dh_in, dg_pre = rms_bwd(h, g_pre, dn, dh, f"{tag}_rms_bwd")
    return dh_in, dg_pre, dg_post


def kernel(x, p, ffn1_norm_pre, ffn1_w_gu, ffn1_w_down, ffn1_norm_post, mix_norm_pre, w_in, b_forget, b_gate, conv_short, conv_dw, conv_dw_bias, conv_ln_gain, conv_ln_bias, attn_sinks, rel_bias, w_br_a, w_br_b, w_br_c, w_br_d, w_o, mix_norm_post, ffn2_norm_pre, ffn2_w_gu, ffn2_w_down, ffn2_norm_post, ple_norm_gate, w_ple_gate, w_ple, ple_norm_post, loss_target, m_ffn1_norm_pre, m_ffn1_w_gu, m_ffn1_w_down, m_ffn1_norm_post, m_mix_norm_pre, m_w_in, m_b_forget, m_b_gate, m_conv_short, m_conv_dw, m_conv_dw_bias, m_conv_ln_gain, m_conv_ln_bias, m_attn_sinks, m_rel_bias, m_w_br_a, m_w_br_b, m_w_br_c, m_w_br_d, m_w_o, m_mix_norm_post, m_ffn2_norm_pre, m_ffn2_w_gu, m_ffn2_w_down, m_ffn2_norm_post, m_ple_norm_gate, m_w_ple_gate, m_w_ple, m_ple_norm_post, v_ffn1_norm_pre, v_ffn1_w_gu, v_ffn1_w_down, v_ffn1_norm_post, v_mix_norm_pre, v_w_in, v_b_forget, v_b_gate, v_conv_short, v_conv_dw, v_conv_dw_bias, v_conv_ln_gain, v_conv_ln_bias, v_attn_sinks, v_rel_bias, v_w_br_a, v_w_br_b, v_w_br_c, v_w_br_d, v_w_o, v_mix_norm_post, v_ffn2_norm_pre, v_ffn2_w_gu, v_ffn2_w_down, v_ffn2_norm_post, v_ple_norm_gate, v_w_ple_gate, v_w_ple, v_ple_norm_post):
    args = dict(locals())
    ws = {n: args[n] for n in WEIGHTS}
    ms = {n: args["m_" + n] for n in WEIGHTS}
    vs = {n: args["v_" + n] for n in WEIGHTS}
    return _step(x, p, loss_target, ws, ms, vs)


def _local(x, p, loss_target, ws, wf, w_short, w_dw):
    bn, s, d = x.shape
    t = bn * s
    depth = w_short.shape[0]

    def vec(a, i):
        return a[i].reshape(1, -1)

    bucket = _bucket_table()
    band_bias = band_bias_fwd(bucket, ws['rel_bias'], "band_bias")

    h = x.reshape(t, d)
    saved = []
    for i in range(depth):
        sv = {}
        h, sv['ffn1'] = _ffn_fwd(h, vec(ws['ffn1_norm_pre'], i), wf['ffn1_w_gu'][i], wf['ffn1_w_down'][i],
                                 vec(ws['ffn1_norm_post'], i), f"l{i}_ffn1")
        h1 = h
        u = rms_fwd(h1, vec(ws['mix_norm_pre'], i), f"l{i}_mix_rms")
        w_in_r = _reorder_w_in(wf['w_in'][i])
        proj = _mm(u, w_in_r, out_dtype=BF16, name=f"l{i}_mm_proj")
        bf = jnp.pad(vec(ws['b_forget'], i), ((0, 0), (0, 128 - A_HEADS)))
        cc = fgate_fwd(proj, bf, bn, s, f"l{i}_fgate")
        c4 = jnp.transpose(cc.reshape(bn, s, 128)[:, :, :A_HEADS], (0, 2, 1))
        c_col = c4[..., None]
        c_row = c4.reshape(bn, A_HEADS, s // FOX_T, 1, FOX_T)
        qa = _heads(proj[:, C_AQ:C_AQ + 256], bn, s, A_HEADS)
        ka = _heads(proj[:, C_AK:C_AK + 256], bn, s, A_HEADS)
        va = _heads(proj[:, C_AV:C_AV + 256], bn, s, A_HEADS)
        oa, lse_a = fox_fwd(qa, ka, va, c_col, c_row, f"l{i}_fox")
        ya = _unheads(oa)
        w_sh = jnp.pad(w_short[i], ((0, 8 - w_short.shape[1]), (0, 0)))
        w_cv = jnp.pad(w_dw[i], ((0, 32 - w_dw.shape[1]), (0, 0)))
        yb = convb_fwd(proj, w_sh[:3], bn, s, f"l{i}_convb")
        cvec = (vec(ws['conv_dw_bias'], i), vec(ws['conv_ln_gain'], i), vec(ws['conv_ln_bias'], i))
        yc = convc_fwd(proj, w_cv[:31], *cvec, bn, s, f"l{i}_convc")
        qd = _heads(proj[:, C_DQ:C_DQ + 512], bn, s, D_Q_HEADS)
        kd = _heads(proj[:, C_DK:C_DK + 128], bn, s, D_KV_HEADS)
        vd = _heads(proj[:, C_DV:C_DV + 128], bn, s, D_KV_HEADS)
        sinks = jnp.broadcast_to(ws['attn_sinks'][i].reshape(D_Q_HEADS, 1, 1), (D_Q_HEADS, QB, 1))
        od, lse_d = swa_fwd(qd, kd, vd, band_bias, sinks, f"l{i}_swa")
        yd = _unheads(od)
        ys = (ya, yb, yc, yd)
        wbr = (wf['w_br_a'][i], wf['w_br_b'][i], wf['w_br_c'][i], wf['w_br_d'][i])
        zs = [_mm(yk, wk, out_dtype=BF16, name=f"l{i}_mm_br{k}") for k, (yk, wk) in enumerate(zip(ys, wbr))]
        bgs = [ws['b_gate'][i, k * d:(k + 1) * d].reshape(1, d) for k in range(4)]
        merged = merge_fwd(proj, zs, bgs, f"l{i}_merge")
        mo = _mm(merged, wf['w_o'][i], out_dtype=F32, name=f"l{i}_mm_o")
        h2 = res_rms_fwd(h1, mo, vec(ws['mix_norm_post'], i), 1.0, f"l{i}_mix_res")
        sv['mix'] = dict(h1=h1, u=u, proj=proj, w_in_r=w_in_r, bf=bf, c_col=c_col, c_row=c_row, qa=qa, ka=ka, va=va,
                         oa=oa, lse_a=lse_a, w_sh=w_sh, w_cv=w_cv, cvec=cvec, qd=qd, kd=kd, vd=vd, sinks=sinks,
                         od=od, lse_d=lse_d, ys=ys, wbr=wbr, zs=zs, bgs=bgs, merged=merged, mo=mo)
        h, sv['ffn2'] = _ffn_fwd(h2, vec(ws['ffn2_norm_pre'], i), wf['ffn2_w_gu'][i], wf['ffn2_w_down'][i],
                                 vec(ws['ffn2_norm_post'], i), f"l{i}_ffn2")
        h3 = h
        ng = rms_fwd(h3, vec(ws['ple_norm_gate'], i), f"l{i}_ple_rms")
        pgl = _mm(ng, wf['w_ple_gate'][i], out_dtype=BF16, name=f"l{i}_mm_pgl")
        p_i = p[i].reshape(t, -1)
        pr = _mm(p_i, wf['w_ple'][i], out_dtype=F32, name=f"l{i}_mm_pr")
        h = ple_fwd(h3, pgl, pr, vec(ws['ple_norm_post'], i), f"l{i}_ple")
        sv['ple'] = dict(h3=h3, ng=ng, pgl=pgl, p_i=p_i, pr=pr)
        saved.append(sv)

    dh, loss_vec = loss_fwd_bwd(h, loss_target.reshape(t, d), "loss")
    loss_part = jnp.sum(loss_vec)

    gst = {}
    gwin = [None] * depth

    def dw(n, a, b, nm):
        gst[n] = _mm(a, b, ta=True, name=nm, stack=(gst.get(n), depth, i))

    gsmall = {n: [None] * depth for n in REPLICATED + CONV_SHARDED if n != 'rel_bias'}
    dbias_layers = []
    for i in reversed(range(depth)):
        sv = saved[i]
        pv = sv['ple']
        dpgl, dpr, dg = ple_bwd(pv['pgl'], pv['pr'], vec(ws['ple_norm_post'], i), dh, f"l{i}_ple_bwd")
        gsmall['ple_norm_post'][i] = dg
        dw('w_ple', pv['p_i'], dpr, f"l{i}_mm_dwple")
        dw('w_ple_gate', pv['ng'], dpgl, f"l{i}_mm_dwpg")
        dng = _mm(dpgl, wf['w_ple_gate'][i], tb=True, out_dtype=BF16, name=f"l{i}_mm_dng")
        dh, gsmall['ple_norm_gate'][i] = rms_bwd(pv['h3'], vec(ws['ple_norm_gate'], i), dng, dh,
                                                  f"l{i}_ple_rms_bwd")
        dh, gsmall['ffn2_norm_pre'][i], gsmall['ffn2_norm_post'][i] = _ffn_bwd(
            dh, sv['ffn2'], vec(ws['ffn2_norm_pre'], i), wf['ffn2_w_gu'][i], wf['ffn2_w_down'][i],
            vec(ws['ffn2_norm_post'], i), f"l{i}_ffn2", dw, 'ffn2_w_gu', 'ffn2_w_down')
        mv = sv['mix']
        dmo, gsmall['mix_norm_post'][i] = res_rms_bwd(mv['mo'], vec(ws['mix_norm_post'], i), dh, 1.0,
                                                      f"l{i}_mix_res_bwd")
        dw('w_o', mv['merged'], dmo, f"l{i}_mm_dwo")
        dmerged = _mm(dmo, wf['w_o'][i], tb=True, out_dtype=BF16, name=f"l{i}_mm_dmerged")
        mb = merge_bwd(mv['proj'], mv['zs'], mv['bgs'], dmerged, f"l{i}_merge_bwd")
        dgates, dzs, dbg = mb[0:4], mb[4:8], mb[8:12]
        gsmall['b_gate'][i] = jnp.concatenate(dbg, axis=1)
        dys = []
        for k, nm in enumerate(('w_br_a', 'w_br_b', 'w_br_c', 'w_br_d')):
            dw(nm, mv['ys'][k], dzs[k], f"l{i}_mm_dwbr{k}")
            dys.append(_mm(dzs[k], mv['wbr'][k], tb=True, out_dtype=BF16, name=f"l{i}_mm_dy{k}"))
        doa = _heads(dys[0], bn, s, A_HEADS)
        dqa, dka, dva, dck, dcq = fox_bwd(mv['qa'], mv['ka'], mv['va'], mv['c_col'], mv['c_row'], mv['oa'], mv['lse_a'],
                                     doa, f"l{i}_fox_bwd")
        dc = jnp.transpose(dck.reshape(bn, A_HEADS, s) + dcq.reshape(bn, A_HEADS, s), (0, 2, 1))
        dc = jnp.pad(dc, ((0, 0), (0, 0), (0, 128 - A_HEADS))).reshape(t, 128)
        daf, dbf = fgate_bwd(mv['proj'], mv['bf'], dc, bn, s, f"l{i}_fgate_bwd")
        gsmall['b_forget'][i] = dbf[:, :A_HEADS]
        dbg_, dcg_, dxb_, dwsh = convb_bwd(mv['proj'], mv['w_sh'][:3], dys[1], bn, s, f"l{i}_convb_bwd")
        gsmall['conv_short'][i] = dwsh[:3]
        dca, dcb, dwcv, dcbias, dlg, dlb = convc_bwd(mv['proj'], mv['w_cv'][:31], *mv['cvec'], dys[2], bn, s,
                                                     f"l{i}_convc_bwd")
        gsmall['conv_dw'][i] = dwcv[:31]
        gsmall['conv_dw_bias'][i] = dcbias
        gsmall['conv_ln_gain'][i] = dlg
        gsmall['conv_ln_bias'][i] = dlb
        dod = _heads(dys[3], bn, s, D_Q_HEADS)
        dqd, dkd, dvd, dbias, dsink = swa_bwd(mv['qd'], mv['kd'], mv['vd'], band_bias, mv['sinks'], mv['od'],
                                              mv['lse_d'], dod, f"l{i}_swa_bwd")
        dbias_layers.append(dbias)
        gsmall['attn_sinks'][i] = jnp.sum(dsink, axis=(1, 2))
        dproj = jnp.concatenate(
            list(dgates)
            + [_unheads(a).astype(BF16) for a in (dqa, dka, dva)]
            + [dbg_, dcg_, dxb_, dca, dcb]
            + [_unheads(a).astype(BF16) for a in (dqd, dkd, dvd)]
            + [daf, jnp.zeros((t, N_PROJ - C_AF - 128), BF16)], axis=1)
        dwin = _restore_dw_in(_mm(mv['u'], dproj, ta=True, name=f"l{i}_mm_dwin"))
        gwin[i] = jnp.transpose(dwin.reshape(d, N_CHIPS, -1), (1, 0, 2))
        du = _mm(dproj, mv['w_in_r'], tb=True, out_dtype=BF16, name=f"l{i}_mm_du")
        dh, gsmall['mix_norm_pre'][i] = rms_bwd(mv['h1'], vec(ws['mix_norm_pre'], i), du, dh, f"l{i}_mix_rms_bwd")
        dh, gsmall['ffn1_norm_pre'][i], gsmall['ffn1_norm_post'][i] = _ffn_bwd(
            dh, sv['ffn1'], vec(ws['ffn1_norm_pre'], i), wf['ffn1_w_gu'][i], wf['ffn1_w_down'][i],
            vec(ws['ffn1_norm_post'], i), f"l{i}_ffn1", dw, 'ffn1_w_gu', 'ffn1_w_down')
    grad_x = dh.reshape(bn, s, d)

    drel = band_bias_bwd(bucket, dbias_layers, "band_bias_bwd")
    gst['w_in'] = jnp.stack(gwin)
    full_shapes = {n: ws[n].shape for n in REPLICATED}
    full_shapes['conv_short'] = w_short.shape
    full_shapes['conv_dw'] = w_dw.shape
    gs = {n: jnp.stack([a.reshape(full_shapes[n][1:]) for a in gsmall[n]]) for n in gsmall}
    gs['rel_bias'] = jnp.transpose(drel[:, :, 0])
    return loss_part, grad_x, gst, gs


def _step(x, p, loss_target, ws, ms, vs):
    chip = 2 * lax.axis_index("x") + lax.axis_index("y")

    wf = gather_weights({n: ws[n].astype(BF16) for n in SHARDED}, "gather_weights")
    w_in_all = wf['w_in']
    wf['w_in'] = jnp.transpose(w_in_all, (1, 2, 0, 3)).reshape(w_in_all.shape[1], w_in_all.shape[2], -1)
    conv_shapes = [ws[n].shape for n in CONV_SHARDED]
    conv_all = gather_small(_pack_small([ws[n] for n in CONV_SHARDED]), "gather_conv")
    conv_full = []
    for idx, n in enumerate(CONV_SHARDED):
        per_chip = [_unpack_small(conv_all[2 * j], conv_shapes)[idx] for j in range(N_CHIPS)]
        conv_full.append(jnp.concatenate(per_chip, axis=-1))
    w_short, w_dw = conv_full

    loss_part, grad_x, gst, gs = _local(x, p, loss_target, {n: ws[n] for n in REPLICATED}, wf, w_short, w_dw)

    c_arr = lax.axis_index("c").astype(jnp.int32).reshape(1)
    recv = rs_sibling(gst, "rs_sibling")
    chip_sum = {n: rs_add(_kind(n), gst[n], recv[n], c_arr, f"rs_add_{n}") for n in SHARDED}
    slots = rs_chips(chip_sum, "rs_chips")
    red_half = {n: rs_sum(slots[n], f"rs_sum_{n}") for n in SHARDED}
    g_shard = rs_join(red_half, "rs_join")

    small_names = [n for n in REPLICATED + CONV_SHARDED]
    small_parts = [gs[n] for n in small_names]
    small_shapes = [g.shape for g in small_parts]
    small_parts.append(loss_part.reshape(1))
    small_shapes.append((1,))
    small_all = gather_small(_pack_small(small_parts), "gather_small")
    small_red = sum_slots(small_all, "small_sum")
    small_g = _unpack_small(small_red, small_shapes)
    loss = small_g[-1].reshape(())
    g_small = dict(zip(small_names, small_g[:-1]))

    grads = {}
    for n in WEIGHTS:
        if n in SHARDED:
            grads[n] = g_shard[n]
        elif n in CONV_SHARDED:
            wdt = ws[n].shape[-1]
            grads[n] = lax.dynamic_slice_in_dim(g_small[n], chip * wdt, wdt, axis=2)
        else:
            grads[n] = g_small[n]

    deltas, new_m, new_v = {}, {}, {}
    small_upd = [n for n in WEIGHTS if n not in SHARDED]
    for n in SHARDED:
        deltas[n], new_m[n], new_v[n] = adamw(ws[n], grads[n], ms[n], vs[n], f"adamw_{n}")
    shapes_u = [ws[n].shape for n in small_upd]
    packs = [_pack_small([src[n] for n in small_upd]) for src in (ws, grads, ms, vs)]
    upd = adamw(*packs, "adamw_small")
    for res, dst in zip(upd, (deltas, new_m, new_v)):
        for n, a in zip(small_upd, _unpack_small(res, shapes_u)):
            dst[n] = a

    return (loss, grad_x, *[grads[n] for n in WEIGHTS], *[deltas[n] for n in WEIGHTS],
            *[new_m[n] for n in WEIGHTS], *[new_v[n] for n in WEIGHTS])
```

```python
import functools
import math

import jax
import jax.numpy as jnp
from jax import lax
from jax.experimental import pallas as pl
from jax.experimental.pallas import tpu as pltpu

F32 = jnp.float32
BF16 = jnp.bfloat16
MESH = pl.DeviceIdType.MESH

D_MODEL = 1024
DEPTH = 4
HEAD_DIM = 64
A_HEADS = 4
D_Q_HEADS = 8
D_KV_HEADS = 2
D_GROUP = 4
WINDOW = 128
QB = 128
REL_BUCKETS = 32
REL_MAX_DIST = 128
D_FF = 2816
EPS = 1e-6
NEG = -1e30
SCALE = HEAD_DIM ** -0.5
N_CHIPS = 4
N_DEV = 8

ADAM_LR = 0.001
ADAM_B1 = 0.9
ADAM_B2 = 0.999
ADAM_EPS = 1e-08
ADAM_WD = 0.01
ADAM_STEP = 10

C_GATE = 0
C_AQ, C_AK, C_AV = 4096, 4352, 4608
C_BG, C_CG, C_XB = 4864, 5120, 5376
C_CA, C_CB = 5632, 5888
C_DQ, C_DK, C_DV = 6144, 6656, 6784
C_AF = 6912
N_PROJ = 7168

VMEM_LIMIT = 56 * 1024 * 1024
PACK_COLS = 1024
PACK_ROW_ALIGN = 1024

SHARDED = ('ffn1_w_gu', 'ffn1_w_down', 'w_in', 'w_br_a', 'w_br_b', 'w_br_c', 'w_br_d', 'w_o',
           'ffn2_w_gu', 'ffn2_w_down', 'w_ple_gate', 'w_ple')
ROW_SHARDED = ('ffn1_w_down', 'w_o', 'ffn2_w_down', 'w_ple_gate')
CONV_SHARDED = ('conv_short', 'conv_dw')
REPLICATED = ('ffn1_norm_pre', 'ffn1_norm_post', 'mix_norm_pre', 'b_forget', 'b_gate', 'conv_dw_bias',
              'conv_ln_gain', 'conv_ln_bias', 'attn_sinks', 'rel_bias', 'mix_norm_post', 'ffn2_norm_pre',
              'ffn2_norm_post', 'ple_norm_gate', 'ple_norm_post')
WEIGHTS = ('ffn1_norm_pre', 'ffn1_w_gu', 'ffn1_w_down', 'ffn1_norm_post', 'mix_norm_pre', 'w_in', 'b_forget',
           'b_gate', 'conv_short', 'conv_dw', 'conv_dw_bias', 'conv_ln_gain', 'conv_ln_bias', 'attn_sinks',
           'rel_bias', 'w_br_a', 'w_br_b', 'w_br_c', 'w_br_d', 'w_o', 'mix_norm_post', 'ffn2_norm_pre',
           'ffn2_w_gu', 'ffn2_w_down', 'ffn2_norm_post', 'ple_norm_gate', 'w_ple_gate', 'w_ple', 'ple_norm_post')


def _cparams(sem=None):
    return pltpu.CompilerParams(dimension_semantics=sem, vmem_limit_bytes=VMEM_LIMIT)


def _pick(dim, cands):
    for c in cands:
        if dim % c == 0:
            return c
    return dim


MM_VMEM_BUDGET = 40 * 1024 * 1024
MXU_FLOPS = 9.0e14
HBM_BYTES_PER_S = 3.0e12
GRID_STEP_S = 0.35e-6


def _divisors(dim, cands):
    out = [c for c in cands if c <= dim and dim % c == 0]
    return out or [dim]


def _mm_tiles(m, n, k, ab, bb, ob):
    best = None
    for tm in _divisors(m, (2048, 1408, 1024, 512, 256, 128)):
        for tn in _divisors(n, (2816, 2048, 1792, 1408, 1024, 512, 256, 128)):
            for tk in _divisors(k, (k if k <= 2048 else 2816, 2816, 2048, 1792, 1408, 1024, 512, 256, 128)):
                nk = k // tk
                vmem = 2 * (tm * tk * ab + tk * tn * bb + tm * tn * ob) + tm * tn * 4 * (2 if nk > 1 else 1)
                if vmem > MM_VMEM_BUDGET:
                    continue
                steps = (m // tm) * (n // tn) * nk
                a_bytes = m * k * ab * (1 if nk == 1 else n // tn)
                b_bytes = k * n * bb * (1 if (nk == 1 and n == tn) else m // tm)
                mem = (a_bytes + b_bytes + m * n * ob) / HBM_BYTES_PER_S
                acc = steps * tm * tn * 1.5e-12 if nk > 1 else 0.0
                cost = steps * GRID_STEP_S + max(2.0 * m * n * k / MXU_FLOPS, mem) + acc
                if best is None or cost < best[0]:
                    best = (cost, tm, tn, tk)
    assert best is not None, (m, n, k)
    return best[1:]


def _mm(a, b, *, ta=False, tb=False, out_dtype=F32, name="mm", stack=None):
    if ta:
        kdim, m = a.shape
    else:
        m, kdim = a.shape
    if tb:
        n, kb = b.shape
    else:
        kb, n = b.shape
    assert kb == kdim, (a.shape, b.shape, ta, tb)
    tm, tn, tk = _mm_tiles(m, n, kdim, a.dtype.itemsize, b.dtype.itemsize, jnp.dtype(out_dtype).itemsize)
    nk = kdim // tk
    dims = (((0,) if ta else (1,), (1,) if tb else (0,)), ((), ()))

    def dot(a_ref, b_ref):
        return lax.dot_general(a_ref[...].astype(BF16), b_ref[...].astype(BF16), dims, preferred_element_type=F32)

    if nk == 1:
        def body(a_ref, b_ref, *rest):
            o_ref = rest[-1]
            o_ref[...] = dot(a_ref, b_ref).astype(o_ref.dtype)
        scratch = []
    else:
        def body(a_ref, b_ref, *rest):
            o_ref, acc_ref = rest[-2], rest[-1]
            k = pl.program_id(2)

            @pl.when(k == 0)
            def _():
                acc_ref[...] = dot(a_ref, b_ref)

            @pl.when(jnp.logical_and(k > 0, k < nk - 1))
            def _():
                acc_ref[...] += dot(a_ref, b_ref)

            @pl.when(k == nk - 1)
            def _():
                o_ref[...] = (acc_ref[...] + dot(a_ref, b_ref)).astype(o_ref.dtype)
        scratch = [pltpu.VMEM((tm, tn), F32)]

    a_spec = pl.BlockSpec((tk, tm), lambda i, j, k: (k, i)) if ta else pl.BlockSpec((tm, tk), lambda i, j, k: (i, k))
    b_spec = pl.BlockSpec((tn, tk), lambda i, j, k: (j, k)) if tb else pl.BlockSpec((tk, tn), lambda i, j, k: (k, j))
    in_specs, operands, aliases = [a_spec, b_spec], [a, b], {}
    if stack is None:
        out_spec = pl.BlockSpec((tm, tn), lambda i, j, k: (i, j))
        out_shape = jax.ShapeDtypeStruct((m, n), out_dtype)
    else:
        buf, depth, layer = stack
        out_spec = pl.BlockSpec((None, tm, tn), lambda i, j, k: (layer, i, j))
        out_shape = jax.ShapeDtypeStruct((depth, m, n), out_dtype)
        if buf is not None:
            in_specs.append(pl.BlockSpec(memory_space=pl.ANY))
            operands.append(buf)
            aliases = {2: 0}
    return pl.pallas_call(
        body, name=name, grid=(m // tm, n // tn, nk),
        in_specs=in_specs, out_specs=out_spec, out_shape=out_shape, scratch_shapes=scratch,
        input_output_aliases=aliases,
        compiler_params=_cparams(("parallel", "parallel", "arbitrary")),
    )(*operands)


def _rowwise(fn, rows, params, outs, pouts=(), *, tm=256, name="rowwise"):
    t = rows[0][0].shape[0]
    assert t % tm == 0
    n_r, n_p, n_o, n_po = len(rows), len(params), len(outs), len(pouts)

    def body(*refs):
        r_refs = refs[:n_r]
        p_refs = refs[n_r:n_r + n_p]
        o_refs = refs[n_r + n_p:n_r + n_p + n_o]
        po_refs = refs[n_r + n_p + n_o:]
        res = fn(*[r[...] for r in r_refs], *[p[...] for p in p_refs])
        if not isinstance(res, (tuple, list)):
            res = (res,)
        assert len(res) == n_o + n_po, (len(res), n_o, n_po)
        for o, val in zip(o_refs, res[:n_o]):
            o[...] = val.astype(o.dtype)
        if n_po:
            first = pl.program_id(0) == 0

            @pl.when(first)
            def _():
                for o, val in zip(po_refs, res[n_o:]):
                    o[...] = val.astype(F32)

            @pl.when(jnp.logical_not(first))
            def _():
                for o, val in zip(po_refs, res[n_o:]):
                    o[...] += val.astype(F32)

    in_specs = [pl.BlockSpec((tm, w), functools.partial(lambda i, cb: (i, cb), cb=cb)) for (_, w, cb) in rows]
    in_specs += [pl.BlockSpec(p.shape, lambda i: (0, 0)) for p in params]
    out_specs = [pl.BlockSpec((tm, w), lambda i: (i, 0)) for (w, _) in outs]
    out_specs += [pl.BlockSpec((1, w), lambda i: (0, 0)) for w in pouts]
    out_shape = [jax.ShapeDtypeStruct((t, w), dt) for (w, dt) in outs]
    out_shape += [jax.ShapeDtypeStruct((1, w), F32) for w in pouts]
    res = pl.pallas_call(
        body, name=name, grid=(t // tm,), in_specs=in_specs, out_specs=out_specs, out_shape=out_shape,
        compiler_params=_cparams(("arbitrary",)),
    )(*[r[0] for r in rows], *params)
    return res


def _full(a):
    return (a, a.shape[1], 0)


def _rms(x, g):
    x = x.astype(F32)
    return x * lax.rsqrt(jnp.mean(x * x, axis=-1, keepdims=True) + EPS) * g


def _sum0(v):
    return jnp.sum(v, axis=0, keepdims=True)


def rms_fwd(h, g, name):
    return _rowwise(lambda x, gg: _rms(x, gg), [_full(h)], [g], [(h.shape[1], BF16)], name=name)[0]


def rms_bwd(h, g, dn, dres, name):
    def fn(x, d, r, gg):
        _, vjp = jax.vjp(_rms, x, gg)
        dx, dg = vjp(d.astype(F32))
        return dx + r, dg
    w = h.shape[1]
    return _rowwise(fn, [_full(h), _full(dn), _full(dres)], [g], [(w, F32)], [w], name=name)


def res_rms_fwd(h, f, g, coef, name):
    return _rowwise(lambda x, y, gg: x + coef * _rms(y, gg), [_full(h), _full(f)], [g], [(h.shape[1], F32)],
                    name=name)[0]


def res_rms_bwd(f, g, dh, coef, name):
    def fn(y, d, gg):
        _, vjp = jax.vjp(lambda a, b: coef * _rms(a, b), y, gg)
        dy, dg = vjp(d)
        return dy, dg
    w = f.shape[1]
    return _rowwise(fn, [_full(f), _full(dh)], [g], [(w, BF16)], [w], name=name)


def swiglu_fwd(gu, name):
    f = gu.shape[1] // 2

    def fn(gate, up):
        gate = gate.astype(F32)
        return gate * jax.nn.sigmoid(gate) * up.astype(F32)
    return _rowwise(fn, [(gu, f, 0), (gu, f, 1)], [], [(f, BF16)], name=name)[0]


def swiglu_bwd(gu, da, name):
    t, f2 = gu.shape
    f = f2 // 2
    tm = 256

    def body(gate_ref, up_ref, da_ref, o_ref):
        gate = gate_ref[...].astype(F32)
        up = up_ref[...].astype(F32)
        d = da_ref[...].astype(F32)
        sg = jax.nn.sigmoid(gate)
        silu = gate * sg
        o_ref[:, :f] = (d * up * (sg + silu * (1.0 - sg))).astype(o_ref.dtype)
        o_ref[:, f:] = (d * silu).astype(o_ref.dtype)

    return pl.pallas_call(
        body, name=name, grid=(t // tm,),
        in_specs=[pl.BlockSpec((tm, f), lambda i: (i, 0)), pl.BlockSpec((tm, f), lambda i: (i, 1)),
                  pl.BlockSpec((tm, f), lambda i: (i, 0))],
        out_specs=pl.BlockSpec((tm, f2), lambda i: (i, 0)),
        out_shape=jax.ShapeDtypeStruct((t, f2), BF16),
        compiler_params=_cparams(("parallel",)),
    )(gu, gu, da)


def _merge(g0, g1, g2, g3, z0, z1, z2, z3, b0, b1, b2, b3):
    acc = jax.nn.sigmoid(g0.astype(F32) + b0) * z0.astype(F32)
    acc += jax.nn.sigmoid(g1.astype(F32) + b1) * z1.astype(F32)
    acc += jax.nn.sigmoid(g2.astype(F32) + b2) * z2.astype(F32)
    acc += jax.nn.sigmoid(g3.astype(F32) + b3) * z3.astype(F32)
    return acc


def merge_fwd(proj, zs, bs, name):
    rows = [(proj, D_MODEL, k) for k in range(4)] + [_full(z) for z in zs]
    return _rowwise(_merge, rows, list(bs), [(D_MODEL, BF16)], name=name)[0]


def merge_bwd(proj, zs, bs, dmerged, name):
    def fn(*args):
        d = args[8].astype(F32)
        prim = args[:8] + args[9:]
        _, vjp = jax.vjp(_merge, *prim)
        return vjp(d)
    rows = [(proj, D_MODEL, k) for k in range(4)] + [_full(z) for z in zs] + [_full(dmerged)]
    outs = [(D_MODEL, BF16)] * 8
    return _rowwise(fn, rows, list(bs), outs, [D_MODEL] * 4, name=name)


def _ple(pgl, pr, g):
    return jax.nn.sigmoid(pgl.astype(F32)) * _rms(pr, g)


def ple_fwd(h, pgl, pr, g, name):
    return _rowwise(lambda x, a, b, gg: x + _ple(a, b, gg), [_full(h), _full(pgl), _full(pr)], [g],
                    [(D_MODEL, F32)], name=name)[0]


def ple_bwd(pgl, pr, g, dh, name):
    def fn(a, b, d, gg):
        _, vjp = jax.vjp(_ple, a, b, gg)
        return vjp(d)
    return _rowwise(fn, [_full(pgl), _full(pr), _full(dh)], [g], [(D_MODEL, BF16), (D_MODEL, BF16)], [D_MODEL],
                    name=name)


def loss_fwd_bwd(y, target, name):
    def fn(a, b):
        err = a - b
        return err * (1.0 / D_MODEL), _sum0(err * err) * (0.5 / D_MODEL)
    return _rowwise(fn, [_full(y), _full(target)], [], [(D_MODEL, F32)], [D_MODEL], name=name)


def _shift_down(x, d, row):
    if d == 0:
        return x
    return jnp.where(row >= d, pltpu.roll(x, d, 0), 0.0)


def _shift_up(x, d, row):
    if d == 0:
        return x
    s = x.shape[0]
    return jnp.where(row < s - d, pltpu.roll(x, s - d, 0), 0.0)


def fgate_fwd(proj, bf, bn, s, name):
    def body(a_ref, b_ref, o_ref):
        x = a_ref[...].astype(F32) + b_ref[...]
        c = jnp.minimum(x, 0.0) - jnp.log(1.0 + jnp.exp(-jnp.abs(x)))
        row = lax.broadcasted_iota(jnp.int32, c.shape, 0)
        sh = 1
        while sh < s:
            c = c + _shift_down(c, sh, row)
            sh *= 2
        o_ref[...] = c

    return pl.pallas_call(
        body, name=name, grid=(bn,),
        in_specs=[pl.BlockSpec((s, 128), lambda b: (b, C_AF // 128)), pl.BlockSpec((1, 128), lambda b: (0, 0))],
        out_specs=pl.BlockSpec((s, 128), lambda b: (b, 0)),
        out_shape=jax.ShapeDtypeStruct((bn * s, 128), F32),
        compiler_params=_cparams(("parallel",)),
    )(proj, bf)


def fgate_bwd(proj, bf, dc, bn, s, name):
    def body(a_ref, b_ref, dc_ref, da_ref, db_ref):
        x = a_ref[...].astype(F32) + b_ref[...]
        d = dc_ref[...]
        row = lax.broadcasted_iota(jnp.int32, d.shape, 0)
        sh = 1
        while sh < s:
            d = d + _shift_up(d, sh, row)
            sh *= 2
        da = d * jax.nn.sigmoid(-x)
        da_ref[...] = da.astype(da_ref.dtype)
        first = pl.program_id(0) == 0

        @pl.when(first)
        def _():
            db_ref[...] = _sum0(da)

        @pl.when(jnp.logical_not(first))
        def _():
            db_ref[...] += _sum0(da)

    return pl.pallas_call(
        body, name=name, grid=(bn,),
        in_specs=[pl.BlockSpec((s, 128), lambda b: (b, C_AF // 128)), pl.BlockSpec((1, 128), lambda b: (0, 0)),
                  pl.BlockSpec((s, 128), lambda b: (b, 0))],
        out_specs=[pl.BlockSpec((s, 128), lambda b: (b, 0)), pl.BlockSpec((1, 128), lambda b: (0, 0))],
        out_shape=[jax.ShapeDtypeStruct((bn * s, 128), BF16), jax.ShapeDtypeStruct((1, 128), F32)],
        compiler_params=_cparams(("arbitrary",)),
    )(proj, bf, dc)


FOX_T = 256


def _fox_scores(q, k, cq, ck, j, i):
    t = FOX_T
    s = lax.dot_general(q, k, (((1,), (1,)), ((), ())), preferred_element_type=F32) * SCALE
    qpos = j * t + lax.broadcasted_iota(jnp.int32, (t, t), 0)
    kpos = i * t + lax.broadcasted_iota(jnp.int32, (t, t), 1)
    return jnp.where(qpos >= kpos, s + (cq - ck), NEG)


def fox_fwd(q, k, v, c_col, c_row, name):
    bn, h, s, d = q.shape
    t = FOX_T
    nq = s // t

    def body(q_ref, k_ref, v_ref, cq_ref, ck_ref, o_ref, lse_ref):
        j = pl.program_id(2)
        qv = q_ref[...]
        cq = cq_ref[...]

        def step(i, carry):
            m, l, acc = carry
            ks = pl.multiple_of(i * t, t)
            kc = k_ref[pl.ds(ks, t), :]
            vc = v_ref[pl.ds(ks, t), :]
            sc = _fox_scores(qv, kc, cq, ck_ref[i], j, i)
            m_new = jnp.maximum(m, jnp.max(sc, axis=-1, keepdims=True))
            alpha = jnp.exp(m - m_new)
            p = jnp.exp(sc - m_new)
            l = alpha * l + jnp.sum(p, axis=-1, keepdims=True)
            acc = alpha * acc + jnp.dot(p.astype(BF16), vc, preferred_element_type=F32)
            return m_new, l, acc

        init = (jnp.full((t, 1), NEG, F32), jnp.zeros((t, 1), F32), jnp.zeros((t, d), F32))
        m, l, acc = lax.fori_loop(0, j + 1, step, init)
        o_ref[...] = (acc / l).astype(o_ref.dtype)
        lse_ref[...] = m + jnp.log(l)

    blk_q = pl.BlockSpec((None, None, t, d), lambda b, hh, j: (b, hh, j, 0))
    blk_kv = pl.BlockSpec((None, None, s, d), lambda b, hh, j: (b, hh, 0, 0))
    blk_c1 = pl.BlockSpec((None, None, t, 1), lambda b, hh, j: (b, hh, j, 0))
    blk_cr = pl.BlockSpec((None, None, nq, 1, t), lambda b, hh, j: (b, hh, 0, 0, 0))
    return pl.pallas_call(
        body, name=name, grid=(bn, h, nq),
        in_specs=[blk_q, blk_kv, blk_kv, blk_c1, blk_cr],
        out_specs=[blk_q, blk_c1],
        out_shape=[jax.ShapeDtypeStruct((bn, h, s, d), F32), jax.ShapeDtypeStruct((bn, h, s, 1), F32)],
        compiler_params=_cparams(("parallel", "parallel", "arbitrary")),
    )(q, k, v, c_col, c_row)


def fox_bwd(q, k, v, c_col, c_row, o, lse, do, name):
    bn, h, s, d = q.shape
    t = FOX_T
    nq = s // t

    def body(q_ref, k_ref, v_ref, cq_ref, ck_ref, o_ref, lse_ref, do_ref, dq_ref, dk_ref, dv_ref, dck_ref,
             dcq_ref):
        j = pl.program_id(2)

        @pl.when(j == 0)
        def _():
            dk_ref[...] = jnp.zeros_like(dk_ref)
            dv_ref[...] = jnp.zeros_like(dv_ref)
            dck_ref[...] = jnp.zeros_like(dck_ref)

        qv = q_ref[...]
        cq = cq_ref[...]
        dov = do_ref[...]
        lse = lse_ref[...]
        delta = jnp.sum(dov.astype(F32) * o_ref[...].astype(F32), axis=-1, keepdims=True)

        def step(i, carry):
            dq, dcq = carry
            ks = pl.multiple_of(i * t, t)
            kc = k_ref[pl.ds(ks, t), :]
            vc = v_ref[pl.ds(ks, t), :]
            sc = _fox_scores(qv, kc, cq, ck_ref[i], j, i)
            p = jnp.exp(sc - lse)
            dp = lax.dot_general(dov, vc, (((1,), (1,)), ((), ())), preferred_element_type=F32)
            ds = p * (dp - delta)
            dsb = ds.astype(BF16)
            dq = dq + jnp.dot(dsb, kc, preferred_element_type=F32) * SCALE
            dk_ref[pl.ds(ks, t), :] += lax.dot_general(dsb, qv, (((0,), (0,)), ((), ())),
                                                       preferred_element_type=F32) * SCALE
            dv_ref[pl.ds(ks, t), :] += lax.dot_general(p.astype(BF16), dov, (((0,), (0,)), ((), ())),
                                                       preferred_element_type=F32)
            dck_ref[i] += -_sum0(ds)
            return dq, dcq + jnp.sum(ds, axis=-1, keepdims=True)

        dq, dcq = lax.fori_loop(0, j + 1, step, (jnp.zeros((t, d), F32), jnp.zeros((t, 1), F32)))
        dq_ref[...] = dq
        dcq_ref[...] = dcq

    blk_q = pl.BlockSpec((None, None, t, d), lambda b, hh, j: (b, hh, j, 0))
    blk_kv = pl.BlockSpec((None, None, s, d), lambda b, hh, j: (b, hh, 0, 0))
    blk_c1 = pl.BlockSpec((None, None, t, 1), lambda b, hh, j: (b, hh, j, 0))
    blk_cr = pl.BlockSpec((None, None, nq, 1, t), lambda b, hh, j: (b, hh, 0, 0, 0))
    return pl.pallas_call(
        body, name=name, grid=(bn, h, nq),
        in_specs=[blk_q, blk_kv, blk_kv, blk_c1, blk_cr, blk_q, blk_c1, blk_q],
        out_specs=[blk_q, blk_kv, blk_kv, blk_cr, blk_c1],
        out_shape=[jax.ShapeDtypeStruct((bn, h, s, d), F32), jax.ShapeDtypeStruct((bn, h, s, d), F32),
                   jax.ShapeDtypeStruct((bn, h, s, d), F32), jax.ShapeDtypeStruct((bn, h, nq, 1, t), F32),
                   jax.ShapeDtypeStruct((bn, h, s, 1), F32)],
        compiler_params=_cparams(("parallel", "parallel", "arbitrary")),
    )(q, k, v, c_col, c_row, o, lse, do)


def _swa_valid(n):
    qi = lax.broadcasted_iota(jnp.int32, (QB, 2 * QB), 0)
    kj = lax.broadcasted_iota(jnp.int32, (QB, 2 * QB), 1)
    dist = qi + QB - kj
    return (dist >= 0) & (dist < WINDOW) & ((kj >= QB) | (n > 0))


def _swa_band(ref, n):
    qs = pl.multiple_of(n * QB, QB)
    ps = pl.multiple_of(jnp.maximum(n - 1, 0) * QB, QB)
    return jnp.concatenate([ref[pl.ds(ps, QB), :], ref[pl.ds(qs, QB), :]], axis=0), qs, ps


def swa_fwd(q, k, v, bias, sinks, name):
    bn, hq, s, d = q.shape
    nb = s // QB

    def body(q_ref, k_ref, v_ref, b_ref, s_ref, o_ref, lse_ref):
        def step(n, _):
            kb, qs, _ps = _swa_band(k_ref, n)
            vb, _, _ = _swa_band(v_ref, n)
            valid = _swa_valid(n)
            for g in range(D_GROUP):
                qg = q_ref[g, pl.ds(qs, QB), :]
                sc = lax.dot_general(qg, kb, (((1,), (1,)), ((), ())), preferred_element_type=F32) * SCALE
                sc = jnp.where(valid, sc + b_ref[g], NEG)
                sink = s_ref[g]
                m = jnp.maximum(jnp.max(sc, axis=-1, keepdims=True), sink)
                e = jnp.exp(sc - m)
                z = jnp.sum(e, axis=-1, keepdims=True) + jnp.exp(sink - m)
                p = e / z
                o_ref[g, pl.ds(qs, QB), :] = jnp.dot(p.astype(BF16), vb, preferred_element_type=F32
                                                     ).astype(o_ref.dtype)
                lse_ref[g, pl.ds(qs, QB), :] = m + jnp.log(z)
            return 0

        lax.fori_loop(0, nb, step, 0)

    blk_q = pl.BlockSpec((None, D_GROUP, s, d), lambda b, kh: (b, kh, 0, 0))
    blk_kv = pl.BlockSpec((None, None, s, d), lambda b, kh: (b, kh, 0, 0))
    blk_l = pl.BlockSpec((None, D_GROUP, s, 1), lambda b, kh: (b, kh, 0, 0))
    return pl.pallas_call(
        body, name=name, grid=(bn, D_KV_HEADS),
        in_specs=[blk_q, blk_kv, blk_kv, pl.BlockSpec((D_GROUP, QB, 2 * QB), lambda b, kh: (kh, 0, 0)),
                  pl.BlockSpec((D_GROUP, QB, 1), lambda b, kh: (kh, 0, 0))],
        out_specs=[blk_q, blk_l],
        out_shape=[jax.ShapeDtypeStruct((bn, hq, s, d), BF16), jax.ShapeDtypeStruct((bn, hq, s, 1), F32)],
        compiler_params=_cparams(("parallel", "parallel")),
    )(q, k, v, bias, sinks)


def swa_bwd(q, k, v, bias, sinks, o, lse, do, name):
    bn, hq, s, d = q.shape
    nb = s // QB

    def body(q_ref, k_ref, v_ref, b_ref, s_ref, o_ref, lse_ref, do_ref, dq_ref, dk_ref, dv_ref, db_ref, dsk_ref):
        @pl.when(pl.program_id(1) == 0)
        def _():
            db_ref[...] = jnp.zeros_like(db_ref)
            dsk_ref[...] = jnp.zeros_like(dsk_ref)

        dk_ref[...] = jnp.zeros_like(dk_ref)
        dv_ref[...] = jnp.zeros_like(dv_ref)

        def step(n, _):
            kb, qs, ps = _swa_band(k_ref, n)
            vb, _, _ = _swa_band(v_ref, n)
            valid = _swa_valid(n)
            dkb = jnp.zeros((2 * QB, d), F32)
            dvb = jnp.zeros((2 * QB, d), F32)
            for g in range(D_GROUP):
                qg = q_ref[g, pl.ds(qs, QB), :]
                dog = do_ref[g, pl.ds(qs, QB), :]
                og = o_ref[g, pl.ds(qs, QB), :]
                lse = lse_ref[g, pl.ds(qs, QB), :]
                sc = lax.dot_general(qg, kb, (((1,), (1,)), ((), ())), preferred_element_type=F32) * SCALE
                sc = jnp.where(valid, sc + b_ref[g], NEG)
                p = jnp.exp(sc - lse)
                delta = jnp.sum(dog.astype(F32) * og.astype(F32), axis=-1, keepdims=True)
                dp = lax.dot_general(dog, vb, (((1,), (1,)), ((), ())), preferred_element_type=F32)
                ds = p * (dp - delta)
                dsb = ds.astype(BF16)
                dq_ref[g, pl.ds(qs, QB), :] = jnp.dot(dsb, kb, preferred_element_type=F32) * SCALE
                dkb = dkb + lax.dot_general(dsb, qg, (((0,), (0,)), ((), ())), preferred_element_type=F32) * SCALE
                dvb = dvb + lax.dot_general(p.astype(BF16), dog, (((0,), (0,)), ((), ())),
                                            preferred_element_type=F32)
                db_ref[g] += ds
                dsk_ref[g] += -jnp.exp(s_ref[g] - lse) * delta
            dk_ref[pl.ds(ps, QB), :] += dkb[:QB]
            dk_ref[pl.ds(qs, QB), :] += dkb[QB:]
            dv_ref[pl.ds(ps, QB), :] += dvb[:QB]
            dv_ref[pl.ds(qs, QB), :] += dvb[QB:]
            return 0

        lax.fori_loop(0, nb, step, 0)

    blk_q = pl.BlockSpec((None, D_GROUP, s, d), lambda kh, b: (b, kh, 0, 0))
    blk_kv = pl.BlockSpec((None, None, s, d), lambda kh, b: (b, kh, 0, 0))
    blk_l = pl.BlockSpec((None, D_GROUP, s, 1), lambda kh, b: (b, kh, 0, 0))
    blk_b = pl.BlockSpec((D_GROUP, QB, 2 * QB), lambda kh, b: (kh, 0, 0))
    blk_s = pl.BlockSpec((D_GROUP, QB, 1), lambda kh, b: (kh, 0, 0))
    return pl.pallas_call(
        body, name=name, grid=(D_KV_HEADS, bn),
        in_specs=[blk_q, blk_kv, blk_kv, blk_b, blk_s, blk_q, blk_l, blk_q],
        out_specs=[blk_q, blk_kv, blk_kv, blk_b, blk_s],
        out_shape=[jax.ShapeDtypeStruct((bn, hq, s, d), F32), jax.ShapeDtypeStruct((bn, D_KV_HEADS, s, d), F32),
                   jax.ShapeDtypeStruct((bn, D_KV_HEADS, s, d), F32),
                   jax.ShapeDtypeStruct((hq, QB, 2 * QB), F32), jax.ShapeDtypeStruct((hq, QB, 1), F32)],
        compiler_params=_cparams(("parallel", "arbitrary")),
    )(q, k, v, bias, sinks, o, lse, do)


def _sel(nh, width):
    r = lax.broadcasted_iota(jnp.int32, (width, HEAD_DIM), 0)
    c = lax.broadcasted_iota(jnp.int32, (width, HEAD_DIM), 1)
    return [(r == c + HEAD_DIM * h).astype(BF16) for h in range(nh)]


def _pick_head(x, e):
    return jnp.dot(x, e, preferred_element_type=F32).astype(BF16)


def _place_head(x, e):
    return lax.dot_general(x.astype(BF16), e, (((1,), (1,)), ((), ())), preferred_element_type=F32)


def fox2_fwd(proj, c_col, c_row, bn, s, name):
    t = FOX_T
    nq = s // t
    nh, d = A_HEADS, HEAD_DIM

    def body(q_ref, k_ref, v_ref, cq_ref, ck_ref, y_ref, o_ref, lse_ref, kh_ref, vh_ref):
        j = pl.program_id(1)
        es = _sel(nh, 256)

        @pl.when(j == 0)
        def _():
            for h in range(nh):
                kh_ref[h] = _pick_head(k_ref[...], es[h])
                vh_ref[h] = _pick_head(v_ref[...], es[h])

        q4 = q_ref[...]
        qs = [_pick_head(q4, es[h]) for h in range(nh)]
        cqs = [cq_ref[h] for h in range(nh)]

        def step(i, carry):
            ks = pl.multiple_of(i * t, t)
            out = []
            for h in range(nh):
                m, l, acc = carry[h]
                sc = _fox_scores(qs[h], kh_ref[h, pl.ds(ks, t), :], cqs[h], ck_ref[h, i], j, i)
                m_new = jnp.maximum(m, jnp.max(sc, axis=-1, keepdims=True))
                alpha = jnp.exp(m - m_new)
                p = jnp.exp(sc - m_new)
                l = alpha * l + jnp.sum(p, axis=-1, keepdims=True)
                acc = alpha * acc + jnp.dot(p.astype(BF16), vh_ref[h, pl.ds(ks, t), :], preferred_element_type=F32)
                out.append((m_new, l, acc))
            return tuple(out)

        init = tuple((jnp.full((t, 1), NEG, F32), jnp.zeros((t, 1), F32), jnp.zeros((t, d), F32)) for _ in range(nh))
        res = lax.fori_loop(0, j + 1, step, init)
        y = jnp.zeros((t, 256), F32)
        for h in range(nh):
            m, l, acc = res[h]
            o = acc / l
            o_ref[h] = o
            lse_ref[h] = m + jnp.log(l)
            y = y + _place_head(o, es[h])
        y_ref[...] = y.astype(y_ref.dtype)

    blk_q = pl.BlockSpec((t, 256), lambda b, j: (b * nq + j, C_AQ // 256))
    blk_k = pl.BlockSpec((s, 256), lambda b, j: (b, C_AK // 256))
    blk_v = pl.BlockSpec((s, 256), lambda b, j: (b, C_AV // 256))
    blk_c1 = pl.BlockSpec((None, nh, t, 1), lambda b, j: (b, 0, j, 0))
    blk_cr = pl.BlockSpec((None, nh, nq, 1, t), lambda b, j: (b, 0, 0, 0, 0))
    blk_o = pl.BlockSpec((None, nh, t, d), lambda b, j: (b, 0, j, 0))
    return pl.pallas_call(
        body, name=name, grid=(bn, nq),
        in_specs=[blk_q, blk_k, blk_v, blk_c1, blk_cr],
        out_specs=[pl.BlockSpec((t, 256), lambda b, j: (b * nq + j, 0)), blk_o, blk_c1],
        out_shape=[jax.ShapeDtypeStruct((bn * s, 256), BF16), jax.ShapeDtypeStruct((bn, nh, s, d), F32),
                   jax.ShapeDtypeStruct((bn, nh, s, 1), F32)],
        scratch_shapes=[pltpu.VMEM((nh, s, d), BF16), pltpu.VMEM((nh, s, d), BF16)],
        compiler_params=_cparams(("arbitrary", "arbitrary")),
    )(proj, proj, proj, c_col, c_row)


def fox2_bwd(proj, c_col, c_row, o, lse, dya, bn, s, name):
    t = FOX_T
    nq = s // t
    nh, d = A_HEADS, HEAD_DIM

    def body(q_ref, k_ref, v_ref, cq_ref, ck_ref, o_ref, lse_ref, dy_ref, dq_ref, dk_ref, dv_ref, dck_ref, dcq_ref,
             kh_ref, vh_ref, dkh_ref, dvh_ref):
        j = pl.program_id(1)
        es = _sel(nh, 256)

        @pl.when(j == 0)
        def _():
            for h in range(nh):
                kh_ref[h] = _pick_head(k_ref[...], es[h])
                vh_ref[h] = _pick_head(v_ref[...], es[h])
            dkh_ref[...] = jnp.zeros_like(dkh_ref)
            dvh_ref[...] = jnp.zeros_like(dvh_ref)
            dck_ref[...] = jnp.zeros_like(dck_ref)

        q4 = q_ref[...]
        dy4 = dy_ref[...]
        qs = [_pick_head(q4, es[h]) for h in range(nh)]
        dos = [_pick_head(dy4, es[h]) for h in range(nh)]
        cqs = [cq_ref[h] for h in range(nh)]
        lses = [lse_ref[h] for h in range(nh)]
        deltas = [jnp.sum(dos[h].astype(F32) * o_ref[h], axis=-1, keepdims=True) for h in range(nh)]

        def step(i, carry):
            ks = pl.multiple_of(i * t, t)
            out = []
            for h in range(nh):
                dq, dcq = carry[h]
                kc = kh_ref[h, pl.ds(ks, t), :]
                sc = _fox_scores(qs[h], kc, cqs[h], ck_ref[h, i], j, i)
                p = jnp.exp(sc - lses[h])
                dp = lax.dot_general(dos[h], vh_ref[h, pl.ds(ks, t), :], (((1,), (1,)), ((), ())),
                                     preferred_element_type=F32)
                ds = p * (dp - deltas[h])
                dsb = ds.astype(BF16)
                dq = dq + jnp.dot(dsb, kc, preferred_element_type=F32) * SCALE
                dkh_ref[h, pl.ds(ks, t), :] += lax.dot_general(dsb, qs[h], (((0,), (0,)), ((), ())),
                                                               preferred_element_type=F32) * SCALE
                dvh_ref[h, pl.ds(ks, t), :] += lax.dot_general(p.astype(BF16), dos[h], (((0,), (0,)), ((), ())),
                                                               preferred_element_type=F32)
                dck_ref[h, i] += -_sum0(ds)
                out.append((dq, dcq + jnp.sum(ds, axis=-1, keepdims=True)))
            return tuple(out)

        init = tuple((jnp.zeros((t, d), F32), jnp.zeros((t, 1), F32)) for _ in range(nh))
        res = lax.fori_loop(0, j + 1, step, init)
        dq4 = jnp.zeros((t, 256), F32)
        for h in range(nh):
            dq4 = dq4 + _place_head(res[h][0], es[h])
            dcq_ref[h] = res[h][1]
        dq_ref[...] = dq4.astype(dq_ref.dtype)

        @pl.when(j == nq - 1)
        def _():
            dk4 = jnp.zeros((s, 256), F32)
            dv4 = jnp.zeros((s, 256), F32)
            for h in range(nh):
                dk4 = dk4 + _place_head(dkh_ref[h], es[h])
                dv4 = dv4 + _place_head(dvh_ref[h], es[h])
            dk_ref[...] = dk4.astype(dk_ref.dtype)
            dv_ref[...] = dv4.astype(dv_ref.dtype)

    blk_q = pl.BlockSpec((t, 256), lambda b, j: (b * nq + j, C_AQ // 256))
    blk_k = pl.BlockSpec((s, 256), lambda b, j: (b, C_AK // 256))
    blk_v = pl.BlockSpec((s, 256), lambda b, j: (b, C_AV // 256))
    blk_c1 = pl.BlockSpec((None, nh, t, 1), lambda b, j: (b, 0, j, 0))
    blk_cr = pl.BlockSpec((None, nh, nq, 1, t), lambda b, j: (b, 0, 0, 0, 0))
    blk_o = pl.BlockSpec((None, nh, t, d), lambda b, j: (b, 0, j, 0))
    blk_t = pl.BlockSpec((t, 256), lambda b, j: (b * nq + j, 0))
    blk_s = pl.BlockSpec((s, 256), lambda b, j: (b, 0))
    return pl.pallas_call(
        body, name=name, grid=(bn, nq),
        in_specs=[blk_q, blk_k, blk_v, blk_c1, blk_cr, blk_o, blk_c1, blk_t],
        out_specs=[blk_t, blk_s, blk_s, blk_cr, blk_c1],
        out_shape=[jax.ShapeDtypeStruct((bn * s, 256), BF16)] * 3
        + [jax.ShapeDtypeStruct((bn, nh, nq, 1, t), F32), jax.ShapeDtypeStruct((bn, nh, s, 1), F32)],
        scratch_shapes=[pltpu.VMEM((nh, s, d), BF16), pltpu.VMEM((nh, s, d), BF16),
                        pltpu.VMEM((nh, s, d), F32), pltpu.VMEM((nh, s, d), F32)],
        compiler_params=_cparams(("arbitrary", "arbitrary")),
    )(proj, proj, proj, c_col, c_row, o, lse, dya)


def _band3(ref, h, n):
    qs = pl.multiple_of(n * QB, QB)
    ps = pl.multiple_of(jnp.maximum(n - 1, 0) * QB, QB)
    return jnp.concatenate([ref[h, pl.ds(ps, QB), :], ref[h, pl.ds(qs, QB), :]], axis=0), qs, ps


def swa2_fwd(proj, bias, sinks, bn, s, name):
    nb = s // QB
    d = HEAD_DIM

    def body(q0_ref, q1_ref, k_ref, v_ref, b_ref, s_ref, y_ref, lse_ref, qh_ref, kh_ref, vh_ref):
        e4 = _sel(D_GROUP, 256)
        e2 = _sel(D_KV_HEADS, 128)
        for kh, q_ref in enumerate((q0_ref, q1_ref)):
            kh_ref[kh] = _pick_head(k_ref[...], e2[kh])
            vh_ref[kh] = _pick_head(v_ref[...], e2[kh])
            for g in range(D_GROUP):
                qh_ref[D_GROUP * kh + g] = _pick_head(q_ref[...], e4[g])

        def step(n, _):
            valid = _swa_valid(n)
            for kh in range(D_KV_HEADS):
                kb, qs, _ps = _band3(kh_ref, kh, n)
                vb, _, _ = _band3(vh_ref, kh, n)
                y = jnp.zeros((QB, 256), F32)
                for g in range(D_GROUP):
                    hh = D_GROUP * kh + g
                    qg = qh_ref[hh, pl.ds(qs, QB), :]
                    sc = lax.dot_general(qg, kb, (((1,), (1,)), ((), ())), preferred_element_type=F32) * SCALE
                    sc = jnp.where(valid, sc + b_ref[hh], NEG)
                    sink = s_ref[hh]
                    m = jnp.maximum(jnp.max(sc, axis=-1, keepdims=True), sink)
                    e = jnp.exp(sc - m)
                    z = jnp.sum(e, axis=-1, keepdims=True) + jnp.exp(sink - m)
                    o = jnp.dot((e / z).astype(BF16), vb, preferred_element_type=F32)
                    lse_ref[hh, pl.ds(qs, QB), :] = m + jnp.log(z)
                    y = y + _place_head(o, e4[g])
                y_ref[pl.ds(qs, QB), 256 * kh:256 * (kh + 1)] = y.astype(y_ref.dtype)
            return 0

        lax.fori_loop(0, nb, step, 0)

    return pl.pallas_call(
        body, name=name, grid=(bn,),
        in_specs=[pl.BlockSpec((s, 256), lambda b: (b, C_DQ // 256)), pl.BlockSpec((s, 256), lambda b: (b, C_DQ // 256 + 1)),
                  pl.BlockSpec((s, 128), lambda b: (b, C_DK // 128)), pl.BlockSpec((s, 128), lambda b: (b, C_DV // 128)),
                  pl.BlockSpec((D_Q_HEADS, QB, 2 * QB), lambda b: (0, 0, 0)),
                  pl.BlockSpec((D_Q_HEADS, QB, 1), lambda b: (0, 0, 0))],
        out_specs=[pl.BlockSpec((s, 512), lambda b: (b, 0)), pl.BlockSpec((None, D_Q_HEADS, s, 1), lambda b: (b, 0, 0, 0))],
        out_shape=[jax.ShapeDtypeStruct((bn * s, 512), BF16), jax.ShapeDtypeStruct((bn, D_Q_HEADS, s, 1), F32)],
        scratch_shapes=[pltpu.VMEM((D_Q_HEADS, s, d), BF16), pltpu.VMEM((D_KV_HEADS, s, d), BF16),
                        pltpu.VMEM((D_KV_HEADS, s, d), BF16)],
        compiler_params=_cparams(("parallel",)),
    )(proj, proj, proj, proj, bias, sinks)


def swa2_bwd(proj, bias, sinks, yd, lse, dyd, bn, s, name):
    nb = s // QB
    d = HEAD_DIM

    def body(q0_ref, q1_ref, k_ref, v_ref, b_ref, s_ref, y_ref, lse_ref, dy_ref, dq_ref, dk_ref, dv_ref, db_ref,
             dsk_ref, qh_ref, kh_ref, vh_ref, oh_ref, doh_ref, dkh_ref, dvh_ref):
        @pl.when(pl.program_id(0) == 0)
        def _():
            db_ref[...] = jnp.zeros_like(db_ref)
            dsk_ref[...] = jnp.zeros_like(dsk_ref)

        e4 = _sel(D_GROUP, 256)
        e2 = _sel(D_KV_HEADS, 128)
        for kh, q_ref in enumerate((q0_ref, q1_ref)):
            kh_ref[kh] = _pick_head(k_ref[...], e2[kh])
            vh_ref[kh] = _pick_head(v_ref[...], e2[kh])
            for g in range(D_GROUP):
                hh = D_GROUP * kh + g
                qh_ref[hh] = _pick_head(q_ref[...], e4[g])
                oh_ref[hh] = _pick_head(y_ref[:, 256 * kh:256 * (kh + 1)], e4[g])
                doh_ref[hh] = _pick_head(dy_ref[:, 256 * kh:256 * (kh + 1)], e4[g])
        dkh_ref[...] = jnp.zeros_like(dkh_ref)
        dvh_ref[...] = jnp.zeros_like(dvh_ref)

        def step(n, _):
            valid = _swa_valid(n)
            for kh in range(D_KV_HEADS):
                kb, qs, ps = _band3(kh_ref, kh, n)
                vb, _, _ = _band3(vh_ref, kh, n)
                dkb = jnp.zeros((2 * QB, d), F32)
                dvb = jnp.zeros((2 * QB, d), F32)
                dq4 = jnp.zeros((QB, 256), F32)
                for g in range(D_GROUP):
                    hh = D_GROUP * kh + g
                    qg = qh_ref[hh, pl.ds(qs, QB), :]
                    dog = doh_ref[hh, pl.ds(qs, QB), :]
                    og = oh_ref[hh, pl.ds(qs, QB), :]
                    lse = lse_ref[hh, pl.ds(qs, QB), :]
                    sc = lax.dot_general(qg, kb, (((1,), (1,)), ((), ())), preferred_element_type=F32) * SCALE
                    sc = jnp.where(valid, sc + b_ref[hh], NEG)
                    p = jnp.exp(sc - lse)
                    delta = jnp.sum(dog.astype(F32) * og.astype(F32), axis=-1, keepdims=True)
                    dp = lax.dot_general(dog, vb, (((1,), (1,)), ((), ())), preferred_element_type=F32)
                    ds = p * (dp - delta)
                    dsb = ds.astype(BF16)
                    dq4 = dq4 + _place_head(jnp.dot(dsb, kb, preferred_element_type=F32) * SCALE, e4[g])
                    dkb = dkb + lax.dot_general(dsb, qg, (((0,), (0,)), ((), ())),
                                                preferred_element_type=F32) * SCALE
                    dvb = dvb + lax.dot_general(p.astype(BF16), dog, (((0,), (0,)), ((), ())),
                                                preferred_element_type=F32)
                    db_ref[hh] += ds
                    dsk_ref[hh] += -jnp.exp(s_ref[hh] - lse) * delta
                dq_ref[pl.ds(qs, QB), 256 * kh:256 * (kh + 1)] = dq4.astype(dq_ref.dtype)
                dkh_ref[kh, pl.ds(ps, QB), :] += dkb[:QB]
                dkh_ref[kh, pl.ds(qs, QB), :] += dkb[QB:]
                dvh_ref[kh, pl.ds(ps, QB), :] += dvb[:QB]
                dvh_ref[kh, pl.ds(qs, QB), :] += dvb[QB:]
            return 0

        lax.fori_loop(0, nb, step, 0)
        dk2 = jnp.zeros((s, 128), F32)
        dv2 = jnp.zeros((s, 128), F32)
        for kh in range(D_KV_HEADS):
            dk2 = dk2 + _place_head(dkh_ref[kh], e2[kh])
            dv2 = dv2 + _place_head(dvh_ref[kh], e2[kh])
        dk_ref[...] = dk2.astype(dk_ref.dtype)
        dv_ref[...] = dv2.astype(dv_ref.dtype)

    blk512 = pl.BlockSpec((s, 512), lambda b: (b, 0))
    blk128 = pl.BlockSpec((s, 128), lambda b: (b, 0))
    blk_b = pl.BlockSpec((D_Q_HEADS, QB, 2 * QB), lambda b: (0, 0, 0))
    blk_s = pl.BlockSpec((D_Q_HEADS, QB, 1), lambda b: (0, 0, 0))
    return pl.pallas_call(
        body, name=name, grid=(bn,),
        in_specs=[pl.BlockSpec((s, 256), lambda b: (b, C_DQ // 256)), pl.BlockSpec((s, 256), lambda b: (b, C_DQ // 256 + 1)),
                  pl.BlockSpec((s, 128), lambda b: (b, C_DK // 128)), pl.BlockSpec((s, 128), lambda b: (b, C_DV // 128)),
                  blk_b, blk_s, blk512, pl.BlockSpec((None, D_Q_HEADS, s, 1), lambda b: (b, 0, 0, 0)), blk512],
        out_specs=[blk512, blk128, blk128, blk_b, blk_s],
        out_shape=[jax.ShapeDtypeStruct((bn * s, 512), BF16), jax.ShapeDtypeStruct((bn * s, 128), BF16),
                   jax.ShapeDtypeStruct((bn * s, 128), BF16),
                   jax.ShapeDtypeStruct((D_Q_HEADS, QB, 2 * QB), F32), jax.ShapeDtypeStruct((D_Q_HEADS, QB, 1), F32)],
        scratch_shapes=[pltpu.VMEM((D_Q_HEADS, s, d), BF16), pltpu.VMEM((D_KV_HEADS, s, d), BF16),
                        pltpu.VMEM((D_KV_HEADS, s, d), BF16), pltpu.VMEM((D_Q_HEADS, s, d), BF16),
                        pltpu.VMEM((D_Q_HEADS, s, d), BF16), pltpu.VMEM((D_KV_HEADS, s, d), F32),
                        pltpu.VMEM((D_KV_HEADS, s, d), F32)],
        compiler_params=_cparams(("arbitrary",)),
    )(proj, proj, proj, proj, bias, sinks, yd, lse, dyd)


def _sel_at(off, width):
    r = lax.broadcasted_iota(jnp.int32, (width, HEAD_DIM), 0)
    c = lax.broadcasted_iota(jnp.int32, (width, HEAD_DIM), 1)
    return (r == c + off).astype(BF16)


def _eye(n):
    return lax.broadcasted_iota(jnp.int32, (n, n), 0) == lax.broadcasted_iota(jnp.int32, (n, n), 1)


def _row_to_col(row, eye):
    return jnp.sum(jnp.where(eye, row, 0.0), axis=1, keepdims=True)


def _col_to_row(col, eye):
    return jnp.sum(jnp.where(eye, col, 0.0), axis=0, keepdims=True)


def swa3_fwd(proj, bias, sinks, bn, s, name):
    nb = s // QB
    d = HEAD_DIM

    def body(q_ref, k_ref, v_ref, b_ref, s_ref, y_ref, lse_ref, qh_ref, kh_ref, vh_ref):
        kh = pl.program_id(0)
        e4 = _sel(D_GROUP, 256)
        ek = _sel_at(HEAD_DIM * kh, 128)
        eye = _eye(QB)
        kh_ref[...] = _pick_head(k_ref[...], ek)
        vh_ref[...] = _pick_head(v_ref[...], ek)
        for g in range(D_GROUP):
            qh_ref[g] = _pick_head(q_ref[...], e4[g])

        def step(n, _):
            valid = _swa_valid(n)
            kb, qs, _ps = _swa_band(kh_ref, n)
            vb, _, _ = _swa_band(vh_ref, n)
            y = jnp.zeros((QB, 256), F32)
            for g in range(D_GROUP):
                qg = qh_ref[g, pl.ds(qs, QB), :]
                sc = lax.dot_general(qg, kb, (((1,), (1,)), ((), ())), preferred_element_type=F32) * SCALE
                sc = jnp.where(valid, sc + b_ref[g], NEG)
                sink = s_ref[g]
                m = jnp.maximum(jnp.max(sc, axis=-1, keepdims=True), sink)
                e = jnp.exp(sc - m)
                z = jnp.sum(e, axis=-1, keepdims=True) + jnp.exp(sink - m)
                o = jnp.dot((e / z).astype(BF16), vb, preferred_element_type=F32)
                lse_ref[g, n] = _col_to_row(m + jnp.log(z), eye)
                y = y + _place_head(o, e4[g])
            y_ref[pl.ds(qs, QB), :] = y.astype(y_ref.dtype)
            return 0

        lax.fori_loop(0, nb, step, 0)

    return pl.pallas_call(
        body, name=name, grid=(D_KV_HEADS, bn),
        in_specs=[pl.BlockSpec((s, 256), lambda kh, b: (b, C_DQ // 256 + kh)),
                  pl.BlockSpec((s, 128), lambda kh, b: (b, C_DK // 128)),
                  pl.BlockSpec((s, 128), lambda kh, b: (b, C_DV // 128)),
                  pl.BlockSpec((D_GROUP, QB, 2 * QB), lambda kh, b: (kh, 0, 0)),
                  pl.BlockSpec((D_GROUP, QB, 1), lambda kh, b: (kh, 0, 0))],
        out_specs=[pl.BlockSpec((s, 256), lambda kh, b: (b, kh)),
                   pl.BlockSpec((None, D_GROUP, nb, 1, QB), lambda kh, b: (b, kh, 0, 0, 0))],
        out_shape=[jax.ShapeDtypeStruct((bn * s, 512), BF16), jax.ShapeDtypeStruct((bn, D_Q_HEADS, nb, 1, QB), F32)],
        scratch_shapes=[pltpu.VMEM((D_GROUP, s, d), BF16), pltpu.VMEM((s, d), BF16), pltpu.VMEM((s, d), BF16)],
        compiler_params=_cparams(("parallel", "parallel")),
    )(proj, proj, proj, bias, sinks)


def swa3_bwd(proj, bias, sinks, yd, lse, dyd, bn, s, name):
    nb = s // QB
    d = HEAD_DIM

    def body(q_ref, k_ref, v_ref, b_ref, s_ref, y_ref, lse_ref, dy_ref, dq_ref, dk_ref, dv_ref, db_ref, dsk_ref,
             qh_ref, kh_ref, vh_ref, oh_ref, doh_ref, dkh_ref, dvh_ref):
        kh = pl.program_id(0)

        @pl.when(pl.program_id(1) == 0)
        def _():
            db_ref[...] = jnp.zeros_like(db_ref)
            dsk_ref[...] = jnp.zeros_like(dsk_ref)

        e4 = _sel(D_GROUP, 256)
        ek = _sel_at(HEAD_DIM * kh, 128)
        eye = _eye(QB)
        kh_ref[...] = _pick_head(k_ref[...], ek)
        vh_ref[...] = _pick_head(v_ref[...], ek)
        for g in range(D_GROUP):
            qh_ref[g] = _pick_head(q_ref[...], e4[g])
            oh_ref[g] = _pick_head(y_ref[...], e4[g])
            doh_ref[g] = _pick_head(dy_ref[...], e4[g])
        dkh_ref[...] = jnp.zeros_like(dkh_ref)
        dvh_ref[...] = jnp.zeros_like(dvh_ref)

        def step(n, _):
            valid = _swa_valid(n)
            kb, qs, ps = _swa_band(kh_ref, n)
            vb, _, _ = _swa_band(vh_ref, n)
            dkb = jnp.zeros((2 * QB, d), F32)
            dvb = jnp.zeros((2 * QB, d), F32)
            dq4 = jnp.zeros((QB, 256), F32)
            for g in range(D_GROUP):
                qg = qh_ref[g, pl.ds(qs, QB), :]
                dog = doh_ref[g, pl.ds(qs, QB), :]
                og = oh_ref[g, pl.ds(qs, QB), :]
                lse = _row_to_col(lse_ref[g, n], eye)
                sc = lax.dot_general(qg, kb, (((1,), (1,)), ((), ())), preferred_element_type=F32) * SCALE
                sc = jnp.where(valid, sc + b_ref[g], NEG)
                p = jnp.exp(sc - lse)
                delta = jnp.sum(dog.astype(F32) * og.astype(F32), axis=-1, keepdims=True)
                dp = lax.dot_general(dog, vb, (((1,), (1,)), ((), ())), preferred_element_type=F32)
                ds = p * (dp - delta)
                dsb = ds.astype(BF16)
                dq4 = dq4 + _place_head(jnp.dot(dsb, kb, preferred_element_type=F32) * SCALE, e4[g])
                dkb = dkb + lax.dot_general(dsb, qg, (((0,), (0,)), ((), ())), preferred_element_type=F32) * SCALE
                dvb = dvb + lax.dot_general(p.astype(BF16), dog, (((0,), (0,)), ((), ())),
                                            preferred_element_type=F32)
                db_ref[g] += ds
                dsk_ref[g] += -jnp.exp(s_ref[g] - lse) * delta
            dq_ref[pl.ds(qs, QB), :] = dq4.astype(dq_ref.dtype)
            dkh_ref[pl.ds(ps, QB), :] += dkb[:QB]
            dkh_ref[pl.ds(qs, QB), :] += dkb[QB:]
            dvh_ref[pl.ds(ps, QB), :] += dvb[:QB]
            dvh_ref[pl.ds(qs, QB), :] += dvb[QB:]
            return 0

        lax.fori_loop(0, nb, step, 0)
        dk_ref[...] = dkh_ref[...].astype(dk_ref.dtype)
        dv_ref[...] = dvh_ref[...].astype(dv_ref.dtype)

    blk256 = pl.BlockSpec((s, 256), lambda kh, b: (b, kh))
    blk_kv = pl.BlockSpec((None, s, d), lambda kh, b: (kh, b, 0))
    blk_b = pl.BlockSpec((D_GROUP, QB, 2 * QB), lambda kh, b: (kh, 0, 0))
    blk_s = pl.BlockSpec((D_GROUP, QB, 1), lambda kh, b: (kh, 0, 0))
    return pl.pallas_call(
        body, name=name, grid=(D_KV_HEADS, bn),
        in_specs=[pl.BlockSpec((s, 256), lambda kh, b: (b, C_DQ // 256 + kh)),
                  pl.BlockSpec((s, 128), lambda kh, b: (b, C_DK // 128)),
                  pl.BlockSpec((s, 128), lambda kh, b: (b, C_DV // 128)),
                  blk_b, blk_s, blk256,
                  pl.BlockSpec((None, D_GROUP, nb, 1, QB), lambda kh, b: (b, kh, 0, 0, 0)), blk256],
        out_specs=[blk256, blk_kv, blk_kv, blk_b, blk_s],
        out_shape=[jax.ShapeDtypeStruct((bn * s, 512), BF16), jax.ShapeDtypeStruct((D_KV_HEADS, bn * s, d), BF16),
                   jax.ShapeDtypeStruct((D_KV_HEADS, bn * s, d), BF16),
                   jax.ShapeDtypeStruct((D_Q_HEADS, QB, 2 * QB), F32), jax.ShapeDtypeStruct((D_Q_HEADS, QB, 1), F32)],
        scratch_shapes=[pltpu.VMEM((D_GROUP, s, d), BF16), pltpu.VMEM((s, d), BF16), pltpu.VMEM((s, d), BF16),
                        pltpu.VMEM((D_GROUP, s, d), BF16), pltpu.VMEM((D_GROUP, s, d), BF16),
                        pltpu.VMEM((s, d), F32), pltpu.VMEM((s, d), F32)],
        compiler_params=_cparams(("parallel", "arbitrary")),
    )(proj, proj, proj, bias, sinks, yd, lse, dyd)


def _bucket_table():
    dist = jnp.maximum(jnp.arange(QB)[:, None] + QB - jnp.arange(2 * QB)[None, :], 0)
    max_exact = REL_BUCKETS // 2
    large = max_exact + (jnp.log(jnp.maximum(dist, 1).astype(F32) / max_exact)
                         / math.log(REL_MAX_DIST / max_exact) * (REL_BUCKETS - max_exact)).astype(jnp.int32)
    large = jnp.minimum(large, REL_BUCKETS - 1)
    return jnp.where(dist < max_exact, dist, large).astype(F32)


def band_bias_fwd(bucket, rel_bias, name):
    def body(bk_ref, rel_ref, o_ref):
        bk = bk_ref[...]
        for hh in range(D_Q_HEADS):
            acc = jnp.zeros(bk.shape, F32)
            for b in range(REL_BUCKETS):
                acc = jnp.where(bk == float(b), rel_ref[b, hh], acc)
            o_ref[hh] = acc

    return pl.pallas_call(
        body, name=name,
        in_specs=[pl.BlockSpec(memory_space=pltpu.VMEM), pl.BlockSpec(memory_space=pltpu.SMEM)],
        out_specs=pl.BlockSpec(memory_space=pltpu.VMEM),
        out_shape=jax.ShapeDtypeStruct((D_Q_HEADS, QB, 2 * QB), F32),
    )(bucket, rel_bias)


def band_bias_bwd(bucket, dbias_layers, name):
    nl = len(dbias_layers)

    def body(bk_ref, *refs):
        o_ref = refs[nl]
        bk = bk_ref[...]
        for hh in range(D_Q_HEADS):
            tot = refs[0][hh]
            for r in refs[1:nl]:
                tot = tot + r[hh]
            for b in range(REL_BUCKETS):
                part = jnp.sum(jnp.where(bk == float(b), tot, 0.0), axis=0, keepdims=True)
                val = jnp.sum(part, axis=1, keepdims=True)
                o_ref[hh, b:b + 1, :] = jnp.broadcast_to(val, (1, 128))

    return pl.pallas_call(
        body, name=name,
        in_specs=[pl.BlockSpec(memory_space=pltpu.VMEM)] * (nl + 1),
        out_specs=pl.BlockSpec(memory_space=pltpu.VMEM),
        out_shape=jax.ShapeDtypeStruct((D_Q_HEADS, REL_BUCKETS, 128), F32),
    )(bucket, *dbias_layers)


def _proj_blk(s, col):
    return pl.BlockSpec((s, 256), functools.partial(lambda b, cb: (b, cb), cb=col // 256))


def convb_fwd(proj, w, bn, s, name):
    kk = w.shape[0]

    def body(bg_ref, cg_ref, xb_ref, w_ref, o_ref):
        x = cg_ref[...].astype(F32) * xb_ref[...].astype(F32)
        row = lax.broadcasted_iota(jnp.int32, x.shape, 0)
        y = jnp.zeros_like(x)
        for k in range(kk):
            y = y + w_ref[k:k + 1, :] * _shift_down(x, kk - 1 - k, row)
        o_ref[...] = (bg_ref[...].astype(F32) * y).astype(o_ref.dtype)

    return pl.pallas_call(
        body, name=name, grid=(bn,),
        in_specs=[_proj_blk(s, C_BG), _proj_blk(s, C_CG), _proj_blk(s, C_XB), pl.BlockSpec(w.shape, lambda b: (0, 0))],
        out_specs=pl.BlockSpec((s, 256), lambda b: (b, 0)),
        out_shape=jax.ShapeDtypeStruct((bn * s, 256), BF16),
        compiler_params=_cparams(("parallel",)),
    )(proj, proj, proj, w)


def convb_bwd(proj, w, dyb, bn, s, name):
    kk = w.shape[0]

    def body(bg_ref, cg_ref, xb_ref, w_ref, d_ref, dbg_ref, dcg_ref, dxb_ref, dw_ref):
        @pl.when(pl.program_id(0) == 0)
        def _():
            dw_ref[...] = jnp.zeros_like(dw_ref)

        cg = cg_ref[...].astype(F32)
        xb = xb_ref[...].astype(F32)
        d = d_ref[...].astype(F32)
        x = cg * xb
        row = lax.broadcasted_iota(jnp.int32, x.shape, 0)
        dy = d * bg_ref[...].astype(F32)
        y = jnp.zeros_like(x)
        dx = jnp.zeros_like(x)
        for k in range(kk):
            xs = _shift_down(x, kk - 1 - k, row)
            y = y + w_ref[k:k + 1, :] * xs
            dx = dx + w_ref[k:k + 1, :] * _shift_up(dy, kk - 1 - k, row)
            dw_ref[k:k + 1, :] += _sum0(dy * xs)
        dbg_ref[...] = (d * y).astype(dbg_ref.dtype)
        dcg_ref[...] = (dx * xb).astype(dcg_ref.dtype)
        dxb_ref[...] = (dx * cg).astype(dxb_ref.dtype)

    blk = pl.BlockSpec((s, 256), lambda b: (b, 0))
    return pl.pallas_call(
        body, name=name, grid=(bn,),
        in_specs=[_proj_blk(s, C_BG), _proj_blk(s, C_CG), _proj_blk(s, C_XB), pl.BlockSpec(w.shape, lambda b: (0, 0)),
                  blk],
        out_specs=[blk, blk, blk, pl.BlockSpec((8, 256), lambda b: (0, 0))],
        out_shape=[jax.ShapeDtypeStruct((bn * s, 256), BF16)] * 3 + [jax.ShapeDtypeStruct((8, 256), F32)],
        compiler_params=_cparams(("arbitrary",)),
    )(proj, proj, proj, w, dyb)


def _convc_core(ca, cb, w_ref, bias, kk, row):
    sg = jax.nn.sigmoid(cb)
    glu = ca * sg
    y = jnp.zeros_like(glu)
    for k in range(kk):
        y = y + w_ref[k:k + 1, :] * _shift_down(glu, kk - 1 - k, row)
    y = y + bias
    mu = jnp.mean(y, axis=-1, keepdims=True)
    xc = y - mu
    r = lax.rsqrt(jnp.mean(xc * xc, axis=-1, keepdims=True) + EPS)
    return sg, glu, xc * r, r


def convc_fwd(proj, w, bias, gain, lbias, bn, s, name):
    kk = w.shape[0]

    def body(ca_ref, cb_ref, w_ref, b_ref, g_ref, lb_ref, o_ref):
        ca = ca_ref[...].astype(F32)
        row = lax.broadcasted_iota(jnp.int32, ca.shape, 0)
        _, _, xh, _ = _convc_core(ca, cb_ref[...].astype(F32), w_ref, b_ref[...], kk, row)
        ln = xh * g_ref[...] + lb_ref[...]
        o_ref[...] = (ln * jax.nn.sigmoid(ln)).astype(o_ref.dtype)

    vec = pl.BlockSpec((1, 256), lambda b: (0, 0))
    return pl.pallas_call(
        body, name=name, grid=(bn,),
        in_specs=[_proj_blk(s, C_CA), _proj_blk(s, C_CB), pl.BlockSpec(w.shape, lambda b: (0, 0)), vec, vec, vec],
        out_specs=pl.BlockSpec((s, 256), lambda b: (b, 0)),
        out_shape=jax.ShapeDtypeStruct((bn * s, 256), BF16),
        compiler_params=_cparams(("parallel",)),
    )(proj, proj, w, bias, gain, lbias)


def convc_bwd(proj, w, bias, gain, lbias, dyc, bn, s, name):
    kk = w.shape[0]

    def body(ca_ref, cb_ref, w_ref, b_ref, g_ref, lb_ref, d_ref, dca_ref, dcb_ref, dw_ref, db_ref, dg_ref, dlb_ref):
        @pl.when(pl.program_id(0) == 0)
        def _():
            dw_ref[...] = jnp.zeros_like(dw_ref)
            db_ref[...] = jnp.zeros_like(db_ref)
            dg_ref[...] = jnp.zeros_like(dg_ref)
            dlb_ref[...] = jnp.zeros_like(dlb_ref)

        ca = ca_ref[...].astype(F32)
        row = lax.broadcasted_iota(jnp.int32, ca.shape, 0)
        sg, glu, xh, r = _convc_core(ca, cb_ref[...].astype(F32), w_ref, b_ref[...], kk, row)
        ln = xh * g_ref[...] + lb_ref[...]
        sl = jax.nn.sigmoid(ln)
        dl = d_ref[...].astype(F32) * (sl + ln * sl * (1.0 - sl))
        dg_ref[...] += _sum0(dl * xh)
        dlb_ref[...] += _sum0(dl)
        dxh = dl * g_ref[...]
        dy = r * (dxh - jnp.mean(dxh, axis=-1, keepdims=True) - xh * jnp.mean(dxh * xh, axis=-1, keepdims=True))
        db_ref[...] += _sum0(dy)
        dglu = jnp.zeros_like(glu)
        for k in range(kk):
            dw_ref[k:k + 1, :] += _sum0(dy * _shift_down(glu, kk - 1 - k, row))
            dglu = dglu + w_ref[k:k + 1, :] * _shift_up(dy, kk - 1 - k, row)
        dca_ref[...] = (dglu * sg).astype(dca_ref.dtype)
        dcb_ref[...] = (dglu * ca * sg * (1.0 - sg)).astype(dcb_ref.dtype)

    vec = pl.BlockSpec((1, 256), lambda b: (0, 0))
    blk = pl.BlockSpec((s, 256), lambda b: (b, 0))
    return pl.pallas_call(
        body, name=name, grid=(bn,),
        in_specs=[_proj_blk(s, C_CA), _proj_blk(s, C_CB), pl.BlockSpec(w.shape, lambda b: (0, 0)), vec, vec, vec, blk],
        out_specs=[blk, blk, pl.BlockSpec((32, 256), lambda b: (0, 0)), vec, vec, vec],
        out_shape=[jax.ShapeDtypeStruct((bn * s, 256), BF16)] * 2 + [jax.ShapeDtypeStruct((32, 256), F32)]
        + [jax.ShapeDtypeStruct((1, 256), F32)] * 3,
        compiler_params=_cparams(("arbitrary",)),
    )(proj, proj, w, bias, gain, lbias, dyc)


def adamw(w, g, m, v, name):
    shape = w.shape
    cols = shape[-1]
    rows = w.size // cols
    tr = _pick(rows, (256, 128, 64, 32, 16, 8))

    def body(w_ref, g_ref, m_ref, v_ref, d_ref, nm_ref, nv_ref):
        gg = g_ref[...]
        mm = ADAM_B1 * m_ref[...] + (1.0 - ADAM_B1) * gg
        vv = ADAM_B2 * v_ref[...] + (1.0 - ADAM_B2) * jnp.square(gg)
        m_hat = mm / (1.0 - ADAM_B1 ** ADAM_STEP)
        v_hat = vv / (1.0 - ADAM_B2 ** ADAM_STEP)
        d_ref[...] = -ADAM_LR * (m_hat / (jnp.sqrt(v_hat) + ADAM_EPS) + ADAM_WD * w_ref[...])
        nm_ref[...] = mm
        nv_ref[...] = vv

    blk = pl.BlockSpec((tr, cols), lambda i: (i, 0))
    outs = pl.pallas_call(
        body, name=name, grid=(rows // tr,), in_specs=[blk] * 4, out_specs=[blk] * 3,
        out_shape=[jax.ShapeDtypeStruct((rows, cols), F32)] * 3,
        compiler_params=_cparams(("parallel",)),
    )(*[a.reshape(rows, cols) for a in (w, g, m, v)])
    return [o.reshape(shape) for o in outs]


def add_halves(own, recv, name):
    n, r, c = own.shape
    tr = _pick(r, (512, 256, 128, 64, 32, 16, 8))
    blk = pl.BlockSpec((None, tr, c), lambda i, j: (i, j, 0))

    def body(a_ref, b_ref, o_ref):
        o_ref[...] = a_ref[...] + b_ref[...]

    return pl.pallas_call(
        body, name=name, grid=(n, r // tr), in_specs=[blk, blk], out_specs=blk,
        out_shape=jax.ShapeDtypeStruct((n, r, c), F32), compiler_params=_cparams(("parallel", "parallel")),
    )(own, recv)


def sum_slots(slots, name):
    n, r, c = slots.shape
    tr = _pick(r, (512, 256, 128, 64, 32, 16, 8))

    def body(a_ref, o_ref):
        acc = a_ref[0]
        for k in range(1, n):
            acc = acc + a_ref[k]
        o_ref[...] = acc

    return pl.pallas_call(
        body, name=name, grid=(r // tr,), in_specs=[pl.BlockSpec((n, tr, c), lambda j: (0, j, 0))],
        out_specs=pl.BlockSpec((tr, c), lambda j: (j, 0)),
        out_shape=jax.ShapeDtypeStruct((r, c), F32), compiler_params=_cparams(("parallel",)),
    )(slots)


ANY = pl.BlockSpec(memory_space=pl.ANY)


def _place():
    x, y, c = lax.axis_index("x"), lax.axis_index("y"), lax.axis_index("c")
    return x, y, c


def gather_shards(pack, name):
    r, cols = pack.shape
    half = r // 2

    def body(src_ref, out_ref, send_sems, recv_sems, local_sem):
        x, y, c = _place()
        sibling = (x, y, 1 - c)
        chips = [(1 - x, y), (x, 1 - y), (1 - x, 1 - y)]

        def rows(px, py, pc):
            return out_ref.at[2 * px + py, pl.ds(pc * half, half), :]

        mine = pltpu.make_async_copy(src_ref, out_ref.at[2 * x + y], local_sem)
        mine.start()

        def copy(k, blk, to, src=None):
            return pltpu.make_async_remote_copy(
                src_ref=rows(*blk) if src is None else src, dst_ref=rows(*blk),
                send_sem=send_sems.at[k], recv_sem=recv_sems.at[k], device_id=to, device_id_type=MESH)

        first = [copy(j, (x, y, c), (*chip, c), src=src_ref.at[pl.ds(c * half, half), :])
                 for j, chip in enumerate(chips)]
        for cp in first:
            cp.start()
        passed = [copy(3 + j, (*chip, c), sibling) for j, chip in enumerate(chips)]
        for j, chip in enumerate(chips):
            copy(j, (*chip, c), (x, y, c)).wait_recv()
            passed[j].start()
        for j, chip in enumerate(chips):
            copy(3 + j, (*chip, 1 - c), (x, y, c)).wait_recv()
        for cp in first + passed:
            cp.wait_send()
        mine.wait()

    return pl.pallas_call(
        body, name=name, in_specs=[ANY], out_specs=ANY,
        out_shape=jax.ShapeDtypeStruct((N_CHIPS, r, cols), pack.dtype),
        scratch_shapes=[pltpu.SemaphoreType.DMA((6,)), pltpu.SemaphoreType.DMA((6,)), pltpu.SemaphoreType.DMA],
    )(pack)


def exchange_sibling_halves(g, name):
    n, r, cols = g.shape
    half = r // 2

    def body(g_ref, own_ref, recv_ref, send_sems, recv_sems, local_sem):
        x, y, c = _place()
        sibling = (x, y, 1 - c)
        mine = pltpu.make_async_copy(g_ref.at[:, pl.ds(c * half, half), :], own_ref, local_sem)
        mine.start()
        cp = pltpu.make_async_remote_copy(
            src_ref=g_ref.at[:, pl.ds((1 - c) * half, half), :], dst_ref=recv_ref,
            send_sem=send_sems.at[0], recv_sem=recv_sems.at[0], device_id=sibling, device_id_type=MESH)
        cp.start()
        cp.wait()
        mine.wait()

    return pl.pallas_call(
        body, name=name, in_specs=[ANY], out_specs=[ANY, ANY],
        out_shape=[jax.ShapeDtypeStruct((n, half, cols), g.dtype)] * 2,
        scratch_shapes=[pltpu.SemaphoreType.DMA((1,)), pltpu.SemaphoreType.DMA((1,)), pltpu.SemaphoreType.DMA],
    )(g)


def scatter_to_chips(part, name):
    n, h, cols = part.shape

    def body(p_ref, slot_ref, send_sems, recv_sems, local_sem):
        x, y, c = _place()
        me = 2 * x + y
        chips = [(1 - x, y), (x, 1 - y), (1 - x, 1 - y)]
        mine = pltpu.make_async_copy(p_ref.at[me], slot_ref.at[me], local_sem)
        mine.start()
        cps = [pltpu.make_async_remote_copy(
            src_ref=p_ref.at[2 * px + py], dst_ref=slot_ref.at[me],
            send_sem=send_sems.at[j], recv_sem=recv_sems.at[j], device_id=(px, py, c), device_id_type=MESH)
            for j, (px, py) in enumerate(chips)]
        for cp in cps:
            cp.start()
        for j, (px, py) in enumerate(chips):
            pltpu.make_async_remote_copy(
                src_ref=p_ref.at[me], dst_ref=slot_ref.at[2 * px + py],
                send_sem=send_sems.at[j], recv_sem=recv_sems.at[j], device_id=(px, py, c),
                device_id_type=MESH).wait_recv()
        for cp in cps:
            cp.wait_send()
        mine.wait()

    return pl.pallas_call(
        body, name=name, in_specs=[ANY], out_specs=ANY,
        out_shape=jax.ShapeDtypeStruct((n, h, cols), part.dtype),
        scratch_shapes=[pltpu.SemaphoreType.DMA((3,)), pltpu.SemaphoreType.DMA((3,)), pltpu.SemaphoreType.DMA],
    )(part)


def join_sibling_halves(mine_half, name):
    h, cols = mine_half.shape

    def body(m_ref, out_ref, send_sems, recv_sems, local_sem):
        x, y, c = _place()
        sibling = (x, y, 1 - c)
        own = pltpu.make_async_copy(m_ref, out_ref.at[pl.ds(c * h, h), :], local_sem)
        own.start()
        cp = pltpu.make_async_remote_copy(
            src_ref=m_ref, dst_ref=out_ref.at[pl.ds(c * h, h), :],
            send_sem=send_sems.at[0], recv_sem=recv_sems.at[0], device_id=sibling, device_id_type=MESH)
        cp.start()
        pltpu.make_async_remote_copy(
            src_ref=m_ref, dst_ref=out_ref.at[pl.ds((1 - c) * h, h), :],
            send_sem=send_sems.at[0], recv_sem=recv_sems.at[0], device_id=sibling, device_id_type=MESH).wait_recv()
        cp.wait_send()
        own.wait()

    return pl.pallas_call(
        body, name=name, in_specs=[ANY], out_specs=ANY,
        out_shape=jax.ShapeDtypeStruct((2 * h, cols), mine_half.dtype),
        scratch_shapes=[pltpu.SemaphoreType.DMA((1,)), pltpu.SemaphoreType.DMA((1,)), pltpu.SemaphoreType.DMA],
    )(mine_half)


def _kind(n):
    return 'win' if n == 'w_in' else ('row' if n in ROW_SHARDED else 'col')


def _chip_ids():
    x, y, c = _place()
    chips = [(1 - x, y), (x, 1 - y), (1 - x, 1 - y)]
    return x, y, c, 2 * x + y, chips, [2 * px + py for px, py in chips]


def gather_weights(shards, name):
    names = list(SHARDED)
    nt = len(names)
    kinds = [_kind(n) for n in names]
    shapes = [shards[n].shape for n in names]
    depth = shapes[0][0]
    half = depth // 2

    def out_shape(kind, shp):
        if kind == 'col':
            return (shp[0], shp[1], N_CHIPS * shp[2])
        if kind == 'row':
            return (shp[0], N_CHIPS * shp[1], shp[2])
        return (N_CHIPS,) + tuple(shp)

    def body(*refs):
        src, out = refs[:nt], refs[nt:2 * nt]
        send_sems, recv_sems = refs[2 * nt:]
        x, y, c, me, chips, chip_idx = _chip_ids()
        sibling = (x, y, 1 - c)

        def win(t, chip, lo, cnt):
            _, a, b = shapes[t]
            if kinds[t] == 'col':
                return out[t].at[pl.ds(lo, cnt), :, pl.ds(chip * b, b)]
            if kinds[t] == 'row':
                return out[t].at[pl.ds(lo, cnt), pl.ds(chip * a, a), :]
            return out[t].at[chip, pl.ds(lo, cnt)]

        def remote(t, k, chip, lo, to, src_ref=None):
            w = win(t, chip, lo, half)
            return pltpu.make_async_remote_copy(
                src_ref=w if src_ref is None else src_ref, dst_ref=w, send_sem=send_sems.at[7 * t + k],
                recv_sem=recv_sems.at[7 * t + k], device_id=to, device_id_type=MESH)

        def own(t):
            return pltpu.make_async_remote_copy(
                src_ref=src[t], dst_ref=win(t, me, 0, depth), send_sem=send_sems.at[7 * t + 6],
                recv_sem=recv_sems.at[7 * t + 6], device_id=sibling, device_id_type=MESH)

        mine = [own(t) for t in range(nt)]
        for cp in mine:
            cp.start()
        first = [[remote(t, j, me, c * half, (*chips[j], c), src_ref=src[t].at[pl.ds(c * half, half)])
                  for j in range(3)] for t in range(nt)]
        for t in range(nt):
            for cp in first[t]:
                cp.start()
        passed = [[remote(t, 3 + j, chip_idx[j], c * half, sibling) for j in range(3)] for t in range(nt)]
        for t in range(nt):
            for j in range(3):
                remote(t, j, chip_idx[j], c * half, (x, y, c)).wait_recv()
                passed[t][j].start()
        for t in range(nt):
            for j in range(3):
                remote(t, 3 + j, chip_idx[j], (1 - c) * half, (x, y, c)).wait_recv()
        for t in range(nt):
            for cp in first[t] + passed[t]:
                cp.wait_send()
            mine[t].wait()

    outs = pl.pallas_call(
        body, name=name, in_specs=[ANY] * nt, out_specs=[ANY] * nt,
        out_shape=[jax.ShapeDtypeStruct(out_shape(k, s), BF16) for k, s in zip(kinds, shapes)],
        scratch_shapes=[pltpu.SemaphoreType.DMA((7 * nt,)), pltpu.SemaphoreType.DMA((7 * nt,))],
    )(*[shards[n] for n in names])
    return dict(zip(names, outs))


def _half_win(ref, kind, hc, layer):
    if kind == 'col':
        hk = ref.shape[1] // 2
        return ref.at[layer, pl.ds(hc * hk, hk), :]
    if kind == 'row':
        hn = ref.shape[2] // 2
        return ref.at[layer, :, pl.ds(hc * hn, hn)]
    hk = ref.shape[2] // 2
    return ref.at[layer, :, pl.ds(hc * hk, hk), :]


def _half_shape(kind, shp):
    if kind == 'col':
        return (shp[0], shp[1] // 2, shp[2])
    if kind == 'row':
        return (shp[0], shp[1], shp[2] // 2)
    return (shp[0], shp[1], shp[2] // 2, shp[3])


def rs_sibling(grads, name):
    names = list(SHARDED)
    nt = len(names)
    kinds = [_kind(n) for n in names]
    shapes = [grads[n].shape for n in names]
    depth = shapes[0][0]

    def body(*refs):
        src, out = refs[:nt], refs[nt:2 * nt]
        send_sems, recv_sems = refs[2 * nt:]
        x, y, c = _place()
        cps = []
        for t in range(nt):
            for l in range(depth):
                cps.append(pltpu.make_async_remote_copy(
                    src_ref=_half_win(src[t], kinds[t], 1 - c, l), dst_ref=out[t].at[l],
                    send_sem=send_sems.at[depth * t + l], recv_sem=recv_sems.at[depth * t + l],
                    device_id=(x, y, 1 - c), device_id_type=MESH))
        for cp in cps:
            cp.start()
        for cp in cps:
            cp.wait()

    outs = pl.pallas_call(
        body, name=name, in_specs=[ANY] * nt, out_specs=[ANY] * nt,
        out_shape=[jax.ShapeDtypeStruct(_half_shape(k, s), F32) for k, s in zip(kinds, shapes)],
        scratch_shapes=[pltpu.SemaphoreType.DMA((depth * nt,)), pltpu.SemaphoreType.DMA((depth * nt,))],
    )(*[grads[n] for n in names])
    return dict(zip(names, outs))


EW_BLOCK_ELEMS = 512 * 1024


def rs_add(kind, g, recv, c_arr, name):
    shp = recv.shape
    rows, cols = shp[-2], shp[-1]
    tr = _pick(rows, [r for r in (1408, 1024, 704, 512, 256, 128, 64, 32, 16, 8) if r * cols <= EW_BLOCK_ELEMS])
    nb = rows // tr
    lead = (None,) * (len(shp) - 2)
    blk = pl.BlockSpec(lead + (tr, cols), lambda *a: tuple(a[:len(shp) - 2]) + (a[len(shp) - 2], 0))
    if kind == 'row':
        g_blk = pl.BlockSpec(lead + (tr, cols), lambda *a: tuple(a[:len(shp) - 2]) + (a[len(shp) - 2], a[-1][0]))
    else:
        g_blk = pl.BlockSpec(lead + (tr, cols),
                             lambda *a: tuple(a[:len(shp) - 2]) + (a[-1][0] * nb + a[len(shp) - 2], 0))

    def body(c_ref, g_ref, r_ref, o_ref):
        o_ref[...] = (g_ref[...] + r_ref[...]).astype(o_ref.dtype)

    grid_spec = pltpu.PrefetchScalarGridSpec(
        num_scalar_prefetch=1, grid=tuple(shp[:-2]) + (nb,), in_specs=[g_blk, blk], out_specs=blk)
    return pl.pallas_call(
        body, name=name, grid_spec=grid_spec, out_shape=jax.ShapeDtypeStruct(shp, BF16),
        compiler_params=_cparams(None),
    )(c_arr, g, recv)


def _chip_win(ref, kind, chip):
    if kind == 'col':
        ns = ref.shape[2] // N_CHIPS
        return ref.at[:, :, pl.ds(chip * ns, ns)]
    if kind == 'row':
        ks = ref.shape[1] // N_CHIPS
        return ref.at[:, pl.ds(chip * ks, ks), :]
    return ref.at[:, chip]


def _chip_shape(kind, shp):
    if kind == 'col':
        return (shp[0], shp[1], shp[2] // N_CHIPS)
    if kind == 'row':
        return (shp[0], shp[1] // N_CHIPS, shp[2])
    return (shp[0], shp[2], shp[3])


def rs_chips(parts, name):
    names = list(SHARDED)
    nt = len(names)
    kinds = [_kind(n) for n in names]
    shapes = [parts[n].shape for n in names]

    def body(*refs):
        src, out = refs[:nt], refs[nt:2 * nt]
        send_sems, recv_sems, local_sems = refs[2 * nt:]
        x, y, c, me, chips, chip_idx = _chip_ids()
        mine = [pltpu.make_async_copy(_chip_win(src[t], kinds[t], me), out[t].at[me], local_sems.at[t])
                for t in range(nt)]
        for cp in mine:
            cp.start()
        cps = [[pltpu.make_async_remote_copy(
            src_ref=_chip_win(src[t], kinds[t], chip_idx[j]), dst_ref=out[t].at[me],
            send_sem=send_sems.at[3 * t + j], recv_sem=recv_sems.at[3 * t + j],
            device_id=(*chips[j], c), device_id_type=MESH) for j in range(3)] for t in range(nt)]
        for t in range(nt):
            for cp in cps[t]:
                cp.start()
        for t in range(nt):
            for j in range(3):
                pltpu.make_async_remote_copy(
                    src_ref=_chip_win(src[t], kinds[t], me), dst_ref=out[t].at[chip_idx[j]],
                    send_sem=send_sems.at[3 * t + j], recv_sem=recv_sems.at[3 * t + j],
                    device_id=(*chips[j], c), device_id_type=MESH).wait_recv()
        for t in range(nt):
            for cp in cps[t]:
                cp.wait_send()
            mine[t].wait()

    outs = pl.pallas_call(
        body, name=name, in_specs=[ANY] * nt, out_specs=[ANY] * nt,
        out_shape=[jax.ShapeDtypeStruct((N_CHIPS,) + _chip_shape(k, s), parts[n].dtype)
                   for n, k, s in zip(names, kinds, shapes)],
        scratch_shapes=[pltpu.SemaphoreType.DMA((3 * nt,)), pltpu.SemaphoreType.DMA((3 * nt,)),
                        pltpu.SemaphoreType.DMA((nt,))],
    )(*[parts[n] for n in names])
    return dict(zip(names, outs))


def rs_sum(slots, name):
    n, depth, r, cols = slots.shape
    tr = _pick(r, [q for q in (1408, 1024, 704, 512, 256, 128, 64, 32, 16, 8) if q * cols * n <= 2 * EW_BLOCK_ELEMS])

    def body(a_ref, o_ref):
        acc = a_ref[0].astype(F32)
        for k in range(1, n):
            acc = acc + a_ref[k].astype(F32)
        o_ref[...] = acc

    return pl.pallas_call(
        body, name=name, grid=(depth, r // tr),
        in_specs=[pl.BlockSpec((n, None, tr, cols), lambda l, i: (0, l, i, 0))],
        out_specs=pl.BlockSpec((None, tr, cols), lambda l, i: (l, i, 0)),
        out_shape=jax.ShapeDtypeStruct((depth, r, cols), F32), compiler_params=_cparams(("parallel", "parallel")),
    )(slots)


def rs_join(reds, name):
    names = list(SHARDED)
    nt = len(names)
    kinds = [_kind(n) for n in names]
    shapes = [reds[n].shape for n in names]
    depth = shapes[0][0]

    def full_shape(kind, shp):
        if kind == 'row':
            return (shp[0], shp[1], 2 * shp[2])
        return (shp[0], 2 * shp[1], shp[2])

    def win(ref, kind, hc, layer):
        if kind == 'row':
            hn = ref.shape[2] // 2
            return ref.at[layer, :, pl.ds(hc * hn, hn)]
        hk = ref.shape[1] // 2
        return ref.at[layer, pl.ds(hc * hk, hk), :]

    def body(*refs):
        src, out = refs[:nt], refs[nt:2 * nt]
        send_sems, recv_sems, local_sems = refs[2 * nt:]
        x, y, c = _place()
        own, cps = [], []
        for t in range(nt):
            for l in range(depth):
                i = depth * t + l
                own.append(pltpu.make_async_copy(src[t].at[l], win(out[t], kinds[t], c, l), local_sems.at[i]))
                cps.append(pltpu.make_async_remote_copy(
                    src_ref=src[t].at[l], dst_ref=win(out[t], kinds[t], c, l), send_sem=send_sems.at[i],
                    recv_sem=recv_sems.at[i], device_id=(x, y, 1 - c), device_id_type=MESH))
        for cp in own + cps:
            cp.start()
        for t in range(nt):
            for l in range(depth):
                i = depth * t + l
                pltpu.make_async_remote_copy(
                    src_ref=src[t].at[l], dst_ref=win(out[t], kinds[t], 1 - c, l), send_sem=send_sems.at[i],
                    recv_sem=recv_sems.at[i], device_id=(x, y, 1 - c), device_id_type=MESH).wait_recv()
        for cp in cps:
            cp.wait_send()
        for cp in own:
            cp.wait()

    outs = pl.pallas_call(
        body, name=name, in_specs=[ANY] * nt, out_specs=[ANY] * nt,
        out_shape=[jax.ShapeDtypeStruct(full_shape(k, s), F32) for k, s in zip(kinds, shapes)],
        scratch_shapes=[pltpu.SemaphoreType.DMA((depth * nt,)), pltpu.SemaphoreType.DMA((depth * nt,)),
                        pltpu.SemaphoreType.DMA((depth * nt,))],
    )(*[reds[n] for n in names])
    return dict(zip(names, outs))


def rs_chips2(parts, name):
    names = list(SHARDED)
    nt = len(names)
    kinds = [_kind(n) for n in names]
    shapes = [parts[n].shape for n in names]

    def body(*refs):
        src, out = refs[:nt], refs[nt:2 * nt]
        send_sems, recv_sems = refs[2 * nt:]
        x, y, c, me, chips, chip_idx = _chip_ids()
        cps = [[pltpu.make_async_remote_copy(
            src_ref=_chip_win(src[t], kinds[t], chip_idx[j]), dst_ref=out[t].at[j],
            send_sem=send_sems.at[3 * t + j], recv_sem=recv_sems.at[3 * t + j],
            device_id=(*chips[j], c), device_id_type=MESH) for j in range(3)] for t in range(nt)]
        for t in range(nt):
            for cp in cps[t]:
                cp.start()
        for t in range(nt):
            for cp in cps[t]:
                cp.wait()

    outs = pl.pallas_call(
        body, name=name, in_specs=[ANY] * nt, out_specs=[ANY] * nt,
        out_shape=[jax.ShapeDtypeStruct((3,) + _chip_shape(k, s), parts[n].dtype)
                   for n, k, s in zip(names, kinds, shapes)],
        scratch_shapes=[pltpu.SemaphoreType.DMA((3 * nt,)), pltpu.SemaphoreType.DMA((3 * nt,))],
    )(*[parts[n] for n in names])
    return dict(zip(names, outs))


def rs_sum2(kind, part, slots, sc_arr, name):
    _, depth, r, cols = slots.shape
    tr = _pick(r, [q for q in (1408, 1024, 704, 512, 256, 128, 64, 32, 16) if q * cols <= EW_BLOCK_ELEMS // 2])
    nb = r // tr
    if kind == 'col':
        own_blk = pl.BlockSpec((None, tr, cols), lambda l, i, sc: (l, i, sc[0]))
        out_blk = pl.BlockSpec((None, tr, cols), lambda l, i, sc: (l, sc[1] * nb + i, 0))
        out_shape = (depth, 2 * r, cols)
    elif kind == 'row':
        own_blk = pl.BlockSpec((None, tr, cols), lambda l, i, sc: (l, sc[0] * nb + i, 0))
        out_blk = pl.BlockSpec((None, tr, cols), lambda l, i, sc: (l, i, sc[1]))
        out_shape = (depth, r, 2 * cols)
    else:
        own_blk = pl.BlockSpec((None, None, tr, cols), lambda l, i, sc: (l, sc[0], i, 0))
        out_blk = pl.BlockSpec((None, tr, cols), lambda l, i, sc: (l, sc[1] * nb + i, 0))
        out_shape = (depth, 2 * r, cols)

    def body(sc_ref, own_ref, s_ref, o_ref):
        acc = own_ref[...].astype(F32)
        for k in range(3):
            acc = acc + s_ref[k].astype(F32)
        o_ref[...] = acc

    grid_spec = pltpu.PrefetchScalarGridSpec(
        num_scalar_prefetch=1, grid=(depth, nb),
        in_specs=[own_blk, pl.BlockSpec((3, None, tr, cols), lambda l, i, sc: (0, l, i, 0))], out_specs=out_blk)
    return pl.pallas_call(
        body, name=name, grid_spec=grid_spec, out_shape=jax.ShapeDtypeStruct(out_shape, F32),
        compiler_params=_cparams(None),
    )(sc_arr, part, slots)


def rs_join2(halves, name):
    names = list(SHARDED)
    nt = len(names)
    kinds = [_kind(n) for n in names]
    shapes = [halves[n].shape for n in names]
    depth = shapes[0][0]

    def win(ref, kind, hc, layer):
        if kind == 'row':
            hn = ref.shape[2] // 2
            return ref.at[layer, :, pl.ds(hc * hn, hn)]
        hk = ref.shape[1] // 2
        return ref.at[layer, pl.ds(hc * hk, hk), :]

    def body(*refs):
        src, out = refs[:nt], refs[nt:2 * nt]
        send_sems, recv_sems = refs[2 * nt:]
        x, y, c = _place()
        cps = []
        for t in range(nt):
            for l in range(depth):
                i = depth * t + l
                cps.append(pltpu.make_async_remote_copy(
                    src_ref=win(src[t], kinds[t], c, l), dst_ref=win(out[t], kinds[t], c, l),
                    send_sem=send_sems.at[i], recv_sem=recv_sems.at[i], device_id=(x, y, 1 - c),
                    device_id_type=MESH))
        for cp in cps:
            cp.start()
        for t in range(nt):
            for l in range(depth):
                i = depth * t + l
                pltpu.make_async_remote_copy(
                    src_ref=win(src[t], kinds[t], c, l), dst_ref=win(out[t], kinds[t], 1 - c, l),
                    send_sem=send_sems.at[i], recv_sem=recv_sems.at[i], device_id=(x, y, 1 - c),
                    device_id_type=MESH).wait_recv()
        for cp in cps:
            cp.wait_send()

    outs = pl.pallas_call(
        body, name=name, in_specs=[ANY] * nt, out_specs=[ANY] * nt,
        out_shape=[jax.ShapeDtypeStruct(s, F32) for s in shapes],
        input_output_aliases={t: t for t in range(nt)},
        scratch_shapes=[pltpu.SemaphoreType.DMA((depth * nt,)), pltpu.SemaphoreType.DMA((depth * nt,))],
    )(*[halves[n] for n in names])
    return dict(zip(names, outs))


def gather_small(v, name):
    r, cols = v.shape

    def body(v_ref, out_ref, send_sems, recv_sems):
        x, y, c = _place()
        me = 4 * x + 2 * y + c
        out_ref[me] = v_ref[...]
        cps = []
        for rel in range(1, N_DEV):
            px = 1 - x if (rel >> 2) & 1 else x
            py = 1 - y if (rel >> 1) & 1 else y
            pc = 1 - c if rel & 1 else c
            cps.append(pltpu.make_async_remote_copy(
                src_ref=v_ref, dst_ref=out_ref.at[me], send_sem=send_sems.at[rel - 1],
                recv_sem=recv_sems.at[rel - 1], device_id=(px, py, pc), device_id_type=MESH))
        for cp in cps:
            cp.start()
        for cp in cps:
            cp.wait()

    return pl.pallas_call(
        body, name=name, in_specs=[pl.BlockSpec(memory_space=pltpu.VMEM)],
        out_specs=pl.BlockSpec(memory_space=pltpu.VMEM),
        out_shape=jax.ShapeDtypeStruct((N_DEV, r, cols), v.dtype),
        scratch_shapes=[pltpu.SemaphoreType.DMA((N_DEV - 1,)), pltpu.SemaphoreType.DMA((N_DEV - 1,))],
        compiler_params=pltpu.CompilerParams(vmem_limit_bytes=VMEM_LIMIT),
    )(v)


def _pad_rows(flat, row_align):
    n = flat.shape[-1]
    unit = PACK_COLS * row_align
    tot = -(-n // unit) * unit
    pad = [(0, 0)] * (flat.ndim - 1) + [(0, tot - n)]
    return jnp.pad(flat, pad)


def _pack_shards(ws):
    flat = jnp.concatenate([ws[n].astype(BF16).reshape(-1) for n in SHARDED])
    return _pad_rows(flat, PACK_ROW_ALIGN).reshape(-1, PACK_COLS)


def _unpack_full(gathered, shard_shapes):
    flat = gathered.reshape(N_CHIPS, -1)
    out, off = {}, 0
    for n in SHARDED:
        shp = shard_shapes[n]
        size = math.prod(shp)
        seg = flat[:, off:off + size].reshape((N_CHIPS,) + tuple(shp))
        off += size
        if n in ROW_SHARDED:
            out[n] = jnp.transpose(seg, (1, 0, 2, 3)).reshape(shp[0], N_CHIPS * shp[1], shp[2])
        else:
            out[n] = jnp.transpose(seg, (1, 2, 0, 3)).reshape(shp[0], shp[1], N_CHIPS * shp[2])
    return out


def _pack_grads(gfull, shard_shapes):
    segs = []
    for n in SHARDED:
        shp = shard_shapes[n]
        g = gfull[n]
        if n in ROW_SHARDED:
            seg = jnp.transpose(g.reshape(shp[0], N_CHIPS, shp[1], shp[2]), (1, 0, 2, 3))
        else:
            seg = jnp.transpose(g.reshape(shp[0], shp[1], N_CHIPS, shp[2]), (2, 0, 1, 3))
        segs.append(seg.reshape(N_CHIPS, -1))
    flat = jnp.concatenate(segs, axis=1)
    return _pad_rows(flat, PACK_ROW_ALIGN).reshape(N_CHIPS, -1, PACK_COLS)


def _unpack_shard_grads(red, shard_shapes):
    flat = red.reshape(-1)
    out, off = {}, 0
    for n in SHARDED:
        shp = shard_shapes[n]
        size = math.prod(shp)
        out[n] = flat[off:off + size].reshape(shp)
        off += size
    return out


def _pack_small(parts):
    flat = jnp.concatenate([p.astype(F32).reshape(-1) for p in parts])
    return _pad_rows(flat, 8).reshape(-1, PACK_COLS)


def _unpack_small(flat2d, shapes):
    flat = flat2d.reshape(-1)
    out, off = [], 0
    for shp in shapes:
        size = math.prod(shp)
        out.append(flat[off:off + size].reshape(shp))
        off += size
    return out


def _heads(a, bn, s, h):
    return jnp.transpose(a.reshape(bn, s, h, HEAD_DIM), (0, 2, 1, 3))


def _unheads(a):
    bn, h, s, d = a.shape
    return jnp.transpose(a, (0, 2, 1, 3)).reshape(bn * s, h * d)


def _reorder_w_in(w):
    d = w.shape[0]
    return jnp.concatenate([w[:, 2820:6916], w[:, 0:768], w[:, 772:1540], w[:, 1540:2052], w[:, 2052:2820],
                            w[:, 768:772], jnp.zeros((d, N_PROJ - 6916), w.dtype)], axis=1)


def _restore_dw_in(g):
    return jnp.concatenate([g[:, 4096:4864], g[:, 6912:6916], g[:, 4864:5632], g[:, 5632:6144], g[:, 6144:6912],
                            g[:, 0:4096]], axis=1)


def _ffn_fwd(h, g_pre, w_gu, w_down, g_post, tag):
    n = rms_fwd(h, g_pre, f"{tag}_rms")
    gu = _mm(n, w_gu, out_dtype=BF16, name=f"{tag}_mm_gu")
    a = swiglu_fwd(gu, f"{tag}_swiglu")
    f = _mm(a, w_down, out_dtype=F32, name=f"{tag}_mm_down")
    h_out = res_rms_fwd(h, f, g_post, 0.5, f"{tag}_res")
    return h_out, (h, n, gu, a, f)


def _ffn_bwd(dh, saved, g_pre, w_gu, w_down, g_post, tag, dw, n_gu, n_down):
    h, n, gu, a, f = saved
    df, dg_post = res_rms_bwd(f, g_post, dh, 0.5, f"{tag}_res_bwd")
    da = _mm(df, w_down, tb=True, out_dtype=BF16, name=f"{tag}_mm_da")
    dw(n_down, a, df, f"{tag}_mm_dwdown")
    dgu = swiglu_bwd(gu, da, f"{tag}_swiglu_bwd")
    dw(n_gu, n, dgu, f"{tag}_mm_dwgu")
    dn = _mm(dgu, w_gu, tb=True, out_dtype=BF16, name=f"{tag}_mm_dn")
    dh_in, dg_pre = rms_bwd(h, g_pre, dn, dh, f"{tag}_rms_bwd")
    return dh_in, dg_pre, dg_post


def kernel(x, p, ffn1_norm_pre, ffn1_w_gu, ffn1_w_down, ffn1_norm_post, mix_norm_pre, w_in, b_forget, b_gate, conv_short, conv_dw, conv_dw_bias, conv_ln_gain, conv_ln_bias, attn_sinks, rel_bias, w_br_a, w_br_b, w_br_c, w_br_d, w_o, mix_norm_post, ffn2_norm_pre, ffn2_w_gu, ffn2_w_down, ffn2_norm_post, ple_norm_gate, w_ple_gate, w_ple, ple_norm_post, loss_target, m_ffn1_norm_pre, m_ffn1_w_gu, m_ffn1_w_down, m_ffn1_norm_post, m_mix_norm_pre, m_w_in, m_b_forget, m_b_gate, m_conv_short, m_conv_dw, m_conv_dw_bias, m_conv_ln_gain, m_conv_ln_bias, m_attn_sinks, m_rel_bias, m_w_br_a, m_w_br_b, m_w_br_c, m_w_br_d, m_w_o, m_mix_norm_post, m_ffn2_norm_pre, m_ffn2_w_gu, m_ffn2_w_down, m_ffn2_norm_post, m_ple_norm_gate, m_w_ple_gate, m_w_ple, m_ple_norm_post, v_ffn1_norm_pre, v_ffn1_w_gu, v_ffn1_w_down, v_ffn1_norm_post, v_mix_norm_pre, v_w_in, v_b_forget, v_b_gate, v_conv_short, v_conv_dw, v_conv_dw_bias, v_conv_ln_gain, v_conv_ln_bias, v_attn_sinks, v_rel_bias, v_w_br_a, v_w_br_b, v_w_br_c, v_w_br_d, v_w_o, v_mix_norm_post, v_ffn2_norm_pre, v_ffn2_w_gu, v_ffn2_w_down, v_ffn2_norm_post, v_ple_norm_gate, v_w_ple_gate, v_w_ple, v_ple_norm_post):
    args = dict(locals())
    ws = {n: args[n] for n in WEIGHTS}
    ms = {n: args["m_" + n] for n in WEIGHTS}
    vs = {n: args["v_" + n] for n in WEIGHTS}
    return _step(x, p, loss_target, ws, ms, vs)


def _local(x, p, loss_target, ws, wf, w_short, w_dw):
    bn, s, d = x.shape
    t = bn * s
    depth = w_short.shape[0]

    def vec(a, i):
        return a[i].reshape(1, -1)

    bucket = _bucket_table()
    band_bias = band_bias_fwd(bucket, ws['rel_bias'], "band_bias")

    h = x.reshape(t, d)
    saved = []
    for i in range(depth):
        sv = {}
        h, sv['ffn1'] = _ffn_fwd(h, vec(ws['ffn1_norm_pre'], i), wf['ffn1_w_gu'][i], wf['ffn1_w_down'][i],
                                 vec(ws['ffn1_norm_post'], i), f"l{i}_ffn1")
        h1 = h
        u = rms_fwd(h1, vec(ws['mix_norm_pre'], i), f"l{i}_mix_rms")
        w_in_r = _reorder_w_in(wf['w_in'][i])
        proj = _mm(u, w_in_r, out_dtype=BF16, name=f"l{i}_mm_proj")
        bf = jnp.pad(vec(ws['b_forget'], i), ((0, 0), (0, 128 - A_HEADS)))
        cc = fgate_fwd(proj, bf, bn, s, f"l{i}_fgate")
        c4 = jnp.transpose(cc.reshape(bn, s, 128)[:, :, :A_HEADS], (0, 2, 1))
        c_col = c4[..., None]
        c_row = c4.reshape(bn, A_HEADS, s // FOX_T, 1, FOX_T)
        ya, oa, lse_a = fox2_fwd(proj, c_col, c_row, bn, s, f"l{i}_fox")
        w_sh = jnp.pad(w_short[i], ((0, 8 - w_short.shape[1]), (0, 0)))
        w_cv = jnp.pad(w_dw[i], ((0, 32 - w_dw.shape[1]), (0, 0)))
        yb = convb_fwd(proj, w_sh[:3], bn, s, f"l{i}_convb")
        cvec = (vec(ws['conv_dw_bias'], i), vec(ws['conv_ln_gain'], i), vec(ws['conv_ln_bias'], i))
        yc = convc_fwd(proj, w_cv[:31], *cvec, bn, s, f"l{i}_convc")
        sinks = jnp.broadcast_to(ws['attn_sinks'][i].reshape(D_Q_HEADS, 1, 1), (D_Q_HEADS, QB, 1))
        yd, lse_d = swa3_fwd(proj, band_bias, sinks, bn, s, f"l{i}_swa")
        ys = (ya, yb, yc, yd)
        wbr = (wf['w_br_a'][i], wf['w_br_b'][i], wf['w_br_c'][i], wf['w_br_d'][i])
        zs = [_mm(yk, wk, out_dtype=BF16, name=f"l{i}_mm_br{k}") for k, (yk, wk) in enumerate(zip(ys, wbr))]
        bgs = [ws['b_gate'][i, k * d:(k + 1) * d].reshape(1, d) for k in range(4)]
        merged = merge_fwd(proj, zs, bgs, f"l{i}_merge")
        mo = _mm(merged, wf['w_o'][i], out_dtype=F32, name=f"l{i}_mm_o")
        h2 = res_rms_fwd(h1, mo, vec(ws['mix_norm_post'], i), 1.0, f"l{i}_mix_res")
        sv['mix'] = dict(h1=h1, u=u, proj=proj, w_in_r=w_in_r, bf=bf, c_col=c_col, c_row=c_row, oa=oa, lse_a=lse_a,
                         w_sh=w_sh, w_cv=w_cv, cvec=cvec, sinks=sinks, lse_d=lse_d, ys=ys, wbr=wbr, zs=zs, bgs=bgs,
                         merged=merged, mo=mo)
        h, sv['ffn2'] = _ffn_fwd(h2, vec(ws['ffn2_norm_pre'], i), wf['ffn2_w_gu'][i], wf['ffn2_w_down'][i],
                                 vec(ws['ffn2_norm_post'], i), f"l{i}_ffn2")
        h3 = h
        ng = rms_fwd(h3, vec(ws['ple_norm_gate'], i), f"l{i}_ple_rms")
        pgl = _mm(ng, wf['w_ple_gate'][i], out_dtype=BF16, name=f"l{i}_mm_pgl")
        p_i = p[i].reshape(t, -1)
        pr = _mm(p_i, wf['w_ple'][i], out_dtype=F32, name=f"l{i}_mm_pr")
        h = ple_fwd(h3, pgl, pr, vec(ws['ple_norm_post'], i), f"l{i}_ple")
        sv['ple'] = dict(h3=h3, ng=ng, pgl=pgl, p_i=p_i, pr=pr)
        saved.append(sv)

    dh, loss_vec = loss_fwd_bwd(h, loss_target.reshape(t, d), "loss")
    loss_part = jnp.sum(loss_vec)

    gst = {}
    gwin = [None] * depth

    def dw(n, a, b, nm):
        gst[n] = _mm(a, b, ta=True, name=nm, stack=(gst.get(n), depth, i))

    gsmall = {n: [None] * depth for n in REPLICATED + CONV_SHARDED if n != 'rel_bias'}
    dbias_layers = []
    for i in reversed(range(depth)):
        sv = saved[i]
        pv = sv['ple']
        dpgl, dpr, dg = ple_bwd(pv['pgl'], pv['pr'], vec(ws['ple_norm_post'], i), dh, f"l{i}_ple_bwd")
        gsmall['ple_norm_post'][i] = dg
        dw('w_ple', pv['p_i'], dpr, f"l{i}_mm_dwple")
        dw('w_ple_gate', pv['ng'], dpgl, f"l{i}_mm_dwpg")
        dng = _mm(dpgl, wf['w_ple_gate'][i], tb=True, out_dtype=BF16, name=f"l{i}_mm_dng")
        dh, gsmall['ple_norm_gate'][i] = rms_bwd(pv['h3'], vec(ws['ple_norm_gate'], i), dng, dh,
                                                  f"l{i}_ple_rms_bwd")
        dh, gsmall['ffn2_norm_pre'][i], gsmall['ffn2_norm_post'][i] = _ffn_bwd(
            dh, sv['ffn2'], vec(ws['ffn2_norm_pre'], i), wf['ffn2_w_gu'][i], wf['ffn2_w_down'][i],
            vec(ws['ffn2_norm_post'], i), f"l{i}_ffn2", dw, 'ffn2_w_gu', 'ffn2_w_down')
        mv = sv['mix']
        dmo, gsmall['mix_norm_post'][i] = res_rms_bwd(mv['mo'], vec(ws['mix_norm_post'], i), dh, 1.0,
                                                      f"l{i}_mix_res_bwd")
        dw('w_o', mv['merged'], dmo, f"l{i}_mm_dwo")
        dmerged = _mm(dmo, wf['w_o'][i], tb=True, out_dtype=BF16, name=f"l{i}_mm_dmerged")
        mb = merge_bwd(mv['proj'], mv['zs'], mv['bgs'], dmerged, f"l{i}_merge_bwd")
        dgates, dzs, dbg = mb[0:4], mb[4:8], mb[8:12]
        gsmall['b_gate'][i] = jnp.concatenate(dbg, axis=1)
        dys = []
        for k, nm in enumerate(('w_br_a', 'w_br_b', 'w_br_c', 'w_br_d')):
            dw(nm, mv['ys'][k], dzs[k], f"l{i}_mm_dwbr{k}")
            dys.append(_mm(dzs[k], mv['wbr'][k], tb=True, out_dtype=BF16, name=f"l{i}_mm_dy{k}"))
        dqa, dka, dva, dck, dcq = fox2_bwd(mv['proj'], mv['c_col'], mv['c_row'], mv['oa'], mv['lse_a'], dys[0], bn, s,
                                           f"l{i}_fox_bwd")
        dc = jnp.transpose(dck.reshape(bn, A_HEADS, s) + dcq.reshape(bn, A_HEADS, s), (0, 2, 1))
        dc = jnp.pad(dc, ((0, 0), (0, 0), (0, 128 - A_HEADS))).reshape(t, 128)
        daf, dbf = fgate_bwd(mv['proj'], mv['bf'], dc, bn, s, f"l{i}_fgate_bwd")
        gsmall['b_forget'][i] = dbf[:, :A_HEADS]
        dbg_, dcg_, dxb_, dwsh = convb_bwd(mv['proj'], mv['w_sh'][:3], dys[1], bn, s, f"l{i}_convb_bwd")
        gsmall['conv_short'][i] = dwsh[:3]
        dca, dcb, dwcv, dcbias, dlg, dlb = convc_bwd(mv['proj'], mv['w_cv'][:31], *mv['cvec'], dys[2], bn, s,
                                                     f"l{i}_convc_bwd")
        gsmall['conv_dw'][i] = dwcv[:31]
        gsmall['conv_dw_bias'][i] = dcbias
        gsmall['conv_ln_gain'][i] = dlg
        gsmall['conv_ln_bias'][i] = dlb
        dqd, dkd, dvd, dbias, dsink = swa3_bwd(mv['proj'], band_bias, mv['sinks'], mv['ys'][3], mv['lse_d'], dys[3],
                                               bn, s, f"l{i}_swa_bwd")
        dbias_layers.append(dbias)
        gsmall['attn_sinks'][i] = jnp.sum(dsink, axis=(1, 2))
        dproj = jnp.concatenate(
            list(dgates)
            + [dqa, dka, dva, dbg_, dcg_, dxb_, dca, dcb, dqd, dkd[0], dkd[1], dvd[0], dvd[1]]
            + [daf, jnp.zeros((t, N_PROJ - C_AF - 128), BF16)], axis=1)
        dwin = _restore_dw_in(_mm(mv['u'], dproj, ta=True, name=f"l{i}_mm_dwin"))
        gwin[i] = jnp.transpose(dwin.reshape(d, N_CHIPS, -1), (1, 0, 2))
        du = _mm(dproj, mv['w_in_r'], tb=True, out_dtype=BF16, name=f"l{i}_mm_du")
        dh, gsmall['mix_norm_pre'][i] = rms_bwd(mv['h1'], vec(ws['mix_norm_pre'], i), du, dh, f"l{i}_mix_rms_bwd")
        dh, gsmall['ffn1_norm_pre'][i], gsmall['ffn1_norm_post'][i] = _ffn_bwd(
            dh, sv['ffn1'], vec(ws['ffn1_norm_pre'], i), wf['ffn1_w_gu'][i], wf['ffn1_w_down'][i],
            vec(ws['ffn1_norm_post'], i), f"l{i}_ffn1", dw, 'ffn1_w_gu', 'ffn1_w_down')
    grad_x = dh.reshape(bn, s, d)

    drel = band_bias_bwd(bucket, dbias_layers, "band_bias_bwd")
    gst['w_in'] = jnp.stack(gwin)
    full_shapes = {n: ws[n].shape for n in REPLICATED}
    full_shapes['conv_short'] = w_short.shape
    full_shapes['conv_dw'] = w_dw.shape
    gs = {n: jnp.stack([a.reshape(full_shapes[n][1:]) for a in gsmall[n]]) for n in gsmall}
    gs['rel_bias'] = jnp.transpose(drel[:, :, 0])
    return loss_part, grad_x, gst, gs


def _step(x, p, loss_target, ws, ms, vs):
    chip = 2 * lax.axis_index("x") + lax.axis_index("y")

    wf = gather_weights({n: ws[n].astype(BF16) for n in SHARDED}, "gather_weights")
    w_in_all = wf['w_in']
    wf['w_in'] = jnp.transpose(w_in_all, (1, 2, 0, 3)).reshape(w_in_all.shape[1], w_in_all.shape[2], -1)
    conv_shapes = [ws[n].shape for n in CONV_SHARDED]
    conv_all = gather_small(_pack_small([ws[n] for n in CONV_SHARDED]), "gather_conv")
    conv_full = []
    for idx, n in enumerate(CONV_SHARDED):
        per_chip = [_unpack_small(conv_all[2 * j], conv_shapes)[idx] for j in range(N_CHIPS)]
        conv_full.append(jnp.concatenate(per_chip, axis=-1))
    w_short, w_dw = conv_full

    loss_part, grad_x, gst, gs = _local(x, p, loss_target, {n: ws[n] for n in REPLICATED}, wf, w_short, w_dw)

    c_arr = lax.axis_index("c").astype(jnp.int32).reshape(1)
    recv = rs_sibling(gst, "rs_sibling")
    chip_sum = {n: rs_add(_kind(n), gst[n], recv[n], c_arr, f"rs_add_{n}") for n in SHARDED}
    slots = rs_chips2(chip_sum, "rs_chips")
    sc_arr = jnp.stack([chip, lax.axis_index("c")]).astype(jnp.int32)
    red_half = {n: rs_sum2(_kind(n), chip_sum[n], slots[n], sc_arr, f"rs_sum_{n}") for n in SHARDED}
    g_shard = rs_join2(red_half, "rs_join")

    small_names = [n for n in REPLICATED + CONV_SHARDED]
    small_parts = [gs[n] for n in small_names]
    small_shapes = [g.shape for g in small_parts]
    small_parts.append(loss_part.reshape(1))
    small_shapes.append((1,))
    small_all = gather_small(_pack_small(small_parts), "gather_small")
    small_red = sum_slots(small_all, "small_sum")
    small_g = _unpack_small(small_red, small_shapes)
    loss = small_g[-1].reshape(())
    g_small = dict(zip(small_names, small_g[:-1]))

    grads = {}
    for n in WEIGHTS:
        if n in SHARDED:
            grads[n] = g_shard[n]
        elif n in CONV_SHARDED:
            wdt = ws[n].shape[-1]
            grads[n] = lax.dynamic_slice_in_dim(g_small[n], chip * wdt, wdt, axis=2)
        else:
            grads[n] = g_small[n]

    deltas, new_m, new_v = {}, {}, {}
    small_upd = [n for n in WEIGHTS if n not in SHARDED]
    for n in SHARDED:
        deltas[n], new_m[n], new_v[n] = adamw(ws[n], grads[n], ms[n], vs[n], f"adamw_{n}")
    shapes_u = [ws[n].shape for n in small_upd]
    packs = [_pack_small([src[n] for n in small_upd]) for src in (ws, grads, ms, vs)]
    upd = adamw(*packs, "adamw_small")
    for res, dst in zip(upd, (deltas, new_m, new_v)):
        for n, a in zip(small_upd, _unpack_small(res, shapes_u)):
            dst[n] = a

    return (loss, grad_x, *[grads[n] for n in WEIGHTS], *[deltas[n] for n in WEIGHTS],
            *[new_m[n] for n in WEIGHTS], *[new_v[n] for n in WEIGHTS])
```

```python
import functools
import math

import jax
import jax.numpy as jnp
from jax import lax
from jax.experimental import pallas as pl
from jax.experimental.pallas import tpu as pltpu

F32 = jnp.float32
BF16 = jnp.bfloat16
MESH = pl.DeviceIdType.MESH

D_MODEL = 1024
DEPTH = 4
HEAD_DIM = 64
A_HEADS = 4
D_Q_HEADS = 8
D_KV_HEADS = 2
D_GROUP = 4
WINDOW = 128
QB = 128
REL_BUCKETS = 32
REL_MAX_DIST = 128
D_FF = 2816
EPS = 1e-6
NEG = -1e30
SCALE = HEAD_DIM ** -0.5
N_CHIPS = 4
N_DEV = 8

ADAM_LR = 0.001
ADAM_B1 = 0.9
ADAM_B2 = 0.999
ADAM_EPS = 1e-08
ADAM_WD = 0.01
ADAM_STEP = 10

C_GATE = 0
C_AQ, C_AK, C_AV = 4096, 4352, 4608
C_BG, C_CG, C_XB = 4864, 5120, 5376
C_CA, C_CB = 5632, 5888
C_DQ, C_DK, C_DV = 6144, 6656, 6784
C_AF = 6912
N_PROJ = 7168

VMEM_LIMIT = 56 * 1024 * 1024
PACK_COLS = 1024
PACK_ROW_ALIGN = 1024

SHARDED = ('ffn1_w_gu', 'ffn1_w_down', 'w_in', 'w_br_a', 'w_br_b', 'w_br_c', 'w_br_d', 'w_o',
           'ffn2_w_gu', 'ffn2_w_down', 'w_ple_gate', 'w_ple')
ROW_SHARDED = ('ffn1_w_down', 'w_o', 'ffn2_w_down', 'w_ple_gate')
CONV_SHARDED = ('conv_short', 'conv_dw')
REPLICATED = ('ffn1_norm_pre', 'ffn1_norm_post', 'mix_norm_pre', 'b_forget', 'b_gate', 'conv_dw_bias',
              'conv_ln_gain', 'conv_ln_bias', 'attn_sinks', 'rel_bias', 'mix_norm_post', 'ffn2_norm_pre',
              'ffn2_norm_post', 'ple_norm_gate', 'ple_norm_post')
WEIGHTS = ('ffn1_norm_pre', 'ffn1_w_gu', 'ffn1_w_down', 'ffn1_norm_post', 'mix_norm_pre', 'w_in', 'b_forget',
           'b_gate', 'conv_short', 'conv_dw', 'conv_dw_bias', 'conv_ln_gain', 'conv_ln_bias', 'attn_sinks',
           'rel_bias', 'w_br_a', 'w_br_b', 'w_br_c', 'w_br_d', 'w_o', 'mix_norm_post', 'ffn2_norm_pre',
           'ffn2_w_gu', 'ffn2_w_down', 'ffn2_norm_post', 'ple_norm_gate', 'w_ple_gate', 'w_ple', 'ple_norm_post')


def _cparams(sem=None):
    return pltpu.CompilerParams(dimension_semantics=sem, vmem_limit_bytes=VMEM_LIMIT)


def _pick(dim, cands):
    for c in cands:
        if dim % c == 0:
            return c
    return dim


MM_VMEM_BUDGET = 40 * 1024 * 1024
MXU_FLOPS = 9.0e14
HBM_BYTES_PER_S = 3.0e12
GRID_STEP_S = 0.35e-6


def _divisors(dim, cands):
    out = [c for c in cands if c <= dim and dim % c == 0]
    return out or [dim]


def _mm_tiles(m, n, k, ab, bb, ob):
    best = None
    for tm in _divisors(m, (2048, 1408, 1024, 512, 256, 128)):
        for tn in _divisors(n, (2816, 2048, 1792, 1408, 1024, 512, 256, 128)):
            for tk in _divisors(k, (k if k <= 2048 else 2816, 2816, 2048, 1792, 1408, 1024, 512, 256, 128)):
                nk = k // tk
                vmem = 2 * (tm * tk * ab + tk * tn * bb + tm * tn * ob) + tm * tn * 4 * (2 if nk > 1 else 1)
                if vmem > MM_VMEM_BUDGET:
                    continue
                steps = (m // tm) * (n // tn) * nk
                a_bytes = m * k * ab * (1 if nk == 1 else n // tn)
                b_bytes = k * n * bb * (1 if (nk == 1 and n == tn) else m // tm)
                mem = (a_bytes + b_bytes + m * n * ob) / HBM_BYTES_PER_S
                acc = steps * tm * tn * 1.5e-12 if nk > 1 else 0.0
                cost = steps * GRID_STEP_S + max(2.0 * m * n * k / MXU_FLOPS, mem) + acc
                if best is None or cost < best[0]:
                    best = (cost, tm, tn, tk)
    assert best is not None, (m, n, k)
    return best[1:]


def _mm(a, b, *, ta=False, tb=False, out_dtype=F32, name="mm", stack=None):
    if ta:
        kdim, m = a.shape
    else:
        m, kdim = a.shape
    if tb:
        n, kb = b.shape
    else:
        kb, n = b.shape
    assert kb == kdim, (a.shape, b.shape, ta, tb)
    tm, tn, tk = _mm_tiles(m, n, kdim, a.dtype.itemsize, b.dtype.itemsize, jnp.dtype(out_dtype).itemsize)
    nk = kdim // tk
    dims = (((0,) if ta else (1,), (1,) if tb else (0,)), ((), ()))

    def dot(a_ref, b_ref):
        return lax.dot_general(a_ref[...].astype(BF16), b_ref[...].astype(BF16), dims, preferred_element_type=F32)

    if nk == 1:
        def body(a_ref, b_ref, *rest):
            o_ref = rest[-1]
            o_ref[...] = dot(a_ref, b_ref).astype(o_ref.dtype)
        scratch = []
    else:
        def body(a_ref, b_ref, *rest):
            o_ref, acc_ref = rest[-2], rest[-1]
            k = pl.program_id(2)

            @pl.when(k == 0)
            def _():
                acc_ref[...] = dot(a_ref, b_ref)

            @pl.when(jnp.logical_and(k > 0, k < nk - 1))
            def _():
                acc_ref[...] += dot(a_ref, b_ref)

            @pl.when(k == nk - 1)
            def _():
                o_ref[...] = (acc_ref[...] + dot(a_ref, b_ref)).astype(o_ref.dtype)
        scratch = [pltpu.VMEM((tm, tn), F32)]

    a_spec = pl.BlockSpec((tk, tm), lambda i, j, k: (k, i)) if ta else pl.BlockSpec((tm, tk), lambda i, j, k: (i, k))
    b_spec = pl.BlockSpec((tn, tk), lambda i, j, k: (j, k)) if tb else pl.BlockSpec((tk, tn), lambda i, j, k: (k, j))
    in_specs, operands, aliases = [a_spec, b_spec], [a, b], {}
    if stack is None:
        out_spec = pl.BlockSpec((tm, tn), lambda i, j, k: (i, j))
        out_shape = jax.ShapeDtypeStruct((m, n), out_dtype)
    else:
        buf, depth, layer = stack
        out_spec = pl.BlockSpec((None, tm, tn), lambda i, j, k: (layer, i, j))
        out_shape = jax.ShapeDtypeStruct((depth, m, n), out_dtype)
        if buf is not None:
            in_specs.append(pl.BlockSpec(memory_space=pl.ANY))
            operands.append(buf)
            aliases = {2: 0}
    return pl.pallas_call(
        body, name=name, grid=(m // tm, n // tn, nk),
        in_specs=in_specs, out_specs=out_spec, out_shape=out_shape, scratch_shapes=scratch,
        input_output_aliases=aliases,
        compiler_params=_cparams(("parallel", "parallel", "arbitrary")),
    )(*operands)


def _rowwise(fn, rows, params, outs, pouts=(), *, tm=256, name="rowwise"):
    t = rows[0][0].shape[0]
    assert t % tm == 0
    n_r, n_p, n_o, n_po = len(rows), len(params), len(outs), len(pouts)

    def body(*refs):
        r_refs = refs[:n_r]
        p_refs = refs[n_r:n_r + n_p]
        o_refs = refs[n_r + n_p:n_r + n_p + n_o]
        po_refs = refs[n_r + n_p + n_o:]
        res = fn(*[r[...] for r in r_refs], *[p[...] for p in p_refs])
        if not isinstance(res, (tuple, list)):
            res = (res,)
        assert len(res) == n_o + n_po, (len(res), n_o, n_po)
        for o, val in zip(o_refs, res[:n_o]):
            o[...] = val.astype(o.dtype)
        if n_po:
            first = pl.program_id(0) == 0

            @pl.when(first)
            def _():
                for o, val in zip(po_refs, res[n_o:]):
                    o[...] = val.astype(F32)

            @pl.when(jnp.logical_not(first))
            def _():
                for o, val in zip(po_refs, res[n_o:]):
                    o[...] += val.astype(F32)

    in_specs = [pl.BlockSpec((tm, w), functools.partial(lambda i, cb: (i, cb), cb=cb)) for (_, w, cb) in rows]
    in_specs += [pl.BlockSpec(p.shape, lambda i: (0, 0)) for p in params]
    out_specs = [pl.BlockSpec((tm, w), lambda i: (i, 0)) for (w, _) in outs]
    out_specs += [pl.BlockSpec((1, w), lambda i: (0, 0)) for w in pouts]
    out_shape = [jax.ShapeDtypeStruct((t, w), dt) for (w, dt) in outs]
    out_shape += [jax.ShapeDtypeStruct((1, w), F32) for w in pouts]
    res = pl.pallas_call(
        body, name=name, grid=(t // tm,), in_specs=in_specs, out_specs=out_specs, out_shape=out_shape,
        compiler_params=_cparams(("arbitrary",)),
    )(*[r[0] for r in rows], *params)
    return res


def _full(a):
    return (a, a.shape[1], 0)


def _rms(x, g):
    x = x.astype(F32)
    return x * lax.rsqrt(jnp.mean(x * x, axis=-1, keepdims=True) + EPS) * g


def _sum0(v):
    return jnp.sum(v, axis=0, keepdims=True)


def rms_fwd(h, g, name):
    return _rowwise(lambda x, gg: _rms(x, gg), [_full(h)], [g], [(h.shape[1], BF16)], tm=512, name=name)[0]


def rms_bwd(h, g, dn, dres, name):
    def fn(x, d, r, gg):
        _, vjp = jax.vjp(_rms, x, gg)
        dx, dg = vjp(d.astype(F32))
        return dx + r, dg
    w = h.shape[1]
    return _rowwise(fn, [_full(h), _full(dn), _full(dres)], [g], [(w, F32)], [w], tm=512, name=name)


def res_rms_fwd(h, f, g, coef, name):
    return _rowwise(lambda x, y, gg: x + coef * _rms(y, gg), [_full(h), _full(f)], [g], [(h.shape[1], F32)],
                    tm=512, name=name)[0]


def res_rms_bwd(f, g, dh, coef, name):
    def fn(y, d, gg):
        _, vjp = jax.vjp(lambda a, b: coef * _rms(a, b), y, gg)
        dy, dg = vjp(d)
        return dy, dg
    w = f.shape[1]
    return _rowwise(fn, [_full(f), _full(dh)], [g], [(w, BF16)], [w], tm=512, name=name)


def swiglu_fwd(gu, name):
    f = gu.shape[1] // 2

    def fn(gate, up):
        gate = gate.astype(F32)
        return gate * jax.nn.sigmoid(gate) * up.astype(F32)
    return _rowwise(fn, [(gu, f, 0), (gu, f, 1)], [], [(f, BF16)], name=name)[0]


def swiglu_bwd(gu, da, name):
    t, f2 = gu.shape
    f = f2 // 2
    tm = 256

    def body(gate_ref, up_ref, da_ref, o_ref):
        gate = gate_ref[...].astype(F32)
        up = up_ref[...].astype(F32)
        d = da_ref[...].astype(F32)
        sg = jax.nn.sigmoid(gate)
        silu = gate * sg
        o_ref[:, :f] = (d * up * (sg + silu * (1.0 - sg))).astype(o_ref.dtype)
        o_ref[:, f:] = (d * silu).astype(o_ref.dtype)

    return pl.pallas_call(
        body, name=name, grid=(t // tm,),
        in_specs=[pl.BlockSpec((tm, f), lambda i: (i, 0)), pl.BlockSpec((tm, f), lambda i: (i, 1)),
                  pl.BlockSpec((tm, f), lambda i: (i, 0))],
        out_specs=pl.BlockSpec((tm, f2), lambda i: (i, 0)),
        out_shape=jax.ShapeDtypeStruct((t, f2), BF16),
        compiler_params=_cparams(("parallel",)),
    )(gu, gu, da)


def _merge(g0, g1, g2, g3, z0, z1, z2, z3, b0, b1, b2, b3):
    acc = jax.nn.sigmoid(g0.astype(F32) + b0) * z0.astype(F32)
    acc += jax.nn.sigmoid(g1.astype(F32) + b1) * z1.astype(F32)
    acc += jax.nn.sigmoid(g2.astype(F32) + b2) * z2.astype(F32)
    acc += jax.nn.sigmoid(g3.astype(F32) + b3) * z3.astype(F32)
    return acc


def merge_fwd(proj, zs, bs, name):
    rows = [(proj, D_MODEL, k) for k in range(4)] + [_full(z) for z in zs]
    return _rowwise(_merge, rows, list(bs), [(D_MODEL, BF16)], name=name)[0]


def merge_bwd(proj, zs, bs, dmerged, name):
    def fn(*args):
        d = args[8].astype(F32)
        prim = args[:8] + args[9:]
        _, vjp = jax.vjp(_merge, *prim)
        return vjp(d)
    rows = [(proj, D_MODEL, k) for k in range(4)] + [_full(z) for z in zs] + [_full(dmerged)]
    outs = [(D_MODEL, BF16)] * 8
    return _rowwise(fn, rows, list(bs), outs, [D_MODEL] * 4, name=name)


def _ple(pgl, pr, g):
    return jax.nn.sigmoid(pgl.astype(F32)) * _rms(pr, g)


def ple_fwd(h, pgl, pr, g, name):
    return _rowwise(lambda x, a, b, gg: x + _ple(a, b, gg), [_full(h), _full(pgl), _full(pr)], [g],
                    [(D_MODEL, F32)], tm=512, name=name)[0]


def ple_bwd(pgl, pr, g, dh, name):
    def fn(a, b, d, gg):
        _, vjp = jax.vjp(_ple, a, b, gg)
        return vjp(d)
    return _rowwise(fn, [_full(pgl), _full(pr), _full(dh)], [g], [(D_MODEL, BF16), (D_MODEL, BF16)], [D_MODEL],
                    tm=512, name=name)


def loss_fwd_bwd(y, target, name):
    def fn(a, b):
        err = a - b
        return err * (1.0 / D_MODEL), _sum0(err * err) * (0.5 / D_MODEL)
    return _rowwise(fn, [_full(y), _full(target)], [], [(D_MODEL, F32)], [D_MODEL], tm=512, name=name)


def _shift_down(x, d, row):
    if d == 0:
        return x
    return jnp.where(row >= d, pltpu.roll(x, d, 0), 0.0)


def _shift_up(x, d, row):
    if d == 0:
        return x
    s = x.shape[0]
    return jnp.where(row < s - d, pltpu.roll(x, s - d, 0), 0.0)


def fgate_fwd(proj, bf, bn, s, name):
    def body(a_ref, b_ref, o_ref):
        x = a_ref[...].astype(F32) + b_ref[...]
        c = jnp.minimum(x, 0.0) - jnp.log(1.0 + jnp.exp(-jnp.abs(x)))
        row = lax.broadcasted_iota(jnp.int32, c.shape, 0)
        sh = 1
        while sh < s:
            c = c + _shift_down(c, sh, row)
            sh *= 2
        o_ref[...] = c

    return pl.pallas_call(
        body, name=name, grid=(bn,),
        in_specs=[pl.BlockSpec((s, 128), lambda b: (b, C_AF // 128)), pl.BlockSpec((1, 128), lambda b: (0, 0))],
        out_specs=pl.BlockSpec((s, 128), lambda b: (b, 0)),
        out_shape=jax.ShapeDtypeStruct((bn * s, 128), F32),
        compiler_params=_cparams(("parallel",)),
    )(proj, bf)


def fgate_bwd(proj, bf, dc, bn, s, name):
    def body(a_ref, b_ref, dc_ref, da_ref, db_ref):
        x = a_ref[...].astype(F32) + b_ref[...]
        d = dc_ref[...]
        row = lax.broadcasted_iota(jnp.int32, d.shape, 0)
        sh = 1
        while sh < s:
            d = d + _shift_up(d, sh, row)
            sh *= 2
        da = d * jax.nn.sigmoid(-x)
        da_ref[...] = da.astype(da_ref.dtype)
        first = pl.program_id(0) == 0

        @pl.when(first)
        def _():
            db_ref[...] = _sum0(da)

        @pl.when(jnp.logical_not(first))
        def _():
            db_ref[...] += _sum0(da)

    return pl.pallas_call(
        body, name=name, grid=(bn,),
        in_specs=[pl.BlockSpec((s, 128), lambda b: (b, C_AF // 128)), pl.BlockSpec((1, 128), lambda b: (0, 0)),
                  pl.BlockSpec((s, 128), lambda b: (b, 0))],
        out_specs=[pl.BlockSpec((s, 128), lambda b: (b, 0)), pl.BlockSpec((1, 128), lambda b: (0, 0))],
        out_shape=[jax.ShapeDtypeStruct((bn * s, 128), BF16), jax.ShapeDtypeStruct((1, 128), F32)],
        compiler_params=_cparams(("arbitrary",)),
    )(proj, bf, dc)


FOX_T = 256


def _fox_scores(q, k, cq, ck, j, i):
    t = FOX_T
    s = lax.dot_general(q, k, (((1,), (1,)), ((), ())), preferred_element_type=F32) * SCALE
    qpos = j * t + lax.broadcasted_iota(jnp.int32, (t, t), 0)
    kpos = i * t + lax.broadcasted_iota(jnp.int32, (t, t), 1)
    return jnp.where(qpos >= kpos, s + (cq - ck), NEG)


def fox_fwd(q, k, v, c_col, c_row, name):
    bn, h, s, d = q.shape
    t = FOX_T
    nq = s // t

    def body(q_ref, k_ref, v_ref, cq_ref, ck_ref, o_ref, lse_ref):
        j = pl.program_id(2)
        qv = q_ref[...]
        cq = cq_ref[...]

        def step(i, carry):
            m, l, acc = carry
            ks = pl.multiple_of(i * t, t)
            kc = k_ref[pl.ds(ks, t), :]
            vc = v_ref[pl.ds(ks, t), :]
            sc = _fox_scores(qv, kc, cq, ck_ref[i], j, i)
            m_new = jnp.maximum(m, jnp.max(sc, axis=-1, keepdims=True))
            alpha = jnp.exp(m - m_new)
            p = jnp.exp(sc - m_new)
            l = alpha * l + jnp.sum(p, axis=-1, keepdims=True)
            acc = alpha * acc + jnp.dot(p.astype(BF16), vc, preferred_element_type=F32)
            return m_new, l, acc

        init = (jnp.full((t, 1), NEG, F32), jnp.zeros((t, 1), F32), jnp.zeros((t, d), F32))
        m, l, acc = lax.fori_loop(0, j + 1, step, init)
        o_ref[...] = (acc / l).astype(o_ref.dtype)
        lse_ref[...] = m + jnp.log(l)

    blk_q = pl.BlockSpec((None, None, t, d), lambda b, hh, j: (b, hh, j, 0))
    blk_kv = pl.BlockSpec((None, None, s, d), lambda b, hh, j: (b, hh, 0, 0))
    blk_c1 = pl.BlockSpec((None, None, t, 1), lambda b, hh, j: (b, hh, j, 0))
    blk_cr = pl.BlockSpec((None, None, nq, 1, t), lambda b, hh, j: (b, hh, 0, 0, 0))
    return pl.pallas_call(
        body, name=name, grid=(bn, h, nq),
        in_specs=[blk_q, blk_kv, blk_kv, blk_c1, blk_cr],
        out_specs=[blk_q, blk_c1],
        out_shape=[jax.ShapeDtypeStruct((bn, h, s, d), F32), jax.ShapeDtypeStruct((bn, h, s, 1), F32)],
        compiler_params=_cparams(("parallel", "parallel", "arbitrary")),
    )(q, k, v, c_col, c_row)


def fox_bwd(q, k, v, c_col, c_row, o, lse, do, name):
    bn, h, s, d = q.shape
    t = FOX_T
    nq = s // t

    def body(q_ref, k_ref, v_ref, cq_ref, ck_ref, o_ref, lse_ref, do_ref, dq_ref, dk_ref, dv_ref, dck_ref,
             dcq_ref):
        j = pl.program_id(2)

        @pl.when(j == 0)
        def _():
            dk_ref[...] = jnp.zeros_like(dk_ref)
            dv_ref[...] = jnp.zeros_like(dv_ref)
            dck_ref[...] = jnp.zeros_like(dck_ref)

        qv = q_ref[...]
        cq = cq_ref[...]
        dov = do_ref[...]
        lse = lse_ref[...]
        delta = jnp.sum(dov.astype(F32) * o_ref[...].astype(F32), axis=-1, keepdims=True)

        def step(i, carry):
            dq, dcq = carry
            ks = pl.multiple_of(i * t, t)
            kc = k_ref[pl.ds(ks, t), :]
            vc = v_ref[pl.ds(ks, t), :]
            sc = _fox_scores(qv, kc, cq, ck_ref[i], j, i)
            p = jnp.exp(sc - lse)
            dp = lax.dot_general(dov, vc, (((1,), (1,)), ((), ())), preferred_element_type=F32)
            ds = p * (dp - delta)
            dsb = ds.astype(BF16)
            dq = dq + jnp.dot(dsb, kc, preferred_element_type=F32) * SCALE
            dk_ref[pl.ds(ks, t), :] += lax.dot_general(dsb, qv, (((0,), (0,)), ((), ())),
                                                       preferred_element_type=F32) * SCALE
            dv_ref[pl.ds(ks, t), :] += lax.dot_general(p.astype(BF16), dov, (((0,), (0,)), ((), ())),
                                                       preferred_element_type=F32)
            dck_ref[i] += -_sum0(ds)
            return dq, dcq + jnp.sum(ds, axis=-1, keepdims=True)

        dq, dcq = lax.fori_loop(0, j + 1, step, (jnp.zeros((t, d), F32), jnp.zeros((t, 1), F32)))
        dq_ref[...] = dq
        dcq_ref[...] = dcq

    blk_q = pl.BlockSpec((None, None, t, d), lambda b, hh, j: (b, hh, j, 0))
    blk_kv = pl.BlockSpec((None, None, s, d), lambda b, hh, j: (b, hh, 0, 0))
    blk_c1 = pl.BlockSpec((None, None, t, 1), lambda b, hh, j: (b, hh, j, 0))
    blk_cr = pl.BlockSpec((None, None, nq, 1, t), lambda b, hh, j: (b, hh, 0, 0, 0))
    return pl.pallas_call(
        body, name=name, grid=(bn, h, nq),
        in_specs=[blk_q, blk_kv, blk_kv, blk_c1, blk_cr, blk_q, blk_c1, blk_q],
        out_specs=[blk_q, blk_kv, blk_kv, blk_cr, blk_c1],
        out_shape=[jax.ShapeDtypeStruct((bn, h, s, d), F32), jax.ShapeDtypeStruct((bn, h, s, d), F32),
                   jax.ShapeDtypeStruct((bn, h, s, d), F32), jax.ShapeDtypeStruct((bn, h, nq, 1, t), F32),
                   jax.ShapeDtypeStruct((bn, h, s, 1), F32)],
        compiler_params=_cparams(("parallel", "parallel", "arbitrary")),
    )(q, k, v, c_col, c_row, o, lse, do)


def _swa_valid(n):
    qi = lax.broadcasted_iota(jnp.int32, (QB, 2 * QB), 0)
    kj = lax.broadcasted_iota(jnp.int32, (QB, 2 * QB), 1)
    dist = qi + QB - kj
    return (dist >= 0) & (dist < WINDOW) & ((kj >= QB) | (n > 0))


def _swa_band(ref, n):
    qs = pl.multiple_of(n * QB, QB)
    ps = pl.multiple_of(jnp.maximum(n - 1, 0) * QB, QB)
    return jnp.concatenate([ref[pl.ds(ps, QB), :], ref[pl.ds(qs, QB), :]], axis=0), qs, ps


def swa_fwd(q, k, v, bias, sinks, name):
    bn, hq, s, d = q.shape
    nb = s // QB

    def body(q_ref, k_ref, v_ref, b_ref, s_ref, o_ref, lse_ref):
        def step(n, _):
            kb, qs, _ps = _swa_band(k_ref, n)
            vb, _, _ = _swa_band(v_ref, n)
            valid = _swa_valid(n)
            for g in range(D_GROUP):
                qg = q_ref[g, pl.ds(qs, QB), :]
                sc = lax.dot_general(qg, kb, (((1,), (1,)), ((), ())), preferred_element_type=F32) * SCALE
                sc = jnp.where(valid, sc + b_ref[g], NEG)
                sink = s_ref[g]
                m = jnp.maximum(jnp.max(sc, axis=-1, keepdims=True), sink)
                e = jnp.exp(sc - m)
                z = jnp.sum(e, axis=-1, keepdims=True) + jnp.exp(sink - m)
                p = e / z
                o_ref[g, pl.ds(qs, QB), :] = jnp.dot(p.astype(BF16), vb, preferred_element_type=F32
                                                     ).astype(o_ref.dtype)
                lse_ref[g, pl.ds(qs, QB), :] = m + jnp.log(z)
            return 0

        lax.fori_loop(0, nb, step, 0)

    blk_q = pl.BlockSpec((None, D_GROUP, s, d), lambda b, kh: (b, kh, 0, 0))
    blk_kv = pl.BlockSpec((None, None, s, d), lambda b, kh: (b, kh, 0, 0))
    blk_l = pl.BlockSpec((None, D_GROUP, s, 1), lambda b, kh: (b, kh, 0, 0))
    return pl.pallas_call(
        body, name=name, grid=(bn, D_KV_HEADS),
        in_specs=[blk_q, blk_kv, blk_kv, pl.BlockSpec((D_GROUP, QB, 2 * QB), lambda b, kh: (kh, 0, 0)),
                  pl.BlockSpec((D_GROUP, QB, 1), lambda b, kh: (kh, 0, 0))],
        out_specs=[blk_q, blk_l],
        out_shape=[jax.ShapeDtypeStruct((bn, hq, s, d), BF16), jax.ShapeDtypeStruct((bn, hq, s, 1), F32)],
        compiler_params=_cparams(("parallel", "parallel")),
    )(q, k, v, bias, sinks)


def swa_bwd(q, k, v, bias, sinks, o, lse, do, name):
    bn, hq, s, d = q.shape
    nb = s // QB

    def body(q_ref, k_ref, v_ref, b_ref, s_ref, o_ref, lse_ref, do_ref, dq_ref, dk_ref, dv_ref, db_ref, dsk_ref):
        @pl.when(pl.program_id(1) == 0)
        def _():
            db_ref[...] = jnp.zeros_like(db_ref)
            dsk_ref[...] = jnp.zeros_like(dsk_ref)

        dk_ref[...] = jnp.zeros_like(dk_ref)
        dv_ref[...] = jnp.zeros_like(dv_ref)

        def step(n, _):
            kb, qs, ps = _swa_band(k_ref, n)
            vb, _, _ = _swa_band(v_ref, n)
            valid = _swa_valid(n)
            dkb = jnp.zeros((2 * QB, d), F32)
            dvb = jnp.zeros((2 * QB, d), F32)
            for g in range(D_GROUP):
                qg = q_ref[g, pl.ds(qs, QB), :]
                dog = do_ref[g, pl.ds(qs, QB), :]
                og = o_ref[g, pl.ds(qs, QB), :]
                lse = lse_ref[g, pl.ds(qs, QB), :]
                sc = lax.dot_general(qg, kb, (((1,), (1,)), ((), ())), preferred_element_type=F32) * SCALE
                sc = jnp.where(valid, sc + b_ref[g], NEG)
                p = jnp.exp(sc - lse)
                delta = jnp.sum(dog.astype(F32) * og.astype(F32), axis=-1, keepdims=True)
                dp = lax.dot_general(dog, vb, (((1,), (1,)), ((), ())), preferred_element_type=F32)
                ds = p * (dp - delta)
                dsb = ds.astype(BF16)
                dq_ref[g, pl.ds(qs, QB), :] = jnp.dot(dsb, kb, preferred_element_type=F32) * SCALE
                dkb = dkb + lax.dot_general(dsb, qg, (((0,), (0,)), ((), ())), preferred_element_type=F32) * SCALE
                dvb = dvb + lax.dot_general(p.astype(BF16), dog, (((0,), (0,)), ((), ())),
                                            preferred_element_type=F32)
                db_ref[g] += ds
                dsk_ref[g] += -jnp.exp(s_ref[g] - lse) * delta
            dk_ref[pl.ds(ps, QB), :] += dkb[:QB]
            dk_ref[pl.ds(qs, QB), :] += dkb[QB:]
            dv_ref[pl.ds(ps, QB), :] += dvb[:QB]
            dv_ref[pl.ds(qs, QB), :] += dvb[QB:]
            return 0

        lax.fori_loop(0, nb, step, 0)

    blk_q = pl.BlockSpec((None, D_GROUP, s, d), lambda kh, b: (b, kh, 0, 0))
    blk_kv = pl.BlockSpec((None, None, s, d), lambda kh, b: (b, kh, 0, 0))
    blk_l = pl.BlockSpec((None, D_GROUP, s, 1), lambda kh, b: (b, kh, 0, 0))
    blk_b = pl.BlockSpec((D_GROUP, QB, 2 * QB), lambda kh, b: (kh, 0, 0))
    blk_s = pl.BlockSpec((D_GROUP, QB, 1), lambda kh, b: (kh, 0, 0))
    return pl.pallas_call(
        body, name=name, grid=(D_KV_HEADS, bn),
        in_specs=[blk_q, blk_kv, blk_kv, blk_b, blk_s, blk_q, blk_l, blk_q],
        out_specs=[blk_q, blk_kv, blk_kv, blk_b, blk_s],
        out_shape=[jax.ShapeDtypeStruct((bn, hq, s, d), F32), jax.ShapeDtypeStruct((bn, D_KV_HEADS, s, d), F32),
                   jax.ShapeDtypeStruct((bn, D_KV_HEADS, s, d), F32),
                   jax.ShapeDtypeStruct((hq, QB, 2 * QB), F32), jax.ShapeDtypeStruct((hq, QB, 1), F32)],
        compiler_params=_cparams(("parallel", "arbitrary")),
    )(q, k, v, bias, sinks, o, lse, do)


def _sel(nh, width):
    r = lax.broadcasted_iota(jnp.int32, (width, HEAD_DIM), 0)
    c = lax.broadcasted_iota(jnp.int32, (width, HEAD_DIM), 1)
    return [(r == c + HEAD_DIM * h).astype(BF16) for h in range(nh)]


def _pick_head(x, e):
    return jnp.dot(x, e, preferred_element_type=F32).astype(BF16)


def _place_head(x, e):
    return lax.dot_general(x.astype(BF16), e, (((1,), (1,)), ((), ())), preferred_element_type=F32)


def fox2_fwd(proj, c_col, c_row, bn, s, name):
    t = FOX_T
    nq = s // t
    nh, d = A_HEADS, HEAD_DIM

    def body(q_ref, k_ref, v_ref, cq_ref, ck_ref, y_ref, o_ref, lse_ref, kh_ref, vh_ref):
        j = pl.program_id(1)
        es = _sel(nh, 256)

        @pl.when(j == 0)
        def _():
            for h in range(nh):
                kh_ref[h] = _pick_head(k_ref[...], es[h])
                vh_ref[h] = _pick_head(v_ref[...], es[h])

        q4 = q_ref[...]
        qs = [_pick_scaled(q4, es[h]) for h in range(nh)]
        cqs = [cq_ref[h] for h in range(nh)]

        def chunk(i, carry, masked):
            ks = pl.multiple_of(i * t, t)
            out = []
            for h in range(nh):
                m, l, acc = carry[h]
                sc = _fox_scores2(qs[h], kh_ref[h, pl.ds(ks, t), :], cqs[h], ck_ref[h, i], masked)
                m_new = jnp.maximum(m, jnp.max(sc, axis=-1, keepdims=True))
                alpha = jnp.exp(m - m_new)
                p = jnp.exp(sc - m_new)
                l = alpha * l + jnp.sum(p, axis=-1, keepdims=True)
                acc = alpha * acc + jnp.dot(p.astype(BF16), vh_ref[h, pl.ds(ks, t), :], preferred_element_type=F32)
                out.append((m_new, l, acc))
            return tuple(out)

        init = tuple((jnp.full((t, 1), NEG, F32), jnp.zeros((t, 1), F32), jnp.zeros((t, d), F32)) for _ in range(nh))
        res = lax.fori_loop(0, j, lambda i, carry: chunk(i, carry, False), init)
        res = chunk(j, res, True)
        y = jnp.zeros((t, 256), F32)
        for h in range(nh):
            m, l, acc = res[h]
            o = acc / l
            o_ref[h] = o
            lse_ref[h] = m + jnp.log(l)
            y = y + _place_head(o, es[h])
        y_ref[...] = y.astype(y_ref.dtype)

    blk_q = pl.BlockSpec((t, 256), lambda b, j: (b * nq + j, C_AQ // 256))
    blk_k = pl.BlockSpec((s, 256), lambda b, j: (b, C_AK // 256))
    blk_v = pl.BlockSpec((s, 256), lambda b, j: (b, C_AV // 256))
    blk_c1 = pl.BlockSpec((None, nh, t, 1), lambda b, j: (b, 0, j, 0))
    blk_cr = pl.BlockSpec((None, nh, nq, 1, t), lambda b, j: (b, 0, 0, 0, 0))
    blk_o = pl.BlockSpec((None, nh, t, d), lambda b, j: (b, 0, j, 0))
    return pl.pallas_call(
        body, name=name, grid=(bn, nq),
        in_specs=[blk_q, blk_k, blk_v, blk_c1, blk_cr],
        out_specs=[pl.BlockSpec((t, 256), lambda b, j: (b * nq + j, 0)), blk_o, blk_c1],
        out_shape=[jax.ShapeDtypeStruct((bn * s, 256), BF16), jax.ShapeDtypeStruct((bn, nh, s, d), F32),
                   jax.ShapeDtypeStruct((bn, nh, s, 1), F32)],
        scratch_shapes=[pltpu.VMEM((nh, s, d), BF16), pltpu.VMEM((nh, s, d), BF16)],
        compiler_params=_cparams(("arbitrary", "arbitrary")),
    )(proj, proj, proj, c_col, c_row)


def fox2_bwd(proj, c_col, c_row, o, lse, dya, bn, s, name):
    t = FOX_T
    nq = s // t
    nh, d = A_HEADS, HEAD_DIM

    def body(q_ref, k_ref, v_ref, cq_ref, ck_ref, o_ref, lse_ref, dy_ref, dq_ref, dk_ref, dv_ref, dck_ref, dcq_ref,
             kh_ref, vh_ref, dkh_ref, dvh_ref):
        j = pl.program_id(1)
        es = _sel(nh, 256)

        @pl.when(j == 0)
        def _():
            for h in range(nh):
                kh_ref[h] = _pick_head(k_ref[...], es[h])
                vh_ref[h] = _pick_head(v_ref[...], es[h])
            dkh_ref[...] = jnp.zeros_like(dkh_ref)
            dvh_ref[...] = jnp.zeros_like(dvh_ref)
            dck_ref[...] = jnp.zeros_like(dck_ref)

        q4 = q_ref[...]
        dy4 = dy_ref[...]
        qs = [_pick_scaled(q4, es[h]) for h in range(nh)]
        dos = [_pick_head(dy4, es[h]) for h in range(nh)]
        cqs = [cq_ref[h] for h in range(nh)]
        lses = [lse_ref[h] for h in range(nh)]
        deltas = [jnp.sum(dos[h].astype(F32) * o_ref[h], axis=-1, keepdims=True) for h in range(nh)]

        def chunk(i, carry, masked):
            ks = pl.multiple_of(i * t, t)
            out = []
            for h in range(nh):
                dq, dcq = carry[h]
                kc = kh_ref[h, pl.ds(ks, t), :]
                sc = _fox_scores2(qs[h], kc, cqs[h], ck_ref[h, i], masked)
                p = jnp.exp(sc - lses[h])
                dp = lax.dot_general(dos[h], vh_ref[h, pl.ds(ks, t), :], (((1,), (1,)), ((), ())),
                                     preferred_element_type=F32)
                ds = p * (dp - deltas[h])
                dsb = ds.astype(BF16)
                dq = dq + jnp.dot(dsb, kc, preferred_element_type=F32)
                dkh_ref[h, pl.ds(ks, t), :] += lax.dot_general(dsb, qs[h], (((0,), (0,)), ((), ())),
                                                               preferred_element_type=F32)
                dvh_ref[h, pl.ds(ks, t), :] += lax.dot_general(p.astype(BF16), dos[h], (((0,), (0,)), ((), ())),
                                                               preferred_element_type=F32)
                dck_ref[h, i] += -_sum0(ds)
                out.append((dq, dcq + jnp.sum(ds, axis=-1, keepdims=True)))
            return tuple(out)

        init = tuple((jnp.zeros((t, d), F32), jnp.zeros((t, 1), F32)) for _ in range(nh))
        res = lax.fori_loop(0, j, lambda i, carry: chunk(i, carry, False), init)
        res = chunk(j, res, True)
        dq4 = jnp.zeros((t, 256), F32)
        for h in range(nh):
            dq4 = dq4 + _place_head(res[h][0] * SCALE, es[h])
            dcq_ref[h] = res[h][1]
        dq_ref[...] = dq4.astype(dq_ref.dtype)

        @pl.when(j == nq - 1)
        def _():
            dk4 = jnp.zeros((s, 256), F32)
            dv4 = jnp.zeros((s, 256), F32)
            for h in range(nh):
                dk4 = dk4 + _place_head(dkh_ref[h], es[h])
                dv4 = dv4 + _place_head(dvh_ref[h], es[h])
            dk_ref[...] = dk4.astype(dk_ref.dtype)
            dv_ref[...] = dv4.astype(dv_ref.dtype)

    blk_q = pl.BlockSpec((t, 256), lambda b, j: (b * nq + j, C_AQ // 256))
    blk_k = pl.BlockSpec((s, 256), lambda b, j: (b, C_AK // 256))
    blk_v = pl.BlockSpec((s, 256), lambda b, j: (b, C_AV // 256))
    blk_c1 = pl.BlockSpec((None, nh, t, 1), lambda b, j: (b, 0, j, 0))
    blk_cr = pl.BlockSpec((None, nh, nq, 1, t), lambda b, j: (b, 0, 0, 0, 0))
    blk_o = pl.BlockSpec((None, nh, t, d), lambda b, j: (b, 0, j, 0))
    blk_t = pl.BlockSpec((t, 256), lambda b, j: (b * nq + j, 0))
    blk_s = pl.BlockSpec((s, 256), lambda b, j: (b, 0))
    return pl.pallas_call(
        body, name=name, grid=(bn, nq),
        in_specs=[blk_q, blk_k, blk_v, blk_c1, blk_cr, blk_o, blk_c1, blk_t],
        out_specs=[blk_t, blk_s, blk_s, blk_cr, blk_c1],
        out_shape=[jax.ShapeDtypeStruct((bn * s, 256), BF16)] * 3
        + [jax.ShapeDtypeStruct((bn, nh, nq, 1, t), F32), jax.ShapeDtypeStruct((bn, nh, s, 1), F32)],
        scratch_shapes=[pltpu.VMEM((nh, s, d), BF16), pltpu.VMEM((nh, s, d), BF16),
                        pltpu.VMEM((nh, s, d), F32), pltpu.VMEM((nh, s, d), F32)],
        compiler_params=_cparams(("arbitrary", "arbitrary")),
    )(proj, proj, proj, c_col, c_row, o, lse, dya)


def _band3(ref, h, n):
    qs = pl.multiple_of(n * QB, QB)
    ps = pl.multiple_of(jnp.maximum(n - 1, 0) * QB, QB)
    return jnp.concatenate([ref[h, pl.ds(ps, QB), :], ref[h, pl.ds(qs, QB), :]], axis=0), qs, ps


def swa2_fwd(proj, bias, sinks, bn, s, name):
    nb = s // QB
    d = HEAD_DIM

    def body(q0_ref, q1_ref, k_ref, v_ref, b_ref, s_ref, y_ref, lse_ref, qh_ref, kh_ref, vh_ref):
        e4 = _sel(D_GROUP, 256)
        e2 = _sel(D_KV_HEADS, 128)
        for kh, q_ref in enumerate((q0_ref, q1_ref)):
            kh_ref[kh] = _pick_head(k_ref[...], e2[kh])
            vh_ref[kh] = _pick_head(v_ref[...], e2[kh])
            for g in range(D_GROUP):
                qh_ref[D_GROUP * kh + g] = _pick_head(q_ref[...], e4[g])

        def step(n, _):
            valid = _swa_valid(n)
            for kh in range(D_KV_HEADS):
                kb, qs, _ps = _band3(kh_ref, kh, n)
                vb, _, _ = _band3(vh_ref, kh, n)
                y = jnp.zeros((QB, 256), F32)
                for g in range(D_GROUP):
                    hh = D_GROUP * kh + g
                    qg = qh_ref[hh, pl.ds(qs, QB), :]
                    sc = lax.dot_general(qg, kb, (((1,), (1,)), ((), ())), preferred_element_type=F32) * SCALE
                    sc = jnp.where(valid, sc + b_ref[hh], NEG)
                    sink = s_ref[hh]
                    m = jnp.maximum(jnp.max(sc, axis=-1, keepdims=True), sink)
                    e = jnp.exp(sc - m)
                    z = jnp.sum(e, axis=-1, keepdims=True) + jnp.exp(sink - m)
                    o = jnp.dot((e / z).astype(BF16), vb, preferred_element_type=F32)
                    lse_ref[hh, pl.ds(qs, QB), :] = m + jnp.log(z)
                    y = y + _place_head(o, e4[g])
                y_ref[pl.ds(qs, QB), 256 * kh:256 * (kh + 1)] = y.astype(y_ref.dtype)
            return 0

        lax.fori_loop(0, nb, step, 0)

    return pl.pallas_call(
        body, name=name, grid=(bn,),
        in_specs=[pl.BlockSpec((s, 256), lambda b: (b, C_DQ // 256)), pl.BlockSpec((s, 256), lambda b: (b, C_DQ // 256 + 1)),
                  pl.BlockSpec((s, 128), lambda b: (b, C_DK // 128)), pl.BlockSpec((s, 128), lambda b: (b, C_DV // 128)),
                  pl.BlockSpec((D_Q_HEADS, QB, 2 * QB), lambda b: (0, 0, 0)),
                  pl.BlockSpec((D_Q_HEADS, QB, 1), lambda b: (0, 0, 0))],
        out_specs=[pl.BlockSpec((s, 512), lambda b: (b, 0)), pl.BlockSpec((None, D_Q_HEADS, s, 1), lambda b: (b, 0, 0, 0))],
        out_shape=[jax.ShapeDtypeStruct((bn * s, 512), BF16), jax.ShapeDtypeStruct((bn, D_Q_HEADS, s, 1), F32)],
        scratch_shapes=[pltpu.VMEM((D_Q_HEADS, s, d), BF16), pltpu.VMEM((D_KV_HEADS, s, d), BF16),
                        pltpu.VMEM((D_KV_HEADS, s, d), BF16)],
        compiler_params=_cparams(("parallel",)),
    )(proj, proj, proj, proj, bias, sinks)


def swa2_bwd(proj, bias, sinks, yd, lse, dyd, bn, s, name):
    nb = s // QB
    d = HEAD_DIM

    def body(q0_ref, q1_ref, k_ref, v_ref, b_ref, s_ref, y_ref, lse_ref, dy_ref, dq_ref, dk_ref, dv_ref, db_ref,
             dsk_ref, qh_ref, kh_ref, vh_ref, oh_ref, doh_ref, dkh_ref, dvh_ref):
        @pl.when(pl.program_id(0) == 0)
        def _():
            db_ref[...] = jnp.zeros_like(db_ref)
            dsk_ref[...] = jnp.zeros_like(dsk_ref)

        e4 = _sel(D_GROUP, 256)
        e2 = _sel(D_KV_HEADS, 128)
        for kh, q_ref in enumerate((q0_ref, q1_ref)):
            kh_ref[kh] = _pick_head(k_ref[...], e2[kh])
            vh_ref[kh] = _pick_head(v_ref[...], e2[kh])
            for g in range(D_GROUP):
                hh = D_GROUP * kh + g
                qh_ref[hh] = _pick_head(q_ref[...], e4[g])
                oh_ref[hh] = _pick_head(y_ref[:, 256 * kh:256 * (kh + 1)], e4[g])
                doh_ref[hh] = _pick_head(dy_ref[:, 256 * kh:256 * (kh + 1)], e4[g])
        dkh_ref[...] = jnp.zeros_like(dkh_ref)
        dvh_ref[...] = jnp.zeros_like(dvh_ref)

        def step(n, _):
            valid = _swa_valid(n)
            for kh in range(D_KV_HEADS):
                kb, qs, ps = _band3(kh_ref, kh, n)
                vb, _, _ = _band3(vh_ref, kh, n)
                dkb = jnp.zeros((2 * QB, d), F32)
                dvb = jnp.zeros((2 * QB, d), F32)
                dq4 = jnp.zeros((QB, 256), F32)
                for g in range(D_GROUP):
                    hh = D_GROUP * kh + g
                    qg = qh_ref[hh, pl.ds(qs, QB), :]
                    dog = doh_ref[hh, pl.ds(qs, QB), :]
                    og = oh_ref[hh, pl.ds(qs, QB), :]
                    lse = lse_ref[hh, pl.ds(qs, QB), :]
                    sc = lax.dot_general(qg, kb, (((1,), (1,)), ((), ())), preferred_element_type=F32) * SCALE
                    sc = jnp.where(valid, sc + b_ref[hh], NEG)
                    p = jnp.exp(sc - lse)
                    delta = jnp.sum(dog.astype(F32) * og.astype(F32), axis=-1, keepdims=True)
                    dp = lax.dot_general(dog, vb, (((1,), (1,)), ((), ())), preferred_element_type=F32)
                    ds = p * (dp - delta)
                    dsb = ds.astype(BF16)
                    dq4 = dq4 + _place_head(jnp.dot(dsb, kb, preferred_element_type=F32) * SCALE, e4[g])
                    dkb = dkb + lax.dot_general(dsb, qg, (((0,), (0,)), ((), ())),
                                                preferred_element_type=F32) * SCALE
                    dvb = dvb + lax.dot_general(p.astype(BF16), dog, (((0,), (0,)), ((), ())),
                                                preferred_element_type=F32)
                    db_ref[hh] += ds
                    dsk_ref[hh] += -jnp.exp(s_ref[hh] - lse) * delta
                dq_ref[pl.ds(qs, QB), 256 * kh:256 * (kh + 1)] = dq4.astype(dq_ref.dtype)
                dkh_ref[kh, pl.ds(ps, QB), :] += dkb[:QB]
                dkh_ref[kh, pl.ds(qs, QB), :] += dkb[QB:]
                dvh_ref[kh, pl.ds(ps, QB), :] += dvb[:QB]
                dvh_ref[kh, pl.ds(qs, QB), :] += dvb[QB:]
            return 0

        lax.fori_loop(0, nb, step, 0)
        dk2 = jnp.zeros((s, 128), F32)
        dv2 = jnp.zeros((s, 128), F32)
        for kh in range(D_KV_HEADS):
            dk2 = dk2 + _place_head(dkh_ref[kh], e2[kh])
            dv2 = dv2 + _place_head(dvh_ref[kh], e2[kh])
        dk_ref[...] = dk2.astype(dk_ref.dtype)
        dv_ref[...] = dv2.astype(dv_ref.dtype)

    blk512 = pl.BlockSpec((s, 512), lambda b: (b, 0))
    blk128 = pl.BlockSpec((s, 128), lambda b: (b, 0))
    blk_b = pl.BlockSpec((D_Q_HEADS, QB, 2 * QB), lambda b: (0, 0, 0))
    blk_s = pl.BlockSpec((D_Q_HEADS, QB, 1), lambda b: (0, 0, 0))
    return pl.pallas_call(
        body, name=name, grid=(bn,),
        in_specs=[pl.BlockSpec((s, 256), lambda b: (b, C_DQ // 256)), pl.BlockSpec((s, 256), lambda b: (b, C_DQ // 256 + 1)),
                  pl.BlockSpec((s, 128), lambda b: (b, C_DK // 128)), pl.BlockSpec((s, 128), lambda b: (b, C_DV // 128)),
                  blk_b, blk_s, blk512, pl.BlockSpec((None, D_Q_HEADS, s, 1), lambda b: (b, 0, 0, 0)), blk512],
        out_specs=[blk512, blk128, blk128, blk_b, blk_s],
        out_shape=[jax.ShapeDtypeStruct((bn * s, 512), BF16), jax.ShapeDtypeStruct((bn * s, 128), BF16),
                   jax.ShapeDtypeStruct((bn * s, 128), BF16),
                   jax.ShapeDtypeStruct((D_Q_HEADS, QB, 2 * QB), F32), jax.ShapeDtypeStruct((D_Q_HEADS, QB, 1), F32)],
        scratch_shapes=[pltpu.VMEM((D_Q_HEADS, s, d), BF16), pltpu.VMEM((D_KV_HEADS, s, d), BF16),
                        pltpu.VMEM((D_KV_HEADS, s, d), BF16), pltpu.VMEM((D_Q_HEADS, s, d), BF16),
                        pltpu.VMEM((D_Q_HEADS, s, d), BF16), pltpu.VMEM((D_KV_HEADS, s, d), F32),
                        pltpu.VMEM((D_KV_HEADS, s, d), F32)],
        compiler_params=_cparams(("arbitrary",)),
    )(proj, proj, proj, proj, bias, sinks, yd, lse, dyd)


def _pick_scaled(x, e):
    return (jnp.dot(x, e, preferred_element_type=F32) * SCALE).astype(BF16)


def _swa_valid4(n):
    qi = lax.broadcasted_iota(jnp.int32, (D_GROUP * QB, 2 * QB), 0) & (QB - 1)
    kj = lax.broadcasted_iota(jnp.int32, (D_GROUP * QB, 2 * QB), 1)
    dist = qi + QB - kj
    return (dist >= 0) & (dist < WINDOW) & ((kj >= QB) | (n > 0))


def _fox_scores2(q, k, cq, ck, masked):
    t = FOX_T
    s = lax.dot_general(q, k, (((1,), (1,)), ((), ())), preferred_element_type=F32) + (cq - ck)
    if masked:
        keep = lax.broadcasted_iota(jnp.int32, (t, t), 0) >= lax.broadcasted_iota(jnp.int32, (t, t), 1)
        s = jnp.where(keep, s, NEG)
    return s


def _sel_at(off, width):
    r = lax.broadcasted_iota(jnp.int32, (width, HEAD_DIM), 0)
    c = lax.broadcasted_iota(jnp.int32, (width, HEAD_DIM), 1)
    return (r == c + off).astype(BF16)


def _eye(n):
    return lax.broadcasted_iota(jnp.int32, (n, n), 0) == lax.broadcasted_iota(jnp.int32, (n, n), 1)


def _row_to_col(row, eye):
    return jnp.sum(jnp.where(eye, row, 0.0), axis=1, keepdims=True)


def _col_to_row(col, eye):
    return jnp.sum(jnp.where(eye, col, 0.0), axis=0, keepdims=True)


def swa3_fwd(proj, bias, sinks, bn, s, name):
    nb = s // QB
    d = HEAD_DIM

    def body(q_ref, k_ref, v_ref, b_ref, s_ref, y_ref, lse_ref, qh_ref, kh_ref, vh_ref):
        kh = pl.program_id(0)
        e4 = _sel(D_GROUP, 256)
        ek = _sel_at(HEAD_DIM * kh, 128)
        eye = _eye(QB)
        kh_ref[...] = _pick_head(k_ref[...], ek)
        vh_ref[...] = _pick_head(v_ref[...], ek)
        for g in range(D_GROUP):
            qh_ref[g] = _pick_scaled(q_ref[...], e4[g])
        bias4 = b_ref[...].reshape(D_GROUP * QB, 2 * QB)
        sink4 = s_ref[...].reshape(D_GROUP * QB, 1)

        def step(n, _):
            valid = _swa_valid4(n)
            kb, qs, _ps = _swa_band(kh_ref, n)
            vb, _, _ = _swa_band(vh_ref, n)
            q4 = jnp.concatenate([qh_ref[g, pl.ds(qs, QB), :] for g in range(D_GROUP)], axis=0)
            sc = lax.dot_general(q4, kb, (((1,), (1,)), ((), ())), preferred_element_type=F32)
            sc = jnp.where(valid, sc + bias4, NEG)
            m = jnp.maximum(jnp.max(sc, axis=-1, keepdims=True), sink4)
            e = jnp.exp(sc - m)
            z = jnp.sum(e, axis=-1, keepdims=True) + jnp.exp(sink4 - m)
            o4 = jnp.dot((e / z).astype(BF16), vb, preferred_element_type=F32)
            lse4 = m + jnp.log(z)
            y = jnp.zeros((QB, 256), F32)
            for g in range(D_GROUP):
                lse_ref[g, n] = _col_to_row(lse4[g * QB:(g + 1) * QB], eye)
                y = y + _place_head(o4[g * QB:(g + 1) * QB], e4[g])
            y_ref[pl.ds(qs, QB), :] = y.astype(y_ref.dtype)
            return 0

        lax.fori_loop(0, nb, step, 0)

    return pl.pallas_call(
        body, name=name, grid=(D_KV_HEADS, bn),
        in_specs=[pl.BlockSpec((s, 256), lambda kh, b: (b, C_DQ // 256 + kh)),
                  pl.BlockSpec((s, 128), lambda kh, b: (b, C_DK // 128)),
                  pl.BlockSpec((s, 128), lambda kh, b: (b, C_DV // 128)),
                  pl.BlockSpec((D_GROUP, QB, 2 * QB), lambda kh, b: (kh, 0, 0)),
                  pl.BlockSpec((D_GROUP, QB, 1), lambda kh, b: (kh, 0, 0))],
        out_specs=[pl.BlockSpec((s, 256), lambda kh, b: (b, kh)),
                   pl.BlockSpec((None, D_GROUP, nb, 1, QB), lambda kh, b: (b, kh, 0, 0, 0))],
        out_shape=[jax.ShapeDtypeStruct((bn * s, 512), BF16), jax.ShapeDtypeStruct((bn, D_Q_HEADS, nb, 1, QB), F32)],
        scratch_shapes=[pltpu.VMEM((D_GROUP, s, d), BF16), pltpu.VMEM((s, d), BF16), pltpu.VMEM((s, d), BF16)],
        compiler_params=_cparams(("parallel", "parallel")),
    )(proj, proj, proj, bias, sinks)


def swa3_bwd(proj, bias, sinks, yd, lse, dyd, bn, s, name):
    nb = s // QB
    d = HEAD_DIM

    def body(q_ref, k_ref, v_ref, b_ref, s_ref, y_ref, lse_ref, dy_ref, dq_ref, dk_ref, dv_ref, db_ref, dsk_ref,
             qh_ref, kh_ref, vh_ref, oh_ref, doh_ref, dkh_ref, dvh_ref):
        kh = pl.program_id(0)

        @pl.when(pl.program_id(1) == 0)
        def _():
            db_ref[...] = jnp.zeros_like(db_ref)
            dsk_ref[...] = jnp.zeros_like(dsk_ref)

        e4 = _sel(D_GROUP, 256)
        ek = _sel_at(HEAD_DIM * kh, 128)
        eye = _eye(QB)
        kh_ref[...] = _pick_head(k_ref[...], ek)
        vh_ref[...] = _pick_head(v_ref[...], ek)
        for g in range(D_GROUP):
            qh_ref[g] = _pick_scaled(q_ref[...], e4[g])
            oh_ref[g] = _pick_head(y_ref[...], e4[g])
            doh_ref[g] = _pick_head(dy_ref[...], e4[g])
        dkh_ref[...] = jnp.zeros_like(dkh_ref)
        dvh_ref[...] = jnp.zeros_like(dvh_ref)
        bias4 = b_ref[...].reshape(D_GROUP * QB, 2 * QB)
        sink4 = s_ref[...].reshape(D_GROUP * QB, 1)

        def stack(ref, qs):
            return jnp.concatenate([ref[g, pl.ds(qs, QB), :] for g in range(D_GROUP)], axis=0)

        def step(n, _):
            valid = _swa_valid4(n)
            kb, qs, ps = _swa_band(kh_ref, n)
            vb, _, _ = _swa_band(vh_ref, n)
            q4, do4, o4 = stack(qh_ref, qs), stack(doh_ref, qs), stack(oh_ref, qs)
            lse4 = jnp.concatenate([_row_to_col(lse_ref[g, n], eye) for g in range(D_GROUP)], axis=0)
            sc = lax.dot_general(q4, kb, (((1,), (1,)), ((), ())), preferred_element_type=F32)
            sc = jnp.where(valid, sc + bias4, NEG)
            p = jnp.exp(sc - lse4)
            delta = jnp.sum(do4.astype(F32) * o4.astype(F32), axis=-1, keepdims=True)
            dp = lax.dot_general(do4, vb, (((1,), (1,)), ((), ())), preferred_element_type=F32)
            ds = p * (dp - delta)
            dsb = ds.astype(BF16)
            dq4s = jnp.dot(dsb, kb, preferred_element_type=F32) * SCALE
            dkb = lax.dot_general(dsb, q4, (((0,), (0,)), ((), ())), preferred_element_type=F32)
            dvb = lax.dot_general(p.astype(BF16), do4, (((0,), (0,)), ((), ())), preferred_element_type=F32)
            db_ref[...] += ds.reshape(D_GROUP, QB, 2 * QB)
            dsk_ref[...] += (-jnp.exp(sink4 - lse4) * delta).reshape(D_GROUP, QB, 1)
            dq4 = jnp.zeros((QB, 256), F32)
            for g in range(D_GROUP):
                dq4 = dq4 + _place_head(dq4s[g * QB:(g + 1) * QB], e4[g])
            dq_ref[pl.ds(qs, QB), :] = dq4.astype(dq_ref.dtype)
            dkh_ref[pl.ds(ps, QB), :] += dkb[:QB]
            dkh_ref[pl.ds(qs, QB), :] += dkb[QB:]
            dvh_ref[pl.ds(ps, QB), :] += dvb[:QB]
            dvh_ref[pl.ds(qs, QB), :] += dvb[QB:]
            return 0

        lax.fori_loop(0, nb, step, 0)
        dk_ref[...] = dkh_ref[...].astype(dk_ref.dtype)
        dv_ref[...] = dvh_ref[...].astype(dv_ref.dtype)

    blk256 = pl.BlockSpec((s, 256), lambda kh, b: (b, kh))
    blk_kv = pl.BlockSpec((None, s, d), lambda kh, b: (kh, b, 0))
    blk_b = pl.BlockSpec((D_GROUP, QB, 2 * QB), lambda kh, b: (kh, 0, 0))
    blk_s = pl.BlockSpec((D_GROUP, QB, 1), lambda kh, b: (kh, 0, 0))
    return pl.pallas_call(
        body, name=name, grid=(D_KV_HEADS, bn),
        in_specs=[pl.BlockSpec((s, 256), lambda kh, b: (b, C_DQ // 256 + kh)),
                  pl.BlockSpec((s, 128), lambda kh, b: (b, C_DK // 128)),
                  pl.BlockSpec((s, 128), lambda kh, b: (b, C_DV // 128)),
                  blk_b, blk_s, blk256,
                  pl.BlockSpec((None, D_GROUP, nb, 1, QB), lambda kh, b: (b, kh, 0, 0, 0)), blk256],
        out_specs=[blk256, blk_kv, blk_kv, blk_b, blk_s],
        out_shape=[jax.ShapeDtypeStruct((bn * s, 512), BF16), jax.ShapeDtypeStruct((D_KV_HEADS, bn * s, d), BF16),
                   jax.ShapeDtypeStruct((D_KV_HEADS, bn * s, d), BF16),
                   jax.ShapeDtypeStruct((D_Q_HEADS, QB, 2 * QB), F32), jax.ShapeDtypeStruct((D_Q_HEADS, QB, 1), F32)],
        scratch_shapes=[pltpu.VMEM((D_GROUP, s, d), BF16), pltpu.VMEM((s, d), BF16), pltpu.VMEM((s, d), BF16),
                        pltpu.VMEM((D_GROUP, s, d), BF16), pltpu.VMEM((D_GROUP, s, d), BF16),
                        pltpu.VMEM((s, d), F32), pltpu.VMEM((s, d), F32)],
        compiler_params=_cparams(("parallel", "arbitrary")),
    )(proj, proj, proj, bias, sinks, yd, lse, dyd)


def assemble_dproj(pieces, dk, dv, daf, name):
    t = pieces[0].shape[0]
    tm = 512
    widths = [p.shape[1] for p in pieces]
    npc = len(pieces)
    assert sum(widths) == C_DK and all(w % 128 == 0 for w in widths)

    def body(*refs):
        p_refs = refs[:npc]
        dk_ref, dv_ref, af_ref, o_ref = refs[npc:]
        off = 0
        for r, w in zip(p_refs, widths):
            o_ref[:, off:off + w] = r[...]
            off += w
        e2 = _sel(D_KV_HEADS, 128)
        for r in (dk_ref, dv_ref):
            val = _place_head(r[0], e2[0]) + _place_head(r[1], e2[1])
            o_ref[:, off:off + 128] = val.astype(o_ref.dtype)
            off += 128
        o_ref[:, off:off + 128] = af_ref[...]
        off += 128
        o_ref[:, off:] = jnp.zeros((tm, N_PROJ - off), o_ref.dtype)

    kv_blk = pl.BlockSpec((D_KV_HEADS, tm, HEAD_DIM), lambda i: (0, i, 0))
    return pl.pallas_call(
        body, name=name, grid=(t // tm,),
        in_specs=[pl.BlockSpec((tm, w), lambda i: (i, 0)) for w in widths]
        + [kv_blk, kv_blk, pl.BlockSpec((tm, 128), lambda i: (i, 0))],
        out_specs=pl.BlockSpec((tm, N_PROJ), lambda i: (i, 0)),
        out_shape=jax.ShapeDtypeStruct((t, N_PROJ), BF16),
        compiler_params=_cparams(("parallel",)),
    )(*pieces, dk, dv, daf)


def _bucket_table():
    dist = jnp.maximum(jnp.arange(QB)[:, None] + QB - jnp.arange(2 * QB)[None, :], 0)
    max_exact = REL_BUCKETS // 2
    large = max_exact + (jnp.log(jnp.maximum(dist, 1).astype(F32) / max_exact)
                         / math.log(REL_MAX_DIST / max_exact) * (REL_BUCKETS - max_exact)).astype(jnp.int32)
    large = jnp.minimum(large, REL_BUCKETS - 1)
    return jnp.where(dist < max_exact, dist, large).astype(F32)


def band_bias_fwd(bucket, rel_bias, name):
    def body(bk_ref, rel_ref, o_ref):
        bk = bk_ref[...]
        for hh in range(D_Q_HEADS):
            acc = jnp.zeros(bk.shape, F32)
            for b in range(REL_BUCKETS):
                acc = jnp.where(bk == float(b), rel_ref[b, hh], acc)
            o_ref[hh] = acc

    return pl.pallas_call(
        body, name=name,
        in_specs=[pl.BlockSpec(memory_space=pltpu.VMEM), pl.BlockSpec(memory_space=pltpu.SMEM)],
        out_specs=pl.BlockSpec(memory_space=pltpu.VMEM),
        out_shape=jax.ShapeDtypeStruct((D_Q_HEADS, QB, 2 * QB), F32),
    )(bucket, rel_bias)


def band_bias_bwd(bucket, dbias_layers, name):
    nl = len(dbias_layers)

    def body(bk_ref, *refs):
        o_ref = refs[nl]
        bk = bk_ref[...]
        for hh in range(D_Q_HEADS):
            tot = refs[0][hh]
            for r in refs[1:nl]:
                tot = tot + r[hh]
            for b in range(REL_BUCKETS):
                part = jnp.sum(jnp.where(bk == float(b), tot, 0.0), axis=0, keepdims=True)
                val = jnp.sum(part, axis=1, keepdims=True)
                o_ref[hh, b:b + 1, :] = jnp.broadcast_to(val, (1, 128))

    return pl.pallas_call(
        body, name=name,
        in_specs=[pl.BlockSpec(memory_space=pltpu.VMEM)] * (nl + 1),
        out_specs=pl.BlockSpec(memory_space=pltpu.VMEM),
        out_shape=jax.ShapeDtypeStruct((D_Q_HEADS, REL_BUCKETS, 128), F32),
    )(bucket, *dbias_layers)


def _proj_blk(s, col):
    return pl.BlockSpec((s, 256), functools.partial(lambda b, cb: (b, cb), cb=col // 256))


def convb_fwd(proj, w, bn, s, name):
    kk = w.shape[0]

    def body(bg_ref, cg_ref, xb_ref, w_ref, o_ref):
        x = cg_ref[...].astype(F32) * xb_ref[...].astype(F32)
        row = lax.broadcasted_iota(jnp.int32, x.shape, 0)
        y = jnp.zeros_like(x)
        for k in range(kk):
            y = y + w_ref[k:k + 1, :] * _shift_down(x, kk - 1 - k, row)
        o_ref[...] = (bg_ref[...].astype(F32) * y).astype(o_ref.dtype)

    return pl.pallas_call(
        body, name=name, grid=(bn,),
        in_specs=[_proj_blk(s, C_BG), _proj_blk(s, C_CG), _proj_blk(s, C_XB), pl.BlockSpec(w.shape, lambda b: (0, 0))],
        out_specs=pl.BlockSpec((s, 256), lambda b: (b, 0)),
        out_shape=jax.ShapeDtypeStruct((bn * s, 256), BF16),
        compiler_params=_cparams(("parallel",)),
    )(proj, proj, proj, w)


def convb_bwd(proj, w, dyb, bn, s, name):
    kk = w.shape[0]

    def body(bg_ref, cg_ref, xb_ref, w_ref, d_ref, dbg_ref, dcg_ref, dxb_ref, dw_ref):
        @pl.when(pl.program_id(0) == 0)
        def _():
            dw_ref[...] = jnp.zeros_like(dw_ref)

        cg = cg_ref[...].astype(F32)
        xb = xb_ref[...].astype(F32)
        d = d_ref[...].astype(F32)
        x = cg * xb
        row = lax.broadcasted_iota(jnp.int32, x.shape, 0)
        dy = d * bg_ref[...].astype(F32)
        y = jnp.zeros_like(x)
        dx = jnp.zeros_like(x)
        for k in range(kk):
            xs = _shift_down(x, kk - 1 - k, row)
            y = y + w_ref[k:k + 1, :] * xs
            dx = dx + w_ref[k:k + 1, :] * _shift_up(dy, kk - 1 - k, row)
            dw_ref[k:k + 1, :] += _sum0(dy * xs)
        dbg_ref[...] = (d * y).astype(dbg_ref.dtype)
        dcg_ref[...] = (dx * xb).astype(dcg_ref.dtype)
        dxb_ref[...] = (dx * cg).astype(dxb_ref.dtype)

    blk = pl.BlockSpec((s, 256), lambda b: (b, 0))
    return pl.pallas_call(
        body, name=name, grid=(bn,),
        in_specs=[_proj_blk(s, C_BG), _proj_blk(s, C_CG), _proj_blk(s, C_XB), pl.BlockSpec(w.shape, lambda b: (0, 0)),
                  blk],
        out_specs=[blk, blk, blk, pl.BlockSpec((8, 256), lambda b: (0, 0))],
        out_shape=[jax.ShapeDtypeStruct((bn * s, 256), BF16)] * 3 + [jax.ShapeDtypeStruct((8, 256), F32)],
        compiler_params=_cparams(("arbitrary",)),
    )(proj, proj, proj, w, dyb)


def _convc_core(ca, cb, w_ref, bias, kk, row):
    sg = jax.nn.sigmoid(cb)
    glu = ca * sg
    y = jnp.zeros_like(glu)
    for k in range(kk):
        y = y + w_ref[k:k + 1, :] * _shift_down(glu, kk - 1 - k, row)
    y = y + bias
    mu = jnp.mean(y, axis=-1, keepdims=True)
    xc = y - mu
    r = lax.rsqrt(jnp.mean(xc * xc, axis=-1, keepdims=True) + EPS)
    return sg, glu, xc * r, r


def convc_fwd(proj, w, bias, gain, lbias, bn, s, name):
    kk = w.shape[0]

    def body(ca_ref, cb_ref, w_ref, b_ref, g_ref, lb_ref, o_ref):
        ca = ca_ref[...].astype(F32)
        row = lax.broadcasted_iota(jnp.int32, ca.shape, 0)
        _, _, xh, _ = _convc_core(ca, cb_ref[...].astype(F32), w_ref, b_ref[...], kk, row)
        ln = xh * g_ref[...] + lb_ref[...]
        o_ref[...] = (ln * jax.nn.sigmoid(ln)).astype(o_ref.dtype)

    vec = pl.BlockSpec((1, 256), lambda b: (0, 0))
    return pl.pallas_call(
        body, name=name, grid=(bn,),
        in_specs=[_proj_blk(s, C_CA), _proj_blk(s, C_CB), pl.BlockSpec(w.shape, lambda b: (0, 0)), vec, vec, vec],
        out_specs=pl.BlockSpec((s, 256), lambda b: (b, 0)),
        out_shape=jax.ShapeDtypeStruct((bn * s, 256), BF16),
        compiler_params=_cparams(("parallel",)),
    )(proj, proj, w, bias, gain, lbias)


def convc_bwd(proj, w, bias, gain, lbias, dyc, bn, s, name):
    kk = w.shape[0]

    def body(ca_ref, cb_ref, w_ref, b_ref, g_ref, lb_ref, d_ref, dca_ref, dcb_ref, dw_ref, db_ref, dg_ref, dlb_ref):
        @pl.when(pl.program_id(0) == 0)
        def _():
            dw_ref[...] = jnp.zeros_like(dw_ref)
            db_ref[...] = jnp.zeros_like(db_ref)
            dg_ref[...] = jnp.zeros_like(dg_ref)
            dlb_ref[...] = jnp.zeros_like(dlb_ref)

        ca = ca_ref[...].astype(F32)
        row = lax.broadcasted_iota(jnp.int32, ca.shape, 0)
        sg, glu, xh, r = _convc_core(ca, cb_ref[...].astype(F32), w_ref, b_ref[...], kk, row)
        ln = xh * g_ref[...] + lb_ref[...]
        sl = jax.nn.sigmoid(ln)
        dl = d_ref[...].astype(F32) * (sl + ln * sl * (1.0 - sl))
        dg_ref[...] += _sum0(dl * xh)
        dlb_ref[...] += _sum0(dl)
        dxh = dl * g_ref[...]
        dy = r * (dxh - jnp.mean(dxh, axis=-1, keepdims=True) - xh * jnp.mean(dxh * xh, axis=-1, keepdims=True))
        db_ref[...] += _sum0(dy)
        dglu = jnp.zeros_like(glu)
        for k in range(kk):
            dw_ref[k:k + 1, :] += _sum0(dy * _shift_down(glu, kk - 1 - k, row))
            dglu = dglu + w_ref[k:k + 1, :] * _shift_up(dy, kk - 1 - k, row)
        dca_ref[...] = (dglu * sg).astype(dca_ref.dtype)
        dcb_ref[...] = (dglu * ca * sg * (1.0 - sg)).astype(dcb_ref.dtype)

    vec = pl.BlockSpec((1, 256), lambda b: (0, 0))
    blk = pl.BlockSpec((s, 256), lambda b: (b, 0))
    return pl.pallas_call(
        body, name=name, grid=(bn,),
        in_specs=[_proj_blk(s, C_CA), _proj_blk(s, C_CB), pl.BlockSpec(w.shape, lambda b: (0, 0)), vec, vec, vec, blk],
        out_specs=[blk, blk, pl.BlockSpec((32, 256), lambda b: (0, 0)), vec, vec, vec],
        out_shape=[jax.ShapeDtypeStruct((bn * s, 256), BF16)] * 2 + [jax.ShapeDtypeStruct((32, 256), F32)]
        + [jax.ShapeDtypeStruct((1, 256), F32)] * 3,
        compiler_params=_cparams(("arbitrary",)),
    )(proj, proj, w, bias, gain, lbias, dyc)


def adamw(w, g, m, v, name):
    shape = w.shape
    cols = shape[-1]
    rows = w.size // cols
    tr = _pick(rows, (256, 128, 64, 32, 16, 8))

    def body(w_ref, g_ref, m_ref, v_ref, d_ref, nm_ref, nv_ref):
        gg = g_ref[...]
        mm = ADAM_B1 * m_ref[...] + (1.0 - ADAM_B1) * gg
        vv = ADAM_B2 * v_ref[...] + (1.0 - ADAM_B2) * jnp.square(gg)
        m_hat = mm / (1.0 - ADAM_B1 ** ADAM_STEP)
        v_hat = vv / (1.0 - ADAM_B2 ** ADAM_STEP)
        d_ref[...] = -ADAM_LR * (m_hat / (jnp.sqrt(v_hat) + ADAM_EPS) + ADAM_WD * w_ref[...])
        nm_ref[...] = mm
        nv_ref[...] = vv

    blk = pl.BlockSpec((tr, cols), lambda i: (i, 0))
    outs = pl.pallas_call(
        body, name=name, grid=(rows // tr,), in_specs=[blk] * 4, out_specs=[blk] * 3,
        out_shape=[jax.ShapeDtypeStruct((rows, cols), F32)] * 3,
        compiler_params=_cparams(("parallel",)),
    )(*[a.reshape(rows, cols) for a in (w, g, m, v)])
    return [o.reshape(shape) for o in outs]


def add_halves(own, recv, name):
    n, r, c = own.shape
    tr = _pick(r, (512, 256, 128, 64, 32, 16, 8))
    blk = pl.BlockSpec((None, tr, c), lambda i, j: (i, j, 0))

    def body(a_ref, b_ref, o_ref):
        o_ref[...] = a_ref[...] + b_ref[...]

    return pl.pallas_call(
        body, name=name, grid=(n, r // tr), in_specs=[blk, blk], out_specs=blk,
        out_shape=jax.ShapeDtypeStruct((n, r, c), F32), compiler_params=_cparams(("parallel", "parallel")),
    )(own, recv)


def sum_slots(slots, name):
    n, r, c = slots.shape
    tr = _pick(r, (512, 256, 128, 64, 32, 16, 8))

    def body(a_ref, o_ref):
        acc = a_ref[0]
        for k in range(1, n):
            acc = acc + a_ref[k]
        o_ref[...] = acc

    return pl.pallas_call(
        body, name=name, grid=(r // tr,), in_specs=[pl.BlockSpec((n, tr, c), lambda j: (0, j, 0))],
        out_specs=pl.BlockSpec((tr, c), lambda j: (j, 0)),
        out_shape=jax.ShapeDtypeStruct((r, c), F32), compiler_params=_cparams(("parallel",)),
    )(slots)


ANY = pl.BlockSpec(memory_space=pl.ANY)


def _place():
    x, y, c = lax.axis_index("x"), lax.axis_index("y"), lax.axis_index("c")
    return x, y, c


def gather_shards(pack, name):
    r, cols = pack.shape
    half = r // 2

    def body(src_ref, out_ref, send_sems, recv_sems, local_sem):
        x, y, c = _place()
        sibling = (x, y, 1 - c)
        chips = [(1 - x, y), (x, 1 - y), (1 - x, 1 - y)]

        def rows(px, py, pc):
            return out_ref.at[2 * px + py, pl.ds(pc * half, half), :]

        mine = pltpu.make_async_copy(src_ref, out_ref.at[2 * x + y], local_sem)
        mine.start()

        def copy(k, blk, to, src=None):
            return pltpu.make_async_remote_copy(
                src_ref=rows(*blk) if src is None else src, dst_ref=rows(*blk),
                send_sem=send_sems.at[k], recv_sem=recv_sems.at[k], device_id=to, device_id_type=MESH)

        first = [copy(j, (x, y, c), (*chip, c), src=src_ref.at[pl.ds(c * half, half), :])
                 for j, chip in enumerate(chips)]
        for cp in first:
            cp.start()
        passed = [copy(3 + j, (*chip, c), sibling) for j, chip in enumerate(chips)]
        for j, chip in enumerate(chips):
            copy(j, (*chip, c), (x, y, c)).wait_recv()
            passed[j].start()
        for j, chip in enumerate(chips):
            copy(3 + j, (*chip, 1 - c), (x, y, c)).wait_recv()
        for cp in first + passed:
            cp.wait_send()
        mine.wait()

    return pl.pallas_call(
        body, name=name, in_specs=[ANY], out_specs=ANY,
        out_shape=jax.ShapeDtypeStruct((N_CHIPS, r, cols), pack.dtype),
        scratch_shapes=[pltpu.SemaphoreType.DMA((6,)), pltpu.SemaphoreType.DMA((6,)), pltpu.SemaphoreType.DMA],
    )(pack)


def exchange_sibling_halves(g, name):
    n, r, cols = g.shape
    half = r // 2

    def body(g_ref, own_ref, recv_ref, send_sems, recv_sems, local_sem):
        x, y, c = _place()
        sibling = (x, y, 1 - c)
        mine = pltpu.make_async_copy(g_ref.at[:, pl.ds(c * half, half), :], own_ref, local_sem)
        mine.start()
        cp = pltpu.make_async_remote_copy(
            src_ref=g_ref.at[:, pl.ds((1 - c) * half, half), :], dst_ref=recv_ref,
            send_sem=send_sems.at[0], recv_sem=recv_sems.at[0], device_id=sibling, device_id_type=MESH)
        cp.start()
        cp.wait()
        mine.wait()

    return pl.pallas_call(
        body, name=name, in_specs=[ANY], out_specs=[ANY, ANY],
        out_shape=[jax.ShapeDtypeStruct((n, half, cols), g.dtype)] * 2,
        scratch_shapes=[pltpu.SemaphoreType.DMA((1,)), pltpu.SemaphoreType.DMA((1,)), pltpu.SemaphoreType.DMA],
    )(g)


def scatter_to_chips(part, name):
    n, h, cols = part.shape

    def body(p_ref, slot_ref, send_sems, recv_sems, local_sem):
        x, y, c = _place()
        me = 2 * x + y
        chips = [(1 - x, y), (x, 1 - y), (1 - x, 1 - y)]
        mine = pltpu.make_async_copy(p_ref.at[me], slot_ref.at[me], local_sem)
        mine.start()
        cps = [pltpu.make_async_remote_copy(
            src_ref=p_ref.at[2 * px + py], dst_ref=slot_ref.at[me],
            send_sem=send_sems.at[j], recv_sem=recv_sems.at[j], device_id=(px, py, c), device_id_type=MESH)
            for j, (px, py) in enumerate(chips)]
        for cp in cps:
            cp.start()
        for j, (px, py) in enumerate(chips):
            pltpu.make_async_remote_copy(
                src_ref=p_ref.at[me], dst_ref=slot_ref.at[2 * px + py],
                send_sem=send_sems.at[j], recv_sem=recv_sems.at[j], device_id=(px, py, c),
                device_id_type=MESH).wait_recv()
        for cp in cps:
            cp.wait_send()
        mine.wait()

    return pl.pallas_call(
        body, name=name, in_specs=[ANY], out_specs=ANY,
        out_shape=jax.ShapeDtypeStruct((n, h, cols), part.dtype),
        scratch_shapes=[pltpu.SemaphoreType.DMA((3,)), pltpu.SemaphoreType.DMA((3,)), pltpu.SemaphoreType.DMA],
    )(part)


def join_sibling_halves(mine_half, name):
    h, cols = mine_half.shape

    def body(m_ref, out_ref, send_sems, recv_sems, local_sem):
        x, y, c = _place()
        sibling = (x, y, 1 - c)
        own = pltpu.make_async_copy(m_ref, out_ref.at[pl.ds(c * h, h), :], local_sem)
        own.start()
        cp = pltpu.make_async_remote_copy(
            src_ref=m_ref, dst_ref=out_ref.at[pl.ds(c * h, h), :],
            send_sem=send_sems.at[0], recv_sem=recv_sems.at[0], device_id=sibling, device_id_type=MESH)
        cp.start()
        pltpu.make_async_remote_copy(
            src_ref=m_ref, dst_ref=out_ref.at[pl.ds((1 - c) * h, h), :],
            send_sem=send_sems.at[0], recv_sem=recv_sems.at[0], device_id=sibling, device_id_type=MESH).wait_recv()
        cp.wait_send()
        own.wait()

    return pl.pallas_call(
        body, name=name, in_specs=[ANY], out_specs=ANY,
        out_shape=jax.ShapeDtypeStruct((2 * h, cols), mine_half.dtype),
        scratch_shapes=[pltpu.SemaphoreType.DMA((1,)), pltpu.SemaphoreType.DMA((1,)), pltpu.SemaphoreType.DMA],
    )(mine_half)


def _kind(n):
    return 'win' if n == 'w_in' else ('row' if n in ROW_SHARDED else 'col')


def _chip_ids():
    x, y, c = _place()
    chips = [(1 - x, y), (x, 1 - y), (1 - x, 1 - y)]
    return x, y, c, 2 * x + y, chips, [2 * px + py for px, py in chips]


def gather_weights(shards, name):
    names = list(SHARDED)
    nt = len(names)
    kinds = [_kind(n) for n in names]
    shapes = [shards[n].shape for n in names]
    depth = shapes[0][0]
    half = depth // 2

    def out_shape(kind, shp):
        if kind == 'col':
            return (shp[0], shp[1], N_CHIPS * shp[2])
        if kind == 'row':
            return (shp[0], N_CHIPS * shp[1], shp[2])
        return (N_CHIPS,) + tuple(shp)

    def body(*refs):
        src, out = refs[:nt], refs[nt:2 * nt]
        send_sems, recv_sems = refs[2 * nt:]
        x, y, c, me, chips, chip_idx = _chip_ids()
        sibling = (x, y, 1 - c)

        def win(t, chip, lo, cnt):
            _, a, b = shapes[t]
            if kinds[t] == 'col':
                return out[t].at[pl.ds(lo, cnt), :, pl.ds(chip * b, b)]
            if kinds[t] == 'row':
                return out[t].at[pl.ds(lo, cnt), pl.ds(chip * a, a), :]
            return out[t].at[chip, pl.ds(lo, cnt)]

        def remote(t, k, chip, lo, to, src_ref=None):
            w = win(t, chip, lo, half)
            return pltpu.make_async_remote_copy(
                src_ref=w if src_ref is None else src_ref, dst_ref=w, send_sem=send_sems.at[7 * t + k],
                recv_sem=recv_sems.at[7 * t + k], device_id=to, device_id_type=MESH)

        def own(t):
            return pltpu.make_async_remote_copy(
                src_ref=src[t], dst_ref=win(t, me, 0, depth), send_sem=send_sems.at[7 * t + 6],
                recv_sem=recv_sems.at[7 * t + 6], device_id=sibling, device_id_type=MESH)

        mine = [own(t) for t in range(nt)]
        for cp in mine:
            cp.start()
        first = [[remote(t, j, me, c * half, (*chips[j], c), src_ref=src[t].at[pl.ds(c * half, half)])
                  for j in range(3)] for t in range(nt)]
        for t in range(nt):
            for cp in first[t]:
                cp.start()
        passed = [[remote(t, 3 + j, chip_idx[j], c * half, sibling) for j in range(3)] for t in range(nt)]
        for t in range(nt):
            for j in range(3):
                remote(t, j, chip_idx[j], c * half, (x, y, c)).wait_recv()
                passed[t][j].start()
        for t in range(nt):
            for j in range(3):
                remote(t, 3 + j, chip_idx[j], (1 - c) * half, (x, y, c)).wait_recv()
        for t in range(nt):
            for cp in first[t] + passed[t]:
                cp.wait_send()
            mine[t].wait()

    outs = pl.pallas_call(
        body, name=name, in_specs=[ANY] * nt, out_specs=[ANY] * nt,
        out_shape=[jax.ShapeDtypeStruct(out_shape(k, s), BF16) for k, s in zip(kinds, shapes)],
        scratch_shapes=[pltpu.SemaphoreType.DMA((7 * nt,)), pltpu.SemaphoreType.DMA((7 * nt,))],
    )(*[shards[n] for n in names])
    return dict(zip(names, outs))


def _half_win(ref, kind, hc, layer):
    if kind == 'col':
        hk = ref.shape[1] // 2
        return ref.at[layer, pl.ds(hc * hk, hk), :]
    if kind == 'row':
        hn = ref.shape[2] // 2
        return ref.at[layer, :, pl.ds(hc * hn, hn)]
    hk = ref.shape[2] // 2
    return ref.at[layer, :, pl.ds(hc * hk, hk), :]


def _half_shape(kind, shp):
    if kind == 'col':
        return (shp[0], shp[1] // 2, shp[2])
    if kind == 'row':
        return (shp[0], shp[1], shp[2] // 2)
    return (shp[0], shp[1], shp[2] // 2, shp[3])


def rs_sibling(grads, name):
    names = list(SHARDED)
    nt = len(names)
    kinds = [_kind(n) for n in names]
    shapes = [grads[n].shape for n in names]
    depth = shapes[0][0]

    def body(*refs):
        src, out = refs[:nt], refs[nt:2 * nt]
        send_sems, recv_sems = refs[2 * nt:]
        x, y, c = _place()
        cps = []
        for t in range(nt):
            for l in range(depth):
                cps.append(pltpu.make_async_remote_copy(
                    src_ref=_half_win(src[t], kinds[t], 1 - c, l), dst_ref=out[t].at[l],
                    send_sem=send_sems.at[depth * t + l], recv_sem=recv_sems.at[depth * t + l],
                    device_id=(x, y, 1 - c), device_id_type=MESH))
        for cp in cps:
            cp.start()
        for cp in cps:
            cp.wait()

    outs = pl.pallas_call(
        body, name=name, in_specs=[ANY] * nt, out_specs=[ANY] * nt,
        out_shape=[jax.ShapeDtypeStruct(_half_shape(k, s), F32) for k, s in zip(kinds, shapes)],
        scratch_shapes=[pltpu.SemaphoreType.DMA((depth * nt,)), pltpu.SemaphoreType.DMA((depth * nt,))],
    )(*[grads[n] for n in names])
    return dict(zip(names, outs))


EW_BLOCK_ELEMS = 512 * 1024


def rs_add(kind, g, recv, c_arr, name):
    shp = recv.shape
    rows, cols = shp[-2], shp[-1]
    tr = _pick(rows, [r for r in (1408, 1024, 704, 512, 256, 128, 64, 32, 16, 8) if r * cols <= EW_BLOCK_ELEMS])
    nb = rows // tr
    lead = (None,) * (len(shp) - 2)
    blk = pl.BlockSpec(lead + (tr, cols), lambda *a: tuple(a[:len(shp) - 2]) + (a[len(shp) - 2], 0))
    if kind == 'row':
        g_blk = pl.BlockSpec(lead + (tr, cols), lambda *a: tuple(a[:len(shp) - 2]) + (a[len(shp) - 2], a[-1][0]))
    else:
        g_blk = pl.BlockSpec(lead + (tr, cols),
                             lambda *a: tuple(a[:len(shp) - 2]) + (a[-1][0] * nb + a[len(shp) - 2], 0))

    def body(c_ref, g_ref, r_ref, o_ref):
        o_ref[...] = (g_ref[...] + r_ref[...]).astype(o_ref.dtype)

    grid_spec = pltpu.PrefetchScalarGridSpec(
        num_scalar_prefetch=1, grid=tuple(shp[:-2]) + (nb,), in_specs=[g_blk, blk], out_specs=blk)
    return pl.pallas_call(
        body, name=name, grid_spec=grid_spec, out_shape=jax.ShapeDtypeStruct(shp, BF16),
        compiler_params=_cparams(None),
    )(c_arr, g, recv)


def _chip_win(ref, kind, chip):
    if kind == 'col':
        ns = ref.shape[2] // N_CHIPS
        return ref.at[:, :, pl.ds(chip * ns, ns)]
    if kind == 'row':
        ks = ref.shape[1] // N_CHIPS
        return ref.at[:, pl.ds(chip * ks, ks), :]
    return ref.at[:, chip]


def _chip_shape(kind, shp):
    if kind == 'col':
        return (shp[0], shp[1], shp[2] // N_CHIPS)
    if kind == 'row':
        return (shp[0], shp[1] // N_CHIPS, shp[2])
    return (shp[0], shp[2], shp[3])


def rs_chips(parts, name):
    names = list(SHARDED)
    nt = len(names)
    kinds = [_kind(n) for n in names]
    shapes = [parts[n].shape for n in names]

    def body(*refs):
        src, out = refs[:nt], refs[nt:2 * nt]
        send_sems, recv_sems, local_sems = refs[2 * nt:]
        x, y, c, me, chips, chip_idx = _chip_ids()
        mine = [pltpu.make_async_copy(_chip_win(src[t], kinds[t], me), out[t].at[me], local_sems.at[t])
                for t in range(nt)]
        for cp in mine:
            cp.start()
        cps = [[pltpu.make_async_remote_copy(
            src_ref=_chip_win(src[t], kinds[t], chip_idx[j]), dst_ref=out[t].at[me],
            send_sem=send_sems.at[3 * t + j], recv_sem=recv_sems.at[3 * t + j],
            device_id=(*chips[j], c), device_id_type=MESH) for j in range(3)] for t in range(nt)]
        for t in range(nt):
            for cp in cps[t]:
                cp.start()
        for t in range(nt):
            for j in range(3):
                pltpu.make_async_remote_copy(
                    src_ref=_chip_win(src[t], kinds[t], me), dst_ref=out[t].at[chip_idx[j]],
                    send_sem=send_sems.at[3 * t + j], recv_sem=recv_sems.at[3 * t + j],
                    device_id=(*chips[j], c), device_id_type=MESH).wait_recv()
        for t in range(nt):
            for cp in cps[t]:
                cp.wait_send()
            mine[t].wait()

    outs = pl.pallas_call(
        body, name=name, in_specs=[ANY] * nt, out_specs=[ANY] * nt,
        out_shape=[jax.ShapeDtypeStruct((N_CHIPS,) + _chip_shape(k, s), parts[n].dtype)
                   for n, k, s in zip(names, kinds, shapes)],
        scratch_shapes=[pltpu.SemaphoreType.DMA((3 * nt,)), pltpu.SemaphoreType.DMA((3 * nt,)),
                        pltpu.SemaphoreType.DMA((nt,))],
    )(*[parts[n] for n in names])
    return dict(zip(names, outs))


def rs_sum(slots, name):
    n, depth, r, cols = slots.shape
    tr = _pick(r, [q for q in (1408, 1024, 704, 512, 256, 128, 64, 32, 16, 8) if q * cols * n <= 2 * EW_BLOCK_ELEMS])

    def body(a_ref, o_ref):
        acc = a_ref[0].astype(F32)
        for k in range(1, n):
            acc = acc + a_ref[k].astype(F32)
        o_ref[...] = acc

    return pl.pallas_call(
        body, name=name, grid=(depth, r // tr),
        in_specs=[pl.BlockSpec((n, None, tr, cols), lambda l, i: (0, l, i, 0))],
        out_specs=pl.BlockSpec((None, tr, cols), lambda l, i: (l, i, 0)),
        out_shape=jax.ShapeDtypeStruct((depth, r, cols), F32), compiler_params=_cparams(("parallel", "parallel")),
    )(slots)


def rs_join(reds, name):
    names = list(SHARDED)
    nt = len(names)
    kinds = [_kind(n) for n in names]
    shapes = [reds[n].shape for n in names]
    depth = shapes[0][0]

    def full_shape(kind, shp):
        if kind == 'row':
            return (shp[0], shp[1], 2 * shp[2])
        return (shp[0], 2 * shp[1], shp[2])

    def win(ref, kind, hc, layer):
        if kind == 'row':
            hn = ref.shape[2] // 2
            return ref.at[layer, :, pl.ds(hc * hn, hn)]
        hk = ref.shape[1] // 2
        return ref.at[layer, pl.ds(hc * hk, hk), :]

    def body(*refs):
        src, out = refs[:nt], refs[nt:2 * nt]
        send_sems, recv_sems, local_sems = refs[2 * nt:]
        x, y, c = _place()
        own, cps = [], []
        for t in range(nt):
            for l in range(depth):
                i = depth * t + l
                own.append(pltpu.make_async_copy(src[t].at[l], win(out[t], kinds[t], c, l), local_sems.at[i]))
                cps.append(pltpu.make_async_remote_copy(
                    src_ref=src[t].at[l], dst_ref=win(out[t], kinds[t], c, l), send_sem=send_sems.at[i],
                    recv_sem=recv_sems.at[i], device_id=(x, y, 1 - c), device_id_type=MESH))
        for cp in own + cps:
            cp.start()
        for t in range(nt):
            for l in range(depth):
                i = depth * t + l
                pltpu.make_async_remote_copy(
                    src_ref=src[t].at[l], dst_ref=win(out[t], kinds[t], 1 - c, l), send_sem=send_sems.at[i],
                    recv_sem=recv_sems.at[i], device_id=(x, y, 1 - c), device_id_type=MESH).wait_recv()
        for cp in cps:
            cp.wait_send()
        for cp in own:
            cp.wait()

    outs = pl.pallas_call(
        body, name=name, in_specs=[ANY] * nt, out_specs=[ANY] * nt,
        out_shape=[jax.ShapeDtypeStruct(full_shape(k, s), F32) for k, s in zip(kinds, shapes)],
        scratch_shapes=[pltpu.SemaphoreType.DMA((depth * nt,)), pltpu.SemaphoreType.DMA((depth * nt,)),
                        pltpu.SemaphoreType.DMA((depth * nt,))],
    )(*[reds[n] for n in names])
    return dict(zip(names, outs))


def rs_chips2(parts, name):
    names = list(SHARDED)
    nt = len(names)
    kinds = [_kind(n) for n in names]
    shapes = [parts[n].shape for n in names]

    def body(*refs):
        src, out = refs[:nt], refs[nt:2 * nt]
        send_sems, recv_sems = refs[2 * nt:]
        x, y, c, me, chips, chip_idx = _chip_ids()
        cps = [[pltpu.make_async_remote_copy(
            src_ref=_chip_win(src[t], kinds[t], chip_idx[j]), dst_ref=out[t].at[j],
            send_sem=send_sems.at[3 * t + j], recv_sem=recv_sems.at[3 * t + j],
            device_id=(*chips[j], c), device_id_type=MESH) for j in range(3)] for t in range(nt)]
        for t in range(nt):
            for cp in cps[t]:
                cp.start()
        for t in range(nt):
            for cp in cps[t]:
                cp.wait()

    outs = pl.pallas_call(
        body, name=name, in_specs=[ANY] * nt, out_specs=[ANY] * nt,
        out_shape=[jax.ShapeDtypeStruct((3,) + _chip_shape(k, s), parts[n].dtype)
                   for n, k, s in zip(names, kinds, shapes)],
        scratch_shapes=[pltpu.SemaphoreType.DMA((3 * nt,)), pltpu.SemaphoreType.DMA((3 * nt,))],
    )(*[parts[n] for n in names])
    return dict(zip(names, outs))


def rs_sum2(kind, part, slots, sc_arr, name):
    _, depth, r, cols = slots.shape
    tr = _pick(r, [q for q in (1408, 1024, 704, 512, 256, 128, 64, 32, 16) if q * cols <= EW_BLOCK_ELEMS // 2])
    nb = r // tr
    if kind == 'col':
        own_blk = pl.BlockSpec((None, tr, cols), lambda l, i, sc: (l, i, sc[0]))
        out_blk = pl.BlockSpec((None, tr, cols), lambda l, i, sc: (l, sc[1] * nb + i, 0))
        out_shape = (depth, 2 * r, cols)
    elif kind == 'row':
        own_blk = pl.BlockSpec((None, tr, cols), lambda l, i, sc: (l, sc[0] * nb + i, 0))
        out_blk = pl.BlockSpec((None, tr, cols), lambda l, i, sc: (l, i, sc[1]))
        out_shape = (depth, r, 2 * cols)
    else:
        own_blk = pl.BlockSpec((None, None, tr, cols), lambda l, i, sc: (l, sc[0], i, 0))
        out_blk = pl.BlockSpec((None, tr, cols), lambda l, i, sc: (l, sc[1] * nb + i, 0))
        out_shape = (depth, 2 * r, cols)

    def body(sc_ref, own_ref, s_ref, o_ref):
        acc = own_ref[...].astype(F32)
        for k in range(3):
            acc = acc + s_ref[k].astype(F32)
        o_ref[...] = acc

    grid_spec = pltpu.PrefetchScalarGridSpec(
        num_scalar_prefetch=1, grid=(depth, nb),
        in_specs=[own_blk, pl.BlockSpec((3, None, tr, cols), lambda l, i, sc: (0, l, i, 0))], out_specs=out_blk)
    return pl.pallas_call(
        body, name=name, grid_spec=grid_spec, out_shape=jax.ShapeDtypeStruct(out_shape, F32),
        compiler_params=_cparams(None),
    )(sc_arr, part, slots)


def rs_join2(halves, name):
    names = list(SHARDED)
    nt = len(names)
    kinds = [_kind(n) for n in names]
    shapes = [halves[n].shape for n in names]
    depth = shapes[0][0]

    def win(ref, kind, hc, layer):
        if kind == 'row':
            hn = ref.shape[2] // 2
            return ref.at[layer, :, pl.ds(hc * hn, hn)]
        hk = ref.shape[1] // 2
        return ref.at[layer, pl.ds(hc * hk, hk), :]

    def body(*refs):
        src, out = refs[:nt], refs[nt:2 * nt]
        send_sems, recv_sems = refs[2 * nt:]
        x, y, c = _place()
        cps = []
        for t in range(nt):
            for l in range(depth):
                i = depth * t + l
                cps.append(pltpu.make_async_remote_copy(
                    src_ref=win(src[t], kinds[t], c, l), dst_ref=win(out[t], kinds[t], c, l),
                    send_sem=send_sems.at[i], recv_sem=recv_sems.at[i], device_id=(x, y, 1 - c),
                    device_id_type=MESH))
        for cp in cps:
            cp.start()
        for t in range(nt):
            for l in range(depth):
                i = depth * t + l
                pltpu.make_async_remote_copy(
                    src_ref=win(src[t], kinds[t], c, l), dst_ref=win(out[t], kinds[t], 1 - c, l),
                    send_sem=send_sems.at[i], recv_sem=recv_sems.at[i], device_id=(x, y, 1 - c),
                    device_id_type=MESH).wait_recv()
        for cp in cps:
            cp.wait_send()

    outs = pl.pallas_call(
        body, name=name, in_specs=[ANY] * nt, out_specs=[ANY] * nt,
        out_shape=[jax.ShapeDtypeStruct(s, F32) for s in shapes],
        input_output_aliases={t: t for t in range(nt)},
        scratch_shapes=[pltpu.SemaphoreType.DMA((depth * nt,)), pltpu.SemaphoreType.DMA((depth * nt,))],
    )(*[halves[n] for n in names])
    return dict(zip(names, outs))


def gather_small(v, name):
    r, cols = v.shape

    def body(v_ref, out_ref, send_sems, recv_sems):
        x, y, c = _place()
        me = 4 * x + 2 * y + c
        out_ref[me] = v_ref[...]
        cps = []
        for rel in range(1, N_DEV):
            px = 1 - x if (rel >> 2) & 1 else x
            py = 1 - y if (rel >> 1) & 1 else y
            pc = 1 - c if rel & 1 else c
            cps.append(pltpu.make_async_remote_copy(
                src_ref=v_ref, dst_ref=out_ref.at[me], send_sem=send_sems.at[rel - 1],
                recv_sem=recv_sems.at[rel - 1], device_id=(px, py, pc), device_id_type=MESH))
        for cp in cps:
            cp.start()
        for cp in cps:
            cp.wait()

    return pl.pallas_call(
        body, name=name, in_specs=[pl.BlockSpec(memory_space=pltpu.VMEM)],
        out_specs=pl.BlockSpec(memory_space=pltpu.VMEM),
        out_shape=jax.ShapeDtypeStruct((N_DEV, r, cols), v.dtype),
        scratch_shapes=[pltpu.SemaphoreType.DMA((N_DEV - 1,)), pltpu.SemaphoreType.DMA((N_DEV - 1,))],
        compiler_params=pltpu.CompilerParams(vmem_limit_bytes=VMEM_LIMIT),
    )(v)


def _pad_rows(flat, row_align):
    n = flat.shape[-1]
    unit = PACK_COLS * row_align
    tot = -(-n // unit) * unit
    pad = [(0, 0)] * (flat.ndim - 1) + [(0, tot - n)]
    return jnp.pad(flat, pad)


def _pack_shards(ws):
    flat = jnp.concatenate([ws[n].astype(BF16).reshape(-1) for n in SHARDED])
    return _pad_rows(flat, PACK_ROW_ALIGN).reshape(-1, PACK_COLS)


def _unpack_full(gathered, shard_shapes):
    flat = gathered.reshape(N_CHIPS, -1)
    out, off = {}, 0
    for n in SHARDED:
        shp = shard_shapes[n]
        size = math.prod(shp)
        seg = flat[:, off:off + size].reshape((N_CHIPS,) + tuple(shp))
        off += size
        if n in ROW_SHARDED:
            out[n] = jnp.transpose(seg, (1, 0, 2, 3)).reshape(shp[0], N_CHIPS * shp[1], shp[2])
        else:
            out[n] = jnp.transpose(seg, (1, 2, 0, 3)).reshape(shp[0], shp[1], N_CHIPS * shp[2])
    return out


def _pack_grads(gfull, shard_shapes):
    segs = []
    for n in SHARDED:
        shp = shard_shapes[n]
        g = gfull[n]
        if n in ROW_SHARDED:
            seg = jnp.transpose(g.reshape(shp[0], N_CHIPS, shp[1], shp[2]), (1, 0, 2, 3))
        else:
            seg = jnp.transpose(g.reshape(shp[0], shp[1], N_CHIPS, shp[2]), (2, 0, 1, 3))
        segs.append(seg.reshape(N_CHIPS, -1))
    flat = jnp.concatenate(segs, axis=1)
    return _pad_rows(flat, PACK_ROW_ALIGN).reshape(N_CHIPS, -1, PACK_COLS)


def _unpack_shard_grads(red, shard_shapes):
    flat = red.reshape(-1)
    out, off = {}, 0
    for n in SHARDED:
        shp = shard_shapes[n]
        size = math.prod(shp)
        out[n] = flat[off:off + size].reshape(shp)
        off += size
    return out


def _pack_small(parts):
    flat = jnp.concatenate([p.astype(F32).reshape(-1) for p in parts])
    return _pad_rows(flat, 8).reshape(-1, PACK_COLS)


def _unpack_small(flat2d, shapes):
    flat = flat2d.reshape(-1)
    out, off = [], 0
    for shp in shapes:
        size = math.prod(shp)
        out.append(flat[off:off + size].reshape(shp))
        off += size
    return out


def _heads(a, bn, s, h):
    return jnp.transpose(a.reshape(bn, s, h, HEAD_DIM), (0, 2, 1, 3))


def _unheads(a):
    bn, h, s, d = a.shape
    return jnp.transpose(a, (0, 2, 1, 3)).reshape(bn * s, h * d)


def _reorder_w_in(w):
    d = w.shape[0]
    return jnp.concatenate([w[:, 2820:6916], w[:, 0:768], w[:, 772:1540], w[:, 1540:2052], w[:, 2052:2820],
                            w[:, 768:772], jnp.zeros((d, N_PROJ - 6916), w.dtype)], axis=1)


def _restore_dw_in(g):
    return jnp.concatenate([g[:, 4096:4864], g[:, 6912:6916], g[:, 4864:5632], g[:, 5632:6144], g[:, 6144:6912],
                            g[:, 0:4096]], axis=1)


def _ffn_fwd(h, g_pre, w_gu, w_down, g_post, tag):
    n = rms_fwd(h, g_pre, f"{tag}_rms")
    gu = _mm(n, w_gu, out_dtype=BF16, name=f"{tag}_mm_gu")
    a = swiglu_fwd(gu, f"{tag}_swiglu")
    f = _mm(a, w_down, out_dtype=F32, name=f"{tag}_mm_down")
    h_out = res_rms_fwd(h, f, g_post, 0.5, f"{tag}_res")
    return h_out, (h, n, gu, a, f)


def _ffn_bwd(dh, saved, g_pre, w_gu, w_down, g_post, tag, dw, n_gu, n_down):
    h, n, gu, a, f = saved
    df, dg_post = res_rms_bwd(f, g_post, dh, 0.5, f"{tag}_res_bwd")
    da = _mm(df, w_down, tb=True, out_dtype=BF16, name=f"{tag}_mm_da")
    dw(n_down, a, df, f"{tag}_mm_dwdown")
    dgu = swiglu_bwd(gu, da, f"{tag}_swiglu_bwd")
    dw(n_gu, n, dgu, f"{tag}_mm_dwgu")
    dn = _mm(dgu, w_gu, tb=True, out_dtype=BF16, name=f"{tag}_mm_dn")
    dh_in, dg_pre = rms_bwd(h, g_pre, dn, dh, f"{tag}_rms_bwd")
    return dh_in, dg_pre, dg_post


def kernel(x, p, ffn1_norm_pre, ffn1_w_gu, ffn1_w_down, ffn1_norm_post, mix_norm_pre, w_in, b_forget, b_gate, conv_short, conv_dw, conv_dw_bias, conv_ln_gain, conv_ln_bias, attn_sinks, rel_bias, w_br_a, w_br_b, w_br_c, w_br_d, w_o, mix_norm_post, ffn2_norm_pre, ffn2_w_gu, ffn2_w_down, ffn2_norm_post, ple_norm_gate, w_ple_gate, w_ple, ple_norm_post, loss_target, m_ffn1_norm_pre, m_ffn1_w_gu, m_ffn1_w_down, m_ffn1_norm_post, m_mix_norm_pre, m_w_in, m_b_forget, m_b_gate, m_conv_short, m_conv_dw, m_conv_dw_bias, m_conv_ln_gain, m_conv_ln_bias, m_attn_sinks, m_rel_bias, m_w_br_a, m_w_br_b, m_w_br_c, m_w_br_d, m_w_o, m_mix_norm_post, m_ffn2_norm_pre, m_ffn2_w_gu, m_ffn2_w_down, m_ffn2_norm_post, m_ple_norm_gate, m_w_ple_gate, m_w_ple, m_ple_norm_post, v_ffn1_norm_pre, v_ffn1_w_gu, v_ffn1_w_down, v_ffn1_norm_post, v_mix_norm_pre, v_w_in, v_b_forget, v_b_gate, v_conv_short, v_conv_dw, v_conv_dw_bias, v_conv_ln_gain, v_conv_ln_bias, v_attn_sinks, v_rel_bias, v_w_br_a, v_w_br_b, v_w_br_c, v_w_br_d, v_w_o, v_mix_norm_post, v_ffn2_norm_pre, v_ffn2_w_gu, v_ffn2_w_down, v_ffn2_norm_post, v_ple_norm_gate, v_w_ple_gate, v_w_ple, v_ple_norm_post):
    args = dict(locals())
    ws = {n: args[n] for n in WEIGHTS}
    ms = {n: args["m_" + n] for n in WEIGHTS}
    vs = {n: args["v_" + n] for n in WEIGHTS}
    return _step(x, p, loss_target, ws, ms, vs)


def _local(x, p, loss_target, ws, wf, w_short, w_dw):
    bn, s, d = x.shape
    t = bn * s
    depth = w_short.shape[0]

    def vec(a, i):
        return a[i].reshape(1, -1)

    bucket = _bucket_table()
    band_bias = band_bias_fwd(bucket, ws['rel_bias'], "band_bias")

    h = x.reshape(t, d)
    saved = []
    for i in range(depth):
        sv = {}
        h, sv['ffn1'] = _ffn_fwd(h, vec(ws['ffn1_norm_pre'], i), wf['ffn1_w_gu'][i], wf['ffn1_w_down'][i],
                                 vec(ws['ffn1_norm_post'], i), f"l{i}_ffn1")
        h1 = h
        u = rms_fwd(h1, vec(ws['mix_norm_pre'], i), f"l{i}_mix_rms")
        w_in_r = _reorder_w_in(wf['w_in'][i])
        proj = _mm(u, w_in_r, out_dtype=BF16, name=f"l{i}_mm_proj")
        bf = jnp.pad(vec(ws['b_forget'], i), ((0, 0), (0, 128 - A_HEADS)))
        cc = fgate_fwd(proj, bf, bn, s, f"l{i}_fgate")
        c4 = jnp.transpose(cc.reshape(bn, s, 128)[:, :, :A_HEADS], (0, 2, 1))
        c_col = c4[..., None]
        c_row = c4.reshape(bn, A_HEADS, s // FOX_T, 1, FOX_T)
        ya, oa, lse_a = fox2_fwd(proj, c_col, c_row, bn, s, f"l{i}_fox")
        w_sh = jnp.pad(w_short[i], ((0, 8 - w_short.shape[1]), (0, 0)))
        w_cv = jnp.pad(w_dw[i], ((0, 32 - w_dw.shape[1]), (0, 0)))
        yb = convb_fwd(proj, w_sh[:3], bn, s, f"l{i}_convb")
        cvec = (vec(ws['conv_dw_bias'], i), vec(ws['conv_ln_gain'], i), vec(ws['conv_ln_bias'], i))
        yc = convc_fwd(proj, w_cv[:31], *cvec, bn, s, f"l{i}_convc")
        sinks = jnp.broadcast_to(ws['attn_sinks'][i].reshape(D_Q_HEADS, 1, 1), (D_Q_HEADS, QB, 1))
        yd, lse_d = swa3_fwd(proj, band_bias, sinks, bn, s, f"l{i}_swa")
        ys = (ya, yb, yc, yd)
        wbr = (wf['w_br_a'][i], wf['w_br_b'][i], wf['w_br_c'][i], wf['w_br_d'][i])
        zs = [_mm(yk, wk, out_dtype=BF16, name=f"l{i}_mm_br{k}") for k, (yk, wk) in enumerate(zip(ys, wbr))]
        bgs = [ws['b_gate'][i, k * d:(k + 1) * d].reshape(1, d) for k in range(4)]
        merged = merge_fwd(proj, zs, bgs, f"l{i}_merge")
        mo = _mm(merged, wf['w_o'][i], out_dtype=F32, name=f"l{i}_mm_o")
        h2 = res_rms_fwd(h1, mo, vec(ws['mix_norm_post'], i), 1.0, f"l{i}_mix_res")
        sv['mix'] = dict(h1=h1, u=u, proj=proj, w_in_r=w_in_r, bf=bf, c_col=c_col, c_row=c_row, oa=oa, lse_a=lse_a,
                         w_sh=w_sh, w_cv=w_cv, cvec=cvec, sinks=sinks, lse_d=lse_d, ys=ys, wbr=wbr, zs=zs, bgs=bgs,
                         merged=merged, mo=mo)
        h, sv['ffn2'] = _ffn_fwd(h2, vec(ws['ffn2_norm_pre'], i), wf['ffn2_w_gu'][i], wf['ffn2_w_down'][i],
                                 vec(ws['ffn2_norm_post'], i), f"l{i}_ffn2")
        h3 = h
        ng = rms_fwd(h3, vec(ws['ple_norm_gate'], i), f"l{i}_ple_rms")
        pgl = _mm(ng, wf['w_ple_gate'][i], out_dtype=BF16, name=f"l{i}_mm_pgl")
        p_i = p[i].reshape(t, -1)
        pr = _mm(p_i, wf['w_ple'][i], out_dtype=F32, name=f"l{i}_mm_pr")
        h = ple_fwd(h3, pgl, pr, vec(ws['ple_norm_post'], i), f"l{i}_ple")
        sv['ple'] = dict(h3=h3, ng=ng, pgl=pgl, p_i=p_i, pr=pr)
        saved.append(sv)

    dh, loss_vec = loss_fwd_bwd(h, loss_target.reshape(t, d), "loss")
    loss_part = jnp.sum(loss_vec)

    gst = {}
    gwin = [None] * depth

    def dw(n, a, b, nm):
        gst[n] = _mm(a, b, ta=True, name=nm, stack=(gst.get(n), depth, i))

    gsmall = {n: [None] * depth for n in REPLICATED + CONV_SHARDED if n != 'rel_bias'}
    dbias_layers = []
    for i in reversed(range(depth)):
        sv = saved[i]
        pv = sv['ple']
        dpgl, dpr, dg = ple_bwd(pv['pgl'], pv['pr'], vec(ws['ple_norm_post'], i), dh, f"l{i}_ple_bwd")
        gsmall['ple_norm_post'][i] = dg
        dw('w_ple', pv['p_i'], dpr, f"l{i}_mm_dwple")
        dw('w_ple_gate', pv['ng'], dpgl, f"l{i}_mm_dwpg")
        dng = _mm(dpgl, wf['w_ple_gate'][i], tb=True, out_dtype=BF16, name=f"l{i}_mm_dng")
        dh, gsmall['ple_norm_gate'][i] = rms_bwd(pv['h3'], vec(ws['ple_norm_gate'], i), dng, dh,
                                                  f"l{i}_ple_rms_bwd")
        dh, gsmall['ffn2_norm_pre'][i], gsmall['ffn2_norm_post'][i] = _ffn_bwd(
            dh, sv['ffn2'], vec(ws['ffn2_norm_pre'], i), wf['ffn2_w_gu'][i], wf['ffn2_w_down'][i],
            vec(ws['ffn2_norm_post'], i), f"l{i}_ffn2", dw, 'ffn2_w_gu', 'ffn2_w_down')
        mv = sv['mix']
        dmo, gsmall['mix_norm_post'][i] = res_rms_bwd(mv['mo'], vec(ws['mix_norm_post'], i), dh, 1.0,
                                                      f"l{i}_mix_res_bwd")
        dw('w_o', mv['merged'], dmo, f"l{i}_mm_dwo")
        dmerged = _mm(dmo, wf['w_o'][i], tb=True, out_dtype=BF16, name=f"l{i}_mm_dmerged")
        mb = merge_bwd(mv['proj'], mv['zs'], mv['bgs'], dmerged, f"l{i}_merge_bwd")
        dgates, dzs, dbg = mb[0:4], mb[4:8], mb[8:12]
        gsmall['b_gate'][i] = jnp.concatenate(dbg, axis=1)
        dys = []
        for k, nm in enumerate(('w_br_a', 'w_br_b', 'w_br_c', 'w_br_d')):
            dw(nm, mv['ys'][k], dzs[k], f"l{i}_mm_dwbr{k}")
            dys.append(_mm(dzs[k], mv['wbr'][k], tb=True, out_dtype=BF16, name=f"l{i}_mm_dy{k}"))
        dqa, dka, dva, dck, dcq = fox2_bwd(mv['proj'], mv['c_col'], mv['c_row'], mv['oa'], mv['lse_a'], dys[0], bn, s,
                                           f"l{i}_fox_bwd")
        dc = jnp.transpose(dck.reshape(bn, A_HEADS, s) + dcq.reshape(bn, A_HEADS, s), (0, 2, 1))
        dc = jnp.pad(dc, ((0, 0), (0, 0), (0, 128 - A_HEADS))).reshape(t, 128)
        daf, dbf = fgate_bwd(mv['proj'], mv['bf'], dc, bn, s, f"l{i}_fgate_bwd")
        gsmall['b_forget'][i] = dbf[:, :A_HEADS]
        dbg_, dcg_, dxb_, dwsh = convb_bwd(mv['proj'], mv['w_sh'][:3], dys[1], bn, s, f"l{i}_convb_bwd")
        gsmall['conv_short'][i] = dwsh[:3]
        dca, dcb, dwcv, dcbias, dlg, dlb = convc_bwd(mv['proj'], mv['w_cv'][:31], *mv['cvec'], dys[2], bn, s,
                                                     f"l{i}_convc_bwd")
        gsmall['conv_dw'][i] = dwcv[:31]
        gsmall['conv_dw_bias'][i] = dcbias
        gsmall['conv_ln_gain'][i] = dlg
        gsmall['conv_ln_bias'][i] = dlb
        dqd, dkd, dvd, dbias, dsink = swa3_bwd(mv['proj'], band_bias, mv['sinks'], mv['ys'][3], mv['lse_d'], dys[3],
                                               bn, s, f"l{i}_swa_bwd")
        dbias_layers.append(dbias)
        gsmall['attn_sinks'][i] = jnp.sum(dsink, axis=(1, 2))
        dproj = assemble_dproj(list(dgates) + [dqa, dka, dva, dbg_, dcg_, dxb_, dca, dcb, dqd], dkd, dvd, daf,
                               f"l{i}_dproj")
        dwin = _restore_dw_in(_mm(mv['u'], dproj, ta=True, name=f"l{i}_mm_dwin"))
        gwin[i] = jnp.transpose(dwin.reshape(d, N_CHIPS, -1), (1, 0, 2))
        du = _mm(dproj, mv['w_in_r'], tb=True, out_dtype=BF16, name=f"l{i}_mm_du")
        dh, gsmall['mix_norm_pre'][i] = rms_bwd(mv['h1'], vec(ws['mix_norm_pre'], i), du, dh, f"l{i}_mix_rms_bwd")
        dh, gsmall['ffn1_norm_pre'][i], gsmall['ffn1_norm_post'][i] = _ffn_bwd(
            dh, sv['ffn1'], vec(ws['ffn1_norm_pre'], i), wf['ffn1_w_gu'][i], wf['ffn1_w_down'][i],
            vec(ws['ffn1_norm_post'], i), f"l{i}_ffn1", dw, 'ffn1_w_gu', 'ffn1_w_down')
    grad_x = dh.reshape(bn, s, d)

    drel = band_bias_bwd(bucket, dbias_layers, "band_bias_bwd")
    gst['w_in'] = jnp.stack(gwin)
    full_shapes = {n: ws[n].shape for n in REPLICATED}
    full_shapes['conv_short'] = w_short.shape
    full_shapes['conv_dw'] = w_dw.shape
    gs = {n: jnp.stack([a.reshape(full_shapes[n][1:]) for a in gsmall[n]]) for n in gsmall}
    gs['rel_bias'] = jnp.transpose(drel[:, :, 0])
    return loss_part, grad_x, gst, gs


def _step(x, p, loss_target, ws, ms, vs):
    chip = 2 * lax.axis_index("x") + lax.axis_index("y")

    wf = gather_weights({n: ws[n].astype(BF16) for n in SHARDED}, "gather_weights")
    w_in_all = wf['w_in']
    wf['w_in'] = jnp.transpose(w_in_all, (1, 2, 0, 3)).reshape(w_in_all.shape[1], w_in_all.shape[2], -1)
    conv_shapes = [ws[n].shape for n in CONV_SHARDED]
    conv_all = gather_small(_pack_small([ws[n] for n in CONV_SHARDED]), "gather_conv")
    conv_full = []
    for idx, n in enumerate(CONV_SHARDED):
        per_chip = [_unpack_small(conv_all[2 * j], conv_shapes)[idx] for j in range(N_CHIPS)]
        conv_full.append(jnp.concatenate(per_chip, axis=-1))
    w_short, w_dw = conv_full

    loss_part, grad_x, gst, gs = _local(x, p, loss_target, {n: ws[n] for n in REPLICATED}, wf, w_short, w_dw)

    c_arr = lax.axis_index("c").astype(jnp.int32).reshape(1)
    recv = rs_sibling(gst, "rs_sibling")
    chip_sum = {n: rs_add(_kind(n), gst[n], recv[n], c_arr, f"rs_add_{n}") for n in SHARDED}
    slots = rs_chips2(chip_sum, "rs_chips")
    sc_arr = jnp.stack([chip, lax.axis_index("c")]).astype(jnp.int32)
    red_half = {n: rs_sum2(_kind(n), chip_sum[n], slots[n], sc_arr, f"rs_sum_{n}") for n in SHARDED}
    g_shard = rs_join2(red_half, "rs_join")

    small_names = [n for n in REPLICATED + CONV_SHARDED]
    small_parts = [gs[n] for n in small_names]
    small_shapes = [g.shape for g in small_parts]
    small_parts.append(loss_part.reshape(1))
    small_shapes.append((1,))
    small_all = gather_small(_pack_small(small_parts), "gather_small")
    small_red = sum_slots(small_all, "small_sum")
    small_g = _unpack_small(small_red, small_shapes)
    loss = small_g[-1].reshape(())
    g_small = dict(zip(small_names, small_g[:-1]))

    grads = {}
    for n in WEIGHTS:
        if n in SHARDED:
            grads[n] = g_shard[n]
        elif n in CONV_SHARDED:
            wdt = ws[n].shape[-1]
            grads[n] = lax.dynamic_slice_in_dim(g_small[n], chip * wdt, wdt, axis=2)
        else:
            grads[n] = g_small[n]

    deltas, new_m, new_v = {}, {}, {}
    small_upd = [n for n in WEIGHTS if n not in SHARDED]
    for n in SHARDED:
        deltas[n], new_m[n], new_v[n] = adamw(ws[n], grads[n], ms[n], vs[n], f"adamw_{n}")
    shapes_u = [ws[n].shape for n in small_upd]
    packs = [_pack_small([src[n] for n in small_upd]) for src in (ws, grads, ms, vs)]
    upd = adamw(*packs, "adamw_small")
    for res, dst in zip(upd, (deltas, new_m, new_v)):
        for n, a in zip(small_upd, _unpack_small(res, shapes_u)):
            dst[n] = a

    return (loss, grad_x, *[grads[n] for n in WEIGHTS], *[deltas[n] for n in WEIGHTS],
            *[new_m[n] for n in WEIGHTS], *[new_v[n] for n in WEIGHTS])
```

```python
import functools
import math

import jax
import jax.numpy as jnp
from jax import lax
from jax.experimental import pallas as pl
from jax.experimental.pallas import tpu as pltpu

F32 = jnp.float32
BF16 = jnp.bfloat16
MESH = pl.DeviceIdType.MESH

D_MODEL = 1024
DEPTH = 4
HEAD_DIM = 64
A_HEADS = 4
D_Q_HEADS = 8
D_KV_HEADS = 2
D_GROUP = 4
WINDOW = 128
QB = 128
REL_BUCKETS = 32
REL_MAX_DIST = 128
D_FF = 2816
EPS = 1e-6
NEG = -1e30
SCALE = HEAD_DIM ** -0.5
N_CHIPS = 4
N_DEV = 8

ADAM_LR = 0.001
ADAM_B1 = 0.9
ADAM_B2 = 0.999
ADAM_EPS = 1e-08
ADAM_WD = 0.01
ADAM_STEP = 10

C_GATE = 0
C_AQ, C_AK, C_AV = 4096, 4352, 4608
C_BG, C_CG, C_XB = 4864, 5120, 5376
C_CA, C_CB = 5632, 5888
C_DQ, C_DK, C_DV = 6144, 6656, 6784
C_AF = 6912
N_PROJ = 7168

VMEM_LIMIT = 56 * 1024 * 1024
PACK_COLS = 1024
PACK_ROW_ALIGN = 1024

SHARDED = ('ffn1_w_gu', 'ffn1_w_down', 'w_in', 'w_br_a', 'w_br_b', 'w_br_c', 'w_br_d', 'w_o',
           'ffn2_w_gu', 'ffn2_w_down', 'w_ple_gate', 'w_ple')
ROW_SHARDED = ('ffn1_w_down', 'w_o', 'ffn2_w_down', 'w_ple_gate')
CONV_SHARDED = ('conv_short', 'conv_dw')
REPLICATED = ('ffn1_norm_pre', 'ffn1_norm_post', 'mix_norm_pre', 'b_forget', 'b_gate', 'conv_dw_bias',
              'conv_ln_gain', 'conv_ln_bias', 'attn_sinks', 'rel_bias', 'mix_norm_post', 'ffn2_norm_pre',
              'ffn2_norm_post', 'ple_norm_gate', 'ple_norm_post')
WEIGHTS = ('ffn1_norm_pre', 'ffn1_w_gu', 'ffn1_w_down', 'ffn1_norm_post', 'mix_norm_pre', 'w_in', 'b_forget',
           'b_gate', 'conv_short', 'conv_dw', 'conv_dw_bias', 'conv_ln_gain', 'conv_ln_bias', 'attn_sinks',
           'rel_bias', 'w_br_a', 'w_br_b', 'w_br_c', 'w_br_d', 'w_o', 'mix_norm_post', 'ffn2_norm_pre',
           'ffn2_w_gu', 'ffn2_w_down', 'ffn2_norm_post', 'ple_norm_gate', 'w_ple_gate', 'w_ple', 'ple_norm_post')


def _cparams(sem=None):
    return pltpu.CompilerParams(dimension_semantics=sem, vmem_limit_bytes=VMEM_LIMIT)


def _pick(dim, cands):
    for c in cands:
        if dim % c == 0:
            return c
    return dim


MM_VMEM_BUDGET = 40 * 1024 * 1024
MXU_FLOPS = 9.0e14
HBM_BYTES_PER_S = 3.0e12
GRID_STEP_S = 0.35e-6


def _divisors(dim, cands):
    out = [c for c in cands if c <= dim and dim % c == 0]
    return out or [dim]


def _mm_tiles(m, n, k, ab, bb, ob):
    best = None
    for tm in _divisors(m, (2048, 1408, 1024, 512, 256, 128)):
        for tn in _divisors(n, (2816, 2048, 1792, 1408, 1024, 512, 256, 128)):
            for tk in _divisors(k, (k if k <= 2048 else 2816, 2816, 2048, 1792, 1408, 1024, 512, 256, 128)):
                nk = k // tk
                vmem = 2 * (tm * tk * ab + tk * tn * bb + tm * tn * ob) + tm * tn * 4 * (2 if nk > 1 else 1)
                if vmem > MM_VMEM_BUDGET:
                    continue
                steps = (m // tm) * (n // tn) * nk
                a_bytes = m * k * ab * (1 if nk == 1 else n // tn)
                b_bytes = k * n * bb * (1 if (nk == 1 and n == tn) else m // tm)
                mem = (a_bytes + b_bytes + m * n * ob) / HBM_BYTES_PER_S
                acc = steps * tm * tn * 1.5e-12 if nk > 1 else 0.0
                cost = steps * GRID_STEP_S + max(2.0 * m * n * k / MXU_FLOPS, mem) + acc
                if best is None or cost < best[0]:
                    best = (cost, tm, tn, tk)
    assert best is not None, (m, n, k)
    return best[1:]


def _mm(a, b, *, ta=False, tb=False, out_dtype=F32, name="mm", stack=None):
    b_layer = None
    if isinstance(b, tuple):
        b, b_layer = b
    if ta:
        kdim, m = a.shape
    else:
        m, kdim = a.shape
    if tb:
        n, kb = b.shape[-2:]
    else:
        kb, n = b.shape[-2:]
    assert kb == kdim, (a.shape, b.shape, ta, tb)
    tm, tn, tk = _mm_tiles(m, n, kdim, a.dtype.itemsize, b.dtype.itemsize, jnp.dtype(out_dtype).itemsize)
    nk = kdim // tk
    dims = (((0,) if ta else (1,), (1,) if tb else (0,)), ((), ()))

    def dot(a_ref, b_ref):
        return lax.dot_general(a_ref[...].astype(BF16), b_ref[...].astype(BF16), dims, preferred_element_type=F32)

    if nk == 1:
        def body(a_ref, b_ref, *rest):
            o_ref = rest[-1]
            o_ref[...] = dot(a_ref, b_ref).astype(o_ref.dtype)
        scratch = []
    else:
        def body(a_ref, b_ref, *rest):
            o_ref, acc_ref = rest[-2], rest[-1]
            k = pl.program_id(2)

            @pl.when(k == 0)
            def _():
                acc_ref[...] = dot(a_ref, b_ref)

            @pl.when(jnp.logical_and(k > 0, k < nk - 1))
            def _():
                acc_ref[...] += dot(a_ref, b_ref)

            @pl.when(k == nk - 1)
            def _():
                o_ref[...] = (acc_ref[...] + dot(a_ref, b_ref)).astype(o_ref.dtype)
        scratch = [pltpu.VMEM((tm, tn), F32)]

    a_spec = pl.BlockSpec((tk, tm), lambda i, j, k: (k, i)) if ta else pl.BlockSpec((tm, tk), lambda i, j, k: (i, k))
    if b_layer is None:
        b_spec = (pl.BlockSpec((tn, tk), lambda i, j, k: (j, k)) if tb
                  else pl.BlockSpec((tk, tn), lambda i, j, k: (k, j)))
    else:
        b_spec = (pl.BlockSpec((None, tn, tk), lambda i, j, k: (b_layer, j, k)) if tb
                  else pl.BlockSpec((None, tk, tn), lambda i, j, k: (b_layer, k, j)))
    in_specs, operands, aliases = [a_spec, b_spec], [a, b], {}
    if stack is None:
        out_spec = pl.BlockSpec((tm, tn), lambda i, j, k: (i, j))
        out_shape = jax.ShapeDtypeStruct((m, n), out_dtype)
    else:
        buf, depth, layer = stack
        out_spec = pl.BlockSpec((None, tm, tn), lambda i, j, k: (layer, i, j))
        out_shape = jax.ShapeDtypeStruct((depth, m, n), out_dtype)
        if buf is not None:
            in_specs.append(pl.BlockSpec(memory_space=pl.ANY))
            operands.append(buf)
            aliases = {2: 0}
    return pl.pallas_call(
        body, name=name, grid=(m // tm, n // tn, nk),
        in_specs=in_specs, out_specs=out_spec, out_shape=out_shape, scratch_shapes=scratch,
        input_output_aliases=aliases,
        compiler_params=_cparams(("parallel", "parallel", "arbitrary")),
    )(*operands)


def _rowwise(fn, rows, params, outs, pouts=(), *, tm=256, name="rowwise"):
    t = rows[0][0].shape[0]
    assert t % tm == 0
    n_r, n_p, n_o, n_po = len(rows), len(params), len(outs), len(pouts)

    def body(*refs):
        r_refs = refs[:n_r]
        p_refs = refs[n_r:n_r + n_p]
        o_refs = refs[n_r + n_p:n_r + n_p + n_o]
        po_refs = refs[n_r + n_p + n_o:]
        res = fn(*[r[...] for r in r_refs], *[p[...] for p in p_refs])
        if not isinstance(res, (tuple, list)):
            res = (res,)
        assert len(res) == n_o + n_po, (len(res), n_o, n_po)
        for o, val in zip(o_refs, res[:n_o]):
            o[...] = val.astype(o.dtype)
        if n_po:
            first = pl.program_id(0) == 0

            @pl.when(first)
            def _():
                for o, val in zip(po_refs, res[n_o:]):
                    o[...] = val.astype(F32)

            @pl.when(jnp.logical_not(first))
            def _():
                for o, val in zip(po_refs, res[n_o:]):
                    o[...] += val.astype(F32)

    in_specs = [pl.BlockSpec((tm, w), functools.partial(lambda i, cb: (i, cb), cb=cb)) for (_, w, cb) in rows]
    in_specs += [pl.BlockSpec(p.shape, lambda i: (0, 0)) for p in params]
    out_specs = [pl.BlockSpec((tm, w), lambda i: (i, 0)) for (w, _) in outs]
    out_specs += [pl.BlockSpec((1, w), lambda i: (0, 0)) for w in pouts]
    out_shape = [jax.ShapeDtypeStruct((t, w), dt) for (w, dt) in outs]
    out_shape += [jax.ShapeDtypeStruct((1, w), F32) for w in pouts]
    res = pl.pallas_call(
        body, name=name, grid=(t // tm,), in_specs=in_specs, out_specs=out_specs, out_shape=out_shape,
        compiler_params=_cparams(("arbitrary",)),
    )(*[r[0] for r in rows], *params)
    return res


def _full(a):
    return (a, a.shape[1], 0)


def _rms(x, g):
    x = x.astype(F32)
    return x * lax.rsqrt(jnp.mean(x * x, axis=-1, keepdims=True) + EPS) * g


def _sum0(v):
    return jnp.sum(v, axis=0, keepdims=True)


def rms_fwd(h, g, name):
    return _rowwise(lambda x, gg: _rms(x, gg), [_full(h)], [g], [(h.shape[1], BF16)], tm=512, name=name)[0]


def rms_bwd(h, g, dn, dres, name):
    def fn(x, d, r, gg):
        _, vjp = jax.vjp(_rms, x, gg)
        dx, dg = vjp(d.astype(F32))
        return dx + r, dg
    w = h.shape[1]
    return _rowwise(fn, [_full(h), _full(dn), _full(dres)], [g], [(w, F32)], [w], tm=512, name=name)


def res_rms_fwd(h, f, g, coef, name):
    return _rowwise(lambda x, y, gg: x + coef * _rms(y, gg), [_full(h), _full(f)], [g], [(h.shape[1], F32)],
                    tm=512, name=name)[0]


def res_rms_bwd(f, g, dh, coef, name):
    def fn(y, d, gg):
        _, vjp = jax.vjp(lambda a, b: coef * _rms(a, b), y, gg)
        dy, dg = vjp(d)
        return dy, dg
    w = f.shape[1]
    return _rowwise(fn, [_full(f), _full(dh)], [g], [(w, BF16)], [w], tm=512, name=name)


def swiglu_fwd(gu, name):
    f = gu.shape[1] // 2

    def fn(gate, up):
        gate = gate.astype(F32)
        return gate * jax.nn.sigmoid(gate) * up.astype(F32)
    return _rowwise(fn, [(gu, f, 0), (gu, f, 1)], [], [(f, BF16)], name=name)[0]


def swiglu_bwd(gu, da, name):
    t, f2 = gu.shape
    f = f2 // 2
    tm = 256

    def body(gate_ref, up_ref, da_ref, o_ref):
        gate = gate_ref[...].astype(F32)
        up = up_ref[...].astype(F32)
        d = da_ref[...].astype(F32)
        sg = jax.nn.sigmoid(gate)
        silu = gate * sg
        o_ref[:, :f] = (d * up * (sg + silu * (1.0 - sg))).astype(o_ref.dtype)
        o_ref[:, f:] = (d * silu).astype(o_ref.dtype)

    return pl.pallas_call(
        body, name=name, grid=(t // tm,),
        in_specs=[pl.BlockSpec((tm, f), lambda i: (i, 0)), pl.BlockSpec((tm, f), lambda i: (i, 1)),
                  pl.BlockSpec((tm, f), lambda i: (i, 0))],
        out_specs=pl.BlockSpec((tm, f2), lambda i: (i, 0)),
        out_shape=jax.ShapeDtypeStruct((t, f2), BF16),
        compiler_params=_cparams(("parallel",)),
    )(gu, gu, da)


def _merge(g0, g1, g2, g3, z0, z1, z2, z3, b0, b1, b2, b3):
    acc = jax.nn.sigmoid(g0.astype(F32) + b0) * z0.astype(F32)
    acc += jax.nn.sigmoid(g1.astype(F32) + b1) * z1.astype(F32)
    acc += jax.nn.sigmoid(g2.astype(F32) + b2) * z2.astype(F32)
    acc += jax.nn.sigmoid(g3.astype(F32) + b3) * z3.astype(F32)
    return acc


def merge_fwd(proj, zs, bs, name):
    rows = [(proj, D_MODEL, k) for k in range(4)] + [_full(z) for z in zs]
    return _rowwise(_merge, rows, list(bs), [(D_MODEL, BF16)], name=name)[0]


def merge_bwd(proj, zs, bs, dmerged, name):
    def fn(*args):
        d = args[8].astype(F32)
        prim = args[:8] + args[9:]
        _, vjp = jax.vjp(_merge, *prim)
        return vjp(d)
    rows = [(proj, D_MODEL, k) for k in range(4)] + [_full(z) for z in zs] + [_full(dmerged)]
    outs = [(D_MODEL, BF16)] * 8
    return _rowwise(fn, rows, list(bs), outs, [D_MODEL] * 4, name=name)


def _ple(pgl, pr, g):
    return jax.nn.sigmoid(pgl.astype(F32)) * _rms(pr, g)


def ple_fwd(h, pgl, pr, g, name):
    return _rowwise(lambda x, a, b, gg: x + _ple(a, b, gg), [_full(h), _full(pgl), _full(pr)], [g],
                    [(D_MODEL, F32)], tm=512, name=name)[0]


def ple_bwd(pgl, pr, g, dh, name):
    def fn(a, b, d, gg):
        _, vjp = jax.vjp(_ple, a, b, gg)
        return vjp(d)
    return _rowwise(fn, [_full(pgl), _full(pr), _full(dh)], [g], [(D_MODEL, BF16), (D_MODEL, BF16)], [D_MODEL],
                    tm=512, name=name)


def loss_fwd_bwd(y, target, name):
    def fn(a, b):
        err = a - b
        return err * (1.0 / D_MODEL), _sum0(err * err) * (0.5 / D_MODEL)
    return _rowwise(fn, [_full(y), _full(target)], [], [(D_MODEL, F32)], [D_MODEL], tm=512, name=name)


def _shift_down(x, d, row):
    if d == 0:
        return x
    return jnp.where(row >= d, pltpu.roll(x, d, 0), 0.0)


def _shift_up(x, d, row):
    if d == 0:
        return x
    s = x.shape[0]
    return jnp.where(row < s - d, pltpu.roll(x, s - d, 0), 0.0)


def fgate_fwd(proj, bf, bn, s, name):
    def body(a_ref, b_ref, o_ref):
        x = a_ref[...].astype(F32) + b_ref[...]
        c = jnp.minimum(x, 0.0) - jnp.log(1.0 + jnp.exp(-jnp.abs(x)))
        row = lax.broadcasted_iota(jnp.int32, c.shape, 0)
        sh = 1
        while sh < s:
            c = c + _shift_down(c, sh, row)
            sh *= 2
        o_ref[...] = c

    return pl.pallas_call(
        body, name=name, grid=(bn,),
        in_specs=[pl.BlockSpec((s, 128), lambda b: (b, C_AF // 128)), pl.BlockSpec((1, 128), lambda b: (0, 0))],
        out_specs=pl.BlockSpec((s, 128), lambda b: (b, 0)),
        out_shape=jax.ShapeDtypeStruct((bn * s, 128), F32),
        compiler_params=_cparams(("parallel",)),
    )(proj, bf)


def fgate_bwd(proj, bf, dc, bn, s, name):
    def body(a_ref, b_ref, dc_ref, da_ref, db_ref):
        x = a_ref[...].astype(F32) + b_ref[...]
        d = dc_ref[...]
        row = lax.broadcasted_iota(jnp.int32, d.shape, 0)
        sh = 1
        while sh < s:
            d = d + _shift_up(d, sh, row)
            sh *= 2
        da = d * jax.nn.sigmoid(-x)
        da_ref[...] = da.astype(da_ref.dtype)
        first = pl.program_id(0) == 0

        @pl.when(first)
        def _():
            db_ref[...] = _sum0(da)

        @pl.when(jnp.logical_not(first))
        def _():
            db_ref[...] += _sum0(da)

    return pl.pallas_call(
        body, name=name, grid=(bn,),
        in_specs=[pl.BlockSpec((s, 128), lambda b: (b, C_AF // 128)), pl.BlockSpec((1, 128), lambda b: (0, 0)),
                  pl.BlockSpec((s, 128), lambda b: (b, 0))],
        out_specs=[pl.BlockSpec((s, 128), lambda b: (b, 0)), pl.BlockSpec((1, 128), lambda b: (0, 0))],
        out_shape=[jax.ShapeDtypeStruct((bn * s, 128), BF16), jax.ShapeDtypeStruct((1, 128), F32)],
        compiler_params=_cparams(("arbitrary",)),
    )(proj, bf, dc)


FOX_T = 512


def _fox_scores(q, k, cq, ck, j, i):
    t = FOX_T
    s = lax.dot_general(q, k, (((1,), (1,)), ((), ())), preferred_element_type=F32) * SCALE
    qpos = j * t + lax.broadcasted_iota(jnp.int32, (t, t), 0)
    kpos = i * t + lax.broadcasted_iota(jnp.int32, (t, t), 1)
    return jnp.where(qpos >= kpos, s + (cq - ck), NEG)


def fox_fwd(q, k, v, c_col, c_row, name):
    bn, h, s, d = q.shape
    t = FOX_T
    nq = s // t

    def body(q_ref, k_ref, v_ref, cq_ref, ck_ref, o_ref, lse_ref):
        j = pl.program_id(2)
        qv = q_ref[...]
        cq = cq_ref[...]

        def step(i, carry):
            m, l, acc = carry
            ks = pl.multiple_of(i * t, t)
            kc = k_ref[pl.ds(ks, t), :]
            vc = v_ref[pl.ds(ks, t), :]
            sc = _fox_scores(qv, kc, cq, ck_ref[i], j, i)
            m_new = jnp.maximum(m, jnp.max(sc, axis=-1, keepdims=True))
            alpha = jnp.exp(m - m_new)
            p = jnp.exp(sc - m_new)
            l = alpha * l + jnp.sum(p, axis=-1, keepdims=True)
            acc = alpha * acc + jnp.dot(p.astype(BF16), vc, preferred_element_type=F32)
            return m_new, l, acc

        init = (jnp.full((t, 1), NEG, F32), jnp.zeros((t, 1), F32), jnp.zeros((t, d), F32))
        m, l, acc = lax.fori_loop(0, j + 1, step, init)
        o_ref[...] = (acc / l).astype(o_ref.dtype)
        lse_ref[...] = m + jnp.log(l)

    blk_q = pl.BlockSpec((None, None, t, d), lambda b, hh, j: (b, hh, j, 0))
    blk_kv = pl.BlockSpec((None, None, s, d), lambda b, hh, j: (b, hh, 0, 0))
    blk_c1 = pl.BlockSpec((None, None, t, 1), lambda b, hh, j: (b, hh, j, 0))
    blk_cr = pl.BlockSpec((None, None, nq, 1, t), lambda b, hh, j: (b, hh, 0, 0, 0))
    return pl.pallas_call(
        body, name=name, grid=(bn, h, nq),
        in_specs=[blk_q, blk_kv, blk_kv, blk_c1, blk_cr],
        out_specs=[blk_q, blk_c1],
        out_shape=[jax.ShapeDtypeStruct((bn, h, s, d), F32), jax.ShapeDtypeStruct((bn, h, s, 1), F32)],
        compiler_params=_cparams(("parallel", "parallel", "arbitrary")),
    )(q, k, v, c_col, c_row)


def fox_bwd(q, k, v, c_col, c_row, o, lse, do, name):
    bn, h, s, d = q.shape
    t = FOX_T
    nq = s // t

    def body(q_ref, k_ref, v_ref, cq_ref, ck_ref, o_ref, lse_ref, do_ref, dq_ref, dk_ref, dv_ref, dck_ref,
             dcq_ref):
        j = pl.program_id(2)

        @pl.when(j == 0)
        def _():
            dk_ref[...] = jnp.zeros_like(dk_ref)
            dv_ref[...] = jnp.zeros_like(dv_ref)
            dck_ref[...] = jnp.zeros_like(dck_ref)

        qv = q_ref[...]
        cq = cq_ref[...]
        dov = do_ref[...]
        lse = lse_ref[...]
        delta = jnp.sum(dov.astype(F32) * o_ref[...].astype(F32), axis=-1, keepdims=True)

        def step(i, carry):
            dq, dcq = carry
            ks = pl.multiple_of(i * t, t)
            kc = k_ref[pl.ds(ks, t), :]
            vc = v_ref[pl.ds(ks, t), :]
            sc = _fox_scores(qv, kc, cq, ck_ref[i], j, i)
            p = jnp.exp(sc - lse)
            dp = lax.dot_general(dov, vc, (((1,), (1,)), ((), ())), preferred_element_type=F32)
            ds = p * (dp - delta)
            dsb = ds.astype(BF16)
            dq = dq + jnp.dot(dsb, kc, preferred_element_type=F32) * SCALE
            dk_ref[pl.ds(ks, t), :] += lax.dot_general(dsb, qv, (((0,), (0,)), ((), ())),
                                                       preferred_element_type=F32) * SCALE
            dv_ref[pl.ds(ks, t), :] += lax.dot_general(p.astype(BF16), dov, (((0,), (0,)), ((), ())),
                                                       preferred_element_type=F32)
            dck_ref[i] += -_sum0(ds)
            return dq, dcq + jnp.sum(ds, axis=-1, keepdims=True)

        dq, dcq = lax.fori_loop(0, j + 1, step, (jnp.zeros((t, d), F32), jnp.zeros((t, 1), F32)))
        dq_ref[...] = dq
        dcq_ref[...] = dcq

    blk_q = pl.BlockSpec((None, None, t, d), lambda b, hh, j: (b, hh, j, 0))
    blk_kv = pl.BlockSpec((None, None, s, d), lambda b, hh, j: (b, hh, 0, 0))
    blk_c1 = pl.BlockSpec((None, None, t, 1), lambda b, hh, j: (b, hh, j, 0))
    blk_cr = pl.BlockSpec((None, None, nq, 1, t), lambda b, hh, j: (b, hh, 0, 0, 0))
    return pl.pallas_call(
        body, name=name, grid=(bn, h, nq),
        in_specs=[blk_q, blk_kv, blk_kv, blk_c1, blk_cr, blk_q, blk_c1, blk_q],
        out_specs=[blk_q, blk_kv, blk_kv, blk_cr, blk_c1],
        out_shape=[jax.ShapeDtypeStruct((bn, h, s, d), F32), jax.ShapeDtypeStruct((bn, h, s, d), F32),
                   jax.ShapeDtypeStruct((bn, h, s, d), F32), jax.ShapeDtypeStruct((bn, h, nq, 1, t), F32),
                   jax.ShapeDtypeStruct((bn, h, s, 1), F32)],
        compiler_params=_cparams(("parallel", "parallel", "arbitrary")),
    )(q, k, v, c_col, c_row, o, lse, do)


def _swa_valid(n):
    qi = lax.broadcasted_iota(jnp.int32, (QB, 2 * QB), 0)
    kj = lax.broadcasted_iota(jnp.int32, (QB, 2 * QB), 1)
    dist = qi + QB - kj
    return (dist >= 0) & (dist < WINDOW) & ((kj >= QB) | (n > 0))


def _swa_band(ref, n):
    qs = pl.multiple_of(n * QB, QB)
    ps = pl.multiple_of(jnp.maximum(n - 1, 0) * QB, QB)
    return jnp.concatenate([ref[pl.ds(ps, QB), :], ref[pl.ds(qs, QB), :]], axis=0), qs, ps


def swa_fwd(q, k, v, bias, sinks, name):
    bn, hq, s, d = q.shape
    nb = s // QB

    def body(q_ref, k_ref, v_ref, b_ref, s_ref, o_ref, lse_ref):
        def step(n, _):
            kb, qs, _ps = _swa_band(k_ref, n)
            vb, _, _ = _swa_band(v_ref, n)
            valid = _swa_valid(n)
            for g in range(D_GROUP):
                qg = q_ref[g, pl.ds(qs, QB), :]
                sc = lax.dot_general(qg, kb, (((1,), (1,)), ((), ())), preferred_element_type=F32) * SCALE
                sc = jnp.where(valid, sc + b_ref[g], NEG)
                sink = s_ref[g]
                m = jnp.maximum(jnp.max(sc, axis=-1, keepdims=True), sink)
                e = jnp.exp(sc - m)
                z = jnp.sum(e, axis=-1, keepdims=True) + jnp.exp(sink - m)
                p = e / z
                o_ref[g, pl.ds(qs, QB), :] = jnp.dot(p.astype(BF16), vb, preferred_element_type=F32
                                                     ).astype(o_ref.dtype)
                lse_ref[g, pl.ds(qs, QB), :] = m + jnp.log(z)
            return 0

        lax.fori_loop(0, nb, step, 0, unroll=2)

    blk_q = pl.BlockSpec((None, D_GROUP, s, d), lambda b, kh: (b, kh, 0, 0))
    blk_kv = pl.BlockSpec((None, None, s, d), lambda b, kh: (b, kh, 0, 0))
    blk_l = pl.BlockSpec((None, D_GROUP, s, 1), lambda b, kh: (b, kh, 0, 0))
    return pl.pallas_call(
        body, name=name, grid=(bn, D_KV_HEADS),
        in_specs=[blk_q, blk_kv, blk_kv, pl.BlockSpec((D_GROUP, QB, 2 * QB), lambda b, kh: (kh, 0, 0)),
                  pl.BlockSpec((D_GROUP, QB, 1), lambda b, kh: (kh, 0, 0))],
        out_specs=[blk_q, blk_l],
        out_shape=[jax.ShapeDtypeStruct((bn, hq, s, d), BF16), jax.ShapeDtypeStruct((bn, hq, s, 1), F32)],
        compiler_params=_cparams(("parallel", "parallel")),
    )(q, k, v, bias, sinks)


def swa_bwd(q, k, v, bias, sinks, o, lse, do, name):
    bn, hq, s, d = q.shape
    nb = s // QB

    def body(q_ref, k_ref, v_ref, b_ref, s_ref, o_ref, lse_ref, do_ref, dq_ref, dk_ref, dv_ref, db_ref, dsk_ref):
        @pl.when(pl.program_id(1) == 0)
        def _():
            db_ref[...] = jnp.zeros_like(db_ref)
            dsk_ref[...] = jnp.zeros_like(dsk_ref)

        dk_ref[...] = jnp.zeros_like(dk_ref)
        dv_ref[...] = jnp.zeros_like(dv_ref)

        def step(n, _):
            kb, qs, ps = _swa_band(k_ref, n)
            vb, _, _ = _swa_band(v_ref, n)
            valid = _swa_valid(n)
            dkb = jnp.zeros((2 * QB, d), F32)
            dvb = jnp.zeros((2 * QB, d), F32)
            for g in range(D_GROUP):
                qg = q_ref[g, pl.ds(qs, QB), :]
                dog = do_ref[g, pl.ds(qs, QB), :]
                og = o_ref[g, pl.ds(qs, QB), :]
                lse = lse_ref[g, pl.ds(qs, QB), :]
                sc = lax.dot_general(qg, kb, (((1,), (1,)), ((), ())), preferred_element_type=F32) * SCALE
                sc = jnp.where(valid, sc + b_ref[g], NEG)
                p = jnp.exp(sc - lse)
                delta = jnp.sum(dog.astype(F32) * og.astype(F32), axis=-1, keepdims=True)
                dp = lax.dot_general(dog, vb, (((1,), (1,)), ((), ())), preferred_element_type=F32)
                ds = p * (dp - delta)
                dsb = ds.astype(BF16)
                dq_ref[g, pl.ds(qs, QB), :] = jnp.dot(dsb, kb, preferred_element_type=F32) * SCALE
                dkb = dkb + lax.dot_general(dsb, qg, (((0,), (0,)), ((), ())), preferred_element_type=F32) * SCALE
                dvb = dvb + lax.dot_general(p.astype(BF16), dog, (((0,), (0,)), ((), ())),
                                            preferred_element_type=F32)
                db_ref[g] += ds
                dsk_ref[g] += -jnp.exp(s_ref[g] - lse) * delta
            dk_ref[pl.ds(ps, QB), :] += dkb[:QB]
            dk_ref[pl.ds(qs, QB), :] += dkb[QB:]
            dv_ref[pl.ds(ps, QB), :] += dvb[:QB]
            dv_ref[pl.ds(qs, QB), :] += dvb[QB:]
            return 0

        lax.fori_loop(0, nb, step, 0, unroll=2)

    blk_q = pl.BlockSpec((None, D_GROUP, s, d), lambda kh, b: (b, kh, 0, 0))
    blk_kv = pl.BlockSpec((None, None, s, d), lambda kh, b: (b, kh, 0, 0))
    blk_l = pl.BlockSpec((None, D_GROUP, s, 1), lambda kh, b: (b, kh, 0, 0))
    blk_b = pl.BlockSpec((D_GROUP, QB, 2 * QB), lambda kh, b: (kh, 0, 0))
    blk_s = pl.BlockSpec((D_GROUP, QB, 1), lambda kh, b: (kh, 0, 0))
    return pl.pallas_call(
        body, name=name, grid=(D_KV_HEADS, bn),
        in_specs=[blk_q, blk_kv, blk_kv, blk_b, blk_s, blk_q, blk_l, blk_q],
        out_specs=[blk_q, blk_kv, blk_kv, blk_b, blk_s],
        out_shape=[jax.ShapeDtypeStruct((bn, hq, s, d), F32), jax.ShapeDtypeStruct((bn, D_KV_HEADS, s, d), F32),
                   jax.ShapeDtypeStruct((bn, D_KV_HEADS, s, d), F32),
                   jax.ShapeDtypeStruct((hq, QB, 2 * QB), F32), jax.ShapeDtypeStruct((hq, QB, 1), F32)],
        compiler_params=_cparams(("parallel", "arbitrary")),
    )(q, k, v, bias, sinks, o, lse, do)


def _sel(nh, width):
    r = lax.broadcasted_iota(jnp.int32, (width, HEAD_DIM), 0)
    c = lax.broadcasted_iota(jnp.int32, (width, HEAD_DIM), 1)
    return [(r == c + HEAD_DIM * h).astype(BF16) for h in range(nh)]


def _pick_head(x, e):
    return jnp.dot(x, e, preferred_element_type=F32).astype(BF16)


def _place_head(x, e):
    return lax.dot_general(x.astype(BF16), e, (((1,), (1,)), ((), ())), preferred_element_type=F32)


def fox2_fwd(proj, c_col, c_row, bn, s, name):
    t = FOX_T
    nq = s // t
    nh, d = A_HEADS, HEAD_DIM

    def body(q_ref, k_ref, v_ref, cq_ref, ck_ref, y_ref, o_ref, lse_ref, kh_ref, vh_ref):
        j = pl.program_id(1)
        es = _sel(nh, 256)

        @pl.when(j == 0)
        def _():
            for h in range(nh):
                kh_ref[h] = _pick_head(k_ref[...], es[h])
                vh_ref[h] = _pick_head(v_ref[...], es[h])

        q4 = q_ref[...]
        qs = [_pick_scaled(q4, es[h]) for h in range(nh)]
        cqs = [cq_ref[h] for h in range(nh)]

        def chunk(i, carry, masked):
            ks = pl.multiple_of(i * t, t)
            out = []
            for h in range(nh):
                m, l, acc = carry[h]
                sc = _fox_scores2(qs[h], kh_ref[h, pl.ds(ks, t), :], cqs[h], ck_ref[h, i], masked)
                m_new = jnp.maximum(m, jnp.max(sc, axis=-1, keepdims=True))
                alpha = jnp.exp(m - m_new)
                p = jnp.exp(sc - m_new)
                l = alpha * l + jnp.sum(p, axis=-1, keepdims=True)
                acc = alpha * acc + jnp.dot(p.astype(BF16), vh_ref[h, pl.ds(ks, t), :], preferred_element_type=F32)
                out.append((m_new, l, acc))
            return tuple(out)

        init = tuple((jnp.full((t, 1), NEG, F32), jnp.zeros((t, 1), F32), jnp.zeros((t, d), F32)) for _ in range(nh))
        res = lax.fori_loop(0, j, lambda i, carry: chunk(i, carry, False), init)
        res = chunk(j, res, True)
        y = jnp.zeros((t, 256), F32)
        for h in range(nh):
            m, l, acc = res[h]
            o = acc / l
            o_ref[h] = o
            lse_ref[h] = m + jnp.log(l)
            y = y + _place_head(o, es[h])
        y_ref[...] = y.astype(y_ref.dtype)

    blk_q = pl.BlockSpec((t, 256), lambda b, j: (b * nq + j, C_AQ // 256))
    blk_k = pl.BlockSpec((s, 256), lambda b, j: (b, C_AK // 256))
    blk_v = pl.BlockSpec((s, 256), lambda b, j: (b, C_AV // 256))
    blk_c1 = pl.BlockSpec((None, nh, t, 1), lambda b, j: (b, 0, j, 0))
    blk_cr = pl.BlockSpec((None, nh, nq, 1, t), lambda b, j: (b, 0, 0, 0, 0))
    blk_o = pl.BlockSpec((None, nh, t, d), lambda b, j: (b, 0, j, 0))
    return pl.pallas_call(
        body, name=name, grid=(bn, nq),
        in_specs=[blk_q, blk_k, blk_v, blk_c1, blk_cr],
        out_specs=[pl.BlockSpec((t, 256), lambda b, j: (b * nq + j, 0)), blk_o, blk_c1],
        out_shape=[jax.ShapeDtypeStruct((bn * s, 256), BF16), jax.ShapeDtypeStruct((bn, nh, s, d), F32),
                   jax.ShapeDtypeStruct((bn, nh, s, 1), F32)],
        scratch_shapes=[pltpu.VMEM((nh, s, d), BF16), pltpu.VMEM((nh, s, d), BF16)],
        compiler_params=_cparams(("arbitrary", "arbitrary")),
    )(proj, proj, proj, c_col, c_row)


def fox2_bwd(proj, c_col, c_row, o, lse, dya, bn, s, name):
    t = FOX_T
    nq = s // t
    nh, d = A_HEADS, HEAD_DIM

    def body(q_ref, k_ref, v_ref, cq_ref, ck_ref, o_ref, lse_ref, dy_ref, dq_ref, dk_ref, dv_ref, dck_ref, dcq_ref,
             kh_ref, vh_ref, dkh_ref, dvh_ref):
        j = pl.program_id(1)
        es = _sel(nh, 256)

        @pl.when(j == 0)
        def _():
            for h in range(nh):
                kh_ref[h] = _pick_head(k_ref[...], es[h])
                vh_ref[h] = _pick_head(v_ref[...], es[h])
            dkh_ref[...] = jnp.zeros_like(dkh_ref)
            dvh_ref[...] = jnp.zeros_like(dvh_ref)
            dck_ref[...] = jnp.zeros_like(dck_ref)

        q4 = q_ref[...]
        dy4 = dy_ref[...]
        qs = [_pick_scaled(q4, es[h]) for h in range(nh)]
        dos = [_pick_head(dy4, es[h]) for h in range(nh)]
        cqs = [cq_ref[h] for h in range(nh)]
        lses = [lse_ref[h] for h in range(nh)]
        deltas = [jnp.sum(dos[h].astype(F32) * o_ref[h], axis=-1, keepdims=True) for h in range(nh)]

        def chunk(i, carry, masked):
            ks = pl.multiple_of(i * t, t)
            out = []
            for h in range(nh):
                dq, dcq = carry[h]
                kc = kh_ref[h, pl.ds(ks, t), :]
                sc = _fox_scores2(qs[h], kc, cqs[h], ck_ref[h, i], masked)
                p = jnp.exp(sc - lses[h])
                dp = lax.dot_general(dos[h], vh_ref[h, pl.ds(ks, t), :], (((1,), (1,)), ((), ())),
                                     preferred_element_type=F32)
                ds = p * (dp - deltas[h])
                dsb = ds.astype(BF16)
                dq = dq + jnp.dot(dsb, kc, preferred_element_type=F32)
                dkh_ref[h, pl.ds(ks, t), :] += lax.dot_general(dsb, qs[h], (((0,), (0,)), ((), ())),
                                                               preferred_element_type=F32)
                dvh_ref[h, pl.ds(ks, t), :] += lax.dot_general(p.astype(BF16), dos[h], (((0,), (0,)), ((), ())),
                                                               preferred_element_type=F32)
                dck_ref[h, i] += -_sum0(ds)
                out.append((dq, dcq + jnp.sum(ds, axis=-1, keepdims=True)))
            return tuple(out)

        init = tuple((jnp.zeros((t, d), F32), jnp.zeros((t, 1), F32)) for _ in range(nh))
        res = lax.fori_loop(0, j, lambda i, carry: chunk(i, carry, False), init)
        res = chunk(j, res, True)
        dq4 = jnp.zeros((t, 256), F32)
        for h in range(nh):
            dq4 = dq4 + _place_head(res[h][0] * SCALE, es[h])
            dcq_ref[h] = res[h][1]
        dq_ref[...] = dq4.astype(dq_ref.dtype)

        @pl.when(j == nq - 1)
        def _():
            dk4 = jnp.zeros((s, 256), F32)
            dv4 = jnp.zeros((s, 256), F32)
            for h in range(nh):
                dk4 = dk4 + _place_head(dkh_ref[h], es[h])
                dv4 = dv4 + _place_head(dvh_ref[h], es[h])
            dk_ref[...] = dk4.astype(dk_ref.dtype)
            dv_ref[...] = dv4.astype(dv_ref.dtype)

    blk_q = pl.BlockSpec((t, 256), lambda b, j: (b * nq + j, C_AQ // 256))
    blk_k = pl.BlockSpec((s, 256), lambda b, j: (b, C_AK // 256))
    blk_v = pl.BlockSpec((s, 256), lambda b, j: (b, C_AV // 256))
    blk_c1 = pl.BlockSpec((None, nh, t, 1), lambda b, j: (b, 0, j, 0))
    blk_cr = pl.BlockSpec((None, nh, nq, 1, t), lambda b, j: (b, 0, 0, 0, 0))
    blk_o = pl.BlockSpec((None, nh, t, d), lambda b, j: (b, 0, j, 0))
    blk_t = pl.BlockSpec((t, 256), lambda b, j: (b * nq + j, 0))
    blk_s = pl.BlockSpec((s, 256), lambda b, j: (b, 0))
    return pl.pallas_call(
        body, name=name, grid=(bn, nq),
        in_specs=[blk_q, blk_k, blk_v, blk_c1, blk_cr, blk_o, blk_c1, blk_t],
        out_specs=[blk_t, blk_s, blk_s, blk_cr, blk_c1],
        out_shape=[jax.ShapeDtypeStruct((bn * s, 256), BF16)] * 3
        + [jax.ShapeDtypeStruct((bn, nh, nq, 1, t), F32), jax.ShapeDtypeStruct((bn, nh, s, 1), F32)],
        scratch_shapes=[pltpu.VMEM((nh, s, d), BF16), pltpu.VMEM((nh, s, d), BF16),
                        pltpu.VMEM((nh, s, d), F32), pltpu.VMEM((nh, s, d), F32)],
        compiler_params=_cparams(("arbitrary", "arbitrary")),
    )(proj, proj, proj, c_col, c_row, o, lse, dya)


def _band3(ref, h, n):
    qs = pl.multiple_of(n * QB, QB)
    ps = pl.multiple_of(jnp.maximum(n - 1, 0) * QB, QB)
    return jnp.concatenate([ref[h, pl.ds(ps, QB), :], ref[h, pl.ds(qs, QB), :]], axis=0), qs, ps


def swa2_fwd(proj, bias, sinks, bn, s, name):
    nb = s // QB
    d = HEAD_DIM

    def body(q0_ref, q1_ref, k_ref, v_ref, b_ref, s_ref, y_ref, lse_ref, qh_ref, kh_ref, vh_ref):
        e4 = _sel(D_GROUP, 256)
        e2 = _sel(D_KV_HEADS, 128)
        for kh, q_ref in enumerate((q0_ref, q1_ref)):
            kh_ref[kh] = _pick_head(k_ref[...], e2[kh])
            vh_ref[kh] = _pick_head(v_ref[...], e2[kh])
            for g in range(D_GROUP):
                qh_ref[D_GROUP * kh + g] = _pick_head(q_ref[...], e4[g])

        def step(n, _):
            valid = _swa_valid(n)
            for kh in range(D_KV_HEADS):
                kb, qs, _ps = _band3(kh_ref, kh, n)
                vb, _, _ = _band3(vh_ref, kh, n)
                y = jnp.zeros((QB, 256), F32)
                for g in range(D_GROUP):
                    hh = D_GROUP * kh + g
                    qg = qh_ref[hh, pl.ds(qs, QB), :]
                    sc = lax.dot_general(qg, kb, (((1,), (1,)), ((), ())), preferred_element_type=F32) * SCALE
                    sc = jnp.where(valid, sc + b_ref[hh], NEG)
                    sink = s_ref[hh]
                    m = jnp.maximum(jnp.max(sc, axis=-1, keepdims=True), sink)
                    e = jnp.exp(sc - m)
                    z = jnp.sum(e, axis=-1, keepdims=True) + jnp.exp(sink - m)
                    o = jnp.dot((e / z).astype(BF16), vb, preferred_element_type=F32)
                    lse_ref[hh, pl.ds(qs, QB), :] = m + jnp.log(z)
                    y = y + _place_head(o, e4[g])
                y_ref[pl.ds(qs, QB), 256 * kh:256 * (kh + 1)] = y.astype(y_ref.dtype)
            return 0

        lax.fori_loop(0, nb, step, 0, unroll=2)

    return pl.pallas_call(
        body, name=name, grid=(bn,),
        in_specs=[pl.BlockSpec((s, 256), lambda b: (b, C_DQ // 256)), pl.BlockSpec((s, 256), lambda b: (b, C_DQ // 256 + 1)),
                  pl.BlockSpec((s, 128), lambda b: (b, C_DK // 128)), pl.BlockSpec((s, 128), lambda b: (b, C_DV // 128)),
                  pl.BlockSpec((D_Q_HEADS, QB, 2 * QB), lambda b: (0, 0, 0)),
                  pl.BlockSpec((D_Q_HEADS, QB, 1), lambda b: (0, 0, 0))],
        out_specs=[pl.BlockSpec((s, 512), lambda b: (b, 0)), pl.BlockSpec((None, D_Q_HEADS, s, 1), lambda b: (b, 0, 0, 0))],
        out_shape=[jax.ShapeDtypeStruct((bn * s, 512), BF16), jax.ShapeDtypeStruct((bn, D_Q_HEADS, s, 1), F32)],
        scratch_shapes=[pltpu.VMEM((D_Q_HEADS, s, d), BF16), pltpu.VMEM((D_KV_HEADS, s, d), BF16),
                        pltpu.VMEM((D_KV_HEADS, s, d), BF16)],
        compiler_params=_cparams(("parallel",)),
    )(proj, proj, proj, proj, bias, sinks)


def swa2_bwd(proj, bias, sinks, yd, lse, dyd, bn, s, name):
    nb = s // QB
    d = HEAD_DIM

    def body(q0_ref, q1_ref, k_ref, v_ref, b_ref, s_ref, y_ref, lse_ref, dy_ref, dq_ref, dk_ref, dv_ref, db_ref,
             dsk_ref, qh_ref, kh_ref, vh_ref, oh_ref, doh_ref, dkh_ref, dvh_ref):
        @pl.when(pl.program_id(0) == 0)
        def _():
            db_ref[...] = jnp.zeros_like(db_ref)
            dsk_ref[...] = jnp.zeros_like(dsk_ref)

        e4 = _sel(D_GROUP, 256)
        e2 = _sel(D_KV_HEADS, 128)
        for kh, q_ref in enumerate((q0_ref, q1_ref)):
            kh_ref[kh] = _pick_head(k_ref[...], e2[kh])
            vh_ref[kh] = _pick_head(v_ref[...], e2[kh])
            for g in range(D_GROUP):
                hh = D_GROUP * kh + g
                qh_ref[hh] = _pick_head(q_ref[...], e4[g])
                oh_ref[hh] = _pick_head(y_ref[:, 256 * kh:256 * (kh + 1)], e4[g])
                doh_ref[hh] = _pick_head(dy_ref[:, 256 * kh:256 * (kh + 1)], e4[g])
        dkh_ref[...] = jnp.zeros_like(dkh_ref)
        dvh_ref[...] = jnp.zeros_like(dvh_ref)

        def step(n, _):
            valid = _swa_valid(n)
            for kh in range(D_KV_HEADS):
                kb, qs, ps = _band3(kh_ref, kh, n)
                vb, _, _ = _band3(vh_ref, kh, n)
                dkb = jnp.zeros((2 * QB, d), F32)
                dvb = jnp.zeros((2 * QB, d), F32)
                dq4 = jnp.zeros((QB, 256), F32)
                for g in range(D_GROUP):
                    hh = D_GROUP * kh + g
                    qg = qh_ref[hh, pl.ds(qs, QB), :]
                    dog = doh_ref[hh, pl.ds(qs, QB), :]
                    og = oh_ref[hh, pl.ds(qs, QB), :]
                    lse = lse_ref[hh, pl.ds(qs, QB), :]
                    sc = lax.dot_general(qg, kb, (((1,), (1,)), ((), ())), preferred_element_type=F32) * SCALE
                    sc = jnp.where(valid, sc + b_ref[hh], NEG)
                    p = jnp.exp(sc - lse)
                    delta = jnp.sum(dog.astype(F32) * og.astype(F32), axis=-1, keepdims=True)
                    dp = lax.dot_general(dog, vb, (((1,), (1,)), ((), ())), preferred_element_type=F32)
                    ds = p * (dp - delta)
                    dsb = ds.astype(BF16)
                    dq4 = dq4 + _place_head(jnp.dot(dsb, kb, preferred_element_type=F32) * SCALE, e4[g])
                    dkb = dkb + lax.dot_general(dsb, qg, (((0,), (0,)), ((), ())),
                                                preferred_element_type=F32) * SCALE
                    dvb = dvb + lax.dot_general(p.astype(BF16), dog, (((0,), (0,)), ((), ())),
                                                preferred_element_type=F32)
                    db_ref[hh] += ds
                    dsk_ref[hh] += -jnp.exp(s_ref[hh] - lse) * delta
                dq_ref[pl.ds(qs, QB), 256 * kh:256 * (kh + 1)] = dq4.astype(dq_ref.dtype)
                dkh_ref[kh, pl.ds(ps, QB), :] += dkb[:QB]
                dkh_ref[kh, pl.ds(qs, QB), :] += dkb[QB:]
                dvh_ref[kh, pl.ds(ps, QB), :] += dvb[:QB]
                dvh_ref[kh, pl.ds(qs, QB), :] += dvb[QB:]
            return 0

        lax.fori_loop(0, nb, step, 0, unroll=2)
        dk2 = jnp.zeros((s, 128), F32)
        dv2 = jnp.zeros((s, 128), F32)
        for kh in range(D_KV_HEADS):
            dk2 = dk2 + _place_head(dkh_ref[kh], e2[kh])
            dv2 = dv2 + _place_head(dvh_ref[kh], e2[kh])
        dk_ref[...] = dk2.astype(dk_ref.dtype)
        dv_ref[...] = dv2.astype(dv_ref.dtype)

    blk512 = pl.BlockSpec((s, 512), lambda b: (b, 0))
    blk128 = pl.BlockSpec((s, 128), lambda b: (b, 0))
    blk_b = pl.BlockSpec((D_Q_HEADS, QB, 2 * QB), lambda b: (0, 0, 0))
    blk_s = pl.BlockSpec((D_Q_HEADS, QB, 1), lambda b: (0, 0, 0))
    return pl.pallas_call(
        body, name=name, grid=(bn,),
        in_specs=[pl.BlockSpec((s, 256), lambda b: (b, C_DQ // 256)), pl.BlockSpec((s, 256), lambda b: (b, C_DQ // 256 + 1)),
                  pl.BlockSpec((s, 128), lambda b: (b, C_DK // 128)), pl.BlockSpec((s, 128), lambda b: (b, C_DV // 128)),
                  blk_b, blk_s, blk512, pl.BlockSpec((None, D_Q_HEADS, s, 1), lambda b: (b, 0, 0, 0)), blk512],
        out_specs=[blk512, blk128, blk128, blk_b, blk_s],
        out_shape=[jax.ShapeDtypeStruct((bn * s, 512), BF16), jax.ShapeDtypeStruct((bn * s, 128), BF16),
                   jax.ShapeDtypeStruct((bn * s, 128), BF16),
                   jax.ShapeDtypeStruct((D_Q_HEADS, QB, 2 * QB), F32), jax.ShapeDtypeStruct((D_Q_HEADS, QB, 1), F32)],
        scratch_shapes=[pltpu.VMEM((D_Q_HEADS, s, d), BF16), pltpu.VMEM((D_KV_HEADS, s, d), BF16),
                        pltpu.VMEM((D_KV_HEADS, s, d), BF16), pltpu.VMEM((D_Q_HEADS, s, d), BF16),
                        pltpu.VMEM((D_Q_HEADS, s, d), BF16), pltpu.VMEM((D_KV_HEADS, s, d), F32),
                        pltpu.VMEM((D_KV_HEADS, s, d), F32)],
        compiler_params=_cparams(("arbitrary",)),
    )(proj, proj, proj, proj, bias, sinks, yd, lse, dyd)


def _pick_scaled(x, e):
    return (jnp.dot(x, e, preferred_element_type=F32) * SCALE).astype(BF16)


def _swa_valid4(n):
    qi = lax.broadcasted_iota(jnp.int32, (D_GROUP * QB, 2 * QB), 0) & (QB - 1)
    kj = lax.broadcasted_iota(jnp.int32, (D_GROUP * QB, 2 * QB), 1)
    dist = qi + QB - kj
    return (dist >= 0) & (dist < WINDOW) & ((kj >= QB) | (n > 0))


def _fox_scores2(q, k, cq, ck, masked):
    t = FOX_T
    s = lax.dot_general(q, k, (((1,), (1,)), ((), ())), preferred_element_type=F32) + (cq - ck)
    if masked:
        keep = lax.broadcasted_iota(jnp.int32, (t, t), 0) >= lax.broadcasted_iota(jnp.int32, (t, t), 1)
        s = jnp.where(keep, s, NEG)
    return s


def _sel_at(off, width):
    r = lax.broadcasted_iota(jnp.int32, (width, HEAD_DIM), 0)
    c = lax.broadcasted_iota(jnp.int32, (width, HEAD_DIM), 1)
    return (r == c + off).astype(BF16)


def _eye(n):
    return lax.broadcasted_iota(jnp.int32, (n, n), 0) == lax.broadcasted_iota(jnp.int32, (n, n), 1)


def _row_to_col(row, eye):
    return jnp.sum(jnp.where(eye, row, 0.0), axis=1, keepdims=True)


def _col_to_row(col, eye):
    return jnp.sum(jnp.where(eye, col, 0.0), axis=0, keepdims=True)


def swa3_fwd(proj, bias, sinks, bn, s, name):
    nb = s // QB
    d = HEAD_DIM

    def body(q_ref, k_ref, v_ref, b_ref, s_ref, y_ref, lse_ref, qh_ref, kh_ref, vh_ref):
        kh = pl.program_id(0)
        e4 = _sel(D_GROUP, 256)
        ek = _sel_at(HEAD_DIM * kh, 128)
        eye = _eye(QB)
        kh_ref[...] = _pick_head(k_ref[...], ek)
        vh_ref[...] = _pick_head(v_ref[...], ek)
        for g in range(D_GROUP):
            qh_ref[g] = _pick_scaled(q_ref[...], e4[g])
        bias4 = b_ref[...].reshape(D_GROUP * QB, 2 * QB)
        sink4 = s_ref[...].reshape(D_GROUP * QB, 1)

        def step(n, _):
            valid = _swa_valid4(n)
            kb, qs, _ps = _swa_band(kh_ref, n)
            vb, _, _ = _swa_band(vh_ref, n)
            q4 = jnp.concatenate([qh_ref[g, pl.ds(qs, QB), :] for g in range(D_GROUP)], axis=0)
            sc = lax.dot_general(q4, kb, (((1,), (1,)), ((), ())), preferred_element_type=F32)
            sc = jnp.where(valid, sc + bias4, NEG)
            m = jnp.maximum(jnp.max(sc, axis=-1, keepdims=True), sink4)
            e = jnp.exp(sc - m)
            z = jnp.sum(e, axis=-1, keepdims=True) + jnp.exp(sink4 - m)
            o4 = jnp.dot((e / z).astype(BF16), vb, preferred_element_type=F32)
            lse4 = m + jnp.log(z)
            y = jnp.zeros((QB, 256), F32)
            for g in range(D_GROUP):
                lse_ref[g, n] = _col_to_row(lse4[g * QB:(g + 1) * QB], eye)
                y = y + _place_head(o4[g * QB:(g + 1) * QB], e4[g])
            y_ref[pl.ds(qs, QB), :] = y.astype(y_ref.dtype)
            return 0

        lax.fori_loop(0, nb, step, 0, unroll=2)

    return pl.pallas_call(
        body, name=name, grid=(D_KV_HEADS, bn),
        in_specs=[pl.BlockSpec((s, 256), lambda kh, b: (b, C_DQ // 256 + kh)),
                  pl.BlockSpec((s, 128), lambda kh, b: (b, C_DK // 128)),
                  pl.BlockSpec((s, 128), lambda kh, b: (b, C_DV // 128)),
                  pl.BlockSpec((D_GROUP, QB, 2 * QB), lambda kh, b: (kh, 0, 0)),
                  pl.BlockSpec((D_GROUP, QB, 1), lambda kh, b: (kh, 0, 0))],
        out_specs=[pl.BlockSpec((s, 256), lambda kh, b: (b, kh)),
                   pl.BlockSpec((None, D_GROUP, nb, 1, QB), lambda kh, b: (b, kh, 0, 0, 0))],
        out_shape=[jax.ShapeDtypeStruct((bn * s, 512), BF16), jax.ShapeDtypeStruct((bn, D_Q_HEADS, nb, 1, QB), F32)],
        scratch_shapes=[pltpu.VMEM((D_GROUP, s, d), BF16), pltpu.VMEM((s, d), BF16), pltpu.VMEM((s, d), BF16)],
        compiler_params=_cparams(("parallel", "parallel")),
    )(proj, proj, proj, bias, sinks)


def swa3_bwd(proj, bias, sinks, yd, lse, dyd, bn, s, name):
    nb = s // QB
    d = HEAD_DIM

    def body(q_ref, k_ref, v_ref, b_ref, s_ref, y_ref, lse_ref, dy_ref, dq_ref, dk_ref, dv_ref, db_ref, dsk_ref,
             qh_ref, kh_ref, vh_ref, oh_ref, doh_ref, dkh_ref, dvh_ref):
        kh = pl.program_id(0)

        @pl.when(pl.program_id(1) == 0)
        def _():
            db_ref[...] = jnp.zeros_like(db_ref)
            dsk_ref[...] = jnp.zeros_like(dsk_ref)

        e4 = _sel(D_GROUP, 256)
        ek = _sel_at(HEAD_DIM * kh, 128)
        eye = _eye(QB)
        kh_ref[...] = _pick_head(k_ref[...], ek)
        vh_ref[...] = _pick_head(v_ref[...], ek)
        for g in range(D_GROUP):
            qh_ref[g] = _pick_scaled(q_ref[...], e4[g])
            oh_ref[g] = _pick_head(y_ref[...], e4[g])
            doh_ref[g] = _pick_head(dy_ref[...], e4[g])
        dkh_ref[...] = jnp.zeros_like(dkh_ref)
        dvh_ref[...] = jnp.zeros_like(dvh_ref)
        bias4 = b_ref[...].reshape(D_GROUP * QB, 2 * QB)
        sink4 = s_ref[...].reshape(D_GROUP * QB, 1)

        def stack(ref, qs):
            return jnp.concatenate([ref[g, pl.ds(qs, QB), :] for g in range(D_GROUP)], axis=0)

        def step(n, _):
            valid = _swa_valid4(n)
            kb, qs, ps = _swa_band(kh_ref, n)
            vb, _, _ = _swa_band(vh_ref, n)
            q4, do4, o4 = stack(qh_ref, qs), stack(doh_ref, qs), stack(oh_ref, qs)
            lse4 = jnp.concatenate([_row_to_col(lse_ref[g, n], eye) for g in range(D_GROUP)], axis=0)
            sc = lax.dot_general(q4, kb, (((1,), (1,)), ((), ())), preferred_element_type=F32)
            sc = jnp.where(valid, sc + bias4, NEG)
            p = jnp.exp(sc - lse4)
            delta = jnp.sum(do4.astype(F32) * o4.astype(F32), axis=-1, keepdims=True)
            dp = lax.dot_general(do4, vb, (((1,), (1,)), ((), ())), preferred_element_type=F32)
            ds = p * (dp - delta)
            dsb = ds.astype(BF16)
            dq4s = jnp.dot(dsb, kb, preferred_element_type=F32) * SCALE
            dkb = lax.dot_general(dsb, q4, (((0,), (0,)), ((), ())), preferred_element_type=F32)
            dvb = lax.dot_general(p.astype(BF16), do4, (((0,), (0,)), ((), ())), preferred_element_type=F32)
            db_ref[...] += ds.reshape(D_GROUP, QB, 2 * QB)
            dsk_ref[...] += (-jnp.exp(sink4 - lse4) * delta).reshape(D_GROUP, QB, 1)
            dq4 = jnp.zeros((QB, 256), F32)
            for g in range(D_GROUP):
                dq4 = dq4 + _place_head(dq4s[g * QB:(g + 1) * QB], e4[g])
            dq_ref[pl.ds(qs, QB), :] = dq4.astype(dq_ref.dtype)
            dkh_ref[pl.ds(ps, QB), :] += dkb[:QB]
            dkh_ref[pl.ds(qs, QB), :] += dkb[QB:]
            dvh_ref[pl.ds(ps, QB), :] += dvb[:QB]
            dvh_ref[pl.ds(qs, QB), :] += dvb[QB:]
            return 0

        lax.fori_loop(0, nb, step, 0, unroll=2)
        dk_ref[...] = dkh_ref[...].astype(dk_ref.dtype)
        dv_ref[...] = dvh_ref[...].astype(dv_ref.dtype)

    blk256 = pl.BlockSpec((s, 256), lambda kh, b: (b, kh))
    blk_kv = pl.BlockSpec((None, s, d), lambda kh, b: (kh, b, 0))
    blk_b = pl.BlockSpec((D_GROUP, QB, 2 * QB), lambda kh, b: (kh, 0, 0))
    blk_s = pl.BlockSpec((D_GROUP, QB, 1), lambda kh, b: (kh, 0, 0))
    return pl.pallas_call(
        body, name=name, grid=(D_KV_HEADS, bn),
        in_specs=[pl.BlockSpec((s, 256), lambda kh, b: (b, C_DQ // 256 + kh)),
                  pl.BlockSpec((s, 128), lambda kh, b: (b, C_DK // 128)),
                  pl.BlockSpec((s, 128), lambda kh, b: (b, C_DV // 128)),
                  blk_b, blk_s, blk256,
                  pl.BlockSpec((None, D_GROUP, nb, 1, QB), lambda kh, b: (b, kh, 0, 0, 0)), blk256],
        out_specs=[blk256, blk_kv, blk_kv, blk_b, blk_s],
        out_shape=[jax.ShapeDtypeStruct((bn * s, 512), BF16), jax.ShapeDtypeStruct((D_KV_HEADS, bn * s, d), BF16),
                   jax.ShapeDtypeStruct((D_KV_HEADS, bn * s, d), BF16),
                   jax.ShapeDtypeStruct((D_Q_HEADS, QB, 2 * QB), F32), jax.ShapeDtypeStruct((D_Q_HEADS, QB, 1), F32)],
        scratch_shapes=[pltpu.VMEM((D_GROUP, s, d), BF16), pltpu.VMEM((s, d), BF16), pltpu.VMEM((s, d), BF16),
                        pltpu.VMEM((D_GROUP, s, d), BF16), pltpu.VMEM((D_GROUP, s, d), BF16),
                        pltpu.VMEM((s, d), F32), pltpu.VMEM((s, d), F32)],
        compiler_params=_cparams(("parallel", "arbitrary")),
    )(proj, proj, proj, bias, sinks, yd, lse, dyd)


def assemble_dproj(pieces, dk, dv, daf, name):
    t = pieces[0].shape[0]
    tm = 512
    widths = [p.shape[1] for p in pieces]
    npc = len(pieces)
    assert sum(widths) == C_DK and all(w % 128 == 0 for w in widths)

    def body(*refs):
        p_refs = refs[:npc]
        dk_ref, dv_ref, af_ref, o_ref = refs[npc:]
        off = 0
        for r, w in zip(p_refs, widths):
            o_ref[:, off:off + w] = r[...]
            off += w
        e2 = _sel(D_KV_HEADS, 128)
        for r in (dk_ref, dv_ref):
            val = _place_head(r[0], e2[0]) + _place_head(r[1], e2[1])
            o_ref[:, off:off + 128] = val.astype(o_ref.dtype)
            off += 128
        o_ref[:, off:off + 128] = af_ref[...]
        off += 128
        o_ref[:, off:] = jnp.zeros((tm, N_PROJ - off), o_ref.dtype)

    kv_blk = pl.BlockSpec((D_KV_HEADS, tm, HEAD_DIM), lambda i: (0, i, 0))
    return pl.pallas_call(
        body, name=name, grid=(t // tm,),
        in_specs=[pl.BlockSpec((tm, w), lambda i: (i, 0)) for w in widths]
        + [kv_blk, kv_blk, pl.BlockSpec((tm, 128), lambda i: (i, 0))],
        out_specs=pl.BlockSpec((tm, N_PROJ), lambda i: (i, 0)),
        out_shape=jax.ShapeDtypeStruct((t, N_PROJ), BF16),
        compiler_params=_cparams(("parallel",)),
    )(*pieces, dk, dv, daf)


def _bucket_table():
    dist = jnp.maximum(jnp.arange(QB)[:, None] + QB - jnp.arange(2 * QB)[None, :], 0)
    max_exact = REL_BUCKETS // 2
    large = max_exact + (jnp.log(jnp.maximum(dist, 1).astype(F32) / max_exact)
                         / math.log(REL_MAX_DIST / max_exact) * (REL_BUCKETS - max_exact)).astype(jnp.int32)
    large = jnp.minimum(large, REL_BUCKETS - 1)
    return jnp.where(dist < max_exact, dist, large).astype(F32)


def band_bias_fwd(bucket, rel_bias, name):
    def body(bk_ref, rel_ref, o_ref):
        bk = bk_ref[...]
        for hh in range(D_Q_HEADS):
            acc = jnp.zeros(bk.shape, F32)
            for b in range(REL_BUCKETS):
                acc = jnp.where(bk == float(b), rel_ref[b, hh], acc)
            o_ref[hh] = acc

    return pl.pallas_call(
        body, name=name,
        in_specs=[pl.BlockSpec(memory_space=pltpu.VMEM), pl.BlockSpec(memory_space=pltpu.SMEM)],
        out_specs=pl.BlockSpec(memory_space=pltpu.VMEM),
        out_shape=jax.ShapeDtypeStruct((D_Q_HEADS, QB, 2 * QB), F32),
    )(bucket, rel_bias)


def band_bias_bwd(bucket, dbias_layers, name):
    nl = len(dbias_layers)

    def body(bk_ref, *refs):
        o_ref = refs[nl]
        bk = bk_ref[...]
        for hh in range(D_Q_HEADS):
            tot = refs[0][hh]
            for r in refs[1:nl]:
                tot = tot + r[hh]
            for b in range(REL_BUCKETS):
                part = jnp.sum(jnp.where(bk == float(b), tot, 0.0), axis=0, keepdims=True)
                val = jnp.sum(part, axis=1, keepdims=True)
                o_ref[hh, b:b + 1, :] = jnp.broadcast_to(val, (1, 128))

    return pl.pallas_call(
        body, name=name,
        in_specs=[pl.BlockSpec(memory_space=pltpu.VMEM)] * (nl + 1),
        out_specs=pl.BlockSpec(memory_space=pltpu.VMEM),
        out_shape=jax.ShapeDtypeStruct((D_Q_HEADS, REL_BUCKETS, 128), F32),
    )(bucket, *dbias_layers)


def _proj_blk(s, col):
    return pl.BlockSpec((s, 256), functools.partial(lambda b, cb: (b, cb), cb=col // 256))


def convb_fwd(proj, w, bn, s, name):
    kk = w.shape[0]

    def body(bg_ref, cg_ref, xb_ref, w_ref, o_ref):
        x = cg_ref[...].astype(F32) * xb_ref[...].astype(F32)
        row = lax.broadcasted_iota(jnp.int32, x.shape, 0)
        y = jnp.zeros_like(x)
        for k in range(kk):
            y = y + w_ref[k:k + 1, :] * _shift_down(x, kk - 1 - k, row)
        o_ref[...] = (bg_ref[...].astype(F32) * y).astype(o_ref.dtype)

    return pl.pallas_call(
        body, name=name, grid=(bn,),
        in_specs=[_proj_blk(s, C_BG), _proj_blk(s, C_CG), _proj_blk(s, C_XB), pl.BlockSpec(w.shape, lambda b: (0, 0))],
        out_specs=pl.BlockSpec((s, 256), lambda b: (b, 0)),
        out_shape=jax.ShapeDtypeStruct((bn * s, 256), BF16),
        compiler_params=_cparams(("parallel",)),
    )(proj, proj, proj, w)


def convb_bwd(proj, w, dyb, bn, s, name):
    kk = w.shape[0]

    def body(bg_ref, cg_ref, xb_ref, w_ref, d_ref, dbg_ref, dcg_ref, dxb_ref, dw_ref):
        @pl.when(pl.program_id(0) == 0)
        def _():
            dw_ref[...] = jnp.zeros_like(dw_ref)

        cg = cg_ref[...].astype(F32)
        xb = xb_ref[...].astype(F32)
        d = d_ref[...].astype(F32)
        x = cg * xb
        row = lax.broadcasted_iota(jnp.int32, x.shape, 0)
        dy = d * bg_ref[...].astype(F32)
        y = jnp.zeros_like(x)
        dx = jnp.zeros_like(x)
        for k in range(kk):
            xs = _shift_down(x, kk - 1 - k, row)
            y = y + w_ref[k:k + 1, :] * xs
            dx = dx + w_ref[k:k + 1, :] * _shift_up(dy, kk - 1 - k, row)
            dw_ref[k:k + 1, :] += _sum0(dy * xs)
        dbg_ref[...] = (d * y).astype(dbg_ref.dtype)
        dcg_ref[...] = (dx * xb).astype(dcg_ref.dtype)
        dxb_ref[...] = (dx * cg).astype(dxb_ref.dtype)

    blk = pl.BlockSpec((s, 256), lambda b: (b, 0))
    return pl.pallas_call(
        body, name=name, grid=(bn,),
        in_specs=[_proj_blk(s, C_BG), _proj_blk(s, C_CG), _proj_blk(s, C_XB), pl.BlockSpec(w.shape, lambda b: (0, 0)),
                  blk],
        out_specs=[blk, blk, blk, pl.BlockSpec((8, 256), lambda b: (0, 0))],
        out_shape=[jax.ShapeDtypeStruct((bn * s, 256), BF16)] * 3 + [jax.ShapeDtypeStruct((8, 256), F32)],
        compiler_params=_cparams(("arbitrary",)),
    )(proj, proj, proj, w, dyb)


def _convc_core(ca, cb, w_ref, bias, kk, row):
    sg = jax.nn.sigmoid(cb)
    glu = ca * sg
    y = jnp.zeros_like(glu)
    for k in range(kk):
        y = y + w_ref[k:k + 1, :] * _shift_down(glu, kk - 1 - k, row)
    y = y + bias
    mu = jnp.mean(y, axis=-1, keepdims=True)
    xc = y - mu
    r = lax.rsqrt(jnp.mean(xc * xc, axis=-1, keepdims=True) + EPS)
    return sg, glu, xc * r, r


def convc_fwd(proj, w, bias, gain, lbias, bn, s, name):
    kk = w.shape[0]

    def body(ca_ref, cb_ref, w_ref, b_ref, g_ref, lb_ref, o_ref):
        ca = ca_ref[...].astype(F32)
        row = lax.broadcasted_iota(jnp.int32, ca.shape, 0)
        _, _, xh, _ = _convc_core(ca, cb_ref[...].astype(F32), w_ref, b_ref[...], kk, row)
        ln = xh * g_ref[...] + lb_ref[...]
        o_ref[...] = (ln * jax.nn.sigmoid(ln)).astype(o_ref.dtype)

    vec = pl.BlockSpec((1, 256), lambda b: (0, 0))
    return pl.pallas_call(
        body, name=name, grid=(bn,),
        in_specs=[_proj_blk(s, C_CA), _proj_blk(s, C_CB), pl.BlockSpec(w.shape, lambda b: (0, 0)), vec, vec, vec],
        out_specs=pl.BlockSpec((s, 256), lambda b: (b, 0)),
        out_shape=jax.ShapeDtypeStruct((bn * s, 256), BF16),
        compiler_params=_cparams(("parallel",)),
    )(proj, proj, w, bias, gain, lbias)


def convc_bwd(proj, w, bias, gain, lbias, dyc, bn, s, name):
    kk = w.shape[0]

    def body(ca_ref, cb_ref, w_ref, b_ref, g_ref, lb_ref, d_ref, dca_ref, dcb_ref, dw_ref, db_ref, dg_ref, dlb_ref):
        @pl.when(pl.program_id(0) == 0)
        def _():
            dw_ref[...] = jnp.zeros_like(dw_ref)
            db_ref[...] = jnp.zeros_like(db_ref)
            dg_ref[...] = jnp.zeros_like(dg_ref)
            dlb_ref[...] = jnp.zeros_like(dlb_ref)

        ca = ca_ref[...].astype(F32)
        row = lax.broadcasted_iota(jnp.int32, ca.shape, 0)
        sg, glu, xh, r = _convc_core(ca, cb_ref[...].astype(F32), w_ref, b_ref[...], kk, row)
        ln = xh * g_ref[...] + lb_ref[...]
        sl = jax.nn.sigmoid(ln)
        dl = d_ref[...].astype(F32) * (sl + ln * sl * (1.0 - sl))
        dg_ref[...] += _sum0(dl * xh)
        dlb_ref[...] += _sum0(dl)
        dxh = dl * g_ref[...]
        dy = r * (dxh - jnp.mean(dxh, axis=-1, keepdims=True) - xh * jnp.mean(dxh * xh, axis=-1, keepdims=True))
        db_ref[...] += _sum0(dy)
        dglu = jnp.zeros_like(glu)
        for k in range(kk):
            dw_ref[k:k + 1, :] += _sum0(dy * _shift_down(glu, kk - 1 - k, row))
            dglu = dglu + w_ref[k:k + 1, :] * _shift_up(dy, kk - 1 - k, row)
        dca_ref[...] = (dglu * sg).astype(dca_ref.dtype)
        dcb_ref[...] = (dglu * ca * sg * (1.0 - sg)).astype(dcb_ref.dtype)

    vec = pl.BlockSpec((1, 256), lambda b: (0, 0))
    blk = pl.BlockSpec((s, 256), lambda b: (b, 0))
    return pl.pallas_call(
        body, name=name, grid=(bn,),
        in_specs=[_proj_blk(s, C_CA), _proj_blk(s, C_CB), pl.BlockSpec(w.shape, lambda b: (0, 0)), vec, vec, vec, blk],
        out_specs=[blk, blk, pl.BlockSpec((32, 256), lambda b: (0, 0)), vec, vec, vec],
        out_shape=[jax.ShapeDtypeStruct((bn * s, 256), BF16)] * 2 + [jax.ShapeDtypeStruct((32, 256), F32)]
        + [jax.ShapeDtypeStruct((1, 256), F32)] * 3,
        compiler_params=_cparams(("arbitrary",)),
    )(proj, proj, w, bias, gain, lbias, dyc)


def adamw(w, g, m, v, name):
    shape = w.shape
    cols = shape[-1]
    rows = w.size // cols
    tr = _pick(rows, (256, 128, 64, 32, 16, 8))

    def body(w_ref, g_ref, m_ref, v_ref, d_ref, nm_ref, nv_ref):
        gg = g_ref[...]
        mm = ADAM_B1 * m_ref[...] + (1.0 - ADAM_B1) * gg
        vv = ADAM_B2 * v_ref[...] + (1.0 - ADAM_B2) * jnp.square(gg)
        m_hat = mm / (1.0 - ADAM_B1 ** ADAM_STEP)
        v_hat = vv / (1.0 - ADAM_B2 ** ADAM_STEP)
        d_ref[...] = -ADAM_LR * (m_hat / (jnp.sqrt(v_hat) + ADAM_EPS) + ADAM_WD * w_ref[...])
        nm_ref[...] = mm
        nv_ref[...] = vv

    blk = pl.BlockSpec((tr, cols), lambda i: (i, 0))
    outs = pl.pallas_call(
        body, name=name, grid=(rows // tr,), in_specs=[blk] * 4, out_specs=[blk] * 3,
        out_shape=[jax.ShapeDtypeStruct((rows, cols), F32)] * 3,
        compiler_params=_cparams(("parallel",)),
    )(*[a.reshape(rows, cols) for a in (w, g, m, v)])
    return [o.reshape(shape) for o in outs]


def add_halves(own, recv, name):
    n, r, c = own.shape
    tr = _pick(r, (512, 256, 128, 64, 32, 16, 8))
    blk = pl.BlockSpec((None, tr, c), lambda i, j: (i, j, 0))

    def body(a_ref, b_ref, o_ref):
        o_ref[...] = a_ref[...] + b_ref[...]

    return pl.pallas_call(
        body, name=name, grid=(n, r // tr), in_specs=[blk, blk], out_specs=blk,
        out_shape=jax.ShapeDtypeStruct((n, r, c), F32), compiler_params=_cparams(("parallel", "parallel")),
    )(own, recv)


def sum_slots(slots, name):
    n, r, c = slots.shape
    tr = _pick(r, (512, 256, 128, 64, 32, 16, 8))

    def body(a_ref, o_ref):
        acc = a_ref[0]
        for k in range(1, n):
            acc = acc + a_ref[k]
        o_ref[...] = acc

    return pl.pallas_call(
        body, name=name, grid=(r // tr,), in_specs=[pl.BlockSpec((n, tr, c), lambda j: (0, j, 0))],
        out_specs=pl.BlockSpec((tr, c), lambda j: (j, 0)),
        out_shape=jax.ShapeDtypeStruct((r, c), F32), compiler_params=_cparams(("parallel",)),
    )(slots)


ANY = pl.BlockSpec(memory_space=pl.ANY)


def _place():
    x, y, c = lax.axis_index("x"), lax.axis_index("y"), lax.axis_index("c")
    return x, y, c


def gather_shards(pack, name):
    r, cols = pack.shape
    half = r // 2

    def body(src_ref, out_ref, send_sems, recv_sems, local_sem):
        x, y, c = _place()
        sibling = (x, y, 1 - c)
        chips = [(1 - x, y), (x, 1 - y), (1 - x, 1 - y)]

        def rows(px, py, pc):
            return out_ref.at[2 * px + py, pl.ds(pc * half, half), :]

        mine = pltpu.make_async_copy(src_ref, out_ref.at[2 * x + y], local_sem)
        mine.start()

        def copy(k, blk, to, src=None):
            return pltpu.make_async_remote_copy(
                src_ref=rows(*blk) if src is None else src, dst_ref=rows(*blk),
                send_sem=send_sems.at[k], recv_sem=recv_sems.at[k], device_id=to, device_id_type=MESH)

        first = [copy(j, (x, y, c), (*chip, c), src=src_ref.at[pl.ds(c * half, half), :])
                 for j, chip in enumerate(chips)]
        for cp in first:
            cp.start()
        passed = [copy(3 + j, (*chip, c), sibling) for j, chip in enumerate(chips)]
        for j, chip in enumerate(chips):
            copy(j, (*chip, c), (x, y, c)).wait_recv()
            passed[j].start()
        for j, chip in enumerate(chips):
            copy(3 + j, (*chip, 1 - c), (x, y, c)).wait_recv()
        for cp in first + passed:
            cp.wait_send()
        mine.wait()

    return pl.pallas_call(
        body, name=name, in_specs=[ANY], out_specs=ANY,
        out_shape=jax.ShapeDtypeStruct((N_CHIPS, r, cols), pack.dtype),
        scratch_shapes=[pltpu.SemaphoreType.DMA((6,)), pltpu.SemaphoreType.DMA((6,)), pltpu.SemaphoreType.DMA],
    )(pack)


def exchange_sibling_halves(g, name):
    n, r, cols = g.shape
    half = r // 2

    def body(g_ref, own_ref, recv_ref, send_sems, recv_sems, local_sem):
        x, y, c = _place()
        sibling = (x, y, 1 - c)
        mine = pltpu.make_async_copy(g_ref.at[:, pl.ds(c * half, half), :], own_ref, local_sem)
        mine.start()
        cp = pltpu.make_async_remote_copy(
            src_ref=g_ref.at[:, pl.ds((1 - c) * half, half), :], dst_ref=recv_ref,
            send_sem=send_sems.at[0], recv_sem=recv_sems.at[0], device_id=sibling, device_id_type=MESH)
        cp.start()
        cp.wait()
        mine.wait()

    return pl.pallas_call(
        body, name=name, in_specs=[ANY], out_specs=[ANY, ANY],
        out_shape=[jax.ShapeDtypeStruct((n, half, cols), g.dtype)] * 2,
        scratch_shapes=[pltpu.SemaphoreType.DMA((1,)), pltpu.SemaphoreType.DMA((1,)), pltpu.SemaphoreType.DMA],
    )(g)


def scatter_to_chips(part, name):
    n, h, cols = part.shape

    def body(p_ref, slot_ref, send_sems, recv_sems, local_sem):
        x, y, c = _place()
        me = 2 * x + y
        chips = [(1 - x, y), (x, 1 - y), (1 - x, 1 - y)]
        mine = pltpu.make_async_copy(p_ref.at[me], slot_ref.at[me], local_sem)
        mine.start()
        cps = [pltpu.make_async_remote_copy(
            src_ref=p_ref.at[2 * px + py], dst_ref=slot_ref.at[me],
            send_sem=send_sems.at[j], recv_sem=recv_sems.at[j], device_id=(px, py, c), device_id_type=MESH)
            for j, (px, py) in enumerate(chips)]
        for cp in cps:
            cp.start()
        for j, (px, py) in enumerate(chips):
            pltpu.make_async_remote_copy(
                src_ref=p_ref.at[me], dst_ref=slot_ref.at[2 * px + py],
                send_sem=send_sems.at[j], recv_sem=recv_sems.at[j], device_id=(px, py, c),
                device_id_type=MESH).wait_recv()
        for cp in cps:
            cp.wait_send()
        mine.wait()

    return pl.pallas_call(
        body, name=name, in_specs=[ANY], out_specs=ANY,
        out_shape=jax.ShapeDtypeStruct((n, h, cols), part.dtype),
        scratch_shapes=[pltpu.SemaphoreType.DMA((3,)), pltpu.SemaphoreType.DMA((3,)), pltpu.SemaphoreType.DMA],
    )(part)


def join_sibling_halves(mine_half, name):
    h, cols = mine_half.shape

    def body(m_ref, out_ref, send_sems, recv_sems, local_sem):
        x, y, c = _place()
        sibling = (x, y, 1 - c)
        own = pltpu.make_async_copy(m_ref, out_ref.at[pl.ds(c * h, h), :], local_sem)
        own.start()
        cp = pltpu.make_async_remote_copy(
            src_ref=m_ref, dst_ref=out_ref.at[pl.ds(c * h, h), :],
            send_sem=send_sems.at[0], recv_sem=recv_sems.at[0], device_id=sibling, device_id_type=MESH)
        cp.start()
        pltpu.make_async_remote_copy(
            src_ref=m_ref, dst_ref=out_ref.at[pl.ds((1 - c) * h, h), :],
            send_sem=send_sems.at[0], recv_sem=recv_sems.at[0], device_id=sibling, device_id_type=MESH).wait_recv()
        cp.wait_send()
        own.wait()

    return pl.pallas_call(
        body, name=name, in_specs=[ANY], out_specs=ANY,
        out_shape=jax.ShapeDtypeStruct((2 * h, cols), mine_half.dtype),
        scratch_shapes=[pltpu.SemaphoreType.DMA((1,)), pltpu.SemaphoreType.DMA((1,)), pltpu.SemaphoreType.DMA],
    )(mine_half)


def _kind(n):
    return 'win' if n == 'w_in' else ('row' if n in ROW_SHARDED else 'col')


def _chip_ids():
    x, y, c = _place()
    chips = [(1 - x, y), (x, 1 - y), (1 - x, 1 - y)]
    return x, y, c, 2 * x + y, chips, [2 * px + py for px, py in chips]


def gather_weights(shards, name):
    names = list(SHARDED)
    nt = len(names)
    kinds = [_kind(n) for n in names]
    shapes = [shards[n].shape for n in names]
    depth = shapes[0][0]
    half = depth // 2

    def out_shape(kind, shp):
        if kind == 'col':
            return (shp[0], shp[1], N_CHIPS * shp[2])
        if kind == 'row':
            return (shp[0], N_CHIPS * shp[1], shp[2])
        return (N_CHIPS,) + tuple(shp)

    def body(*refs):
        src, out = refs[:nt], refs[nt:2 * nt]
        send_sems, recv_sems = refs[2 * nt:]
        x, y, c, me, chips, chip_idx = _chip_ids()
        sibling = (x, y, 1 - c)

        def win(t, chip, lo, cnt):
            _, a, b = shapes[t]
            if kinds[t] == 'col':
                return out[t].at[pl.ds(lo, cnt), :, pl.ds(chip * b, b)]
            if kinds[t] == 'row':
                return out[t].at[pl.ds(lo, cnt), pl.ds(chip * a, a), :]
            return out[t].at[chip, pl.ds(lo, cnt)]

        def remote(t, k, chip, lo, to, src_ref=None):
            w = win(t, chip, lo, half)
            return pltpu.make_async_remote_copy(
                src_ref=w if src_ref is None else src_ref, dst_ref=w, send_sem=send_sems.at[7 * t + k],
                recv_sem=recv_sems.at[7 * t + k], device_id=to, device_id_type=MESH)

        def own(t):
            return pltpu.make_async_remote_copy(
                src_ref=src[t], dst_ref=win(t, me, 0, depth), send_sem=send_sems.at[7 * t + 6],
                recv_sem=recv_sems.at[7 * t + 6], device_id=sibling, device_id_type=MESH)

        mine = [own(t) for t in range(nt)]
        for cp in mine:
            cp.start()
        first = [[remote(t, j, me, c * half, (*chips[j], c), src_ref=src[t].at[pl.ds(c * half, half)])
                  for j in range(3)] for t in range(nt)]
        for t in range(nt):
            for cp in first[t]:
                cp.start()
        passed = [[remote(t, 3 + j, chip_idx[j], c * half, sibling) for j in range(3)] for t in range(nt)]
        for t in range(nt):
            for j in range(3):
                remote(t, j, chip_idx[j], c * half, (x, y, c)).wait_recv()
                passed[t][j].start()
        for t in range(nt):
            for j in range(3):
                remote(t, 3 + j, chip_idx[j], (1 - c) * half, (x, y, c)).wait_recv()
        for t in range(nt):
            for cp in first[t] + passed[t]:
                cp.wait_send()
            mine[t].wait()

    outs = pl.pallas_call(
        body, name=name, in_specs=[ANY] * nt, out_specs=[ANY] * nt,
        out_shape=[jax.ShapeDtypeStruct(out_shape(k, s), BF16) for k, s in zip(kinds, shapes)],
        scratch_shapes=[pltpu.SemaphoreType.DMA((7 * nt,)), pltpu.SemaphoreType.DMA((7 * nt,))],
    )(*[shards[n] for n in names])
    return dict(zip(names, outs))


def _half_win(ref, kind, hc, layer):
    if kind == 'col':
        hk = ref.shape[1] // 2
        return ref.at[layer, pl.ds(hc * hk, hk), :]
    if kind == 'row':
        hn = ref.shape[2] // 2
        return ref.at[layer, :, pl.ds(hc * hn, hn)]
    hk = ref.shape[2] // 2
    return ref.at[layer, :, pl.ds(hc * hk, hk), :]


def _half_shape(kind, shp):
    if kind == 'col':
        return (shp[0], shp[1] // 2, shp[2])
    if kind == 'row':
        return (shp[0], shp[1], shp[2] // 2)
    return (shp[0], shp[1], shp[2] // 2, shp[3])


def rs_sibling(grads, name):
    names = list(SHARDED)
    nt = len(names)
    kinds = [_kind(n) for n in names]
    shapes = [grads[n].shape for n in names]
    depth = shapes[0][0]

    def body(*refs):
        src, out = refs[:nt], refs[nt:2 * nt]
        send_sems, recv_sems = refs[2 * nt:]
        x, y, c = _place()
        cps = []
        for t in range(nt):
            for l in range(depth):
                cps.append(pltpu.make_async_remote_copy(
                    src_ref=_half_win(src[t], kinds[t], 1 - c, l), dst_ref=out[t].at[l],
                    send_sem=send_sems.at[depth * t + l], recv_sem=recv_sems.at[depth * t + l],
                    device_id=(x, y, 1 - c), device_id_type=MESH))
        for cp in cps:
            cp.start()
        for cp in cps:
            cp.wait()

    outs = pl.pallas_call(
        body, name=name, in_specs=[ANY] * nt, out_specs=[ANY] * nt,
        out_shape=[jax.ShapeDtypeStruct(_half_shape(k, s), F32) for k, s in zip(kinds, shapes)],
        scratch_shapes=[pltpu.SemaphoreType.DMA((depth * nt,)), pltpu.SemaphoreType.DMA((depth * nt,))],
    )(*[grads[n] for n in names])
    return dict(zip(names, outs))


EW_BLOCK_ELEMS = 512 * 1024


def rs_add(kind, g, recv, c_arr, name):
    shp = recv.shape
    rows, cols = shp[-2], shp[-1]
    tr = _pick(rows, [r for r in (1408, 1024, 704, 512, 256, 128, 64, 32, 16, 8) if r * cols <= EW_BLOCK_ELEMS])
    nb = rows // tr
    lead = (None,) * (len(shp) - 2)
    blk = pl.BlockSpec(lead + (tr, cols), lambda *a: tuple(a[:len(shp) - 2]) + (a[len(shp) - 2], 0))
    if kind == 'row':
        g_blk = pl.BlockSpec(lead + (tr, cols), lambda *a: tuple(a[:len(shp) - 2]) + (a[len(shp) - 2], a[-1][0]))
    else:
        g_blk = pl.BlockSpec(lead + (tr, cols),
                             lambda *a: tuple(a[:len(shp) - 2]) + (a[-1][0] * nb + a[len(shp) - 2], 0))

    def body(c_ref, g_ref, r_ref, o_ref):
        o_ref[...] = (g_ref[...] + r_ref[...]).astype(o_ref.dtype)

    grid_spec = pltpu.PrefetchScalarGridSpec(
        num_scalar_prefetch=1, grid=tuple(shp[:-2]) + (nb,), in_specs=[g_blk, blk], out_specs=blk)
    return pl.pallas_call(
        body, name=name, grid_spec=grid_spec, out_shape=jax.ShapeDtypeStruct(shp, BF16),
        compiler_params=_cparams(None),
    )(c_arr, g, recv)


def _chip_win(ref, kind, chip):
    if kind == 'col':
        ns = ref.shape[2] // N_CHIPS
        return ref.at[:, :, pl.ds(chip * ns, ns)]
    if kind == 'row':
        ks = ref.shape[1] // N_CHIPS
        return ref.at[:, pl.ds(chip * ks, ks), :]
    return ref.at[:, chip]


def _chip_shape(kind, shp):
    if kind == 'col':
        return (shp[0], shp[1], shp[2] // N_CHIPS)
    if kind == 'row':
        return (shp[0], shp[1] // N_CHIPS, shp[2])
    return (shp[0], shp[2], shp[3])


def rs_chips(parts, name):
    names = list(SHARDED)
    nt = len(names)
    kinds = [_kind(n) for n in names]
    shapes = [parts[n].shape for n in names]

    def body(*refs):
        src, out = refs[:nt], refs[nt:2 * nt]
        send_sems, recv_sems, local_sems = refs[2 * nt:]
        x, y, c, me, chips, chip_idx = _chip_ids()
        mine = [pltpu.make_async_copy(_chip_win(src[t], kinds[t], me), out[t].at[me], local_sems.at[t])
                for t in range(nt)]
        for cp in mine:
            cp.start()
        cps = [[pltpu.make_async_remote_copy(
            src_ref=_chip_win(src[t], kinds[t], chip_idx[j]), dst_ref=out[t].at[me],
            send_sem=send_sems.at[3 * t + j], recv_sem=recv_sems.at[3 * t + j],
            device_id=(*chips[j], c), device_id_type=MESH) for j in range(3)] for t in range(nt)]
        for t in range(nt):
            for cp in cps[t]:
                cp.start()
        for t in range(nt):
            for j in range(3):
                pltpu.make_async_remote_copy(
                    src_ref=_chip_win(src[t], kinds[t], me), dst_ref=out[t].at[chip_idx[j]],
                    send_sem=send_sems.at[3 * t + j], recv_sem=recv_sems.at[3 * t + j],
                    device_id=(*chips[j], c), device_id_type=MESH).wait_recv()
        for t in range(nt):
            for cp in cps[t]:
                cp.wait_send()
            mine[t].wait()

    outs = pl.pallas_call(
        body, name=name, in_specs=[ANY] * nt, out_specs=[ANY] * nt,
        out_shape=[jax.ShapeDtypeStruct((N_CHIPS,) + _chip_shape(k, s), parts[n].dtype)
                   for n, k, s in zip(names, kinds, shapes)],
        scratch_shapes=[pltpu.SemaphoreType.DMA((3 * nt,)), pltpu.SemaphoreType.DMA((3 * nt,)),
                        pltpu.SemaphoreType.DMA((nt,))],
    )(*[parts[n] for n in names])
    return dict(zip(names, outs))


def rs_sum(slots, name):
    n, depth, r, cols = slots.shape
    tr = _pick(r, [q for q in (1408, 1024, 704, 512, 256, 128, 64, 32, 16, 8) if q * cols * n <= 2 * EW_BLOCK_ELEMS])

    def body(a_ref, o_ref):
        acc = a_ref[0].astype(F32)
        for k in range(1, n):
            acc = acc + a_ref[k].astype(F32)
        o_ref[...] = acc

    return pl.pallas_call(
        body, name=name, grid=(depth, r // tr),
        in_specs=[pl.BlockSpec((n, None, tr, cols), lambda l, i: (0, l, i, 0))],
        out_specs=pl.BlockSpec((None, tr, cols), lambda l, i: (l, i, 0)),
        out_shape=jax.ShapeDtypeStruct((depth, r, cols), F32), compiler_params=_cparams(("parallel", "parallel")),
    )(slots)


def rs_join(reds, name):
    names = list(SHARDED)
    nt = len(names)
    kinds = [_kind(n) for n in names]
    shapes = [reds[n].shape for n in names]
    depth = shapes[0][0]

    def full_shape(kind, shp):
        if kind == 'row':
            return (shp[0], shp[1], 2 * shp[2])
        return (shp[0], 2 * shp[1], shp[2])

    def win(ref, kind, hc, layer):
        if kind == 'row':
            hn = ref.shape[2] // 2
            return ref.at[layer, :, pl.ds(hc * hn, hn)]
        hk = ref.shape[1] // 2
        return ref.at[layer, pl.ds(hc * hk, hk), :]

    def body(*refs):
        src, out = refs[:nt], refs[nt:2 * nt]
        send_sems, recv_sems, local_sems = refs[2 * nt:]
        x, y, c = _place()
        own, cps = [], []
        for t in range(nt):
            for l in range(depth):
                i = depth * t + l
                own.append(pltpu.make_async_copy(src[t].at[l], win(out[t], kinds[t], c, l), local_sems.at[i]))
                cps.append(pltpu.make_async_remote_copy(
                    src_ref=src[t].at[l], dst_ref=win(out[t], kinds[t], c, l), send_sem=send_sems.at[i],
                    recv_sem=recv_sems.at[i], device_id=(x, y, 1 - c), device_id_type=MESH))
        for cp in own + cps:
            cp.start()
        for t in range(nt):
            for l in range(depth):
                i = depth * t + l
                pltpu.make_async_remote_copy(
                    src_ref=src[t].at[l], dst_ref=win(out[t], kinds[t], 1 - c, l), send_sem=send_sems.at[i],
                    recv_sem=recv_sems.at[i], device_id=(x, y, 1 - c), device_id_type=MESH).wait_recv()
        for cp in cps:
            cp.wait_send()
        for cp in own:
            cp.wait()

    outs = pl.pallas_call(
        body, name=name, in_specs=[ANY] * nt, out_specs=[ANY] * nt,
        out_shape=[jax.ShapeDtypeStruct(full_shape(k, s), F32) for k, s in zip(kinds, shapes)],
        scratch_shapes=[pltpu.SemaphoreType.DMA((depth * nt,)), pltpu.SemaphoreType.DMA((depth * nt,)),
                        pltpu.SemaphoreType.DMA((depth * nt,))],
    )(*[reds[n] for n in names])
    return dict(zip(names, outs))


def rs_chips2(parts, name):
    names = list(SHARDED)
    nt = len(names)
    kinds = [_kind(n) for n in names]
    shapes = [parts[n].shape for n in names]

    def body(*refs):
        src, out = refs[:nt], refs[nt:2 * nt]
        send_sems, recv_sems = refs[2 * nt:]
        x, y, c, me, chips, chip_idx = _chip_ids()
        cps = [[pltpu.make_async_remote_copy(
            src_ref=_chip_win(src[t], kinds[t], chip_idx[j]), dst_ref=out[t].at[j],
            send_sem=send_sems.at[3 * t + j], recv_sem=recv_sems.at[3 * t + j],
            device_id=(*chips[j], c), device_id_type=MESH) for j in range(3)] for t in range(nt)]
        for t in range(nt):
            for cp in cps[t]:
                cp.start()
        for t in range(nt):
            for cp in cps[t]:
                cp.wait()

    outs = pl.pallas_call(
        body, name=name, in_specs=[ANY] * nt, out_specs=[ANY] * nt,
        out_shape=[jax.ShapeDtypeStruct((3,) + _chip_shape(k, s), parts[n].dtype)
                   for n, k, s in zip(names, kinds, shapes)],
        scratch_shapes=[pltpu.SemaphoreType.DMA((3 * nt,)), pltpu.SemaphoreType.DMA((3 * nt,))],
    )(*[parts[n] for n in names])
    return dict(zip(names, outs))


def rs_sum2(kind, part, slots, sc_arr, name):
    _, depth, r, cols = slots.shape
    tr = _pick(r, [q for q in (1408, 1024, 704, 512, 256, 128, 64, 32, 16) if q * cols <= EW_BLOCK_ELEMS // 2])
    nb = r // tr
    if kind == 'col':
        own_blk = pl.BlockSpec((None, tr, cols), lambda l, i, sc: (l, i, sc[0]))
        out_blk = pl.BlockSpec((None, tr, cols), lambda l, i, sc: (l, sc[1] * nb + i, 0))
        out_shape = (depth, 2 * r, cols)
    elif kind == 'row':
        own_blk = pl.BlockSpec((None, tr, cols), lambda l, i, sc: (l, sc[0] * nb + i, 0))
        out_blk = pl.BlockSpec((None, tr, cols), lambda l, i, sc: (l, i, sc[1]))
        out_shape = (depth, r, 2 * cols)
    else:
        own_blk = pl.BlockSpec((None, None, tr, cols), lambda l, i, sc: (l, sc[0], i, 0))
        out_blk = pl.BlockSpec((None, tr, cols), lambda l, i, sc: (l, sc[1] * nb + i, 0))
        out_shape = (depth, 2 * r, cols)

    def body(sc_ref, own_ref, s_ref, o_ref):
        acc = own_ref[...].astype(F32)
        for k in range(3):
            acc = acc + s_ref[k].astype(F32)
        o_ref[...] = acc

    grid_spec = pltpu.PrefetchScalarGridSpec(
        num_scalar_prefetch=1, grid=(depth, nb),
        in_specs=[own_blk, pl.BlockSpec((3, None, tr, cols), lambda l, i, sc: (0, l, i, 0))], out_specs=out_blk)
    return pl.pallas_call(
        body, name=name, grid_spec=grid_spec, out_shape=jax.ShapeDtypeStruct(out_shape, F32),
        compiler_params=_cparams(None),
    )(sc_arr, part, slots)


def rs_join2(halves, name):
    names = list(SHARDED)
    nt = len(names)
    kinds = [_kind(n) for n in names]
    shapes = [halves[n].shape for n in names]
    depth = shapes[0][0]

    def win(ref, kind, hc, layer):
        if kind == 'row':
            hn = ref.shape[2] // 2
            return ref.at[layer, :, pl.ds(hc * hn, hn)]
        hk = ref.shape[1] // 2
        return ref.at[layer, pl.ds(hc * hk, hk), :]

    def body(*refs):
        src, out = refs[:nt], refs[nt:2 * nt]
        send_sems, recv_sems = refs[2 * nt:]
        x, y, c = _place()
        cps = []
        for t in range(nt):
            for l in range(depth):
                i = depth * t + l
                cps.append(pltpu.make_async_remote_copy(
                    src_ref=win(src[t], kinds[t], c, l), dst_ref=win(out[t], kinds[t], c, l),
                    send_sem=send_sems.at[i], recv_sem=recv_sems.at[i], device_id=(x, y, 1 - c),
                    device_id_type=MESH))
        for cp in cps:
            cp.start()
        for t in range(nt):
            for l in range(depth):
                i = depth * t + l
                pltpu.make_async_remote_copy(
                    src_ref=win(src[t], kinds[t], c, l), dst_ref=win(out[t], kinds[t], 1 - c, l),
                    send_sem=send_sems.at[i], recv_sem=recv_sems.at[i], device_id=(x, y, 1 - c),
                    device_id_type=MESH).wait_recv()
        for cp in cps:
            cp.wait_send()

    outs = pl.pallas_call(
        body, name=name, in_specs=[ANY] * nt, out_specs=[ANY] * nt,
        out_shape=[jax.ShapeDtypeStruct(s, F32) for s in shapes],
        input_output_aliases={t: t for t in range(nt)},
        scratch_shapes=[pltpu.SemaphoreType.DMA((depth * nt,)), pltpu.SemaphoreType.DMA((depth * nt,))],
    )(*[halves[n] for n in names])
    return dict(zip(names, outs))


def gather_small(v, name):
    r, cols = v.shape

    def body(v_ref, out_ref, send_sems, recv_sems):
        x, y, c = _place()
        me = 4 * x + 2 * y + c
        out_ref[me] = v_ref[...]
        cps = []
        for rel in range(1, N_DEV):
            px = 1 - x if (rel >> 2) & 1 else x
            py = 1 - y if (rel >> 1) & 1 else y
            pc = 1 - c if rel & 1 else c
            cps.append(pltpu.make_async_remote_copy(
                src_ref=v_ref, dst_ref=out_ref.at[me], send_sem=send_sems.at[rel - 1],
                recv_sem=recv_sems.at[rel - 1], device_id=(px, py, pc), device_id_type=MESH))
        for cp in cps:
            cp.start()
        for cp in cps:
            cp.wait()

    return pl.pallas_call(
        body, name=name, in_specs=[pl.BlockSpec(memory_space=pltpu.VMEM)],
        out_specs=pl.BlockSpec(memory_space=pltpu.VMEM),
        out_shape=jax.ShapeDtypeStruct((N_DEV, r, cols), v.dtype),
        scratch_shapes=[pltpu.SemaphoreType.DMA((N_DEV - 1,)), pltpu.SemaphoreType.DMA((N_DEV - 1,))],
        compiler_params=pltpu.CompilerParams(vmem_limit_bytes=VMEM_LIMIT),
    )(v)


def _pad_rows(flat, row_align):
    n = flat.shape[-1]
    unit = PACK_COLS * row_align
    tot = -(-n // unit) * unit
    pad = [(0, 0)] * (flat.ndim - 1) + [(0, tot - n)]
    return jnp.pad(flat, pad)


def _pack_shards(ws):
    flat = jnp.concatenate([ws[n].astype(BF16).reshape(-1) for n in SHARDED])
    return _pad_rows(flat, PACK_ROW_ALIGN).reshape(-1, PACK_COLS)


def _unpack_full(gathered, shard_shapes):
    flat = gathered.reshape(N_CHIPS, -1)
    out, off = {}, 0
    for n in SHARDED:
        shp = shard_shapes[n]
        size = math.prod(shp)
        seg = flat[:, off:off + size].reshape((N_CHIPS,) + tuple(shp))
        off += size
        if n in ROW_SHARDED:
            out[n] = jnp.transpose(seg, (1, 0, 2, 3)).reshape(shp[0], N_CHIPS * shp[1], shp[2])
        else:
            out[n] = jnp.transpose(seg, (1, 2, 0, 3)).reshape(shp[0], shp[1], N_CHIPS * shp[2])
    return out


def _pack_grads(gfull, shard_shapes):
    segs = []
    for n in SHARDED:
        shp = shard_shapes[n]
        g = gfull[n]
        if n in ROW_SHARDED:
            seg = jnp.transpose(g.reshape(shp[0], N_CHIPS, shp[1], shp[2]), (1, 0, 2, 3))
        else:
            seg = jnp.transpose(g.reshape(shp[0], shp[1], N_CHIPS, shp[2]), (2, 0, 1, 3))
        segs.append(seg.reshape(N_CHIPS, -1))
    flat = jnp.concatenate(segs, axis=1)
    return _pad_rows(flat, PACK_ROW_ALIGN).reshape(N_CHIPS, -1, PACK_COLS)


def _unpack_shard_grads(red, shard_shapes):
    flat = red.reshape(-1)
    out, off = {}, 0
    for n in SHARDED:
        shp = shard_shapes[n]
        size = math.prod(shp)
        out[n] = flat[off:off + size].reshape(shp)
        off += size
    return out


def _pack_small(parts):
    flat = jnp.concatenate([p.astype(F32).reshape(-1) for p in parts])
    return _pad_rows(flat, 8).reshape(-1, PACK_COLS)


def _unpack_small(flat2d, shapes):
    flat = flat2d.reshape(-1)
    out, off = [], 0
    for shp in shapes:
        size = math.prod(shp)
        out.append(flat[off:off + size].reshape(shp))
        off += size
    return out


def _heads(a, bn, s, h):
    return jnp.transpose(a.reshape(bn, s, h, HEAD_DIM), (0, 2, 1, 3))


def _unheads(a):
    bn, h, s, d = a.shape
    return jnp.transpose(a, (0, 2, 1, 3)).reshape(bn * s, h * d)


def _reorder_w_in(w):
    d = w.shape[0]
    return jnp.concatenate([w[:, 2820:6916], w[:, 0:768], w[:, 772:1540], w[:, 1540:2052], w[:, 2052:2820],
                            w[:, 768:772], jnp.zeros((d, N_PROJ - 6916), w.dtype)], axis=1)


def _restore_dw_in(g):
    return jnp.concatenate([g[:, 4096:4864], g[:, 6912:6916], g[:, 4864:5632], g[:, 5632:6144], g[:, 6144:6912],
                            g[:, 0:4096]], axis=1)


def _ffn_fwd(h, g_pre, w_gu, w_down, g_post, tag):
    n = rms_fwd(h, g_pre, f"{tag}_rms")
    gu = _mm(n, w_gu, out_dtype=BF16, name=f"{tag}_mm_gu")
    a = swiglu_fwd(gu, f"{tag}_swiglu")
    f = _mm(a, w_down, out_dtype=F32, name=f"{tag}_mm_down")
    h_out = res_rms_fwd(h, f, g_post, 0.5, f"{tag}_res")
    return h_out, (h, n, gu, a, f)


def _ffn_bwd(dh, saved, g_pre, w_gu, w_down, g_post, tag, dw, n_gu, n_down):
    h, n, gu, a, f = saved
    df, dg_post = res_rms_bwd(f, g_post, dh, 0.5, f"{tag}_res_bwd")
    da = _mm(df, w_down, tb=True, out_dtype=BF16, name=f"{tag}_mm_da")
    dw(n_down, a, df, f"{tag}_mm_dwdown")
    dgu = swiglu_bwd(gu, da, f"{tag}_swiglu_bwd")
    dw(n_gu, n, dgu, f"{tag}_mm_dwgu")
    dn = _mm(dgu, w_gu, tb=True, out_dtype=BF16, name=f"{tag}_mm_dn")
    dh_in, dg_pre = rms_bwd(h, g_pre, dn, dh, f"{tag}_rms_bwd")
    return dh_in, dg_pre, dg_post


def kernel(x, p, ffn1_norm_pre, ffn1_w_gu, ffn1_w_down, ffn1_norm_post, mix_norm_pre, w_in, b_forget, b_gate, conv_short, conv_dw, conv_dw_bias, conv_ln_gain, conv_ln_bias, attn_sinks, rel_bias, w_br_a, w_br_b, w_br_c, w_br_d, w_o, mix_norm_post, ffn2_norm_pre, ffn2_w_gu, ffn2_w_down, ffn2_norm_post, ple_norm_gate, w_ple_gate, w_ple, ple_norm_post, loss_target, m_ffn1_norm_pre, m_ffn1_w_gu, m_ffn1_w_down, m_ffn1_norm_post, m_mix_norm_pre, m_w_in, m_b_forget, m_b_gate, m_conv_short, m_conv_dw, m_conv_dw_bias, m_conv_ln_gain, m_conv_ln_bias, m_attn_sinks, m_rel_bias, m_w_br_a, m_w_br_b, m_w_br_c, m_w_br_d, m_w_o, m_mix_norm_post, m_ffn2_norm_pre, m_ffn2_w_gu, m_ffn2_w_down, m_ffn2_norm_post, m_ple_norm_gate, m_w_ple_gate, m_w_ple, m_ple_norm_post, v_ffn1_norm_pre, v_ffn1_w_gu, v_ffn1_w_down, v_ffn1_norm_post, v_mix_norm_pre, v_w_in, v_b_forget, v_b_gate, v_conv_short, v_conv_dw, v_conv_dw_bias, v_conv_ln_gain, v_conv_ln_bias, v_attn_sinks, v_rel_bias, v_w_br_a, v_w_br_b, v_w_br_c, v_w_br_d, v_w_o, v_mix_norm_post, v_ffn2_norm_pre, v_ffn2_w_gu, v_ffn2_w_down, v_ffn2_norm_post, v_ple_norm_gate, v_w_ple_gate, v_w_ple, v_ple_norm_post):
    args = dict(locals())
    ws = {n: args[n] for n in WEIGHTS}
    ms = {n: args["m_" + n] for n in WEIGHTS}
    vs = {n: args["v_" + n] for n in WEIGHTS}
    return _step(x, p, loss_target, ws, ms, vs)


def _local(x, p, loss_target, ws, wf, w_short, w_dw):
    bn, s, d = x.shape
    t = bn * s
    depth = w_short.shape[0]

    def vec(a, i):
        return a[i].reshape(1, -1)

    bucket = _bucket_table()
    band_bias = band_bias_fwd(bucket, ws['rel_bias'], "band_bias")

    h = x.reshape(t, d)
    saved = []
    for i in range(depth):
        sv = {}
        h, sv['ffn1'] = _ffn_fwd(h, vec(ws['ffn1_norm_pre'], i), (wf['ffn1_w_gu'], i), (wf['ffn1_w_down'], i),
                                 vec(ws['ffn1_norm_post'], i), f"l{i}_ffn1")
        h1 = h
        u = rms_fwd(h1, vec(ws['mix_norm_pre'], i), f"l{i}_mix_rms")
        w_in_r = _reorder_w_in(wf['w_in'][i])
        proj = _mm(u, w_in_r, out_dtype=BF16, name=f"l{i}_mm_proj")
        bf = jnp.pad(vec(ws['b_forget'], i), ((0, 0), (0, 128 - A_HEADS)))
        cc = fgate_fwd(proj, bf, bn, s, f"l{i}_fgate")
        c4 = jnp.transpose(cc.reshape(bn, s, 128)[:, :, :A_HEADS], (0, 2, 1))
        c_col = c4[..., None]
        c_row = c4.reshape(bn, A_HEADS, s // FOX_T, 1, FOX_T)
        ya, oa, lse_a = fox2_fwd(proj, c_col, c_row, bn, s, f"l{i}_fox")
        w_sh = jnp.pad(w_short[i], ((0, 8 - w_short.shape[1]), (0, 0)))
        w_cv = jnp.pad(w_dw[i], ((0, 32 - w_dw.shape[1]), (0, 0)))
        yb = convb_fwd(proj, w_sh[:3], bn, s, f"l{i}_convb")
        cvec = (vec(ws['conv_dw_bias'], i), vec(ws['conv_ln_gain'], i), vec(ws['conv_ln_bias'], i))
        yc = convc_fwd(proj, w_cv[:31], *cvec, bn, s, f"l{i}_convc")
        sinks = jnp.broadcast_to(ws['attn_sinks'][i].reshape(D_Q_HEADS, 1, 1), (D_Q_HEADS, QB, 1))
        yd, lse_d = swa3_fwd(proj, band_bias, sinks, bn, s, f"l{i}_swa")
        ys = (ya, yb, yc, yd)
        wbr = ((wf['w_br_a'], i), (wf['w_br_b'], i), (wf['w_br_c'], i), (wf['w_br_d'], i))
        zs = [_mm(yk, wk, out_dtype=BF16, name=f"l{i}_mm_br{k}") for k, (yk, wk) in enumerate(zip(ys, wbr))]
        bgs = [ws['b_gate'][i, k * d:(k + 1) * d].reshape(1, d) for k in range(4)]
        merged = merge_fwd(proj, zs, bgs, f"l{i}_merge")
        mo = _mm(merged, (wf['w_o'], i), out_dtype=F32, name=f"l{i}_mm_o")
        h2 = res_rms_fwd(h1, mo, vec(ws['mix_norm_post'], i), 1.0, f"l{i}_mix_res")
        sv['mix'] = dict(h1=h1, u=u, proj=proj, w_in_r=w_in_r, bf=bf, c_col=c_col, c_row=c_row, oa=oa, lse_a=lse_a,
                         w_sh=w_sh, w_cv=w_cv, cvec=cvec, sinks=sinks, lse_d=lse_d, ys=ys, wbr=wbr, zs=zs, bgs=bgs,
                         merged=merged, mo=mo)
        h, sv['ffn2'] = _ffn_fwd(h2, vec(ws['ffn2_norm_pre'], i), (wf['ffn2_w_gu'], i), (wf['ffn2_w_down'], i),
                                 vec(ws['ffn2_norm_post'], i), f"l{i}_ffn2")
        h3 = h
        ng = rms_fwd(h3, vec(ws['ple_norm_gate'], i), f"l{i}_ple_rms")
        pgl = _mm(ng, (wf['w_ple_gate'], i), out_dtype=BF16, name=f"l{i}_mm_pgl")
        p_i = p[i].reshape(t, -1)
        pr = _mm(p_i, (wf['w_ple'], i), out_dtype=F32, name=f"l{i}_mm_pr")
        h = ple_fwd(h3, pgl, pr, vec(ws['ple_norm_post'], i), f"l{i}_ple")
        sv['ple'] = dict(h3=h3, ng=ng, pgl=pgl, p_i=p_i, pr=pr)
        saved.append(sv)

    dh, loss_vec = loss_fwd_bwd(h, loss_target.reshape(t, d), "loss")
    loss_part = jnp.sum(loss_vec)

    gst = {}
    gwin = [None] * depth

    def dw(n, a, b, nm):
        gst[n] = _mm(a, b, ta=True, name=nm, stack=(gst.get(n), depth, i))

    gsmall = {n: [None] * depth for n in REPLICATED + CONV_SHARDED if n != 'rel_bias'}
    dbias_layers = []
    for i in reversed(range(depth)):
        sv = saved[i]
        pv = sv['ple']
        dpgl, dpr, dg = ple_bwd(pv['pgl'], pv['pr'], vec(ws['ple_norm_post'], i), dh, f"l{i}_ple_bwd")
        gsmall['ple_norm_post'][i] = dg
        dw('w_ple', pv['p_i'], dpr, f"l{i}_mm_dwple")
        dw('w_ple_gate', pv['ng'], dpgl, f"l{i}_mm_dwpg")
        dng = _mm(dpgl, (wf['w_ple_gate'], i), tb=True, out_dtype=BF16, name=f"l{i}_mm_dng")
        dh, gsmall['ple_norm_gate'][i] = rms_bwd(pv['h3'], vec(ws['ple_norm_gate'], i), dng, dh,
                                                  f"l{i}_ple_rms_bwd")
        dh, gsmall['ffn2_norm_pre'][i], gsmall['ffn2_norm_post'][i] = _ffn_bwd(
            dh, sv['ffn2'], vec(ws['ffn2_norm_pre'], i), (wf['ffn2_w_gu'], i), (wf['ffn2_w_down'], i),
            vec(ws['ffn2_norm_post'], i), f"l{i}_ffn2", dw, 'ffn2_w_gu', 'ffn2_w_down')
        mv = sv['mix']
        dmo, gsmall['mix_norm_post'][i] = res_rms_bwd(mv['mo'], vec(ws['mix_norm_post'], i), dh, 1.0,
                                                      f"l{i}_mix_res_bwd")
        dw('w_o', mv['merged'], dmo, f"l{i}_mm_dwo")
        dmerged = _mm(dmo, (wf['w_o'], i), tb=True, out_dtype=BF16, name=f"l{i}_mm_dmerged")
        mb = merge_bwd(mv['proj'], mv['zs'], mv['bgs'], dmerged, f"l{i}_merge_bwd")
        dgates, dzs, dbg = mb[0:4], mb[4:8], mb[8:12]
        gsmall['b_gate'][i] = jnp.concatenate(dbg, axis=1)
        dys = []
        for k, nm in enumerate(('w_br_a', 'w_br_b', 'w_br_c', 'w_br_d')):
            dw(nm, mv['ys'][k], dzs[k], f"l{i}_mm_dwbr{k}")
            dys.append(_mm(dzs[k], mv['wbr'][k], tb=True, out_dtype=BF16, name=f"l{i}_mm_dy{k}"))
        dqa, dka, dva, dck, dcq = fox2_bwd(mv['proj'], mv['c_col'], mv['c_row'], mv['oa'], mv['lse_a'], dys[0], bn, s,
                                           f"l{i}_fox_bwd")
        dc = jnp.transpose(dck.reshape(bn, A_HEADS, s) + dcq.reshape(bn, A_HEADS, s), (0, 2, 1))
        dc = jnp.pad(dc, ((0, 0), (0, 0), (0, 128 - A_HEADS))).reshape(t, 128)
        daf, dbf = fgate_bwd(mv['proj'], mv['bf'], dc, bn, s, f"l{i}_fgate_bwd")
        gsmall['b_forget'][i] = dbf[:, :A_HEADS]
        dbg_, dcg_, dxb_, dwsh = convb_bwd(mv['proj'], mv['w_sh'][:3], dys[1], bn, s, f"l{i}_convb_bwd")
        gsmall['conv_short'][i] = dwsh[:3]
        dca, dcb, dwcv, dcbias, dlg, dlb = convc_bwd(mv['proj'], mv['w_cv'][:31], *mv['cvec'], dys[2], bn, s,
                                                     f"l{i}_convc_bwd")
        gsmall['conv_dw'][i] = dwcv[:31]
        gsmall['conv_dw_bias'][i] = dcbias
        gsmall['conv_ln_gain'][i] = dlg
        gsmall['conv_ln_bias'][i] = dlb
        dqd, dkd, dvd, dbias, dsink = swa3_bwd(mv['proj'], band_bias, mv['sinks'], mv['ys'][3], mv['lse_d'], dys[3],
                                               bn, s, f"l{i}_swa_bwd")
        dbias_layers.append(dbias)
        gsmall['attn_sinks'][i] = jnp.sum(dsink, axis=(1, 2))
        dproj = assemble_dproj(list(dgates) + [dqa, dka, dva, dbg_, dcg_, dxb_, dca, dcb, dqd], dkd, dvd, daf,
                               f"l{i}_dproj")
        dwin = _restore_dw_in(_mm(mv['u'], dproj, ta=True, name=f"l{i}_mm_dwin"))
        gwin[i] = jnp.transpose(dwin.reshape(d, N_CHIPS, -1), (1, 0, 2))
        du = _mm(dproj, mv['w_in_r'], tb=True, out_dtype=BF16, name=f"l{i}_mm_du")
        dh, gsmall['mix_norm_pre'][i] = rms_bwd(mv['h1'], vec(ws['mix_norm_pre'], i), du, dh, f"l{i}_mix_rms_bwd")
        dh, gsmall['ffn1_norm_pre'][i], gsmall['ffn1_norm_post'][i] = _ffn_bwd(
            dh, sv['ffn1'], vec(ws['ffn1_norm_pre'], i), (wf['ffn1_w_gu'], i), (wf['ffn1_w_down'], i),
            vec(ws['ffn1_norm_post'], i), f"l{i}_ffn1", dw, 'ffn1_w_gu', 'ffn1_w_down')
    grad_x = dh.reshape(bn, s, d)

    drel = band_bias_bwd(bucket, dbias_layers, "band_bias_bwd")
    gst['w_in'] = jnp.stack(gwin)
    full_shapes = {n: ws[n].shape for n in REPLICATED}
    full_shapes['conv_short'] = w_short.shape
    full_shapes['conv_dw'] = w_dw.shape
    gs = {n: jnp.stack([a.reshape(full_shapes[n][1:]) for a in gsmall[n]]) for n in gsmall}
    gs['rel_bias'] = jnp.transpose(drel[:, :, 0])
    return loss_part, grad_x, gst, gs


def _step(x, p, loss_target, ws, ms, vs):
    chip = 2 * lax.axis_index("x") + lax.axis_index("y")

    wf = gather_weights({n: ws[n].astype(BF16) for n in SHARDED}, "gather_weights")
    w_in_all = wf['w_in']
    wf['w_in'] = jnp.transpose(w_in_all, (1, 2, 0, 3)).reshape(w_in_all.shape[1], w_in_all.shape[2], -1)
    conv_shapes = [ws[n].shape for n in CONV_SHARDED]
    conv_all = gather_small(_pack_small([ws[n] for n in CONV_SHARDED]), "gather_conv")
    conv_full = []
    for idx, n in enumerate(CONV_SHARDED):
        per_chip = [_unpack_small(conv_all[2 * j], conv_shapes)[idx] for j in range(N_CHIPS)]
        conv_full.append(jnp.concatenate(per_chip, axis=-1))
    w_short, w_dw = conv_full

    loss_part, grad_x, gst, gs = _local(x, p, loss_target, {n: ws[n] for n in REPLICATED}, wf, w_short, w_dw)

    c_arr = lax.axis_index("c").astype(jnp.int32).reshape(1)
    recv = rs_sibling(gst, "rs_sibling")
    chip_sum = {n: rs_add(_kind(n), gst[n], recv[n], c_arr, f"rs_add_{n}") for n in SHARDED}
    slots = rs_chips2(chip_sum, "rs_chips")
    sc_arr = jnp.stack([chip, lax.axis_index("c")]).astype(jnp.int32)
    red_half = {n: rs_sum2(_kind(n), chip_sum[n], slots[n], sc_arr, f"rs_sum_{n}") for n in SHARDED}
    g_shard = rs_join2(red_half, "rs_join")

    small_names = [n for n in REPLICATED + CONV_SHARDED]
    small_parts = [gs[n] for n in small_names]
    small_shapes = [g.shape for g in small_parts]
    small_parts.append(loss_part.reshape(1))
    small_shapes.append((1,))
    small_all = gather_small(_pack_small(small_parts), "gather_small")
    small_red = sum_slots(small_all, "small_sum")
    small_g = _unpack_small(small_red, small_shapes)
    loss = small_g[-1].reshape(())
    g_small = dict(zip(small_names, small_g[:-1]))

    grads = {}
    for n in WEIGHTS:
        if n in SHARDED:
            grads[n] = g_shard[n]
        elif n in CONV_SHARDED:
            wdt = ws[n].shape[-1]
            grads[n] = lax.dynamic_slice_in_dim(g_small[n], chip * wdt, wdt, axis=2)
        else:
            grads[n] = g_small[n]

    deltas, new_m, new_v = {}, {}, {}
    small_upd = [n for n in WEIGHTS if n not in SHARDED]
    for n in SHARDED:
        deltas[n], new_m[n], new_v[n] = adamw(ws[n], grads[n], ms[n], vs[n], f"adamw_{n}")
    shapes_u = [ws[n].shape for n in small_upd]
    packs = [_pack_small([src[n] for n in small_upd]) for src in (ws, grads, ms, vs)]
    upd = adamw(*packs, "adamw_small")
    for res, dst in zip(upd, (deltas, new_m, new_v)):
        for n, a in zip(small_upd, _unpack_small(res, shapes_u)):
            dst[n] = a

    return (loss, grad_x, *[grads[n] for n in WEIGHTS], *[deltas[n] for n in WEIGHTS],
            *[new_m[n] for n in WEIGHTS], *[new_v[n] for n in WEIGHTS])
```

```python
import functools
import math

import jax
import jax.numpy as jnp
from jax import lax
from jax.experimental import pallas as pl
from jax.experimental.pallas import tpu as pltpu

F32 = jnp.float32
BF16 = jnp.bfloat16
MESH = pl.DeviceIdType.MESH

D_MODEL = 1024
DEPTH = 4
HEAD_DIM = 64
A_HEADS = 4
D_Q_HEADS = 8
D_KV_HEADS = 2
D_GROUP = 4
WINDOW = 128
QB = 128
REL_BUCKETS = 32
REL_MAX_DIST = 128
D_FF = 2816
EPS = 1e-6
NEG = -1e30
SCALE = HEAD_DIM ** -0.5
N_CHIPS = 4
N_DEV = 8

ADAM_LR = 0.001
ADAM_B1 = 0.9
ADAM_B2 = 0.999
ADAM_EPS = 1e-08
ADAM_WD = 0.01
ADAM_STEP = 10

C_GATE = 0
C_AQ, C_AK, C_AV = 4096, 4352, 4608
C_BG, C_CG, C_XB = 4864, 5120, 5376
C_CA, C_CB = 5632, 5888
C_DQ, C_DK, C_DV = 6144, 6656, 6784
C_AF = 6912
N_PROJ = 7168

VMEM_LIMIT = 56 * 1024 * 1024
PACK_COLS = 1024
PACK_ROW_ALIGN = 1024

SHARDED = ('ffn1_w_gu', 'ffn1_w_down', 'w_in', 'w_br_a', 'w_br_b', 'w_br_c', 'w_br_d', 'w_o',
           'ffn2_w_gu', 'ffn2_w_down', 'w_ple_gate', 'w_ple')
ROW_SHARDED = ('ffn1_w_down', 'w_o', 'ffn2_w_down', 'w_ple_gate')
CONV_SHARDED = ('conv_short', 'conv_dw')
REPLICATED = ('ffn1_norm_pre', 'ffn1_norm_post', 'mix_norm_pre', 'b_forget', 'b_gate', 'conv_dw_bias',
              'conv_ln_gain', 'conv_ln_bias', 'attn_sinks', 'rel_bias', 'mix_norm_post', 'ffn2_norm_pre',
              'ffn2_norm_post', 'ple_norm_gate', 'ple_norm_post')
WEIGHTS = ('ffn1_norm_pre', 'ffn1_w_gu', 'ffn1_w_down', 'ffn1_norm_post', 'mix_norm_pre', 'w_in', 'b_forget',
           'b_gate', 'conv_short', 'conv_dw', 'conv_dw_bias', 'conv_ln_gain', 'conv_ln_bias', 'attn_sinks',
           'rel_bias', 'w_br_a', 'w_br_b', 'w_br_c', 'w_br_d', 'w_o', 'mix_norm_post', 'ffn2_norm_pre',
           'ffn2_w_gu', 'ffn2_w_down', 'ffn2_norm_post', 'ple_norm_gate', 'w_ple_gate', 'w_ple', 'ple_norm_post')


def _cparams(sem=None):
    return pltpu.CompilerParams(dimension_semantics=sem, vmem_limit_bytes=VMEM_LIMIT)


def _pick(dim, cands):
    for c in cands:
        if dim % c == 0:
            return c
    return dim


MM_VMEM_BUDGET = 40 * 1024 * 1024
MXU_FLOPS = 9.0e14
HBM_BYTES_PER_S = 3.0e12
GRID_STEP_S = 0.35e-6


def _divisors(dim, cands):
    out = [c for c in cands if c <= dim and dim % c == 0]
    return out or [dim]


def _mm_tiles(m, n, k, ab, bb, ob):
    best = None
    for tm in _divisors(m, (2048, 1408, 1024, 512, 256, 128)):
        for tn in _divisors(n, (2816, 2048, 1792, 1408, 1024, 512, 256, 128)):
            for tk in _divisors(k, (k if k <= 2048 else 2816, 2816, 2048, 1792, 1408, 1024, 512, 256, 128)):
                nk = k // tk
                vmem = 2 * (tm * tk * ab + tk * tn * bb + tm * tn * ob) + tm * tn * 4 * (2 if nk > 1 else 1)
                if vmem > MM_VMEM_BUDGET:
                    continue
                steps = (m // tm) * (n // tn) * nk
                a_bytes = m * k * ab * (1 if nk == 1 else n // tn)
                b_bytes = k * n * bb * (1 if (nk == 1 and n == tn) else m // tm)
                mem = (a_bytes + b_bytes + m * n * ob) / HBM_BYTES_PER_S
                acc = steps * tm * tn * 1.5e-12 if nk > 1 else 0.0
                cost = steps * GRID_STEP_S + max(2.0 * m * n * k / MXU_FLOPS, mem) + acc
                if best is None or cost < best[0]:
                    best = (cost, tm, tn, tk)
    assert best is not None, (m, n, k)
    return best[1:]


def _mm(a, b, *, ta=False, tb=False, out_dtype=F32, name="mm", stack=None):
    b_layer = None
    if isinstance(b, tuple):
        b, b_layer = b
    if ta:
        kdim, m = a.shape
    else:
        m, kdim = a.shape
    if tb:
        n, kb = b.shape[-2:]
    else:
        kb, n = b.shape[-2:]
    assert kb == kdim, (a.shape, b.shape, ta, tb)
    tm, tn, tk = _mm_tiles(m, n, kdim, a.dtype.itemsize, b.dtype.itemsize, jnp.dtype(out_dtype).itemsize)
    nk = kdim // tk
    dims = (((0,) if ta else (1,), (1,) if tb else (0,)), ((), ()))

    def dot(a_ref, b_ref):
        return lax.dot_general(a_ref[...].astype(BF16), b_ref[...].astype(BF16), dims, preferred_element_type=F32)

    if nk == 1:
        def body(a_ref, b_ref, *rest):
            o_ref = rest[-1]
            o_ref[...] = dot(a_ref, b_ref).astype(o_ref.dtype)
        scratch = []
    else:
        def body(a_ref, b_ref, *rest):
            o_ref, acc_ref = rest[-2], rest[-1]
            k = pl.program_id(2)

            @pl.when(k == 0)
            def _():
                acc_ref[...] = dot(a_ref, b_ref)

            @pl.when(jnp.logical_and(k > 0, k < nk - 1))
            def _():
                acc_ref[...] += dot(a_ref, b_ref)

            @pl.when(k == nk - 1)
            def _():
                o_ref[...] = (acc_ref[...] + dot(a_ref, b_ref)).astype(o_ref.dtype)
        scratch = [pltpu.VMEM((tm, tn), F32)]

    a_spec = pl.BlockSpec((tk, tm), lambda i, j, k: (k, i)) if ta else pl.BlockSpec((tm, tk), lambda i, j, k: (i, k))
    if b_layer is None:
        b_spec = (pl.BlockSpec((tn, tk), lambda i, j, k: (j, k)) if tb
                  else pl.BlockSpec((tk, tn), lambda i, j, k: (k, j)))
    else:
        b_spec = (pl.BlockSpec((None, tn, tk), lambda i, j, k: (b_layer, j, k)) if tb
                  else pl.BlockSpec((None, tk, tn), lambda i, j, k: (b_layer, k, j)))
    in_specs, operands, aliases = [a_spec, b_spec], [a, b], {}
    if stack is None:
        out_spec = pl.BlockSpec((tm, tn), lambda i, j, k: (i, j))
        out_shape = jax.ShapeDtypeStruct((m, n), out_dtype)
    else:
        buf, depth, layer = stack
        out_spec = pl.BlockSpec((None, tm, tn), lambda i, j, k: (layer, i, j))
        out_shape = jax.ShapeDtypeStruct((depth, m, n), out_dtype)
        if buf is not None:
            in_specs.append(pl.BlockSpec(memory_space=pl.ANY))
            operands.append(buf)
            aliases = {2: 0}
    return pl.pallas_call(
        body, name=name, grid=(m // tm, n // tn, nk),
        in_specs=in_specs, out_specs=out_spec, out_shape=out_shape, scratch_shapes=scratch,
        input_output_aliases=aliases,
        compiler_params=_cparams(("parallel", "parallel", "arbitrary")),
    )(*operands)


def _rowwise(fn, rows, params, outs, pouts=(), *, tm=256, name="rowwise"):
    t = rows[0][0].shape[0]
    assert t % tm == 0
    n_r, n_p, n_o, n_po = len(rows), len(params), len(outs), len(pouts)

    def body(*refs):
        r_refs = refs[:n_r]
        p_refs = refs[n_r:n_r + n_p]
        o_refs = refs[n_r + n_p:n_r + n_p + n_o]
        po_refs = refs[n_r + n_p + n_o:]
        res = fn(*[r[...] for r in r_refs], *[p[...] for p in p_refs])
        if not isinstance(res, (tuple, list)):
            res = (res,)
        assert len(res) == n_o + n_po, (len(res), n_o, n_po)
        for o, val in zip(o_refs, res[:n_o]):
            o[...] = val.astype(o.dtype)
        if n_po:
            first = pl.program_id(0) == 0

            @pl.when(first)
            def _():
                for o, val in zip(po_refs, res[n_o:]):
                    o[...] = val.astype(F32)

            @pl.when(jnp.logical_not(first))
            def _():
                for o, val in zip(po_refs, res[n_o:]):
                    o[...] += val.astype(F32)

    in_specs = [pl.BlockSpec((tm, w), functools.partial(lambda i, cb: (i, cb), cb=cb)) for (_, w, cb) in rows]
    in_specs += [pl.BlockSpec(p.shape, lambda i: (0, 0)) for p in params]
    out_specs = [pl.BlockSpec((tm, w), lambda i: (i, 0)) for (w, _) in outs]
    out_specs += [pl.BlockSpec((1, w), lambda i: (0, 0)) for w in pouts]
    out_shape = [jax.ShapeDtypeStruct((t, w), dt) for (w, dt) in outs]
    out_shape += [jax.ShapeDtypeStruct((1, w), F32) for w in pouts]
    res = pl.pallas_call(
        body, name=name, grid=(t // tm,), in_specs=in_specs, out_specs=out_specs, out_shape=out_shape,
        compiler_params=_cparams(("arbitrary",)),
    )(*[r[0] for r in rows], *params)
    return res


def _full(a):
    return (a, a.shape[1], 0)


def _rms(x, g):
    x = x.astype(F32)
    return x * lax.rsqrt(jnp.mean(x * x, axis=-1, keepdims=True) + EPS) * g


def _sum0(v):
    return jnp.sum(v, axis=0, keepdims=True)


def rms_fwd(h, g, name):
    return _rowwise(lambda x, gg: _rms(x, gg), [_full(h)], [g], [(h.shape[1], BF16)], tm=512, name=name)[0]


def rms_bwd(h, g, dn, dres, name):
    def fn(x, d, r, gg):
        _, vjp = jax.vjp(_rms, x, gg)
        dx, dg = vjp(d.astype(F32))
        return dx + r, dg
    w = h.shape[1]
    return _rowwise(fn, [_full(h), _full(dn), _full(dres)], [g], [(w, F32)], [w], tm=512, name=name)


def res_rms_fwd(h, f, g, coef, name):
    return _rowwise(lambda x, y, gg: x + coef * _rms(y, gg), [_full(h), _full(f)], [g], [(h.shape[1], F32)],
                    tm=512, name=name)[0]


def res_rms_bwd(f, g, dh, coef, name):
    def fn(y, d, gg):
        _, vjp = jax.vjp(lambda a, b: coef * _rms(a, b), y, gg)
        dy, dg = vjp(d)
        return dy, dg
    w = f.shape[1]
    return _rowwise(fn, [_full(f), _full(dh)], [g], [(w, BF16)], [w], tm=512, name=name)


def swiglu_fwd(gu, name):
    f = gu.shape[1] // 2

    def fn(gate, up):
        gate = gate.astype(F32)
        return gate * jax.nn.sigmoid(gate) * up.astype(F32)
    return _rowwise(fn, [(gu, f, 0), (gu, f, 1)], [], [(f, BF16)], name=name)[0]


def swiglu_bwd(gu, da, name):
    t, f2 = gu.shape
    f = f2 // 2
    tm = 256

    def body(gate_ref, up_ref, da_ref, o_ref):
        gate = gate_ref[...].astype(F32)
        up = up_ref[...].astype(F32)
        d = da_ref[...].astype(F32)
        sg = jax.nn.sigmoid(gate)
        silu = gate * sg
        o_ref[:, :f] = (d * up * (sg + silu * (1.0 - sg))).astype(o_ref.dtype)
        o_ref[:, f:] = (d * silu).astype(o_ref.dtype)

    return pl.pallas_call(
        body, name=name, grid=(t // tm,),
        in_specs=[pl.BlockSpec((tm, f), lambda i: (i, 0)), pl.BlockSpec((tm, f), lambda i: (i, 1)),
                  pl.BlockSpec((tm, f), lambda i: (i, 0))],
        out_specs=pl.BlockSpec((tm, f2), lambda i: (i, 0)),
        out_shape=jax.ShapeDtypeStruct((t, f2), BF16),
        compiler_params=_cparams(("parallel",)),
    )(gu, gu, da)


def _merge(g0, g1, g2, g3, z0, z1, z2, z3, b0, b1, b2, b3):
    acc = jax.nn.sigmoid(g0.astype(F32) + b0) * z0.astype(F32)
    acc += jax.nn.sigmoid(g1.astype(F32) + b1) * z1.astype(F32)
    acc += jax.nn.sigmoid(g2.astype(F32) + b2) * z2.astype(F32)
    acc += jax.nn.sigmoid(g3.astype(F32) + b3) * z3.astype(F32)
    return acc


def merge_fwd(proj, zs, bs, name):
    rows = [(proj, D_MODEL, k) for k in range(4)] + [_full(z) for z in zs]
    return _rowwise(_merge, rows, list(bs), [(D_MODEL, BF16)], name=name)[0]


def merge_bwd(proj, zs, bs, dmerged, name):
    def fn(*args):
        d = args[8].astype(F32)
        prim = args[:8] + args[9:]
        _, vjp = jax.vjp(_merge, *prim)
        return vjp(d)
    rows = [(proj, D_MODEL, k) for k in range(4)] + [_full(z) for z in zs] + [_full(dmerged)]
    outs = [(D_MODEL, BF16)] * 8
    return _rowwise(fn, rows, list(bs), outs, [D_MODEL] * 4, name=name)


def _ple(pgl, pr, g):
    return jax.nn.sigmoid(pgl.astype(F32)) * _rms(pr, g)


def ple_fwd(h, pgl, pr, g, name):
    return _rowwise(lambda x, a, b, gg: x + _ple(a, b, gg), [_full(h), _full(pgl), _full(pr)], [g],
                    [(D_MODEL, F32)], tm=512, name=name)[0]


def ple_bwd(pgl, pr, g, dh, name):
    def fn(a, b, d, gg):
        _, vjp = jax.vjp(_ple, a, b, gg)
        return vjp(d)
    return _rowwise(fn, [_full(pgl), _full(pr), _full(dh)], [g], [(D_MODEL, BF16), (D_MODEL, BF16)], [D_MODEL],
                    tm=512, name=name)


def loss_fwd_bwd(y, target, name):
    def fn(a, b):
        err = a - b
        return err * (1.0 / D_MODEL), _sum0(err * err) * (0.5 / D_MODEL)
    return _rowwise(fn, [_full(y), _full(target)], [], [(D_MODEL, F32)], [D_MODEL], tm=512, name=name)


def _shift_down(x, d, row):
    if d == 0:
        return x
    return jnp.where(row >= d, pltpu.roll(x, d, 0), 0.0)


def _shift_up(x, d, row):
    if d == 0:
        return x
    s = x.shape[0]
    return jnp.where(row < s - d, pltpu.roll(x, s - d, 0), 0.0)


def fgate_fwd(proj, bf, bn, s, name):
    def body(a_ref, b_ref, o_ref):
        x = a_ref[...].astype(F32) + b_ref[...]
        c = jnp.minimum(x, 0.0) - jnp.log(1.0 + jnp.exp(-jnp.abs(x)))
        row = lax.broadcasted_iota(jnp.int32, c.shape, 0)
        sh = 1
        while sh < s:
            c = c + _shift_down(c, sh, row)
            sh *= 2
        o_ref[...] = c

    return pl.pallas_call(
        body, name=name, grid=(bn,),
        in_specs=[pl.BlockSpec((s, 128), lambda b: (b, C_AF // 128)), pl.BlockSpec((1, 128), lambda b: (0, 0))],
        out_specs=pl.BlockSpec((s, 128), lambda b: (b, 0)),
        out_shape=jax.ShapeDtypeStruct((bn * s, 128), F32),
        compiler_params=_cparams(("parallel",)),
    )(proj, bf)


def fgate_bwd(proj, bf, dc, bn, s, name):
    def body(a_ref, b_ref, dc_ref, da_ref, db_ref):
        x = a_ref[...].astype(F32) + b_ref[...]
        d = dc_ref[...]
        row = lax.broadcasted_iota(jnp.int32, d.shape, 0)
        sh = 1
        while sh < s:
            d = d + _shift_up(d, sh, row)
            sh *= 2
        da = d * jax.nn.sigmoid(-x)
        da_ref[...] = da.astype(da_ref.dtype)
        first = pl.program_id(0) == 0

        @pl.when(first)
        def _():
            db_ref[...] = _sum0(da)

        @pl.when(jnp.logical_not(first))
        def _():
            db_ref[...] += _sum0(da)

    return pl.pallas_call(
        body, name=name, grid=(bn,),
        in_specs=[pl.BlockSpec((s, 128), lambda b: (b, C_AF // 128)), pl.BlockSpec((1, 128), lambda b: (0, 0)),
                  pl.BlockSpec((s, 128), lambda b: (b, 0))],
        out_specs=[pl.BlockSpec((s, 128), lambda b: (b, 0)), pl.BlockSpec((1, 128), lambda b: (0, 0))],
        out_shape=[jax.ShapeDtypeStruct((bn * s, 128), BF16), jax.ShapeDtypeStruct((1, 128), F32)],
        compiler_params=_cparams(("arbitrary",)),
    )(proj, bf, dc)


FOX_T = 512


def _fox_scores(q, k, cq, ck, j, i):
    t = FOX_T
    s = lax.dot_general(q, k, (((1,), (1,)), ((), ())), preferred_element_type=F32) * SCALE
    qpos = j * t + lax.broadcasted_iota(jnp.int32, (t, t), 0)
    kpos = i * t + lax.broadcasted_iota(jnp.int32, (t, t), 1)
    return jnp.where(qpos >= kpos, s + (cq - ck), NEG)


def fox_fwd(q, k, v, c_col, c_row, name):
    bn, h, s, d = q.shape
    t = FOX_T
    nq = s // t

    def body(q_ref, k_ref, v_ref, cq_ref, ck_ref, o_ref, lse_ref):
        j = pl.program_id(2)
        qv = q_ref[...]
        cq = cq_ref[...]

        def step(i, carry):
            m, l, acc = carry
            ks = pl.multiple_of(i * t, t)
            kc = k_ref[pl.ds(ks, t), :]
            vc = v_ref[pl.ds(ks, t), :]
            sc = _fox_scores(qv, kc, cq, ck_ref[i], j, i)
            m_new = jnp.maximum(m, jnp.max(sc, axis=-1, keepdims=True))
            alpha = jnp.exp(m - m_new)
            p = jnp.exp(sc - m_new)
            l = alpha * l + jnp.sum(p, axis=-1, keepdims=True)
            acc = alpha * acc + jnp.dot(p.astype(BF16), vc, preferred_element_type=F32)
            return m_new, l, acc

        init = (jnp.full((t, 1), NEG, F32), jnp.zeros((t, 1), F32), jnp.zeros((t, d), F32))
        m, l, acc = lax.fori_loop(0, j + 1, step, init)
        o_ref[...] = (acc / l).astype(o_ref.dtype)
        lse_ref[...] = m + jnp.log(l)

    blk_q = pl.BlockSpec((None, None, t, d), lambda b, hh, j: (b, hh, j, 0))
    blk_kv = pl.BlockSpec((None, None, s, d), lambda b, hh, j: (b, hh, 0, 0))
    blk_c1 = pl.BlockSpec((None, None, t, 1), lambda b, hh, j: (b, hh, j, 0))
    blk_cr = pl.BlockSpec((None, None, nq, 1, t), lambda b, hh, j: (b, hh, 0, 0, 0))
    return pl.pallas_call(
        body, name=name, grid=(bn, h, nq),
        in_specs=[blk_q, blk_kv, blk_kv, blk_c1, blk_cr],
        out_specs=[blk_q, blk_c1],
        out_shape=[jax.ShapeDtypeStruct((bn, h, s, d), F32), jax.ShapeDtypeStruct((bn, h, s, 1), F32)],
        compiler_params=_cparams(("parallel", "parallel", "arbitrary")),
    )(q, k, v, c_col, c_row)


def fox_bwd(q, k, v, c_col, c_row, o, lse, do, name):
    bn, h, s, d = q.shape
    t = FOX_T
    nq = s // t

    def body(q_ref, k_ref, v_ref, cq_ref, ck_ref, o_ref, lse_ref, do_ref, dq_ref, dk_ref, dv_ref, dck_ref,
             dcq_ref):
        j = pl.program_id(2)

        @pl.when(j == 0)
        def _():
            dk_ref[...] = jnp.zeros_like(dk_ref)
            dv_ref[...] = jnp.zeros_like(dv_ref)
            dck_ref[...] = jnp.zeros_like(dck_ref)

        qv = q_ref[...]
        cq = cq_ref[...]
        dov = do_ref[...]
        lse = lse_ref[...]
        delta = jnp.sum(dov.astype(F32) * o_ref[...].astype(F32), axis=-1, keepdims=True)

        def step(i, carry):
            dq, dcq = carry
            ks = pl.multiple_of(i * t, t)
            kc = k_ref[pl.ds(ks, t), :]
            vc = v_ref[pl.ds(ks, t), :]
            sc = _fox_scores(qv, kc, cq, ck_ref[i], j, i)
            p = jnp.exp(sc - lse)
            dp = lax.dot_general(dov, vc, (((1,), (1,)), ((), ())), preferred_element_type=F32)
            ds = p * (dp - delta)
            dsb = ds.astype(BF16)
            dq = dq + jnp.dot(dsb, kc, preferred_element_type=F32) * SCALE
            dk_ref[pl.ds(ks, t), :] += lax.dot_general(dsb, qv, (((0,), (0,)), ((), ())),
                                                       preferred_element_type=F32) * SCALE
            dv_ref[pl.ds(ks, t), :] += lax.dot_general(p.astype(BF16), dov, (((0,), (0,)), ((), ())),
                                                       preferred_element_type=F32)
            dck_ref[i] += -_sum0(ds)
            return dq, dcq + jnp.sum(ds, axis=-1, keepdims=True)

        dq, dcq = lax.fori_loop(0, j + 1, step, (jnp.zeros((t, d), F32), jnp.zeros((t, 1), F32)))
        dq_ref[...] = dq
        dcq_ref[...] = dcq

    blk_q = pl.BlockSpec((None, None, t, d), lambda b, hh, j: (b, hh, j, 0))
    blk_kv = pl.BlockSpec((None, None, s, d), lambda b, hh, j: (b, hh, 0, 0))
    blk_c1 = pl.BlockSpec((None, None, t, 1), lambda b, hh, j: (b, hh, j, 0))
    blk_cr = pl.BlockSpec((None, None, nq, 1, t), lambda b, hh, j: (b, hh, 0, 0, 0))
    return pl.pallas_call(
        body, name=name, grid=(bn, h, nq),
        in_specs=[blk_q, blk_kv, blk_kv, blk_c1, blk_cr, blk_q, blk_c1, blk_q],
        out_specs=[blk_q, blk_kv, blk_kv, blk_cr, blk_c1],
        out_shape=[jax.ShapeDtypeStruct((bn, h, s, d), F32), jax.ShapeDtypeStruct((bn, h, s, d), F32),
                   jax.ShapeDtypeStruct((bn, h, s, d), F32), jax.ShapeDtypeStruct((bn, h, nq, 1, t), F32),
                   jax.ShapeDtypeStruct((bn, h, s, 1), F32)],
        compiler_params=_cparams(("parallel", "parallel", "arbitrary")),
    )(q, k, v, c_col, c_row, o, lse, do)


def _swa_valid(n):
    qi = lax.broadcasted_iota(jnp.int32, (QB, 2 * QB), 0)
    kj = lax.broadcasted_iota(jnp.int32, (QB, 2 * QB), 1)
    dist = qi + QB - kj
    return (dist >= 0) & (dist < WINDOW) & ((kj >= QB) | (n > 0))


def _swa_band(ref, n):
    qs = pl.multiple_of(n * QB, QB)
    ps = pl.multiple_of(jnp.maximum(n - 1, 0) * QB, QB)
    return jnp.concatenate([ref[pl.ds(ps, QB), :], ref[pl.ds(qs, QB), :]], axis=0), qs, ps


def swa_fwd(q, k, v, bias, sinks, name):
    bn, hq, s, d = q.shape
    nb = s // QB

    def body(q_ref, k_ref, v_ref, b_ref, s_ref, o_ref, lse_ref):
        def step(n, _):
            kb, qs, _ps = _swa_band(k_ref, n)
            vb, _, _ = _swa_band(v_ref, n)
            valid = _swa_valid(n)
            for g in range(D_GROUP):
                qg = q_ref[g, pl.ds(qs, QB), :]
                sc = lax.dot_general(qg, kb, (((1,), (1,)), ((), ())), preferred_element_type=F32) * SCALE
                sc = jnp.where(valid, sc + b_ref[g], NEG)
                sink = s_ref[g]
                m = jnp.maximum(jnp.max(sc, axis=-1, keepdims=True), sink)
                e = jnp.exp(sc - m)
                z = jnp.sum(e, axis=-1, keepdims=True) + jnp.exp(sink - m)
                p = e / z
                o_ref[g, pl.ds(qs, QB), :] = jnp.dot(p.astype(BF16), vb, preferred_element_type=F32
                                                     ).astype(o_ref.dtype)
                lse_ref[g, pl.ds(qs, QB), :] = m + jnp.log(z)
            return 0

        lax.fori_loop(0, nb, step, 0, unroll=2)

    blk_q = pl.BlockSpec((None, D_GROUP, s, d), lambda b, kh: (b, kh, 0, 0))
    blk_kv = pl.BlockSpec((None, None, s, d), lambda b, kh: (b, kh, 0, 0))
    blk_l = pl.BlockSpec((None, D_GROUP, s, 1), lambda b, kh: (b, kh, 0, 0))
    return pl.pallas_call(
        body, name=name, grid=(bn, D_KV_HEADS),
        in_specs=[blk_q, blk_kv, blk_kv, pl.BlockSpec((D_GROUP, QB, 2 * QB), lambda b, kh: (kh, 0, 0)),
                  pl.BlockSpec((D_GROUP, QB, 1), lambda b, kh: (kh, 0, 0))],
        out_specs=[blk_q, blk_l],
        out_shape=[jax.ShapeDtypeStruct((bn, hq, s, d), BF16), jax.ShapeDtypeStruct((bn, hq, s, 1), F32)],
        compiler_params=_cparams(("parallel", "parallel")),
    )(q, k, v, bias, sinks)


def swa_bwd(q, k, v, bias, sinks, o, lse, do, name):
    bn, hq, s, d = q.shape
    nb = s // QB

    def body(q_ref, k_ref, v_ref, b_ref, s_ref, o_ref, lse_ref, do_ref, dq_ref, dk_ref, dv_ref, db_ref, dsk_ref):
        @pl.when(pl.program_id(1) == 0)
        def _():
            db_ref[...] = jnp.zeros_like(db_ref)
            dsk_ref[...] = jnp.zeros_like(dsk_ref)

        dk_ref[...] = jnp.zeros_like(dk_ref)
        dv_ref[...] = jnp.zeros_like(dv_ref)

        def step(n, _):
            kb, qs, ps = _swa_band(k_ref, n)
            vb, _, _ = _swa_band(v_ref, n)
            valid = _swa_valid(n)
            dkb = jnp.zeros((2 * QB, d), F32)
            dvb = jnp.zeros((2 * QB, d), F32)
            for g in range(D_GROUP):
                qg = q_ref[g, pl.ds(qs, QB), :]
                dog = do_ref[g, pl.ds(qs, QB), :]
                og = o_ref[g, pl.ds(qs, QB), :]
                lse = lse_ref[g, pl.ds(qs, QB), :]
                sc = lax.dot_general(qg, kb, (((1,), (1,)), ((), ())), preferred_element_type=F32) * SCALE
                sc = jnp.where(valid, sc + b_ref[g], NEG)
                p = jnp.exp(sc - lse)
                delta = jnp.sum(dog.astype(F32) * og.astype(F32), axis=-1, keepdims=True)
                dp = lax.dot_general(dog, vb, (((1,), (1,)), ((), ())), preferred_element_type=F32)
                ds = p * (dp - delta)
                dsb = ds.astype(BF16)
                dq_ref[g, pl.ds(qs, QB), :] = jnp.dot(dsb, kb, preferred_element_type=F32) * SCALE
                dkb = dkb + lax.dot_general(dsb, qg, (((0,), (0,)), ((), ())), preferred_element_type=F32) * SCALE
                dvb = dvb + lax.dot_general(p.astype(BF16), dog, (((0,), (0,)), ((), ())),
                                            preferred_element_type=F32)
                db_ref[g] += ds
                dsk_ref[g] += -jnp.exp(s_ref[g] - lse) * delta
            dk_ref[pl.ds(ps, QB), :] += dkb[:QB]
            dk_ref[pl.ds(qs, QB), :] += dkb[QB:]
            dv_ref[pl.ds(ps, QB), :] += dvb[:QB]
            dv_ref[pl.ds(qs, QB), :] += dvb[QB:]
            return 0

        lax.fori_loop(0, nb, step, 0, unroll=2)

    blk_q = pl.BlockSpec((None, D_GROUP, s, d), lambda kh, b: (b, kh, 0, 0))
    blk_kv = pl.BlockSpec((None, None, s, d), lambda kh, b: (b, kh, 0, 0))
    blk_l = pl.BlockSpec((None, D_GROUP, s, 1), lambda kh, b: (b, kh, 0, 0))
    blk_b = pl.BlockSpec((D_GROUP, QB, 2 * QB), lambda kh, b: (kh, 0, 0))
    blk_s = pl.BlockSpec((D_GROUP, QB, 1), lambda kh, b: (kh, 0, 0))
    return pl.pallas_call(
        body, name=name, grid=(D_KV_HEADS, bn),
        in_specs=[blk_q, blk_kv, blk_kv, blk_b, blk_s, blk_q, blk_l, blk_q],
        out_specs=[blk_q, blk_kv, blk_kv, blk_b, blk_s],
        out_shape=[jax.ShapeDtypeStruct((bn, hq, s, d), F32), jax.ShapeDtypeStruct((bn, D_KV_HEADS, s, d), F32),
                   jax.ShapeDtypeStruct((bn, D_KV_HEADS, s, d), F32),
                   jax.ShapeDtypeStruct((hq, QB, 2 * QB), F32), jax.ShapeDtypeStruct((hq, QB, 1), F32)],
        compiler_params=_cparams(("parallel", "arbitrary")),
    )(q, k, v, bias, sinks, o, lse, do)


def _sel(nh, width):
    r = lax.broadcasted_iota(jnp.int32, (width, HEAD_DIM), 0)
    c = lax.broadcasted_iota(jnp.int32, (width, HEAD_DIM), 1)
    return [(r == c + HEAD_DIM * h).astype(BF16) for h in range(nh)]


def _pick_head(x, e):
    return jnp.dot(x, e, preferred_element_type=F32).astype(BF16)


def _place_head(x, e):
    return lax.dot_general(x.astype(BF16), e, (((1,), (1,)), ((), ())), preferred_element_type=F32)


def fox2_fwd(proj, c_col, c_row, bn, s, name):
    t = FOX_T
    nq = s // t
    nh, d = A_HEADS, HEAD_DIM

    def body(q_ref, k_ref, v_ref, cq_ref, ck_ref, y_ref, o_ref, lse_ref, kh_ref, vh_ref):
        j = pl.program_id(1)
        es = _sel(nh, 256)

        @pl.when(j == 0)
        def _():
            for h in range(nh):
                kh_ref[h] = _pick_head(k_ref[...], es[h])
                vh_ref[h] = _pick_head(v_ref[...], es[h])

        q4 = q_ref[...]
        qs = [_pick_scaled(q4, es[h]) for h in range(nh)]
        cqs = [cq_ref[h] for h in range(nh)]

        def chunk(i, carry, masked):
            ks = pl.multiple_of(i * t, t)
            out = []
            for h in range(nh):
                m, l, acc = carry[h]
                sc = _fox_scores2(qs[h], kh_ref[h, pl.ds(ks, t), :], cqs[h], ck_ref[h, i], masked)
                m_new = jnp.maximum(m, jnp.max(sc, axis=-1, keepdims=True))
                alpha = jnp.exp(m - m_new)
                p = jnp.exp(sc - m_new)
                l = alpha * l + jnp.sum(p, axis=-1, keepdims=True)
                acc = alpha * acc + jnp.dot(p.astype(BF16), vh_ref[h, pl.ds(ks, t), :], preferred_element_type=F32)
                out.append((m_new, l, acc))
            return tuple(out)

        init = tuple((jnp.full((t, 1), NEG, F32), jnp.zeros((t, 1), F32), jnp.zeros((t, d), F32)) for _ in range(nh))
        res = lax.fori_loop(0, j, lambda i, carry: chunk(i, carry, False), init)
        res = chunk(j, res, True)
        y = jnp.zeros((t, 256), F32)
        for h in range(nh):
            m, l, acc = res[h]
            o = acc / l
            o_ref[h] = o
            lse_ref[h] = m + jnp.log(l)
            y = y + _place_head(o, es[h])
        y_ref[...] = y.astype(y_ref.dtype)

    blk_q = pl.BlockSpec((t, 256), lambda b, j: (b * nq + j, C_AQ // 256))
    blk_k = pl.BlockSpec((s, 256), lambda b, j: (b, C_AK // 256))
    blk_v = pl.BlockSpec((s, 256), lambda b, j: (b, C_AV // 256))
    blk_c1 = pl.BlockSpec((None, nh, t, 1), lambda b, j: (b, 0, j, 0))
    blk_cr = pl.BlockSpec((None, nh, nq, 1, t), lambda b, j: (b, 0, 0, 0, 0))
    blk_o = pl.BlockSpec((None, nh, t, d), lambda b, j: (b, 0, j, 0))
    return pl.pallas_call(
        body, name=name, grid=(bn, nq),
        in_specs=[blk_q, blk_k, blk_v, blk_c1, blk_cr],
        out_specs=[pl.BlockSpec((t, 256), lambda b, j: (b * nq + j, 0)), blk_o, blk_c1],
        out_shape=[jax.ShapeDtypeStruct((bn * s, 256), BF16), jax.ShapeDtypeStruct((bn, nh, s, d), F32),
                   jax.ShapeDtypeStruct((bn, nh, s, 1), F32)],
        scratch_shapes=[pltpu.VMEM((nh, s, d), BF16), pltpu.VMEM((nh, s, d), BF16)],
        compiler_params=_cparams(("arbitrary", "arbitrary")),
    )(proj, proj, proj, c_col, c_row)


def fox2_bwd(proj, c_col, c_row, o, lse, dya, bn, s, name):
    t = FOX_T
    nq = s // t
    nh, d = A_HEADS, HEAD_DIM

    def body(q_ref, k_ref, v_ref, cq_ref, ck_ref, o_ref, lse_ref, dy_ref, dq_ref, dk_ref, dv_ref, dck_ref, dcq_ref,
             kh_ref, vh_ref, dkh_ref, dvh_ref):
        j = pl.program_id(1)
        es = _sel(nh, 256)

        @pl.when(j == 0)
        def _():
            for h in range(nh):
                kh_ref[h] = _pick_head(k_ref[...], es[h])
                vh_ref[h] = _pick_head(v_ref[...], es[h])
            dkh_ref[...] = jnp.zeros_like(dkh_ref)
            dvh_ref[...] = jnp.zeros_like(dvh_ref)
            dck_ref[...] = jnp.zeros_like(dck_ref)

        q4 = q_ref[...]
        dy4 = dy_ref[...]
        qs = [_pick_scaled(q4, es[h]) for h in range(nh)]
        dos = [_pick_head(dy4, es[h]) for h in range(nh)]
        cqs = [cq_ref[h] for h in range(nh)]
        lses = [lse_ref[h] for h in range(nh)]
        deltas = [jnp.sum(dos[h].astype(F32) * o_ref[h], axis=-1, keepdims=True) for h in range(nh)]

        def chunk(i, carry, masked):
            ks = pl.multiple_of(i * t, t)
            out = []
            for h in range(nh):
                dq, dcq = carry[h]
                kc = kh_ref[h, pl.ds(ks, t), :]
                sc = _fox_scores2(qs[h], kc, cqs[h], ck_ref[h, i], masked)
                p = jnp.exp(sc - lses[h])
                dp = lax.dot_general(dos[h], vh_ref[h, pl.ds(ks, t), :], (((1,), (1,)), ((), ())),
                                     preferred_element_type=F32)
                ds = p * (dp - deltas[h])
                dsb = ds.astype(BF16)
                dq = dq + jnp.dot(dsb, kc, preferred_element_type=F32)
                dkh_ref[h, pl.ds(ks, t), :] += lax.dot_general(dsb, qs[h], (((0,), (0,)), ((), ())),
                                                               preferred_element_type=F32)
                dvh_ref[h, pl.ds(ks, t), :] += lax.dot_general(p.astype(BF16), dos[h], (((0,), (0,)), ((), ())),
                                                               preferred_element_type=F32)
                dck_ref[h, i] += -_sum0(ds)
                out.append((dq, dcq + jnp.sum(ds, axis=-1, keepdims=True)))
            return tuple(out)

        init = tuple((jnp.zeros((t, d), F32), jnp.zeros((t, 1), F32)) for _ in range(nh))
        res = lax.fori_loop(0, j, lambda i, carry: chunk(i, carry, False), init)
        res = chunk(j, res, True)
        dq4 = jnp.zeros((t, 256), F32)
        for h in range(nh):
            dq4 = dq4 + _place_head(res[h][0] * SCALE, es[h])
            dcq_ref[h] = res[h][1]
        dq_ref[...] = dq4.astype(dq_ref.dtype)

        @pl.when(j == nq - 1)
        def _():
            dk4 = jnp.zeros((s, 256), F32)
            dv4 = jnp.zeros((s, 256), F32)
            for h in range(nh):
                dk4 = dk4 + _place_head(dkh_ref[h], es[h])
                dv4 = dv4 + _place_head(dvh_ref[h], es[h])
            dk_ref[...] = dk4.astype(dk_ref.dtype)
            dv_ref[...] = dv4.astype(dv_ref.dtype)

    blk_q = pl.BlockSpec((t, 256), lambda b, j: (b * nq + j, C_AQ // 256))
    blk_k = pl.BlockSpec((s, 256), lambda b, j: (b, C_AK // 256))
    blk_v = pl.BlockSpec((s, 256), lambda b, j: (b, C_AV // 256))
    blk_c1 = pl.BlockSpec((None, nh, t, 1), lambda b, j: (b, 0, j, 0))
    blk_cr = pl.BlockSpec((None, nh, nq, 1, t), lambda b, j: (b, 0, 0, 0, 0))
    blk_o = pl.BlockSpec((None, nh, t, d), lambda b, j: (b, 0, j, 0))
    blk_t = pl.BlockSpec((t, 256), lambda b, j: (b * nq + j, 0))
    blk_s = pl.BlockSpec((s, 256), lambda b, j: (b, 0))
    return pl.pallas_call(
        body, name=name, grid=(bn, nq),
        in_specs=[blk_q, blk_k, blk_v, blk_c1, blk_cr, blk_o, blk_c1, blk_t],
        out_specs=[blk_t, blk_s, blk_s, blk_cr, blk_c1],
        out_shape=[jax.ShapeDtypeStruct((bn * s, 256), BF16)] * 3
        + [jax.ShapeDtypeStruct((bn, nh, nq, 1, t), F32), jax.ShapeDtypeStruct((bn, nh, s, 1), F32)],
        scratch_shapes=[pltpu.VMEM((nh, s, d), BF16), pltpu.VMEM((nh, s, d), BF16),
                        pltpu.VMEM((nh, s, d), F32), pltpu.VMEM((nh, s, d), F32)],
        compiler_params=_cparams(("arbitrary", "arbitrary")),
    )(proj, proj, proj, c_col, c_row, o, lse, dya)


def _band3(ref, h, n):
    qs = pl.multiple_of(n * QB, QB)
    ps = pl.multiple_of(jnp.maximum(n - 1, 0) * QB, QB)
    return jnp.concatenate([ref[h, pl.ds(ps, QB), :], ref[h, pl.ds(qs, QB), :]], axis=0), qs, ps


def swa2_fwd(proj, bias, sinks, bn, s, name):
    nb = s // QB
    d = HEAD_DIM

    def body(q0_ref, q1_ref, k_ref, v_ref, b_ref, s_ref, y_ref, lse_ref, qh_ref, kh_ref, vh_ref):
        e4 = _sel(D_GROUP, 256)
        e2 = _sel(D_KV_HEADS, 128)
        for kh, q_ref in enumerate((q0_ref, q1_ref)):
            kh_ref[kh] = _pick_head(k_ref[...], e2[kh])
            vh_ref[kh] = _pick_head(v_ref[...], e2[kh])
            for g in range(D_GROUP):
                qh_ref[D_GROUP * kh + g] = _pick_head(q_ref[...], e4[g])

        def step(n, _):
            valid = _swa_valid(n)
            for kh in range(D_KV_HEADS):
                kb, qs, _ps = _band3(kh_ref, kh, n)
                vb, _, _ = _band3(vh_ref, kh, n)
                y = jnp.zeros((QB, 256), F32)
                for g in range(D_GROUP):
                    hh = D_GROUP * kh + g
                    qg = qh_ref[hh, pl.ds(qs, QB), :]
                    sc = lax.dot_general(qg, kb, (((1,), (1,)), ((), ())), preferred_element_type=F32) * SCALE
                    sc = jnp.where(valid, sc + b_ref[hh], NEG)
                    sink = s_ref[hh]
                    m = jnp.maximum(jnp.max(sc, axis=-1, keepdims=True), sink)
                    e = jnp.exp(sc - m)
                    z = jnp.sum(e, axis=-1, keepdims=True) + jnp.exp(sink - m)
                    o = jnp.dot((e / z).astype(BF16), vb, preferred_element_type=F32)
                    lse_ref[hh, pl.ds(qs, QB), :] = m + jnp.log(z)
                    y = y + _place_head(o, e4[g])
                y_ref[pl.ds(qs, QB), 256 * kh:256 * (kh + 1)] = y.astype(y_ref.dtype)
            return 0

        lax.fori_loop(0, nb, step, 0, unroll=2)

    return pl.pallas_call(
        body, name=name, grid=(bn,),
        in_specs=[pl.BlockSpec((s, 256), lambda b: (b, C_DQ // 256)), pl.BlockSpec((s, 256), lambda b: (b, C_DQ // 256 + 1)),
                  pl.BlockSpec((s, 128), lambda b: (b, C_DK // 128)), pl.BlockSpec((s, 128), lambda b: (b, C_DV // 128)),
                  pl.BlockSpec((D_Q_HEADS, QB, 2 * QB), lambda b: (0, 0, 0)),
                  pl.BlockSpec((D_Q_HEADS, QB, 1), lambda b: (0, 0, 0))],
        out_specs=[pl.BlockSpec((s, 512), lambda b: (b, 0)), pl.BlockSpec((None, D_Q_HEADS, s, 1), lambda b: (b, 0, 0, 0))],
        out_shape=[jax.ShapeDtypeStruct((bn * s, 512), BF16), jax.ShapeDtypeStruct((bn, D_Q_HEADS, s, 1), F32)],
        scratch_shapes=[pltpu.VMEM((D_Q_HEADS, s, d), BF16), pltpu.VMEM((D_KV_HEADS, s, d), BF16),
                        pltpu.VMEM((D_KV_HEADS, s, d), BF16)],
        compiler_params=_cparams(("parallel",)),
    )(proj, proj, proj, proj, bias, sinks)


def swa2_bwd(proj, bias, sinks, yd, lse, dyd, bn, s, name):
    nb = s // QB
    d = HEAD_DIM

    def body(q0_ref, q1_ref, k_ref, v_ref, b_ref, s_ref, y_ref, lse_ref, dy_ref, dq_ref, dk_ref, dv_ref, db_ref,
             dsk_ref, qh_ref, kh_ref, vh_ref, oh_ref, doh_ref, dkh_ref, dvh_ref):
        @pl.when(pl.program_id(0) == 0)
        def _():
            db_ref[...] = jnp.zeros_like(db_ref)
            dsk_ref[...] = jnp.zeros_like(dsk_ref)

        e4 = _sel(D_GROUP, 256)
        e2 = _sel(D_KV_HEADS, 128)
        for kh, q_ref in enumerate((q0_ref, q1_ref)):
            kh_ref[kh] = _pick_head(k_ref[...], e2[kh])
            vh_ref[kh] = _pick_head(v_ref[...], e2[kh])
            for g in range(D_GROUP):
                hh = D_GROUP * kh + g
                qh_ref[hh] = _pick_head(q_ref[...], e4[g])
                oh_ref[hh] = _pick_head(y_ref[:, 256 * kh:256 * (kh + 1)], e4[g])
                doh_ref[hh] = _pick_head(dy_ref[:, 256 * kh:256 * (kh + 1)], e4[g])
        dkh_ref[...] = jnp.zeros_like(dkh_ref)
        dvh_ref[...] = jnp.zeros_like(dvh_ref)

        def step(n, _):
            valid = _swa_valid(n)
            for kh in range(D_KV_HEADS):
                kb, qs, ps = _band3(kh_ref, kh, n)
                vb, _, _ = _band3(vh_ref, kh, n)
                dkb = jnp.zeros((2 * QB, d), F32)
                dvb = jnp.zeros((2 * QB, d), F32)
                dq4 = jnp.zeros((QB, 256), F32)
                for g in range(D_GROUP):
                    hh = D_GROUP * kh + g
                    qg = qh_ref[hh, pl.ds(qs, QB), :]
                    dog = doh_ref[hh, pl.ds(qs, QB), :]
                    og = oh_ref[hh, pl.ds(qs, QB), :]
                    lse = lse_ref[hh, pl.ds(qs, QB), :]
                    sc = lax.dot_general(qg, kb, (((1,), (1,)), ((), ())), preferred_element_type=F32) * SCALE
                    sc = jnp.where(valid, sc + b_ref[hh], NEG)
                    p = jnp.exp(sc - lse)
                    delta = jnp.sum(dog.astype(F32) * og.astype(F32), axis=-1, keepdims=True)
                    dp = lax.dot_general(dog, vb, (((1,), (1,)), ((), ())), preferred_element_type=F32)
                    ds = p * (dp - delta)
                    dsb = ds.astype(BF16)
                    dq4 = dq4 + _place_head(jnp.dot(dsb, kb, preferred_element_type=F32) * SCALE, e4[g])
                    dkb = dkb + lax.dot_general(dsb, qg, (((0,), (0,)), ((), ())),
                                                preferred_element_type=F32) * SCALE
                    dvb = dvb + lax.dot_general(p.astype(BF16), dog, (((0,), (0,)), ((), ())),
                                                preferred_element_type=F32)
                    db_ref[hh] += ds
                    dsk_ref[hh] += -jnp.exp(s_ref[hh] - lse) * delta
                dq_ref[pl.ds(qs, QB), 256 * kh:256 * (kh + 1)] = dq4.astype(dq_ref.dtype)
                dkh_ref[kh, pl.ds(ps, QB), :] += dkb[:QB]
                dkh_ref[kh, pl.ds(qs, QB), :] += dkb[QB:]
                dvh_ref[kh, pl.ds(ps, QB), :] += dvb[:QB]
                dvh_ref[kh, pl.ds(qs, QB), :] += dvb[QB:]
            return 0

        lax.fori_loop(0, nb, step, 0, unroll=2)
        dk2 = jnp.zeros((s, 128), F32)
        dv2 = jnp.zeros((s, 128), F32)
        for kh in range(D_KV_HEADS):
            dk2 = dk2 + _place_head(dkh_ref[kh], e2[kh])
            dv2 = dv2 + _place_head(dvh_ref[kh], e2[kh])
        dk_ref[...] = dk2.astype(dk_ref.dtype)
        dv_ref[...] = dv2.astype(dv_ref.dtype)

    blk512 = pl.BlockSpec((s, 512), lambda b: (b, 0))
    blk128 = pl.BlockSpec((s, 128), lambda b: (b, 0))
    blk_b = pl.BlockSpec((D_Q_HEADS, QB, 2 * QB), lambda b: (0, 0, 0))
    blk_s = pl.BlockSpec((D_Q_HEADS, QB, 1), lambda b: (0, 0, 0))
    return pl.pallas_call(
        body, name=name, grid=(bn,),
        in_specs=[pl.BlockSpec((s, 256), lambda b: (b, C_DQ // 256)), pl.BlockSpec((s, 256), lambda b: (b, C_DQ // 256 + 1)),
                  pl.BlockSpec((s, 128), lambda b: (b, C_DK // 128)), pl.BlockSpec((s, 128), lambda b: (b, C_DV // 128)),
                  blk_b, blk_s, blk512, pl.BlockSpec((None, D_Q_HEADS, s, 1), lambda b: (b, 0, 0, 0)), blk512],
        out_specs=[blk512, blk128, blk128, blk_b, blk_s],
        out_shape=[jax.ShapeDtypeStruct((bn * s, 512), BF16), jax.ShapeDtypeStruct((bn * s, 128), BF16),
                   jax.ShapeDtypeStruct((bn * s, 128), BF16),
                   jax.ShapeDtypeStruct((D_Q_HEADS, QB, 2 * QB), F32), jax.ShapeDtypeStruct((D_Q_HEADS, QB, 1), F32)],
        scratch_shapes=[pltpu.VMEM((D_Q_HEADS, s, d), BF16), pltpu.VMEM((D_KV_HEADS, s, d), BF16),
                        pltpu.VMEM((D_KV_HEADS, s, d), BF16), pltpu.VMEM((D_Q_HEADS, s, d), BF16),
                        pltpu.VMEM((D_Q_HEADS, s, d), BF16), pltpu.VMEM((D_KV_HEADS, s, d), F32),
                        pltpu.VMEM((D_KV_HEADS, s, d), F32)],
        compiler_params=_cparams(("arbitrary",)),
    )(proj, proj, proj, proj, bias, sinks, yd, lse, dyd)


def _pick_scaled(x, e):
    return (jnp.dot(x, e, preferred_element_type=F32) * SCALE).astype(BF16)


def _swa_valid4(n):
    qi = lax.broadcasted_iota(jnp.int32, (D_GROUP * QB, 2 * QB), 0) & (QB - 1)
    kj = lax.broadcasted_iota(jnp.int32, (D_GROUP * QB, 2 * QB), 1)
    dist = qi + QB - kj
    return (dist >= 0) & (dist < WINDOW) & ((kj >= QB) | (n > 0))


def _fox_scores2(q, k, cq, ck, masked):
    t = FOX_T
    s = lax.dot_general(q, k, (((1,), (1,)), ((), ())), preferred_element_type=F32) + (cq - ck)
    if masked:
        keep = lax.broadcasted_iota(jnp.int32, (t, t), 0) >= lax.broadcasted_iota(jnp.int32, (t, t), 1)
        s = jnp.where(keep, s, NEG)
    return s


def _sel_at(off, width):
    r = lax.broadcasted_iota(jnp.int32, (width, HEAD_DIM), 0)
    c = lax.broadcasted_iota(jnp.int32, (width, HEAD_DIM), 1)
    return (r == c + off).astype(BF16)


def _eye(n):
    return lax.broadcasted_iota(jnp.int32, (n, n), 0) == lax.broadcasted_iota(jnp.int32, (n, n), 1)


def _row_to_col(row, eye):
    return jnp.sum(jnp.where(eye, row, 0.0), axis=1, keepdims=True)


def _col_to_row(col, eye):
    return jnp.sum(jnp.where(eye, col, 0.0), axis=0, keepdims=True)


def swa3_fwd(proj, bias, sinks, bn, s, name):
    nb = s // QB
    d = HEAD_DIM

    def body(q_ref, k_ref, v_ref, b_ref, s_ref, y_ref, lse_ref, qh_ref, kh_ref, vh_ref):
        kh = pl.program_id(0)
        e4 = _sel(D_GROUP, 256)
        ek = _sel_at(HEAD_DIM * kh, 128)
        eye = _eye(QB)
        kh_ref[...] = _pick_head(k_ref[...], ek)
        vh_ref[...] = _pick_head(v_ref[...], ek)
        for g in range(D_GROUP):
            qh_ref[g] = _pick_scaled(q_ref[...], e4[g])
        bias4 = b_ref[...].reshape(D_GROUP * QB, 2 * QB)
        sink4 = s_ref[...].reshape(D_GROUP * QB, 1)

        def step(n, _):
            valid = _swa_valid4(n)
            kb, qs, _ps = _swa_band(kh_ref, n)
            vb, _, _ = _swa_band(vh_ref, n)
            q4 = jnp.concatenate([qh_ref[g, pl.ds(qs, QB), :] for g in range(D_GROUP)], axis=0)
            sc = lax.dot_general(q4, kb, (((1,), (1,)), ((), ())), preferred_element_type=F32)
            sc = jnp.where(valid, sc + bias4, NEG)
            m = jnp.maximum(jnp.max(sc, axis=-1, keepdims=True), sink4)
            e = jnp.exp(sc - m)
            z = jnp.sum(e, axis=-1, keepdims=True) + jnp.exp(sink4 - m)
            o4 = jnp.dot((e / z).astype(BF16), vb, preferred_element_type=F32)
            lse4 = m + jnp.log(z)
            y = jnp.zeros((QB, 256), F32)
            for g in range(D_GROUP):
                lse_ref[g, n] = _col_to_row(lse4[g * QB:(g + 1) * QB], eye)
                y = y + _place_head(o4[g * QB:(g + 1) * QB], e4[g])
            y_ref[pl.ds(qs, QB), :] = y.astype(y_ref.dtype)
            return 0

        lax.fori_loop(0, nb, step, 0, unroll=2)

    return pl.pallas_call(
        body, name=name, grid=(D_KV_HEADS, bn),
        in_specs=[pl.BlockSpec((s, 256), lambda kh, b: (b, C_DQ // 256 + kh)),
                  pl.BlockSpec((s, 128), lambda kh, b: (b, C_DK // 128)),
                  pl.BlockSpec((s, 128), lambda kh, b: (b, C_DV // 128)),
                  pl.BlockSpec((D_GROUP, QB, 2 * QB), lambda kh, b: (kh, 0, 0)),
                  pl.BlockSpec((D_GROUP, QB, 1), lambda kh, b: (kh, 0, 0))],
        out_specs=[pl.BlockSpec((s, 256), lambda kh, b: (b, kh)),
                   pl.BlockSpec((None, D_GROUP, nb, 1, QB), lambda kh, b: (b, kh, 0, 0, 0))],
        out_shape=[jax.ShapeDtypeStruct((bn * s, 512), BF16), jax.ShapeDtypeStruct((bn, D_Q_HEADS, nb, 1, QB), F32)],
        scratch_shapes=[pltpu.VMEM((D_GROUP, s, d), BF16), pltpu.VMEM((s, d), BF16), pltpu.VMEM((s, d), BF16)],
        compiler_params=_cparams(("parallel", "parallel")),
    )(proj, proj, proj, bias, sinks)


def swa3_bwd(proj, bias, sinks, yd, lse, dyd, bn, s, name):
    nb = s // QB
    d = HEAD_DIM

    def body(q_ref, k_ref, v_ref, b_ref, s_ref, y_ref, lse_ref, dy_ref, dq_ref, dk_ref, dv_ref, db_ref, dsk_ref,
             qh_ref, kh_ref, vh_ref, oh_ref, doh_ref, dkh_ref, dvh_ref):
        kh = pl.program_id(0)

        @pl.when(pl.program_id(1) == 0)
        def _():
            db_ref[...] = jnp.zeros_like(db_ref)
            dsk_ref[...] = jnp.zeros_like(dsk_ref)

        e4 = _sel(D_GROUP, 256)
        ek = _sel_at(HEAD_DIM * kh, 128)
        eye = _eye(QB)
        kh_ref[...] = _pick_head(k_ref[...], ek)
        vh_ref[...] = _pick_head(v_ref[...], ek)
        for g in range(D_GROUP):
            qh_ref[g] = _pick_scaled(q_ref[...], e4[g])
            oh_ref[g] = _pick_head(y_ref[...], e4[g])
            doh_ref[g] = _pick_head(dy_ref[...], e4[g])
        dkh_ref[...] = jnp.zeros_like(dkh_ref)
        dvh_ref[...] = jnp.zeros_like(dvh_ref)
        bias4 = b_ref[...].reshape(D_GROUP * QB, 2 * QB)
        sink4 = s_ref[...].reshape(D_GROUP * QB, 1)

        def stack(ref, qs):
            return jnp.concatenate([ref[g, pl.ds(qs, QB), :] for g in range(D_GROUP)], axis=0)

        def step(n, _):
            valid = _swa_valid4(n)
            kb, qs, ps = _swa_band(kh_ref, n)
            vb, _, _ = _swa_band(vh_ref, n)
            q4, do4, o4 = stack(qh_ref, qs), stack(doh_ref, qs), stack(oh_ref, qs)
            lse4 = jnp.concatenate([_row_to_col(lse_ref[g, n], eye) for g in range(D_GROUP)], axis=0)
            sc = lax.dot_general(q4, kb, (((1,), (1,)), ((), ())), preferred_element_type=F32)
            sc = jnp.where(valid, sc + bias4, NEG)
            p = jnp.exp(sc - lse4)
            delta = jnp.sum(do4.astype(F32) * o4.astype(F32), axis=-1, keepdims=True)
            dp = lax.dot_general(do4, vb, (((1,), (1,)), ((), ())), preferred_element_type=F32)
            ds = p * (dp - delta)
            dsb = ds.astype(BF16)
            dq4s = jnp.dot(dsb, kb, preferred_element_type=F32) * SCALE
            dkb = lax.dot_general(dsb, q4, (((0,), (0,)), ((), ())), preferred_element_type=F32)
            dvb = lax.dot_general(p.astype(BF16), do4, (((0,), (0,)), ((), ())), preferred_element_type=F32)
            db_ref[...] += ds.reshape(D_GROUP, QB, 2 * QB)
            dsk_ref[...] += (-jnp.exp(sink4 - lse4) * delta).reshape(D_GROUP, QB, 1)
            dq4 = jnp.zeros((QB, 256), F32)
            for g in range(D_GROUP):
                dq4 = dq4 + _place_head(dq4s[g * QB:(g + 1) * QB], e4[g])
            dq_ref[pl.ds(qs, QB), :] = dq4.astype(dq_ref.dtype)
            dkh_ref[pl.ds(ps, QB), :] += dkb[:QB]
            dkh_ref[pl.ds(qs, QB), :] += dkb[QB:]
            dvh_ref[pl.ds(ps, QB), :] += dvb[:QB]
            dvh_ref[pl.ds(qs, QB), :] += dvb[QB:]
            return 0

        lax.fori_loop(0, nb, step, 0, unroll=2)
        dk_ref[...] = dkh_ref[...].astype(dk_ref.dtype)
        dv_ref[...] = dvh_ref[...].astype(dv_ref.dtype)

    blk256 = pl.BlockSpec((s, 256), lambda kh, b: (b, kh))
    blk_kv = pl.BlockSpec((None, s, d), lambda kh, b: (kh, b, 0))
    blk_b = pl.BlockSpec((D_GROUP, QB, 2 * QB), lambda kh, b: (kh, 0, 0))
    blk_s = pl.BlockSpec((D_GROUP, QB, 1), lambda kh, b: (kh, 0, 0))
    return pl.pallas_call(
        body, name=name, grid=(D_KV_HEADS, bn),
        in_specs=[pl.BlockSpec((s, 256), lambda kh, b: (b, C_DQ // 256 + kh)),
                  pl.BlockSpec((s, 128), lambda kh, b: (b, C_DK // 128)),
                  pl.BlockSpec((s, 128), lambda kh, b: (b, C_DV // 128)),
                  blk_b, blk_s, blk256,
                  pl.BlockSpec((None, D_GROUP, nb, 1, QB), lambda kh, b: (b, kh, 0, 0, 0)), blk256],
        out_specs=[blk256, blk_kv, blk_kv, blk_b, blk_s],
        out_shape=[jax.ShapeDtypeStruct((bn * s, 512), BF16), jax.ShapeDtypeStruct((D_KV_HEADS, bn * s, d), BF16),
                   jax.ShapeDtypeStruct((D_KV_HEADS, bn * s, d), BF16),
                   jax.ShapeDtypeStruct((D_Q_HEADS, QB, 2 * QB), F32), jax.ShapeDtypeStruct((D_Q_HEADS, QB, 1), F32)],
        scratch_shapes=[pltpu.VMEM((D_GROUP, s, d), BF16), pltpu.VMEM((s, d), BF16), pltpu.VMEM((s, d), BF16),
                        pltpu.VMEM((D_GROUP, s, d), BF16), pltpu.VMEM((D_GROUP, s, d), BF16),
                        pltpu.VMEM((s, d), F32), pltpu.VMEM((s, d), F32)],
        compiler_params=_cparams(("parallel", "arbitrary")),
    )(proj, proj, proj, bias, sinks, yd, lse, dyd)


def assemble_dproj(pieces, dk, dv, daf, name):
    t = pieces[0].shape[0]
    tm = 512
    widths = [p.shape[1] for p in pieces]
    npc = len(pieces)
    assert sum(widths) == C_DK and all(w % 128 == 0 for w in widths)

    def body(*refs):
        p_refs = refs[:npc]
        dk_ref, dv_ref, af_ref, o_ref = refs[npc:]
        off = 0
        for r, w in zip(p_refs, widths):
            o_ref[:, off:off + w] = r[...]
            off += w
        e2 = _sel(D_KV_HEADS, 128)
        for r in (dk_ref, dv_ref):
            val = _place_head(r[0], e2[0]) + _place_head(r[1], e2[1])
            o_ref[:, off:off + 128] = val.astype(o_ref.dtype)
            off += 128
        o_ref[:, off:off + 128] = af_ref[...]
        off += 128
        o_ref[:, off:] = jnp.zeros((tm, N_PROJ - off), o_ref.dtype)

    kv_blk = pl.BlockSpec((D_KV_HEADS, tm, HEAD_DIM), lambda i: (0, i, 0))
    return pl.pallas_call(
        body, name=name, grid=(t // tm,),
        in_specs=[pl.BlockSpec((tm, w), lambda i: (i, 0)) for w in widths]
        + [kv_blk, kv_blk, pl.BlockSpec((tm, 128), lambda i: (i, 0))],
        out_specs=pl.BlockSpec((tm, N_PROJ), lambda i: (i, 0)),
        out_shape=jax.ShapeDtypeStruct((t, N_PROJ), BF16),
        compiler_params=_cparams(("parallel",)),
    )(*pieces, dk, dv, daf)


def _bucket_table():
    dist = jnp.maximum(jnp.arange(QB)[:, None] + QB - jnp.arange(2 * QB)[None, :], 0)
    max_exact = REL_BUCKETS // 2
    large = max_exact + (jnp.log(jnp.maximum(dist, 1).astype(F32) / max_exact)
                         / math.log(REL_MAX_DIST / max_exact) * (REL_BUCKETS - max_exact)).astype(jnp.int32)
    large = jnp.minimum(large, REL_BUCKETS - 1)
    return jnp.where(dist < max_exact, dist, large).astype(F32)


def band_bias_fwd(bucket, rel_bias, name):
    def body(bk_ref, rel_ref, o_ref):
        bk = bk_ref[...]
        for hh in range(D_Q_HEADS):
            acc = jnp.zeros(bk.shape, F32)
            for b in range(REL_BUCKETS):
                acc = jnp.where(bk == float(b), rel_ref[b, hh], acc)
            o_ref[hh] = acc

    return pl.pallas_call(
        body, name=name,
        in_specs=[pl.BlockSpec(memory_space=pltpu.VMEM), pl.BlockSpec(memory_space=pltpu.SMEM)],
        out_specs=pl.BlockSpec(memory_space=pltpu.VMEM),
        out_shape=jax.ShapeDtypeStruct((D_Q_HEADS, QB, 2 * QB), F32),
    )(bucket, rel_bias)


def band_bias_bwd(bucket, dbias_layers, name):
    nl = len(dbias_layers)

    def body(bk_ref, *refs):
        o_ref = refs[nl]
        bk = bk_ref[...]
        for hh in range(D_Q_HEADS):
            tot = refs[0][hh]
            for r in refs[1:nl]:
                tot = tot + r[hh]
            for b in range(REL_BUCKETS):
                part = jnp.sum(jnp.where(bk == float(b), tot, 0.0), axis=0, keepdims=True)
                val = jnp.sum(part, axis=1, keepdims=True)
                o_ref[hh, b:b + 1, :] = jnp.broadcast_to(val, (1, 128))

    return pl.pallas_call(
        body, name=name,
        in_specs=[pl.BlockSpec(memory_space=pltpu.VMEM)] * (nl + 1),
        out_specs=pl.BlockSpec(memory_space=pltpu.VMEM),
        out_shape=jax.ShapeDtypeStruct((D_Q_HEADS, REL_BUCKETS, 128), F32),
    )(bucket, *dbias_layers)


def _proj_blk(s, col):
    return pl.BlockSpec((s, 256), functools.partial(lambda b, cb: (b, cb), cb=col // 256))


def convb_fwd(proj, w, bn, s, name):
    kk = w.shape[0]

    def body(bg_ref, cg_ref, xb_ref, w_ref, o_ref):
        x = cg_ref[...].astype(F32) * xb_ref[...].astype(F32)
        row = lax.broadcasted_iota(jnp.int32, x.shape, 0)
        y = jnp.zeros_like(x)
        for k in range(kk):
            y = y + w_ref[k:k + 1, :] * _shift_down(x, kk - 1 - k, row)
        o_ref[...] = (bg_ref[...].astype(F32) * y).astype(o_ref.dtype)

    return pl.pallas_call(
        body, name=name, grid=(bn,),
        in_specs=[_proj_blk(s, C_BG), _proj_blk(s, C_CG), _proj_blk(s, C_XB), pl.BlockSpec(w.shape, lambda b: (0, 0))],
        out_specs=pl.BlockSpec((s, 256), lambda b: (b, 0)),
        out_shape=jax.ShapeDtypeStruct((bn * s, 256), BF16),
        compiler_params=_cparams(("parallel",)),
    )(proj, proj, proj, w)


def convb_bwd(proj, w, dyb, bn, s, name):
    kk = w.shape[0]

    def body(bg_ref, cg_ref, xb_ref, w_ref, d_ref, dbg_ref, dcg_ref, dxb_ref, dw_ref):
        @pl.when(pl.program_id(0) == 0)
        def _():
            dw_ref[...] = jnp.zeros_like(dw_ref)

        cg = cg_ref[...].astype(F32)
        xb = xb_ref[...].astype(F32)
        d = d_ref[...].astype(F32)
        x = cg * xb
        row = lax.broadcasted_iota(jnp.int32, x.shape, 0)
        dy = d * bg_ref[...].astype(F32)
        y = jnp.zeros_like(x)
        dx = jnp.zeros_like(x)
        for k in range(kk):
            xs = _shift_down(x, kk - 1 - k, row)
            y = y + w_ref[k:k + 1, :] * xs
            dx = dx + w_ref[k:k + 1, :] * _shift_up(dy, kk - 1 - k, row)
            dw_ref[k:k + 1, :] += _sum0(dy * xs)
        dbg_ref[...] = (d * y).astype(dbg_ref.dtype)
        dcg_ref[...] = (dx * xb).astype(dcg_ref.dtype)
        dxb_ref[...] = (dx * cg).astype(dxb_ref.dtype)

    blk = pl.BlockSpec((s, 256), lambda b: (b, 0))
    return pl.pallas_call(
        body, name=name, grid=(bn,),
        in_specs=[_proj_blk(s, C_BG), _proj_blk(s, C_CG), _proj_blk(s, C_XB), pl.BlockSpec(w.shape, lambda b: (0, 0)),
                  blk],
        out_specs=[blk, blk, blk, pl.BlockSpec((8, 256), lambda b: (0, 0))],
        out_shape=[jax.ShapeDtypeStruct((bn * s, 256), BF16)] * 3 + [jax.ShapeDtypeStruct((8, 256), F32)],
        compiler_params=_cparams(("arbitrary",)),
    )(proj, proj, proj, w, dyb)


def _convc_core(ca, cb, w_ref, bias, kk, row, y=None):
    sg = jax.nn.sigmoid(cb)
    glu = ca * sg
    if y is None:
        y = jnp.zeros_like(glu)
        for k in range(kk):
            y = y + w_ref[k:k + 1, :] * _shift_down(glu, kk - 1 - k, row)
        y = y + bias
    mu = jnp.mean(y, axis=-1, keepdims=True)
    xc = y - mu
    r = lax.rsqrt(jnp.mean(xc * xc, axis=-1, keepdims=True) + EPS)
    return sg, glu, xc * r, r, y


def convc_fwd(proj, w, bias, gain, lbias, bn, s, name):
    kk = w.shape[0]

    def body(ca_ref, cb_ref, w_ref, b_ref, g_ref, lb_ref, o_ref, y_ref):
        ca = ca_ref[...].astype(F32)
        row = lax.broadcasted_iota(jnp.int32, ca.shape, 0)
        _, _, xh, _, y = _convc_core(ca, cb_ref[...].astype(F32), w_ref, b_ref[...], kk, row)
        ln = xh * g_ref[...] + lb_ref[...]
        o_ref[...] = (ln * jax.nn.sigmoid(ln)).astype(o_ref.dtype)
        y_ref[...] = y

    vec = pl.BlockSpec((1, 256), lambda b: (0, 0))
    blk = pl.BlockSpec((s, 256), lambda b: (b, 0))
    return pl.pallas_call(
        body, name=name, grid=(bn,),
        in_specs=[_proj_blk(s, C_CA), _proj_blk(s, C_CB), pl.BlockSpec(w.shape, lambda b: (0, 0)), vec, vec, vec],
        out_specs=[blk, blk],
        out_shape=[jax.ShapeDtypeStruct((bn * s, 256), BF16), jax.ShapeDtypeStruct((bn * s, 256), F32)],
        compiler_params=_cparams(("parallel",)),
    )(proj, proj, w, bias, gain, lbias)


def convc_bwd(proj, w, bias, gain, lbias, yconv, dyc, bn, s, name):
    kk = w.shape[0]

    def body(ca_ref, cb_ref, w_ref, b_ref, g_ref, lb_ref, y_ref, d_ref, dca_ref, dcb_ref, dw_ref, db_ref, dg_ref,
             dlb_ref):
        @pl.when(pl.program_id(0) == 0)
        def _():
            dw_ref[...] = jnp.zeros_like(dw_ref)
            db_ref[...] = jnp.zeros_like(db_ref)
            dg_ref[...] = jnp.zeros_like(dg_ref)
            dlb_ref[...] = jnp.zeros_like(dlb_ref)

        ca = ca_ref[...].astype(F32)
        row = lax.broadcasted_iota(jnp.int32, ca.shape, 0)
        sg, glu, xh, r, _ = _convc_core(ca, cb_ref[...].astype(F32), w_ref, b_ref[...], kk, row, y=y_ref[...])
        ln = xh * g_ref[...] + lb_ref[...]
        sl = jax.nn.sigmoid(ln)
        dl = d_ref[...].astype(F32) * (sl + ln * sl * (1.0 - sl))
        dg_ref[...] += _sum0(dl * xh)
        dlb_ref[...] += _sum0(dl)
        dxh = dl * g_ref[...]
        dy = r * (dxh - jnp.mean(dxh, axis=-1, keepdims=True) - xh * jnp.mean(dxh * xh, axis=-1, keepdims=True))
        db_ref[...] += _sum0(dy)
        dglu = jnp.zeros_like(glu)
        for k in range(kk):
            dw_ref[k:k + 1, :] += _sum0(dy * _shift_down(glu, kk - 1 - k, row))
            dglu = dglu + w_ref[k:k + 1, :] * _shift_up(dy, kk - 1 - k, row)
        dca_ref[...] = (dglu * sg).astype(dca_ref.dtype)
        dcb_ref[...] = (dglu * ca * sg * (1.0 - sg)).astype(dcb_ref.dtype)

    vec = pl.BlockSpec((1, 256), lambda b: (0, 0))
    blk = pl.BlockSpec((s, 256), lambda b: (b, 0))
    return pl.pallas_call(
        body, name=name, grid=(bn,),
        in_specs=[_proj_blk(s, C_CA), _proj_blk(s, C_CB), pl.BlockSpec(w.shape, lambda b: (0, 0)), vec, vec, vec, blk,
                  blk],
        out_specs=[blk, blk, pl.BlockSpec((32, 256), lambda b: (0, 0)), vec, vec, vec],
        out_shape=[jax.ShapeDtypeStruct((bn * s, 256), BF16)] * 2 + [jax.ShapeDtypeStruct((32, 256), F32)]
        + [jax.ShapeDtypeStruct((1, 256), F32)] * 3,
        compiler_params=_cparams(("arbitrary",)),
    )(proj, proj, w, bias, gain, lbias, yconv, dyc)


def adamw(w, g, m, v, name):
    shape = w.shape
    cols = shape[-1]
    rows = w.size // cols
    tr = _pick(rows, (256, 128, 64, 32, 16, 8))

    def body(w_ref, g_ref, m_ref, v_ref, d_ref, nm_ref, nv_ref):
        gg = g_ref[...]
        mm = ADAM_B1 * m_ref[...] + (1.0 - ADAM_B1) * gg
        vv = ADAM_B2 * v_ref[...] + (1.0 - ADAM_B2) * jnp.square(gg)
        m_hat = mm / (1.0 - ADAM_B1 ** ADAM_STEP)
        v_hat = vv / (1.0 - ADAM_B2 ** ADAM_STEP)
        d_ref[...] = -ADAM_LR * (m_hat / (jnp.sqrt(v_hat) + ADAM_EPS) + ADAM_WD * w_ref[...])
        nm_ref[...] = mm
        nv_ref[...] = vv

    blk = pl.BlockSpec((tr, cols), lambda i: (i, 0))
    outs = pl.pallas_call(
        body, name=name, grid=(rows // tr,), in_specs=[blk] * 4, out_specs=[blk] * 3,
        out_shape=[jax.ShapeDtypeStruct((rows, cols), F32)] * 3,
        compiler_params=_cparams(("parallel",)),
    )(*[a.reshape(rows, cols) for a in (w, g, m, v)])
    return [o.reshape(shape) for o in outs]


def add_halves(own, recv, name):
    n, r, c = own.shape
    tr = _pick(r, (512, 256, 128, 64, 32, 16, 8))
    blk = pl.BlockSpec((None, tr, c), lambda i, j: (i, j, 0))

    def body(a_ref, b_ref, o_ref):
        o_ref[...] = a_ref[...] + b_ref[...]

    return pl.pallas_call(
        body, name=name, grid=(n, r // tr), in_specs=[blk, blk], out_specs=blk,
        out_shape=jax.ShapeDtypeStruct((n, r, c), F32), compiler_params=_cparams(("parallel", "parallel")),
    )(own, recv)


def sum_slots(slots, name):
    n, r, c = slots.shape
    tr = _pick(r, (512, 256, 128, 64, 32, 16, 8))

    def body(a_ref, o_ref):
        acc = a_ref[0]
        for k in range(1, n):
            acc = acc + a_ref[k]
        o_ref[...] = acc

    return pl.pallas_call(
        body, name=name, grid=(r // tr,), in_specs=[pl.BlockSpec((n, tr, c), lambda j: (0, j, 0))],
        out_specs=pl.BlockSpec((tr, c), lambda j: (j, 0)),
        out_shape=jax.ShapeDtypeStruct((r, c), F32), compiler_params=_cparams(("parallel",)),
    )(slots)


ANY = pl.BlockSpec(memory_space=pl.ANY)


def _place():
    x, y, c = lax.axis_index("x"), lax.axis_index("y"), lax.axis_index("c")
    return x, y, c


def gather_shards(pack, name):
    r, cols = pack.shape
    half = r // 2

    def body(src_ref, out_ref, send_sems, recv_sems, local_sem):
        x, y, c = _place()
        sibling = (x, y, 1 - c)
        chips = [(1 - x, y), (x, 1 - y), (1 - x, 1 - y)]

        def rows(px, py, pc):
            return out_ref.at[2 * px + py, pl.ds(pc * half, half), :]

        mine = pltpu.make_async_copy(src_ref, out_ref.at[2 * x + y], local_sem)
        mine.start()

        def copy(k, blk, to, src=None):
            return pltpu.make_async_remote_copy(
                src_ref=rows(*blk) if src is None else src, dst_ref=rows(*blk),
                send_sem=send_sems.at[k], recv_sem=recv_sems.at[k], device_id=to, device_id_type=MESH)

        first = [copy(j, (x, y, c), (*chip, c), src=src_ref.at[pl.ds(c * half, half), :])
                 for j, chip in enumerate(chips)]
        for cp in first:
            cp.start()
        passed = [copy(3 + j, (*chip, c), sibling) for j, chip in enumerate(chips)]
        for j, chip in enumerate(chips):
            copy(j, (*chip, c), (x, y, c)).wait_recv()
            passed[j].start()
        for j, chip in enumerate(chips):
            copy(3 + j, (*chip, 1 - c), (x, y, c)).wait_recv()
        for cp in first + passed:
            cp.wait_send()
        mine.wait()

    return pl.pallas_call(
        body, name=name, in_specs=[ANY], out_specs=ANY,
        out_shape=jax.ShapeDtypeStruct((N_CHIPS, r, cols), pack.dtype),
        scratch_shapes=[pltpu.SemaphoreType.DMA((6,)), pltpu.SemaphoreType.DMA((6,)), pltpu.SemaphoreType.DMA],
    )(pack)


def exchange_sibling_halves(g, name):
    n, r, cols = g.shape
    half = r // 2

    def body(g_ref, own_ref, recv_ref, send_sems, recv_sems, local_sem):
        x, y, c = _place()
        sibling = (x, y, 1 - c)
        mine = pltpu.make_async_copy(g_ref.at[:, pl.ds(c * half, half), :], own_ref, local_sem)
        mine.start()
        cp = pltpu.make_async_remote_copy(
            src_ref=g_ref.at[:, pl.ds((1 - c) * half, half), :], dst_ref=recv_ref,
            send_sem=send_sems.at[0], recv_sem=recv_sems.at[0], device_id=sibling, device_id_type=MESH)
        cp.start()
        cp.wait()
        mine.wait()

    return pl.pallas_call(
        body, name=name, in_specs=[ANY], out_specs=[ANY, ANY],
        out_shape=[jax.ShapeDtypeStruct((n, half, cols), g.dtype)] * 2,
        scratch_shapes=[pltpu.SemaphoreType.DMA((1,)), pltpu.SemaphoreType.DMA((1,)), pltpu.SemaphoreType.DMA],
    )(g)


def scatter_to_chips(part, name):
    n, h, cols = part.shape

    def body(p_ref, slot_ref, send_sems, recv_sems, local_sem):
        x, y, c = _place()
        me = 2 * x + y
        chips = [(1 - x, y), (x, 1 - y), (1 - x, 1 - y)]
        mine = pltpu.make_async_copy(p_ref.at[me], slot_ref.at[me], local_sem)
        mine.start()
        cps = [pltpu.make_async_remote_copy(
            src_ref=p_ref.at[2 * px + py], dst_ref=slot_ref.at[me],
            send_sem=send_sems.at[j], recv_sem=recv_sems.at[j], device_id=(px, py, c), device_id_type=MESH)
            for j, (px, py) in enumerate(chips)]
        for cp in cps:
            cp.start()
        for j, (px, py) in enumerate(chips):
            pltpu.make_async_remote_copy(
                src_ref=p_ref.at[me], dst_ref=slot_ref.at[2 * px + py],
                send_sem=send_sems.at[j], recv_sem=recv_sems.at[j], device_id=(px, py, c),
                device_id_type=MESH).wait_recv()
        for cp in cps:
            cp.wait_send()
        mine.wait()

    return pl.pallas_call(
        body, name=name, in_specs=[ANY], out_specs=ANY,
        out_shape=jax.ShapeDtypeStruct((n, h, cols), part.dtype),
        scratch_shapes=[pltpu.SemaphoreType.DMA((3,)), pltpu.SemaphoreType.DMA((3,)), pltpu.SemaphoreType.DMA],
    )(part)


def join_sibling_halves(mine_half, name):
    h, cols = mine_half.shape

    def body(m_ref, out_ref, send_sems, recv_sems, local_sem):
        x, y, c = _place()
        sibling = (x, y, 1 - c)
        own = pltpu.make_async_copy(m_ref, out_ref.at[pl.ds(c * h, h), :], local_sem)
        own.start()
        cp = pltpu.make_async_remote_copy(
            src_ref=m_ref, dst_ref=out_ref.at[pl.ds(c * h, h), :],
            send_sem=send_sems.at[0], recv_sem=recv_sems.at[0], device_id=sibling, device_id_type=MESH)
        cp.start()
        pltpu.make_async_remote_copy(
            src_ref=m_ref, dst_ref=out_ref.at[pl.ds((1 - c) * h, h), :],
            send_sem=send_sems.at[0], recv_sem=recv_sems.at[0], device_id=sibling, device_id_type=MESH).wait_recv()
        cp.wait_send()
        own.wait()

    return pl.pallas_call(
        body, name=name, in_specs=[ANY], out_specs=ANY,
        out_shape=jax.ShapeDtypeStruct((2 * h, cols), mine_half.dtype),
        scratch_shapes=[pltpu.SemaphoreType.DMA((1,)), pltpu.SemaphoreType.DMA((1,)), pltpu.SemaphoreType.DMA],
    )(mine_half)


def _kind(n):
    return 'win' if n == 'w_in' else ('row' if n in ROW_SHARDED else 'col')


def _chip_ids():
    x, y, c = _place()
    chips = [(1 - x, y), (x, 1 - y), (1 - x, 1 - y)]
    return x, y, c, 2 * x + y, chips, [2 * px + py for px, py in chips]


def gather_weights(shards, name):
    names = list(SHARDED)
    nt = len(names)
    kinds = [_kind(n) for n in names]
    shapes = [shards[n].shape for n in names]
    depth = shapes[0][0]
    half = depth // 2

    def out_shape(kind, shp):
        if kind == 'col':
            return (shp[0], shp[1], N_CHIPS * shp[2])
        if kind == 'row':
            return (shp[0], N_CHIPS * shp[1], shp[2])
        return (N_CHIPS,) + tuple(shp)

    def body(*refs):
        src, out = refs[:nt], refs[nt:2 * nt]
        send_sems, recv_sems = refs[2 * nt:]
        x, y, c, me, chips, chip_idx = _chip_ids()
        sibling = (x, y, 1 - c)

        def win(t, chip, lo, cnt):
            _, a, b = shapes[t]
            if kinds[t] == 'col':
                return out[t].at[pl.ds(lo, cnt), :, pl.ds(chip * b, b)]
            if kinds[t] == 'row':
                return out[t].at[pl.ds(lo, cnt), pl.ds(chip * a, a), :]
            return out[t].at[chip, pl.ds(lo, cnt)]

        def remote(t, k, chip, lo, to, src_ref=None):
            w = win(t, chip, lo, half)
            return pltpu.make_async_remote_copy(
                src_ref=w if src_ref is None else src_ref, dst_ref=w, send_sem=send_sems.at[7 * t + k],
                recv_sem=recv_sems.at[7 * t + k], device_id=to, device_id_type=MESH)

        def own(t):
            return pltpu.make_async_remote_copy(
                src_ref=src[t], dst_ref=win(t, me, 0, depth), send_sem=send_sems.at[7 * t + 6],
                recv_sem=recv_sems.at[7 * t + 6], device_id=sibling, device_id_type=MESH)

        mine = [own(t) for t in range(nt)]
        for cp in mine:
            cp.start()
        first = [[remote(t, j, me, c * half, (*chips[j], c), src_ref=src[t].at[pl.ds(c * half, half)])
                  for j in range(3)] for t in range(nt)]
        for t in range(nt):
            for cp in first[t]:
                cp.start()
        passed = [[remote(t, 3 + j, chip_idx[j], c * half, sibling) for j in range(3)] for t in range(nt)]
        for t in range(nt):
            for j in range(3):
                remote(t, j, chip_idx[j], c * half, (x, y, c)).wait_recv()
                passed[t][j].start()
        for t in range(nt):
            for j in range(3):
                remote(t, 3 + j, chip_idx[j], (1 - c) * half, (x, y, c)).wait_recv()
        for t in range(nt):
            for cp in first[t] + passed[t]:
                cp.wait_send()
            mine[t].wait()

    outs = pl.pallas_call(
        body, name=name, in_specs=[ANY] * nt, out_specs=[ANY] * nt,
        out_shape=[jax.ShapeDtypeStruct(out_shape(k, s), BF16) for k, s in zip(kinds, shapes)],
        scratch_shapes=[pltpu.SemaphoreType.DMA((7 * nt,)), pltpu.SemaphoreType.DMA((7 * nt,))],
    )(*[shards[n] for n in names])
    return dict(zip(names, outs))


def _half_win(ref, kind, hc, layer):
    if kind == 'col':
        hk = ref.shape[1] // 2
        return ref.at[layer, pl.ds(hc * hk, hk), :]
    if kind == 'row':
        hn = ref.shape[2] // 2
        return ref.at[layer, :, pl.ds(hc * hn, hn)]
    hk = ref.shape[2] // 2
    return ref.at[layer, :, pl.ds(hc * hk, hk), :]


def _half_shape(kind, shp):
    if kind == 'col':
        return (shp[0], shp[1] // 2, shp[2])
    if kind == 'row':
        return (shp[0], shp[1], shp[2] // 2)
    return (shp[0], shp[1], shp[2] // 2, shp[3])


def rs_sibling(grads, name):
    names = list(SHARDED)
    nt = len(names)
    kinds = [_kind(n) for n in names]
    shapes = [grads[n].shape for n in names]
    depth = shapes[0][0]

    def body(*refs):
        src, out = refs[:nt], refs[nt:2 * nt]
        send_sems, recv_sems = refs[2 * nt:]
        x, y, c = _place()
        cps = []
        for t in range(nt):
            for l in range(depth):
                cps.append(pltpu.make_async_remote_copy(
                    src_ref=_half_win(src[t], kinds[t], 1 - c, l), dst_ref=out[t].at[l],
                    send_sem=send_sems.at[depth * t + l], recv_sem=recv_sems.at[depth * t + l],
                    device_id=(x, y, 1 - c), device_id_type=MESH))
        for cp in cps:
            cp.start()
        for cp in cps:
            cp.wait()

    outs = pl.pallas_call(
        body, name=name, in_specs=[ANY] * nt, out_specs=[ANY] * nt,
        out_shape=[jax.ShapeDtypeStruct(_half_shape(k, s), F32) for k, s in zip(kinds, shapes)],
        scratch_shapes=[pltpu.SemaphoreType.DMA((depth * nt,)), pltpu.SemaphoreType.DMA((depth * nt,))],
    )(*[grads[n] for n in names])
    return dict(zip(names, outs))


EW_BLOCK_ELEMS = 512 * 1024


def rs_add(kind, g, recv, c_arr, name):
    shp = recv.shape
    rows, cols = shp[-2], shp[-1]
    tr = _pick(rows, [r for r in (1408, 1024, 704, 512, 256, 128, 64, 32, 16, 8) if r * cols <= EW_BLOCK_ELEMS])
    nb = rows // tr
    lead = (None,) * (len(shp) - 2)
    blk = pl.BlockSpec(lead + (tr, cols), lambda *a: tuple(a[:len(shp) - 2]) + (a[len(shp) - 2], 0))
    if kind == 'row':
        g_blk = pl.BlockSpec(lead + (tr, cols), lambda *a: tuple(a[:len(shp) - 2]) + (a[len(shp) - 2], a[-1][0]))
    else:
        g_blk = pl.BlockSpec(lead + (tr, cols),
                             lambda *a: tuple(a[:len(shp) - 2]) + (a[-1][0] * nb + a[len(shp) - 2], 0))

    def body(c_ref, g_ref, r_ref, o_ref):
        o_ref[...] = (g_ref[...] + r_ref[...]).astype(o_ref.dtype)

    grid_spec = pltpu.PrefetchScalarGridSpec(
        num_scalar_prefetch=1, grid=tuple(shp[:-2]) + (nb,), in_specs=[g_blk, blk], out_specs=blk)
    return pl.pallas_call(
        body, name=name, grid_spec=grid_spec, out_shape=jax.ShapeDtypeStruct(shp, BF16),
        compiler_params=_cparams(None),
    )(c_arr, g, recv)


def _chip_win(ref, kind, chip):
    if kind == 'col':
        ns = ref.shape[2] // N_CHIPS
        return ref.at[:, :, pl.ds(chip * ns, ns)]
    if kind == 'row':
        ks = ref.shape[1] // N_CHIPS
        return ref.at[:, pl.ds(chip * ks, ks), :]
    return ref.at[:, chip]


def _chip_shape(kind, shp):
    if kind == 'col':
        return (shp[0], shp[1], shp[2] // N_CHIPS)
    if kind == 'row':
        return (shp[0], shp[1] // N_CHIPS, shp[2])
    return (shp[0], shp[2], shp[3])


def rs_chips(parts, name):
    names = list(SHARDED)
    nt = len(names)
    kinds = [_kind(n) for n in names]
    shapes = [parts[n].shape for n in names]

    def body(*refs):
        src, out = refs[:nt], refs[nt:2 * nt]
        send_sems, recv_sems, local_sems = refs[2 * nt:]
        x, y, c, me, chips, chip_idx = _chip_ids()
        mine = [pltpu.make_async_copy(_chip_win(src[t], kinds[t], me), out[t].at[me], local_sems.at[t])
                for t in range(nt)]
        for cp in mine:
            cp.start()
        cps = [[pltpu.make_async_remote_copy(
            src_ref=_chip_win(src[t], kinds[t], chip_idx[j]), dst_ref=out[t].at[me],
            send_sem=send_sems.at[3 * t + j], recv_sem=recv_sems.at[3 * t + j],
            device_id=(*chips[j], c), device_id_type=MESH) for j in range(3)] for t in range(nt)]
        for t in range(nt):
            for cp in cps[t]:
                cp.start()
        for t in range(nt):
            for j in range(3):
                pltpu.make_async_remote_copy(
                    src_ref=_chip_win(src[t], kinds[t], me), dst_ref=out[t].at[chip_idx[j]],
                    send_sem=send_sems.at[3 * t + j], recv_sem=recv_sems.at[3 * t + j],
                    device_id=(*chips[j], c), device_id_type=MESH).wait_recv()
        for t in range(nt):
            for cp in cps[t]:
                cp.wait_send()
            mine[t].wait()

    outs = pl.pallas_call(
        body, name=name, in_specs=[ANY] * nt, out_specs=[ANY] * nt,
        out_shape=[jax.ShapeDtypeStruct((N_CHIPS,) + _chip_shape(k, s), parts[n].dtype)
                   for n, k, s in zip(names, kinds, shapes)],
        scratch_shapes=[pltpu.SemaphoreType.DMA((3 * nt,)), pltpu.SemaphoreType.DMA((3 * nt,)),
                        pltpu.SemaphoreType.DMA((nt,))],
    )(*[parts[n] for n in names])
    return dict(zip(names, outs))


def rs_sum(slots, name):
    n, depth, r, cols = slots.shape
    tr = _pick(r, [q for q in (1408, 1024, 704, 512, 256, 128, 64, 32, 16, 8) if q * cols * n <= 2 * EW_BLOCK_ELEMS])

    def body(a_ref, o_ref):
        acc = a_ref[0].astype(F32)
        for k in range(1, n):
            acc = acc + a_ref[k].astype(F32)
        o_ref[...] = acc

    return pl.pallas_call(
        body, name=name, grid=(depth, r // tr),
        in_specs=[pl.BlockSpec((n, None, tr, cols), lambda l, i: (0, l, i, 0))],
        out_specs=pl.BlockSpec((None, tr, cols), lambda l, i: (l, i, 0)),
        out_shape=jax.ShapeDtypeStruct((depth, r, cols), F32), compiler_params=_cparams(("parallel", "parallel")),
    )(slots)


def rs_join(reds, name):
    names = list(SHARDED)
    nt = len(names)
    kinds = [_kind(n) for n in names]
    shapes = [reds[n].shape for n in names]
    depth = shapes[0][0]

    def full_shape(kind, shp):
        if kind == 'row':
            return (shp[0], shp[1], 2 * shp[2])
        return (shp[0], 2 * shp[1], shp[2])

    def win(ref, kind, hc, layer):
        if kind == 'row':
            hn = ref.shape[2] // 2
            return ref.at[layer, :, pl.ds(hc * hn, hn)]
        hk = ref.shape[1] // 2
        return ref.at[layer, pl.ds(hc * hk, hk), :]

    def body(*refs):
        src, out = refs[:nt], refs[nt:2 * nt]
        send_sems, recv_sems, local_sems = refs[2 * nt:]
        x, y, c = _place()
        own, cps = [], []
        for t in range(nt):
            for l in range(depth):
                i = depth * t + l
                own.append(pltpu.make_async_copy(src[t].at[l], win(out[t], kinds[t], c, l), local_sems.at[i]))
                cps.append(pltpu.make_async_remote_copy(
                    src_ref=src[t].at[l], dst_ref=win(out[t], kinds[t], c, l), send_sem=send_sems.at[i],
                    recv_sem=recv_sems.at[i], device_id=(x, y, 1 - c), device_id_type=MESH))
        for cp in own + cps:
            cp.start()
        for t in range(nt):
            for l in range(depth):
                i = depth * t + l
                pltpu.make_async_remote_copy(
                    src_ref=src[t].at[l], dst_ref=win(out[t], kinds[t], 1 - c, l), send_sem=send_sems.at[i],
                    recv_sem=recv_sems.at[i], device_id=(x, y, 1 - c), device_id_type=MESH).wait_recv()
        for cp in cps:
            cp.wait_send()
        for cp in own:
            cp.wait()

    outs = pl.pallas_call(
        body, name=name, in_specs=[ANY] * nt, out_specs=[ANY] * nt,
        out_shape=[jax.ShapeDtypeStruct(full_shape(k, s), F32) for k, s in zip(kinds, shapes)],
        scratch_shapes=[pltpu.SemaphoreType.DMA((depth * nt,)), pltpu.SemaphoreType.DMA((depth * nt,)),
                        pltpu.SemaphoreType.DMA((depth * nt,))],
    )(*[reds[n] for n in names])
    return dict(zip(names, outs))


def rs_chips2(parts, name):
    names = list(SHARDED)
    nt = len(names)
    kinds = [_kind(n) for n in names]
    shapes = [parts[n].shape for n in names]

    def body(*refs):
        src, out = refs[:nt], refs[nt:2 * nt]
        send_sems, recv_sems = refs[2 * nt:]
        x, y, c, me, chips, chip_idx = _chip_ids()
        cps = [[pltpu.make_async_remote_copy(
            src_ref=_chip_win(src[t], kinds[t], chip_idx[j]), dst_ref=out[t].at[j],
            send_sem=send_sems.at[3 * t + j], recv_sem=recv_sems.at[3 * t + j],
            device_id=(*chips[j], c), device_id_type=MESH) for j in range(3)] for t in range(nt)]
        for t in range(nt):
            for cp in cps[t]:
                cp.start()
        for t in range(nt):
            for cp in cps[t]:
                cp.wait()

    outs = pl.pallas_call(
        body, name=name, in_specs=[ANY] * nt, out_specs=[ANY] * nt,
        out_shape=[jax.ShapeDtypeStruct((3,) + _chip_shape(k, s), parts[n].dtype)
                   for n, k, s in zip(names, kinds, shapes)],
        scratch_shapes=[pltpu.SemaphoreType.DMA((3 * nt,)), pltpu.SemaphoreType.DMA((3 * nt,))],
    )(*[parts[n] for n in names])
    return dict(zip(names, outs))


def rs_sum2(kind, part, slots, sc_arr, name):
    _, depth, r, cols = slots.shape
    tr = _pick(r, [q for q in (1408, 1024, 704, 512, 256, 128, 64, 32, 16) if q * cols <= EW_BLOCK_ELEMS // 2])
    nb = r // tr
    if kind == 'col':
        own_blk = pl.BlockSpec((None, tr, cols), lambda l, i, sc: (l, i, sc[0]))
        out_blk = pl.BlockSpec((None, tr, cols), lambda l, i, sc: (l, sc[1] * nb + i, 0))
        out_shape = (depth, 2 * r, cols)
    elif kind == 'row':
        own_blk = pl.BlockSpec((None, tr, cols), lambda l, i, sc: (l, sc[0] * nb + i, 0))
        out_blk = pl.BlockSpec((None, tr, cols), lambda l, i, sc: (l, i, sc[1]))
        out_shape = (depth, r, 2 * cols)
    else:
        own_blk = pl.BlockSpec((None, None, tr, cols), lambda l, i, sc: (l, sc[0], i, 0))
        out_blk = pl.BlockSpec((None, tr, cols), lambda l, i, sc: (l, sc[1] * nb + i, 0))
        out_shape = (depth, 2 * r, cols)

    def body(sc_ref, own_ref, s_ref, o_ref):
        acc = own_ref[...].astype(F32)
        for k in range(3):
            acc = acc + s_ref[k].astype(F32)
        o_ref[...] = acc

    grid_spec = pltpu.PrefetchScalarGridSpec(
        num_scalar_prefetch=1, grid=(depth, nb),
        in_specs=[own_blk, pl.BlockSpec((3, None, tr, cols), lambda l, i, sc: (0, l, i, 0))], out_specs=out_blk)
    return pl.pallas_call(
        body, name=name, grid_spec=grid_spec, out_shape=jax.ShapeDtypeStruct(out_shape, F32),
        compiler_params=_cparams(None),
    )(sc_arr, part, slots)


def rs_join2(halves, name):
    names = list(SHARDED)
    nt = len(names)
    kinds = [_kind(n) for n in names]
    shapes = [halves[n].shape for n in names]
    depth = shapes[0][0]

    def win(ref, kind, hc, layer):
        if kind == 'row':
            hn = ref.shape[2] // 2
            return ref.at[layer, :, pl.ds(hc * hn, hn)]
        hk = ref.shape[1] // 2
        return ref.at[layer, pl.ds(hc * hk, hk), :]

    def body(*refs):
        src, out = refs[:nt], refs[nt:2 * nt]
        send_sems, recv_sems = refs[2 * nt:]
        x, y, c = _place()
        cps = []
        for t in range(nt):
            for l in range(depth):
                i = depth * t + l
                cps.append(pltpu.make_async_remote_copy(
                    src_ref=win(src[t], kinds[t], c, l), dst_ref=win(out[t], kinds[t], c, l),
                    send_sem=send_sems.at[i], recv_sem=recv_sems.at[i], device_id=(x, y, 1 - c),
                    device_id_type=MESH))
        for cp in cps:
            cp.start()
        for t in range(nt):
            for l in range(depth):
                i = depth * t + l
                pltpu.make_async_remote_copy(
                    src_ref=win(src[t], kinds[t], c, l), dst_ref=win(out[t], kinds[t], 1 - c, l),
                    send_sem=send_sems.at[i], recv_sem=recv_sems.at[i], device_id=(x, y, 1 - c),
                    device_id_type=MESH).wait_recv()
        for cp in cps:
            cp.wait_send()

    outs = pl.pallas_call(
        body, name=name, in_specs=[ANY] * nt, out_specs=[ANY] * nt,
        out_shape=[jax.ShapeDtypeStruct(s, F32) for s in shapes],
        input_output_aliases={t: t for t in range(nt)},
        scratch_shapes=[pltpu.SemaphoreType.DMA((depth * nt,)), pltpu.SemaphoreType.DMA((depth * nt,))],
    )(*[halves[n] for n in names])
    return dict(zip(names, outs))


def gather_small(v, name):
    r, cols = v.shape

    def body(v_ref, out_ref, send_sems, recv_sems):
        x, y, c = _place()
        me = 4 * x + 2 * y + c
        out_ref[me] = v_ref[...]
        cps = []
        for rel in range(1, N_DEV):
            px = 1 - x if (rel >> 2) & 1 else x
            py = 1 - y if (rel >> 1) & 1 else y
            pc = 1 - c if rel & 1 else c
            cps.append(pltpu.make_async_remote_copy(
                src_ref=v_ref, dst_ref=out_ref.at[me], send_sem=send_sems.at[rel - 1],
                recv_sem=recv_sems.at[rel - 1], device_id=(px, py, pc), device_id_type=MESH))
        for cp in cps:
            cp.start()
        for cp in cps:
            cp.wait()

    return pl.pallas_call(
        body, name=name, in_specs=[pl.BlockSpec(memory_space=pltpu.VMEM)],
        out_specs=pl.BlockSpec(memory_space=pltpu.VMEM),
        out_shape=jax.ShapeDtypeStruct((N_DEV, r, cols), v.dtype),
        scratch_shapes=[pltpu.SemaphoreType.DMA((N_DEV - 1,)), pltpu.SemaphoreType.DMA((N_DEV - 1,))],
        compiler_params=pltpu.CompilerParams(vmem_limit_bytes=VMEM_LIMIT),
    )(v)


def _pad_rows(flat, row_align):
    n = flat.shape[-1]
    unit = PACK_COLS * row_align
    tot = -(-n // unit) * unit
    pad = [(0, 0)] * (flat.ndim - 1) + [(0, tot - n)]
    return jnp.pad(flat, pad)


def _pack_shards(ws):
    flat = jnp.concatenate([ws[n].astype(BF16).reshape(-1) for n in SHARDED])
    return _pad_rows(flat, PACK_ROW_ALIGN).reshape(-1, PACK_COLS)


def _unpack_full(gathered, shard_shapes):
    flat = gathered.reshape(N_CHIPS, -1)
    out, off = {}, 0
    for n in SHARDED:
        shp = shard_shapes[n]
        size = math.prod(shp)
        seg = flat[:, off:off + size].reshape((N_CHIPS,) + tuple(shp))
        off += size
        if n in ROW_SHARDED:
            out[n] = jnp.transpose(seg, (1, 0, 2, 3)).reshape(shp[0], N_CHIPS * shp[1], shp[2])
        else:
            out[n] = jnp.transpose(seg, (1, 2, 0, 3)).reshape(shp[0], shp[1], N_CHIPS * shp[2])
    return out


def _pack_grads(gfull, shard_shapes):
    segs = []
    for n in SHARDED:
        shp = shard_shapes[n]
        g = gfull[n]
        if n in ROW_SHARDED:
            seg = jnp.transpose(g.reshape(shp[0], N_CHIPS, shp[1], shp[2]), (1, 0, 2, 3))
        else:
            seg = jnp.transpose(g.reshape(shp[0], shp[1], N_CHIPS, shp[2]), (2, 0, 1, 3))
        segs.append(seg.reshape(N_CHIPS, -1))
    flat = jnp.concatenate(segs, axis=1)
    return _pad_rows(flat, PACK_ROW_ALIGN).reshape(N_CHIPS, -1, PACK_COLS)


def _unpack_shard_grads(red, shard_shapes):
    flat = red.reshape(-1)
    out, off = {}, 0
    for n in SHARDED:
        shp = shard_shapes[n]
        size = math.prod(shp)
        out[n] = flat[off:off + size].reshape(shp)
        off += size
    return out


def _pack_small(parts):
    flat = jnp.concatenate([p.astype(F32).reshape(-1) for p in parts])
    return _pad_rows(flat, 8).reshape(-1, PACK_COLS)


def _unpack_small(flat2d, shapes):
    flat = flat2d.reshape(-1)
    out, off = [], 0
    for shp in shapes:
        size = math.prod(shp)
        out.append(flat[off:off + size].reshape(shp))
        off += size
    return out


def _heads(a, bn, s, h):
    return jnp.transpose(a.reshape(bn, s, h, HEAD_DIM), (0, 2, 1, 3))


def _unheads(a):
    bn, h, s, d = a.shape
    return jnp.transpose(a, (0, 2, 1, 3)).reshape(bn * s, h * d)


def _reorder_w_in(w):
    d = w.shape[0]
    return jnp.concatenate([w[:, 2820:6916], w[:, 0:768], w[:, 772:1540], w[:, 1540:2052], w[:, 2052:2820],
                            w[:, 768:772], jnp.zeros((d, N_PROJ - 6916), w.dtype)], axis=1)


def _restore_dw_in(g):
    return jnp.concatenate([g[:, 4096:4864], g[:, 6912:6916], g[:, 4864:5632], g[:, 5632:6144], g[:, 6144:6912],
                            g[:, 0:4096]], axis=1)


def res_rms_next_fwd(h, f, g, coef, g_next, name):
    def fn(x, y, gg, gn):
        hn = x + coef * _rms(y, gg)
        return hn, _rms(hn, gn)
    return _rowwise(fn, [_full(h), _full(f)], [g, g_next], [(h.shape[1], F32), (h.shape[1], BF16)], tm=512,
                    name=name)


def ple_next_fwd(h, pgl, pr, g, g_next, name):
    def fn(x, a, b, gg, gn):
        hn = x + _ple(a, b, gg)
        return hn, _rms(hn, gn)
    return _rowwise(fn, [_full(h), _full(pgl), _full(pr)], [g, g_next], [(D_MODEL, F32), (D_MODEL, BF16)], tm=512,
                    name=name)


def _ffn_fwd(h, n, w_gu, w_down, g_post, g_next, tag):
    gu = _mm(n, w_gu, out_dtype=BF16, name=f"{tag}_mm_gu")
    a = swiglu_fwd(gu, f"{tag}_swiglu")
    f = _mm(a, w_down, out_dtype=F32, name=f"{tag}_mm_down")
    h_out, n_next = res_rms_next_fwd(h, f, g_post, 0.5, g_next, f"{tag}_res")
    return h_out, n_next, (h, n, gu, a, f)


def rms_res_bwd(h, g, dn, dres, f, g2, coef, name):
    def fn(x, d, r, y, gg, gg2):
        _, vjp = jax.vjp(_rms, x, gg)
        dx, dg = vjp(d.astype(F32))
        dh = dx + r
        _, vjp2 = jax.vjp(lambda a, b: coef * _rms(a, b), y, gg2)
        dy, dg2 = vjp2(dh)
        return dh, dy, dg, dg2
    w = h.shape[1]
    return _rowwise(fn, [_full(h), _full(dn), _full(dres), _full(f)], [g, g2], [(w, F32), (w, BF16)], [w, w],
                    name=name)


def _ffn_bwd(df, saved, w_gu, w_down, tag, dw, n_gu, n_down):
    _, n, gu, a, _ = saved
    da = _mm(df, w_down, tb=True, out_dtype=BF16, name=f"{tag}_mm_da")
    dw(n_down, a, df, f"{tag}_mm_dwdown")
    dgu = swiglu_bwd(gu, da, f"{tag}_swiglu_bwd")
    dw(n_gu, n, dgu, f"{tag}_mm_dwgu")
    return _mm(dgu, w_gu, tb=True, out_dtype=BF16, name=f"{tag}_mm_dn")


def kernel(x, p, ffn1_norm_pre, ffn1_w_gu, ffn1_w_down, ffn1_norm_post, mix_norm_pre, w_in, b_forget, b_gate, conv_short, conv_dw, conv_dw_bias, conv_ln_gain, conv_ln_bias, attn_sinks, rel_bias, w_br_a, w_br_b, w_br_c, w_br_d, w_o, mix_norm_post, ffn2_norm_pre, ffn2_w_gu, ffn2_w_down, ffn2_norm_post, ple_norm_gate, w_ple_gate, w_ple, ple_norm_post, loss_target, m_ffn1_norm_pre, m_ffn1_w_gu, m_ffn1_w_down, m_ffn1_norm_post, m_mix_norm_pre, m_w_in, m_b_forget, m_b_gate, m_conv_short, m_conv_dw, m_conv_dw_bias, m_conv_ln_gain, m_conv_ln_bias, m_attn_sinks, m_rel_bias, m_w_br_a, m_w_br_b, m_w_br_c, m_w_br_d, m_w_o, m_mix_norm_post, m_ffn2_norm_pre, m_ffn2_w_gu, m_ffn2_w_down, m_ffn2_norm_post, m_ple_norm_gate, m_w_ple_gate, m_w_ple, m_ple_norm_post, v_ffn1_norm_pre, v_ffn1_w_gu, v_ffn1_w_down, v_ffn1_norm_post, v_mix_norm_pre, v_w_in, v_b_forget, v_b_gate, v_conv_short, v_conv_dw, v_conv_dw_bias, v_conv_ln_gain, v_conv_ln_bias, v_attn_sinks, v_rel_bias, v_w_br_a, v_w_br_b, v_w_br_c, v_w_br_d, v_w_o, v_mix_norm_post, v_ffn2_norm_pre, v_ffn2_w_gu, v_ffn2_w_down, v_ffn2_norm_post, v_ple_norm_gate, v_w_ple_gate, v_w_ple, v_ple_norm_post):
    args = dict(locals())
    ws = {n: args[n] for n in WEIGHTS}
    ms = {n: args["m_" + n] for n in WEIGHTS}
    vs = {n: args["v_" + n] for n in WEIGHTS}
    return _step(x, p, loss_target, ws, ms, vs)


def _local(x, p, loss_target, ws, wf, w_short, w_dw):
    bn, s, d = x.shape
    t = bn * s
    depth = w_short.shape[0]

    def vec(a, i):
        return a[i].reshape(1, -1)

    bucket = _bucket_table()
    band_bias = band_bias_fwd(bucket, ws['rel_bias'], "band_bias")

    h = x.reshape(t, d)
    saved = []
    n1 = rms_fwd(h, vec(ws['ffn1_norm_pre'], 0), "l0_ffn1_rms")
    for i in range(depth):
        sv = {}
        h1, u, sv['ffn1'] = _ffn_fwd(h, n1, (wf['ffn1_w_gu'], i), (wf['ffn1_w_down'], i),
                                     vec(ws['ffn1_norm_post'], i), vec(ws['mix_norm_pre'], i), f"l{i}_ffn1")
        w_in_r = _reorder_w_in(wf['w_in'][i])
        proj = _mm(u, w_in_r, out_dtype=BF16, name=f"l{i}_mm_proj")
        bf = jnp.pad(vec(ws['b_forget'], i), ((0, 0), (0, 128 - A_HEADS)))
        cc = fgate_fwd(proj, bf, bn, s, f"l{i}_fgate")
        c4 = jnp.transpose(cc.reshape(bn, s, 128)[:, :, :A_HEADS], (0, 2, 1))
        c_col = c4[..., None]
        c_row = c4.reshape(bn, A_HEADS, s // FOX_T, 1, FOX_T)
        ya, oa, lse_a = fox2_fwd(proj, c_col, c_row, bn, s, f"l{i}_fox")
        w_sh = jnp.pad(w_short[i], ((0, 8 - w_short.shape[1]), (0, 0)))
        w_cv = jnp.pad(w_dw[i], ((0, 32 - w_dw.shape[1]), (0, 0)))
        yb = convb_fwd(proj, w_sh[:3], bn, s, f"l{i}_convb")
        cvec = (vec(ws['conv_dw_bias'], i), vec(ws['conv_ln_gain'], i), vec(ws['conv_ln_bias'], i))
        yc, yconv = convc_fwd(proj, w_cv[:31], *cvec, bn, s, f"l{i}_convc")
        sinks = jnp.broadcast_to(ws['attn_sinks'][i].reshape(D_Q_HEADS, 1, 1), (D_Q_HEADS, QB, 1))
        yd, lse_d = swa3_fwd(proj, band_bias, sinks, bn, s, f"l{i}_swa")
        ys = (ya, yb, yc, yd)
        wbr = ((wf['w_br_a'], i), (wf['w_br_b'], i), (wf['w_br_c'], i), (wf['w_br_d'], i))
        zs = [_mm(yk, wk, out_dtype=BF16, name=f"l{i}_mm_br{k}") for k, (yk, wk) in enumerate(zip(ys, wbr))]
        bgs = [ws['b_gate'][i, k * d:(k + 1) * d].reshape(1, d) for k in range(4)]
        merged = merge_fwd(proj, zs, bgs, f"l{i}_merge")
        mo = _mm(merged, (wf['w_o'], i), out_dtype=F32, name=f"l{i}_mm_o")
        h2, n2 = res_rms_next_fwd(h1, mo, vec(ws['mix_norm_post'], i), 1.0, vec(ws['ffn2_norm_pre'], i),
                                  f"l{i}_mix_res")
        sv['mix'] = dict(h1=h1, u=u, proj=proj, w_in_r=w_in_r, bf=bf, c_col=c_col, c_row=c_row, oa=oa, lse_a=lse_a,
                         w_sh=w_sh, w_cv=w_cv, cvec=cvec, yconv=yconv, sinks=sinks, lse_d=lse_d, ys=ys, wbr=wbr, zs=zs, bgs=bgs,
                         merged=merged, mo=mo)
        h3, ng, sv['ffn2'] = _ffn_fwd(h2, n2, (wf['ffn2_w_gu'], i), (wf['ffn2_w_down'], i),
                                      vec(ws['ffn2_norm_post'], i), vec(ws['ple_norm_gate'], i), f"l{i}_ffn2")
        pgl = _mm(ng, (wf['w_ple_gate'], i), out_dtype=BF16, name=f"l{i}_mm_pgl")
        p_i = p[i].reshape(t, -1)
        pr = _mm(p_i, (wf['w_ple'], i), out_dtype=F32, name=f"l{i}_mm_pr")
        if i + 1 < depth:
            h, n1 = ple_next_fwd(h3, pgl, pr, vec(ws['ple_norm_post'], i), vec(ws['ffn1_norm_pre'], i + 1),
                                 f"l{i}_ple")
        else:
            h = ple_fwd(h3, pgl, pr, vec(ws['ple_norm_post'], i), f"l{i}_ple")
        sv['ple'] = dict(h3=h3, ng=ng, pgl=pgl, p_i=p_i, pr=pr)
        saved.append(sv)

    dh, loss_vec = loss_fwd_bwd(h, loss_target.reshape(t, d), "loss")
    loss_part = jnp.sum(loss_vec)

    gst = {}
    gwin = [None] * depth

    def dw(n, a, b, nm):
        gst[n] = _mm(a, b, ta=True, name=nm, stack=(gst.get(n), depth, i))

    gsmall = {n: [None] * depth for n in REPLICATED + CONV_SHARDED if n != 'rel_bias'}
    dbias_layers = []
    for i in reversed(range(depth)):
        sv = saved[i]
        pv = sv['ple']
        dpgl, dpr, dg = ple_bwd(pv['pgl'], pv['pr'], vec(ws['ple_norm_post'], i), dh, f"l{i}_ple_bwd")
        gsmall['ple_norm_post'][i] = dg
        dw('w_ple', pv['p_i'], dpr, f"l{i}_mm_dwple")
        dw('w_ple_gate', pv['ng'], dpgl, f"l{i}_mm_dwpg")
        dng = _mm(dpgl, (wf['w_ple_gate'], i), tb=True, out_dtype=BF16, name=f"l{i}_mm_dng")
        dh, df2, gsmall['ple_norm_gate'][i], gsmall['ffn2_norm_post'][i] = rms_res_bwd(
            pv['h3'], vec(ws['ple_norm_gate'], i), dng, dh, sv['ffn2'][4], vec(ws['ffn2_norm_post'], i), 0.5,
            f"l{i}_ple_rms_bwd")
        dn2 = _ffn_bwd(df2, sv['ffn2'], (wf['ffn2_w_gu'], i), (wf['ffn2_w_down'], i), f"l{i}_ffn2", dw,
                       'ffn2_w_gu', 'ffn2_w_down')
        mv = sv['mix']
        dh, dmo, gsmall['ffn2_norm_pre'][i], gsmall['mix_norm_post'][i] = rms_res_bwd(
            sv['ffn2'][0], vec(ws['ffn2_norm_pre'], i), dn2, dh, mv['mo'], vec(ws['mix_norm_post'], i), 1.0,
            f"l{i}_ffn2_rms_bwd")
        dw('w_o', mv['merged'], dmo, f"l{i}_mm_dwo")
        dmerged = _mm(dmo, (wf['w_o'], i), tb=True, out_dtype=BF16, name=f"l{i}_mm_dmerged")
        mb = merge_bwd(mv['proj'], mv['zs'], mv['bgs'], dmerged, f"l{i}_merge_bwd")
        dgates, dzs, dbg = mb[0:4], mb[4:8], mb[8:12]
        gsmall['b_gate'][i] = jnp.concatenate(dbg, axis=1)
        dys = []
        for k, nm in enumerate(('w_br_a', 'w_br_b', 'w_br_c', 'w_br_d')):
            dw(nm, mv['ys'][k], dzs[k], f"l{i}_mm_dwbr{k}")
            dys.append(_mm(dzs[k], mv['wbr'][k], tb=True, out_dtype=BF16, name=f"l{i}_mm_dy{k}"))
        dqa, dka, dva, dck, dcq = fox2_bwd(mv['proj'], mv['c_col'], mv['c_row'], mv['oa'], mv['lse_a'], dys[0], bn, s,
                                           f"l{i}_fox_bwd")
        dc = jnp.transpose(dck.reshape(bn, A_HEADS, s) + dcq.reshape(bn, A_HEADS, s), (0, 2, 1))
        dc = jnp.pad(dc, ((0, 0), (0, 0), (0, 128 - A_HEADS))).reshape(t, 128)
        daf, dbf = fgate_bwd(mv['proj'], mv['bf'], dc, bn, s, f"l{i}_fgate_bwd")
        gsmall['b_forget'][i] = dbf[:, :A_HEADS]
        dbg_, dcg_, dxb_, dwsh = convb_bwd(mv['proj'], mv['w_sh'][:3], dys[1], bn, s, f"l{i}_convb_bwd")
        gsmall['conv_short'][i] = dwsh[:3]
        dca, dcb, dwcv, dcbias, dlg, dlb = convc_bwd(mv['proj'], mv['w_cv'][:31], *mv['cvec'], mv['yconv'], dys[2], bn, s,
                                                     f"l{i}_convc_bwd")
        gsmall['conv_dw'][i] = dwcv[:31]
        gsmall['conv_dw_bias'][i] = dcbias
        gsmall['conv_ln_gain'][i] = dlg
        gsmall['conv_ln_bias'][i] = dlb
        dqd, dkd, dvd, dbias, dsink = swa3_bwd(mv['proj'], band_bias, mv['sinks'], mv['ys'][3], mv['lse_d'], dys[3],
                                               bn, s, f"l{i}_swa_bwd")
        dbias_layers.append(dbias)
        gsmall['attn_sinks'][i] = jnp.sum(dsink, axis=(1, 2))
        dproj = assemble_dproj(list(dgates) + [dqa, dka, dva, dbg_, dcg_, dxb_, dca, dcb, dqd], dkd, dvd, daf,
                               f"l{i}_dproj")
        dwin = _restore_dw_in(_mm(mv['u'], dproj, ta=True, name=f"l{i}_mm_dwin"))
        gwin[i] = jnp.transpose(dwin.reshape(d, N_CHIPS, -1), (1, 0, 2))
        du = _mm(dproj, mv['w_in_r'], tb=True, out_dtype=BF16, name=f"l{i}_mm_du")
        dh, df1, gsmall['mix_norm_pre'][i], gsmall['ffn1_norm_post'][i] = rms_res_bwd(
            mv['h1'], vec(ws['mix_norm_pre'], i), du, dh, sv['ffn1'][4], vec(ws['ffn1_norm_post'], i), 0.5,
            f"l{i}_mix_rms_bwd")
        dn1 = _ffn_bwd(df1, sv['ffn1'], (wf['ffn1_w_gu'], i), (wf['ffn1_w_down'], i), f"l{i}_ffn1", dw,
                       'ffn1_w_gu', 'ffn1_w_down')
        dh, gsmall['ffn1_norm_pre'][i] = rms_bwd(sv['ffn1'][0], vec(ws['ffn1_norm_pre'], i), dn1, dh,
                                                  f"l{i}_ffn1_rms_bwd")
    grad_x = dh.reshape(bn, s, d)

    drel = band_bias_bwd(bucket, dbias_layers, "band_bias_bwd")
    gst['w_in'] = jnp.stack(gwin)
    full_shapes = {n: ws[n].shape for n in REPLICATED}
    full_shapes['conv_short'] = w_short.shape
    full_shapes['conv_dw'] = w_dw.shape
    gs = {n: jnp.stack([a.reshape(full_shapes[n][1:]) for a in gsmall[n]]) for n in gsmall}
    gs['rel_bias'] = jnp.transpose(drel[:, :, 0])
    return loss_part, grad_x, gst, gs


def _step(x, p, loss_target, ws, ms, vs):
    chip = 2 * lax.axis_index("x") + lax.axis_index("y")

    wf = gather_weights({n: ws[n].astype(BF16) for n in SHARDED}, "gather_weights")
    w_in_all = wf['w_in']
    wf['w_in'] = jnp.transpose(w_in_all, (1, 2, 0, 3)).reshape(w_in_all.shape[1], w_in_all.shape[2], -1)
    conv_shapes = [ws[n].shape for n in CONV_SHARDED]
    conv_all = gather_small(_pack_small([ws[n] for n in CONV_SHARDED]), "gather_conv")
    conv_full = []
    for idx, n in enumerate(CONV_SHARDED):
        per_chip = [_unpack_small(conv_all[2 * j], conv_shapes)[idx] for j in range(N_CHIPS)]
        conv_full.append(jnp.concatenate(per_chip, axis=-1))
    w_short, w_dw = conv_full

    loss_part, grad_x, gst, gs = _local(x, p, loss_target, {n: ws[n] for n in REPLICATED}, wf, w_short, w_dw)

    c_arr = lax.axis_index("c").astype(jnp.int32).reshape(1)
    recv = rs_sibling(gst, "rs_sibling")
    chip_sum = {n: rs_add(_kind(n), gst[n], recv[n], c_arr, f"rs_add_{n}") for n in SHARDED}
    slots = rs_chips2(chip_sum, "rs_chips")
    sc_arr = jnp.stack([chip, lax.axis_index("c")]).astype(jnp.int32)
    red_half = {n: rs_sum2(_kind(n), chip_sum[n], slots[n], sc_arr, f"rs_sum_{n}") for n in SHARDED}
    g_shard = rs_join2(red_half, "rs_join")

    small_names = [n for n in REPLICATED + CONV_SHARDED]
    small_parts = [gs[n] for n in small_names]
    small_shapes = [g.shape for g in small_parts]
    small_parts.append(loss_part.reshape(1))
    small_shapes.append((1,))
    small_all = gather_small(_pack_small(small_parts), "gather_small")
    small_red = sum_slots(small_all, "small_sum")
    small_g = _unpack_small(small_red, small_shapes)
    loss = small_g[-1].reshape(())
    g_small = dict(zip(small_names, small_g[:-1]))

    grads = {}
    for n in WEIGHTS:
        if n in SHARDED:
            grads[n] = g_shard[n]
        elif n in CONV_SHARDED:
            wdt = ws[n].shape[-1]
            grads[n] = lax.dynamic_slice_in_dim(g_small[n], chip * wdt, wdt, axis=2)
        else:
            grads[n] = g_small[n]

    deltas, new_m, new_v = {}, {}, {}
    small_upd = [n for n in WEIGHTS if n not in SHARDED]
    for n in SHARDED:
        deltas[n], new_m[n], new_v[n] = adamw(ws[n], grads[n], ms[n], vs[n], f"adamw_{n}")
    shapes_u = [ws[n].shape for n in small_upd]
    packs = [_pack_small([src[n] for n in small_upd]) for src in (ws, grads, ms, vs)]
    upd = adamw(*packs, "adamw_small")
    for res, dst in zip(upd, (deltas, new_m, new_v)):
        for n, a in zip(small_upd, _unpack_small(res, shapes_u)):
            dst[n] = a

    return (loss, grad_x, *[grads[n] for n in WEIGHTS], *[deltas[n] for n in WEIGHTS],
            *[new_m[n] for n in WEIGHTS], *[new_v[n] for n in WEIGHTS])
```

```python
import functools
import math

import jax
import jax.numpy as jnp
from jax import lax
from jax.experimental import pallas as pl
from jax.experimental.pallas import tpu as pltpu

F32 = jnp.float32
BF16 = jnp.bfloat16
MESH = pl.DeviceIdType.MESH

D_MODEL = 1024
DEPTH = 4
HEAD_DIM = 64
A_HEADS = 4
D_Q_HEADS = 8
D_KV_HEADS = 2
D_GROUP = 4
WINDOW = 128
QB = 128
REL_BUCKETS = 32
REL_MAX_DIST = 128
D_FF = 2816
EPS = 1e-6
NEG = -1e30
SCALE = HEAD_DIM ** -0.5
N_CHIPS = 4
N_DEV = 8

ADAM_LR = 0.001
ADAM_B1 = 0.9
ADAM_B2 = 0.999
ADAM_EPS = 1e-08
ADAM_WD = 0.01
ADAM_STEP = 10

C_GATE = 0
C_AQ, C_AK, C_AV = 4096, 4352, 4608
C_BG, C_CG, C_XB = 4864, 5120, 5376
C_CA, C_CB = 5632, 5888
C_DQ, C_DK, C_DV = 6144, 6656, 6784
C_AF = 6912
N_PROJ = 7168

VMEM_LIMIT = 56 * 1024 * 1024
PACK_COLS = 1024
PACK_ROW_ALIGN = 1024

SHARDED = ('ffn1_w_gu', 'ffn1_w_down', 'w_in', 'w_br_a', 'w_br_b', 'w_br_c', 'w_br_d', 'w_o',
           'ffn2_w_gu', 'ffn2_w_down', 'w_ple_gate', 'w_ple')
ROW_SHARDED = ('ffn1_w_down', 'w_o', 'ffn2_w_down', 'w_ple_gate')
CONV_SHARDED = ('conv_short', 'conv_dw')
REPLICATED = ('ffn1_norm_pre', 'ffn1_norm_post', 'mix_norm_pre', 'b_forget', 'b_gate', 'conv_dw_bias',
              'conv_ln_gain', 'conv_ln_bias', 'attn_sinks', 'rel_bias', 'mix_norm_post', 'ffn2_norm_pre',
              'ffn2_norm_post', 'ple_norm_gate', 'ple_norm_post')
WEIGHTS = ('ffn1_norm_pre', 'ffn1_w_gu', 'ffn1_w_down', 'ffn1_norm_post', 'mix_norm_pre', 'w_in', 'b_forget',
           'b_gate', 'conv_short', 'conv_dw', 'conv_dw_bias', 'conv_ln_gain', 'conv_ln_bias', 'attn_sinks',
           'rel_bias', 'w_br_a', 'w_br_b', 'w_br_c', 'w_br_d', 'w_o', 'mix_norm_post', 'ffn2_norm_pre',
           'ffn2_w_gu', 'ffn2_w_down', 'ffn2_norm_post', 'ple_norm_gate', 'w_ple_gate', 'w_ple', 'ple_norm_post')


def _cparams(sem=None):
    return pltpu.CompilerParams(dimension_semantics=sem, vmem_limit_bytes=VMEM_LIMIT)


def _pick(dim, cands):
    for c in cands:
        if dim % c == 0:
            return c
    return dim


MM_VMEM_BUDGET = 40 * 1024 * 1024
MXU_FLOPS = 9.0e14
HBM_BYTES_PER_S = 3.0e12
GRID_STEP_S = 0.35e-6


def _divisors(dim, cands):
    out = [c for c in cands if c <= dim and dim % c == 0]
    return out or [dim]


def _mm_tiles(m, n, k, ab, bb, ob):
    best = None
    for tm in _divisors(m, (2048, 1408, 1024, 512, 256, 128)):
        for tn in _divisors(n, (2816, 2048, 1792, 1408, 1024, 512, 256, 128)):
            for tk in _divisors(k, (k if k <= 2048 else 2816, 2816, 2048, 1792, 1408, 1024, 512, 256, 128)):
                nk = k // tk
                vmem = 2 * (tm * tk * ab + tk * tn * bb + tm * tn * ob) + tm * tn * 4 * (2 if nk > 1 else 1)
                if vmem > MM_VMEM_BUDGET:
                    continue
                steps = (m // tm) * (n // tn) * nk
                a_bytes = m * k * ab * (1 if nk == 1 else n // tn)
                b_bytes = k * n * bb * (1 if (nk == 1 and n == tn) else m // tm)
                mem = (a_bytes + b_bytes + m * n * ob) / HBM_BYTES_PER_S
                acc = steps * tm * tn * 1.5e-12 if nk > 1 else 0.0
                cost = steps * GRID_STEP_S + max(2.0 * m * n * k / MXU_FLOPS, mem) + acc
                if best is None or cost < best[0]:
                    best = (cost, tm, tn, tk)
    assert best is not None, (m, n, k)
    return best[1:]


def _mm(a, b, *, ta=False, tb=False, out_dtype=F32, name="mm", stack=None):
    b_layer = None
    if isinstance(b, tuple):
        b, b_layer = b
    if ta:
        kdim, m = a.shape
    else:
        m, kdim = a.shape
    if tb:
        n, kb = b.shape[-2:]
    else:
        kb, n = b.shape[-2:]
    assert kb == kdim, (a.shape, b.shape, ta, tb)
    tm, tn, tk = _mm_tiles(m, n, kdim, a.dtype.itemsize, b.dtype.itemsize, jnp.dtype(out_dtype).itemsize)
    nk = kdim // tk
    dims = (((0,) if ta else (1,), (1,) if tb else (0,)), ((), ()))

    def dot(a_ref, b_ref):
        return lax.dot_general(a_ref[...].astype(BF16), b_ref[...].astype(BF16), dims, preferred_element_type=F32)

    if nk == 1:
        def body(a_ref, b_ref, *rest):
            o_ref = rest[-1]
            o_ref[...] = dot(a_ref, b_ref).astype(o_ref.dtype)
        scratch = []
    else:
        def body(a_ref, b_ref, *rest):
            o_ref, acc_ref = rest[-2], rest[-1]
            k = pl.program_id(2)

            @pl.when(k == 0)
            def _():
                acc_ref[...] = dot(a_ref, b_ref)

            @pl.when(jnp.logical_and(k > 0, k < nk - 1))
            def _():
                acc_ref[...] += dot(a_ref, b_ref)

            @pl.when(k == nk - 1)
            def _():
                o_ref[...] = (acc_ref[...] + dot(a_ref, b_ref)).astype(o_ref.dtype)
        scratch = [pltpu.VMEM((tm, tn), F32)]

    a_spec = pl.BlockSpec((tk, tm), lambda i, j, k: (k, i)) if ta else pl.BlockSpec((tm, tk), lambda i, j, k: (i, k))
    if b_layer is None:
        b_spec = (pl.BlockSpec((tn, tk), lambda i, j, k: (j, k)) if tb
                  else pl.BlockSpec((tk, tn), lambda i, j, k: (k, j)))
    else:
        b_spec = (pl.BlockSpec((None, tn, tk), lambda i, j, k: (b_layer, j, k)) if tb
                  else pl.BlockSpec((None, tk, tn), lambda i, j, k: (b_layer, k, j)))
    in_specs, operands, aliases = [a_spec, b_spec], [a, b], {}
    if stack is None:
        out_spec = pl.BlockSpec((tm, tn), lambda i, j, k: (i, j))
        out_shape = jax.ShapeDtypeStruct((m, n), out_dtype)
    else:
        buf, depth, layer = stack
        out_spec = pl.BlockSpec((None, tm, tn), lambda i, j, k: (layer, i, j))
        out_shape = jax.ShapeDtypeStruct((depth, m, n), out_dtype)
        if buf is not None:
            in_specs.append(pl.BlockSpec(memory_space=pl.ANY))
            operands.append(buf)
            aliases = {2: 0}
    return pl.pallas_call(
        body, name=name, grid=(m // tm, n // tn, nk),
        in_specs=in_specs, out_specs=out_spec, out_shape=out_shape, scratch_shapes=scratch,
        input_output_aliases=aliases,
        compiler_params=_cparams(("parallel", "parallel", "arbitrary")),
    )(*operands)


def _rowwise(fn, rows, params, outs, pouts=(), *, tm=256, name="rowwise"):
    t = rows[0][0].shape[0]
    assert t % tm == 0
    n_r, n_p, n_o, n_po = len(rows), len(params), len(outs), len(pouts)

    def body(*refs):
        r_refs = refs[:n_r]
        p_refs = refs[n_r:n_r + n_p]
        o_refs = refs[n_r + n_p:n_r + n_p + n_o]
        po_refs = refs[n_r + n_p + n_o:]
        res = fn(*[r[...] for r in r_refs], *[p[...] for p in p_refs])
        if not isinstance(res, (tuple, list)):
            res = (res,)
        assert len(res) == n_o + n_po, (len(res), n_o, n_po)
        for o, val in zip(o_refs, res[:n_o]):
            o[...] = val.astype(o.dtype)
        if n_po:
            first = pl.program_id(0) == 0

            @pl.when(first)
            def _():
                for o, val in zip(po_refs, res[n_o:]):
                    o[...] = val.astype(F32)

            @pl.when(jnp.logical_not(first))
            def _():
                for o, val in zip(po_refs, res[n_o:]):
                    o[...] += val.astype(F32)

    in_specs = [pl.BlockSpec((tm, w), functools.partial(lambda i, cb: (i, cb), cb=cb)) for (_, w, cb) in rows]
    in_specs += [pl.BlockSpec(p.shape, lambda i: (0, 0)) for p in params]
    out_specs = [pl.BlockSpec((tm, w), lambda i: (i, 0)) for (w, _) in outs]
    out_specs += [pl.BlockSpec((1, w), lambda i: (0, 0)) for w in pouts]
    out_shape = [jax.ShapeDtypeStruct((t, w), dt) for (w, dt) in outs]
    out_shape += [jax.ShapeDtypeStruct((1, w), F32) for w in pouts]
    res = pl.pallas_call(
        body, name=name, grid=(t // tm,), in_specs=in_specs, out_specs=out_specs, out_shape=out_shape,
        compiler_params=_cparams(("arbitrary",)),
    )(*[r[0] for r in rows], *params)
    return res


def _full(a):
    return (a, a.shape[1], 0)


def _rms(x, g):
    x = x.astype(F32)
    return x * lax.rsqrt(jnp.mean(x * x, axis=-1, keepdims=True) + EPS) * g


def _sum0(v):
    return jnp.sum(v, axis=0, keepdims=True)


def rms_fwd(h, g, name):
    return _rowwise(lambda x, gg: _rms(x, gg), [_full(h)], [g], [(h.shape[1], BF16)], tm=512, name=name)[0]


def rms_bwd(h, g, dn, dres, name):
    def fn(x, d, r, gg):
        _, vjp = jax.vjp(_rms, x, gg)
        dx, dg = vjp(d.astype(F32))
        return dx + r, dg
    w = h.shape[1]
    return _rowwise(fn, [_full(h), _full(dn), _full(dres)], [g], [(w, F32)], [w], tm=512, name=name)


def res_rms_fwd(h, f, g, coef, name):
    return _rowwise(lambda x, y, gg: x + coef * _rms(y, gg), [_full(h), _full(f)], [g], [(h.shape[1], F32)],
                    tm=512, name=name)[0]


def res_rms_bwd(f, g, dh, coef, name):
    def fn(y, d, gg):
        _, vjp = jax.vjp(lambda a, b: coef * _rms(a, b), y, gg)
        dy, dg = vjp(d)
        return dy, dg
    w = f.shape[1]
    return _rowwise(fn, [_full(f), _full(dh)], [g], [(w, BF16)], [w], tm=512, name=name)


def swiglu_fwd(gu, name):
    f = gu.shape[1] // 2

    def fn(gate, up):
        gate = gate.astype(F32)
        return gate * jax.nn.sigmoid(gate) * up.astype(F32)
    return _rowwise(fn, [(gu, f, 0), (gu, f, 1)], [], [(f, BF16)], name=name)[0]


def swiglu_bwd(gu, da, name):
    t, f2 = gu.shape
    f = f2 // 2
    tm = 256

    def body(gate_ref, up_ref, da_ref, o_ref):
        gate = gate_ref[...].astype(F32)
        up = up_ref[...].astype(F32)
        d = da_ref[...].astype(F32)
        sg = jax.nn.sigmoid(gate)
        silu = gate * sg
        o_ref[:, :f] = (d * up * (sg + silu * (1.0 - sg))).astype(o_ref.dtype)
        o_ref[:, f:] = (d * silu).astype(o_ref.dtype)

    return pl.pallas_call(
        body, name=name, grid=(t // tm,),
        in_specs=[pl.BlockSpec((tm, f), lambda i: (i, 0)), pl.BlockSpec((tm, f), lambda i: (i, 1)),
                  pl.BlockSpec((tm, f), lambda i: (i, 0))],
        out_specs=pl.BlockSpec((tm, f2), lambda i: (i, 0)),
        out_shape=jax.ShapeDtypeStruct((t, f2), BF16),
        compiler_params=_cparams(("parallel",)),
    )(gu, gu, da)


def _merge(g0, g1, g2, g3, z0, z1, z2, z3, b0, b1, b2, b3):
    acc = jax.nn.sigmoid(g0.astype(F32) + b0) * z0.astype(F32)
    acc += jax.nn.sigmoid(g1.astype(F32) + b1) * z1.astype(F32)
    acc += jax.nn.sigmoid(g2.astype(F32) + b2) * z2.astype(F32)
    acc += jax.nn.sigmoid(g3.astype(F32) + b3) * z3.astype(F32)
    return acc


def merge_fwd(proj, zs, bs, name):
    rows = [(proj, D_MODEL, k) for k in range(4)] + [_full(z) for z in zs]
    return _rowwise(_merge, rows, list(bs), [(D_MODEL, BF16)], name=name)[0]


def merge_bwd(proj, zs, bs, dmerged, name):
    def fn(*args):
        d = args[8].astype(F32)
        prim = args[:8] + args[9:]
        _, vjp = jax.vjp(_merge, *prim)
        return vjp(d)
    rows = [(proj, D_MODEL, k) for k in range(4)] + [_full(z) for z in zs] + [_full(dmerged)]
    outs = [(D_MODEL, BF16)] * 8
    return _rowwise(fn, rows, list(bs), outs, [D_MODEL] * 4, name=name)


def _ple(pgl, pr, g):
    return jax.nn.sigmoid(pgl.astype(F32)) * _rms(pr, g)


def ple_fwd(h, pgl, pr, g, name):
    return _rowwise(lambda x, a, b, gg: x + _ple(a, b, gg), [_full(h), _full(pgl), _full(pr)], [g],
                    [(D_MODEL, F32)], tm=512, name=name)[0]


def ple_bwd(pgl, pr, g, dh, name):
    def fn(a, b, d, gg):
        _, vjp = jax.vjp(_ple, a, b, gg)
        return vjp(d)
    return _rowwise(fn, [_full(pgl), _full(pr), _full(dh)], [g], [(D_MODEL, BF16), (D_MODEL, BF16)], [D_MODEL],
                    tm=512, name=name)


def loss_fwd_bwd(y, target, name):
    def fn(a, b):
        err = a - b
        return err * (1.0 / D_MODEL), _sum0(err * err) * (0.5 / D_MODEL)
    return _rowwise(fn, [_full(y), _full(target)], [], [(D_MODEL, F32)], [D_MODEL], tm=512, name=name)


def _shift_down(x, d, row):
    if d == 0:
        return x
    return jnp.where(row >= d, pltpu.roll(x, d, 0), 0.0)


def _shift_up(x, d, row):
    if d == 0:
        return x
    s = x.shape[0]
    return jnp.where(row < s - d, pltpu.roll(x, s - d, 0), 0.0)


def fgate_fwd(proj, bf, bn, s, name):
    def body(a_ref, b_ref, o_ref):
        x = a_ref[...].astype(F32) + b_ref[...]
        c = jnp.minimum(x, 0.0) - jnp.log(1.0 + jnp.exp(-jnp.abs(x)))
        row = lax.broadcasted_iota(jnp.int32, c.shape, 0)
        sh = 1
        while sh < s:
            c = c + _shift_down(c, sh, row)
            sh *= 2
        o_ref[...] = c

    return pl.pallas_call(
        body, name=name, grid=(bn,),
        in_specs=[pl.BlockSpec((s, 128), lambda b: (b, C_AF // 128)), pl.BlockSpec((1, 128), lambda b: (0, 0))],
        out_specs=pl.BlockSpec((s, 128), lambda b: (b, 0)),
        out_shape=jax.ShapeDtypeStruct((bn * s, 128), F32),
        compiler_params=_cparams(("parallel",)),
    )(proj, bf)


def fgate_bwd(proj, bf, dc, bn, s, name):
    def body(a_ref, b_ref, dc_ref, da_ref, db_ref):
        x = a_ref[...].astype(F32) + b_ref[...]
        d = dc_ref[...]
        row = lax.broadcasted_iota(jnp.int32, d.shape, 0)
        sh = 1
        while sh < s:
            d = d + _shift_up(d, sh, row)
            sh *= 2
        da = d * jax.nn.sigmoid(-x)
        da_ref[...] = da.astype(da_ref.dtype)
        first = pl.program_id(0) == 0

        @pl.when(first)
        def _():
            db_ref[...] = _sum0(da)

        @pl.when(jnp.logical_not(first))
        def _():
            db_ref[...] += _sum0(da)

    return pl.pallas_call(
        body, name=name, grid=(bn,),
        in_specs=[pl.BlockSpec((s, 128), lambda b: (b, C_AF // 128)), pl.BlockSpec((1, 128), lambda b: (0, 0)),
                  pl.BlockSpec((s, 128), lambda b: (b, 0))],
        out_specs=[pl.BlockSpec((s, 128), lambda b: (b, 0)), pl.BlockSpec((1, 128), lambda b: (0, 0))],
        out_shape=[jax.ShapeDtypeStruct((bn * s, 128), BF16), jax.ShapeDtypeStruct((1, 128), F32)],
        compiler_params=_cparams(("arbitrary",)),
    )(proj, bf, dc)


FOX_T = 512


def _fox_scores(q, k, cq, ck, j, i):
    t = FOX_T
    s = lax.dot_general(q, k, (((1,), (1,)), ((), ())), preferred_element_type=F32) * SCALE
    qpos = j * t + lax.broadcasted_iota(jnp.int32, (t, t), 0)
    kpos = i * t + lax.broadcasted_iota(jnp.int32, (t, t), 1)
    return jnp.where(qpos >= kpos, s + (cq - ck), NEG)


def fox_fwd(q, k, v, c_col, c_row, name):
    bn, h, s, d = q.shape
    t = FOX_T
    nq = s // t

    def body(q_ref, k_ref, v_ref, cq_ref, ck_ref, o_ref, lse_ref):
        j = pl.program_id(2)
        qv = q_ref[...]
        cq = cq_ref[...]

        def step(i, carry):
            m, l, acc = carry
            ks = pl.multiple_of(i * t, t)
            kc = k_ref[pl.ds(ks, t), :]
            vc = v_ref[pl.ds(ks, t), :]
            sc = _fox_scores(qv, kc, cq, ck_ref[i], j, i)
            m_new = jnp.maximum(m, jnp.max(sc, axis=-1, keepdims=True))
            alpha = jnp.exp(m - m_new)
            p = jnp.exp(sc - m_new)
            l = alpha * l + jnp.sum(p, axis=-1, keepdims=True)
            acc = alpha * acc + jnp.dot(p.astype(BF16), vc, preferred_element_type=F32)
            return m_new, l, acc

        init = (jnp.full((t, 1), NEG, F32), jnp.zeros((t, 1), F32), jnp.zeros((t, d), F32))
        m, l, acc = lax.fori_loop(0, j + 1, step, init)
        o_ref[...] = (acc / l).astype(o_ref.dtype)
        lse_ref[...] = m + jnp.log(l)

    blk_q = pl.BlockSpec((None, None, t, d), lambda b, hh, j: (b, hh, j, 0))
    blk_kv = pl.BlockSpec((None, None, s, d), lambda b, hh, j: (b, hh, 0, 0))
    blk_c1 = pl.BlockSpec((None, None, t, 1), lambda b, hh, j: (b, hh, j, 0))
    blk_cr = pl.BlockSpec((None, None, nq, 1, t), lambda b, hh, j: (b, hh, 0, 0, 0))
    return pl.pallas_call(
        body, name=name, grid=(bn, h, nq),
        in_specs=[blk_q, blk_kv, blk_kv, blk_c1, blk_cr],
        out_specs=[blk_q, blk_c1],
        out_shape=[jax.ShapeDtypeStruct((bn, h, s, d), F32), jax.ShapeDtypeStruct((bn, h, s, 1), F32)],
        compiler_params=_cparams(("parallel", "parallel", "arbitrary")),
    )(q, k, v, c_col, c_row)


def fox_bwd(q, k, v, c_col, c_row, o, lse, do, name):
    bn, h, s, d = q.shape
    t = FOX_T
    nq = s // t

    def body(q_ref, k_ref, v_ref, cq_ref, ck_ref, o_ref, lse_ref, do_ref, dq_ref, dk_ref, dv_ref, dck_ref,
             dcq_ref):
        j = pl.program_id(2)

        @pl.when(j == 0)
        def _():
            dk_ref[...] = jnp.zeros_like(dk_ref)
            dv_ref[...] = jnp.zeros_like(dv_ref)
            dck_ref[...] = jnp.zeros_like(dck_ref)

        qv = q_ref[...]
        cq = cq_ref[...]
        dov = do_ref[...]
        lse = lse_ref[...]
        delta = jnp.sum(dov.astype(F32) * o_ref[...].astype(F32), axis=-1, keepdims=True)

        def step(i, carry):
            dq, dcq = carry
            ks = pl.multiple_of(i * t, t)
            kc = k_ref[pl.ds(ks, t), :]
            vc = v_ref[pl.ds(ks, t), :]
            sc = _fox_scores(qv, kc, cq, ck_ref[i], j, i)
            p = jnp.exp(sc - lse)
            dp = lax.dot_general(dov, vc, (((1,), (1,)), ((), ())), preferred_element_type=F32)
            ds = p * (dp - delta)
            dsb = ds.astype(BF16)
            dq = dq + jnp.dot(dsb, kc, preferred_element_type=F32) * SCALE
            dk_ref[pl.ds(ks, t), :] += lax.dot_general(dsb, qv, (((0,), (0,)), ((), ())),
                                                       preferred_element_type=F32) * SCALE
            dv_ref[pl.ds(ks, t), :] += lax.dot_general(p.astype(BF16), dov, (((0,), (0,)), ((), ())),
                                                       preferred_element_type=F32)
            dck_ref[i] += -_sum0(ds)
            return dq, dcq + jnp.sum(ds, axis=-1, keepdims=True)

        dq, dcq = lax.fori_loop(0, j + 1, step, (jnp.zeros((t, d), F32), jnp.zeros((t, 1), F32)))
        dq_ref[...] = dq
        dcq_ref[...] = dcq

    blk_q = pl.BlockSpec((None, None, t, d), lambda b, hh, j: (b, hh, j, 0))
    blk_kv = pl.BlockSpec((None, None, s, d), lambda b, hh, j: (b, hh, 0, 0))
    blk_c1 = pl.BlockSpec((None, None, t, 1), lambda b, hh, j: (b, hh, j, 0))
    blk_cr = pl.BlockSpec((None, None, nq, 1, t), lambda b, hh, j: (b, hh, 0, 0, 0))
    return pl.pallas_call(
        body, name=name, grid=(bn, h, nq),
        in_specs=[blk_q, blk_kv, blk_kv, blk_c1, blk_cr, blk_q, blk_c1, blk_q],
        out_specs=[blk_q, blk_kv, blk_kv, blk_cr, blk_c1],
        out_shape=[jax.ShapeDtypeStruct((bn, h, s, d), F32), jax.ShapeDtypeStruct((bn, h, s, d), F32),
                   jax.ShapeDtypeStruct((bn, h, s, d), F32), jax.ShapeDtypeStruct((bn, h, nq, 1, t), F32),
                   jax.ShapeDtypeStruct((bn, h, s, 1), F32)],
        compiler_params=_cparams(("parallel", "parallel", "arbitrary")),
    )(q, k, v, c_col, c_row, o, lse, do)


def _swa_valid(n):
    qi = lax.broadcasted_iota(jnp.int32, (QB, 2 * QB), 0)
    kj = lax.broadcasted_iota(jnp.int32, (QB, 2 * QB), 1)
    dist = qi + QB - kj
    return (dist >= 0) & (dist < WINDOW) & ((kj >= QB) | (n > 0))


def _swa_band(ref, n):
    qs = pl.multiple_of(n * QB, QB)
    ps = pl.multiple_of(jnp.maximum(n - 1, 0) * QB, QB)
    return jnp.concatenate([ref[pl.ds(ps, QB), :], ref[pl.ds(qs, QB), :]], axis=0), qs, ps


def swa_fwd(q, k, v, bias, sinks, name):
    bn, hq, s, d = q.shape
    nb = s // QB

    def body(q_ref, k_ref, v_ref, b_ref, s_ref, o_ref, lse_ref):
        def step(n, _):
            kb, qs, _ps = _swa_band(k_ref, n)
            vb, _, _ = _swa_band(v_ref, n)
            valid = _swa_valid(n)
            for g in range(D_GROUP):
                qg = q_ref[g, pl.ds(qs, QB), :]
                sc = lax.dot_general(qg, kb, (((1,), (1,)), ((), ())), preferred_element_type=F32) * SCALE
                sc = jnp.where(valid, sc + b_ref[g], NEG)
                sink = s_ref[g]
                m = jnp.maximum(jnp.max(sc, axis=-1, keepdims=True), sink)
                e = jnp.exp(sc - m)
                z = jnp.sum(e, axis=-1, keepdims=True) + jnp.exp(sink - m)
                p = e / z
                o_ref[g, pl.ds(qs, QB), :] = jnp.dot(p.astype(BF16), vb, preferred_element_type=F32
                                                     ).astype(o_ref.dtype)
                lse_ref[g, pl.ds(qs, QB), :] = m + jnp.log(z)
            return 0

        lax.fori_loop(0, nb, step, 0, unroll=2)

    blk_q = pl.BlockSpec((None, D_GROUP, s, d), lambda b, kh: (b, kh, 0, 0))
    blk_kv = pl.BlockSpec((None, None, s, d), lambda b, kh: (b, kh, 0, 0))
    blk_l = pl.BlockSpec((None, D_GROUP, s, 1), lambda b, kh: (b, kh, 0, 0))
    return pl.pallas_call(
        body, name=name, grid=(bn, D_KV_HEADS),
        in_specs=[blk_q, blk_kv, blk_kv, pl.BlockSpec((D_GROUP, QB, 2 * QB), lambda b, kh: (kh, 0, 0)),
                  pl.BlockSpec((D_GROUP, QB, 1), lambda b, kh: (kh, 0, 0))],
        out_specs=[blk_q, blk_l],
        out_shape=[jax.ShapeDtypeStruct((bn, hq, s, d), BF16), jax.ShapeDtypeStruct((bn, hq, s, 1), F32)],
        compiler_params=_cparams(("parallel", "parallel")),
    )(q, k, v, bias, sinks)


def swa_bwd(q, k, v, bias, sinks, o, lse, do, name):
    bn, hq, s, d = q.shape
    nb = s // QB

    def body(q_ref, k_ref, v_ref, b_ref, s_ref, o_ref, lse_ref, do_ref, dq_ref, dk_ref, dv_ref, db_ref, dsk_ref):
        @pl.when(pl.program_id(1) == 0)
        def _():
            db_ref[...] = jnp.zeros_like(db_ref)
            dsk_ref[...] = jnp.zeros_like(dsk_ref)

        dk_ref[...] = jnp.zeros_like(dk_ref)
        dv_ref[...] = jnp.zeros_like(dv_ref)

        def step(n, _):
            kb, qs, ps = _swa_band(k_ref, n)
            vb, _, _ = _swa_band(v_ref, n)
            valid = _swa_valid(n)
            dkb = jnp.zeros((2 * QB, d), F32)
            dvb = jnp.zeros((2 * QB, d), F32)
            for g in range(D_GROUP):
                qg = q_ref[g, pl.ds(qs, QB), :]
                dog = do_ref[g, pl.ds(qs, QB), :]
                og = o_ref[g, pl.ds(qs, QB), :]
                lse = lse_ref[g, pl.ds(qs, QB), :]
                sc = lax.dot_general(qg, kb, (((1,), (1,)), ((), ())), preferred_element_type=F32) * SCALE
                sc = jnp.where(valid, sc + b_ref[g], NEG)
                p = jnp.exp(sc - lse)
                delta = jnp.sum(dog.astype(F32) * og.astype(F32), axis=-1, keepdims=True)
                dp = lax.dot_general(dog, vb, (((1,), (1,)), ((), ())), preferred_element_type=F32)
                ds = p * (dp - delta)
                dsb = ds.astype(BF16)
                dq_ref[g, pl.ds(qs, QB), :] = jnp.dot(dsb, kb, preferred_element_type=F32) * SCALE
                dkb = dkb + lax.dot_general(dsb, qg, (((0,), (0,)), ((), ())), preferred_element_type=F32) * SCALE
                dvb = dvb + lax.dot_general(p.astype(BF16), dog, (((0,), (0,)), ((), ())),
                                            preferred_element_type=F32)
                db_ref[g] += ds
                dsk_ref[g] += -jnp.exp(s_ref[g] - lse) * delta
            dk_ref[pl.ds(ps, QB), :] += dkb[:QB]
            dk_ref[pl.ds(qs, QB), :] += dkb[QB:]
            dv_ref[pl.ds(ps, QB), :] += dvb[:QB]
            dv_ref[pl.ds(qs, QB), :] += dvb[QB:]
            return 0

        lax.fori_loop(0, nb, step, 0, unroll=2)

    blk_q = pl.BlockSpec((None, D_GROUP, s, d), lambda kh, b: (b, kh, 0, 0))
    blk_kv = pl.BlockSpec((None, None, s, d), lambda kh, b: (b, kh, 0, 0))
    blk_l = pl.BlockSpec((None, D_GROUP, s, 1), lambda kh, b: (b, kh, 0, 0))
    blk_b = pl.BlockSpec((D_GROUP, QB, 2 * QB), lambda kh, b: (kh, 0, 0))
    blk_s = pl.BlockSpec((D_GROUP, QB, 1), lambda kh, b: (kh, 0, 0))
    return pl.pallas_call(
        body, name=name, grid=(D_KV_HEADS, bn),
        in_specs=[blk_q, blk_kv, blk_kv, blk_b, blk_s, blk_q, blk_l, blk_q],
        out_specs=[blk_q, blk_kv, blk_kv, blk_b, blk_s],
        out_shape=[jax.ShapeDtypeStruct((bn, hq, s, d), F32), jax.ShapeDtypeStruct((bn, D_KV_HEADS, s, d), F32),
                   jax.ShapeDtypeStruct((bn, D_KV_HEADS, s, d), F32),
                   jax.ShapeDtypeStruct((hq, QB, 2 * QB), F32), jax.ShapeDtypeStruct((hq, QB, 1), F32)],
        compiler_params=_cparams(("parallel", "arbitrary")),
    )(q, k, v, bias, sinks, o, lse, do)


def _sel(nh, width):
    r = lax.broadcasted_iota(jnp.int32, (width, HEAD_DIM), 0)
    c = lax.broadcasted_iota(jnp.int32, (width, HEAD_DIM), 1)
    return [(r == c + HEAD_DIM * h).astype(BF16) for h in range(nh)]


def _pick_head(x, e):
    return jnp.dot(x, e, preferred_element_type=F32).astype(BF16)


def _place_head(x, e):
    return lax.dot_general(x.astype(BF16), e, (((1,), (1,)), ((), ())), preferred_element_type=F32)


def fox2_fwd(proj, c_col, c_row, bn, s, name):
    t = FOX_T
    nq = s // t
    nh, d = A_HEADS, HEAD_DIM

    def body(q_ref, k_ref, v_ref, cq_ref, ck_ref, y_ref, o_ref, lse_ref, kh_ref, vh_ref):
        j = pl.program_id(1)
        es = _sel(nh, 256)

        @pl.when(j == 0)
        def _():
            for h in range(nh):
                kh_ref[h] = _pick_head(k_ref[...], es[h])
                vh_ref[h] = _pick_head(v_ref[...], es[h])

        q4 = q_ref[...]
        qs = [_pick_scaled(q4, es[h]) for h in range(nh)]
        cqs = [cq_ref[h] for h in range(nh)]

        def chunk(i, carry, masked):
            ks = pl.multiple_of(i * t, t)
            out = []
            for h in range(nh):
                m, l, acc = carry[h]
                sc = _fox_scores2(qs[h], kh_ref[h, pl.ds(ks, t), :], cqs[h], ck_ref[h, i], masked)
                m_new = jnp.maximum(m, jnp.max(sc, axis=-1, keepdims=True))
                alpha = jnp.exp(m - m_new)
                p = jnp.exp(sc - m_new)
                l = alpha * l + jnp.sum(p, axis=-1, keepdims=True)
                acc = alpha * acc + jnp.dot(p.astype(BF16), vh_ref[h, pl.ds(ks, t), :], preferred_element_type=F32)
                out.append((m_new, l, acc))
            return tuple(out)

        init = tuple((jnp.full((t, 1), NEG, F32), jnp.zeros((t, 1), F32), jnp.zeros((t, d), F32)) for _ in range(nh))
        res = lax.fori_loop(0, j, lambda i, carry: chunk(i, carry, False), init)
        res = chunk(j, res, True)
        y = jnp.zeros((t, 256), F32)
        for h in range(nh):
            m, l, acc = res[h]
            o = acc / l
            o_ref[h] = o
            lse_ref[h] = m + jnp.log(l)
            y = y + _place_head(o, es[h])
        y_ref[...] = y.astype(y_ref.dtype)

    blk_q = pl.BlockSpec((t, 256), lambda b, j: (b * nq + j, C_AQ // 256))
    blk_k = pl.BlockSpec((s, 256), lambda b, j: (b, C_AK // 256))
    blk_v = pl.BlockSpec((s, 256), lambda b, j: (b, C_AV // 256))
    blk_c1 = pl.BlockSpec((None, nh, t, 1), lambda b, j: (b, 0, j, 0))
    blk_cr = pl.BlockSpec((None, nh, nq, 1, t), lambda b, j: (b, 0, 0, 0, 0))
    blk_o = pl.BlockSpec((None, nh, t, d), lambda b, j: (b, 0, j, 0))
    return pl.pallas_call(
        body, name=name, grid=(bn, nq),
        in_specs=[blk_q, blk_k, blk_v, blk_c1, blk_cr],
        out_specs=[pl.BlockSpec((t, 256), lambda b, j: (b * nq + j, 0)), blk_o, blk_c1],
        out_shape=[jax.ShapeDtypeStruct((bn * s, 256), BF16), jax.ShapeDtypeStruct((bn, nh, s, d), F32),
                   jax.ShapeDtypeStruct((bn, nh, s, 1), F32)],
        scratch_shapes=[pltpu.VMEM((nh, s, d), BF16), pltpu.VMEM((nh, s, d), BF16)],
        compiler_params=_cparams(("arbitrary", "arbitrary")),
    )(proj, proj, proj, c_col, c_row)


def fox2_bwd(proj, c_col, c_row, o, lse, dya, bn, s, name):
    t = FOX_T
    nq = s // t
    nh, d = A_HEADS, HEAD_DIM

    def body(q_ref, k_ref, v_ref, cq_ref, ck_ref, o_ref, lse_ref, dy_ref, dq_ref, dk_ref, dv_ref, dck_ref, dcq_ref,
             kh_ref, vh_ref, dkh_ref, dvh_ref):
        j = pl.program_id(1)
        es = _sel(nh, 256)

        @pl.when(j == 0)
        def _():
            for h in range(nh):
                kh_ref[h] = _pick_head(k_ref[...], es[h])
                vh_ref[h] = _pick_head(v_ref[...], es[h])
            dkh_ref[...] = jnp.zeros_like(dkh_ref)
            dvh_ref[...] = jnp.zeros_like(dvh_ref)
            dck_ref[...] = jnp.zeros_like(dck_ref)

        q4 = q_ref[...]
        dy4 = dy_ref[...]
        qs = [_pick_scaled(q4, es[h]) for h in range(nh)]
        dos = [_pick_head(dy4, es[h]) for h in range(nh)]
        cqs = [cq_ref[h] for h in range(nh)]
        lses = [lse_ref[h] for h in range(nh)]
        deltas = [jnp.sum(dos[h].astype(F32) * o_ref[h], axis=-1, keepdims=True) for h in range(nh)]

        def chunk(i, carry, masked):
            ks = pl.multiple_of(i * t, t)
            out = []
            for h in range(nh):
                dq, dcq = carry[h]
                kc = kh_ref[h, pl.ds(ks, t), :]
                sc = _fox_scores2(qs[h], kc, cqs[h], ck_ref[h, i], masked)
                p = jnp.exp(sc - lses[h])
                dp = lax.dot_general(dos[h], vh_ref[h, pl.ds(ks, t), :], (((1,), (1,)), ((), ())),
                                     preferred_element_type=F32)
                ds = p * (dp - deltas[h])
                dsb = ds.astype(BF16)
                dq = dq + jnp.dot(dsb, kc, preferred_element_type=F32)
                dkh_ref[h, pl.ds(ks, t), :] += lax.dot_general(dsb, qs[h], (((0,), (0,)), ((), ())),
                                                               preferred_element_type=F32)
                dvh_ref[h, pl.ds(ks, t), :] += lax.dot_general(p.astype(BF16), dos[h], (((0,), (0,)), ((), ())),
                                                               preferred_element_type=F32)
                dck_ref[h, i] += -_sum0(ds)
                out.append((dq, dcq + jnp.sum(ds, axis=-1, keepdims=True)))
            return tuple(out)

        init = tuple((jnp.zeros((t, d), F32), jnp.zeros((t, 1), F32)) for _ in range(nh))
        res = lax.fori_loop(0, j, lambda i, carry: chunk(i, carry, False), init)
        res = chunk(j, res, True)
        dq4 = jnp.zeros((t, 256), F32)
        for h in range(nh):
            dq4 = dq4 + _place_head(res[h][0] * SCALE, es[h])
            dcq_ref[h] = res[h][1]
        dq_ref[...] = dq4.astype(dq_ref.dtype)

        @pl.when(j == nq - 1)
        def _():
            dk4 = jnp.zeros((s, 256), F32)
            dv4 = jnp.zeros((s, 256), F32)
            for h in range(nh):
                dk4 = dk4 + _place_head(dkh_ref[h], es[h])
                dv4 = dv4 + _place_head(dvh_ref[h], es[h])
            dk_ref[...] = dk4.astype(dk_ref.dtype)
            dv_ref[...] = dv4.astype(dv_ref.dtype)

    blk_q = pl.BlockSpec((t, 256), lambda b, j: (b * nq + j, C_AQ // 256))
    blk_k = pl.BlockSpec((s, 256), lambda b, j: (b, C_AK // 256))
    blk_v = pl.BlockSpec((s, 256), lambda b, j: (b, C_AV // 256))
    blk_c1 = pl.BlockSpec((None, nh, t, 1), lambda b, j: (b, 0, j, 0))
    blk_cr = pl.BlockSpec((None, nh, nq, 1, t), lambda b, j: (b, 0, 0, 0, 0))
    blk_o = pl.BlockSpec((None, nh, t, d), lambda b, j: (b, 0, j, 0))
    blk_t = pl.BlockSpec((t, 256), lambda b, j: (b * nq + j, 0))
    blk_s = pl.BlockSpec((s, 256), lambda b, j: (b, 0))
    return pl.pallas_call(
        body, name=name, grid=(bn, nq),
        in_specs=[blk_q, blk_k, blk_v, blk_c1, blk_cr, blk_o, blk_c1, blk_t],
        out_specs=[blk_t, blk_s, blk_s, blk_cr, blk_c1],
        out_shape=[jax.ShapeDtypeStruct((bn * s, 256), BF16)] * 3
        + [jax.ShapeDtypeStruct((bn, nh, nq, 1, t), F32), jax.ShapeDtypeStruct((bn, nh, s, 1), F32)],
        scratch_shapes=[pltpu.VMEM((nh, s, d), BF16), pltpu.VMEM((nh, s, d), BF16),
                        pltpu.VMEM((nh, s, d), F32), pltpu.VMEM((nh, s, d), F32)],
        compiler_params=_cparams(("arbitrary", "arbitrary")),
    )(proj, proj, proj, c_col, c_row, o, lse, dya)


def _band3(ref, h, n):
    qs = pl.multiple_of(n * QB, QB)
    ps = pl.multiple_of(jnp.maximum(n - 1, 0) * QB, QB)
    return jnp.concatenate([ref[h, pl.ds(ps, QB), :], ref[h, pl.ds(qs, QB), :]], axis=0), qs, ps


def swa2_fwd(proj, bias, sinks, bn, s, name):
    nb = s // QB
    d = HEAD_DIM

    def body(q0_ref, q1_ref, k_ref, v_ref, b_ref, s_ref, y_ref, lse_ref, qh_ref, kh_ref, vh_ref):
        e4 = _sel(D_GROUP, 256)
        e2 = _sel(D_KV_HEADS, 128)
        for kh, q_ref in enumerate((q0_ref, q1_ref)):
            kh_ref[kh] = _pick_head(k_ref[...], e2[kh])
            vh_ref[kh] = _pick_head(v_ref[...], e2[kh])
            for g in range(D_GROUP):
                qh_ref[D_GROUP * kh + g] = _pick_head(q_ref[...], e4[g])

        def step(n, _):
            valid = _swa_valid(n)
            for kh in range(D_KV_HEADS):
                kb, qs, _ps = _band3(kh_ref, kh, n)
                vb, _, _ = _band3(vh_ref, kh, n)
                y = jnp.zeros((QB, 256), F32)
                for g in range(D_GROUP):
                    hh = D_GROUP * kh + g
                    qg = qh_ref[hh, pl.ds(qs, QB), :]
                    sc = lax.dot_general(qg, kb, (((1,), (1,)), ((), ())), preferred_element_type=F32) * SCALE
                    sc = jnp.where(valid, sc + b_ref[hh], NEG)
                    sink = s_ref[hh]
                    m = jnp.maximum(jnp.max(sc, axis=-1, keepdims=True), sink)
                    e = jnp.exp(sc - m)
                    z = jnp.sum(e, axis=-1, keepdims=True) + jnp.exp(sink - m)
                    o = jnp.dot((e / z).astype(BF16), vb, preferred_element_type=F32)
                    lse_ref[hh, pl.ds(qs, QB), :] = m + jnp.log(z)
                    y = y + _place_head(o, e4[g])
                y_ref[pl.ds(qs, QB), 256 * kh:256 * (kh + 1)] = y.astype(y_ref.dtype)
            return 0

        lax.fori_loop(0, nb, step, 0, unroll=2)

    return pl.pallas_call(
        body, name=name, grid=(bn,),
        in_specs=[pl.BlockSpec((s, 256), lambda b: (b, C_DQ // 256)), pl.BlockSpec((s, 256), lambda b: (b, C_DQ // 256 + 1)),
                  pl.BlockSpec((s, 128), lambda b: (b, C_DK // 128)), pl.BlockSpec((s, 128), lambda b: (b, C_DV // 128)),
                  pl.BlockSpec((D_Q_HEADS, QB, 2 * QB), lambda b: (0, 0, 0)),
                  pl.BlockSpec((D_Q_HEADS, QB, 1), lambda b: (0, 0, 0))],
        out_specs=[pl.BlockSpec((s, 512), lambda b: (b, 0)), pl.BlockSpec((None, D_Q_HEADS, s, 1), lambda b: (b, 0, 0, 0))],
        out_shape=[jax.ShapeDtypeStruct((bn * s, 512), BF16), jax.ShapeDtypeStruct((bn, D_Q_HEADS, s, 1), F32)],
        scratch_shapes=[pltpu.VMEM((D_Q_HEADS, s, d), BF16), pltpu.VMEM((D_KV_HEADS, s, d), BF16),
                        pltpu.VMEM((D_KV_HEADS, s, d), BF16)],
        compiler_params=_cparams(("parallel",)),
    )(proj, proj, proj, proj, bias, sinks)


def swa2_bwd(proj, bias, sinks, yd, lse, dyd, bn, s, name):
    nb = s // QB
    d = HEAD_DIM

    def body(q0_ref, q1_ref, k_ref, v_ref, b_ref, s_ref, y_ref, lse_ref, dy_ref, dq_ref, dk_ref, dv_ref, db_ref,
             dsk_ref, qh_ref, kh_ref, vh_ref, oh_ref, doh_ref, dkh_ref, dvh_ref):
        @pl.when(pl.program_id(0) == 0)
        def _():
            db_ref[...] = jnp.zeros_like(db_ref)
            dsk_ref[...] = jnp.zeros_like(dsk_ref)

        e4 = _sel(D_GROUP, 256)
        e2 = _sel(D_KV_HEADS, 128)
        for kh, q_ref in enumerate((q0_ref, q1_ref)):
            kh_ref[kh] = _pick_head(k_ref[...], e2[kh])
            vh_ref[kh] = _pick_head(v_ref[...], e2[kh])
            for g in range(D_GROUP):
                hh = D_GROUP * kh + g
                qh_ref[hh] = _pick_head(q_ref[...], e4[g])
                oh_ref[hh] = _pick_head(y_ref[:, 256 * kh:256 * (kh + 1)], e4[g])
                doh_ref[hh] = _pick_head(dy_ref[:, 256 * kh:256 * (kh + 1)], e4[g])
        dkh_ref[...] = jnp.zeros_like(dkh_ref)
        dvh_ref[...] = jnp.zeros_like(dvh_ref)

        def step(n, _):
            valid = _swa_valid(n)
            for kh in range(D_KV_HEADS):
                kb, qs, ps = _band3(kh_ref, kh, n)
                vb, _, _ = _band3(vh_ref, kh, n)
                dkb = jnp.zeros((2 * QB, d), F32)
                dvb = jnp.zeros((2 * QB, d), F32)
                dq4 = jnp.zeros((QB, 256), F32)
                for g in range(D_GROUP):
                    hh = D_GROUP * kh + g
                    qg = qh_ref[hh, pl.ds(qs, QB), :]
                    dog = doh_ref[hh, pl.ds(qs, QB), :]
                    og = oh_ref[hh, pl.ds(qs, QB), :]
                    lse = lse_ref[hh, pl.ds(qs, QB), :]
                    sc = lax.dot_general(qg, kb, (((1,), (1,)), ((), ())), preferred_element_type=F32) * SCALE
                    sc = jnp.where(valid, sc + b_ref[hh], NEG)
                    p = jnp.exp(sc - lse)
                    delta = jnp.sum(dog.astype(F32) * og.astype(F32), axis=-1, keepdims=True)
                    dp = lax.dot_general(dog, vb, (((1,), (1,)), ((), ())), preferred_element_type=F32)
                    ds = p * (dp - delta)
                    dsb = ds.astype(BF16)
                    dq4 = dq4 + _place_head(jnp.dot(dsb, kb, preferred_element_type=F32) * SCALE, e4[g])
                    dkb = dkb + lax.dot_general(dsb, qg, (((0,), (0,)), ((), ())),
                                                preferred_element_type=F32) * SCALE
                    dvb = dvb + lax.dot_general(p.astype(BF16), dog, (((0,), (0,)), ((), ())),
                                                preferred_element_type=F32)
                    db_ref[hh] += ds
                    dsk_ref[hh] += -jnp.exp(s_ref[hh] - lse) * delta
                dq_ref[pl.ds(qs, QB), 256 * kh:256 * (kh + 1)] = dq4.astype(dq_ref.dtype)
                dkh_ref[kh, pl.ds(ps, QB), :] += dkb[:QB]
                dkh_ref[kh, pl.ds(qs, QB), :] += dkb[QB:]
                dvh_ref[kh, pl.ds(ps, QB), :] += dvb[:QB]
                dvh_ref[kh, pl.ds(qs, QB), :] += dvb[QB:]
            return 0

        lax.fori_loop(0, nb, step, 0, unroll=2)
        dk2 = jnp.zeros((s, 128), F32)
        dv2 = jnp.zeros((s, 128), F32)
        for kh in range(D_KV_HEADS):
            dk2 = dk2 + _place_head(dkh_ref[kh], e2[kh])
            dv2 = dv2 + _place_head(dvh_ref[kh], e2[kh])
        dk_ref[...] = dk2.astype(dk_ref.dtype)
        dv_ref[...] = dv2.astype(dv_ref.dtype)

    blk512 = pl.BlockSpec((s, 512), lambda b: (b, 0))
    blk128 = pl.BlockSpec((s, 128), lambda b: (b, 0))
    blk_b = pl.BlockSpec((D_Q_HEADS, QB, 2 * QB), lambda b: (0, 0, 0))
    blk_s = pl.BlockSpec((D_Q_HEADS, QB, 1), lambda b: (0, 0, 0))
    return pl.pallas_call(
        body, name=name, grid=(bn,),
        in_specs=[pl.BlockSpec((s, 256), lambda b: (b, C_DQ // 256)), pl.BlockSpec((s, 256), lambda b: (b, C_DQ // 256 + 1)),
                  pl.BlockSpec((s, 128), lambda b: (b, C_DK // 128)), pl.BlockSpec((s, 128), lambda b: (b, C_DV // 128)),
                  blk_b, blk_s, blk512, pl.BlockSpec((None, D_Q_HEADS, s, 1), lambda b: (b, 0, 0, 0)), blk512],
        out_specs=[blk512, blk128, blk128, blk_b, blk_s],
        out_shape=[jax.ShapeDtypeStruct((bn * s, 512), BF16), jax.ShapeDtypeStruct((bn * s, 128), BF16),
                   jax.ShapeDtypeStruct((bn * s, 128), BF16),
                   jax.ShapeDtypeStruct((D_Q_HEADS, QB, 2 * QB), F32), jax.ShapeDtypeStruct((D_Q_HEADS, QB, 1), F32)],
        scratch_shapes=[pltpu.VMEM((D_Q_HEADS, s, d), BF16), pltpu.VMEM((D_KV_HEADS, s, d), BF16),
                        pltpu.VMEM((D_KV_HEADS, s, d), BF16), pltpu.VMEM((D_Q_HEADS, s, d), BF16),
                        pltpu.VMEM((D_Q_HEADS, s, d), BF16), pltpu.VMEM((D_KV_HEADS, s, d), F32),
                        pltpu.VMEM((D_KV_HEADS, s, d), F32)],
        compiler_params=_cparams(("arbitrary",)),
    )(proj, proj, proj, proj, bias, sinks, yd, lse, dyd)


def _pick_scaled(x, e):
    return (jnp.dot(x, e, preferred_element_type=F32) * SCALE).astype(BF16)


def _swa_valid4(n):
    qi = lax.broadcasted_iota(jnp.int32, (D_GROUP * QB, 2 * QB), 0) & (QB - 1)
    kj = lax.broadcasted_iota(jnp.int32, (D_GROUP * QB, 2 * QB), 1)
    dist = qi + QB - kj
    return (dist >= 0) & (dist < WINDOW) & ((kj >= QB) | (n > 0))


def _fox_scores2(q, k, cq, ck, masked):
    t = FOX_T
    s = lax.dot_general(q, k, (((1,), (1,)), ((), ())), preferred_element_type=F32) + (cq - ck)
    if masked:
        keep = lax.broadcasted_iota(jnp.int32, (t, t), 0) >= lax.broadcasted_iota(jnp.int32, (t, t), 1)
        s = jnp.where(keep, s, NEG)
    return s


def _sel_at(off, width):
    r = lax.broadcasted_iota(jnp.int32, (width, HEAD_DIM), 0)
    c = lax.broadcasted_iota(jnp.int32, (width, HEAD_DIM), 1)
    return (r == c + off).astype(BF16)


def _eye(n):
    return lax.broadcasted_iota(jnp.int32, (n, n), 0) == lax.broadcasted_iota(jnp.int32, (n, n), 1)


def _row_to_col(row, eye):
    return jnp.sum(jnp.where(eye, row, 0.0), axis=1, keepdims=True)


def _col_to_row(col, eye):
    return jnp.sum(jnp.where(eye, col, 0.0), axis=0, keepdims=True)


def swa3_fwd(proj, bias, sinks, bn, s, name):
    nb = s // QB
    d = HEAD_DIM

    def body(q_ref, k_ref, v_ref, b_ref, s_ref, y_ref, lse_ref, qh_ref, kh_ref, vh_ref):
        kh = pl.program_id(0)
        e4 = _sel(D_GROUP, 256)
        ek = _sel_at(HEAD_DIM * kh, 128)
        eye = _eye(QB)
        kh_ref[...] = _pick_head(k_ref[...], ek)
        vh_ref[...] = _pick_head(v_ref[...], ek)
        for g in range(D_GROUP):
            qh_ref[g] = _pick_scaled(q_ref[...], e4[g])
        bias4 = b_ref[...].reshape(D_GROUP * QB, 2 * QB)
        sink4 = s_ref[...].reshape(D_GROUP * QB, 1)

        def step(n, _):
            valid = _swa_valid4(n)
            kb, qs, _ps = _swa_band(kh_ref, n)
            vb, _, _ = _swa_band(vh_ref, n)
            q4 = jnp.concatenate([qh_ref[g, pl.ds(qs, QB), :] for g in range(D_GROUP)], axis=0)
            sc = lax.dot_general(q4, kb, (((1,), (1,)), ((), ())), preferred_element_type=F32)
            sc = jnp.where(valid, sc + bias4, NEG)
            m = jnp.maximum(jnp.max(sc, axis=-1, keepdims=True), sink4)
            e = jnp.exp(sc - m)
            z = jnp.sum(e, axis=-1, keepdims=True) + jnp.exp(sink4 - m)
            o4 = jnp.dot((e / z).astype(BF16), vb, preferred_element_type=F32)
            lse4 = m + jnp.log(z)
            y = jnp.zeros((QB, 256), F32)
            for g in range(D_GROUP):
                lse_ref[g, n] = _col_to_row(lse4[g * QB:(g + 1) * QB], eye)
                y = y + _place_head(o4[g * QB:(g + 1) * QB], e4[g])
            y_ref[pl.ds(qs, QB), :] = y.astype(y_ref.dtype)
            return 0

        lax.fori_loop(0, nb, step, 0, unroll=2)

    return pl.pallas_call(
        body, name=name, grid=(D_KV_HEADS, bn),
        in_specs=[pl.BlockSpec((s, 256), lambda kh, b: (b, C_DQ // 256 + kh)),
                  pl.BlockSpec((s, 128), lambda kh, b: (b, C_DK // 128)),
                  pl.BlockSpec((s, 128), lambda kh, b: (b, C_DV // 128)),
                  pl.BlockSpec((D_GROUP, QB, 2 * QB), lambda kh, b: (kh, 0, 0)),
                  pl.BlockSpec((D_GROUP, QB, 1), lambda kh, b: (kh, 0, 0))],
        out_specs=[pl.BlockSpec((s, 256), lambda kh, b: (b, kh)),
                   pl.BlockSpec((None, D_GROUP, nb, 1, QB), lambda kh, b: (b, kh, 0, 0, 0))],
        out_shape=[jax.ShapeDtypeStruct((bn * s, 512), BF16), jax.ShapeDtypeStruct((bn, D_Q_HEADS, nb, 1, QB), F32)],
        scratch_shapes=[pltpu.VMEM((D_GROUP, s, d), BF16), pltpu.VMEM((s, d), BF16), pltpu.VMEM((s, d), BF16)],
        compiler_params=_cparams(("parallel", "parallel")),
    )(proj, proj, proj, bias, sinks)


def swa3_bwd(proj, bias, sinks, yd, lse, dyd, bn, s, name):
    nb = s // QB
    d = HEAD_DIM

    def body(q_ref, k_ref, v_ref, b_ref, s_ref, y_ref, lse_ref, dy_ref, dq_ref, dk_ref, dv_ref, db_ref, dsk_ref,
             qh_ref, kh_ref, vh_ref, oh_ref, doh_ref, dkh_ref, dvh_ref):
        kh = pl.program_id(0)

        @pl.when(pl.program_id(1) == 0)
        def _():
            db_ref[...] = jnp.zeros_like(db_ref)
            dsk_ref[...] = jnp.zeros_like(dsk_ref)

        e4 = _sel(D_GROUP, 256)
        ek = _sel_at(HEAD_DIM * kh, 128)
        eye = _eye(QB)
        kh_ref[...] = _pick_head(k_ref[...], ek)
        vh_ref[...] = _pick_head(v_ref[...], ek)
        for g in range(D_GROUP):
            qh_ref[g] = _pick_scaled(q_ref[...], e4[g])
            oh_ref[g] = _pick_head(y_ref[...], e4[g])
            doh_ref[g] = _pick_head(dy_ref[...], e4[g])
        dkh_ref[...] = jnp.zeros_like(dkh_ref)
        dvh_ref[...] = jnp.zeros_like(dvh_ref)
        bias4 = b_ref[...].reshape(D_GROUP * QB, 2 * QB)
        sink4 = s_ref[...].reshape(D_GROUP * QB, 1)

        def stack(ref, qs):
            return jnp.concatenate([ref[g, pl.ds(qs, QB), :] for g in range(D_GROUP)], axis=0)

        def step(n, _):
            valid = _swa_valid4(n)
            kb, qs, ps = _swa_band(kh_ref, n)
            vb, _, _ = _swa_band(vh_ref, n)
            q4, do4, o4 = stack(qh_ref, qs), stack(doh_ref, qs), stack(oh_ref, qs)
            lse4 = jnp.concatenate([_row_to_col(lse_ref[g, n], eye) for g in range(D_GROUP)], axis=0)
            sc = lax.dot_general(q4, kb, (((1,), (1,)), ((), ())), preferred_element_type=F32)
            sc = jnp.where(valid, sc + bias4, NEG)
            p = jnp.exp(sc - lse4)
            delta = jnp.sum(do4.astype(F32) * o4.astype(F32), axis=-1, keepdims=True)
            dp = lax.dot_general(do4, vb, (((1,), (1,)), ((), ())), preferred_element_type=F32)
            ds = p * (dp - delta)
            dsb = ds.astype(BF16)
            dq4s = jnp.dot(dsb, kb, preferred_element_type=F32) * SCALE
            dkb = lax.dot_general(dsb, q4, (((0,), (0,)), ((), ())), preferred_element_type=F32)
            dvb = lax.dot_general(p.astype(BF16), do4, (((0,), (0,)), ((), ())), preferred_element_type=F32)
            db_ref[...] += ds.reshape(D_GROUP, QB, 2 * QB)
            dsk_ref[...] += (-jnp.exp(sink4 - lse4) * delta).reshape(D_GROUP, QB, 1)
            dq4 = jnp.zeros((QB, 256), F32)
            for g in range(D_GROUP):
                dq4 = dq4 + _place_head(dq4s[g * QB:(g + 1) * QB], e4[g])
            dq_ref[pl.ds(qs, QB), :] = dq4.astype(dq_ref.dtype)
            dkh_ref[pl.ds(ps, QB), :] += dkb[:QB]
            dkh_ref[pl.ds(qs, QB), :] += dkb[QB:]
            dvh_ref[pl.ds(ps, QB), :] += dvb[:QB]
            dvh_ref[pl.ds(qs, QB), :] += dvb[QB:]
            return 0

        lax.fori_loop(0, nb, step, 0, unroll=2)
        dk_ref[...] = dkh_ref[...].astype(dk_ref.dtype)
        dv_ref[...] = dvh_ref[...].astype(dv_ref.dtype)

    blk256 = pl.BlockSpec((s, 256), lambda kh, b: (b, kh))
    blk_kv = pl.BlockSpec((None, s, d), lambda kh, b: (kh, b, 0))
    blk_b = pl.BlockSpec((D_GROUP, QB, 2 * QB), lambda kh, b: (kh, 0, 0))
    blk_s = pl.BlockSpec((D_GROUP, QB, 1), lambda kh, b: (kh, 0, 0))
    return pl.pallas_call(
        body, name=name, grid=(D_KV_HEADS, bn),
        in_specs=[pl.BlockSpec((s, 256), lambda kh, b: (b, C_DQ // 256 + kh)),
                  pl.BlockSpec((s, 128), lambda kh, b: (b, C_DK // 128)),
                  pl.BlockSpec((s, 128), lambda kh, b: (b, C_DV // 128)),
                  blk_b, blk_s, blk256,
                  pl.BlockSpec((None, D_GROUP, nb, 1, QB), lambda kh, b: (b, kh, 0, 0, 0)), blk256],
        out_specs=[blk256, blk_kv, blk_kv, blk_b, blk_s],
        out_shape=[jax.ShapeDtypeStruct((bn * s, 512), BF16), jax.ShapeDtypeStruct((D_KV_HEADS, bn * s, d), BF16),
                   jax.ShapeDtypeStruct((D_KV_HEADS, bn * s, d), BF16),
                   jax.ShapeDtypeStruct((D_Q_HEADS, QB, 2 * QB), F32), jax.ShapeDtypeStruct((D_Q_HEADS, QB, 1), F32)],
        scratch_shapes=[pltpu.VMEM((D_GROUP, s, d), BF16), pltpu.VMEM((s, d), BF16), pltpu.VMEM((s, d), BF16),
                        pltpu.VMEM((D_GROUP, s, d), BF16), pltpu.VMEM((D_GROUP, s, d), BF16),
                        pltpu.VMEM((s, d), F32), pltpu.VMEM((s, d), F32)],
        compiler_params=_cparams(("parallel", "arbitrary")),
    )(proj, proj, proj, bias, sinks, yd, lse, dyd)


def assemble_dproj(pieces, dk, dv, daf, name):
    t = pieces[0].shape[0]
    tm = 512
    widths = [p.shape[1] for p in pieces]
    npc = len(pieces)
    assert sum(widths) == C_DK and all(w % 128 == 0 for w in widths)

    def body(*refs):
        p_refs = refs[:npc]
        dk_ref, dv_ref, af_ref, o_ref = refs[npc:]
        off = 0
        for r, w in zip(p_refs, widths):
            o_ref[:, off:off + w] = r[...]
            off += w
        e2 = _sel(D_KV_HEADS, 128)
        for r in (dk_ref, dv_ref):
            val = _place_head(r[0], e2[0]) + _place_head(r[1], e2[1])
            o_ref[:, off:off + 128] = val.astype(o_ref.dtype)
            off += 128
        o_ref[:, off:off + 128] = af_ref[...]
        off += 128
        o_ref[:, off:] = jnp.zeros((tm, N_PROJ - off), o_ref.dtype)

    kv_blk = pl.BlockSpec((D_KV_HEADS, tm, HEAD_DIM), lambda i: (0, i, 0))
    return pl.pallas_call(
        body, name=name, grid=(t // tm,),
        in_specs=[pl.BlockSpec((tm, w), lambda i: (i, 0)) for w in widths]
        + [kv_blk, kv_blk, pl.BlockSpec((tm, 128), lambda i: (i, 0))],
        out_specs=pl.BlockSpec((tm, N_PROJ), lambda i: (i, 0)),
        out_shape=jax.ShapeDtypeStruct((t, N_PROJ), BF16),
        compiler_params=_cparams(("parallel",)),
    )(*pieces, dk, dv, daf)


def _bucket_table():
    dist = jnp.maximum(jnp.arange(QB)[:, None] + QB - jnp.arange(2 * QB)[None, :], 0)
    max_exact = REL_BUCKETS // 2
    large = max_exact + (jnp.log(jnp.maximum(dist, 1).astype(F32) / max_exact)
                         / math.log(REL_MAX_DIST / max_exact) * (REL_BUCKETS - max_exact)).astype(jnp.int32)
    large = jnp.minimum(large, REL_BUCKETS - 1)
    return jnp.where(dist < max_exact, dist, large).astype(F32)


def band_bias_fwd(bucket, rel_bias, name):
    def body(bk_ref, rel_ref, o_ref):
        bk = bk_ref[...]
        for hh in range(D_Q_HEADS):
            acc = jnp.zeros(bk.shape, F32)
            for b in range(REL_BUCKETS):
                acc = jnp.where(bk == float(b), rel_ref[b, hh], acc)
            o_ref[hh] = acc

    return pl.pallas_call(
        body, name=name,
        in_specs=[pl.BlockSpec(memory_space=pltpu.VMEM), pl.BlockSpec(memory_space=pltpu.SMEM)],
        out_specs=pl.BlockSpec(memory_space=pltpu.VMEM),
        out_shape=jax.ShapeDtypeStruct((D_Q_HEADS, QB, 2 * QB), F32),
    )(bucket, rel_bias)


def band_bias_bwd(bucket, dbias_layers, name):
    nl = len(dbias_layers)

    def body(bk_ref, *refs):
        o_ref = refs[nl]
        bk = bk_ref[...]
        for hh in range(D_Q_HEADS):
            tot = refs[0][hh]
            for r in refs[1:nl]:
                tot = tot + r[hh]
            for b in range(REL_BUCKETS):
                part = jnp.sum(jnp.where(bk == float(b), tot, 0.0), axis=0, keepdims=True)
                val = jnp.sum(part, axis=1, keepdims=True)
                o_ref[hh, b:b + 1, :] = jnp.broadcast_to(val, (1, 128))

    return pl.pallas_call(
        body, name=name,
        in_specs=[pl.BlockSpec(memory_space=pltpu.VMEM)] * (nl + 1),
        out_specs=pl.BlockSpec(memory_space=pltpu.VMEM),
        out_shape=jax.ShapeDtypeStruct((D_Q_HEADS, REL_BUCKETS, 128), F32),
    )(bucket, *dbias_layers)


def _proj_blk(s, col):
    return pl.BlockSpec((s, 256), functools.partial(lambda b, cb: (b, cb), cb=col // 256))


def convb_fwd(proj, w, bn, s, name):
    kk = w.shape[0]

    def body(bg_ref, cg_ref, xb_ref, w_ref, o_ref):
        x = cg_ref[...].astype(F32) * xb_ref[...].astype(F32)
        row = lax.broadcasted_iota(jnp.int32, x.shape, 0)
        y = jnp.zeros_like(x)
        for k in range(kk):
            y = y + w_ref[k:k + 1, :] * _shift_down(x, kk - 1 - k, row)
        o_ref[...] = (bg_ref[...].astype(F32) * y).astype(o_ref.dtype)

    return pl.pallas_call(
        body, name=name, grid=(bn,),
        in_specs=[_proj_blk(s, C_BG), _proj_blk(s, C_CG), _proj_blk(s, C_XB), pl.BlockSpec(w.shape, lambda b: (0, 0))],
        out_specs=pl.BlockSpec((s, 256), lambda b: (b, 0)),
        out_shape=jax.ShapeDtypeStruct((bn * s, 256), BF16),
        compiler_params=_cparams(("parallel",)),
    )(proj, proj, proj, w)


def convb_bwd(proj, w, dyb, bn, s, name):
    kk = w.shape[0]

    def body(bg_ref, cg_ref, xb_ref, w_ref, d_ref, dbg_ref, dcg_ref, dxb_ref, dw_ref):
        @pl.when(pl.program_id(0) == 0)
        def _():
            dw_ref[...] = jnp.zeros_like(dw_ref)

        cg = cg_ref[...].astype(F32)
        xb = xb_ref[...].astype(F32)
        d = d_ref[...].astype(F32)
        x = cg * xb
        row = lax.broadcasted_iota(jnp.int32, x.shape, 0)
        dy = d * bg_ref[...].astype(F32)
        y = jnp.zeros_like(x)
        dx = jnp.zeros_like(x)
        for k in range(kk):
            xs = _shift_down(x, kk - 1 - k, row)
            y = y + w_ref[k:k + 1, :] * xs
            dx = dx + w_ref[k:k + 1, :] * _shift_up(dy, kk - 1 - k, row)
            dw_ref[k:k + 1, :] += _sum0(dy * xs)
        dbg_ref[...] = (d * y).astype(dbg_ref.dtype)
        dcg_ref[...] = (dx * xb).astype(dcg_ref.dtype)
        dxb_ref[...] = (dx * cg).astype(dxb_ref.dtype)

    blk = pl.BlockSpec((s, 256), lambda b: (b, 0))
    return pl.pallas_call(
        body, name=name, grid=(bn,),
        in_specs=[_proj_blk(s, C_BG), _proj_blk(s, C_CG), _proj_blk(s, C_XB), pl.BlockSpec(w.shape, lambda b: (0, 0)),
                  blk],
        out_specs=[blk, blk, blk, pl.BlockSpec((8, 256), lambda b: (0, 0))],
        out_shape=[jax.ShapeDtypeStruct((bn * s, 256), BF16)] * 3 + [jax.ShapeDtypeStruct((8, 256), F32)],
        compiler_params=_cparams(("arbitrary",)),
    )(proj, proj, proj, w, dyb)


def _convc_core(ca, cb, w_ref, bias, kk, row, y=None):
    sg = jax.nn.sigmoid(cb)
    glu = ca * sg
    if y is None:
        y = jnp.zeros_like(glu)
        for k in range(kk):
            y = y + w_ref[k:k + 1, :] * _shift_down(glu, kk - 1 - k, row)
        y = y + bias
    mu = jnp.mean(y, axis=-1, keepdims=True)
    xc = y - mu
    r = lax.rsqrt(jnp.mean(xc * xc, axis=-1, keepdims=True) + EPS)
    return sg, glu, xc * r, r, y


def convc_fwd(proj, w, bias, gain, lbias, bn, s, name):
    kk = w.shape[0]

    def body(ca_ref, cb_ref, w_ref, b_ref, g_ref, lb_ref, o_ref, y_ref):
        ca = ca_ref[...].astype(F32)
        row = lax.broadcasted_iota(jnp.int32, ca.shape, 0)
        _, _, xh, _, y = _convc_core(ca, cb_ref[...].astype(F32), w_ref, b_ref[...], kk, row)
        ln = xh * g_ref[...] + lb_ref[...]
        o_ref[...] = (ln * jax.nn.sigmoid(ln)).astype(o_ref.dtype)
        y_ref[...] = y

    vec = pl.BlockSpec((1, 256), lambda b: (0, 0))
    blk = pl.BlockSpec((s, 256), lambda b: (b, 0))
    return pl.pallas_call(
        body, name=name, grid=(bn,),
        in_specs=[_proj_blk(s, C_CA), _proj_blk(s, C_CB), pl.BlockSpec(w.shape, lambda b: (0, 0)), vec, vec, vec],
        out_specs=[blk, blk],
        out_shape=[jax.ShapeDtypeStruct((bn * s, 256), BF16), jax.ShapeDtypeStruct((bn * s, 256), F32)],
        compiler_params=_cparams(("parallel",)),
    )(proj, proj, w, bias, gain, lbias)


def convc_bwd(proj, w, bias, gain, lbias, yconv, dyc, bn, s, name):
    kk = w.shape[0]

    def body(ca_ref, cb_ref, w_ref, b_ref, g_ref, lb_ref, y_ref, d_ref, dca_ref, dcb_ref, dw_ref, db_ref, dg_ref,
             dlb_ref):
        @pl.when(pl.program_id(0) == 0)
        def _():
            dw_ref[...] = jnp.zeros_like(dw_ref)
            db_ref[...] = jnp.zeros_like(db_ref)
            dg_ref[...] = jnp.zeros_like(dg_ref)
            dlb_ref[...] = jnp.zeros_like(dlb_ref)

        ca = ca_ref[...].astype(F32)
        row = lax.broadcasted_iota(jnp.int32, ca.shape, 0)
        sg, glu, xh, r, _ = _convc_core(ca, cb_ref[...].astype(F32), w_ref, b_ref[...], kk, row, y=y_ref[...])
        ln = xh * g_ref[...] + lb_ref[...]
        sl = jax.nn.sigmoid(ln)
        dl = d_ref[...].astype(F32) * (sl + ln * sl * (1.0 - sl))
        dg_ref[...] += _sum0(dl * xh)
        dlb_ref[...] += _sum0(dl)
        dxh = dl * g_ref[...]
        dy = r * (dxh - jnp.mean(dxh, axis=-1, keepdims=True) - xh * jnp.mean(dxh * xh, axis=-1, keepdims=True))
        db_ref[...] += _sum0(dy)
        dglu = jnp.zeros_like(glu)
        for k in range(kk):
            dw_ref[k:k + 1, :] += _sum0(dy * _shift_down(glu, kk - 1 - k, row))
            dglu = dglu + w_ref[k:k + 1, :] * _shift_up(dy, kk - 1 - k, row)
        dca_ref[...] = (dglu * sg).astype(dca_ref.dtype)
        dcb_ref[...] = (dglu * ca * sg * (1.0 - sg)).astype(dcb_ref.dtype)

    vec = pl.BlockSpec((1, 256), lambda b: (0, 0))
    blk = pl.BlockSpec((s, 256), lambda b: (b, 0))
    return pl.pallas_call(
        body, name=name, grid=(bn,),
        in_specs=[_proj_blk(s, C_CA), _proj_blk(s, C_CB), pl.BlockSpec(w.shape, lambda b: (0, 0)), vec, vec, vec, blk,
                  blk],
        out_specs=[blk, blk, pl.BlockSpec((32, 256), lambda b: (0, 0)), vec, vec, vec],
        out_shape=[jax.ShapeDtypeStruct((bn * s, 256), BF16)] * 2 + [jax.ShapeDtypeStruct((32, 256), F32)]
        + [jax.ShapeDtypeStruct((1, 256), F32)] * 3,
        compiler_params=_cparams(("arbitrary",)),
    )(proj, proj, w, bias, gain, lbias, yconv, dyc)


def adamw(w, g, m, v, name):
    shape = w.shape
    cols = shape[-1]
    rows = w.size // cols
    tr = _pick(rows, (256, 128, 64, 32, 16, 8))

    def body(w_ref, g_ref, m_ref, v_ref, d_ref, nm_ref, nv_ref):
        gg = g_ref[...]
        mm = ADAM_B1 * m_ref[...] + (1.0 - ADAM_B1) * gg
        vv = ADAM_B2 * v_ref[...] + (1.0 - ADAM_B2) * jnp.square(gg)
        m_hat = mm / (1.0 - ADAM_B1 ** ADAM_STEP)
        v_hat = vv / (1.0 - ADAM_B2 ** ADAM_STEP)
        d_ref[...] = -ADAM_LR * (m_hat / (jnp.sqrt(v_hat) + ADAM_EPS) + ADAM_WD * w_ref[...])
        nm_ref[...] = mm
        nv_ref[...] = vv

    blk = pl.BlockSpec((tr, cols), lambda i: (i, 0))
    outs = pl.pallas_call(
        body, name=name, grid=(rows // tr,), in_specs=[blk] * 4, out_specs=[blk] * 3,
        out_shape=[jax.ShapeDtypeStruct((rows, cols), F32)] * 3,
        compiler_params=_cparams(("parallel",)),
    )(*[a.reshape(rows, cols) for a in (w, g, m, v)])
    return [o.reshape(shape) for o in outs]


def add_halves(own, recv, name):
    n, r, c = own.shape
    tr = _pick(r, (512, 256, 128, 64, 32, 16, 8))
    blk = pl.BlockSpec((None, tr, c), lambda i, j: (i, j, 0))

    def body(a_ref, b_ref, o_ref):
        o_ref[...] = a_ref[...] + b_ref[...]

    return pl.pallas_call(
        body, name=name, grid=(n, r // tr), in_specs=[blk, blk], out_specs=blk,
        out_shape=jax.ShapeDtypeStruct((n, r, c), F32), compiler_params=_cparams(("parallel", "parallel")),
    )(own, recv)


def sum_slots(slots, name):
    n, r, c = slots.shape
    tr = _pick(r, (512, 256, 128, 64, 32, 16, 8))

    def body(a_ref, o_ref):
        acc = a_ref[0]
        for k in range(1, n):
            acc = acc + a_ref[k]
        o_ref[...] = acc

    return pl.pallas_call(
        body, name=name, grid=(r // tr,), in_specs=[pl.BlockSpec((n, tr, c), lambda j: (0, j, 0))],
        out_specs=pl.BlockSpec((tr, c), lambda j: (j, 0)),
        out_shape=jax.ShapeDtypeStruct((r, c), F32), compiler_params=_cparams(("parallel",)),
    )(slots)


ANY = pl.BlockSpec(memory_space=pl.ANY)


def _place():
    x, y, c = lax.axis_index("x"), lax.axis_index("y"), lax.axis_index("c")
    return x, y, c


def gather_shards(pack, name):
    r, cols = pack.shape
    half = r // 2

    def body(src_ref, out_ref, send_sems, recv_sems, local_sem):
        x, y, c = _place()
        sibling = (x, y, 1 - c)
        chips = [(1 - x, y), (x, 1 - y), (1 - x, 1 - y)]

        def rows(px, py, pc):
            return out_ref.at[2 * px + py, pl.ds(pc * half, half), :]

        mine = pltpu.make_async_copy(src_ref, out_ref.at[2 * x + y], local_sem)
        mine.start()

        def copy(k, blk, to, src=None):
            return pltpu.make_async_remote_copy(
                src_ref=rows(*blk) if src is None else src, dst_ref=rows(*blk),
                send_sem=send_sems.at[k], recv_sem=recv_sems.at[k], device_id=to, device_id_type=MESH)

        first = [copy(j, (x, y, c), (*chip, c), src=src_ref.at[pl.ds(c * half, half), :])
                 for j, chip in enumerate(chips)]
        for cp in first:
            cp.start()
        passed = [copy(3 + j, (*chip, c), sibling) for j, chip in enumerate(chips)]
        for j, chip in enumerate(chips):
            copy(j, (*chip, c), (x, y, c)).wait_recv()
            passed[j].start()
        for j, chip in enumerate(chips):
            copy(3 + j, (*chip, 1 - c), (x, y, c)).wait_recv()
        for cp in first + passed:
            cp.wait_send()
        mine.wait()

    return pl.pallas_call(
        body, name=name, in_specs=[ANY], out_specs=ANY,
        out_shape=jax.ShapeDtypeStruct((N_CHIPS, r, cols), pack.dtype),
        scratch_shapes=[pltpu.SemaphoreType.DMA((6,)), pltpu.SemaphoreType.DMA((6,)), pltpu.SemaphoreType.DMA],
    )(pack)


def exchange_sibling_halves(g, name):
    n, r, cols = g.shape
    half = r // 2

    def body(g_ref, own_ref, recv_ref, send_sems, recv_sems, local_sem):
        x, y, c = _place()
        sibling = (x, y, 1 - c)
        mine = pltpu.make_async_copy(g_ref.at[:, pl.ds(c * half, half), :], own_ref, local_sem)
        mine.start()
        cp = pltpu.make_async_remote_copy(
            src_ref=g_ref.at[:, pl.ds((1 - c) * half, half), :], dst_ref=recv_ref,
            send_sem=send_sems.at[0], recv_sem=recv_sems.at[0], device_id=sibling, device_id_type=MESH)
        cp.start()
        cp.wait()
        mine.wait()

    return pl.pallas_call(
        body, name=name, in_specs=[ANY], out_specs=[ANY, ANY],
        out_shape=[jax.ShapeDtypeStruct((n, half, cols), g.dtype)] * 2,
        scratch_shapes=[pltpu.SemaphoreType.DMA((1,)), pltpu.SemaphoreType.DMA((1,)), pltpu.SemaphoreType.DMA],
    )(g)


def scatter_to_chips(part, name):
    n, h, cols = part.shape

    def body(p_ref, slot_ref, send_sems, recv_sems, local_sem):
        x, y, c = _place()
        me = 2 * x + y
        chips = [(1 - x, y), (x, 1 - y), (1 - x, 1 - y)]
        mine = pltpu.make_async_copy(p_ref.at[me], slot_ref.at[me], local_sem)
        mine.start()
        cps = [pltpu.make_async_remote_copy(
            src_ref=p_ref.at[2 * px + py], dst_ref=slot_ref.at[me],
            send_sem=send_sems.at[j], recv_sem=recv_sems.at[j], device_id=(px, py, c), device_id_type=MESH)
            for j, (px, py) in enumerate(chips)]
        for cp in cps:
            cp.start()
        for j, (px, py) in enumerate(chips):
            pltpu.make_async_remote_copy(
                src_ref=p_ref.at[me], dst_ref=slot_ref.at[2 * px + py],
                send_sem=send_sems.at[j], recv_sem=recv_sems.at[j], device_id=(px, py, c),
                device_id_type=MESH).wait_recv()
        for cp in cps:
            cp.wait_send()
        mine.wait()

    return pl.pallas_call(
        body, name=name, in_specs=[ANY], out_specs=ANY,
        out_shape=jax.ShapeDtypeStruct((n, h, cols), part.dtype),
        scratch_shapes=[pltpu.SemaphoreType.DMA((3,)), pltpu.SemaphoreType.DMA((3,)), pltpu.SemaphoreType.DMA],
    )(part)


def join_sibling_halves(mine_half, name):
    h, cols = mine_half.shape

    def body(m_ref, out_ref, send_sems, recv_sems, local_sem):
        x, y, c = _place()
        sibling = (x, y, 1 - c)
        own = pltpu.make_async_copy(m_ref, out_ref.at[pl.ds(c * h, h), :], local_sem)
        own.start()
        cp = pltpu.make_async_remote_copy(
            src_ref=m_ref, dst_ref=out_ref.at[pl.ds(c * h, h), :],
            send_sem=send_sems.at[0], recv_sem=recv_sems.at[0], device_id=sibling, device_id_type=MESH)
        cp.start()
        pltpu.make_async_remote_copy(
            src_ref=m_ref, dst_ref=out_ref.at[pl.ds((1 - c) * h, h), :],
            send_sem=send_sems.at[0], recv_sem=recv_sems.at[0], device_id=sibling, device_id_type=MESH).wait_recv()
        cp.wait_send()
        own.wait()

    return pl.pallas_call(
        body, name=name, in_specs=[ANY], out_specs=ANY,
        out_shape=jax.ShapeDtypeStruct((2 * h, cols), mine_half.dtype),
        scratch_shapes=[pltpu.SemaphoreType.DMA((1,)), pltpu.SemaphoreType.DMA((1,)), pltpu.SemaphoreType.DMA],
    )(mine_half)


def _kind(n):
    return 'win' if n == 'w_in' else ('row' if n in ROW_SHARDED else 'col')


def _chip_ids():
    x, y, c = _place()
    chips = [(1 - x, y), (x, 1 - y), (1 - x, 1 - y)]
    return x, y, c, 2 * x + y, chips, [2 * px + py for px, py in chips]


def gather_weights(shards, name):
    names = list(SHARDED)
    nt = len(names)
    kinds = [_kind(n) for n in names]
    shapes = [shards[n].shape for n in names]
    depth = shapes[0][0]
    half = depth // 2

    def out_shape(kind, shp):
        if kind == 'col':
            return (shp[0], shp[1], N_CHIPS * shp[2])
        if kind == 'row':
            return (shp[0], N_CHIPS * shp[1], shp[2])
        return (N_CHIPS,) + tuple(shp)

    def body(*refs):
        src, out = refs[:nt], refs[nt:2 * nt]
        send_sems, recv_sems = refs[2 * nt:]
        x, y, c, me, chips, chip_idx = _chip_ids()
        sibling = (x, y, 1 - c)

        def win(t, chip, lo, cnt):
            _, a, b = shapes[t]
            if kinds[t] == 'col':
                return out[t].at[pl.ds(lo, cnt), :, pl.ds(chip * b, b)]
            if kinds[t] == 'row':
                return out[t].at[pl.ds(lo, cnt), pl.ds(chip * a, a), :]
            return out[t].at[chip, pl.ds(lo, cnt)]

        def remote(t, k, chip, lo, to, src_ref=None):
            w = win(t, chip, lo, half)
            return pltpu.make_async_remote_copy(
                src_ref=w if src_ref is None else src_ref, dst_ref=w, send_sem=send_sems.at[7 * t + k],
                recv_sem=recv_sems.at[7 * t + k], device_id=to, device_id_type=MESH)

        def own(t):
            return pltpu.make_async_remote_copy(
                src_ref=src[t], dst_ref=win(t, me, 0, depth), send_sem=send_sems.at[7 * t + 6],
                recv_sem=recv_sems.at[7 * t + 6], device_id=sibling, device_id_type=MESH)

        mine = [own(t) for t in range(nt)]
        for cp in mine:
            cp.start()
        first = [[remote(t, j, me, c * half, (*chips[j], c), src_ref=src[t].at[pl.ds(c * half, half)])
                  for j in range(3)] for t in range(nt)]
        for t in range(nt):
            for cp in first[t]:
                cp.start()
        passed = [[remote(t, 3 + j, chip_idx[j], c * half, sibling) for j in range(3)] for t in range(nt)]
        for t in range(nt):
            for j in range(3):
                remote(t, j, chip_idx[j], c * half, (x, y, c)).wait_recv()
                passed[t][j].start()
        for t in range(nt):
            for j in range(3):
                remote(t, 3 + j, chip_idx[j], (1 - c) * half, (x, y, c)).wait_recv()
        for t in range(nt):
            for cp in first[t] + passed[t]:
                cp.wait_send()
            mine[t].wait()

    outs = pl.pallas_call(
        body, name=name, in_specs=[ANY] * nt, out_specs=[ANY] * nt,
        out_shape=[jax.ShapeDtypeStruct(out_shape(k, s), BF16) for k, s in zip(kinds, shapes)],
        scratch_shapes=[pltpu.SemaphoreType.DMA((7 * nt,)), pltpu.SemaphoreType.DMA((7 * nt,))],
    )(*[shards[n] for n in names])
    return dict(zip(names, outs))


def gather_weights2(shards, name):
    names = list(SHARDED)
    nt = len(names)
    kinds = [_kind(n) for n in names]
    shapes = [shards[n].shape for n in names]
    depth = shapes[0][0]
    assert depth == 4
    ns = 13

    def out_shape(kind, shp):
        if kind == 'col':
            return (shp[0], shp[1], N_CHIPS * shp[2])
        if kind == 'row':
            return (shp[0], N_CHIPS * shp[1], shp[2])
        return (N_CHIPS,) + tuple(shp)

    def body(*refs):
        src, out = refs[:nt], refs[nt:2 * nt]
        send_sems, recv_sems = refs[2 * nt:]
        x, y, c, me, chips, chip_idx = _chip_ids()
        sibling = (x, y, 1 - c)
        nbr = [(*chips[0], c), (*chips[1], c)]

        def win(t, chip, layer, cnt=1):
            _, a, b = shapes[t]
            if kinds[t] == 'col':
                return out[t].at[pl.ds(layer, cnt), :, pl.ds(chip * b, b)]
            if kinds[t] == 'row':
                return out[t].at[pl.ds(layer, cnt), pl.ds(chip * a, a), :]
            return out[t].at[chip, pl.ds(layer, cnt)]

        def rc(t, k, chip, layer, to, src_ref=None, cnt=1):
            w = win(t, chip, layer, cnt)
            return pltpu.make_async_remote_copy(
                src_ref=w if src_ref is None else src_ref, dst_ref=w, send_sem=send_sems.at[ns * t + k],
                recv_sem=recv_sems.at[ns * t + k], device_id=to, device_id_type=MESH)

        la = [2 * c, 2 * c + 1]
        lo = [2 - 2 * c, 3 - 2 * c]
        sends = []
        for t in range(nt):
            sends.append(rc(t, 12, me, 0, sibling, src_ref=src[t], cnt=depth))
            for ax in range(2):
                for li in range(2):
                    sends.append(rc(t, 2 * ax + li, me, la[li], nbr[ax], src_ref=src[t].at[pl.ds(la[li], 1)]))
        for cp in sends:
            cp.start()
        landed = [(0, 0, 0, 0), (1, 1, 0, 1), (2, 2, 1, 0), (3, 3, 1, 1), (4, 4, 2, 0), (5, 5, 2, 1)]
        later = []
        for t in range(nt):
            for piece, sem, origin, li in landed:
                rc(t, sem, chip_idx[origin], la[li], (x, y, c)).wait_recv()
                if piece == 0:
                    fw = rc(t, 4, chip_idx[0], la[0], nbr[1])
                    fw.start()
                    later.append(fw)
                if piece == 3:
                    fw = rc(t, 5, chip_idx[1], la[1], nbr[0])
                    fw.start()
                    later.append(fw)
                ps = rc(t, 6 + piece, chip_idx[origin], la[li], sibling)
                ps.start()
                later.append(ps)
        for t in range(nt):
            rc(t, 12, me, 0, (x, y, c), cnt=depth).wait_recv()
            for piece, sem, origin, li in landed:
                rc(t, 6 + piece, chip_idx[origin], lo[li], (x, y, c)).wait_recv()
        for cp in sends + later:
            cp.wait_send()

    outs = pl.pallas_call(
        body, name=name, in_specs=[ANY] * nt, out_specs=[ANY] * nt,
        out_shape=[jax.ShapeDtypeStruct(out_shape(k, s), BF16) for k, s in zip(kinds, shapes)],
        scratch_shapes=[pltpu.SemaphoreType.DMA((ns * nt,)), pltpu.SemaphoreType.DMA((ns * nt,))],
    )(*[shards[n] for n in names])
    return dict(zip(names, outs))


def _half_win(ref, kind, hc, layer):
    if kind == 'col':
        hk = ref.shape[1] // 2
        return ref.at[layer, pl.ds(hc * hk, hk), :]
    if kind == 'row':
        hn = ref.shape[2] // 2
        return ref.at[layer, :, pl.ds(hc * hn, hn)]
    hk = ref.shape[2] // 2
    return ref.at[layer, :, pl.ds(hc * hk, hk), :]


def _half_shape(kind, shp):
    if kind == 'col':
        return (shp[0], shp[1] // 2, shp[2])
    if kind == 'row':
        return (shp[0], shp[1], shp[2] // 2)
    return (shp[0], shp[1], shp[2] // 2, shp[3])


def rs_sibling(grads, name):
    names = list(SHARDED)
    nt = len(names)
    kinds = [_kind(n) for n in names]
    shapes = [grads[n].shape for n in names]
    depth = shapes[0][0]

    def body(*refs):
        src, out = refs[:nt], refs[nt:2 * nt]
        send_sems, recv_sems = refs[2 * nt:]
        x, y, c = _place()
        cps = []
        for t in range(nt):
            for l in range(depth):
                cps.append(pltpu.make_async_remote_copy(
                    src_ref=_half_win(src[t], kinds[t], 1 - c, l), dst_ref=out[t].at[l],
                    send_sem=send_sems.at[depth * t + l], recv_sem=recv_sems.at[depth * t + l],
                    device_id=(x, y, 1 - c), device_id_type=MESH))
        for cp in cps:
            cp.start()
        for cp in cps:
            cp.wait()

    outs = pl.pallas_call(
        body, name=name, in_specs=[ANY] * nt, out_specs=[ANY] * nt,
        out_shape=[jax.ShapeDtypeStruct(_half_shape(k, s), F32) for k, s in zip(kinds, shapes)],
        scratch_shapes=[pltpu.SemaphoreType.DMA((depth * nt,)), pltpu.SemaphoreType.DMA((depth * nt,))],
    )(*[grads[n] for n in names])
    return dict(zip(names, outs))


EW_BLOCK_ELEMS = 512 * 1024


def rs_add(kind, g, recv, c_arr, name):
    shp = recv.shape
    rows, cols = shp[-2], shp[-1]
    tr = _pick(rows, [r for r in (1408, 1024, 704, 512, 256, 128, 64, 32, 16, 8) if r * cols <= EW_BLOCK_ELEMS])
    nb = rows // tr
    lead = (None,) * (len(shp) - 2)
    blk = pl.BlockSpec(lead + (tr, cols), lambda *a: tuple(a[:len(shp) - 2]) + (a[len(shp) - 2], 0))
    if kind == 'row':
        g_blk = pl.BlockSpec(lead + (tr, cols), lambda *a: tuple(a[:len(shp) - 2]) + (a[len(shp) - 2], a[-1][0]))
    else:
        g_blk = pl.BlockSpec(lead + (tr, cols),
                             lambda *a: tuple(a[:len(shp) - 2]) + (a[-1][0] * nb + a[len(shp) - 2], 0))

    def body(c_ref, g_ref, r_ref, o_ref):
        o_ref[...] = (g_ref[...] + r_ref[...]).astype(o_ref.dtype)

    grid_spec = pltpu.PrefetchScalarGridSpec(
        num_scalar_prefetch=1, grid=tuple(shp[:-2]) + (nb,), in_specs=[g_blk, blk], out_specs=blk)
    return pl.pallas_call(
        body, name=name, grid_spec=grid_spec, out_shape=jax.ShapeDtypeStruct(shp, BF16),
        compiler_params=_cparams(None),
    )(c_arr, g, recv)


def _chip_win(ref, kind, chip):
    if kind == 'col':
        ns = ref.shape[2] // N_CHIPS
        return ref.at[:, :, pl.ds(chip * ns, ns)]
    if kind == 'row':
        ks = ref.shape[1] // N_CHIPS
        return ref.at[:, pl.ds(chip * ks, ks), :]
    return ref.at[:, chip]


def _chip_shape(kind, shp):
    if kind == 'col':
        return (shp[0], shp[1], shp[2] // N_CHIPS)
    if kind == 'row':
        return (shp[0], shp[1] // N_CHIPS, shp[2])
    return (shp[0], shp[2], shp[3])


def rs_chips(parts, name):
    names = list(SHARDED)
    nt = len(names)
    kinds = [_kind(n) for n in names]
    shapes = [parts[n].shape for n in names]

    def body(*refs):
        src, out = refs[:nt], refs[nt:2 * nt]
        send_sems, recv_sems, local_sems = refs[2 * nt:]
        x, y, c, me, chips, chip_idx = _chip_ids()
        mine = [pltpu.make_async_copy(_chip_win(src[t], kinds[t], me), out[t].at[me], local_sems.at[t])
                for t in range(nt)]
        for cp in mine:
            cp.start()
        cps = [[pltpu.make_async_remote_copy(
            src_ref=_chip_win(src[t], kinds[t], chip_idx[j]), dst_ref=out[t].at[me],
            send_sem=send_sems.at[3 * t + j], recv_sem=recv_sems.at[3 * t + j],
            device_id=(*chips[j], c), device_id_type=MESH) for j in range(3)] for t in range(nt)]
        for t in range(nt):
            for cp in cps[t]:
                cp.start()
        for t in range(nt):
            for j in range(3):
                pltpu.make_async_remote_copy(
                    src_ref=_chip_win(src[t], kinds[t], me), dst_ref=out[t].at[chip_idx[j]],
                    send_sem=send_sems.at[3 * t + j], recv_sem=recv_sems.at[3 * t + j],
                    device_id=(*chips[j], c), device_id_type=MESH).wait_recv()
        for t in range(nt):
            for cp in cps[t]:
                cp.wait_send()
            mine[t].wait()

    outs = pl.pallas_call(
        body, name=name, in_specs=[ANY] * nt, out_specs=[ANY] * nt,
        out_shape=[jax.ShapeDtypeStruct((N_CHIPS,) + _chip_shape(k, s), parts[n].dtype)
                   for n, k, s in zip(names, kinds, shapes)],
        scratch_shapes=[pltpu.SemaphoreType.DMA((3 * nt,)), pltpu.SemaphoreType.DMA((3 * nt,)),
                        pltpu.SemaphoreType.DMA((nt,))],
    )(*[parts[n] for n in names])
    return dict(zip(names, outs))


def rs_sum(slots, name):
    n, depth, r, cols = slots.shape
    tr = _pick(r, [q for q in (1408, 1024, 704, 512, 256, 128, 64, 32, 16, 8) if q * cols * n <= 2 * EW_BLOCK_ELEMS])

    def body(a_ref, o_ref):
        acc = a_ref[0].astype(F32)
        for k in range(1, n):
            acc = acc + a_ref[k].astype(F32)
        o_ref[...] = acc

    return pl.pallas_call(
        body, name=name, grid=(depth, r // tr),
        in_specs=[pl.BlockSpec((n, None, tr, cols), lambda l, i: (0, l, i, 0))],
        out_specs=pl.BlockSpec((None, tr, cols), lambda l, i: (l, i, 0)),
        out_shape=jax.ShapeDtypeStruct((depth, r, cols), F32), compiler_params=_cparams(("parallel", "parallel")),
    )(slots)


def rs_join(reds, name):
    names = list(SHARDED)
    nt = len(names)
    kinds = [_kind(n) for n in names]
    shapes = [reds[n].shape for n in names]
    depth = shapes[0][0]

    def full_shape(kind, shp):
        if kind == 'row':
            return (shp[0], shp[1], 2 * shp[2])
        return (shp[0], 2 * shp[1], shp[2])

    def win(ref, kind, hc, layer):
        if kind == 'row':
            hn = ref.shape[2] // 2
            return ref.at[layer, :, pl.ds(hc * hn, hn)]
        hk = ref.shape[1] // 2
        return ref.at[layer, pl.ds(hc * hk, hk), :]

    def body(*refs):
        src, out = refs[:nt], refs[nt:2 * nt]
        send_sems, recv_sems, local_sems = refs[2 * nt:]
        x, y, c = _place()
        own, cps = [], []
        for t in range(nt):
            for l in range(depth):
                i = depth * t + l
                own.append(pltpu.make_async_copy(src[t].at[l], win(out[t], kinds[t], c, l), local_sems.at[i]))
                cps.append(pltpu.make_async_remote_copy(
                    src_ref=src[t].at[l], dst_ref=win(out[t], kinds[t], c, l), send_sem=send_sems.at[i],
                    recv_sem=recv_sems.at[i], device_id=(x, y, 1 - c), device_id_type=MESH))
        for cp in own + cps:
            cp.start()
        for t in range(nt):
            for l in range(depth):
                i = depth * t + l
                pltpu.make_async_remote_copy(
                    src_ref=src[t].at[l], dst_ref=win(out[t], kinds[t], 1 - c, l), send_sem=send_sems.at[i],
                    recv_sem=recv_sems.at[i], device_id=(x, y, 1 - c), device_id_type=MESH).wait_recv()
        for cp in cps:
            cp.wait_send()
        for cp in own:
            cp.wait()

    outs = pl.pallas_call(
        body, name=name, in_specs=[ANY] * nt, out_specs=[ANY] * nt,
        out_shape=[jax.ShapeDtypeStruct(full_shape(k, s), F32) for k, s in zip(kinds, shapes)],
        scratch_shapes=[pltpu.SemaphoreType.DMA((depth * nt,)), pltpu.SemaphoreType.DMA((depth * nt,)),
                        pltpu.SemaphoreType.DMA((depth * nt,))],
    )(*[reds[n] for n in names])
    return dict(zip(names, outs))


def rs_chips2(parts, name):
    names = list(SHARDED)
    nt = len(names)
    kinds = [_kind(n) for n in names]
    shapes = [parts[n].shape for n in names]

    def body(*refs):
        src, out = refs[:nt], refs[nt:2 * nt]
        send_sems, recv_sems = refs[2 * nt:]
        x, y, c, me, chips, chip_idx = _chip_ids()
        cps = [[pltpu.make_async_remote_copy(
            src_ref=_chip_win(src[t], kinds[t], chip_idx[j]), dst_ref=out[t].at[j],
            send_sem=send_sems.at[3 * t + j], recv_sem=recv_sems.at[3 * t + j],
            device_id=(*chips[j], c), device_id_type=MESH) for j in range(3)] for t in range(nt)]
        for t in range(nt):
            for cp in cps[t]:
                cp.start()
        for t in range(nt):
            for cp in cps[t]:
                cp.wait()

    outs = pl.pallas_call(
        body, name=name, in_specs=[ANY] * nt, out_specs=[ANY] * nt,
        out_shape=[jax.ShapeDtypeStruct((3,) + _chip_shape(k, s), parts[n].dtype)
                   for n, k, s in zip(names, kinds, shapes)],
        scratch_shapes=[pltpu.SemaphoreType.DMA((3 * nt,)), pltpu.SemaphoreType.DMA((3 * nt,))],
    )(*[parts[n] for n in names])
    return dict(zip(names, outs))


def rs_sum2(kind, part, slots, sc_arr, name):
    _, depth, r, cols = slots.shape
    tr = _pick(r, [q for q in (1408, 1024, 704, 512, 256, 128, 64, 32, 16) if q * cols <= EW_BLOCK_ELEMS // 2])
    nb = r // tr
    if kind == 'col':
        own_blk = pl.BlockSpec((None, tr, cols), lambda l, i, sc: (l, i, sc[0]))
        out_blk = pl.BlockSpec((None, tr, cols), lambda l, i, sc: (l, sc[1] * nb + i, 0))
        out_shape = (depth, 2 * r, cols)
    elif kind == 'row':
        own_blk = pl.BlockSpec((None, tr, cols), lambda l, i, sc: (l, sc[0] * nb + i, 0))
        out_blk = pl.BlockSpec((None, tr, cols), lambda l, i, sc: (l, i, sc[1]))
        out_shape = (depth, r, 2 * cols)
    else:
        own_blk = pl.BlockSpec((None, None, tr, cols), lambda l, i, sc: (l, sc[0], i, 0))
        out_blk = pl.BlockSpec((None, tr, cols), lambda l, i, sc: (l, sc[1] * nb + i, 0))
        out_shape = (depth, 2 * r, cols)

    def body(sc_ref, own_ref, s_ref, o_ref):
        acc = own_ref[...].astype(F32)
        for k in range(3):
            acc = acc + s_ref[k].astype(F32)
        o_ref[...] = acc

    grid_spec = pltpu.PrefetchScalarGridSpec(
        num_scalar_prefetch=1, grid=(depth, nb),
        in_specs=[own_blk, pl.BlockSpec((3, None, tr, cols), lambda l, i, sc: (0, l, i, 0))], out_specs=out_blk)
    return pl.pallas_call(
        body, name=name, grid_spec=grid_spec, out_shape=jax.ShapeDtypeStruct(out_shape, F32),
        compiler_params=_cparams(None),
    )(sc_arr, part, slots)


def rs_join2(halves, name):
    names = list(SHARDED)
    nt = len(names)
    kinds = [_kind(n) for n in names]
    shapes = [halves[n].shape for n in names]
    depth = shapes[0][0]

    def win(ref, kind, hc, layer):
        if kind == 'row':
            hn = ref.shape[2] // 2
            return ref.at[layer, :, pl.ds(hc * hn, hn)]
        hk = ref.shape[1] // 2
        return ref.at[layer, pl.ds(hc * hk, hk), :]

    def body(*refs):
        src, out = refs[:nt], refs[nt:2 * nt]
        send_sems, recv_sems = refs[2 * nt:]
        x, y, c = _place()
        cps = []
        for t in range(nt):
            for l in range(depth):
                i = depth * t + l
                cps.append(pltpu.make_async_remote_copy(
                    src_ref=win(src[t], kinds[t], c, l), dst_ref=win(out[t], kinds[t], c, l),
                    send_sem=send_sems.at[i], recv_sem=recv_sems.at[i], device_id=(x, y, 1 - c),
                    device_id_type=MESH))
        for cp in cps:
            cp.start()
        for t in range(nt):
            for l in range(depth):
                i = depth * t + l
                pltpu.make_async_remote_copy(
                    src_ref=win(src[t], kinds[t], c, l), dst_ref=win(out[t], kinds[t], 1 - c, l),
                    send_sem=send_sems.at[i], recv_sem=recv_sems.at[i], device_id=(x, y, 1 - c),
                    device_id_type=MESH).wait_recv()
        for cp in cps:
            cp.wait_send()

    outs = pl.pallas_call(
        body, name=name, in_specs=[ANY] * nt, out_specs=[ANY] * nt,
        out_shape=[jax.ShapeDtypeStruct(s, F32) for s in shapes],
        input_output_aliases={t: t for t in range(nt)},
        scratch_shapes=[pltpu.SemaphoreType.DMA((depth * nt,)), pltpu.SemaphoreType.DMA((depth * nt,))],
    )(*[halves[n] for n in names])
    return dict(zip(names, outs))


def gather_small(v, name):
    r, cols = v.shape

    def body(v_ref, out_ref, send_sems, recv_sems):
        x, y, c = _place()
        me = 4 * x + 2 * y + c
        out_ref[me] = v_ref[...]
        cps = []
        for rel in range(1, N_DEV):
            px = 1 - x if (rel >> 2) & 1 else x
            py = 1 - y if (rel >> 1) & 1 else y
            pc = 1 - c if rel & 1 else c
            cps.append(pltpu.make_async_remote_copy(
                src_ref=v_ref, dst_ref=out_ref.at[me], send_sem=send_sems.at[rel - 1],
                recv_sem=recv_sems.at[rel - 1], device_id=(px, py, pc), device_id_type=MESH))
        for cp in cps:
            cp.start()
        for cp in cps:
            cp.wait()

    return pl.pallas_call(
        body, name=name, in_specs=[pl.BlockSpec(memory_space=pltpu.VMEM)],
        out_specs=pl.BlockSpec(memory_space=pltpu.VMEM),
        out_shape=jax.ShapeDtypeStruct((N_DEV, r, cols), v.dtype),
        scratch_shapes=[pltpu.SemaphoreType.DMA((N_DEV - 1,)), pltpu.SemaphoreType.DMA((N_DEV - 1,))],
        compiler_params=pltpu.CompilerParams(vmem_limit_bytes=VMEM_LIMIT),
    )(v)


def _pad_rows(flat, row_align):
    n = flat.shape[-1]
    unit = PACK_COLS * row_align
    tot = -(-n // unit) * unit
    pad = [(0, 0)] * (flat.ndim - 1) + [(0, tot - n)]
    return jnp.pad(flat, pad)


def _pack_shards(ws):
    flat = jnp.concatenate([ws[n].astype(BF16).reshape(-1) for n in SHARDED])
    return _pad_rows(flat, PACK_ROW_ALIGN).reshape(-1, PACK_COLS)


def _unpack_full(gathered, shard_shapes):
    flat = gathered.reshape(N_CHIPS, -1)
    out, off = {}, 0
    for n in SHARDED:
        shp = shard_shapes[n]
        size = math.prod(shp)
        seg = flat[:, off:off + size].reshape((N_CHIPS,) + tuple(shp))
        off += size
        if n in ROW_SHARDED:
            out[n] = jnp.transpose(seg, (1, 0, 2, 3)).reshape(shp[0], N_CHIPS * shp[1], shp[2])
        else:
            out[n] = jnp.transpose(seg, (1, 2, 0, 3)).reshape(shp[0], shp[1], N_CHIPS * shp[2])
    return out


def _pack_grads(gfull, shard_shapes):
    segs = []
    for n in SHARDED:
        shp = shard_shapes[n]
        g = gfull[n]
        if n in ROW_SHARDED:
            seg = jnp.transpose(g.reshape(shp[0], N_CHIPS, shp[1], shp[2]), (1, 0, 2, 3))
        else:
            seg = jnp.transpose(g.reshape(shp[0], shp[1], N_CHIPS, shp[2]), (2, 0, 1, 3))
        segs.append(seg.reshape(N_CHIPS, -1))
    flat = jnp.concatenate(segs, axis=1)
    return _pad_rows(flat, PACK_ROW_ALIGN).reshape(N_CHIPS, -1, PACK_COLS)


def _unpack_shard_grads(red, shard_shapes):
    flat = red.reshape(-1)
    out, off = {}, 0
    for n in SHARDED:
        shp = shard_shapes[n]
        size = math.prod(shp)
        out[n] = flat[off:off + size].reshape(shp)
        off += size
    return out


def _pack_small(parts):
    flat = jnp.concatenate([p.astype(F32).reshape(-1) for p in parts])
    return _pad_rows(flat, 8).reshape(-1, PACK_COLS)


def _unpack_small(flat2d, shapes):
    flat = flat2d.reshape(-1)
    out, off = [], 0
    for shp in shapes:
        size = math.prod(shp)
        out.append(flat[off:off + size].reshape(shp))
        off += size
    return out


def _heads(a, bn, s, h):
    return jnp.transpose(a.reshape(bn, s, h, HEAD_DIM), (0, 2, 1, 3))


def _unheads(a):
    bn, h, s, d = a.shape
    return jnp.transpose(a, (0, 2, 1, 3)).reshape(bn * s, h * d)


def _reorder_w_in(w):
    d = w.shape[0]
    return jnp.concatenate([w[:, 2820:6916], w[:, 0:768], w[:, 772:1540], w[:, 1540:2052], w[:, 2052:2820],
                            w[:, 768:772], jnp.zeros((d, N_PROJ - 6916), w.dtype)], axis=1)


def _restore_dw_in(g):
    return jnp.concatenate([g[:, 4096:4864], g[:, 6912:6916], g[:, 4864:5632], g[:, 5632:6144], g[:, 6144:6912],
                            g[:, 0:4096]], axis=1)


def res_rms_next_fwd(h, f, g, coef, g_next, name):
    def fn(x, y, gg, gn):
        hn = x + coef * _rms(y, gg)
        return hn, _rms(hn, gn)
    return _rowwise(fn, [_full(h), _full(f)], [g, g_next], [(h.shape[1], F32), (h.shape[1], BF16)], tm=512,
                    name=name)


def ple_next_fwd(h, pgl, pr, g, g_next, name):
    def fn(x, a, b, gg, gn):
        hn = x + _ple(a, b, gg)
        return hn, _rms(hn, gn)
    return _rowwise(fn, [_full(h), _full(pgl), _full(pr)], [g, g_next], [(D_MODEL, F32), (D_MODEL, BF16)], tm=512,
                    name=name)


def _ffn_fwd(h, n, w_gu, w_down, g_post, g_next, tag):
    gu = _mm(n, w_gu, out_dtype=BF16, name=f"{tag}_mm_gu")
    a = swiglu_fwd(gu, f"{tag}_swiglu")
    f = _mm(a, w_down, out_dtype=F32, name=f"{tag}_mm_down")
    h_out, n_next = res_rms_next_fwd(h, f, g_post, 0.5, g_next, f"{tag}_res")
    return h_out, n_next, (h, n, gu, a, f)


def rms_res_bwd(h, g, dn, dres, f, g2, coef, name):
    def fn(x, d, r, y, gg, gg2):
        _, vjp = jax.vjp(_rms, x, gg)
        dx, dg = vjp(d.astype(F32))
        dh = dx + r
        _, vjp2 = jax.vjp(lambda a, b: coef * _rms(a, b), y, gg2)
        dy, dg2 = vjp2(dh)
        return dh, dy, dg, dg2
    w = h.shape[1]
    return _rowwise(fn, [_full(h), _full(dn), _full(dres), _full(f)], [g, g2], [(w, F32), (w, BF16)], [w, w],
                    name=name)


def _ffn_bwd(df, saved, w_gu, w_down, tag, dw, n_gu, n_down):
    _, n, gu, a, _ = saved
    da = _mm(df, w_down, tb=True, out_dtype=BF16, name=f"{tag}_mm_da")
    dw(n_down, a, df, f"{tag}_mm_dwdown")
    dgu = swiglu_bwd(gu, da, f"{tag}_swiglu_bwd")
    dw(n_gu, n, dgu, f"{tag}_mm_dwgu")
    return _mm(dgu, w_gu, tb=True, out_dtype=BF16, name=f"{tag}_mm_dn")


def kernel(x, p, ffn1_norm_pre, ffn1_w_gu, ffn1_w_down, ffn1_norm_post, mix_norm_pre, w_in, b_forget, b_gate, conv_short, conv_dw, conv_dw_bias, conv_ln_gain, conv_ln_bias, attn_sinks, rel_bias, w_br_a, w_br_b, w_br_c, w_br_d, w_o, mix_norm_post, ffn2_norm_pre, ffn2_w_gu, ffn2_w_down, ffn2_norm_post, ple_norm_gate, w_ple_gate, w_ple, ple_norm_post, loss_target, m_ffn1_norm_pre, m_ffn1_w_gu, m_ffn1_w_down, m_ffn1_norm_post, m_mix_norm_pre, m_w_in, m_b_forget, m_b_gate, m_conv_short, m_conv_dw, m_conv_dw_bias, m_conv_ln_gain, m_conv_ln_bias, m_attn_sinks, m_rel_bias, m_w_br_a, m_w_br_b, m_w_br_c, m_w_br_d, m_w_o, m_mix_norm_post, m_ffn2_norm_pre, m_ffn2_w_gu, m_ffn2_w_down, m_ffn2_norm_post, m_ple_norm_gate, m_w_ple_gate, m_w_ple, m_ple_norm_post, v_ffn1_norm_pre, v_ffn1_w_gu, v_ffn1_w_down, v_ffn1_norm_post, v_mix_norm_pre, v_w_in, v_b_forget, v_b_gate, v_conv_short, v_conv_dw, v_conv_dw_bias, v_conv_ln_gain, v_conv_ln_bias, v_attn_sinks, v_rel_bias, v_w_br_a, v_w_br_b, v_w_br_c, v_w_br_d, v_w_o, v_mix_norm_post, v_ffn2_norm_pre, v_ffn2_w_gu, v_ffn2_w_down, v_ffn2_norm_post, v_ple_norm_gate, v_w_ple_gate, v_w_ple, v_ple_norm_post):
    args = dict(locals())
    ws = {n: args[n] for n in WEIGHTS}
    ms = {n: args["m_" + n] for n in WEIGHTS}
    vs = {n: args["v_" + n] for n in WEIGHTS}
    return _step(x, p, loss_target, ws, ms, vs)


def _local(x, p, loss_target, ws, wf, w_short, w_dw):
    bn, s, d = x.shape
    t = bn * s
    depth = w_short.shape[0]

    def vec(a, i):
        return a[i].reshape(1, -1)

    bucket = _bucket_table()
    band_bias = band_bias_fwd(bucket, ws['rel_bias'], "band_bias")

    h = x.reshape(t, d)
    saved = []
    n1 = rms_fwd(h, vec(ws['ffn1_norm_pre'], 0), "l0_ffn1_rms")
    for i in range(depth):
        sv = {}
        h1, u, sv['ffn1'] = _ffn_fwd(h, n1, (wf['ffn1_w_gu'], i), (wf['ffn1_w_down'], i),
                                     vec(ws['ffn1_norm_post'], i), vec(ws['mix_norm_pre'], i), f"l{i}_ffn1")
        w_in_r = _reorder_w_in(wf['w_in'][i])
        proj = _mm(u, w_in_r, out_dtype=BF16, name=f"l{i}_mm_proj")
        bf = jnp.pad(vec(ws['b_forget'], i), ((0, 0), (0, 128 - A_HEADS)))
        cc = fgate_fwd(proj, bf, bn, s, f"l{i}_fgate")
        c4 = jnp.transpose(cc.reshape(bn, s, 128)[:, :, :A_HEADS], (0, 2, 1))
        c_col = c4[..., None]
        c_row = c4.reshape(bn, A_HEADS, s // FOX_T, 1, FOX_T)
        ya, oa, lse_a = fox2_fwd(proj, c_col, c_row, bn, s, f"l{i}_fox")
        w_sh = jnp.pad(w_short[i], ((0, 8 - w_short.shape[1]), (0, 0)))
        w_cv = jnp.pad(w_dw[i], ((0, 32 - w_dw.shape[1]), (0, 0)))
        yb = convb_fwd(proj, w_sh[:3], bn, s, f"l{i}_convb")
        cvec = (vec(ws['conv_dw_bias'], i), vec(ws['conv_ln_gain'], i), vec(ws['conv_ln_bias'], i))
        yc, yconv = convc_fwd(proj, w_cv[:31], *cvec, bn, s, f"l{i}_convc")
        sinks = jnp.broadcast_to(ws['attn_sinks'][i].reshape(D_Q_HEADS, 1, 1), (D_Q_HEADS, QB, 1))
        yd, lse_d = swa3_fwd(proj, band_bias, sinks, bn, s, f"l{i}_swa")
        ys = (ya, yb, yc, yd)
        wbr = ((wf['w_br_a'], i), (wf['w_br_b'], i), (wf['w_br_c'], i), (wf['w_br_d'], i))
        zs = [_mm(yk, wk, out_dtype=BF16, name=f"l{i}_mm_br{k}") for k, (yk, wk) in enumerate(zip(ys, wbr))]
        bgs = [ws['b_gate'][i, k * d:(k + 1) * d].reshape(1, d) for k in range(4)]
        merged = merge_fwd(proj, zs, bgs, f"l{i}_merge")
        mo = _mm(merged, (wf['w_o'], i), out_dtype=F32, name=f"l{i}_mm_o")
        h2, n2 = res_rms_next_fwd(h1, mo, vec(ws['mix_norm_post'], i), 1.0, vec(ws['ffn2_norm_pre'], i),
                                  f"l{i}_mix_res")
        sv['mix'] = dict(h1=h1, u=u, proj=proj, w_in_r=w_in_r, bf=bf, c_col=c_col, c_row=c_row, oa=oa, lse_a=lse_a,
                         w_sh=w_sh, w_cv=w_cv, cvec=cvec, yconv=yconv, sinks=sinks, lse_d=lse_d, ys=ys, wbr=wbr, zs=zs, bgs=bgs,
                         merged=merged, mo=mo)
        h3, ng, sv['ffn2'] = _ffn_fwd(h2, n2, (wf['ffn2_w_gu'], i), (wf['ffn2_w_down'], i),
                                      vec(ws['ffn2_norm_post'], i), vec(ws['ple_norm_gate'], i), f"l{i}_ffn2")
        pgl = _mm(ng, (wf['w_ple_gate'], i), out_dtype=BF16, name=f"l{i}_mm_pgl")
        p_i = p[i].reshape(t, -1)
        pr = _mm(p_i, (wf['w_ple'], i), out_dtype=F32, name=f"l{i}_mm_pr")
        if i + 1 < depth:
            h, n1 = ple_next_fwd(h3, pgl, pr, vec(ws['ple_norm_post'], i), vec(ws['ffn1_norm_pre'], i + 1),
                                 f"l{i}_ple")
        else:
            h = ple_fwd(h3, pgl, pr, vec(ws['ple_norm_post'], i), f"l{i}_ple")
        sv['ple'] = dict(h3=h3, ng=ng, pgl=pgl, p_i=p_i, pr=pr)
        saved.append(sv)

    dh, loss_vec = loss_fwd_bwd(h, loss_target.reshape(t, d), "loss")
    loss_part = jnp.sum(loss_vec)

    gst = {}
    gwin = [None] * depth

    def dw(n, a, b, nm):
        gst[n] = _mm(a, b, ta=True, name=nm, stack=(gst.get(n), depth, i))

    gsmall = {n: [None] * depth for n in REPLICATED + CONV_SHARDED if n != 'rel_bias'}
    dbias_layers = []
    for i in reversed(range(depth)):
        sv = saved[i]
        pv = sv['ple']
        dpgl, dpr, dg = ple_bwd(pv['pgl'], pv['pr'], vec(ws['ple_norm_post'], i), dh, f"l{i}_ple_bwd")
        gsmall['ple_norm_post'][i] = dg
        dw('w_ple', pv['p_i'], dpr, f"l{i}_mm_dwple")
        dw('w_ple_gate', pv['ng'], dpgl, f"l{i}_mm_dwpg")
        dng = _mm(dpgl, (wf['w_ple_gate'], i), tb=True, out_dtype=BF16, name=f"l{i}_mm_dng")
        dh, df2, gsmall['ple_norm_gate'][i], gsmall['ffn2_norm_post'][i] = rms_res_bwd(
            pv['h3'], vec(ws['ple_norm_gate'], i), dng, dh, sv['ffn2'][4], vec(ws['ffn2_norm_post'], i), 0.5,
            f"l{i}_ple_rms_bwd")
        dn2 = _ffn_bwd(df2, sv['ffn2'], (wf['ffn2_w_gu'], i), (wf['ffn2_w_down'], i), f"l{i}_ffn2", dw,
                       'ffn2_w_gu', 'ffn2_w_down')
        mv = sv['mix']
        dh, dmo, gsmall['ffn2_norm_pre'][i], gsmall['mix_norm_post'][i] = rms_res_bwd(
            sv['ffn2'][0], vec(ws['ffn2_norm_pre'], i), dn2, dh, mv['mo'], vec(ws['mix_norm_post'], i), 1.0,
            f"l{i}_ffn2_rms_bwd")
        dw('w_o', mv['merged'], dmo, f"l{i}_mm_dwo")
        dmerged = _mm(dmo, (wf['w_o'], i), tb=True, out_dtype=BF16, name=f"l{i}_mm_dmerged")
        mb = merge_bwd(mv['proj'], mv['zs'], mv['bgs'], dmerged, f"l{i}_merge_bwd")
        dgates, dzs, dbg = mb[0:4], mb[4:8], mb[8:12]
        gsmall['b_gate'][i] = jnp.concatenate(dbg, axis=1)
        dys = []
        for k, nm in enumerate(('w_br_a', 'w_br_b', 'w_br_c', 'w_br_d')):
            dw(nm, mv['ys'][k], dzs[k], f"l{i}_mm_dwbr{k}")
            dys.append(_mm(dzs[k], mv['wbr'][k], tb=True, out_dtype=BF16, name=f"l{i}_mm_dy{k}"))
        dqa, dka, dva, dck, dcq = fox2_bwd(mv['proj'], mv['c_col'], mv['c_row'], mv['oa'], mv['lse_a'], dys[0], bn, s,
                                           f"l{i}_fox_bwd")
        dc = jnp.transpose(dck.reshape(bn, A_HEADS, s) + dcq.reshape(bn, A_HEADS, s), (0, 2, 1))
        dc = jnp.pad(dc, ((0, 0), (0, 0), (0, 128 - A_HEADS))).reshape(t, 128)
        daf, dbf = fgate_bwd(mv['proj'], mv['bf'], dc, bn, s, f"l{i}_fgate_bwd")
        gsmall['b_forget'][i] = dbf[:, :A_HEADS]
        dbg_, dcg_, dxb_, dwsh = convb_bwd(mv['proj'], mv['w_sh'][:3], dys[1], bn, s, f"l{i}_convb_bwd")
        gsmall['conv_short'][i] = dwsh[:3]
        dca, dcb, dwcv, dcbias, dlg, dlb = convc_bwd(mv['proj'], mv['w_cv'][:31], *mv['cvec'], mv['yconv'], dys[2], bn, s,
                                                     f"l{i}_convc_bwd")
        gsmall['conv_dw'][i] = dwcv[:31]
        gsmall['conv_dw_bias'][i] = dcbias
        gsmall['conv_ln_gain'][i] = dlg
        gsmall['conv_ln_bias'][i] = dlb
        dqd, dkd, dvd, dbias, dsink = swa3_bwd(mv['proj'], band_bias, mv['sinks'], mv['ys'][3], mv['lse_d'], dys[3],
                                               bn, s, f"l{i}_swa_bwd")
        dbias_layers.append(dbias)
        gsmall['attn_sinks'][i] = jnp.sum(dsink, axis=(1, 2))
        dproj = assemble_dproj(list(dgates) + [dqa, dka, dva, dbg_, dcg_, dxb_, dca, dcb, dqd], dkd, dvd, daf,
                               f"l{i}_dproj")
        dwin = _restore_dw_in(_mm(mv['u'], dproj, ta=True, name=f"l{i}_mm_dwin"))
        gwin[i] = jnp.transpose(dwin.reshape(d, N_CHIPS, -1), (1, 0, 2))
        du = _mm(dproj, mv['w_in_r'], tb=True, out_dtype=BF16, name=f"l{i}_mm_du")
        dh, df1, gsmall['mix_norm_pre'][i], gsmall['ffn1_norm_post'][i] = rms_res_bwd(
            mv['h1'], vec(ws['mix_norm_pre'], i), du, dh, sv['ffn1'][4], vec(ws['ffn1_norm_post'], i), 0.5,
            f"l{i}_mix_rms_bwd")
        dn1 = _ffn_bwd(df1, sv['ffn1'], (wf['ffn1_w_gu'], i), (wf['ffn1_w_down'], i), f"l{i}_ffn1", dw,
                       'ffn1_w_gu', 'ffn1_w_down')
        dh, gsmall['ffn1_norm_pre'][i] = rms_bwd(sv['ffn1'][0], vec(ws['ffn1_norm_pre'], i), dn1, dh,
                                                  f"l{i}_ffn1_rms_bwd")
    grad_x = dh.reshape(bn, s, d)

    drel = band_bias_bwd(bucket, dbias_layers, "band_bias_bwd")
    gst['w_in'] = jnp.stack(gwin)
    full_shapes = {n: ws[n].shape for n in REPLICATED}
    full_shapes['conv_short'] = w_short.shape
    full_shapes['conv_dw'] = w_dw.shape
    gs = {n: jnp.stack([a.reshape(full_shapes[n][1:]) for a in gsmall[n]]) for n in gsmall}
    gs['rel_bias'] = jnp.transpose(drel[:, :, 0])
    return loss_part, grad_x, gst, gs


def _step(x, p, loss_target, ws, ms, vs):
    chip = 2 * lax.axis_index("x") + lax.axis_index("y")

    wf = gather_weights2({n: ws[n].astype(BF16) for n in SHARDED}, "gather_weights")
    w_in_all = wf['w_in']
    wf['w_in'] = jnp.transpose(w_in_all, (1, 2, 0, 3)).reshape(w_in_all.shape[1], w_in_all.shape[2], -1)
    conv_shapes = [ws[n].shape for n in CONV_SHARDED]
    conv_all = gather_small(_pack_small([ws[n] for n in CONV_SHARDED]), "gather_conv")
    conv_full = []
    for idx, n in enumerate(CONV_SHARDED):
        per_chip = [_unpack_small(conv_all[2 * j], conv_shapes)[idx] for j in range(N_CHIPS)]
        conv_full.append(jnp.concatenate(per_chip, axis=-1))
    w_short, w_dw = conv_full

    loss_part, grad_x, gst, gs = _local(x, p, loss_target, {n: ws[n] for n in REPLICATED}, wf, w_short, w_dw)

    c_arr = lax.axis_index("c").astype(jnp.int32).reshape(1)
    recv = rs_sibling(gst, "rs_sibling")
    chip_sum = {n: rs_add(_kind(n), gst[n], recv[n], c_arr, f"rs_add_{n}") for n in SHARDED}
    slots = rs_chips2(chip_sum, "rs_chips")
    sc_arr = jnp.stack([chip, lax.axis_index("c")]).astype(jnp.int32)
    red_half = {n: rs_sum2(_kind(n), chip_sum[n], slots[n], sc_arr, f"rs_sum_{n}") for n in SHARDED}
    g_shard = rs_join2(red_half, "rs_join")

    small_names = [n for n in REPLICATED + CONV_SHARDED]
    small_parts = [gs[n] for n in small_names]
    small_shapes = [g.shape for g in small_parts]
    small_parts.append(loss_part.reshape(1))
    small_shapes.append((1,))
    small_all = gather_small(_pack_small(small_parts), "gather_small")
    small_red = sum_slots(small_all, "small_sum")
    small_g = _unpack_small(small_red, small_shapes)
    loss = small_g[-1].reshape(())
    g_small = dict(zip(small_names, small_g[:-1]))

    grads = {}
    for n in WEIGHTS:
        if n in SHARDED:
            grads[n] = g_shard[n]
        elif n in CONV_SHARDED:
            wdt = ws[n].shape[-1]
            grads[n] = lax.dynamic_slice_in_dim(g_small[n], chip * wdt, wdt, axis=2)
        else:
            grads[n] = g_small[n]

    deltas, new_m, new_v = {}, {}, {}
    small_upd = [n for n in WEIGHTS if n not in SHARDED]
    for n in SHARDED:
        deltas[n], new_m[n], new_v[n] = adamw(ws[n], grads[n], ms[n], vs[n], f"adamw_{n}")
    shapes_u = [ws[n].shape for n in small_upd]
    packs = [_pack_small([src[n] for n in small_upd]) for src in (ws, grads, ms, vs)]
    upd = adamw(*packs, "adamw_small")
    for res, dst in zip(upd, (deltas, new_m, new_v)):
        for n, a in zip(small_upd, _unpack_small(res, shapes_u)):
            dst[n] = a

    return (loss, grad_x, *[grads[n] for n in WEIGHTS], *[deltas[n] for n in WEIGHTS],
            *[new_m[n] for n in WEIGHTS], *[new_v[n] for n in WEIGHTS])
```

```python
import functools
import math

import jax
import jax.numpy as jnp
from jax import lax
from jax.experimental import pallas as pl
from jax.experimental.pallas import tpu as pltpu

F32 = jnp.float32
BF16 = jnp.bfloat16
MESH = pl.DeviceIdType.MESH

D_MODEL = 1024
DEPTH = 4
HEAD_DIM = 64
A_HEADS = 4
D_Q_HEADS = 8
D_KV_HEADS = 2
D_GROUP = 4
WINDOW = 128
QB = 128
REL_BUCKETS = 32
REL_MAX_DIST = 128
D_FF = 2816
EPS = 1e-6
NEG = -1e30
SCALE = HEAD_DIM ** -0.5
N_CHIPS = 4
N_DEV = 8

ADAM_LR = 0.001
ADAM_B1 = 0.9
ADAM_B2 = 0.999
ADAM_EPS = 1e-08
ADAM_WD = 0.01
ADAM_STEP = 10

C_GATE = 0
C_AQ, C_AK, C_AV = 4096, 4352, 4608
C_BG, C_CG, C_XB = 4864, 5120, 5376
C_CA, C_CB = 5632, 5888
C_DQ, C_DK, C_DV = 6144, 6656, 6784
C_AF = 6912
N_PROJ = 7168

VMEM_LIMIT = 56 * 1024 * 1024
PACK_COLS = 1024
PACK_ROW_ALIGN = 1024

SHARDED = ('ffn1_w_gu', 'ffn1_w_down', 'w_in', 'w_br_a', 'w_br_b', 'w_br_c', 'w_br_d', 'w_o',
           'ffn2_w_gu', 'ffn2_w_down', 'w_ple_gate', 'w_ple')
ROW_SHARDED = ('ffn1_w_down', 'w_o', 'ffn2_w_down', 'w_ple_gate')
CONV_SHARDED = ('conv_short', 'conv_dw')
REPLICATED = ('ffn1_norm_pre', 'ffn1_norm_post', 'mix_norm_pre', 'b_forget', 'b_gate', 'conv_dw_bias',
              'conv_ln_gain', 'conv_ln_bias', 'attn_sinks', 'rel_bias', 'mix_norm_post', 'ffn2_norm_pre',
              'ffn2_norm_post', 'ple_norm_gate', 'ple_norm_post')
WEIGHTS = ('ffn1_norm_pre', 'ffn1_w_gu', 'ffn1_w_down', 'ffn1_norm_post', 'mix_norm_pre', 'w_in', 'b_forget',
           'b_gate', 'conv_short', 'conv_dw', 'conv_dw_bias', 'conv_ln_gain', 'conv_ln_bias', 'attn_sinks',
           'rel_bias', 'w_br_a', 'w_br_b', 'w_br_c', 'w_br_d', 'w_o', 'mix_norm_post', 'ffn2_norm_pre',
           'ffn2_w_gu', 'ffn2_w_down', 'ffn2_norm_post', 'ple_norm_gate', 'w_ple_gate', 'w_ple', 'ple_norm_post')


def _cparams(sem=None):
    return pltpu.CompilerParams(dimension_semantics=sem, vmem_limit_bytes=VMEM_LIMIT)


def _pick(dim, cands):
    for c in cands:
        if dim % c == 0:
            return c
    return dim


MM_VMEM_BUDGET = 40 * 1024 * 1024
MXU_FLOPS = 9.0e14
HBM_BYTES_PER_S = 3.0e12
GRID_STEP_S = 0.35e-6


def _divisors(dim, cands):
    out = [c for c in cands if c <= dim and dim % c == 0]
    return out or [dim]


def _mm_tiles(m, n, k, ab, bb, ob):
    best = None
    for tm in _divisors(m, (2048, 1408, 1024, 512, 256, 128)):
        for tn in _divisors(n, (2816, 2048, 1792, 1408, 1024, 512, 256, 128)):
            for tk in _divisors(k, (k if k <= 2048 else 2816, 2816, 2048, 1792, 1408, 1024, 512, 256, 128)):
                nk = k // tk
                vmem = 2 * (tm * tk * ab + tk * tn * bb + tm * tn * ob) + tm * tn * 4 * (2 if nk > 1 else 1)
                if vmem > MM_VMEM_BUDGET:
                    continue
                steps = (m // tm) * (n // tn) * nk
                a_bytes = m * k * ab * (1 if nk == 1 else n // tn)
                b_bytes = k * n * bb * (1 if (nk == 1 and n == tn) else m // tm)
                mem = (a_bytes + b_bytes + m * n * ob) / HBM_BYTES_PER_S
                acc = steps * tm * tn * 1.5e-12 if nk > 1 else 0.0
                cost = steps * GRID_STEP_S + max(2.0 * m * n * k / MXU_FLOPS, mem) + acc
                if best is None or cost < best[0]:
                    best = (cost, tm, tn, tk)
    assert best is not None, (m, n, k)
    return best[1:]


def _mm(a, b, *, ta=False, tb=False, out_dtype=F32, name="mm", stack=None):
    b_layer = None
    if isinstance(b, tuple):
        b, b_layer = b
    if ta:
        kdim, m = a.shape
    else:
        m, kdim = a.shape
    if tb:
        n, kb = b.shape[-2:]
    else:
        kb, n = b.shape[-2:]
    assert kb == kdim, (a.shape, b.shape, ta, tb)
    tm, tn, tk = _mm_tiles(m, n, kdim, a.dtype.itemsize, b.dtype.itemsize, jnp.dtype(out_dtype).itemsize)
    nk = kdim // tk
    dims = (((0,) if ta else (1,), (1,) if tb else (0,)), ((), ()))

    def dot(a_ref, b_ref):
        return lax.dot_general(a_ref[...].astype(BF16), b_ref[...].astype(BF16), dims, preferred_element_type=F32)

    if nk == 1:
        def body(a_ref, b_ref, *rest):
            o_ref = rest[-1]
            o_ref[...] = dot(a_ref, b_ref).astype(o_ref.dtype)
        scratch = []
    else:
        def body(a_ref, b_ref, *rest):
            o_ref, acc_ref = rest[-2], rest[-1]
            k = pl.program_id(2)

            @pl.when(k == 0)
            def _():
                acc_ref[...] = dot(a_ref, b_ref)

            @pl.when(jnp.logical_and(k > 0, k < nk - 1))
            def _():
                acc_ref[...] += dot(a_ref, b_ref)

            @pl.when(k == nk - 1)
            def _():
                o_ref[...] = (acc_ref[...] + dot(a_ref, b_ref)).astype(o_ref.dtype)
        scratch = [pltpu.VMEM((tm, tn), F32)]

    a_spec = pl.BlockSpec((tk, tm), lambda i, j, k: (k, i)) if ta else pl.BlockSpec((tm, tk), lambda i, j, k: (i, k))
    if b_layer is None:
        b_spec = (pl.BlockSpec((tn, tk), lambda i, j, k: (j, k)) if tb
                  else pl.BlockSpec((tk, tn), lambda i, j, k: (k, j)))
    else:
        b_spec = (pl.BlockSpec((None, tn, tk), lambda i, j, k: (b_layer, j, k)) if tb
                  else pl.BlockSpec((None, tk, tn), lambda i, j, k: (b_layer, k, j)))
    in_specs, operands, aliases = [a_spec, b_spec], [a, b], {}
    if stack is None:
        out_spec = pl.BlockSpec((tm, tn), lambda i, j, k: (i, j))
        out_shape = jax.ShapeDtypeStruct((m, n), out_dtype)
    else:
        buf, depth, layer = stack
        out_spec = pl.BlockSpec((None, tm, tn), lambda i, j, k: (layer, i, j))
        out_shape = jax.ShapeDtypeStruct((depth, m, n), out_dtype)
        if buf is not None:
            in_specs.append(pl.BlockSpec(memory_space=pl.ANY))
            operands.append(buf)
            aliases = {2: 0}
    return pl.pallas_call(
        body, name=name, grid=(m // tm, n // tn, nk),
        in_specs=in_specs, out_specs=out_spec, out_shape=out_shape, scratch_shapes=scratch,
        input_output_aliases=aliases,
        compiler_params=_cparams(("parallel", "parallel", "arbitrary")),
    )(*operands)


def _rowwise(fn, rows, params, outs, pouts=(), *, tm=256, name="rowwise"):
    t = rows[0][0].shape[0]
    assert t % tm == 0
    n_r, n_p, n_o, n_po = len(rows), len(params), len(outs), len(pouts)

    def body(*refs):
        r_refs = refs[:n_r]
        p_refs = refs[n_r:n_r + n_p]
        o_refs = refs[n_r + n_p:n_r + n_p + n_o]
        po_refs = refs[n_r + n_p + n_o:]
        res = fn(*[r[...] for r in r_refs], *[p[...] for p in p_refs])
        if not isinstance(res, (tuple, list)):
            res = (res,)
        assert len(res) == n_o + n_po, (len(res), n_o, n_po)
        for o, val in zip(o_refs, res[:n_o]):
            o[...] = val.astype(o.dtype)
        if n_po:
            first = pl.program_id(0) == 0

            @pl.when(first)
            def _():
                for o, val in zip(po_refs, res[n_o:]):
                    o[...] = val.astype(F32)

            @pl.when(jnp.logical_not(first))
            def _():
                for o, val in zip(po_refs, res[n_o:]):
                    o[...] += val.astype(F32)

    in_specs = [pl.BlockSpec((tm, w), functools.partial(lambda i, cb: (i, cb), cb=cb)) for (_, w, cb) in rows]
    in_specs += [pl.BlockSpec(p.shape, lambda i: (0, 0)) for p in params]
    out_specs = [pl.BlockSpec((tm, w), lambda i: (i, 0)) for (w, _) in outs]
    out_specs += [pl.BlockSpec((1, w), lambda i: (0, 0)) for w in pouts]
    out_shape = [jax.ShapeDtypeStruct((t, w), dt) for (w, dt) in outs]
    out_shape += [jax.ShapeDtypeStruct((1, w), F32) for w in pouts]
    res = pl.pallas_call(
        body, name=name, grid=(t // tm,), in_specs=in_specs, out_specs=out_specs, out_shape=out_shape,
        compiler_params=_cparams(("arbitrary",)),
    )(*[r[0] for r in rows], *params)
    return res


def _full(a):
    return (a, a.shape[1], 0)


def _rms(x, g):
    x = x.astype(F32)
    return x * lax.rsqrt(jnp.mean(x * x, axis=-1, keepdims=True) + EPS) * g


def _sum0(v):
    return jnp.sum(v, axis=0, keepdims=True)


def rms_fwd(h, g, name):
    return _rowwise(lambda x, gg: _rms(x, gg), [_full(h)], [g], [(h.shape[1], BF16)], tm=512, name=name)[0]


def rms_bwd(h, g, dn, dres, name):
    def fn(x, d, r, gg):
        _, vjp = jax.vjp(_rms, x, gg)
        dx, dg = vjp(d.astype(F32))
        return dx + r, dg
    w = h.shape[1]
    return _rowwise(fn, [_full(h), _full(dn), _full(dres)], [g], [(w, F32)], [w], tm=512, name=name)


def res_rms_fwd(h, f, g, coef, name):
    return _rowwise(lambda x, y, gg: x + coef * _rms(y, gg), [_full(h), _full(f)], [g], [(h.shape[1], F32)],
                    tm=512, name=name)[0]


def res_rms_bwd(f, g, dh, coef, name):
    def fn(y, d, gg):
        _, vjp = jax.vjp(lambda a, b: coef * _rms(a, b), y, gg)
        dy, dg = vjp(d)
        return dy, dg
    w = f.shape[1]
    return _rowwise(fn, [_full(f), _full(dh)], [g], [(w, BF16)], [w], tm=512, name=name)


def swiglu_fwd(gu, name):
    f = gu.shape[1] // 2

    def fn(gate, up):
        gate = gate.astype(F32)
        return gate * jax.nn.sigmoid(gate) * up.astype(F32)
    return _rowwise(fn, [(gu, f, 0), (gu, f, 1)], [], [(f, BF16)], tm=512, name=name)[0]


def swiglu_bwd(gu, da, name):
    t, f2 = gu.shape
    f = f2 // 2
    tm = 256

    def body(gate_ref, up_ref, da_ref, o_ref):
        gate = gate_ref[...].astype(F32)
        up = up_ref[...].astype(F32)
        d = da_ref[...].astype(F32)
        sg = jax.nn.sigmoid(gate)
        silu = gate * sg
        o_ref[:, :f] = (d * up * (sg + silu * (1.0 - sg))).astype(o_ref.dtype)
        o_ref[:, f:] = (d * silu).astype(o_ref.dtype)

    return pl.pallas_call(
        body, name=name, grid=(t // tm,),
        in_specs=[pl.BlockSpec((tm, f), lambda i: (i, 0)), pl.BlockSpec((tm, f), lambda i: (i, 1)),
                  pl.BlockSpec((tm, f), lambda i: (i, 0))],
        out_specs=pl.BlockSpec((tm, f2), lambda i: (i, 0)),
        out_shape=jax.ShapeDtypeStruct((t, f2), BF16),
        compiler_params=_cparams(("parallel",)),
    )(gu, gu, da)


def _merge(g0, g1, g2, g3, z0, z1, z2, z3, b0, b1, b2, b3):
    acc = jax.nn.sigmoid(g0.astype(F32) + b0) * z0.astype(F32)
    acc += jax.nn.sigmoid(g1.astype(F32) + b1) * z1.astype(F32)
    acc += jax.nn.sigmoid(g2.astype(F32) + b2) * z2.astype(F32)
    acc += jax.nn.sigmoid(g3.astype(F32) + b3) * z3.astype(F32)
    return acc


def merge_fwd(proj, zs, bs, name):
    rows = [(proj, D_MODEL, k) for k in range(4)] + [_full(z) for z in zs]
    return _rowwise(_merge, rows, list(bs), [(D_MODEL, BF16)], tm=512, name=name)[0]


def merge_bwd(proj, zs, bs, dmerged, name):
    def fn(*args):
        d = args[8].astype(F32)
        prim = args[:8] + args[9:]
        _, vjp = jax.vjp(_merge, *prim)
        return vjp(d)
    rows = [(proj, D_MODEL, k) for k in range(4)] + [_full(z) for z in zs] + [_full(dmerged)]
    outs = [(D_MODEL, BF16)] * 8
    return _rowwise(fn, rows, list(bs), outs, [D_MODEL] * 4, name=name)


def _ple(pgl, pr, g):
    return jax.nn.sigmoid(pgl.astype(F32)) * _rms(pr, g)


def ple_fwd(h, pgl, pr, g, name):
    return _rowwise(lambda x, a, b, gg: x + _ple(a, b, gg), [_full(h), _full(pgl), _full(pr)], [g],
                    [(D_MODEL, F32)], tm=512, name=name)[0]


def ple_bwd(pgl, pr, g, dh, name):
    def fn(a, b, d, gg):
        _, vjp = jax.vjp(_ple, a, b, gg)
        return vjp(d)
    return _rowwise(fn, [_full(pgl), _full(pr), _full(dh)], [g], [(D_MODEL, BF16), (D_MODEL, BF16)], [D_MODEL],
                    tm=512, name=name)


def loss_fwd_bwd(y, target, name):
    def fn(a, b):
        err = a - b
        return err * (1.0 / D_MODEL), _sum0(err * err) * (0.5 / D_MODEL)
    return _rowwise(fn, [_full(y), _full(target)], [], [(D_MODEL, F32)], [D_MODEL], tm=512, name=name)


def _shift_down(x, d, row):
    if d == 0:
        return x
    return jnp.where(row >= d, pltpu.roll(x, d, 0), 0.0)


def _shift_up(x, d, row):
    if d == 0:
        return x
    s = x.shape[0]
    return jnp.where(row < s - d, pltpu.roll(x, s - d, 0), 0.0)


def fgate_fwd(proj, bf, bn, s, name):
    def body(a_ref, b_ref, o_ref):
        x = a_ref[...].astype(F32) + b_ref[...]
        c = jnp.minimum(x, 0.0) - jnp.log(1.0 + jnp.exp(-jnp.abs(x)))
        row = lax.broadcasted_iota(jnp.int32, c.shape, 0)
        sh = 1
        while sh < s:
            c = c + _shift_down(c, sh, row)
            sh *= 2
        o_ref[...] = c

    return pl.pallas_call(
        body, name=name, grid=(bn,),
        in_specs=[pl.BlockSpec((s, 128), lambda b: (b, C_AF // 128)), pl.BlockSpec((1, 128), lambda b: (0, 0))],
        out_specs=pl.BlockSpec((s, 128), lambda b: (b, 0)),
        out_shape=jax.ShapeDtypeStruct((bn * s, 128), F32),
        compiler_params=_cparams(("parallel",)),
    )(proj, bf)


def fgate_bwd(proj, bf, dc, bn, s, name):
    def body(a_ref, b_ref, dc_ref, da_ref, db_ref):
        x = a_ref[...].astype(F32) + b_ref[...]
        d = dc_ref[...]
        row = lax.broadcasted_iota(jnp.int32, d.shape, 0)
        sh = 1
        while sh < s:
            d = d + _shift_up(d, sh, row)
            sh *= 2
        da = d * jax.nn.sigmoid(-x)
        da_ref[...] = da.astype(da_ref.dtype)
        first = pl.program_id(0) == 0

        @pl.when(first)
        def _():
            db_ref[...] = _sum0(da)

        @pl.when(jnp.logical_not(first))
        def _():
            db_ref[...] += _sum0(da)

    return pl.pallas_call(
        body, name=name, grid=(bn,),
        in_specs=[pl.BlockSpec((s, 128), lambda b: (b, C_AF // 128)), pl.BlockSpec((1, 128), lambda b: (0, 0)),
                  pl.BlockSpec((s, 128), lambda b: (b, 0))],
        out_specs=[pl.BlockSpec((s, 128), lambda b: (b, 0)), pl.BlockSpec((1, 128), lambda b: (0, 0))],
        out_shape=[jax.ShapeDtypeStruct((bn * s, 128), BF16), jax.ShapeDtypeStruct((1, 128), F32)],
        compiler_params=_cparams(("arbitrary",)),
    )(proj, bf, dc)


FOX_T = 512


def _fox_scores(q, k, cq, ck, j, i):
    t = FOX_T
    s = lax.dot_general(q, k, (((1,), (1,)), ((), ())), preferred_element_type=F32) * SCALE
    qpos = j * t + lax.broadcasted_iota(jnp.int32, (t, t), 0)
    kpos = i * t + lax.broadcasted_iota(jnp.int32, (t, t), 1)
    return jnp.where(qpos >= kpos, s + (cq - ck), NEG)


def fox_fwd(q, k, v, c_col, c_row, name):
    bn, h, s, d = q.shape
    t = FOX_T
    nq = s // t

    def body(q_ref, k_ref, v_ref, cq_ref, ck_ref, o_ref, lse_ref):
        j = pl.program_id(2)
        qv = q_ref[...]
        cq = cq_ref[...]

        def step(i, carry):
            m, l, acc = carry
            ks = pl.multiple_of(i * t, t)
            kc = k_ref[pl.ds(ks, t), :]
            vc = v_ref[pl.ds(ks, t), :]
            sc = _fox_scores(qv, kc, cq, ck_ref[i], j, i)
            m_new = jnp.maximum(m, jnp.max(sc, axis=-1, keepdims=True))
            alpha = jnp.exp(m - m_new)
            p = jnp.exp(sc - m_new)
            l = alpha * l + jnp.sum(p, axis=-1, keepdims=True)
            acc = alpha * acc + jnp.dot(p.astype(BF16), vc, preferred_element_type=F32)
            return m_new, l, acc

        init = (jnp.full((t, 1), NEG, F32), jnp.zeros((t, 1), F32), jnp.zeros((t, d), F32))
        m, l, acc = lax.fori_loop(0, j + 1, step, init)
        o_ref[...] = (acc / l).astype(o_ref.dtype)
        lse_ref[...] = m + jnp.log(l)

    blk_q = pl.BlockSpec((None, None, t, d), lambda b, hh, j: (b, hh, j, 0))
    blk_kv = pl.BlockSpec((None, None, s, d), lambda b, hh, j: (b, hh, 0, 0))
    blk_c1 = pl.BlockSpec((None, None, t, 1), lambda b, hh, j: (b, hh, j, 0))
    blk_cr = pl.BlockSpec((None, None, nq, 1, t), lambda b, hh, j: (b, hh, 0, 0, 0))
    return pl.pallas_call(
        body, name=name, grid=(bn, h, nq),
        in_specs=[blk_q, blk_kv, blk_kv, blk_c1, blk_cr],
        out_specs=[blk_q, blk_c1],
        out_shape=[jax.ShapeDtypeStruct((bn, h, s, d), F32), jax.ShapeDtypeStruct((bn, h, s, 1), F32)],
        compiler_params=_cparams(("parallel", "parallel", "arbitrary")),
    )(q, k, v, c_col, c_row)


def fox_bwd(q, k, v, c_col, c_row, o, lse, do, name):
    bn, h, s, d = q.shape
    t = FOX_T
    nq = s // t

    def body(q_ref, k_ref, v_ref, cq_ref, ck_ref, o_ref, lse_ref, do_ref, dq_ref, dk_ref, dv_ref, dck_ref,
             dcq_ref):
        j = pl.program_id(2)

        @pl.when(j == 0)
        def _():
            dk_ref[...] = jnp.zeros_like(dk_ref)
            dv_ref[...] = jnp.zeros_like(dv_ref)
            dck_ref[...] = jnp.zeros_like(dck_ref)

        qv = q_ref[...]
        cq = cq_ref[...]
        dov = do_ref[...]
        lse = lse_ref[...]
        delta = jnp.sum(dov.astype(F32) * o_ref[...].astype(F32), axis=-1, keepdims=True)

        def step(i, carry):
            dq, dcq = carry
            ks = pl.multiple_of(i * t, t)
            kc = k_ref[pl.ds(ks, t), :]
            vc = v_ref[pl.ds(ks, t), :]
            sc = _fox_scores(qv, kc, cq, ck_ref[i], j, i)
            p = jnp.exp(sc - lse)
            dp = lax.dot_general(dov, vc, (((1,), (1,)), ((), ())), preferred_element_type=F32)
            ds = p * (dp - delta)
            dsb = ds.astype(BF16)
            dq = dq + jnp.dot(dsb, kc, preferred_element_type=F32) * SCALE
            dk_ref[pl.ds(ks, t), :] += lax.dot_general(dsb, qv, (((0,), (0,)), ((), ())),
                                                       preferred_element_type=F32) * SCALE
            dv_ref[pl.ds(ks, t), :] += lax.dot_general(p.astype(BF16), dov, (((0,), (0,)), ((), ())),
                                                       preferred_element_type=F32)
            dck_ref[i] += -_sum0(ds)
            return dq, dcq + jnp.sum(ds, axis=-1, keepdims=True)

        dq, dcq = lax.fori_loop(0, j + 1, step, (jnp.zeros((t, d), F32), jnp.zeros((t, 1), F32)))
        dq_ref[...] = dq
        dcq_ref[...] = dcq

    blk_q = pl.BlockSpec((None, None, t, d), lambda b, hh, j: (b, hh, j, 0))
    blk_kv = pl.BlockSpec((None, None, s, d), lambda b, hh, j: (b, hh, 0, 0))
    blk_c1 = pl.BlockSpec((None, None, t, 1), lambda b, hh, j: (b, hh, j, 0))
    blk_cr = pl.BlockSpec((None, None, nq, 1, t), lambda b, hh, j: (b, hh, 0, 0, 0))
    return pl.pallas_call(
        body, name=name, grid=(bn, h, nq),
        in_specs=[blk_q, blk_kv, blk_kv, blk_c1, blk_cr, blk_q, blk_c1, blk_q],
        out_specs=[blk_q, blk_kv, blk_kv, blk_cr, blk_c1],
        out_shape=[jax.ShapeDtypeStruct((bn, h, s, d), F32), jax.ShapeDtypeStruct((bn, h, s, d), F32),
                   jax.ShapeDtypeStruct((bn, h, s, d), F32), jax.ShapeDtypeStruct((bn, h, nq, 1, t), F32),
                   jax.ShapeDtypeStruct((bn, h, s, 1), F32)],
        compiler_params=_cparams(("parallel", "parallel", "arbitrary")),
    )(q, k, v, c_col, c_row, o, lse, do)


def _swa_valid(n):
    qi = lax.broadcasted_iota(jnp.int32, (QB, 2 * QB), 0)
    kj = lax.broadcasted_iota(jnp.int32, (QB, 2 * QB), 1)
    dist = qi + QB - kj
    return (dist >= 0) & (dist < WINDOW) & ((kj >= QB) | (n > 0))


def _swa_band(ref, n):
    qs = pl.multiple_of(n * QB, QB)
    ps = pl.multiple_of(jnp.maximum(n - 1, 0) * QB, QB)
    return jnp.concatenate([ref[pl.ds(ps, QB), :], ref[pl.ds(qs, QB), :]], axis=0), qs, ps


def swa_fwd(q, k, v, bias, sinks, name):
    bn, hq, s, d = q.shape
    nb = s // QB

    def body(q_ref, k_ref, v_ref, b_ref, s_ref, o_ref, lse_ref):
        def step(n, _):
            kb, qs, _ps = _swa_band(k_ref, n)
            vb, _, _ = _swa_band(v_ref, n)
            valid = _swa_valid(n)
            for g in range(D_GROUP):
                qg = q_ref[g, pl.ds(qs, QB), :]
                sc = lax.dot_general(qg, kb, (((1,), (1,)), ((), ())), preferred_element_type=F32) * SCALE
                sc = jnp.where(valid, sc + b_ref[g], NEG)
                sink = s_ref[g]
                m = jnp.maximum(jnp.max(sc, axis=-1, keepdims=True), sink)
                e = jnp.exp(sc - m)
                z = jnp.sum(e, axis=-1, keepdims=True) + jnp.exp(sink - m)
                p = e / z
                o_ref[g, pl.ds(qs, QB), :] = jnp.dot(p.astype(BF16), vb, preferred_element_type=F32
                                                     ).astype(o_ref.dtype)
                lse_ref[g, pl.ds(qs, QB), :] = m + jnp.log(z)
            return 0

        lax.fori_loop(0, nb, step, 0, unroll=2)

    blk_q = pl.BlockSpec((None, D_GROUP, s, d), lambda b, kh: (b, kh, 0, 0))
    blk_kv = pl.BlockSpec((None, None, s, d), lambda b, kh: (b, kh, 0, 0))
    blk_l = pl.BlockSpec((None, D_GROUP, s, 1), lambda b, kh: (b, kh, 0, 0))
    return pl.pallas_call(
        body, name=name, grid=(bn, D_KV_HEADS),
        in_specs=[blk_q, blk_kv, blk_kv, pl.BlockSpec((D_GROUP, QB, 2 * QB), lambda b, kh: (kh, 0, 0)),
                  pl.BlockSpec((D_GROUP, QB, 1), lambda b, kh: (kh, 0, 0))],
        out_specs=[blk_q, blk_l],
        out_shape=[jax.ShapeDtypeStruct((bn, hq, s, d), BF16), jax.ShapeDtypeStruct((bn, hq, s, 1), F32)],
        compiler_params=_cparams(("parallel", "parallel")),
    )(q, k, v, bias, sinks)


def swa_bwd(q, k, v, bias, sinks, o, lse, do, name):
    bn, hq, s, d = q.shape
    nb = s // QB

    def body(q_ref, k_ref, v_ref, b_ref, s_ref, o_ref, lse_ref, do_ref, dq_ref, dk_ref, dv_ref, db_ref, dsk_ref):
        @pl.when(pl.program_id(1) == 0)
        def _():
            db_ref[...] = jnp.zeros_like(db_ref)
            dsk_ref[...] = jnp.zeros_like(dsk_ref)

        dk_ref[...] = jnp.zeros_like(dk_ref)
        dv_ref[...] = jnp.zeros_like(dv_ref)

        def step(n, _):
            kb, qs, ps = _swa_band(k_ref, n)
            vb, _, _ = _swa_band(v_ref, n)
            valid = _swa_valid(n)
            dkb = jnp.zeros((2 * QB, d), F32)
            dvb = jnp.zeros((2 * QB, d), F32)
            for g in range(D_GROUP):
                qg = q_ref[g, pl.ds(qs, QB), :]
                dog = do_ref[g, pl.ds(qs, QB), :]
                og = o_ref[g, pl.ds(qs, QB), :]
                lse = lse_ref[g, pl.ds(qs, QB), :]
                sc = lax.dot_general(qg, kb, (((1,), (1,)), ((), ())), preferred_element_type=F32) * SCALE
                sc = jnp.where(valid, sc + b_ref[g], NEG)
                p = jnp.exp(sc - lse)
                delta = jnp.sum(dog.astype(F32) * og.astype(F32), axis=-1, keepdims=True)
                dp = lax.dot_general(dog, vb, (((1,), (1,)), ((), ())), preferred_element_type=F32)
                ds = p * (dp - delta)
                dsb = ds.astype(BF16)
                dq_ref[g, pl.ds(qs, QB), :] = jnp.dot(dsb, kb, preferred_element_type=F32) * SCALE
                dkb = dkb + lax.dot_general(dsb, qg, (((0,), (0,)), ((), ())), preferred_element_type=F32) * SCALE
                dvb = dvb + lax.dot_general(p.astype(BF16), dog, (((0,), (0,)), ((), ())),
                                            preferred_element_type=F32)
                db_ref[g] += ds
                dsk_ref[g] += -jnp.exp(s_ref[g] - lse) * delta
            dk_ref[pl.ds(ps, QB), :] += dkb[:QB]
            dk_ref[pl.ds(qs, QB), :] += dkb[QB:]
            dv_ref[pl.ds(ps, QB), :] += dvb[:QB]
            dv_ref[pl.ds(qs, QB), :] += dvb[QB:]
            return 0

        lax.fori_loop(0, nb, step, 0, unroll=2)

    blk_q = pl.BlockSpec((None, D_GROUP, s, d), lambda kh, b: (b, kh, 0, 0))
    blk_kv = pl.BlockSpec((None, None, s, d), lambda kh, b: (b, kh, 0, 0))
    blk_l = pl.BlockSpec((None, D_GROUP, s, 1), lambda kh, b: (b, kh, 0, 0))
    blk_b = pl.BlockSpec((D_GROUP, QB, 2 * QB), lambda kh, b: (kh, 0, 0))
    blk_s = pl.BlockSpec((D_GROUP, QB, 1), lambda kh, b: (kh, 0, 0))
    return pl.pallas_call(
        body, name=name, grid=(D_KV_HEADS, bn),
        in_specs=[blk_q, blk_kv, blk_kv, blk_b, blk_s, blk_q, blk_l, blk_q],
        out_specs=[blk_q, blk_kv, blk_kv, blk_b, blk_s],
        out_shape=[jax.ShapeDtypeStruct((bn, hq, s, d), F32), jax.ShapeDtypeStruct((bn, D_KV_HEADS, s, d), F32),
                   jax.ShapeDtypeStruct((bn, D_KV_HEADS, s, d), F32),
                   jax.ShapeDtypeStruct((hq, QB, 2 * QB), F32), jax.ShapeDtypeStruct((hq, QB, 1), F32)],
        compiler_params=_cparams(("parallel", "arbitrary")),
    )(q, k, v, bias, sinks, o, lse, do)


def _sel(nh, width):
    r = lax.broadcasted_iota(jnp.int32, (width, HEAD_DIM), 0)
    c = lax.broadcasted_iota(jnp.int32, (width, HEAD_DIM), 1)
    return [(r == c + HEAD_DIM * h).astype(BF16) for h in range(nh)]


def _pick_head(x, e):
    return jnp.dot(x, e, preferred_element_type=F32).astype(BF16)


def _place_head(x, e):
    return lax.dot_general(x.astype(BF16), e, (((1,), (1,)), ((), ())), preferred_element_type=F32)


def fox2_fwd(proj, c_col, c_row, bn, s, name):
    t = FOX_T
    nq = s // t
    nh, d = A_HEADS, HEAD_DIM

    def body(q_ref, k_ref, v_ref, cq_ref, ck_ref, y_ref, o_ref, lse_ref, kh_ref, vh_ref):
        j = pl.program_id(1)
        es = _sel(nh, 256)

        @pl.when(j == 0)
        def _():
            for h in range(nh):
                kh_ref[h] = _pick_head(k_ref[...], es[h])
                vh_ref[h] = _pick_head(v_ref[...], es[h])

        q4 = q_ref[...]
        qs = [_pick_scaled(q4, es[h]) for h in range(nh)]
        cqs = [cq_ref[h] for h in range(nh)]

        def chunk(i, carry, masked):
            ks = pl.multiple_of(i * t, t)
            out = []
            for h in range(nh):
                m, l, acc = carry[h]
                sc = _fox_scores2(qs[h], kh_ref[h, pl.ds(ks, t), :], cqs[h], ck_ref[h, i], masked)
                m_new = jnp.maximum(m, jnp.max(sc, axis=-1, keepdims=True))
                alpha = jnp.exp(m - m_new)
                p = jnp.exp(sc - m_new)
                l = alpha * l + jnp.sum(p, axis=-1, keepdims=True)
                acc = alpha * acc + jnp.dot(p.astype(BF16), vh_ref[h, pl.ds(ks, t), :], preferred_element_type=F32)
                out.append((m_new, l, acc))
            return tuple(out)

        init = tuple((jnp.full((t, 1), NEG, F32), jnp.zeros((t, 1), F32), jnp.zeros((t, d), F32)) for _ in range(nh))
        res = lax.fori_loop(0, j, lambda i, carry: chunk(i, carry, False), init)
        res = chunk(j, res, True)
        y = jnp.zeros((t, 256), F32)
        for h in range(nh):
            m, l, acc = res[h]
            o = acc / l
            o_ref[h] = o
            lse_ref[h] = m + jnp.log(l)
            y = y + _place_head(o, es[h])
        y_ref[...] = y.astype(y_ref.dtype)

    blk_q = pl.BlockSpec((t, 256), lambda b, j: (b * nq + j, C_AQ // 256))
    blk_k = pl.BlockSpec((s, 256), lambda b, j: (b, C_AK // 256))
    blk_v = pl.BlockSpec((s, 256), lambda b, j: (b, C_AV // 256))
    blk_c1 = pl.BlockSpec((None, nh, t, 1), lambda b, j: (b, 0, j, 0))
    blk_cr = pl.BlockSpec((None, nh, nq, 1, t), lambda b, j: (b, 0, 0, 0, 0))
    blk_o = pl.BlockSpec((None, nh, t, d), lambda b, j: (b, 0, j, 0))
    return pl.pallas_call(
        body, name=name, grid=(bn, nq),
        in_specs=[blk_q, blk_k, blk_v, blk_c1, blk_cr],
        out_specs=[pl.BlockSpec((t, 256), lambda b, j: (b * nq + j, 0)), blk_o, blk_c1],
        out_shape=[jax.ShapeDtypeStruct((bn * s, 256), BF16), jax.ShapeDtypeStruct((bn, nh, s, d), F32),
                   jax.ShapeDtypeStruct((bn, nh, s, 1), F32)],
        scratch_shapes=[pltpu.VMEM((nh, s, d), BF16), pltpu.VMEM((nh, s, d), BF16)],
        compiler_params=_cparams(("arbitrary", "arbitrary")),
    )(proj, proj, proj, c_col, c_row)


def fox2_bwd(proj, c_col, c_row, o, lse, dya, bn, s, name):
    t = FOX_T
    nq = s // t
    nh, d = A_HEADS, HEAD_DIM

    def body(q_ref, k_ref, v_ref, cq_ref, ck_ref, o_ref, lse_ref, dy_ref, dq_ref, dk_ref, dv_ref, dck_ref, dcq_ref,
             kh_ref, vh_ref, dkh_ref, dvh_ref):
        j = pl.program_id(1)
        es = _sel(nh, 256)

        @pl.when(j == 0)
        def _():
            for h in range(nh):
                kh_ref[h] = _pick_head(k_ref[...], es[h])
                vh_ref[h] = _pick_head(v_ref[...], es[h])
            dkh_ref[...] = jnp.zeros_like(dkh_ref)
            dvh_ref[...] = jnp.zeros_like(dvh_ref)
            dck_ref[...] = jnp.zeros_like(dck_ref)

        q4 = q_ref[...]
        dy4 = dy_ref[...]
        qs = [_pick_scaled(q4, es[h]) for h in range(nh)]
        dos = [_pick_head(dy4, es[h]) for h in range(nh)]
        cqs = [cq_ref[h] for h in range(nh)]
        lses = [lse_ref[h] for h in range(nh)]
        deltas = [jnp.sum(dos[h].astype(F32) * o_ref[h], axis=-1, keepdims=True) for h in range(nh)]

        def chunk(i, carry, masked):
            ks = pl.multiple_of(i * t, t)
            out = []
            for h in range(nh):
                dq, dcq = carry[h]
                kc = kh_ref[h, pl.ds(ks, t), :]
                sc = _fox_scores2(qs[h], kc, cqs[h], ck_ref[h, i], masked)
                p = jnp.exp(sc - lses[h])
                dp = lax.dot_general(dos[h], vh_ref[h, pl.ds(ks, t), :], (((1,), (1,)), ((), ())),
                                     preferred_element_type=F32)
                ds = p * (dp - deltas[h])
                dsb = ds.astype(BF16)
                dq = dq + jnp.dot(dsb, kc, preferred_element_type=F32)
                dkh_ref[h, pl.ds(ks, t), :] += lax.dot_general(dsb, qs[h], (((0,), (0,)), ((), ())),
                                                               preferred_element_type=F32)
                dvh_ref[h, pl.ds(ks, t), :] += lax.dot_general(p.astype(BF16), dos[h], (((0,), (0,)), ((), ())),
                                                               preferred_element_type=F32)
                dck_ref[h, i] += -_sum0(ds)
                out.append((dq, dcq + jnp.sum(ds, axis=-1, keepdims=True)))
            return tuple(out)

        init = tuple((jnp.zeros((t, d), F32), jnp.zeros((t, 1), F32)) for _ in range(nh))
        res = lax.fori_loop(0, j, lambda i, carry: chunk(i, carry, False), init)
        res = chunk(j, res, True)
        dq4 = jnp.zeros((t, 256), F32)
        for h in range(nh):
            dq4 = dq4 + _place_head(res[h][0] * SCALE, es[h])
            dcq_ref[h] = res[h][1]
        dq_ref[...] = dq4.astype(dq_ref.dtype)

        @pl.when(j == nq - 1)
        def _():
            dk4 = jnp.zeros((s, 256), F32)
            dv4 = jnp.zeros((s, 256), F32)
            for h in range(nh):
                dk4 = dk4 + _place_head(dkh_ref[h], es[h])
                dv4 = dv4 + _place_head(dvh_ref[h], es[h])
            dk_ref[...] = dk4.astype(dk_ref.dtype)
            dv_ref[...] = dv4.astype(dv_ref.dtype)

    blk_q = pl.BlockSpec((t, 256), lambda b, j: (b * nq + j, C_AQ // 256))
    blk_k = pl.BlockSpec((s, 256), lambda b, j: (b, C_AK // 256))
    blk_v = pl.BlockSpec((s, 256), lambda b, j: (b, C_AV // 256))
    blk_c1 = pl.BlockSpec((None, nh, t, 1), lambda b, j: (b, 0, j, 0))
    blk_cr = pl.BlockSpec((None, nh, nq, 1, t), lambda b, j: (b, 0, 0, 0, 0))
    blk_o = pl.BlockSpec((None, nh, t, d), lambda b, j: (b, 0, j, 0))
    blk_t = pl.BlockSpec((t, 256), lambda b, j: (b * nq + j, 0))
    blk_s = pl.BlockSpec((s, 256), lambda b, j: (b, 0))
    return pl.pallas_call(
        body, name=name, grid=(bn, nq),
        in_specs=[blk_q, blk_k, blk_v, blk_c1, blk_cr, blk_o, blk_c1, blk_t],
        out_specs=[blk_t, blk_s, blk_s, blk_cr, blk_c1],
        out_shape=[jax.ShapeDtypeStruct((bn * s, 256), BF16)] * 3
        + [jax.ShapeDtypeStruct((bn, nh, nq, 1, t), F32), jax.ShapeDtypeStruct((bn, nh, s, 1), F32)],
        scratch_shapes=[pltpu.VMEM((nh, s, d), BF16), pltpu.VMEM((nh, s, d), BF16),
                        pltpu.VMEM((nh, s, d), F32), pltpu.VMEM((nh, s, d), F32)],
        compiler_params=_cparams(("arbitrary", "arbitrary")),
    )(proj, proj, proj, c_col, c_row, o, lse, dya)


def _band3(ref, h, n):
    qs = pl.multiple_of(n * QB, QB)
    ps = pl.multiple_of(jnp.maximum(n - 1, 0) * QB, QB)
    return jnp.concatenate([ref[h, pl.ds(ps, QB), :], ref[h, pl.ds(qs, QB), :]], axis=0), qs, ps


def swa2_fwd(proj, bias, sinks, bn, s, name):
    nb = s // QB
    d = HEAD_DIM

    def body(q0_ref, q1_ref, k_ref, v_ref, b_ref, s_ref, y_ref, lse_ref, qh_ref, kh_ref, vh_ref):
        e4 = _sel(D_GROUP, 256)
        e2 = _sel(D_KV_HEADS, 128)
        for kh, q_ref in enumerate((q0_ref, q1_ref)):
            kh_ref[kh] = _pick_head(k_ref[...], e2[kh])
            vh_ref[kh] = _pick_head(v_ref[...], e2[kh])
            for g in range(D_GROUP):
                qh_ref[D_GROUP * kh + g] = _pick_head(q_ref[...], e4[g])

        def step(n, _):
            valid = _swa_valid(n)
            for kh in range(D_KV_HEADS):
                kb, qs, _ps = _band3(kh_ref, kh, n)
                vb, _, _ = _band3(vh_ref, kh, n)
                y = jnp.zeros((QB, 256), F32)
                for g in range(D_GROUP):
                    hh = D_GROUP * kh + g
                    qg = qh_ref[hh, pl.ds(qs, QB), :]
                    sc = lax.dot_general(qg, kb, (((1,), (1,)), ((), ())), preferred_element_type=F32) * SCALE
                    sc = jnp.where(valid, sc + b_ref[hh], NEG)
                    sink = s_ref[hh]
                    m = jnp.maximum(jnp.max(sc, axis=-1, keepdims=True), sink)
                    e = jnp.exp(sc - m)
                    z = jnp.sum(e, axis=-1, keepdims=True) + jnp.exp(sink - m)
                    o = jnp.dot((e / z).astype(BF16), vb, preferred_element_type=F32)
                    lse_ref[hh, pl.ds(qs, QB), :] = m + jnp.log(z)
                    y = y + _place_head(o, e4[g])
                y_ref[pl.ds(qs, QB), 256 * kh:256 * (kh + 1)] = y.astype(y_ref.dtype)
            return 0

        lax.fori_loop(0, nb, step, 0, unroll=2)

    return pl.pallas_call(
        body, name=name, grid=(bn,),
        in_specs=[pl.BlockSpec((s, 256), lambda b: (b, C_DQ // 256)), pl.BlockSpec((s, 256), lambda b: (b, C_DQ // 256 + 1)),
                  pl.BlockSpec((s, 128), lambda b: (b, C_DK // 128)), pl.BlockSpec((s, 128), lambda b: (b, C_DV // 128)),
                  pl.BlockSpec((D_Q_HEADS, QB, 2 * QB), lambda b: (0, 0, 0)),
                  pl.BlockSpec((D_Q_HEADS, QB, 1), lambda b: (0, 0, 0))],
        out_specs=[pl.BlockSpec((s, 512), lambda b: (b, 0)), pl.BlockSpec((None, D_Q_HEADS, s, 1), lambda b: (b, 0, 0, 0))],
        out_shape=[jax.ShapeDtypeStruct((bn * s, 512), BF16), jax.ShapeDtypeStruct((bn, D_Q_HEADS, s, 1), F32)],
        scratch_shapes=[pltpu.VMEM((D_Q_HEADS, s, d), BF16), pltpu.VMEM((D_KV_HEADS, s, d), BF16),
                        pltpu.VMEM((D_KV_HEADS, s, d), BF16)],
        compiler_params=_cparams(("parallel",)),
    )(proj, proj, proj, proj, bias, sinks)


def swa2_bwd(proj, bias, sinks, yd, lse, dyd, bn, s, name):
    nb = s // QB
    d = HEAD_DIM

    def body(q0_ref, q1_ref, k_ref, v_ref, b_ref, s_ref, y_ref, lse_ref, dy_ref, dq_ref, dk_ref, dv_ref, db_ref,
             dsk_ref, qh_ref, kh_ref, vh_ref, oh_ref, doh_ref, dkh_ref, dvh_ref):
        @pl.when(pl.program_id(0) == 0)
        def _():
            db_ref[...] = jnp.zeros_like(db_ref)
            dsk_ref[...] = jnp.zeros_like(dsk_ref)

        e4 = _sel(D_GROUP, 256)
        e2 = _sel(D_KV_HEADS, 128)
        for kh, q_ref in enumerate((q0_ref, q1_ref)):
            kh_ref[kh] = _pick_head(k_ref[...], e2[kh])
            vh_ref[kh] = _pick_head(v_ref[...], e2[kh])
            for g in range(D_GROUP):
                hh = D_GROUP * kh + g
                qh_ref[hh] = _pick_head(q_ref[...], e4[g])
                oh_ref[hh] = _pick_head(y_ref[:, 256 * kh:256 * (kh + 1)], e4[g])
                doh_ref[hh] = _pick_head(dy_ref[:, 256 * kh:256 * (kh + 1)], e4[g])
        dkh_ref[...] = jnp.zeros_like(dkh_ref)
        dvh_ref[...] = jnp.zeros_like(dvh_ref)

        def step(n, _):
            valid = _swa_valid(n)
            for kh in range(D_KV_HEADS):
                kb, qs, ps = _band3(kh_ref, kh, n)
                vb, _, _ = _band3(vh_ref, kh, n)
                dkb = jnp.zeros((2 * QB, d), F32)
                dvb = jnp.zeros((2 * QB, d), F32)
                dq4 = jnp.zeros((QB, 256), F32)
                for g in range(D_GROUP):
                    hh = D_GROUP * kh + g
                    qg = qh_ref[hh, pl.ds(qs, QB), :]
                    dog = doh_ref[hh, pl.ds(qs, QB), :]
                    og = oh_ref[hh, pl.ds(qs, QB), :]
                    lse = lse_ref[hh, pl.ds(qs, QB), :]
                    sc = lax.dot_general(qg, kb, (((1,), (1,)), ((), ())), preferred_element_type=F32) * SCALE
                    sc = jnp.where(valid, sc + b_ref[hh], NEG)
                    p = jnp.exp(sc - lse)
                    delta = jnp.sum(dog.astype(F32) * og.astype(F32), axis=-1, keepdims=True)
                    dp = lax.dot_general(dog, vb, (((1,), (1,)), ((), ())), preferred_element_type=F32)
                    ds = p * (dp - delta)
                    dsb = ds.astype(BF16)
                    dq4 = dq4 + _place_head(jnp.dot(dsb, kb, preferred_element_type=F32) * SCALE, e4[g])
                    dkb = dkb + lax.dot_general(dsb, qg, (((0,), (0,)), ((), ())),
                                                preferred_element_type=F32) * SCALE
                    dvb = dvb + lax.dot_general(p.astype(BF16), dog, (((0,), (0,)), ((), ())),
                                                preferred_element_type=F32)
                    db_ref[hh] += ds
                    dsk_ref[hh] += -jnp.exp(s_ref[hh] - lse) * delta
                dq_ref[pl.ds(qs, QB), 256 * kh:256 * (kh + 1)] = dq4.astype(dq_ref.dtype)
                dkh_ref[kh, pl.ds(ps, QB), :] += dkb[:QB]
                dkh_ref[kh, pl.ds(qs, QB), :] += dkb[QB:]
                dvh_ref[kh, pl.ds(ps, QB), :] += dvb[:QB]
                dvh_ref[kh, pl.ds(qs, QB), :] += dvb[QB:]
            return 0

        lax.fori_loop(0, nb, step, 0, unroll=2)
        dk2 = jnp.zeros((s, 128), F32)
        dv2 = jnp.zeros((s, 128), F32)
        for kh in range(D_KV_HEADS):
            dk2 = dk2 + _place_head(dkh_ref[kh], e2[kh])
            dv2 = dv2 + _place_head(dvh_ref[kh], e2[kh])
        dk_ref[...] = dk2.astype(dk_ref.dtype)
        dv_ref[...] = dv2.astype(dv_ref.dtype)

    blk512 = pl.BlockSpec((s, 512), lambda b: (b, 0))
    blk128 = pl.BlockSpec((s, 128), lambda b: (b, 0))
    blk_b = pl.BlockSpec((D_Q_HEADS, QB, 2 * QB), lambda b: (0, 0, 0))
    blk_s = pl.BlockSpec((D_Q_HEADS, QB, 1), lambda b: (0, 0, 0))
    return pl.pallas_call(
        body, name=name, grid=(bn,),
        in_specs=[pl.BlockSpec((s, 256), lambda b: (b, C_DQ // 256)), pl.BlockSpec((s, 256), lambda b: (b, C_DQ // 256 + 1)),
                  pl.BlockSpec((s, 128), lambda b: (b, C_DK // 128)), pl.BlockSpec((s, 128), lambda b: (b, C_DV // 128)),
                  blk_b, blk_s, blk512, pl.BlockSpec((None, D_Q_HEADS, s, 1), lambda b: (b, 0, 0, 0)), blk512],
        out_specs=[blk512, blk128, blk128, blk_b, blk_s],
        out_shape=[jax.ShapeDtypeStruct((bn * s, 512), BF16), jax.ShapeDtypeStruct((bn * s, 128), BF16),
                   jax.ShapeDtypeStruct((bn * s, 128), BF16),
                   jax.ShapeDtypeStruct((D_Q_HEADS, QB, 2 * QB), F32), jax.ShapeDtypeStruct((D_Q_HEADS, QB, 1), F32)],
        scratch_shapes=[pltpu.VMEM((D_Q_HEADS, s, d), BF16), pltpu.VMEM((D_KV_HEADS, s, d), BF16),
                        pltpu.VMEM((D_KV_HEADS, s, d), BF16), pltpu.VMEM((D_Q_HEADS, s, d), BF16),
                        pltpu.VMEM((D_Q_HEADS, s, d), BF16), pltpu.VMEM((D_KV_HEADS, s, d), F32),
                        pltpu.VMEM((D_KV_HEADS, s, d), F32)],
        compiler_params=_cparams(("arbitrary",)),
    )(proj, proj, proj, proj, bias, sinks, yd, lse, dyd)


def _pick_scaled(x, e):
    return (jnp.dot(x, e, preferred_element_type=F32) * SCALE).astype(BF16)


def _swa_valid4(n):
    qi = lax.broadcasted_iota(jnp.int32, (D_GROUP * QB, 2 * QB), 0) & (QB - 1)
    kj = lax.broadcasted_iota(jnp.int32, (D_GROUP * QB, 2 * QB), 1)
    dist = qi + QB - kj
    return (dist >= 0) & (dist < WINDOW) & ((kj >= QB) | (n > 0))


def _fox_scores2(q, k, cq, ck, masked):
    t = FOX_T
    s = lax.dot_general(q, k, (((1,), (1,)), ((), ())), preferred_element_type=F32) + (cq - ck)
    if masked:
        keep = lax.broadcasted_iota(jnp.int32, (t, t), 0) >= lax.broadcasted_iota(jnp.int32, (t, t), 1)
        s = jnp.where(keep, s, NEG)
    return s


def _sel_at(off, width):
    r = lax.broadcasted_iota(jnp.int32, (width, HEAD_DIM), 0)
    c = lax.broadcasted_iota(jnp.int32, (width, HEAD_DIM), 1)
    return (r == c + off).astype(BF16)


def _eye(n):
    return lax.broadcasted_iota(jnp.int32, (n, n), 0) == lax.broadcasted_iota(jnp.int32, (n, n), 1)


def _row_to_col(row, eye):
    return jnp.sum(jnp.where(eye, row, 0.0), axis=1, keepdims=True)


def _col_to_row(col, eye):
    return jnp.sum(jnp.where(eye, col, 0.0), axis=0, keepdims=True)


def swa3_fwd(proj, bias, sinks, bn, s, name):
    nb = s // QB
    d = HEAD_DIM

    def body(q_ref, k_ref, v_ref, b_ref, s_ref, y_ref, lse_ref, qh_ref, kh_ref, vh_ref):
        kh = pl.program_id(0)
        e4 = _sel(D_GROUP, 256)
        ek = _sel_at(HEAD_DIM * kh, 128)
        eye = _eye(QB)
        kh_ref[...] = _pick_head(k_ref[...], ek)
        vh_ref[...] = _pick_head(v_ref[...], ek)
        for g in range(D_GROUP):
            qh_ref[g] = _pick_scaled(q_ref[...], e4[g])
        bias4 = b_ref[...].reshape(D_GROUP * QB, 2 * QB)
        sink4 = s_ref[...].reshape(D_GROUP * QB, 1)

        def step(n, _):
            valid = _swa_valid4(n)
            kb, qs, _ps = _swa_band(kh_ref, n)
            vb, _, _ = _swa_band(vh_ref, n)
            q4 = jnp.concatenate([qh_ref[g, pl.ds(qs, QB), :] for g in range(D_GROUP)], axis=0)
            sc = lax.dot_general(q4, kb, (((1,), (1,)), ((), ())), preferred_element_type=F32)
            sc = jnp.where(valid, sc + bias4, NEG)
            m = jnp.maximum(jnp.max(sc, axis=-1, keepdims=True), sink4)
            e = jnp.exp(sc - m)
            z = jnp.sum(e, axis=-1, keepdims=True) + jnp.exp(sink4 - m)
            o4 = jnp.dot((e / z).astype(BF16), vb, preferred_element_type=F32)
            lse4 = m + jnp.log(z)
            y = jnp.zeros((QB, 256), F32)
            for g in range(D_GROUP):
                lse_ref[g, n] = _col_to_row(lse4[g * QB:(g + 1) * QB], eye)
                y = y + _place_head(o4[g * QB:(g + 1) * QB], e4[g])
            y_ref[pl.ds(qs, QB), :] = y.astype(y_ref.dtype)
            return 0

        lax.fori_loop(0, nb, step, 0, unroll=2)

    return pl.pallas_call(
        body, name=name, grid=(D_KV_HEADS, bn),
        in_specs=[pl.BlockSpec((s, 256), lambda kh, b: (b, C_DQ // 256 + kh)),
                  pl.BlockSpec((s, 128), lambda kh, b: (b, C_DK // 128)),
                  pl.BlockSpec((s, 128), lambda kh, b: (b, C_DV // 128)),
                  pl.BlockSpec((D_GROUP, QB, 2 * QB), lambda kh, b: (kh, 0, 0)),
                  pl.BlockSpec((D_GROUP, QB, 1), lambda kh, b: (kh, 0, 0))],
        out_specs=[pl.BlockSpec((s, 256), lambda kh, b: (b, kh)),
                   pl.BlockSpec((None, D_GROUP, nb, 1, QB), lambda kh, b: (b, kh, 0, 0, 0))],
        out_shape=[jax.ShapeDtypeStruct((bn * s, 512), BF16), jax.ShapeDtypeStruct((bn, D_Q_HEADS, nb, 1, QB), F32)],
        scratch_shapes=[pltpu.VMEM((D_GROUP, s, d), BF16), pltpu.VMEM((s, d), BF16), pltpu.VMEM((s, d), BF16)],
        compiler_params=_cparams(("parallel", "parallel")),
    )(proj, proj, proj, bias, sinks)


def swa3_bwd(proj, bias, sinks, yd, lse, dyd, bn, s, name):
    nb = s // QB
    d = HEAD_DIM

    def body(q_ref, k_ref, v_ref, b_ref, s_ref, y_ref, lse_ref, dy_ref, dq_ref, dk_ref, dv_ref, db_ref, dsk_ref,
             qh_ref, kh_ref, vh_ref, oh_ref, doh_ref, dkh_ref, dvh_ref):
        kh = pl.program_id(0)

        @pl.when(pl.program_id(1) == 0)
        def _():
            db_ref[...] = jnp.zeros_like(db_ref)
            dsk_ref[...] = jnp.zeros_like(dsk_ref)

        e4 = _sel(D_GROUP, 256)
        ek = _sel_at(HEAD_DIM * kh, 128)
        eye = _eye(QB)
        kh_ref[...] = _pick_head(k_ref[...], ek)
        vh_ref[...] = _pick_head(v_ref[...], ek)
        for g in range(D_GROUP):
            qh_ref[g] = _pick_scaled(q_ref[...], e4[g])
            oh_ref[g] = _pick_head(y_ref[...], e4[g])
            doh_ref[g] = _pick_head(dy_ref[...], e4[g])
        dkh_ref[...] = jnp.zeros_like(dkh_ref)
        dvh_ref[...] = jnp.zeros_like(dvh_ref)
        bias4 = b_ref[...].reshape(D_GROUP * QB, 2 * QB)
        sink4 = s_ref[...].reshape(D_GROUP * QB, 1)

        def stack(ref, qs):
            return jnp.concatenate([ref[g, pl.ds(qs, QB), :] for g in range(D_GROUP)], axis=0)

        def step(n, _):
            valid = _swa_valid4(n)
            kb, qs, ps = _swa_band(kh_ref, n)
            vb, _, _ = _swa_band(vh_ref, n)
            q4, do4, o4 = stack(qh_ref, qs), stack(doh_ref, qs), stack(oh_ref, qs)
            lse4 = jnp.concatenate([_row_to_col(lse_ref[g, n], eye) for g in range(D_GROUP)], axis=0)
            sc = lax.dot_general(q4, kb, (((1,), (1,)), ((), ())), preferred_element_type=F32)
            sc = jnp.where(valid, sc + bias4, NEG)
            p = jnp.exp(sc - lse4)
            delta = jnp.sum(do4.astype(F32) * o4.astype(F32), axis=-1, keepdims=True)
            dp = lax.dot_general(do4, vb, (((1,), (1,)), ((), ())), preferred_element_type=F32)
            ds = p * (dp - delta)
            dsb = ds.astype(BF16)
            dq4s = jnp.dot(dsb, kb, preferred_element_type=F32) * SCALE
            dkb = lax.dot_general(dsb, q4, (((0,), (0,)), ((), ())), preferred_element_type=F32)
            dvb = lax.dot_general(p.astype(BF16), do4, (((0,), (0,)), ((), ())), preferred_element_type=F32)
            db_ref[...] += ds.reshape(D_GROUP, QB, 2 * QB)
            dsk_ref[...] += (-jnp.exp(sink4 - lse4) * delta).reshape(D_GROUP, QB, 1)
            dq4 = jnp.zeros((QB, 256), F32)
            for g in range(D_GROUP):
                dq4 = dq4 + _place_head(dq4s[g * QB:(g + 1) * QB], e4[g])
            dq_ref[pl.ds(qs, QB), :] = dq4.astype(dq_ref.dtype)
            dkh_ref[pl.ds(ps, QB), :] += dkb[:QB]
            dkh_ref[pl.ds(qs, QB), :] += dkb[QB:]
            dvh_ref[pl.ds(ps, QB), :] += dvb[:QB]
            dvh_ref[pl.ds(qs, QB), :] += dvb[QB:]
            return 0

        lax.fori_loop(0, nb, step, 0, unroll=2)
        dk_ref[...] = dkh_ref[...].astype(dk_ref.dtype)
        dv_ref[...] = dvh_ref[...].astype(dv_ref.dtype)

    blk256 = pl.BlockSpec((s, 256), lambda kh, b: (b, kh))
    blk_kv = pl.BlockSpec((None, s, d), lambda kh, b: (kh, b, 0))
    blk_b = pl.BlockSpec((D_GROUP, QB, 2 * QB), lambda kh, b: (kh, 0, 0))
    blk_s = pl.BlockSpec((D_GROUP, QB, 1), lambda kh, b: (kh, 0, 0))
    return pl.pallas_call(
        body, name=name, grid=(D_KV_HEADS, bn),
        in_specs=[pl.BlockSpec((s, 256), lambda kh, b: (b, C_DQ // 256 + kh)),
                  pl.BlockSpec((s, 128), lambda kh, b: (b, C_DK // 128)),
                  pl.BlockSpec((s, 128), lambda kh, b: (b, C_DV // 128)),
                  blk_b, blk_s, blk256,
                  pl.BlockSpec((None, D_GROUP, nb, 1, QB), lambda kh, b: (b, kh, 0, 0, 0)), blk256],
        out_specs=[blk256, blk_kv, blk_kv, blk_b, blk_s],
        out_shape=[jax.ShapeDtypeStruct((bn * s, 512), BF16), jax.ShapeDtypeStruct((D_KV_HEADS, bn * s, d), BF16),
                   jax.ShapeDtypeStruct((D_KV_HEADS, bn * s, d), BF16),
                   jax.ShapeDtypeStruct((D_Q_HEADS, QB, 2 * QB), F32), jax.ShapeDtypeStruct((D_Q_HEADS, QB, 1), F32)],
        scratch_shapes=[pltpu.VMEM((D_GROUP, s, d), BF16), pltpu.VMEM((s, d), BF16), pltpu.VMEM((s, d), BF16),
                        pltpu.VMEM((D_GROUP, s, d), BF16), pltpu.VMEM((D_GROUP, s, d), BF16),
                        pltpu.VMEM((s, d), F32), pltpu.VMEM((s, d), F32)],
        compiler_params=_cparams(("parallel", "arbitrary")),
    )(proj, proj, proj, bias, sinks, yd, lse, dyd)


def assemble_dproj(pieces, dk, dv, daf, name):
    t = pieces[0].shape[0]
    tm = 512
    widths = [p.shape[1] for p in pieces]
    npc = len(pieces)
    assert sum(widths) == C_DK and all(w % 128 == 0 for w in widths)

    def body(*refs):
        p_refs = refs[:npc]
        dk_ref, dv_ref, af_ref, o_ref = refs[npc:]
        off = 0
        for r, w in zip(p_refs, widths):
            o_ref[:, off:off + w] = r[...]
            off += w
        e2 = _sel(D_KV_HEADS, 128)
        for r in (dk_ref, dv_ref):
            val = _place_head(r[0], e2[0]) + _place_head(r[1], e2[1])
            o_ref[:, off:off + 128] = val.astype(o_ref.dtype)
            off += 128
        o_ref[:, off:off + 128] = af_ref[...]
        off += 128
        o_ref[:, off:] = jnp.zeros((tm, N_PROJ - off), o_ref.dtype)

    kv_blk = pl.BlockSpec((D_KV_HEADS, tm, HEAD_DIM), lambda i: (0, i, 0))
    return pl.pallas_call(
        body, name=name, grid=(t // tm,),
        in_specs=[pl.BlockSpec((tm, w), lambda i: (i, 0)) for w in widths]
        + [kv_blk, kv_blk, pl.BlockSpec((tm, 128), lambda i: (i, 0))],
        out_specs=pl.BlockSpec((tm, N_PROJ), lambda i: (i, 0)),
        out_shape=jax.ShapeDtypeStruct((t, N_PROJ), BF16),
        compiler_params=_cparams(("parallel",)),
    )(*pieces, dk, dv, daf)


def _bucket_table():
    dist = jnp.maximum(jnp.arange(QB)[:, None] + QB - jnp.arange(2 * QB)[None, :], 0)
    max_exact = REL_BUCKETS // 2
    large = max_exact + (jnp.log(jnp.maximum(dist, 1).astype(F32) / max_exact)
                         / math.log(REL_MAX_DIST / max_exact) * (REL_BUCKETS - max_exact)).astype(jnp.int32)
    large = jnp.minimum(large, REL_BUCKETS - 1)
    return jnp.where(dist < max_exact, dist, large).astype(F32)


def band_bias_fwd(bucket, rel_bias, name):
    def body(bk_ref, rel_ref, o_ref):
        bk = bk_ref[...]
        for hh in range(D_Q_HEADS):
            acc = jnp.zeros(bk.shape, F32)
            for b in range(REL_BUCKETS):
                acc = jnp.where(bk == float(b), rel_ref[b, hh], acc)
            o_ref[hh] = acc

    return pl.pallas_call(
        body, name=name,
        in_specs=[pl.BlockSpec(memory_space=pltpu.VMEM), pl.BlockSpec(memory_space=pltpu.SMEM)],
        out_specs=pl.BlockSpec(memory_space=pltpu.VMEM),
        out_shape=jax.ShapeDtypeStruct((D_Q_HEADS, QB, 2 * QB), F32),
    )(bucket, rel_bias)


def band_bias_bwd(bucket, dbias_layers, name):
    nl = len(dbias_layers)

    def body(bk_ref, *refs):
        o_ref = refs[nl]
        bk = bk_ref[...]
        for hh in range(D_Q_HEADS):
            tot = refs[0][hh]
            for r in refs[1:nl]:
                tot = tot + r[hh]
            for b in range(REL_BUCKETS):
                part = jnp.sum(jnp.where(bk == float(b), tot, 0.0), axis=0, keepdims=True)
                val = jnp.sum(part, axis=1, keepdims=True)
                o_ref[hh, b:b + 1, :] = jnp.broadcast_to(val, (1, 128))

    return pl.pallas_call(
        body, name=name,
        in_specs=[pl.BlockSpec(memory_space=pltpu.VMEM)] * (nl + 1),
        out_specs=pl.BlockSpec(memory_space=pltpu.VMEM),
        out_shape=jax.ShapeDtypeStruct((D_Q_HEADS, REL_BUCKETS, 128), F32),
    )(bucket, *dbias_layers)


def _proj_blk(s, col):
    return pl.BlockSpec((s, 256), functools.partial(lambda b, cb: (b, cb), cb=col // 256))


def convb_fwd(proj, w, bn, s, name):
    kk = w.shape[0]

    def body(bg_ref, cg_ref, xb_ref, w_ref, o_ref):
        x = cg_ref[...].astype(F32) * xb_ref[...].astype(F32)
        row = lax.broadcasted_iota(jnp.int32, x.shape, 0)
        y = jnp.zeros_like(x)
        for k in range(kk):
            y = y + w_ref[k:k + 1, :] * _shift_down(x, kk - 1 - k, row)
        o_ref[...] = (bg_ref[...].astype(F32) * y).astype(o_ref.dtype)

    return pl.pallas_call(
        body, name=name, grid=(bn,),
        in_specs=[_proj_blk(s, C_BG), _proj_blk(s, C_CG), _proj_blk(s, C_XB), pl.BlockSpec(w.shape, lambda b: (0, 0))],
        out_specs=pl.BlockSpec((s, 256), lambda b: (b, 0)),
        out_shape=jax.ShapeDtypeStruct((bn * s, 256), BF16),
        compiler_params=_cparams(("parallel",)),
    )(proj, proj, proj, w)


def convb_bwd(proj, w, dyb, bn, s, name):
    kk = w.shape[0]

    def body(bg_ref, cg_ref, xb_ref, w_ref, d_ref, dbg_ref, dcg_ref, dxb_ref, dw_ref):
        @pl.when(pl.program_id(0) == 0)
        def _():
            dw_ref[...] = jnp.zeros_like(dw_ref)

        cg = cg_ref[...].astype(F32)
        xb = xb_ref[...].astype(F32)
        d = d_ref[...].astype(F32)
        x = cg * xb
        row = lax.broadcasted_iota(jnp.int32, x.shape, 0)
        dy = d * bg_ref[...].astype(F32)
        y = jnp.zeros_like(x)
        dx = jnp.zeros_like(x)
        for k in range(kk):
            xs = _shift_down(x, kk - 1 - k, row)
            y = y + w_ref[k:k + 1, :] * xs
            dx = dx + w_ref[k:k + 1, :] * _shift_up(dy, kk - 1 - k, row)
            dw_ref[k:k + 1, :] += _sum0(dy * xs)
        dbg_ref[...] = (d * y).astype(dbg_ref.dtype)
        dcg_ref[...] = (dx * xb).astype(dcg_ref.dtype)
        dxb_ref[...] = (dx * cg).astype(dxb_ref.dtype)

    blk = pl.BlockSpec((s, 256), lambda b: (b, 0))
    return pl.pallas_call(
        body, name=name, grid=(bn,),
        in_specs=[_proj_blk(s, C_BG), _proj_blk(s, C_CG), _proj_blk(s, C_XB), pl.BlockSpec(w.shape, lambda b: (0, 0)),
                  blk],
        out_specs=[blk, blk, blk, pl.BlockSpec((8, 256), lambda b: (0, 0))],
        out_shape=[jax.ShapeDtypeStruct((bn * s, 256), BF16)] * 3 + [jax.ShapeDtypeStruct((8, 256), F32)],
        compiler_params=_cparams(("arbitrary",)),
    )(proj, proj, proj, w, dyb)


def _convc_core(ca, cb, w_ref, bias, kk, row, y=None):
    sg = jax.nn.sigmoid(cb)
    glu = ca * sg
    if y is None:
        y = jnp.zeros_like(glu)
        for k in range(kk):
            y = y + w_ref[k:k + 1, :] * _shift_down(glu, kk - 1 - k, row)
        y = y + bias
    mu = jnp.mean(y, axis=-1, keepdims=True)
    xc = y - mu
    r = lax.rsqrt(jnp.mean(xc * xc, axis=-1, keepdims=True) + EPS)
    return sg, glu, xc * r, r, y


def convc_fwd(proj, w, bias, gain, lbias, bn, s, name):
    kk = w.shape[0]

    def body(ca_ref, cb_ref, w_ref, b_ref, g_ref, lb_ref, o_ref, y_ref):
        ca = ca_ref[...].astype(F32)
        row = lax.broadcasted_iota(jnp.int32, ca.shape, 0)
        _, _, xh, _, y = _convc_core(ca, cb_ref[...].astype(F32), w_ref, b_ref[...], kk, row)
        ln = xh * g_ref[...] + lb_ref[...]
        o_ref[...] = (ln * jax.nn.sigmoid(ln)).astype(o_ref.dtype)
        y_ref[...] = y

    vec = pl.BlockSpec((1, 256), lambda b: (0, 0))
    blk = pl.BlockSpec((s, 256), lambda b: (b, 0))
    return pl.pallas_call(
        body, name=name, grid=(bn,),
        in_specs=[_proj_blk(s, C_CA), _proj_blk(s, C_CB), pl.BlockSpec(w.shape, lambda b: (0, 0)), vec, vec, vec],
        out_specs=[blk, blk],
        out_shape=[jax.ShapeDtypeStruct((bn * s, 256), BF16), jax.ShapeDtypeStruct((bn * s, 256), F32)],
        compiler_params=_cparams(("parallel",)),
    )(proj, proj, w, bias, gain, lbias)


def convc_bwd(proj, w, bias, gain, lbias, yconv, dyc, bn, s, name):
    kk = w.shape[0]

    def body(ca_ref, cb_ref, w_ref, b_ref, g_ref, lb_ref, y_ref, d_ref, dca_ref, dcb_ref, dw_ref, db_ref, dg_ref,
             dlb_ref):
        @pl.when(pl.program_id(0) == 0)
        def _():
            dw_ref[...] = jnp.zeros_like(dw_ref)
            db_ref[...] = jnp.zeros_like(db_ref)
            dg_ref[...] = jnp.zeros_like(dg_ref)
            dlb_ref[...] = jnp.zeros_like(dlb_ref)

        ca = ca_ref[...].astype(F32)
        row = lax.broadcasted_iota(jnp.int32, ca.shape, 0)
        sg, glu, xh, r, _ = _convc_core(ca, cb_ref[...].astype(F32), w_ref, b_ref[...], kk, row, y=y_ref[...])
        ln = xh * g_ref[...] + lb_ref[...]
        sl = jax.nn.sigmoid(ln)
        dl = d_ref[...].astype(F32) * (sl + ln * sl * (1.0 - sl))
        dg_ref[...] += _sum0(dl * xh)
        dlb_ref[...] += _sum0(dl)
        dxh = dl * g_ref[...]
        dy = r * (dxh - jnp.mean(dxh, axis=-1, keepdims=True) - xh * jnp.mean(dxh * xh, axis=-1, keepdims=True))
        db_ref[...] += _sum0(dy)
        dglu = jnp.zeros_like(glu)
        for k in range(kk):
            dw_ref[k:k + 1, :] += _sum0(dy * _shift_down(glu, kk - 1 - k, row))
            dglu = dglu + w_ref[k:k + 1, :] * _shift_up(dy, kk - 1 - k, row)
        dca_ref[...] = (dglu * sg).astype(dca_ref.dtype)
        dcb_ref[...] = (dglu * ca * sg * (1.0 - sg)).astype(dcb_ref.dtype)

    vec = pl.BlockSpec((1, 256), lambda b: (0, 0))
    blk = pl.BlockSpec((s, 256), lambda b: (b, 0))
    return pl.pallas_call(
        body, name=name, grid=(bn,),
        in_specs=[_proj_blk(s, C_CA), _proj_blk(s, C_CB), pl.BlockSpec(w.shape, lambda b: (0, 0)), vec, vec, vec, blk,
                  blk],
        out_specs=[blk, blk, pl.BlockSpec((32, 256), lambda b: (0, 0)), vec, vec, vec],
        out_shape=[jax.ShapeDtypeStruct((bn * s, 256), BF16)] * 2 + [jax.ShapeDtypeStruct((32, 256), F32)]
        + [jax.ShapeDtypeStruct((1, 256), F32)] * 3,
        compiler_params=_cparams(("arbitrary",)),
    )(proj, proj, w, bias, gain, lbias, yconv, dyc)


def adamw(w, g, m, v, name):
    shape = w.shape
    cols = shape[-1]
    rows = w.size // cols
    tr = _pick(rows, (256, 128, 64, 32, 16, 8))

    def body(w_ref, g_ref, m_ref, v_ref, d_ref, nm_ref, nv_ref):
        gg = g_ref[...]
        mm = ADAM_B1 * m_ref[...] + (1.0 - ADAM_B1) * gg
        vv = ADAM_B2 * v_ref[...] + (1.0 - ADAM_B2) * jnp.square(gg)
        m_hat = mm / (1.0 - ADAM_B1 ** ADAM_STEP)
        v_hat = vv / (1.0 - ADAM_B2 ** ADAM_STEP)
        d_ref[...] = -ADAM_LR * (m_hat / (jnp.sqrt(v_hat) + ADAM_EPS) + ADAM_WD * w_ref[...])
        nm_ref[...] = mm
        nv_ref[...] = vv

    blk = pl.BlockSpec((tr, cols), lambda i: (i, 0))
    outs = pl.pallas_call(
        body, name=name, grid=(rows // tr,), in_specs=[blk] * 4, out_specs=[blk] * 3,
        out_shape=[jax.ShapeDtypeStruct((rows, cols), F32)] * 3,
        compiler_params=_cparams(("parallel",)),
    )(*[a.reshape(rows, cols) for a in (w, g, m, v)])
    return [o.reshape(shape) for o in outs]


def add_halves(own, recv, name):
    n, r, c = own.shape
    tr = _pick(r, (512, 256, 128, 64, 32, 16, 8))
    blk = pl.BlockSpec((None, tr, c), lambda i, j: (i, j, 0))

    def body(a_ref, b_ref, o_ref):
        o_ref[...] = a_ref[...] + b_ref[...]

    return pl.pallas_call(
        body, name=name, grid=(n, r // tr), in_specs=[blk, blk], out_specs=blk,
        out_shape=jax.ShapeDtypeStruct((n, r, c), F32), compiler_params=_cparams(("parallel", "parallel")),
    )(own, recv)


def sum_slots(slots, name):
    n, r, c = slots.shape
    tr = _pick(r, (512, 256, 128, 64, 32, 16, 8))

    def body(a_ref, o_ref):
        acc = a_ref[0]
        for k in range(1, n):
            acc = acc + a_ref[k]
        o_ref[...] = acc

    return pl.pallas_call(
        body, name=name, grid=(r // tr,), in_specs=[pl.BlockSpec((n, tr, c), lambda j: (0, j, 0))],
        out_specs=pl.BlockSpec((tr, c), lambda j: (j, 0)),
        out_shape=jax.ShapeDtypeStruct((r, c), F32), compiler_params=_cparams(("parallel",)),
    )(slots)


ANY = pl.BlockSpec(memory_space=pl.ANY)


def _place():
    x, y, c = lax.axis_index("x"), lax.axis_index("y"), lax.axis_index("c")
    return x, y, c


def gather_shards(pack, name):
    r, cols = pack.shape
    half = r // 2

    def body(src_ref, out_ref, send_sems, recv_sems, local_sem):
        x, y, c = _place()
        sibling = (x, y, 1 - c)
        chips = [(1 - x, y), (x, 1 - y), (1 - x, 1 - y)]

        def rows(px, py, pc):
            return out_ref.at[2 * px + py, pl.ds(pc * half, half), :]

        mine = pltpu.make_async_copy(src_ref, out_ref.at[2 * x + y], local_sem)
        mine.start()

        def copy(k, blk, to, src=None):
            return pltpu.make_async_remote_copy(
                src_ref=rows(*blk) if src is None else src, dst_ref=rows(*blk),
                send_sem=send_sems.at[k], recv_sem=recv_sems.at[k], device_id=to, device_id_type=MESH)

        first = [copy(j, (x, y, c), (*chip, c), src=src_ref.at[pl.ds(c * half, half), :])
                 for j, chip in enumerate(chips)]
        for cp in first:
            cp.start()
        passed = [copy(3 + j, (*chip, c), sibling) for j, chip in enumerate(chips)]
        for j, chip in enumerate(chips):
            copy(j, (*chip, c), (x, y, c)).wait_recv()
            passed[j].start()
        for j, chip in enumerate(chips):
            copy(3 + j, (*chip, 1 - c), (x, y, c)).wait_recv()
        for cp in first + passed:
            cp.wait_send()
        mine.wait()

    return pl.pallas_call(
        body, name=name, in_specs=[ANY], out_specs=ANY,
        out_shape=jax.ShapeDtypeStruct((N_CHIPS, r, cols), pack.dtype),
        scratch_shapes=[pltpu.SemaphoreType.DMA((6,)), pltpu.SemaphoreType.DMA((6,)), pltpu.SemaphoreType.DMA],
    )(pack)


def exchange_sibling_halves(g, name):
    n, r, cols = g.shape
    half = r // 2

    def body(g_ref, own_ref, recv_ref, send_sems, recv_sems, local_sem):
        x, y, c = _place()
        sibling = (x, y, 1 - c)
        mine = pltpu.make_async_copy(g_ref.at[:, pl.ds(c * half, half), :], own_ref, local_sem)
        mine.start()
        cp = pltpu.make_async_remote_copy(
            src_ref=g_ref.at[:, pl.ds((1 - c) * half, half), :], dst_ref=recv_ref,
            send_sem=send_sems.at[0], recv_sem=recv_sems.at[0], device_id=sibling, device_id_type=MESH)
        cp.start()
        cp.wait()
        mine.wait()

    return pl.pallas_call(
        body, name=name, in_specs=[ANY], out_specs=[ANY, ANY],
        out_shape=[jax.ShapeDtypeStruct((n, half, cols), g.dtype)] * 2,
        scratch_shapes=[pltpu.SemaphoreType.DMA((1,)), pltpu.SemaphoreType.DMA((1,)), pltpu.SemaphoreType.DMA],
    )(g)


def scatter_to_chips(part, name):
    n, h, cols = part.shape

    def body(p_ref, slot_ref, send_sems, recv_sems, local_sem):
        x, y, c = _place()
        me = 2 * x + y
        chips = [(1 - x, y), (x, 1 - y), (1 - x, 1 - y)]
        mine = pltpu.make_async_copy(p_ref.at[me], slot_ref.at[me], local_sem)
        mine.start()
        cps = [pltpu.make_async_remote_copy(
            src_ref=p_ref.at[2 * px + py], dst_ref=slot_ref.at[me],
            send_sem=send_sems.at[j], recv_sem=recv_sems.at[j], device_id=(px, py, c), device_id_type=MESH)
            for j, (px, py) in enumerate(chips)]
        for cp in cps:
            cp.start()
        for j, (px, py) in enumerate(chips):
            pltpu.make_async_remote_copy(
                src_ref=p_ref.at[me], dst_ref=slot_ref.at[2 * px + py],
                send_sem=send_sems.at[j], recv_sem=recv_sems.at[j], device_id=(px, py, c),
                device_id_type=MESH).wait_recv()
        for cp in cps:
            cp.wait_send()
        mine.wait()

    return pl.pallas_call(
        body, name=name, in_specs=[ANY], out_specs=ANY,
        out_shape=jax.ShapeDtypeStruct((n, h, cols), part.dtype),
        scratch_shapes=[pltpu.SemaphoreType.DMA((3,)), pltpu.SemaphoreType.DMA((3,)), pltpu.SemaphoreType.DMA],
    )(part)


def join_sibling_halves(mine_half, name):
    h, cols = mine_half.shape

    def body(m_ref, out_ref, send_sems, recv_sems, local_sem):
        x, y, c = _place()
        sibling = (x, y, 1 - c)
        own = pltpu.make_async_copy(m_ref, out_ref.at[pl.ds(c * h, h), :], local_sem)
        own.start()
        cp = pltpu.make_async_remote_copy(
            src_ref=m_ref, dst_ref=out_ref.at[pl.ds(c * h, h), :],
            send_sem=send_sems.at[0], recv_sem=recv_sems.at[0], device_id=sibling, device_id_type=MESH)
        cp.start()
        pltpu.make_async_remote_copy(
            src_ref=m_ref, dst_ref=out_ref.at[pl.ds((1 - c) * h, h), :],
            send_sem=send_sems.at[0], recv_sem=recv_sems.at[0], device_id=sibling, device_id_type=MESH).wait_recv()
        cp.wait_send()
        own.wait()

    return pl.pallas_call(
        body, name=name, in_specs=[ANY], out_specs=ANY,
        out_shape=jax.ShapeDtypeStruct((2 * h, cols), mine_half.dtype),
        scratch_shapes=[pltpu.SemaphoreType.DMA((1,)), pltpu.SemaphoreType.DMA((1,)), pltpu.SemaphoreType.DMA],
    )(mine_half)


def _kind(n):
    return 'win' if n == 'w_in' else ('row' if n in ROW_SHARDED else 'col')


def _chip_ids():
    x, y, c = _place()
    chips = [(1 - x, y), (x, 1 - y), (1 - x, 1 - y)]
    return x, y, c, 2 * x + y, chips, [2 * px + py for px, py in chips]


def gather_weights(shards, name):
    names = list(SHARDED)
    nt = len(names)
    kinds = [_kind(n) for n in names]
    shapes = [shards[n].shape for n in names]
    depth = shapes[0][0]
    half = depth // 2

    def out_shape(kind, shp):
        if kind == 'col':
            return (shp[0], shp[1], N_CHIPS * shp[2])
        if kind == 'row':
            return (shp[0], N_CHIPS * shp[1], shp[2])
        return (N_CHIPS,) + tuple(shp)

    def body(*refs):
        src, out = refs[:nt], refs[nt:2 * nt]
        send_sems, recv_sems = refs[2 * nt:]
        x, y, c, me, chips, chip_idx = _chip_ids()
        sibling = (x, y, 1 - c)

        def win(t, chip, lo, cnt):
            _, a, b = shapes[t]
            if kinds[t] == 'col':
                return out[t].at[pl.ds(lo, cnt), :, pl.ds(chip * b, b)]
            if kinds[t] == 'row':
                return out[t].at[pl.ds(lo, cnt), pl.ds(chip * a, a), :]
            return out[t].at[chip, pl.ds(lo, cnt)]

        def remote(t, k, chip, lo, to, src_ref=None):
            w = win(t, chip, lo, half)
            return pltpu.make_async_remote_copy(
                src_ref=w if src_ref is None else src_ref, dst_ref=w, send_sem=send_sems.at[7 * t + k],
                recv_sem=recv_sems.at[7 * t + k], device_id=to, device_id_type=MESH)

        def own(t):
            return pltpu.make_async_remote_copy(
                src_ref=src[t], dst_ref=win(t, me, 0, depth), send_sem=send_sems.at[7 * t + 6],
                recv_sem=recv_sems.at[7 * t + 6], device_id=sibling, device_id_type=MESH)

        mine = [own(t) for t in range(nt)]
        for cp in mine:
            cp.start()
        first = [[remote(t, j, me, c * half, (*chips[j], c), src_ref=src[t].at[pl.ds(c * half, half)])
                  for j in range(3)] for t in range(nt)]
        for t in range(nt):
            for cp in first[t]:
                cp.start()
        passed = [[remote(t, 3 + j, chip_idx[j], c * half, sibling) for j in range(3)] for t in range(nt)]
        for t in range(nt):
            for j in range(3):
                remote(t, j, chip_idx[j], c * half, (x, y, c)).wait_recv()
                passed[t][j].start()
        for t in range(nt):
            for j in range(3):
                remote(t, 3 + j, chip_idx[j], (1 - c) * half, (x, y, c)).wait_recv()
        for t in range(nt):
            for cp in first[t] + passed[t]:
                cp.wait_send()
            mine[t].wait()

    outs = pl.pallas_call(
        body, name=name, in_specs=[ANY] * nt, out_specs=[ANY] * nt,
        out_shape=[jax.ShapeDtypeStruct(out_shape(k, s), BF16) for k, s in zip(kinds, shapes)],
        scratch_shapes=[pltpu.SemaphoreType.DMA((7 * nt,)), pltpu.SemaphoreType.DMA((7 * nt,))],
    )(*[shards[n] for n in names])
    return dict(zip(names, outs))


def gather_weights2(shards, name):
    names = list(SHARDED)
    nt = len(names)
    kinds = [_kind(n) for n in names]
    shapes = [shards[n].shape for n in names]
    depth = shapes[0][0]
    assert depth == 4
    ns = 13

    def out_shape(kind, shp):
        if kind == 'col':
            return (shp[0], shp[1], N_CHIPS * shp[2])
        if kind == 'row':
            return (shp[0], N_CHIPS * shp[1], shp[2])
        return (N_CHIPS,) + tuple(shp)

    def body(*refs):
        src, out = refs[:nt], refs[nt:2 * nt]
        send_sems, recv_sems = refs[2 * nt:]
        x, y, c, me, chips, chip_idx = _chip_ids()
        sibling = (x, y, 1 - c)
        nbr = [(*chips[0], c), (*chips[1], c)]

        def win(t, chip, layer, cnt=1):
            _, a, b = shapes[t]
            if kinds[t] == 'col':
                return out[t].at[pl.ds(layer, cnt), :, pl.ds(chip * b, b)]
            if kinds[t] == 'row':
                return out[t].at[pl.ds(layer, cnt), pl.ds(chip * a, a), :]
            return out[t].at[chip, pl.ds(layer, cnt)]

        def rc(t, k, chip, layer, to, src_ref=None, cnt=1):
            w = win(t, chip, layer, cnt)
            return pltpu.make_async_remote_copy(
                src_ref=w if src_ref is None else src_ref, dst_ref=w, send_sem=send_sems.at[ns * t + k],
                recv_sem=recv_sems.at[ns * t + k], device_id=to, device_id_type=MESH)

        la = [2 * c, 2 * c + 1]
        lo = [2 - 2 * c, 3 - 2 * c]
        sends = []
        for t in range(nt):
            sends.append(rc(t, 12, me, 0, sibling, src_ref=src[t], cnt=depth))
            for ax in range(2):
                for li in range(2):
                    sends.append(rc(t, 2 * ax + li, me, la[li], nbr[ax], src_ref=src[t].at[pl.ds(la[li], 1)]))
        for cp in sends:
            cp.start()
        landed = [(0, 0, 0, 0), (1, 1, 0, 1), (2, 2, 1, 0), (3, 3, 1, 1), (4, 4, 2, 0), (5, 5, 2, 1)]
        later = []
        for t in range(nt):
            for piece, sem, origin, li in landed:
                rc(t, sem, chip_idx[origin], la[li], (x, y, c)).wait_recv()
                if piece == 0:
                    fw = rc(t, 4, chip_idx[0], la[0], nbr[1])
                    fw.start()
                    later.append(fw)
                if piece == 3:
                    fw = rc(t, 5, chip_idx[1], la[1], nbr[0])
                    fw.start()
                    later.append(fw)
                ps = rc(t, 6 + piece, chip_idx[origin], la[li], sibling)
                ps.start()
                later.append(ps)
        for t in range(nt):
            rc(t, 12, me, 0, (x, y, c), cnt=depth).wait_recv()
            for piece, sem, origin, li in landed:
                rc(t, 6 + piece, chip_idx[origin], lo[li], (x, y, c)).wait_recv()
        for cp in sends + later:
            cp.wait_send()

    outs = pl.pallas_call(
        body, name=name, in_specs=[ANY] * nt, out_specs=[ANY] * nt,
        out_shape=[jax.ShapeDtypeStruct(out_shape(k, s), BF16) for k, s in zip(kinds, shapes)],
        scratch_shapes=[pltpu.SemaphoreType.DMA((ns * nt,)), pltpu.SemaphoreType.DMA((ns * nt,))],
    )(*[shards[n] for n in names])
    return dict(zip(names, outs))


def _half_win(ref, kind, hc, layer):
    if kind == 'col':
        hk = ref.shape[1] // 2
        return ref.at[layer, pl.ds(hc * hk, hk), :]
    if kind == 'row':
        hn = ref.shape[2] // 2
        return ref.at[layer, :, pl.ds(hc * hn, hn)]
    hk = ref.shape[2] // 2
    return ref.at[layer, :, pl.ds(hc * hk, hk), :]


def _half_shape(kind, shp):
    if kind == 'col':
        return (shp[0], shp[1] // 2, shp[2])
    if kind == 'row':
        return (shp[0], shp[1], shp[2] // 2)
    return (shp[0], shp[1], shp[2] // 2, shp[3])


def rs_sibling(grads, name):
    names = list(SHARDED)
    nt = len(names)
    kinds = [_kind(n) for n in names]
    shapes = [grads[n].shape for n in names]
    depth = shapes[0][0]

    def body(*refs):
        src, out = refs[:nt], refs[nt:2 * nt]
        send_sems, recv_sems = refs[2 * nt:]
        x, y, c = _place()
        cps = []
        for t in range(nt):
            for l in range(depth):
                cps.append(pltpu.make_async_remote_copy(
                    src_ref=_half_win(src[t], kinds[t], 1 - c, l), dst_ref=out[t].at[l],
                    send_sem=send_sems.at[depth * t + l], recv_sem=recv_sems.at[depth * t + l],
                    device_id=(x, y, 1 - c), device_id_type=MESH))
        for cp in cps:
            cp.start()
        for cp in cps:
            cp.wait()

    outs = pl.pallas_call(
        body, name=name, in_specs=[ANY] * nt, out_specs=[ANY] * nt,
        out_shape=[jax.ShapeDtypeStruct(_half_shape(k, s), F32) for k, s in zip(kinds, shapes)],
        scratch_shapes=[pltpu.SemaphoreType.DMA((depth * nt,)), pltpu.SemaphoreType.DMA((depth * nt,))],
    )(*[grads[n] for n in names])
    return dict(zip(names, outs))


EW_BLOCK_ELEMS = 512 * 1024


def rs_add(kind, g, recv, c_arr, name):
    shp = recv.shape
    rows, cols = shp[-2], shp[-1]
    tr = _pick(rows, [r for r in (1408, 1024, 704, 512, 256, 128, 64, 32, 16, 8) if r * cols <= EW_BLOCK_ELEMS])
    nb = rows // tr
    lead = (None,) * (len(shp) - 2)
    blk = pl.BlockSpec(lead + (tr, cols), lambda *a: tuple(a[:len(shp) - 2]) + (a[len(shp) - 2], 0))
    if kind == 'row':
        g_blk = pl.BlockSpec(lead + (tr, cols), lambda *a: tuple(a[:len(shp) - 2]) + (a[len(shp) - 2], a[-1][0]))
    else:
        g_blk = pl.BlockSpec(lead + (tr, cols),
                             lambda *a: tuple(a[:len(shp) - 2]) + (a[-1][0] * nb + a[len(shp) - 2], 0))

    def body(c_ref, g_ref, r_ref, o_ref):
        o_ref[...] = (g_ref[...] + r_ref[...]).astype(o_ref.dtype)

    grid_spec = pltpu.PrefetchScalarGridSpec(
        num_scalar_prefetch=1, grid=tuple(shp[:-2]) + (nb,), in_specs=[g_blk, blk], out_specs=blk)
    return pl.pallas_call(
        body, name=name, grid_spec=grid_spec, out_shape=jax.ShapeDtypeStruct(shp, BF16),
        compiler_params=_cparams(None),
    )(c_arr, g, recv)


def _chip_win(ref, kind, chip):
    if kind == 'col':
        ns = ref.shape[2] // N_CHIPS
        return ref.at[:, :, pl.ds(chip * ns, ns)]
    if kind == 'row':
        ks = ref.shape[1] // N_CHIPS
        return ref.at[:, pl.ds(chip * ks, ks), :]
    return ref.at[:, chip]


def _chip_shape(kind, shp):
    if kind == 'col':
        return (shp[0], shp[1], shp[2] // N_CHIPS)
    if kind == 'row':
        return (shp[0], shp[1] // N_CHIPS, shp[2])
    return (shp[0], shp[2], shp[3])


def rs_chips(parts, name):
    names = list(SHARDED)
    nt = len(names)
    kinds = [_kind(n) for n in names]
    shapes = [parts[n].shape for n in names]

    def body(*refs):
        src, out = refs[:nt], refs[nt:2 * nt]
        send_sems, recv_sems, local_sems = refs[2 * nt:]
        x, y, c, me, chips, chip_idx = _chip_ids()
        mine = [pltpu.make_async_copy(_chip_win(src[t], kinds[t], me), out[t].at[me], local_sems.at[t])
                for t in range(nt)]
        for cp in mine:
            cp.start()
        cps = [[pltpu.make_async_remote_copy(
            src_ref=_chip_win(src[t], kinds[t], chip_idx[j]), dst_ref=out[t].at[me],
            send_sem=send_sems.at[3 * t + j], recv_sem=recv_sems.at[3 * t + j],
            device_id=(*chips[j], c), device_id_type=MESH) for j in range(3)] for t in range(nt)]
        for t in range(nt):
            for cp in cps[t]:
                cp.start()
        for t in range(nt):
            for j in range(3):
                pltpu.make_async_remote_copy(
                    src_ref=_chip_win(src[t], kinds[t], me), dst_ref=out[t].at[chip_idx[j]],
                    send_sem=send_sems.at[3 * t + j], recv_sem=recv_sems.at[3 * t + j],
                    device_id=(*chips[j], c), device_id_type=MESH).wait_recv()
        for t in range(nt):
            for cp in cps[t]:
                cp.wait_send()
            mine[t].wait()

    outs = pl.pallas_call(
        body, name=name, in_specs=[ANY] * nt, out_specs=[ANY] * nt,
        out_shape=[jax.ShapeDtypeStruct((N_CHIPS,) + _chip_shape(k, s), parts[n].dtype)
                   for n, k, s in zip(names, kinds, shapes)],
        scratch_shapes=[pltpu.SemaphoreType.DMA((3 * nt,)), pltpu.SemaphoreType.DMA((3 * nt,)),
                        pltpu.SemaphoreType.DMA((nt,))],
    )(*[parts[n] for n in names])
    return dict(zip(names, outs))


def rs_sum(slots, name):
    n, depth, r, cols = slots.shape
    tr = _pick(r, [q for q in (1408, 1024, 704, 512, 256, 128, 64, 32, 16, 8) if q * cols * n <= 2 * EW_BLOCK_ELEMS])

    def body(a_ref, o_ref):
        acc = a_ref[0].astype(F32)
        for k in range(1, n):
            acc = acc + a_ref[k].astype(F32)
        o_ref[...] = acc

    return pl.pallas_call(
        body, name=name, grid=(depth, r // tr),
        in_specs=[pl.BlockSpec((n, None, tr, cols), lambda l, i: (0, l, i, 0))],
        out_specs=pl.BlockSpec((None, tr, cols), lambda l, i: (l, i, 0)),
        out_shape=jax.ShapeDtypeStruct((depth, r, cols), F32), compiler_params=_cparams(("parallel", "parallel")),
    )(slots)


def rs_join(reds, name):
    names = list(SHARDED)
    nt = len(names)
    kinds = [_kind(n) for n in names]
    shapes = [reds[n].shape for n in names]
    depth = shapes[0][0]

    def full_shape(kind, shp):
        if kind == 'row':
            return (shp[0], shp[1], 2 * shp[2])
        return (shp[0], 2 * shp[1], shp[2])

    def win(ref, kind, hc, layer):
        if kind == 'row':
            hn = ref.shape[2] // 2
            return ref.at[layer, :, pl.ds(hc * hn, hn)]
        hk = ref.shape[1] // 2
        return ref.at[layer, pl.ds(hc * hk, hk), :]

    def body(*refs):
        src, out = refs[:nt], refs[nt:2 * nt]
        send_sems, recv_sems, local_sems = refs[2 * nt:]
        x, y, c = _place()
        own, cps = [], []
        for t in range(nt):
            for l in range(depth):
                i = depth * t + l
                own.append(pltpu.make_async_copy(src[t].at[l], win(out[t], kinds[t], c, l), local_sems.at[i]))
                cps.append(pltpu.make_async_remote_copy(
                    src_ref=src[t].at[l], dst_ref=win(out[t], kinds[t], c, l), send_sem=send_sems.at[i],
                    recv_sem=recv_sems.at[i], device_id=(x, y, 1 - c), device_id_type=MESH))
        for cp in own + cps:
            cp.start()
        for t in range(nt):
            for l in range(depth):
                i = depth * t + l
                pltpu.make_async_remote_copy(
                    src_ref=src[t].at[l], dst_ref=win(out[t], kinds[t], 1 - c, l), send_sem=send_sems.at[i],
                    recv_sem=recv_sems.at[i], device_id=(x, y, 1 - c), device_id_type=MESH).wait_recv()
        for cp in cps:
            cp.wait_send()
        for cp in own:
            cp.wait()

    outs = pl.pallas_call(
        body, name=name, in_specs=[ANY] * nt, out_specs=[ANY] * nt,
        out_shape=[jax.ShapeDtypeStruct(full_shape(k, s), F32) for k, s in zip(kinds, shapes)],
        scratch_shapes=[pltpu.SemaphoreType.DMA((depth * nt,)), pltpu.SemaphoreType.DMA((depth * nt,)),
                        pltpu.SemaphoreType.DMA((depth * nt,))],
    )(*[reds[n] for n in names])
    return dict(zip(names, outs))


def rs_chips2(parts, name):
    names = list(SHARDED)
    nt = len(names)
    kinds = [_kind(n) for n in names]
    shapes = [parts[n].shape for n in names]

    def body(*refs):
        src, out = refs[:nt], refs[nt:2 * nt]
        send_sems, recv_sems = refs[2 * nt:]
        x, y, c, me, chips, chip_idx = _chip_ids()
        cps = [[pltpu.make_async_remote_copy(
            src_ref=_chip_win(src[t], kinds[t], chip_idx[j]), dst_ref=out[t].at[j],
            send_sem=send_sems.at[3 * t + j], recv_sem=recv_sems.at[3 * t + j],
            device_id=(*chips[j], c), device_id_type=MESH) for j in range(3)] for t in range(nt)]
        for t in range(nt):
            for cp in cps[t]:
                cp.start()
        for t in range(nt):
            for cp in cps[t]:
                cp.wait()

    outs = pl.pallas_call(
        body, name=name, in_specs=[ANY] * nt, out_specs=[ANY] * nt,
        out_shape=[jax.ShapeDtypeStruct((3,) + _chip_shape(k, s), parts[n].dtype)
                   for n, k, s in zip(names, kinds, shapes)],
        scratch_shapes=[pltpu.SemaphoreType.DMA((3 * nt,)), pltpu.SemaphoreType.DMA((3 * nt,))],
    )(*[parts[n] for n in names])
    return dict(zip(names, outs))


def rs_sum2(kind, part, slots, sc_arr, name):
    _, depth, r, cols = slots.shape
    tr = _pick(r, [q for q in (1408, 1024, 704, 512, 256, 128, 64, 32, 16) if q * cols <= EW_BLOCK_ELEMS // 2])
    nb = r // tr
    if kind == 'col':
        own_blk = pl.BlockSpec((None, tr, cols), lambda l, i, sc: (l, i, sc[0]))
        out_blk = pl.BlockSpec((None, tr, cols), lambda l, i, sc: (l, sc[1] * nb + i, 0))
        out_shape = (depth, 2 * r, cols)
    elif kind == 'row':
        own_blk = pl.BlockSpec((None, tr, cols), lambda l, i, sc: (l, sc[0] * nb + i, 0))
        out_blk = pl.BlockSpec((None, tr, cols), lambda l, i, sc: (l, i, sc[1]))
        out_shape = (depth, r, 2 * cols)
    else:
        own_blk = pl.BlockSpec((None, None, tr, cols), lambda l, i, sc: (l, sc[0], i, 0))
        out_blk = pl.BlockSpec((None, tr, cols), lambda l, i, sc: (l, sc[1] * nb + i, 0))
        out_shape = (depth, 2 * r, cols)

    def body(sc_ref, own_ref, s_ref, o_ref):
        acc = own_ref[...].astype(F32)
        for k in range(3):
            acc = acc + s_ref[k].astype(F32)
        o_ref[...] = acc

    grid_spec = pltpu.PrefetchScalarGridSpec(
        num_scalar_prefetch=1, grid=(depth, nb),
        in_specs=[own_blk, pl.BlockSpec((3, None, tr, cols), lambda l, i, sc: (0, l, i, 0))], out_specs=out_blk)
    return pl.pallas_call(
        body, name=name, grid_spec=grid_spec, out_shape=jax.ShapeDtypeStruct(out_shape, F32),
        compiler_params=_cparams(None),
    )(sc_arr, part, slots)


def rs_join2(halves, name):
    names = list(SHARDED)
    nt = len(names)
    kinds = [_kind(n) for n in names]
    shapes = [halves[n].shape for n in names]
    depth = shapes[0][0]

    def win(ref, kind, hc, layer):
        if kind == 'row':
            hn = ref.shape[2] // 2
            return ref.at[layer, :, pl.ds(hc * hn, hn)]
        hk = ref.shape[1] // 2
        return ref.at[layer, pl.ds(hc * hk, hk), :]

    def body(*refs):
        src, out = refs[:nt], refs[nt:2 * nt]
        send_sems, recv_sems = refs[2 * nt:]
        x, y, c = _place()
        cps = []
        for t in range(nt):
            for l in range(depth):
                i = depth * t + l
                cps.append(pltpu.make_async_remote_copy(
                    src_ref=win(src[t], kinds[t], c, l), dst_ref=win(out[t], kinds[t], c, l),
                    send_sem=send_sems.at[i], recv_sem=recv_sems.at[i], device_id=(x, y, 1 - c),
                    device_id_type=MESH))
        for cp in cps:
            cp.start()
        for t in range(nt):
            for l in range(depth):
                i = depth * t + l
                pltpu.make_async_remote_copy(
                    src_ref=win(src[t], kinds[t], c, l), dst_ref=win(out[t], kinds[t], 1 - c, l),
                    send_sem=send_sems.at[i], recv_sem=recv_sems.at[i], device_id=(x, y, 1 - c),
                    device_id_type=MESH).wait_recv()
        for cp in cps:
            cp.wait_send()

    outs = pl.pallas_call(
        body, name=name, in_specs=[ANY] * nt, out_specs=[ANY] * nt,
        out_shape=[jax.ShapeDtypeStruct(s, F32) for s in shapes],
        input_output_aliases={t: t for t in range(nt)},
        scratch_shapes=[pltpu.SemaphoreType.DMA((depth * nt,)), pltpu.SemaphoreType.DMA((depth * nt,))],
    )(*[halves[n] for n in names])
    return dict(zip(names, outs))


def gather_small(v, name):
    r, cols = v.shape

    def body(v_ref, out_ref, send_sems, recv_sems):
        x, y, c = _place()
        me = 4 * x + 2 * y + c
        out_ref[me] = v_ref[...]
        cps = []
        for rel in range(1, N_DEV):
            px = 1 - x if (rel >> 2) & 1 else x
            py = 1 - y if (rel >> 1) & 1 else y
            pc = 1 - c if rel & 1 else c
            cps.append(pltpu.make_async_remote_copy(
                src_ref=v_ref, dst_ref=out_ref.at[me], send_sem=send_sems.at[rel - 1],
                recv_sem=recv_sems.at[rel - 1], device_id=(px, py, pc), device_id_type=MESH))
        for cp in cps:
            cp.start()
        for cp in cps:
            cp.wait()

    return pl.pallas_call(
        body, name=name, in_specs=[pl.BlockSpec(memory_space=pltpu.VMEM)],
        out_specs=pl.BlockSpec(memory_space=pltpu.VMEM),
        out_shape=jax.ShapeDtypeStruct((N_DEV, r, cols), v.dtype),
        scratch_shapes=[pltpu.SemaphoreType.DMA((N_DEV - 1,)), pltpu.SemaphoreType.DMA((N_DEV - 1,))],
        compiler_params=pltpu.CompilerParams(vmem_limit_bytes=VMEM_LIMIT),
    )(v)


def _pad_rows(flat, row_align):
    n = flat.shape[-1]
    unit = PACK_COLS * row_align
    tot = -(-n // unit) * unit
    pad = [(0, 0)] * (flat.ndim - 1) + [(0, tot - n)]
    return jnp.pad(flat, pad)


def _pack_shards(ws):
    flat = jnp.concatenate([ws[n].astype(BF16).reshape(-1) for n in SHARDED])
    return _pad_rows(flat, PACK_ROW_ALIGN).reshape(-1, PACK_COLS)


def _unpack_full(gathered, shard_shapes):
    flat = gathered.reshape(N_CHIPS, -1)
    out, off = {}, 0
    for n in SHARDED:
        shp = shard_shapes[n]
        size = math.prod(shp)
        seg = flat[:, off:off + size].reshape((N_CHIPS,) + tuple(shp))
        off += size
        if n in ROW_SHARDED:
            out[n] = jnp.transpose(seg, (1, 0, 2, 3)).reshape(shp[0], N_CHIPS * shp[1], shp[2])
        else:
            out[n] = jnp.transpose(seg, (1, 2, 0, 3)).reshape(shp[0], shp[1], N_CHIPS * shp[2])
    return out


def _pack_grads(gfull, shard_shapes):
    segs = []
    for n in SHARDED:
        shp = shard_shapes[n]
        g = gfull[n]
        if n in ROW_SHARDED:
            seg = jnp.transpose(g.reshape(shp[0], N_CHIPS, shp[1], shp[2]), (1, 0, 2, 3))
        else:
            seg = jnp.transpose(g.reshape(shp[0], shp[1], N_CHIPS, shp[2]), (2, 0, 1, 3))
        segs.append(seg.reshape(N_CHIPS, -1))
    flat = jnp.concatenate(segs, axis=1)
    return _pad_rows(flat, PACK_ROW_ALIGN).reshape(N_CHIPS, -1, PACK_COLS)


def _unpack_shard_grads(red, shard_shapes):
    flat = red.reshape(-1)
    out, off = {}, 0
    for n in SHARDED:
        shp = shard_shapes[n]
        size = math.prod(shp)
        out[n] = flat[off:off + size].reshape(shp)
        off += size
    return out


def _pack_small(parts):
    flat = jnp.concatenate([p.astype(F32).reshape(-1) for p in parts])
    return _pad_rows(flat, 8).reshape(-1, PACK_COLS)


def _unpack_small(flat2d, shapes):
    flat = flat2d.reshape(-1)
    out, off = [], 0
    for shp in shapes:
        size = math.prod(shp)
        out.append(flat[off:off + size].reshape(shp))
        off += size
    return out


def _heads(a, bn, s, h):
    return jnp.transpose(a.reshape(bn, s, h, HEAD_DIM), (0, 2, 1, 3))


def _unheads(a):
    bn, h, s, d = a.shape
    return jnp.transpose(a, (0, 2, 1, 3)).reshape(bn * s, h * d)


def _reorder_w_in(w):
    d = w.shape[0]
    return jnp.concatenate([w[:, 2820:6916], w[:, 0:768], w[:, 772:1540], w[:, 1540:2052], w[:, 2052:2820],
                            w[:, 768:772], jnp.zeros((d, N_PROJ - 6916), w.dtype)], axis=1)


def _restore_dw_in(g):
    return jnp.concatenate([g[:, 4096:4864], g[:, 6912:6916], g[:, 4864:5632], g[:, 5632:6144], g[:, 6144:6912],
                            g[:, 0:4096]], axis=1)


def res_rms_next_fwd(h, f, g, coef, g_next, name):
    def fn(x, y, gg, gn):
        hn = x + coef * _rms(y, gg)
        return hn, _rms(hn, gn)
    return _rowwise(fn, [_full(h), _full(f)], [g, g_next], [(h.shape[1], F32), (h.shape[1], BF16)], tm=512,
                    name=name)


def ple_next_fwd(h, pgl, pr, g, g_next, name):
    def fn(x, a, b, gg, gn):
        hn = x + _ple(a, b, gg)
        return hn, _rms(hn, gn)
    return _rowwise(fn, [_full(h), _full(pgl), _full(pr)], [g, g_next], [(D_MODEL, F32), (D_MODEL, BF16)], tm=512,
                    name=name)


def _ffn_fwd(h, n, w_gu, w_down, g_post, g_next, tag):
    gu = _mm(n, w_gu, out_dtype=BF16, name=f"{tag}_mm_gu")
    a = swiglu_fwd(gu, f"{tag}_swiglu")
    f = _mm(a, w_down, out_dtype=F32, name=f"{tag}_mm_down")
    h_out, n_next = res_rms_next_fwd(h, f, g_post, 0.5, g_next, f"{tag}_res")
    return h_out, n_next, (h, n, gu, a, f)


def rms_res_bwd(h, g, dn, dres, f, g2, coef, name):
    def fn(x, d, r, y, gg, gg2):
        _, vjp = jax.vjp(_rms, x, gg)
        dx, dg = vjp(d.astype(F32))
        dh = dx + r
        _, vjp2 = jax.vjp(lambda a, b: coef * _rms(a, b), y, gg2)
        dy, dg2 = vjp2(dh)
        return dh, dy, dg, dg2
    w = h.shape[1]
    return _rowwise(fn, [_full(h), _full(dn), _full(dres), _full(f)], [g, g2], [(w, F32), (w, BF16)], [w, w],
                    tm=512, name=name)


def _ffn_bwd(df, saved, w_gu, w_down, tag, dw, n_gu, n_down):
    _, n, gu, a, _ = saved
    da = _mm(df, w_down, tb=True, out_dtype=BF16, name=f"{tag}_mm_da")
    dw(n_down, a, df, f"{tag}_mm_dwdown")
    dgu = swiglu_bwd(gu, da, f"{tag}_swiglu_bwd")
    dw(n_gu, n, dgu, f"{tag}_mm_dwgu")
    return _mm(dgu, w_gu, tb=True, out_dtype=BF16, name=f"{tag}_mm_dn")


def kernel(x, p, ffn1_norm_pre, ffn1_w_gu, ffn1_w_down, ffn1_norm_post, mix_norm_pre, w_in, b_forget, b_gate, conv_short, conv_dw, conv_dw_bias, conv_ln_gain, conv_ln_bias, attn_sinks, rel_bias, w_br_a, w_br_b, w_br_c, w_br_d, w_o, mix_norm_post, ffn2_norm_pre, ffn2_w_gu, ffn2_w_down, ffn2_norm_post, ple_norm_gate, w_ple_gate, w_ple, ple_norm_post, loss_target, m_ffn1_norm_pre, m_ffn1_w_gu, m_ffn1_w_down, m_ffn1_norm_post, m_mix_norm_pre, m_w_in, m_b_forget, m_b_gate, m_conv_short, m_conv_dw, m_conv_dw_bias, m_conv_ln_gain, m_conv_ln_bias, m_attn_sinks, m_rel_bias, m_w_br_a, m_w_br_b, m_w_br_c, m_w_br_d, m_w_o, m_mix_norm_post, m_ffn2_norm_pre, m_ffn2_w_gu, m_ffn2_w_down, m_ffn2_norm_post, m_ple_norm_gate, m_w_ple_gate, m_w_ple, m_ple_norm_post, v_ffn1_norm_pre, v_ffn1_w_gu, v_ffn1_w_down, v_ffn1_norm_post, v_mix_norm_pre, v_w_in, v_b_forget, v_b_gate, v_conv_short, v_conv_dw, v_conv_dw_bias, v_conv_ln_gain, v_conv_ln_bias, v_attn_sinks, v_rel_bias, v_w_br_a, v_w_br_b, v_w_br_c, v_w_br_d, v_w_o, v_mix_norm_post, v_ffn2_norm_pre, v_ffn2_w_gu, v_ffn2_w_down, v_ffn2_norm_post, v_ple_norm_gate, v_w_ple_gate, v_w_ple, v_ple_norm_post):
    args = dict(locals())
    ws = {n: args[n] for n in WEIGHTS}
    ms = {n: args["m_" + n] for n in WEIGHTS}
    vs = {n: args["v_" + n] for n in WEIGHTS}
    return _step(x, p, loss_target, ws, ms, vs)


def _local(x, p, loss_target, ws, wf, w_short, w_dw):
    bn, s, d = x.shape
    t = bn * s
    depth = w_short.shape[0]

    def vec(a, i):
        return a[i].reshape(1, -1)

    bucket = _bucket_table()
    band_bias = band_bias_fwd(bucket, ws['rel_bias'], "band_bias")

    h = x.reshape(t, d)
    saved = []
    n1 = rms_fwd(h, vec(ws['ffn1_norm_pre'], 0), "l0_ffn1_rms")
    for i in range(depth):
        sv = {}
        h1, u, sv['ffn1'] = _ffn_fwd(h, n1, (wf['ffn1_w_gu'], i), (wf['ffn1_w_down'], i),
                                     vec(ws['ffn1_norm_post'], i), vec(ws['mix_norm_pre'], i), f"l{i}_ffn1")
        w_in_r = _reorder_w_in(wf['w_in'][i])
        proj = _mm(u, w_in_r, out_dtype=BF16, name=f"l{i}_mm_proj")
        bf = jnp.pad(vec(ws['b_forget'], i), ((0, 0), (0, 128 - A_HEADS)))
        cc = fgate_fwd(proj, bf, bn, s, f"l{i}_fgate")
        c4 = jnp.transpose(cc.reshape(bn, s, 128)[:, :, :A_HEADS], (0, 2, 1))
        c_col = c4[..., None]
        c_row = c4.reshape(bn, A_HEADS, s // FOX_T, 1, FOX_T)
        ya, oa, lse_a = fox2_fwd(proj, c_col, c_row, bn, s, f"l{i}_fox")
        w_sh = jnp.pad(w_short[i], ((0, 8 - w_short.shape[1]), (0, 0)))
        w_cv = jnp.pad(w_dw[i], ((0, 32 - w_dw.shape[1]), (0, 0)))
        yb = convb_fwd(proj, w_sh[:3], bn, s, f"l{i}_convb")
        cvec = (vec(ws['conv_dw_bias'], i), vec(ws['conv_ln_gain'], i), vec(ws['conv_ln_bias'], i))
        yc, yconv = convc_fwd(proj, w_cv[:31], *cvec, bn, s, f"l{i}_convc")
        sinks = jnp.broadcast_to(ws['attn_sinks'][i].reshape(D_Q_HEADS, 1, 1), (D_Q_HEADS, QB, 1))
        yd, lse_d = swa3_fwd(proj, band_bias, sinks, bn, s, f"l{i}_swa")
        ys = (ya, yb, yc, yd)
        wbr = ((wf['w_br_a'], i), (wf['w_br_b'], i), (wf['w_br_c'], i), (wf['w_br_d'], i))
        zs = [_mm(yk, wk, out_dtype=BF16, name=f"l{i}_mm_br{k}") for k, (yk, wk) in enumerate(zip(ys, wbr))]
        bgs = [ws['b_gate'][i, k * d:(k + 1) * d].reshape(1, d) for k in range(4)]
        merged = merge_fwd(proj, zs, bgs, f"l{i}_merge")
        mo = _mm(merged, (wf['w_o'], i), out_dtype=F32, name=f"l{i}_mm_o")
        h2, n2 = res_rms_next_fwd(h1, mo, vec(ws['mix_norm_post'], i), 1.0, vec(ws['ffn2_norm_pre'], i),
                                  f"l{i}_mix_res")
        sv['mix'] = dict(h1=h1, u=u, proj=proj, w_in_r=w_in_r, bf=bf, c_col=c_col, c_row=c_row, oa=oa, lse_a=lse_a,
                         w_sh=w_sh, w_cv=w_cv, cvec=cvec, yconv=yconv, sinks=sinks, lse_d=lse_d, ys=ys, wbr=wbr, zs=zs, bgs=bgs,
                         merged=merged, mo=mo)
        h3, ng, sv['ffn2'] = _ffn_fwd(h2, n2, (wf['ffn2_w_gu'], i), (wf['ffn2_w_down'], i),
                                      vec(ws['ffn2_norm_post'], i), vec(ws['ple_norm_gate'], i), f"l{i}_ffn2")
        pgl = _mm(ng, (wf['w_ple_gate'], i), out_dtype=BF16, name=f"l{i}_mm_pgl")
        p_i = p[i].reshape(t, -1)
        pr = _mm(p_i, (wf['w_ple'], i), out_dtype=F32, name=f"l{i}_mm_pr")
        if i + 1 < depth:
            h, n1 = ple_next_fwd(h3, pgl, pr, vec(ws['ple_norm_post'], i), vec(ws['ffn1_norm_pre'], i + 1),
                                 f"l{i}_ple")
        else:
            h = ple_fwd(h3, pgl, pr, vec(ws['ple_norm_post'], i), f"l{i}_ple")
        sv['ple'] = dict(h3=h3, ng=ng, pgl=pgl, p_i=p_i, pr=pr)
        saved.append(sv)

    dh, loss_vec = loss_fwd_bwd(h, loss_target.reshape(t, d), "loss")
    loss_part = jnp.sum(loss_vec)

    gst = {}
    gwin = [None] * depth

    def dw(n, a, b, nm):
        gst[n] = _mm(a, b, ta=True, name=nm, stack=(gst.get(n), depth, i))

    gsmall = {n: [None] * depth for n in REPLICATED + CONV_SHARDED if n != 'rel_bias'}
    dbias_layers = []
    for i in reversed(range(depth)):
        sv = saved[i]
        pv = sv['ple']
        dpgl, dpr, dg = ple_bwd(pv['pgl'], pv['pr'], vec(ws['ple_norm_post'], i), dh, f"l{i}_ple_bwd")
        gsmall['ple_norm_post'][i] = dg
        dw('w_ple', pv['p_i'], dpr, f"l{i}_mm_dwple")
        dw('w_ple_gate', pv['ng'], dpgl, f"l{i}_mm_dwpg")
        dng = _mm(dpgl, (wf['w_ple_gate'], i), tb=True, out_dtype=BF16, name=f"l{i}_mm_dng")
        dh, df2, gsmall['ple_norm_gate'][i], gsmall['ffn2_norm_post'][i] = rms_res_bwd(
            pv['h3'], vec(ws['ple_norm_gate'], i), dng, dh, sv['ffn2'][4], vec(ws['ffn2_norm_post'], i), 0.5,
            f"l{i}_ple_rms_bwd")
        dn2 = _ffn_bwd(df2, sv['ffn2'], (wf['ffn2_w_gu'], i), (wf['ffn2_w_down'], i), f"l{i}_ffn2", dw,
                       'ffn2_w_gu', 'ffn2_w_down')
        mv = sv['mix']
        dh, dmo, gsmall['ffn2_norm_pre'][i], gsmall['mix_norm_post'][i] = rms_res_bwd(
            sv['ffn2'][0], vec(ws['ffn2_norm_pre'], i), dn2, dh, mv['mo'], vec(ws['mix_norm_post'], i), 1.0,
            f"l{i}_ffn2_rms_bwd")
        dw('w_o', mv['merged'], dmo, f"l{i}_mm_dwo")
        dmerged = _mm(dmo, (wf['w_o'], i), tb=True, out_dtype=BF16, name=f"l{i}_mm_dmerged")
        mb = merge_bwd(mv['proj'], mv['zs'], mv['bgs'], dmerged, f"l{i}_merge_bwd")
        dgates, dzs, dbg = mb[0:4], mb[4:8], mb[8:12]
        gsmall['b_gate'][i] = jnp.concatenate(dbg, axis=1)
        dys = []
        for k, nm in enumerate(('w_br_a', 'w_br_b', 'w_br_c', 'w_br_d')):
            dw(nm, mv['ys'][k], dzs[k], f"l{i}_mm_dwbr{k}")
            dys.append(_mm(dzs[k], mv['wbr'][k], tb=True, out_dtype=BF16, name=f"l{i}_mm_dy{k}"))
        dqa, dka, dva, dck, dcq = fox2_bwd(mv['proj'], mv['c_col'], mv['c_row'], mv['oa'], mv['lse_a'], dys[0], bn, s,
                                           f"l{i}_fox_bwd")
        dc = jnp.transpose(dck.reshape(bn, A_HEADS, s) + dcq.reshape(bn, A_HEADS, s), (0, 2, 1))
        dc = jnp.pad(dc, ((0, 0), (0, 0), (0, 128 - A_HEADS))).reshape(t, 128)
        daf, dbf = fgate_bwd(mv['proj'], mv['bf'], dc, bn, s, f"l{i}_fgate_bwd")
        gsmall['b_forget'][i] = dbf[:, :A_HEADS]
        dbg_, dcg_, dxb_, dwsh = convb_bwd(mv['proj'], mv['w_sh'][:3], dys[1], bn, s, f"l{i}_convb_bwd")
        gsmall['conv_short'][i] = dwsh[:3]
        dca, dcb, dwcv, dcbias, dlg, dlb = convc_bwd(mv['proj'], mv['w_cv'][:31], *mv['cvec'], mv['yconv'], dys[2], bn, s,
                                                     f"l{i}_convc_bwd")
        gsmall['conv_dw'][i] = dwcv[:31]
        gsmall['conv_dw_bias'][i] = dcbias
        gsmall['conv_ln_gain'][i] = dlg
        gsmall['conv_ln_bias'][i] = dlb
        dqd, dkd, dvd, dbias, dsink = swa3_bwd(mv['proj'], band_bias, mv['sinks'], mv['ys'][3], mv['lse_d'], dys[3],
                                               bn, s, f"l{i}_swa_bwd")
        dbias_layers.append(dbias)
        gsmall['attn_sinks'][i] = jnp.sum(dsink, axis=(1, 2))
        dproj = assemble_dproj(list(dgates) + [dqa, dka, dva, dbg_, dcg_, dxb_, dca, dcb, dqd], dkd, dvd, daf,
                               f"l{i}_dproj")
        dwin = _restore_dw_in(_mm(mv['u'], dproj, ta=True, name=f"l{i}_mm_dwin"))
        gwin[i] = jnp.transpose(dwin.reshape(d, N_CHIPS, -1), (1, 0, 2))
        du = _mm(dproj, mv['w_in_r'], tb=True, out_dtype=BF16, name=f"l{i}_mm_du")
        dh, df1, gsmall['mix_norm_pre'][i], gsmall['ffn1_norm_post'][i] = rms_res_bwd(
            mv['h1'], vec(ws['mix_norm_pre'], i), du, dh, sv['ffn1'][4], vec(ws['ffn1_norm_post'], i), 0.5,
            f"l{i}_mix_rms_bwd")
        dn1 = _ffn_bwd(df1, sv['ffn1'], (wf['ffn1_w_gu'], i), (wf['ffn1_w_down'], i), f"l{i}_ffn1", dw,
                       'ffn1_w_gu', 'ffn1_w_down')
        dh, gsmall['ffn1_norm_pre'][i] = rms_bwd(sv['ffn1'][0], vec(ws['ffn1_norm_pre'], i), dn1, dh,
                                                  f"l{i}_ffn1_rms_bwd")
    grad_x = dh.reshape(bn, s, d)

    drel = band_bias_bwd(bucket, dbias_layers, "band_bias_bwd")
    gst['w_in'] = jnp.stack(gwin)
    full_shapes = {n: ws[n].shape for n in REPLICATED}
    full_shapes['conv_short'] = w_short.shape
    full_shapes['conv_dw'] = w_dw.shape
    gs = {n: jnp.stack([a.reshape(full_shapes[n][1:]) for a in gsmall[n]]) for n in gsmall}
    gs['rel_bias'] = jnp.transpose(drel[:, :, 0])
    return loss_part, grad_x, gst, gs


def _step(x, p, loss_target, ws, ms, vs):
    chip = 2 * lax.axis_index("x") + lax.axis_index("y")

    wf = gather_weights2({n: ws[n].astype(BF16) for n in SHARDED}, "gather_weights")
    w_in_all = wf['w_in']
    wf['w_in'] = jnp.transpose(w_in_all, (1, 2, 0, 3)).reshape(w_in_all.shape[1], w_in_all.shape[2], -1)
    conv_shapes = [ws[n].shape for n in CONV_SHARDED]
    conv_all = gather_small(_pack_small([ws[n] for n in CONV_SHARDED]), "gather_conv")
    conv_full = []
    for idx, n in enumerate(CONV_SHARDED):
        per_chip = [_unpack_small(conv_all[2 * j], conv_shapes)[idx] for j in range(N_CHIPS)]
        conv_full.append(jnp.concatenate(per_chip, axis=-1))
    w_short, w_dw = conv_full

    loss_part, grad_x, gst, gs = _local(x, p, loss_target, {n: ws[n] for n in REPLICATED}, wf, w_short, w_dw)

    c_arr = lax.axis_index("c").astype(jnp.int32).reshape(1)
    recv = rs_sibling(gst, "rs_sibling")
    chip_sum = {n: rs_add(_kind(n), gst[n], recv[n], c_arr, f"rs_add_{n}") for n in SHARDED}
    slots = rs_chips2(chip_sum, "rs_chips")
    sc_arr = jnp.stack([chip, lax.axis_index("c")]).astype(jnp.int32)
    red_half = {n: rs_sum2(_kind(n), chip_sum[n], slots[n], sc_arr, f"rs_sum_{n}") for n in SHARDED}
    g_shard = rs_join2(red_half, "rs_join")

    small_names = [n for n in REPLICATED + CONV_SHARDED]
    small_parts = [gs[n] for n in small_names]
    small_shapes = [g.shape for g in small_parts]
    small_parts.append(loss_part.reshape(1))
    small_shapes.append((1,))
    small_all = gather_small(_pack_small(small_parts), "gather_small")
    small_red = sum_slots(small_all, "small_sum")
    small_g = _unpack_small(small_red, small_shapes)
    loss = small_g[-1].reshape(())
    g_small = dict(zip(small_names, small_g[:-1]))

    grads = {}
    for n in WEIGHTS:
        if n in SHARDED:
            grads[n] = g_shard[n]
        elif n in CONV_SHARDED:
            wdt = ws[n].shape[-1]
            grads[n] = lax.dynamic_slice_in_dim(g_small[n], chip * wdt, wdt, axis=2)
        else:
            grads[n] = g_small[n]

    deltas, new_m, new_v = {}, {}, {}
    small_upd = [n for n in WEIGHTS if n not in SHARDED]
    for n in SHARDED:
        deltas[n], new_m[n], new_v[n] = adamw(ws[n], grads[n], ms[n], vs[n], f"adamw_{n}")
    shapes_u = [ws[n].shape for n in small_upd]
    packs = [_pack_small([src[n] for n in small_upd]) for src in (ws, grads, ms, vs)]
    upd = adamw(*packs, "adamw_small")
    for res, dst in zip(upd, (deltas, new_m, new_v)):
        for n, a in zip(small_upd, _unpack_small(res, shapes_u)):
            dst[n] = a

    return (loss, grad_x, *[grads[n] for n in WEIGHTS], *[deltas[n] for n in WEIGHTS],
            *[new_m[n] for n in WEIGHTS], *[new_v[n] for n in WEIGHTS])
```
